```python
import jax, jax.numpy as jnp
from jax import lax
import numpy as np

D_MODEL = 1024
BATCH = 8
SEQ = 8192
DEPTH = 1

D_MIX = D_MODEL
D_POOL = D_MIX // 2
D_MLSTM = D_MIX - D_POOL
POOL_WINDOWS = (2, 4, 8, 16)
N_POOL_GROUPS = len(POOL_WINDOWS)
POOL_GROUP_DIM = D_POOL // N_POOL_GROUPS
N_HEADS = 4
HEAD_DIM = D_MLSTM // N_HEADS
CONV_WIDTH = 4
CHUNK = 128
EPS = 1e-6
OFF_U = 0
OFF_ZP = OFF_U + D_POOL
OFF_Q = OFF_ZP + D_POOL
OFF_K = OFF_Q + D_MLSTM
OFF_V = OFF_K + D_MLSTM
OFF_O = OFF_V + D_MLSTM
OFF_ZM = OFF_O + D_MLSTM
OFF_G = OFF_ZM + D_MLSTM
D_IN_PROJ = OFF_G + 2 * N_HEADS

kernel_name = "hymba_pool_mlstm_adaln_block"


def rms_norm(x, g):
    xf = x.astype(jnp.float32)
    y = xf * lax.rsqrt(jnp.mean(xf * xf, axis=-1, keepdims=True) + EPS)
    return (y * g.astype(jnp.float32)).astype(x.dtype)


def causal_pool_mixer(u, w_pool, ls_pool):
    B, S, _ = u.shape
    ug = u.astype(jnp.float32).reshape(B, S, N_POOL_GROUPS, POOL_GROUP_DIM)
    cs = jnp.cumsum(ug, axis=1)
    pos = jnp.arange(1, S + 1, dtype=jnp.float32)
    outs = []
    for g, w in enumerate(POOL_WINDOWS):
        c_g = cs[:, :, g]
        lag = jnp.pad(c_g[:, :S - w], ((0, 0), (w, 0), (0, 0)))
        mean = (c_g - lag) / jnp.minimum(pos, float(w))[None, :, None]
        outs.append(mean - ug[:, :, g])
    pooled = jnp.stack(outs, axis=2)
    y = jnp.einsum('bsgc,gcd->bsgd', pooled, w_pool.astype(jnp.float32))
    y = y.reshape(B, S, D_POOL) * ls_pool.astype(jnp.float32)
    return y.astype(u.dtype)


def causal_depthwise_conv(x, w, b):
    S = x.shape[1]
    xp = jnp.pad(x, ((0, 0), (CONV_WIDTH - 1, 0), (0, 0)))
    y = b
    for j in range(CONV_WIDTH):
        y = y + w[j] * xp[:, j:j + S]
    return y


def _mlstm_chunk_step(carry, inp):
    C, n, m = carry
    q, k, v, ig, lf = inp
    L = q.shape[2]
    b = jnp.cumsum(lf, axis=-1)
    causal = jnp.tril(jnp.ones((L, L), dtype=bool))
    d_log = jnp.where(causal, b[..., :, None] - b[..., None, :] + ig[..., None, :], -jnp.inf)
    inter_log = b + m[..., None]
    m_t = jnp.maximum(inter_log, jnp.max(d_log, axis=-1))
    dmat = jnp.exp(d_log - m_t[..., None])
    inter = jnp.exp(inter_log - m_t)
    s = jnp.einsum('bhtd,bhsd->bhts', q, k) * dmat
    num = jnp.einsum('bhts,bhsd->bhtd', s, v) + inter[..., None] * jnp.einsum('bhvd,bhtd->bhtv', C, q)
    den = jnp.sum(s, axis=-1) + inter * jnp.einsum('bhd,bhtd->bht', n, q)
    h = num / jnp.maximum(jnp.abs(den), jnp.exp(-m_t))[..., None]
    b_last = b[..., -1]
    w_log = b_last[..., None] - b + ig
    m_new = jnp.maximum(b_last + m, jnp.max(w_log, axis=-1))
    w = jnp.exp(w_log - m_new[..., None])
    decay = jnp.exp(b_last + m - m_new)
    C_new = decay[..., None, None] * C + jnp.einsum('bhs,bhsv,bhsd->bhvd', w, v, k)
    n_new = decay[..., None] * n + jnp.einsum('bhs,bhsd->bhd', w, k)
    return (C_new, n_new, m_new), h


def _to_chunks(a):
    B, S = a.shape[:2]
    a = a.reshape((B, S // CHUNK, CHUNK) + a.shape[2:])
    return jnp.swapaxes(jnp.moveaxis(a, 1, 0), 2, 3)


def mlstm_chunkwise(q, k, v, ig, lf):
    B, S, H, dh = q.shape
    f32 = jnp.float32
    k = k / jnp.sqrt(jnp.asarray(dh, f32)).astype(k.dtype)
    xs = tuple(_to_chunks(a.astype(f32)) for a in (q, k, v, ig, lf))
    carry0 = (jnp.zeros((B, H, dh, dh), f32), jnp.zeros((B, H, dh), f32), jnp.zeros((B, H), f32))
    _, hs = lax.scan(_mlstm_chunk_step, carry0, xs)
    hs = jnp.moveaxis(jnp.swapaxes(hs, 2, 3), 0, 1)
    return hs.reshape(B, S, H, dh)


def _fwd_setup_inputs(seed: int = 0) -> dict:
    key = jax.random.key(seed)
    ks = jax.random.split(key, 16)
    nrm = jax.random.normal
    f32 = jnp.float32
    x = nrm(ks[0], (BATCH, SEQ, D_MODEL), f32)
    c = nrm(ks[1], (BATCH, D_MODEL), f32)
    norm_g = 1.0 + 0.02 * nrm(ks[2], (DEPTH, D_MODEL), f32)
    w_ada = 0.5 * D_MODEL ** -0.5 * nrm(ks[3], (DEPTH, D_MODEL, 3 * D_MODEL), f32)
    b_ada = 0.01 * nrm(ks[4], (DEPTH, 3 * D_MODEL), f32)
    w_in = D_MODEL ** -0.5 * nrm(ks[5], (DEPTH, D_MODEL, D_IN_PROJ), f32)
    b_igate = 0.1 * nrm(ks[6], (DEPTH, N_HEADS), f32)
    b_fgate = jnp.linspace(3.0, 6.0, N_HEADS, dtype=f32)[None, :] + 0.1 * nrm(ks[7], (DEPTH, N_HEADS), f32)
    b_gates = jnp.concatenate([b_igate, b_fgate], axis=-1)
    conv_w = 0.5 * nrm(ks[8], (DEPTH, CONV_WIDTH, 2 * D_MLSTM), f32)
    conv_b = 0.02 * nrm(ks[9], (DEPTH, 2 * D_MLSTM), f32)
    w_pool = POOL_GROUP_DIM ** -0.5 * nrm(ks[10], (DEPTH, N_POOL_GROUPS, POOL_GROUP_DIM, POOL_GROUP_DIM), f32)
    ls_pool = 1.0 + 0.1 * nrm(ks[11], (DEPTH, D_POOL), f32)
    mh_norm_g = 1.0 + 0.02 * nrm(ks[12], (DEPTH, D_MLSTM), f32)
    w_out = D_MIX ** -0.5 * nrm(ks[13], (DEPTH, D_MIX, D_MODEL), f32)
    final_g = 1.0 + 0.02 * nrm(ks[14], (D_MODEL,), f32)
    return {"x": x, "c": c, "norm_g": norm_g, "w_ada": w_ada, "b_ada": b_ada,
            "w_in": w_in, "b_gates": b_gates, "conv_w": conv_w, "conv_b": conv_b,
            "w_pool": w_pool, "ls_pool": ls_pool, "mh_norm_g": mh_norm_g,
            "w_out": w_out, "final_g": final_g}


def _fwd_reference(x, c, norm_g, w_ada, b_ada, w_in, b_gates, conv_w, conv_b,
              w_pool, ls_pool, mh_norm_g, w_out, final_g):
    B, S, _ = x.shape
    for l in range(DEPTH):
        mod = jax.nn.silu(c) @ w_ada[l] + b_ada[l]
        shift, scale, gate = jnp.split(mod, 3, axis=-1)
        h = rms_norm(x, norm_g[l]) * (1.0 + scale[:, None, :]) + shift[:, None, :]
        proj = h @ w_in[l]

        u = proj[..., OFF_U:OFF_ZP]
        z_pool = proj[..., OFF_ZP:OFF_Q]
        pool_out = causal_pool_mixer(u, w_pool[l], ls_pool[l]) * jax.nn.silu(z_pool)

        qk = jax.nn.silu(causal_depthwise_conv(proj[..., OFF_Q:OFF_V], conv_w[l], conv_b[l]))
        q = qk[..., :D_MLSTM].reshape(B, S, N_HEADS, HEAD_DIM)
        k = qk[..., D_MLSTM:].reshape(B, S, N_HEADS, HEAD_DIM)
        v = proj[..., OFF_V:OFF_O].reshape(B, S, N_HEADS, HEAD_DIM)
        o_gate = jax.nn.sigmoid(proj[..., OFF_O:OFF_ZM])
        z_mlstm = proj[..., OFF_ZM:OFF_G]
        gates = (proj[..., OFF_G:] + b_gates[l]).astype(jnp.float32)
        ig = gates[..., :N_HEADS]
        lf = jax.nn.log_sigmoid(gates[..., N_HEADS:])
        hm = mlstm_chunkwise(q, k, v, ig, lf)
        hm = hm * lax.rsqrt(jnp.mean(hm * hm, axis=-1, keepdims=True) + EPS)
        hm = hm.reshape(B, S, D_MLSTM) * mh_norm_g[l].astype(jnp.float32)
        mlstm_out = hm.astype(x.dtype) * o_gate * jax.nn.silu(z_mlstm)

        mix = jnp.concatenate([pool_out, mlstm_out], axis=-1)
        x = x + gate[:, None, :] * (mix @ w_out[l])
    return rms_norm(x, final_g)


import jax as _jax
import jax.numpy as _jnp

TWIN_FORMAT = 'train_step'
FWD_PARAMS = ['x', 'c', 'norm_g', 'w_ada', 'b_ada', 'w_in', 'b_gates', 'conv_w', 'conv_b', 'w_pool', 'ls_pool', 'mh_norm_g', 'w_out', 'final_g']
TWIN_WEIGHTS = ['norm_g', 'w_ada', 'b_ada', 'w_in', 'b_gates', 'conv_w', 'conv_b', 'w_pool', 'ls_pool', 'mh_norm_g', 'w_out', 'final_g']
TWIN_DIFF_INPUT = 'x'
TWIN_INPUTS = ['x', 'c', 'norm_g', 'w_ada', 'b_ada', 'w_in', 'b_gates', 'conv_w', 'conv_b', 'w_pool', 'ls_pool', 'mh_norm_g', 'w_out', 'final_g', 'loss_target', 'm_norm_g', 'm_w_ada', 'm_b_ada', 'm_w_in', 'm_b_gates', 'm_conv_w', 'm_conv_b', 'm_w_pool', 'm_ls_pool', 'm_mh_norm_g', 'm_w_out', 'm_final_g', 'v_norm_g', 'v_w_ada', 'v_b_ada', 'v_w_in', 'v_b_gates', 'v_conv_w', 'v_conv_b', 'v_w_pool', 'v_ls_pool', 'v_mh_norm_g', 'v_w_out', 'v_final_g']
TWIN_OUTPUTS = ['loss', 'grad_x', 'grad_norm_g', 'grad_w_ada', 'grad_b_ada', 'grad_w_in', 'grad_b_gates', 'grad_conv_w', 'grad_conv_b', 'grad_w_pool', 'grad_ls_pool', 'grad_mh_norm_g', 'grad_w_out', 'grad_final_g', 'delta_norm_g', 'delta_w_ada', 'delta_b_ada', 'delta_w_in', 'delta_b_gates', 'delta_conv_w', 'delta_conv_b', 'delta_w_pool', 'delta_ls_pool', 'delta_mh_norm_g', 'delta_w_out', 'delta_final_g', 'new_m_norm_g', 'new_m_w_ada', 'new_m_b_ada', 'new_m_w_in', 'new_m_b_gates', 'new_m_conv_w', 'new_m_conv_b', 'new_m_w_pool', 'new_m_ls_pool', 'new_m_mh_norm_g', 'new_m_w_out', 'new_m_final_g', 'new_v_norm_g', 'new_v_w_ada', 'new_v_b_ada', 'new_v_w_in', 'new_v_b_gates', 'new_v_conv_w', 'new_v_conv_b', 'new_v_w_pool', 'new_v_ls_pool', 'new_v_mh_norm_g', 'new_v_w_out', 'new_v_final_g']
TWIN_LEAF_KINDS = {'loss': 'loss', 'grad_x': 'grad_x', 'grad_norm_g': 'grad_w', 'grad_w_ada': 'grad_w', 'grad_b_ada': 'grad_w', 'grad_w_in': 'grad_w', 'grad_b_gates': 'grad_w', 'grad_conv_w': 'grad_w', 'grad_conv_b': 'grad_w', 'grad_w_pool': 'grad_w', 'grad_ls_pool': 'grad_w', 'grad_mh_norm_g': 'grad_w', 'grad_w_out': 'grad_w', 'grad_final_g': 'grad_w', 'delta_norm_g': 'delta_w', 'delta_w_ada': 'delta_w', 'delta_b_ada': 'delta_w', 'delta_w_in': 'delta_w', 'delta_b_gates': 'delta_w', 'delta_conv_w': 'delta_w', 'delta_conv_b': 'delta_w', 'delta_w_pool': 'delta_w', 'delta_ls_pool': 'delta_w', 'delta_mh_norm_g': 'delta_w', 'delta_w_out': 'delta_w', 'delta_final_g': 'delta_w', 'new_m_norm_g': 'new_m', 'new_m_w_ada': 'new_m', 'new_m_b_ada': 'new_m', 'new_m_w_in': 'new_m', 'new_m_b_gates': 'new_m', 'new_m_conv_w': 'new_m', 'new_m_conv_b': 'new_m', 'new_m_w_pool': 'new_m', 'new_m_ls_pool': 'new_m', 'new_m_mh_norm_g': 'new_m', 'new_m_w_out': 'new_m', 'new_m_final_g': 'new_m', 'new_v_norm_g': 'new_v', 'new_v_w_ada': 'new_v', 'new_v_b_ada': 'new_v', 'new_v_w_in': 'new_v', 'new_v_b_gates': 'new_v', 'new_v_conv_w': 'new_v', 'new_v_conv_b': 'new_v', 'new_v_w_pool': 'new_v', 'new_v_ls_pool': 'new_v', 'new_v_mh_norm_g': 'new_v', 'new_v_w_out': 'new_v', 'new_v_final_g': 'new_v'}


def _forward(args):
    return _fwd_reference(*[args[k] for k in FWD_PARAMS])


def _output_shape():
    def fwd():
        inp = _fwd_setup_inputs(0)
        return _fwd_reference(*[inp[k] for k in FWD_PARAMS])
    out = _jax.eval_shape(fwd)
    return out.shape, out.dtype

N_MICROBATCH = 1
ADAM_LR = 0.001
ADAM_B1 = 0.9
ADAM_B2 = 0.999
ADAM_EPS = 1e-08
ADAM_WD = 0.01
ADAM_STEP = 10
PER_EXAMPLE_BATCH_AXIS = {'x': 0, 'c': 0, 'loss_target': 0}
SHARED_INPUTS = []
_WEIGHT_DTYPES = {'norm_g': _jnp.float32, 'w_ada': _jnp.float32, 'b_ada': _jnp.float32, 'w_in': _jnp.float32, 'b_gates': _jnp.float32, 'conv_w': _jnp.float32, 'conv_b': _jnp.float32, 'w_pool': _jnp.float32, 'ls_pool': _jnp.float32, 'mh_norm_g': _jnp.float32, 'w_out': _jnp.float32, 'final_g': _jnp.float32}
MOMENT_SCALE = {'norm_g': 5.363676e-02, 'w_ada': 5.366953e-02, 'b_ada': 8.904288e-02, 'w_in': 3.058000e-02, 'b_gates': 9.906998e-02, 'conv_w': 1.282319e-02, 'conv_b': 1.188808e-02, 'w_pool': 4.576488e-02, 'ls_pool': 4.884702e-02, 'mh_norm_g': 2.853428e-02, 'w_out': 3.755404e-02, 'final_g': 6.399793e+01}


def _to_microbatches(a, axis):
    t = _jnp.moveaxis(a, axis, 0)
    t = t.reshape((N_MICROBATCH, t.shape[0] // N_MICROBATCH) + t.shape[1:])
    return _jnp.moveaxis(t, 1, axis + 1)


def setup_inputs(seed: int = 0) -> dict:
    inp = _fwd_setup_inputs(seed)
    key = _jax.random.fold_in(_jax.random.key(seed), 7919)
    shape, _ = _output_shape()
    out = dict(inp)
    out["loss_target"] = _jax.random.normal(_jax.random.fold_in(key, 0), shape, _jnp.float32)
    for i, name in enumerate(TWIN_WEIGHTS):
        w = inp[name].astype(_jnp.float32)
        if MOMENT_SCALE is None:
            s = _jnp.sqrt(_jnp.mean(_jnp.square(w)) + 1e-30)
        else:
            s = MOMENT_SCALE[name]
        km, kv = _jax.random.split(_jax.random.fold_in(key, i + 1))
        out[name] = w
        out["m_" + name] = s * _jax.random.normal(km, w.shape, _jnp.float32)
        out["v_" + name] = (s * s) * _jax.random.uniform(kv, w.shape, _jnp.float32, 0.5, 1.5)
    if N_MICROBATCH > 1:
        for name, axis in PER_EXAMPLE_BATCH_AXIS.items():
            out[name] = _to_microbatches(out[name], axis)
    return {'x': out['x'], 'c': out['c'], 'norm_g': out['norm_g'], 'w_ada': out['w_ada'], 'b_ada': out['b_ada'], 'w_in': out['w_in'], 'b_gates': out['b_gates'], 'conv_w': out['conv_w'], 'conv_b': out['conv_b'], 'w_pool': out['w_pool'], 'ls_pool': out['ls_pool'], 'mh_norm_g': out['mh_norm_g'], 'w_out': out['w_out'], 'final_g': out['final_g'], 'loss_target': out['loss_target'], 'm_norm_g': out['m_norm_g'], 'm_w_ada': out['m_w_ada'], 'm_b_ada': out['m_b_ada'], 'm_w_in': out['m_w_in'], 'm_b_gates': out['m_b_gates'], 'm_conv_w': out['m_conv_w'], 'm_conv_b': out['m_conv_b'], 'm_w_pool': out['m_w_pool'], 'm_ls_pool': out['m_ls_pool'], 'm_mh_norm_g': out['m_mh_norm_g'], 'm_w_out': out['m_w_out'], 'm_final_g': out['m_final_g'], 'v_norm_g': out['v_norm_g'], 'v_w_ada': out['v_w_ada'], 'v_b_ada': out['v_b_ada'], 'v_w_in': out['v_w_in'], 'v_b_gates': out['v_b_gates'], 'v_conv_w': out['v_conv_w'], 'v_conv_b': out['v_conv_b'], 'v_w_pool': out['v_w_pool'], 'v_ls_pool': out['v_ls_pool'], 'v_mh_norm_g': out['v_mh_norm_g'], 'v_w_out': out['v_w_out'], 'v_final_g': out['v_final_g']}


def _loss(weights, diff, rest, loss_target):
    with _jax.named_scope("forward"):
        args = {**rest, TWIN_DIFF_INPUT: diff, **{k: w.astype(_WEIGHT_DTYPES[k]) for k, w in weights.items()}}
        y = _forward(args)
    with _jax.named_scope("loss_head"):
        err = _jnp.square(y.astype(_jnp.float32) - loss_target)
        return 0.5 * _jnp.sum(_jnp.mean(err, axis=-1)) if err.ndim else 0.5 * err


def _adamw(w, g, m, v):
    m = ADAM_B1 * m + (1.0 - ADAM_B1) * g
    v = ADAM_B2 * v + (1.0 - ADAM_B2) * _jnp.square(g)
    m_hat = m / (1.0 - ADAM_B1 ** ADAM_STEP)
    v_hat = v / (1.0 - ADAM_B2 ** ADAM_STEP)
    delta = -ADAM_LR * (m_hat / (_jnp.sqrt(v_hat) + ADAM_EPS) + ADAM_WD * w)
    return delta, m, v


def reference(x, c, norm_g, w_ada, b_ada, w_in, b_gates, conv_w, conv_b, w_pool, ls_pool, mh_norm_g, w_out, final_g, loss_target, m_norm_g, m_w_ada, m_b_ada, m_w_in, m_b_gates, m_conv_w, m_conv_b, m_w_pool, m_ls_pool, m_mh_norm_g, m_w_out, m_final_g, v_norm_g, v_w_ada, v_b_ada, v_w_in, v_b_gates, v_conv_w, v_conv_b, v_w_pool, v_ls_pool, v_mh_norm_g, v_w_out, v_final_g):
    given = dict(x=x, c=c, norm_g=norm_g, w_ada=w_ada, b_ada=b_ada, w_in=w_in, b_gates=b_gates, conv_w=conv_w, conv_b=conv_b, w_pool=w_pool, ls_pool=ls_pool, mh_norm_g=mh_norm_g, w_out=w_out, final_g=final_g, loss_target=loss_target, m_norm_g=m_norm_g, m_w_ada=m_w_ada, m_b_ada=m_b_ada, m_w_in=m_w_in, m_b_gates=m_b_gates, m_conv_w=m_conv_w, m_conv_b=m_conv_b, m_w_pool=m_w_pool, m_ls_pool=m_ls_pool, m_mh_norm_g=m_mh_norm_g, m_w_out=m_w_out, m_final_g=m_final_g, v_norm_g=v_norm_g, v_w_ada=v_w_ada, v_b_ada=v_b_ada, v_w_in=v_w_in, v_b_gates=v_b_gates, v_conv_w=v_conv_w, v_conv_b=v_conv_b, v_w_pool=v_w_pool, v_ls_pool=v_ls_pool, v_mh_norm_g=v_mh_norm_g, v_w_out=v_w_out, v_final_g=v_final_g)
    weights = {n: given[n] for n in TWIN_WEIGHTS}
    shared = {n: given[n] for n in SHARED_INPUTS}
    per_example = {n: given[n] for n in ['x', 'c']}
    grad_fn = _jax.value_and_grad(_loss, argnums=(0, 1))

    def one_microbatch(ex, loss_target):
        ex = dict(ex)
        diff = ex.pop(TWIN_DIFF_INPUT)
        return grad_fn(weights, diff, {**shared, **ex}, loss_target)

    if N_MICROBATCH == 1:
        loss, (grad_w, grad_x) = one_microbatch(per_example, given["loss_target"])
    else:
        def body(carry, xs):
            loss_sum, grad_sum = carry
            l_k, (gw_k, gx_k) = one_microbatch(xs[0], xs[1])
            with _jax.named_scope("update"):
                return (loss_sum + l_k, _jax.tree.map(_jnp.add, grad_sum, gw_k)), gx_k

        init = (_jnp.zeros((), _jnp.float32), _jax.tree.map(_jnp.zeros_like, weights))
        (loss, grad_w), grad_x = _jax.lax.scan(body, init, (per_example, given["loss_target"]))
    with _jax.named_scope("update"):
        delta_w, new_m, new_v = {}, {}, {}
        for n in TWIN_WEIGHTS:
            delta_w[n], new_m[n], new_v[n] = _adamw(weights[n], grad_w[n], given["m_" + n], given["v_" + n])
    return (loss, grad_x, *[grad_w[n] for n in TWIN_WEIGHTS], *[delta_w[n] for n in TWIN_WEIGHTS],
            *[new_m[n] for n in TWIN_WEIGHTS], *[new_v[n] for n in TWIN_WEIGHTS])
```

```python
import jax
import jax.numpy as jnp
from jax import lax
from jax.experimental import pallas as pl
from jax.experimental.pallas import tpu as pltpu

F32 = jnp.float32
BF16 = jnp.bfloat16

D_MODEL = 1024
D_POOL = 512
D_MLSTM = 512
N_HEADS = 4
HEAD_DIM = 128
CHUNK = 128
POOL_WINDOWS = (2, 4, 8, 16)
POOL_GROUP_DIM = 128
CONV_WIDTH = 4
EPS = 1e-6
N_MAIN = 3584
N_IN = 3592
N_PAD = 3840
N_SHARD = N_IN // 8
ADA_SHARD = 3 * D_MODEL // 8
N_DEV = 8
CONV_HALO = 8
POOL_HALO = 16
NEG_BIG = -1e30
VMEM_LIMIT_BYTES = 56 * 1024 * 1024

ADAM_LR = 0.001
ADAM_B1 = 0.9
ADAM_B2 = 0.999
ADAM_EPS = 1e-08
ADAM_WD = 0.01
ADAM_STEP = 10

PACK_ROWS = (("loss", 8), ("final_g", 8), ("norm_g", 8), ("conv_b", 8), ("ls_pool", 8),
             ("mh_norm_g", 8), ("b_gates", 8), ("b_ada", 24), ("w_pool", 512))
PACK_TOTAL = sum(r for _, r in PACK_ROWS)


def _pack_offsets():
    off, out = 0, {}
    for name, rows in PACK_ROWS:
        out[name] = (off, rows)
        off += rows
    return out


PACK_OFF = _pack_offsets()


def _dot(a, b):
    return jnp.dot(a, b, preferred_element_type=F32)


def _dot_nt(a, b):
    return lax.dot_general(a, b, (((1,), (1,)), ((), ())), preferred_element_type=F32)


def _dot_tn(a, b):
    return lax.dot_general(a, b, (((0,), (0,)), ((), ())), preferred_element_type=F32)


def _dot_f32(a, b):
    return jnp.dot(a, b, precision=lax.Precision.HIGHEST, preferred_element_type=F32)


def _sigmoid(x):
    return jax.nn.sigmoid(x)


def _log_sigmoid(x):
    return jnp.minimum(x, 0.0) - jnp.log1p(jnp.exp(-jnp.abs(x)))


def _params(sem):
    return pltpu.CompilerParams(dimension_semantics=sem, vmem_limit_bytes=VMEM_LIMIT_BYTES)


def _full(shape):
    n = len(shape)
    return pl.BlockSpec(shape, lambda *_: (0,) * n)


def _mesh_pos():
    return lax.axis_index("x"), lax.axis_index("y"), lax.axis_index("c")


def _peer(k):
    x, y, c = _mesh_pos()
    px = 1 - x if (k >> 2) & 1 else x
    py = 1 - y if (k >> 1) & 1 else y
    pc = 1 - c if k & 1 else c
    return (px, py, pc), 4 * px + 2 * py + pc


def _gather_weights(w_in_b, w_out_b, conv_w, c_row):
    shards = (w_in_b, w_out_b, conv_w, c_row)
    n = len(shards)

    def body(*refs):
        src, dst = refs[:n], refs[n:2 * n]
        send_sems, recv_sems, local_sems = refs[2 * n:]
        x, y, c = _mesh_pos()
        me = 4 * x + 2 * y + c
        local = [pltpu.make_async_copy(src[a], dst[a].at[me], local_sems.at[a]) for a in range(n)]
        for cp in local:
            cp.start()
        copies = []
        for k in range(1, N_DEV):
            peer, _ = _peer(k)
            for a in range(n):
                copies.append(pltpu.make_async_remote_copy(
                    src_ref=src[a], dst_ref=dst[a].at[me], send_sem=send_sems.at[a, k - 1],
                    recv_sem=recv_sems.at[a, k - 1], device_id=peer, device_id_type=pl.DeviceIdType.MESH))
        for cp in copies:
            cp.start()
        for cp in copies:
            cp.wait()
        for cp in local:
            cp.wait()

    hbm = pl.BlockSpec(memory_space=pltpu.HBM)
    return pl.pallas_call(
        body, name="gather_weights",
        out_shape=tuple(jax.ShapeDtypeStruct((N_DEV,) + s.shape, s.dtype) for s in shards),
        in_specs=[hbm] * n, out_specs=tuple([hbm] * n),
        scratch_shapes=[pltpu.SemaphoreType.DMA((n, N_DEV - 1)), pltpu.SemaphoreType.DMA((n, N_DEV - 1)),
                        pltpu.SemaphoreType.DMA((n,))],
    )(*shards)


def _ada_mod(c_all16, w_ada_blk, b_ada_blk):
    def body(c_ref, w_ref, b_ref, out_ref, sc_ref, send_sems, recv_sems):
        x, y, c = _mesh_pos()
        me = 4 * x + 2 * y + c
        cv = c_ref[...]
        sc = cv * _sigmoid(cv)
        sc_ref[...] = sc
        blk = _dot(sc.astype(BF16), w_ref[...].astype(BF16)) + b_ref[...]
        out_ref[me] = blk[0:N_DEV, :]
        copies = []
        for k in range(1, N_DEV):
            peer, _ = _peer(k)
            copies.append(pltpu.make_async_remote_copy(
                src_ref=out_ref.at[me], dst_ref=out_ref.at[me], send_sem=send_sems.at[k - 1],
                recv_sem=recv_sems.at[k - 1], device_id=peer, device_id_type=pl.DeviceIdType.MESH))
        for cp in copies:
            cp.start()
        for cp in copies:
            cp.wait()

    vmem = pl.BlockSpec(memory_space=pltpu.VMEM)
    return pl.pallas_call(
        body, name="ada_mod",
        out_shape=(jax.ShapeDtypeStruct((N_DEV, N_DEV, ADA_SHARD), F32),
                   jax.ShapeDtypeStruct(c_all16.shape, F32)),
        in_specs=[vmem] * 3, out_specs=(vmem, vmem),
        scratch_shapes=[pltpu.SemaphoreType.DMA((N_DEV - 1,)), pltpu.SemaphoreType.DMA((N_DEV - 1,))],
    )(c_all16, w_ada_blk, b_ada_blk)


def _exchange_grads(g_in, g_out, g_cw, pack):
    scattered = (g_in, g_out, g_cw)
    n = len(scattered) + 1

    def body(*refs):
        src, dst = refs[:n], refs[n:2 * n]
        send_sems, recv_sems, local_sems = refs[2 * n:]
        x, y, c = _mesh_pos()
        me = 4 * x + 2 * y + c
        local = [pltpu.make_async_copy(src[a].at[me], dst[a].at[me], local_sems.at[a]) for a in range(n - 1)]
        local.append(pltpu.make_async_copy(src[n - 1], dst[n - 1].at[me], local_sems.at[n - 1]))
        for cp in local:
            cp.start()
        copies = []
        for k in range(1, N_DEV):
            peer, p = _peer(k)
            for a in range(n):
                s = src[a].at[p] if a < n - 1 else src[a]
                copies.append(pltpu.make_async_remote_copy(
                    src_ref=s, dst_ref=dst[a].at[me], send_sem=send_sems.at[a, k - 1],
                    recv_sem=recv_sems.at[a, k - 1], device_id=peer, device_id_type=pl.DeviceIdType.MESH))
        for cp in copies:
            cp.start()
        for cp in copies:
            cp.wait()
        for cp in local:
            cp.wait()

    hbm = pl.BlockSpec(memory_space=pltpu.HBM)
    out_shape = tuple(jax.ShapeDtypeStruct(s.shape, s.dtype) for s in scattered)
    out_shape += (jax.ShapeDtypeStruct((N_DEV,) + pack.shape, pack.dtype),)
    return pl.pallas_call(
        body, name="exchange_grads", out_shape=out_shape,
        in_specs=[hbm] * n, out_specs=tuple([hbm] * n),
        scratch_shapes=[pltpu.SemaphoreType.DMA((n, N_DEV - 1)), pltpu.SemaphoreType.DMA((n, N_DEV - 1)),
                        pltpu.SemaphoreType.DMA((n,))],
    )(g_in, g_out, g_cw, pack)


def _fwd_proj(x, norm_g, scale, shift, w_main, w_g):
    seq = x.shape[0]
    tm = min(512, seq)
    tn = 512

    def body(x_ref, ng_ref, sc_ref, sh_ref, w_ref, wg_ref, proj_ref, gates_ref, h_ref, h_scr):
        @pl.when(pl.program_id(1) == 0)
        def _():
            xt = x_ref[...]
            r = lax.rsqrt(jnp.mean(xt * xt, axis=-1, keepdims=True) + EPS)
            h = ((xt * r) * ng_ref[...]) * (1.0 + sc_ref[...]) + sh_ref[...]
            hb = h.astype(BF16)
            h_scr[...] = hb
            h_ref[...] = hb
            gates_ref[...] = _dot(hb, wg_ref[...])

        proj_ref[...] = _dot(h_scr[...], w_ref[...])

    vec = pl.BlockSpec((1, D_MODEL), lambda i, j: (0, 0))
    return pl.pallas_call(
        body, name="fwd_proj", grid=(seq // tm, N_MAIN // tn),
        out_shape=(jax.ShapeDtypeStruct((seq, N_MAIN), F32), jax.ShapeDtypeStruct((seq, 128), F32),
                   jax.ShapeDtypeStruct((seq, D_MODEL), BF16)),
        in_specs=[pl.BlockSpec((tm, D_MODEL), lambda i, j: (i, 0)), vec, vec, vec,
                  pl.BlockSpec((D_MODEL, tn), lambda i, j: (0, j)),
                  pl.BlockSpec((D_MODEL, 128), lambda i, j: (0, 0))],
        out_specs=(pl.BlockSpec((tm, tn), lambda i, j: (i, j)), pl.BlockSpec((tm, 128), lambda i, j: (i, 0)),
                   pl.BlockSpec((tm, D_MODEL), lambda i, j: (i, 0))),
        scratch_shapes=[pltpu.VMEM((tm, D_MODEL), BF16)],
        compiler_params=_params(("arbitrary", "arbitrary")),
    )(x, norm_g, scale, shift, w_main, w_g)


def _gate_forms(gpre):
    r = lax.broadcasted_iota(jnp.int32, (CHUNK, CHUNK), 0)
    c = lax.broadcasted_iota(jnp.int32, (CHUNK, CHUNK), 1)
    causal = c <= r
    ltri = jnp.where(causal, 1.0, 0.0).astype(F32)
    utri = jnp.where(r <= c, 1.0, 0.0).astype(F32)
    bcol = _dot_f32(ltri, _log_sigmoid(gpre))
    gt8 = gpre.T[0:8, :]
    brow = _dot_f32(_log_sigmoid(gt8), utri)
    return causal, utri, bcol, gt8, brow


def _head_fwd(qh, kh, vh, bc, br, igr, m_prev, c_h, n_row, causal):
    dlog = jnp.where(causal, bc - br + igr, NEG_BIG)
    inter_log = bc + m_prev
    m_t = jnp.maximum(inter_log, jnp.max(dlog, axis=-1, keepdims=True))
    dmat = jnp.exp(dlog - m_t)
    inter = jnp.exp(inter_log - m_t)
    qb, kb, vb, cb = qh.astype(BF16), kh.astype(BF16), vh.astype(BF16), c_h.astype(BF16)
    s = _dot_nt(qb, kb) * dmat
    cq = _dot_nt(qb, cb)
    num = _dot(s.astype(BF16), vb) + inter * cq
    nq = jnp.sum(qh * n_row, axis=-1, keepdims=True)
    den = jnp.sum(s, axis=-1, keepdims=True) + inter * nq
    emt = jnp.exp(-m_t)
    dn = jnp.maximum(jnp.abs(den), emt)
    hm = num / dn
    return dict(dmat=dmat, inter=inter, qb=qb, kb=kb, vb=vb, cb=cb, s=s, cq=cq, nq=nq, den=den, emt=emt,
                dn=dn, hm=hm)


def _state_weights(bc, igc, m_prev, m_new=None):
    last = lax.broadcasted_iota(jnp.int32, (CHUNK, 1), 0) == CHUNK - 1
    b_last = jnp.sum(jnp.where(last, bc, 0.0), axis=0, keepdims=True)
    wlog = b_last - bc + igc
    if m_new is None:
        m_new = jnp.maximum(b_last + m_prev, jnp.max(wlog, axis=0, keepdims=True))
    w_c = jnp.exp(wlog - m_new)
    decay = jnp.exp(b_last + m_prev - m_new)
    return w_c, decay, m_new, last


def _conv_pre(xpad_ref, cw_ref, cb_ref):
    a = cb_ref[...]
    for j in range(CONV_WIDTH):
        lo = CONV_HALO - (CONV_WIDTH - 1) + j
        a = a + cw_ref[j:j + 1, :] * xpad_ref[lo:lo + CHUNK, :]
    return a


def _pool_window_sum(upad_ref, g, w):
    lanes = slice(g * POOL_GROUP_DIM, (g + 1) * POOL_GROUP_DIM)
    acc = upad_ref[POOL_HALO:POOL_HALO + CHUNK, lanes]
    for j in range(1, w):
        acc = acc + upad_ref[POOL_HALO - j:POOL_HALO - j + CHUNK, lanes]
    return acc


def _pool_inv_count(chunk_idx, w):
    pos = chunk_idx * CHUNK + lax.broadcasted_iota(jnp.int32, (CHUNK, 1), 0) + 1
    return 1.0 / jnp.minimum(pos, w).astype(F32)


def _mixer_in_specs(cmap, n_chunks):
    def rows(i):
        return cmap(i)
    return [
        pl.BlockSpec((CHUNK, 1024), lambda i: (rows(i), 0)),
        pl.BlockSpec((CHUNK, 1024), lambda i: (rows(i), 1)),
        pl.BlockSpec((CHUNK, 512), lambda i: (rows(i), 4)),
        pl.BlockSpec((CHUNK, 512), lambda i: (rows(i), 5)),
        pl.BlockSpec((CHUNK, 512), lambda i: (rows(i), 6)),
        pl.BlockSpec((POOL_HALO, 512), lambda i: (jnp.maximum(rows(i) * (CHUNK // POOL_HALO) - 1, 0), 0)),
        pl.BlockSpec((CONV_HALO, 1024), lambda i: (jnp.maximum(rows(i) * (CHUNK // CONV_HALO) - 1, 0), 1)),
    ]


def _mix_fwd(proj, gates, bg_pad, conv_w8, conv_b, w_pool, ls_pool, mh_g):
    seq = proj.shape[0]
    n_chunks = seq // CHUNK

    def body(uz_ref, qk_ref, v_ref, o_ref, zm_ref, uh_ref, qkh_ref, g_ref, bg_ref, cw_ref, cb_ref, wp_ref,
             ls_ref, mhg_ref, mix_ref, cst_ref, nst_ref, mst_ref, c_scr, n_scr, m_scr, xpad, upad):
        i = pl.program_id(0)

        @pl.when(i == 0)
        def _():
            c_scr[...] = jnp.zeros_like(c_scr)
            n_scr[...] = jnp.zeros_like(n_scr)
            m_scr[...] = jnp.zeros_like(m_scr)

        cst_ref[0] = c_scr[...]
        nst_ref[0] = n_scr[...]
        mst_ref[0] = m_scr[...]
        first = i == 0

        upad[0:POOL_HALO, :] = jnp.where(first, 0.0, uh_ref[...])
        upad[POOL_HALO:POOL_HALO + CHUNK, :] = uz_ref[:, 0:D_POOL]
        for g, w in enumerate(POOL_WINDOWS):
            lanes = slice(g * POOL_GROUP_DIM, (g + 1) * POOL_GROUP_DIM)
            pooled = _pool_window_sum(upad, g, w) * _pool_inv_count(i, w) - uz_ref[:, lanes]
            y = _dot(pooled.astype(BF16), wp_ref[g].astype(BF16)) * ls_ref[:, lanes]
            zp = uz_ref[:, D_POOL + g * POOL_GROUP_DIM:D_POOL + (g + 1) * POOL_GROUP_DIM]
            mix_ref[:, lanes] = (y * (zp * _sigmoid(zp))).astype(BF16)

        xpad[0:CONV_HALO, :] = jnp.where(first, 0.0, qkh_ref[...])
        xpad[CONV_HALO:CONV_HALO + CHUNK, :] = qk_ref[...]
        a = _conv_pre(xpad, cw_ref, cb_ref)
        qk = a * _sigmoid(a)

        gpre = g_ref[...] + bg_ref[...]
        causal, _, bcol, gt8, brow = _gate_forms(gpre)
        for h in range(N_HEADS):
            lanes = slice(h * HEAD_DIM, (h + 1) * HEAD_DIM)
            qh = qk[:, lanes]
            kh = qk[:, D_MLSTM + h * HEAD_DIM:D_MLSTM + (h + 1) * HEAD_DIM] * (HEAD_DIM ** -0.5)
            vh = v_ref[:, lanes]
            bc = bcol[:, N_HEADS + h:N_HEADS + h + 1]
            br = brow[N_HEADS + h:N_HEADS + h + 1, :]
            igr = gt8[h:h + 1, :]
            igc = gpre[:, h:h + 1]
            m_prev = m_scr[h:h + 1, 0:1]
            c_h = c_scr[h]
            n_row = n_scr[h:h + 1, :]
            f = _head_fwd(qh, kh, vh, bc, br, igr, m_prev, c_h, n_row, causal)
            w_c, decay, m_new, _ = _state_weights(bc, igc, m_prev)
            c_scr[h] = decay * c_h + _dot_tn((vh * w_c).astype(BF16), f["kb"])
            n_scr[h:h + 1, :] = decay * n_row + jnp.sum(w_c * kh, axis=0, keepdims=True)
            m_scr[h:h + 1, :] = jnp.broadcast_to(m_new, (1, 128))
            hm = f["hm"]
            hn = hm * lax.rsqrt(jnp.mean(hm * hm, axis=-1, keepdims=True) + EPS) * mhg_ref[:, lanes]
            zm = zm_ref[:, lanes]
            out = hn * _sigmoid(o_ref[:, lanes]) * (zm * _sigmoid(zm))
            mix_ref[:, D_POOL + h * HEAD_DIM:D_POOL + (h + 1) * HEAD_DIM] = out.astype(BF16)

    cmap = lambda i: i
    in_specs = _mixer_in_specs(cmap, n_chunks) + [
        pl.BlockSpec((CHUNK, 128), lambda i: (i, 0)),
        _full((1, 128)), _full((8, 1024)), _full((1, 1024)), _full((4, 128, 128)), _full((1, 512)),
        _full((1, 512))]
    return pl.pallas_call(
        body, name="mix_fwd", grid=(n_chunks,),
        out_shape=(jax.ShapeDtypeStruct((seq, D_MODEL), BF16),
                   jax.ShapeDtypeStruct((n_chunks, N_HEADS, HEAD_DIM, HEAD_DIM), F32),
                   jax.ShapeDtypeStruct((n_chunks, 8, 128), F32),
                   jax.ShapeDtypeStruct((n_chunks, 8, 128), F32)),
        in_specs=in_specs,
        out_specs=(pl.BlockSpec((CHUNK, D_MODEL), lambda i: (i, 0)),
                   pl.BlockSpec((1, N_HEADS, HEAD_DIM, HEAD_DIM), lambda i: (i, 0, 0, 0)),
                   pl.BlockSpec((1, 8, 128), lambda i: (i, 0, 0)),
                   pl.BlockSpec((1, 8, 128), lambda i: (i, 0, 0))),
        scratch_shapes=[pltpu.VMEM((N_HEADS, HEAD_DIM, HEAD_DIM), F32), pltpu.VMEM((8, 128), F32),
                        pltpu.VMEM((8, 128), F32), pltpu.VMEM((CONV_HALO + CHUNK, 1024), F32),
                        pltpu.VMEM((POOL_HALO + CHUNK, D_POOL), F32)],
        compiler_params=_params(("arbitrary",)),
    )(proj, proj, proj, proj, proj, proj, proj, gates, bg_pad, conv_w8, conv_b, w_pool, ls_pool, mh_g)


def _out_fwd_bwd(mix, x, tgt, w_out_b, gate, final_g):
    seq = x.shape[0]
    tm = min(256, seq)

    def body(mix_ref, x_ref, t_ref, w_ref, gate_ref, fg_ref, dx2_ref, dmix_ref, dwo_ref, dgate_ref, dfg_ref,
             loss_ref):
        @pl.when(pl.program_id(0) == 0)
        def _():
            dwo_ref[...] = jnp.zeros_like(dwo_ref)
            dgate_ref[...] = jnp.zeros_like(dgate_ref)
            dfg_ref[...] = jnp.zeros_like(dfg_ref)
            loss_ref[...] = jnp.zeros_like(loss_ref)

        mixb = mix_ref[...]
        w = w_ref[...]
        gate_v = gate_ref[...]
        fg = fg_ref[...]
        o2 = _dot(mixb, w)
        x2 = x_ref[...] + gate_v * o2
        r2 = lax.rsqrt(jnp.mean(x2 * x2, axis=-1, keepdims=True) + EPS)
        x2n = x2 * r2
        err = x2n * fg - t_ref[...]
        part = 0.5 * jnp.sum(jnp.sum(err * err, axis=-1, keepdims=True), axis=0, keepdims=True) / D_MODEL
        loss_ref[...] += jnp.broadcast_to(part, loss_ref.shape)
        dy = err / D_MODEL
        dfg_ref[...] += jnp.sum(dy * x2n, axis=0, keepdims=True)
        gdy = dy * fg
        dx2 = r2 * (gdy - x2n * jnp.mean(gdy * x2n, axis=-1, keepdims=True))
        dx2_ref[...] = dx2
        dgate_ref[...] += jnp.sum(dx2 * o2, axis=0, keepdims=True)
        do2 = (dx2 * gate_v).astype(BF16)
        dmix_ref[...] = _dot_nt(do2, w)
        dwo_ref[...] += _dot_tn(mixb, do2)

    tile = pl.BlockSpec((tm, D_MODEL), lambda i: (i, 0))
    vec = _full((1, D_MODEL))
    return pl.pallas_call(
        body, name="out_fwd_bwd", grid=(seq // tm,),
        out_shape=(jax.ShapeDtypeStruct((seq, D_MODEL), F32), jax.ShapeDtypeStruct((seq, D_MODEL), F32),
                   jax.ShapeDtypeStruct((D_MODEL, D_MODEL), F32), jax.ShapeDtypeStruct((1, D_MODEL), F32),
                   jax.ShapeDtypeStruct((1, D_MODEL), F32), jax.ShapeDtypeStruct((1, 128), F32)),
        in_specs=[tile, tile, tile, _full((D_MODEL, D_MODEL)), vec, vec],
        out_specs=(tile, tile, _full((D_MODEL, D_MODEL)), vec, vec, _full((1, 128))),
        compiler_params=_params(("arbitrary",)),
    )(mix, x, tgt, w_out_b, gate, final_g)


def _mix_bwd(proj, gates, dmix, cst, nst, mst, bg_pad, conv_w8, conv_b, w_pool, ls_pool, mh_g):
    seq = proj.shape[0]
    n_chunks = seq // CHUNK

    def body(uz_ref, qk_ref, v_ref, o_ref, zm_ref, uh_ref, qkh_ref, g_ref, dmix_ref, cst_ref, nst_ref, mst_ref,
             mnx_ref, bg_ref, cw_ref, cb_ref, wp_ref, ls_ref, mhg_ref,
             dp_ref, dcw_ref, dcb_ref, dwp_ref, dls_ref, dmhg_ref, dbg_ref,
             dc_scr, dn_scr, xpad, upad, dapad, dpipad):
        i = pl.program_id(0)
        ci = n_chunks - 1 - i

        @pl.when(i == 0)
        def _():
            for ref in (dc_scr, dn_scr, dcw_ref, dcb_ref, dwp_ref, dls_ref, dmhg_ref, dbg_ref):
                ref[...] = jnp.zeros_like(ref)
            dapad[CHUNK:CHUNK + CONV_HALO, :] = jnp.zeros((CONV_HALO, 1024), F32)
            dpipad[CHUNK:CHUNK + POOL_HALO, :] = jnp.zeros((POOL_HALO, D_POOL), F32)

        first = ci == 0

        upad[0:POOL_HALO, :] = jnp.where(first, 0.0, uh_ref[...])
        upad[POOL_HALO:POOL_HALO + CHUNK, :] = uz_ref[:, 0:D_POOL]
        dpooled = []
        for g, w in enumerate(POOL_WINDOWS):
            lanes = slice(g * POOL_GROUP_DIM, (g + 1) * POOL_GROUP_DIM)
            zlanes = slice(D_POOL + g * POOL_GROUP_DIM, D_POOL + (g + 1) * POOL_GROUP_DIM)
            inv = _pool_inv_count(ci, w)
            pooled = _pool_window_sum(upad, g, w) * inv - uz_ref[:, lanes]
            pb = pooled.astype(BF16)
            wpb = wp_ref[g].astype(BF16)
            yw = _dot(pb, wpb)
            ls = ls_ref[:, lanes]
            zp = uz_ref[:, zlanes]
            sg = _sigmoid(zp)
            dpo = dmix_ref[:, lanes]
            dp_ref[:, zlanes] = (dpo * (yw * ls) * (sg * (1.0 + zp * (1.0 - sg)))).astype(BF16)
            dy = dpo * (zp * sg)
            dls_ref[:, lanes] += jnp.sum(dy * yw, axis=0, keepdims=True)
            dyw = (dy * ls).astype(BF16)
            dwp_ref[g] += _dot_tn(pb, dyw)
            dpl = _dot_nt(dyw, wpb)
            dpooled.append(dpl)
            dpipad[0:CHUNK, lanes] = dpl * inv
        for g, w in enumerate(POOL_WINDOWS):
            lanes = slice(g * POOL_GROUP_DIM, (g + 1) * POOL_GROUP_DIM)
            du = -dpooled[g]
            for j in range(w):
                du = du + dpipad[j:j + CHUNK, lanes]
            dp_ref[:, lanes] = du.astype(BF16)
        dpipad[CHUNK:CHUNK + POOL_HALO, :] = dpipad[0:POOL_HALO, :]

        xpad[0:CONV_HALO, :] = jnp.where(first, 0.0, qkh_ref[...])
        xpad[CONV_HALO:CONV_HALO + CHUNK, :] = qk_ref[...]
        a = _conv_pre(xpad, cw_ref, cb_ref)
        sga = _sigmoid(a)
        qk = a * sga
        dsilu_a = sga * (1.0 + a * (1.0 - sga))

        gpre = g_ref[...] + bg_ref[...]
        causal, utri, bcol, gt8, brow = _gate_forms(gpre)
        lane = lax.broadcasted_iota(jnp.int32, (CHUNK, 128), 1)
        row = lax.broadcasted_iota(jnp.int32, (CHUNK, 128), 0)
        colsum_rows = jnp.zeros((CHUNK, 128), F32)
        dig_cols = jnp.zeros((CHUNK, 128), F32)
        db_cols = jnp.zeros((CHUNK, 128), F32)
        scale_k = HEAD_DIM ** -0.5

        for h in range(N_HEADS):
            lanes = slice(h * HEAD_DIM, (h + 1) * HEAD_DIM)
            klanes = slice(D_MLSTM + h * HEAD_DIM, D_MLSTM + (h + 1) * HEAD_DIM)
            qh = qk[:, lanes]
            kh = qk[:, klanes] * scale_k
            vh = v_ref[:, lanes]
            bc = bcol[:, N_HEADS + h:N_HEADS + h + 1]
            br = brow[N_HEADS + h:N_HEADS + h + 1, :]
            igr = gt8[h:h + 1, :]
            igc = gpre[:, h:h + 1]
            m_prev = mst_ref[0, h:h + 1, 0:1]
            m_next = mnx_ref[0, h:h + 1, 0:1]
            c_h = cst_ref[0, h]
            n_row = nst_ref[0, h:h + 1, :]
            f = _head_fwd(qh, kh, vh, bc, br, igr, m_prev, c_h, n_row, causal)
            qb, kb, vb, cb = f["qb"], f["kb"], f["vb"], f["cb"]
            s, dmat, inter, den, dn, hm = f["s"], f["dmat"], f["inter"], f["den"], f["dn"], f["hm"]

            rinv = lax.rsqrt(jnp.mean(hm * hm, axis=-1, keepdims=True) + EPS)
            hmn = hm * rinv
            gh = mhg_ref[:, lanes]
            o_pre = o_ref[:, lanes]
            og = _sigmoid(o_pre)
            zm = zm_ref[:, lanes]
            sgz = _sigmoid(zm)
            sz = zm * sgz
            dout = dmix_ref[:, D_POOL + h * HEAD_DIM:D_POOL + (h + 1) * HEAD_DIM]
            hn = hmn * gh
            dp_ref[:, 2560 + h * HEAD_DIM:2560 + (h + 1) * HEAD_DIM] = (
                dout * hn * sz * og * (1.0 - og)).astype(BF16)
            dp_ref[:, 3072 + h * HEAD_DIM:3072 + (h + 1) * HEAD_DIM] = (
                dout * hn * og * (sgz * (1.0 + zm * (1.0 - sgz)))).astype(BF16)
            dhn = dout * og * sz
            dmhg_ref[:, lanes] += jnp.sum(dhn * hmn, axis=0, keepdims=True)
            dyn = dhn * gh
            dhm = rinv * (dyn - hmn * jnp.mean(dyn * hmn, axis=-1, keepdims=True))

            inv_dn = 1.0 / dn
            dnum = dhm * inv_dn
            hd = jnp.sum(dhm * hm, axis=-1, keepdims=True)
            dden = jnp.where(jnp.abs(den) > f["emt"], -hd * inv_dn * jnp.sign(den), 0.0)
            dnb = dnum.astype(BF16)
            ds = _dot_nt(dnb, vb) + dden
            dv = _dot_tn(s.astype(BF16), dnb)
            dqk = (ds * dmat).astype(BF16)
            dq = _dot(dqk, kb) + inter * (_dot(dnb, cb) + dden * n_row)
            dk = _dot_tn(dqk, qb)
            gmat = ds * s
            row_g = jnp.sum(gmat, axis=-1, keepdims=True)
            col_g = jnp.sum(gmat, axis=0, keepdims=True)
            gcol = inter * (jnp.sum(dnum * f["cq"], axis=-1, keepdims=True) + dden * f["nq"])
            dc_prev = _dot_tn((inter * dnum).astype(BF16), qb)
            dn_prev = jnp.sum((inter * dden) * qh, axis=0, keepdims=True)

            w_c, decay, _, last = _state_weights(bc, igc, m_prev, m_next)
            dcn = dc_scr[h]
            dnn = dn_scr[h:h + 1, :]
            ddecay = (jnp.sum(jnp.sum(dcn * c_h, axis=-1, keepdims=True), axis=0, keepdims=True)
                      + jnp.sum(dnn * n_row, axis=-1, keepdims=True))
            dcnb = dcn.astype(BF16)
            amat = _dot(vb, dcnb) + dnn
            dw = jnp.sum(amat * kh, axis=-1, keepdims=True)
            dv = dv + w_c * _dot_nt(kb, dcnb)
            dk = dk + w_c * amat
            e = dw * w_c
            db_last = ddecay * decay + jnp.sum(e, axis=0, keepdims=True)
            db_c = row_g + gcol - e + jnp.where(last, db_last, 0.0)
            dc_scr[h] = decay * dcn + dc_prev
            dn_scr[h:h + 1, :] = decay * dnn + dn_prev

            colsum_rows = colsum_rows + jnp.where(row == h, col_g, 0.0)
            dig_cols = dig_cols + jnp.where(lane == h, e, 0.0)
            db_cols = db_cols + jnp.where(lane == N_HEADS + h, db_c, 0.0)

            dp_ref[:, 2048 + h * HEAD_DIM:2048 + (h + 1) * HEAD_DIM] = dv.astype(BF16)
            dapad[0:CHUNK, lanes] = dq * dsilu_a[:, lanes]
            dapad[0:CHUNK, klanes] = dk * scale_k * dsilu_a[:, klanes]

        cs_t = colsum_rows.T
        dig_all = dig_cols + cs_t
        shifted = jnp.zeros((CHUNK, 128), F32)
        for h in range(N_HEADS):
            shifted = shifted + jnp.where(lane == N_HEADS + h, cs_t[:, h:h + 1], 0.0)
        dlf = _dot_f32(utri, db_cols - shifted)
        dgates = dig_all + dlf * _sigmoid(-gpre)
        dp_ref[:, N_MAIN:N_MAIN + 128] = dgates.astype(BF16)
        dp_ref[:, N_MAIN + 128:N_PAD] = jnp.zeros((CHUNK, N_PAD - N_MAIN - 128), BF16)
        dbg_ref[...] += jnp.sum(dgates, axis=0, keepdims=True)

        da = dapad[0:CHUNK, :]
        dcb_ref[...] += jnp.sum(da, axis=0, keepdims=True)
        dx = jnp.zeros((CHUNK, 1024), F32)
        for j in range(CONV_WIDTH):
            lo = CONV_HALO - (CONV_WIDTH - 1) + j
            dcw_ref[j:j + 1, :] += jnp.sum(da * xpad[lo:lo + CHUNK, :], axis=0, keepdims=True)
            hi = CONV_WIDTH - 1 - j
            dx = dx + cw_ref[j:j + 1, :] * dapad[hi:hi + CHUNK, :]
        dp_ref[:, 1024:2048] = dx.astype(BF16)
        dapad[CHUNK:CHUNK + CONV_HALO, :] = dapad[0:CONV_HALO, :]

    cmap = lambda i: n_chunks - 1 - i
    in_specs = _mixer_in_specs(cmap, n_chunks) + [
        pl.BlockSpec((CHUNK, 128), lambda i: (cmap(i), 0)),
        pl.BlockSpec((CHUNK, D_MODEL), lambda i: (cmap(i), 0)),
        pl.BlockSpec((1, N_HEADS, HEAD_DIM, HEAD_DIM), lambda i: (cmap(i), 0, 0, 0)),
        pl.BlockSpec((1, 8, 128), lambda i: (cmap(i), 0, 0)),
        pl.BlockSpec((1, 8, 128), lambda i: (cmap(i), 0, 0)),
        pl.BlockSpec((1, 8, 128), lambda i: (jnp.minimum(cmap(i) + 1, n_chunks - 1), 0, 0)),
        _full((1, 128)), _full((8, 1024)), _full((1, 1024)), _full((4, 128, 128)), _full((1, 512)),
        _full((1, 512))]
    return pl.pallas_call(
        body, name="mix_bwd", grid=(n_chunks,),
        out_shape=(jax.ShapeDtypeStruct((seq, N_PAD), BF16), jax.ShapeDtypeStruct((8, 1024), F32),
                   jax.ShapeDtypeStruct((1, 1024), F32), jax.ShapeDtypeStruct((4, 128, 128), F32),
                   jax.ShapeDtypeStruct((1, 512), F32), jax.ShapeDtypeStruct((1, 512), F32),
                   jax.ShapeDtypeStruct((1, 128), F32)),
        in_specs=in_specs,
        out_specs=(pl.BlockSpec((CHUNK, N_PAD), lambda i: (cmap(i), 0)), _full((8, 1024)), _full((1, 1024)),
                   _full((4, 128, 128)), _full((1, 512)), _full((1, 512)), _full((1, 128))),
        scratch_shapes=[pltpu.VMEM((N_HEADS, HEAD_DIM, HEAD_DIM), F32), pltpu.VMEM((8, 128), F32),
                        pltpu.VMEM((CONV_HALO + CHUNK, 1024), F32), pltpu.VMEM((POOL_HALO + CHUNK, D_POOL), F32),
                        pltpu.VMEM((CHUNK + CONV_HALO, 1024), F32), pltpu.VMEM((CHUNK + POOL_HALO, D_POOL), F32)],
        compiler_params=_params(("arbitrary",)),
    )(proj, proj, proj, proj, proj, proj, proj, gates, dmix, cst, nst, mst, mst, bg_pad, conv_w8, conv_b,
      w_pool, ls_pool, mh_g)


def _bwd_in(dproj, w_in_t, x, dx2, norm_g, scale):
    seq = x.shape[0]
    tm = min(256, seq)

    def body(dp_ref, wt_ref, x_ref, dx2_ref, ng_ref, sc_ref, gx_ref, dsh_ref, dsc_ref, dng_ref):
        @pl.when(pl.program_id(0) == 0)
        def _():
            dsh_ref[...] = jnp.zeros_like(dsh_ref)
            dsc_ref[...] = jnp.zeros_like(dsc_ref)
            dng_ref[...] = jnp.zeros_like(dng_ref)

        dh = _dot(dp_ref[...], wt_ref[...])
        xt = x_ref[...]
        r = lax.rsqrt(jnp.mean(xt * xt, axis=-1, keepdims=True) + EPS)
        xn = xt * r
        ng = ng_ref[...]
        one_sc = 1.0 + sc_ref[...]
        dsh_ref[...] += jnp.sum(dh, axis=0, keepdims=True)
        dhxn = dh * xn
        dsc_ref[...] += jnp.sum(dhxn * ng, axis=0, keepdims=True)
        dng_ref[...] += jnp.sum(dhxn * one_sc, axis=0, keepdims=True)
        dxn = dh * (ng * one_sc)
        gx_ref[...] = r * (dxn - xn * jnp.mean(dxn * xn, axis=-1, keepdims=True)) + dx2_ref[...]

    tile = pl.BlockSpec((tm, D_MODEL), lambda i: (i, 0))
    vec = _full((1, D_MODEL))
    return pl.pallas_call(
        body, name="bwd_in", grid=(seq // tm,),
        out_shape=(jax.ShapeDtypeStruct((seq, D_MODEL), F32),) + (jax.ShapeDtypeStruct((1, D_MODEL), F32),) * 3,
        in_specs=[pl.BlockSpec((tm, N_PAD), lambda i: (i, 0)), _full((N_PAD, D_MODEL)), tile, tile, vec, vec],
        out_specs=(tile, vec, vec, vec),
        compiler_params=_params(("arbitrary",)),
    )(dproj, w_in_t, x, dx2, norm_g, scale)


def _dw_in(h_b, dproj):
    seq = h_b.shape[0]
    tk = min(1024, seq)
    tn = 768

    def body(h_ref, dp_ref, dw_ref):
        @pl.when(pl.program_id(1) == 0)
        def _():
            dw_ref[...] = jnp.zeros_like(dw_ref)

        dw_ref[...] += _dot_tn(h_ref[...], dp_ref[...])

    return pl.pallas_call(
        body, name="dw_in", grid=(N_PAD // tn, seq // tk),
        out_shape=jax.ShapeDtypeStruct((D_MODEL, N_PAD), F32),
        in_specs=[pl.BlockSpec((tk, D_MODEL), lambda j, t: (t, 0)), pl.BlockSpec((tk, tn), lambda j, t: (t, j))],
        out_specs=pl.BlockSpec((D_MODEL, tn), lambda j, t: (0, j)),
        compiler_params=_params(("arbitrary", "arbitrary")),
    )(h_b, dproj)


def _adam_update(g, w, m, v, g_ref, d_ref, m_ref, v_ref):
    mn = ADAM_B1 * m + (1.0 - ADAM_B1) * g
    vn = ADAM_B2 * v + (1.0 - ADAM_B2) * (g * g)
    m_hat = mn / (1.0 - ADAM_B1 ** ADAM_STEP)
    v_hat = vn / (1.0 - ADAM_B2 ** ADAM_STEP)
    g_ref[...] = g
    d_ref[...] = -ADAM_LR * (m_hat / (jnp.sqrt(v_hat) + ADAM_EPS) + ADAM_WD * w)
    m_ref[...] = mn
    v_ref[...] = vn


def _adam_sum(name, parts, w, m, v, row_tile):
    rows, cols = w.shape

    def body(p_ref, w_ref, m_ref, v_ref, g_out, d_out, m_out, v_out):
        g = p_ref[0]
        for j in range(1, N_DEV):
            g = g + p_ref[j]
        _adam_update(g, w_ref[...], m_ref[...], v_ref[...], g_out, d_out, m_out, v_out)

    tile = pl.BlockSpec((row_tile, cols), lambda i: (i, 0))
    return pl.pallas_call(
        body, name=name, grid=(rows // row_tile,),
        out_shape=(jax.ShapeDtypeStruct((rows, cols), F32),) * 4,
        in_specs=[pl.BlockSpec((N_DEV, row_tile, cols), lambda i: (0, i, 0)), tile, tile, tile],
        out_specs=(tile,) * 4,
        compiler_params=_params(("arbitrary",)),
    )(parts, w, m, v)


def _adam_ada(sc_all16, dmod_blk16, w, m, v):
    rows, cols = w.shape

    def body(sc_ref, dm_ref, w_ref, m_ref, v_ref, g_out, d_out, m_out, v_out):
        g = _dot_tn(sc_ref[...].astype(BF16), dm_ref[...].astype(BF16))
        _adam_update(g, w_ref[...], m_ref[...], v_ref[...], g_out, d_out, m_out, v_out)

    return pl.pallas_call(
        body, name="adam_w_ada", grid=(1,),
        out_shape=(jax.ShapeDtypeStruct((rows, cols), F32),) * 4,
        in_specs=[_full(sc_all16.shape), _full(dmod_blk16.shape)] + [_full((rows, cols))] * 3,
        out_specs=(_full((rows, cols)),) * 4,
        compiler_params=_params(("arbitrary",)),
    )(sc_all16, dmod_blk16, w, m, v)


def _pack_small(pieces):
    rows = []
    for name, n in PACK_ROWS:
        a = pieces[name].reshape(-1).astype(F32)
        a = jnp.pad(a, (0, n * 128 - a.shape[0]))
        rows.append(a.reshape(n, 128))
    return jnp.concatenate(rows, axis=0)


def _unpack_small(pack, name, shape):
    off, _ = PACK_OFF[name]
    size = 1
    for s in shape:
        size *= s
    n_rows = -(-size // 128)
    return pack[off:off + n_rows].reshape(-1)[:size].reshape(shape)


def _local_step(x2, tgt2, shift, scale, gate, norm_g, w_full_b, w_out_b, conv_w, conv_b, w_pool, ls_pool,
                mh_norm_g, b_gates, final_g):
    w_main = w_full_b[:, :N_MAIN]
    w_g = jnp.pad(w_full_b[:, N_MAIN:], ((0, 0), (0, 128 - (N_IN - N_MAIN))))
    w_in_t = jnp.pad(w_full_b.T, ((0, N_PAD - N_IN), (0, 0)))
    bg_pad = jnp.pad(b_gates, ((0, 0), (0, 128 - b_gates.shape[1])))
    conv_w8 = jnp.pad(conv_w, ((0, 8 - CONV_WIDTH), (0, 0)))
    fg = final_g.reshape(1, D_MODEL)

    proj, gates, h_b = _fwd_proj(x2, norm_g, scale, shift, w_main, w_g)
    mix, cst, nst, mst = _mix_fwd(proj, gates, bg_pad, conv_w8, conv_b, w_pool, ls_pool, mh_norm_g)
    dx2, dmix, dwo, dgate, dfg, loss = _out_fwd_bwd(mix, x2, tgt2, w_out_b, gate, fg)
    dproj, dcw8, dcb, dwp, dls, dmhg, dbg = _mix_bwd(proj, gates, dmix, cst, nst, mst, bg_pad, conv_w8, conv_b,
                                                      w_pool, ls_pool, mh_norm_g)
    gx, dsh, dsc, dng = _bwd_in(dproj, w_in_t, x2, dx2, norm_g, scale)
    dw_in = _dw_in(h_b, dproj)
    return dict(loss=loss, grad_x=gx, dw_in=dw_in[:, :N_IN], dw_out=dwo, dconv_w=dcw8[:CONV_WIDTH], conv_b=dcb,
                w_pool=dwp, ls_pool=dls, mh_norm_g=dmhg, b_gates=dbg, final_g=dfg, norm_g=dng,
                dmod=jnp.concatenate([dsh, dsc, dgate], axis=1))


def kernel(x, c, norm_g, w_ada, b_ada, w_in, b_gates, conv_w, conv_b, w_pool, ls_pool, mh_norm_g, w_out, final_g, loss_target, m_norm_g, m_w_ada, m_b_ada, m_w_in, m_b_gates, m_conv_w, m_conv_b, m_w_pool, m_ls_pool, m_mh_norm_g, m_w_out, m_final_g, v_norm_g, v_w_ada, v_b_ada, v_w_in, v_b_gates, v_conv_w, v_conv_b, v_w_pool, v_ls_pool, v_mh_norm_g, v_w_out, v_final_g):
    seq = x.shape[1]
    me = 4 * lax.axis_index("x") + 2 * lax.axis_index("y") + lax.axis_index("c")

    g_in, g_out, g_cw, g_c = _gather_weights(w_in[0].astype(BF16), w_out[0].astype(BF16), conv_w[0], c)
    w_full_b = jnp.transpose(g_in, (1, 0, 2)).reshape(D_MODEL, N_IN)
    w_out_b = g_out.reshape(D_MODEL, D_MODEL)
    conv_w_full = jnp.transpose(g_cw, (1, 0, 2)).reshape(CONV_WIDTH, 2 * D_MLSTM)
    c_all16 = jnp.pad(g_c.reshape(N_DEV, D_MODEL), ((0, 8), (0, 0)))

    b_ada_blk = lax.dynamic_slice(b_ada, (0, me * ADA_SHARD), (1, ADA_SHARD))
    mod_all, sc_all16 = _ada_mod(c_all16, w_ada[0], b_ada_blk)
    mod = lax.dynamic_index_in_dim(mod_all, me, axis=1, keepdims=False).reshape(1, 3 * D_MODEL)
    shift, scale, gate = mod[:, :D_MODEL], mod[:, D_MODEL:2 * D_MODEL], mod[:, 2 * D_MODEL:]

    r = _local_step(x[0], loss_target[0], shift, scale, gate, norm_g, w_full_b, w_out_b, conv_w_full, conv_b,
                    w_pool[0], ls_pool, mh_norm_g, b_gates, final_g)

    dw_in_blocks = jnp.transpose(r["dw_in"].reshape(D_MODEL, N_DEV, N_SHARD), (1, 0, 2))
    dw_out_blocks = r["dw_out"].reshape(N_DEV, D_MODEL // N_DEV, D_MODEL)
    dcw_blocks = jnp.transpose(r["dconv_w"].reshape(CONV_WIDTH, N_DEV, 128), (1, 0, 2))
    pack = _pack_small(dict(loss=r["loss"][:, :1], final_g=r["final_g"], norm_g=r["norm_g"], conv_b=r["conv_b"],
                            ls_pool=r["ls_pool"], mh_norm_g=r["mh_norm_g"], b_gates=r["b_gates"][:, :8],
                            b_ada=r["dmod"], w_pool=r["w_pool"]))
    p_in, p_out, p_cw, p_pack = _exchange_grads(dw_in_blocks, dw_out_blocks, dcw_blocks, pack)

    gi, di, mi, vi = _adam_sum("adam_w_in", p_in, w_in[0], m_w_in[0], v_w_in[0], 256)
    go, do_, mo, vo = _adam_sum("adam_w_out", p_out, w_out[0], m_w_out[0], v_w_out[0], 128)
    gc, dc, mc, vc = _adam_sum("adam_conv_w", p_cw, conv_w[0], m_conv_w[0], v_conv_w[0], CONV_WIDTH)

    def small(loss_like, fg_, ng_, cb_, ls_, mh_, bg_, ba_, wp_):
        return _pack_small(dict(loss=loss_like, final_g=fg_, norm_g=ng_, conv_b=cb_, ls_pool=ls_, mh_norm_g=mh_,
                                b_gates=bg_, b_ada=ba_, w_pool=wp_))

    zero = jnp.zeros((1, 1), F32)
    w_pack = small(zero, final_g, norm_g, conv_b, ls_pool, mh_norm_g, b_gates, b_ada, w_pool)
    m_pack = small(zero, m_final_g, m_norm_g, m_conv_b, m_ls_pool, m_mh_norm_g, m_b_gates, m_b_ada, m_w_pool)
    v_pack = small(zero, v_final_g, v_norm_g, v_conv_b, v_ls_pool, v_mh_norm_g, v_b_gates, v_b_ada, v_w_pool)
    gp, dp, mp, vp = _adam_sum("adam_small", p_pack, w_pack, m_pack, v_pack, PACK_TOTAL)

    off, rows = PACK_OFF["b_ada"]
    dmod_all = p_pack[:, off:off + rows, :].reshape(N_DEV, 3 * D_MODEL)
    dmod_blk16 = jnp.pad(lax.dynamic_slice(dmod_all, (0, me * ADA_SHARD), (N_DEV, ADA_SHARD)), ((0, 8), (0, 0)))
    ga, da, ma, va = _adam_ada(sc_all16, dmod_blk16, w_ada[0], m_w_ada[0], v_w_ada[0])

    names = ("norm_g", "w_ada", "b_ada", "w_in", "b_gates", "conv_w", "conv_b", "w_pool", "ls_pool", "mh_norm_g",
             "w_out", "final_g")
    shapes = dict(norm_g=norm_g.shape, b_ada=b_ada.shape, b_gates=b_gates.shape, conv_b=conv_b.shape,
                  w_pool=w_pool.shape, ls_pool=ls_pool.shape, mh_norm_g=mh_norm_g.shape, final_g=final_g.shape)
    sharded = dict(w_ada=(ga, da, ma, va), w_in=(gi, di, mi, vi), conv_w=(gc, dc, mc, vc), w_out=(go, do_, mo, vo))
    outs = []
    for kind in range(4):
        for nm in names:
            if nm in sharded:
                outs.append(sharded[nm][kind][None])
            else:
                outs.append(_unpack_small((gp, dp, mp, vp)[kind], nm, shapes[nm]))
    loss = gp[0, 0]
    grad_x = r["grad_x"].reshape(1, seq, D_MODEL)
    return (loss, grad_x, *outs)
```

```python
import jax
import jax.numpy as jnp
from jax import lax
from jax.experimental import pallas as pl
from jax.experimental.pallas import tpu as pltpu

F32 = jnp.float32
BF16 = jnp.bfloat16

D_MODEL = 1024
D_POOL = 512
D_MLSTM = 512
N_HEADS = 4
HEAD_DIM = 128
CHUNK = 128
POOL_WINDOWS = (2, 4, 8, 16)
POOL_GROUP_DIM = 128
CONV_WIDTH = 4
EPS = 1e-6
N_MAIN = 3584
N_IN = 3592
N_PAD = 3840
N_SHARD = N_IN // 8
ADA_SHARD = 3 * D_MODEL // 8
N_DEV = 8
CONV_HALO = 8
POOL_HALO = 16
NEG_BIG = -1e30
VMEM_LIMIT_BYTES = 56 * 1024 * 1024

ADAM_LR = 0.001
ADAM_B1 = 0.9
ADAM_B2 = 0.999
ADAM_EPS = 1e-08
ADAM_WD = 0.01
ADAM_STEP = 10

PACK_ROWS = (("loss", 8), ("final_g", 8), ("norm_g", 8), ("conv_b", 8), ("ls_pool", 8),
             ("mh_norm_g", 8), ("b_gates", 8), ("b_ada", 24), ("w_pool", 512))
PACK_TOTAL = sum(r for _, r in PACK_ROWS)


def _pack_offsets():
    off, out = 0, {}
    for name, rows in PACK_ROWS:
        out[name] = (off, rows)
        off += rows
    return out


PACK_OFF = _pack_offsets()


def _dot(a, b):
    return jnp.dot(a, b, preferred_element_type=F32)


def _dot_nt(a, b):
    return lax.dot_general(a, b, (((1,), (1,)), ((), ())), preferred_element_type=F32)


def _dot_tn(a, b):
    return lax.dot_general(a, b, (((0,), (0,)), ((), ())), preferred_element_type=F32)


def _dot_f32(a, b):
    return jnp.dot(a, b, precision=lax.Precision.HIGHEST, preferred_element_type=F32)


def _sigmoid(x):
    return jax.nn.sigmoid(x)


def _log_sigmoid(x):
    return jnp.minimum(x, 0.0) - jnp.log1p(jnp.exp(-jnp.abs(x)))


def _params(sem):
    return pltpu.CompilerParams(dimension_semantics=sem, vmem_limit_bytes=VMEM_LIMIT_BYTES)


def _full(shape):
    n = len(shape)
    return pl.BlockSpec(shape, lambda *_: (0,) * n)


def _mesh_pos():
    return lax.axis_index("x"), lax.axis_index("y"), lax.axis_index("c")


def _peer(k):
    x, y, c = _mesh_pos()
    px = 1 - x if (k >> 2) & 1 else x
    py = 1 - y if (k >> 1) & 1 else y
    pc = 1 - c if k & 1 else c
    return (px, py, pc), 4 * px + 2 * py + pc


def _remote(src, dst, send_sem, recv_sem, to):
    return pltpu.make_async_remote_copy(src_ref=src, dst_ref=dst, send_sem=send_sem, recv_sem=recv_sem, device_id=to,
                                        device_id_type=pl.DeviceIdType.MESH)


def _other_chips():
    x, y, _ = _mesh_pos()
    return [(1 - x, y), (x, 1 - y), (1 - x, 1 - y)]


def _two_level_gather(src, dst, send_sems, recv_sems, local_sems):
    n = len(src)
    x, y, c = _mesh_pos()
    me = 4 * x + 2 * y + c
    sibling = (x, y, 1 - c)
    chips = _other_chips()

    def copy(a, k, block, to, own):
        return _remote(src[a] if own else dst[a].at[block], dst[a].at[block], send_sems.at[a, k], recv_sems.at[a, k], to)

    local = [pltpu.make_async_copy(src[a], dst[a].at[me], local_sems.at[a]) for a in range(n)]
    first = [copy(a, 0, me, sibling, True) for a in range(n)]
    first += [copy(a, 1 + j, me, (*chip, c), True) for j, chip in enumerate(chips) for a in range(n)]
    for cp in local + first:
        cp.start()
    passed = []
    for j, (px, py) in enumerate(chips):
        block = 4 * px + 2 * py + c
        for a in range(n):
            copy(a, 1 + j, block, sibling, False).wait_recv()
            passed.append(copy(a, 4 + j, block, sibling, False))
            passed[-1].start()
    for a in range(n):
        copy(a, 0, 4 * x + 2 * y + (1 - c), sibling, False).wait_recv()
    for j, (px, py) in enumerate(chips):
        for a in range(n):
            copy(a, 4 + j, 4 * px + 2 * py + (1 - c), sibling, False).wait_recv()
    for cp in first + passed:
        cp.wait_send()
    for cp in local:
        cp.wait()


GATHER_COPIES = 7


def _gather_weights(w_in_b, w_out_b, conv_w, c_row):
    shards = (w_in_b, w_out_b, conv_w, c_row)
    n = len(shards)

    def body(*refs):
        _two_level_gather(refs[:n], refs[n:2 * n], *refs[2 * n:])

    hbm = pl.BlockSpec(memory_space=pltpu.HBM)
    return pl.pallas_call(
        body, name="gather_weights",
        out_shape=tuple(jax.ShapeDtypeStruct((N_DEV,) + s.shape, s.dtype) for s in shards),
        in_specs=[hbm] * n, out_specs=tuple([hbm] * n),
        scratch_shapes=[pltpu.SemaphoreType.DMA((n, GATHER_COPIES)), pltpu.SemaphoreType.DMA((n, GATHER_COPIES)),
                        pltpu.SemaphoreType.DMA((n,))],
    )(*shards)


def _ada_mod(c_all16, w_ada_blk, b_ada_blk):
    def body(c_ref, w_ref, b_ref, out_ref, sc_ref, send_sems, recv_sems):
        x, y, c = _mesh_pos()
        me = 4 * x + 2 * y + c
        cv = c_ref[...]
        sc = cv * _sigmoid(cv)
        sc_ref[...] = sc
        blk = _dot(sc.astype(BF16), w_ref[...].astype(BF16)) + b_ref[...]
        out_ref[me] = blk[0:N_DEV, :]
        copies = []
        for k in range(1, N_DEV):
            peer, _ = _peer(k)
            copies.append(pltpu.make_async_remote_copy(
                src_ref=out_ref.at[me], dst_ref=out_ref.at[me], send_sem=send_sems.at[k - 1],
                recv_sem=recv_sems.at[k - 1], device_id=peer, device_id_type=pl.DeviceIdType.MESH))
        for cp in copies:
            cp.start()
        for cp in copies:
            cp.wait()

    vmem = pl.BlockSpec(memory_space=pltpu.VMEM)
    return pl.pallas_call(
        body, name="ada_mod",
        out_shape=(jax.ShapeDtypeStruct((N_DEV, N_DEV, ADA_SHARD), F32),
                   jax.ShapeDtypeStruct(c_all16.shape, F32)),
        in_specs=[vmem] * 3, out_specs=(vmem, vmem),
        scratch_shapes=[pltpu.SemaphoreType.DMA((N_DEV - 1,)), pltpu.SemaphoreType.DMA((N_DEV - 1,))],
    )(c_all16, w_ada_blk, b_ada_blk)


N_CHIPS = 4


def _pair_swap(g_in, g_out, g_cw):
    parts = (g_in, g_out, g_cw)
    n = len(parts)

    def body(*refs):
        src, dst = refs[:n], refs[n:2 * n]
        send_sems, recv_sems = refs[2 * n:]
        x, y, c = _mesh_pos()
        copies = [_remote(src[a].at[2 * q + (1 - c)], dst[a].at[q], send_sems.at[a, q], recv_sems.at[a, q],
                          (x, y, 1 - c)) for a in range(n) for q in range(N_CHIPS)]
        for cp in copies:
            cp.start()
        for cp in copies:
            cp.wait()

    hbm = pl.BlockSpec(memory_space=pltpu.HBM)
    return pl.pallas_call(
        body, name="pair_swap",
        out_shape=tuple(jax.ShapeDtypeStruct((N_CHIPS,) + s.shape[1:], s.dtype) for s in parts),
        in_specs=[hbm] * n, out_specs=tuple([hbm] * n),
        scratch_shapes=[pltpu.SemaphoreType.DMA((n, N_CHIPS)), pltpu.SemaphoreType.DMA((n, N_CHIPS))],
    )(*parts)


def _pair_add(core, g_in, r_in, g_out, r_out, g_cw, r_cw):
    pairs = ((g_in, r_in, BF16), (g_out, r_out, BF16), (g_cw, r_cw, F32))

    def body(core_ref, *refs):
        for a in range(len(pairs)):
            g_ref, r_ref, o_ref = refs[2 * a], refs[2 * a + 1], refs[2 * len(pairs) + a]
            o_ref[0] = (g_ref[0, 0] + r_ref[0]).astype(o_ref.dtype)

    in_specs, out_specs, out_shape, args = [], [], [], []
    for g, r, dt in pairs:
        blk = g.shape[1:]
        zeros = (0,) * len(blk)
        in_specs.append(pl.BlockSpec((1, 1) + blk, lambda q, core_ref, zeros=zeros: (q, core_ref[0]) + zeros))
        in_specs.append(pl.BlockSpec((1,) + blk, lambda q, core_ref, zeros=zeros: (q,) + zeros))
        out_specs.append(pl.BlockSpec((1,) + blk, lambda q, core_ref, zeros=zeros: (q,) + zeros))
        out_shape.append(jax.ShapeDtypeStruct((N_CHIPS,) + blk, dt))
        args += [g.reshape((N_CHIPS, 2) + blk), r]
    return pl.pallas_call(
        body, name="pair_add", out_shape=tuple(out_shape),
        grid_spec=pltpu.PrefetchScalarGridSpec(num_scalar_prefetch=1, grid=(N_CHIPS,), in_specs=in_specs,
                                               out_specs=tuple(out_specs)),
        compiler_params=_params(("arbitrary",)),
    )(core, *args)


def _chip_exchange(s_in, s_out, s_cw, pack):
    parts = (s_in, s_out, s_cw)
    n = len(parts)

    def body(*refs):
        src, pack_ref = refs[:n], refs[n]
        dst, packs_ref = refs[n + 1:2 * n + 1], refs[2 * n + 1]
        send_sems, recv_sems, local_sems, pk_send, pk_recv, pk_local = refs[2 * n + 2:]
        x, y, c = _mesh_pos()
        mine = 2 * x + y
        local = [pltpu.make_async_copy(src[a].at[mine], dst[a].at[mine], local_sems.at[a]) for a in range(n)]
        copies = [_remote(src[a].at[2 * px + py], dst[a].at[mine], send_sems.at[a, j], recv_sems.at[a, j], (px, py, c))
                  for j, (px, py) in enumerate(_other_chips()) for a in range(n)]
        for cp in local + copies:
            cp.start()
        _two_level_gather((pack_ref,), (packs_ref,), pk_send, pk_recv, pk_local)
        for cp in copies:
            cp.wait()
        for cp in local:
            cp.wait()

    hbm = pl.BlockSpec(memory_space=pltpu.HBM)
    out_shape = tuple(jax.ShapeDtypeStruct(s.shape, s.dtype) for s in parts)
    out_shape += (jax.ShapeDtypeStruct((N_DEV,) + pack.shape, pack.dtype),)
    return pl.pallas_call(
        body, name="chip_exchange", out_shape=out_shape,
        in_specs=[hbm] * (n + 1), out_specs=tuple([hbm] * (n + 1)),
        scratch_shapes=[pltpu.SemaphoreType.DMA((n, N_CHIPS - 1)), pltpu.SemaphoreType.DMA((n, N_CHIPS - 1)),
                        pltpu.SemaphoreType.DMA((n,)), pltpu.SemaphoreType.DMA((1, GATHER_COPIES)),
                        pltpu.SemaphoreType.DMA((1, GATHER_COPIES)), pltpu.SemaphoreType.DMA((1,))],
    )(s_in, s_out, s_cw, pack)


def _fwd_proj(x, norm_g, scale, shift, w_main, w_g):
    seq = x.shape[0]
    tm = min(512, seq)
    tn = 512

    def body(x_ref, ng_ref, sc_ref, sh_ref, w_ref, wg_ref, proj_ref, gates_ref, h_ref, h_scr):
        @pl.when(pl.program_id(1) == 0)
        def _():
            xt = x_ref[...]
            r = lax.rsqrt(jnp.mean(xt * xt, axis=-1, keepdims=True) + EPS)
            h = ((xt * r) * ng_ref[...]) * (1.0 + sc_ref[...]) + sh_ref[...]
            hb = h.astype(BF16)
            h_scr[...] = hb
            h_ref[...] = hb
            gates_ref[...] = _dot(hb, wg_ref[...])

        proj_ref[...] = _dot(h_scr[...], w_ref[...])

    vec = pl.BlockSpec((1, D_MODEL), lambda i, j: (0, 0))
    return pl.pallas_call(
        body, name="fwd_proj", grid=(seq // tm, N_MAIN // tn),
        out_shape=(jax.ShapeDtypeStruct((seq, N_MAIN), F32), jax.ShapeDtypeStruct((seq, 128), F32),
                   jax.ShapeDtypeStruct((seq, D_MODEL), BF16)),
        in_specs=[pl.BlockSpec((tm, D_MODEL), lambda i, j: (i, 0)), vec, vec, vec,
                  pl.BlockSpec((D_MODEL, tn), lambda i, j: (0, j)),
                  pl.BlockSpec((D_MODEL, 128), lambda i, j: (0, 0))],
        out_specs=(pl.BlockSpec((tm, tn), lambda i, j: (i, j)), pl.BlockSpec((tm, 128), lambda i, j: (i, 0)),
                   pl.BlockSpec((tm, D_MODEL), lambda i, j: (i, 0))),
        scratch_shapes=[pltpu.VMEM((tm, D_MODEL), BF16)],
        compiler_params=_params(("arbitrary", "arbitrary")),
    )(x, norm_g, scale, shift, w_main, w_g)


def _gate_forms(gpre):
    r = lax.broadcasted_iota(jnp.int32, (CHUNK, CHUNK), 0)
    c = lax.broadcasted_iota(jnp.int32, (CHUNK, CHUNK), 1)
    causal = c <= r
    ltri = jnp.where(causal, 1.0, 0.0).astype(F32)
    utri = jnp.where(r <= c, 1.0, 0.0).astype(F32)
    bcol = _dot_f32(ltri, _log_sigmoid(gpre))
    gt8 = gpre.T[0:8, :]
    brow = _dot_f32(_log_sigmoid(gt8), utri)
    return causal, utri, bcol, gt8, brow


def _head_fwd(qh, kh, vh, bc, br, igr, m_prev, c_h, n_row, causal):
    dlog = jnp.where(causal, bc - br + igr, NEG_BIG)
    inter_log = bc + m_prev
    m_t = jnp.maximum(inter_log, jnp.max(dlog, axis=-1, keepdims=True))
    dmat = jnp.exp(dlog - m_t)
    inter = jnp.exp(inter_log - m_t)
    qb, kb, vb, cb = qh.astype(BF16), kh.astype(BF16), vh.astype(BF16), c_h.astype(BF16)
    s = _dot_nt(qb, kb) * dmat
    cq = _dot_nt(qb, cb)
    num = _dot(s.astype(BF16), vb) + inter * cq
    nq = jnp.sum(qh * n_row, axis=-1, keepdims=True)
    den = jnp.sum(s, axis=-1, keepdims=True) + inter * nq
    emt = jnp.exp(-m_t)
    dn = jnp.maximum(jnp.abs(den), emt)
    hm = num / dn
    return dict(dmat=dmat, inter=inter, qb=qb, kb=kb, vb=vb, cb=cb, s=s, cq=cq, nq=nq, den=den, emt=emt,
                dn=dn, hm=hm)


def _state_weights(bc, igc, m_prev, m_new=None):
    last = lax.broadcasted_iota(jnp.int32, (CHUNK, 1), 0) == CHUNK - 1
    b_last = jnp.sum(jnp.where(last, bc, 0.0), axis=0, keepdims=True)
    wlog = b_last - bc + igc
    if m_new is None:
        m_new = jnp.maximum(b_last + m_prev, jnp.max(wlog, axis=0, keepdims=True))
    w_c = jnp.exp(wlog - m_new)
    decay = jnp.exp(b_last + m_prev - m_new)
    return w_c, decay, m_new, last


def _conv_pre(xpad_ref, cw_ref, cb_ref):
    a = cb_ref[...]
    for j in range(CONV_WIDTH):
        lo = CONV_HALO - (CONV_WIDTH - 1) + j
        a = a + cw_ref[j:j + 1, :] * xpad_ref[lo:lo + CHUNK, :]
    return a


def _pool_window_sum(upad_ref, g, w):
    lanes = slice(g * POOL_GROUP_DIM, (g + 1) * POOL_GROUP_DIM)
    acc = upad_ref[POOL_HALO:POOL_HALO + CHUNK, lanes]
    for j in range(1, w):
        acc = acc + upad_ref[POOL_HALO - j:POOL_HALO - j + CHUNK, lanes]
    return acc


def _pool_inv_count(chunk_idx, w):
    pos = chunk_idx * CHUNK + lax.broadcasted_iota(jnp.int32, (CHUNK, 1), 0) + 1
    return 1.0 / jnp.minimum(pos, w).astype(F32)


def _mixer_in_specs(cmap, n_chunks):
    def rows(i):
        return cmap(i)
    return [
        pl.BlockSpec((CHUNK, 1024), lambda i: (rows(i), 0)),
        pl.BlockSpec((CHUNK, 1024), lambda i: (rows(i), 1)),
        pl.BlockSpec((CHUNK, 512), lambda i: (rows(i), 4)),
        pl.BlockSpec((CHUNK, 512), lambda i: (rows(i), 5)),
        pl.BlockSpec((CHUNK, 512), lambda i: (rows(i), 6)),
        pl.BlockSpec((POOL_HALO, 512), lambda i: (jnp.maximum(rows(i) * (CHUNK // POOL_HALO) - 1, 0), 0)),
        pl.BlockSpec((CONV_HALO, 1024), lambda i: (jnp.maximum(rows(i) * (CHUNK // CONV_HALO) - 1, 0), 1)),
    ]


def _mix_fwd(proj, gates, bg_pad, conv_w8, conv_b, w_pool, ls_pool, mh_g):
    seq = proj.shape[0]
    n_chunks = seq // CHUNK

    def body(uz_ref, qk_ref, v_ref, o_ref, zm_ref, uh_ref, qkh_ref, g_ref, bg_ref, cw_ref, cb_ref, wp_ref,
             ls_ref, mhg_ref, mix_ref, cst_ref, nst_ref, mst_ref, c_scr, n_scr, m_scr, xpad, upad):
        i = pl.program_id(0)

        @pl.when(i == 0)
        def _():
            c_scr[...] = jnp.zeros_like(c_scr)
            n_scr[...] = jnp.zeros_like(n_scr)
            m_scr[...] = jnp.zeros_like(m_scr)

        cst_ref[0] = c_scr[...]
        nst_ref[0] = n_scr[...]
        mst_ref[0] = m_scr[...]
        first = i == 0

        upad[0:POOL_HALO, :] = jnp.where(first, 0.0, uh_ref[...])
        upad[POOL_HALO:POOL_HALO + CHUNK, :] = uz_ref[:, 0:D_POOL]
        for g, w in enumerate(POOL_WINDOWS):
            lanes = slice(g * POOL_GROUP_DIM, (g + 1) * POOL_GROUP_DIM)
            pooled = _pool_window_sum(upad, g, w) * _pool_inv_count(i, w) - uz_ref[:, lanes]
            y = _dot(pooled.astype(BF16), wp_ref[g].astype(BF16)) * ls_ref[:, lanes]
            zp = uz_ref[:, D_POOL + g * POOL_GROUP_DIM:D_POOL + (g + 1) * POOL_GROUP_DIM]
            mix_ref[:, lanes] = (y * (zp * _sigmoid(zp))).astype(BF16)

        xpad[0:CONV_HALO, :] = jnp.where(first, 0.0, qkh_ref[...])
        xpad[CONV_HALO:CONV_HALO + CHUNK, :] = qk_ref[...]
        a = _conv_pre(xpad, cw_ref, cb_ref)
        qk = a * _sigmoid(a)

        gpre = g_ref[...] + bg_ref[...]
        causal, _, bcol, gt8, brow = _gate_forms(gpre)
        for h in range(N_HEADS):
            lanes = slice(h * HEAD_DIM, (h + 1) * HEAD_DIM)
            qh = qk[:, lanes]
            kh = qk[:, D_MLSTM + h * HEAD_DIM:D_MLSTM + (h + 1) * HEAD_DIM] * (HEAD_DIM ** -0.5)
            vh = v_ref[:, lanes]
            bc = bcol[:, N_HEADS + h:N_HEADS + h + 1]
            br = brow[N_HEADS + h:N_HEADS + h + 1, :]
            igr = gt8[h:h + 1, :]
            igc = gpre[:, h:h + 1]
            m_prev = m_scr[h:h + 1, 0:1]
            c_h = c_scr[h]
            n_row = n_scr[h:h + 1, :]
            f = _head_fwd(qh, kh, vh, bc, br, igr, m_prev, c_h, n_row, causal)
            w_c, decay, m_new, _ = _state_weights(bc, igc, m_prev)
            c_scr[h] = decay * c_h + _dot_tn((vh * w_c).astype(BF16), f["kb"])
            n_scr[h:h + 1, :] = decay * n_row + jnp.sum(w_c * kh, axis=0, keepdims=True)
            m_scr[h:h + 1, :] = jnp.broadcast_to(m_new, (1, 128))
            hm = f["hm"]
            hn = hm * lax.rsqrt(jnp.mean(hm * hm, axis=-1, keepdims=True) + EPS) * mhg_ref[:, lanes]
            zm = zm_ref[:, lanes]
            out = hn * _sigmoid(o_ref[:, lanes]) * (zm * _sigmoid(zm))
            mix_ref[:, D_POOL + h * HEAD_DIM:D_POOL + (h + 1) * HEAD_DIM] = out.astype(BF16)

    cmap = lambda i: i
    in_specs = _mixer_in_specs(cmap, n_chunks) + [
        pl.BlockSpec((CHUNK, 128), lambda i: (i, 0)),
        _full((1, 128)), _full((8, 1024)), _full((1, 1024)), _full((4, 128, 128)), _full((1, 512)),
        _full((1, 512))]
    return pl.pallas_call(
        body, name="mix_fwd", grid=(n_chunks,),
        out_shape=(jax.ShapeDtypeStruct((seq, D_MODEL), BF16),
                   jax.ShapeDtypeStruct((n_chunks, N_HEADS, HEAD_DIM, HEAD_DIM), F32),
                   jax.ShapeDtypeStruct((n_chunks, 8, 128), F32),
                   jax.ShapeDtypeStruct((n_chunks, 8, 128), F32)),
        in_specs=in_specs,
        out_specs=(pl.BlockSpec((CHUNK, D_MODEL), lambda i: (i, 0)),
                   pl.BlockSpec((1, N_HEADS, HEAD_DIM, HEAD_DIM), lambda i: (i, 0, 0, 0)),
                   pl.BlockSpec((1, 8, 128), lambda i: (i, 0, 0)),
                   pl.BlockSpec((1, 8, 128), lambda i: (i, 0, 0))),
        scratch_shapes=[pltpu.VMEM((N_HEADS, HEAD_DIM, HEAD_DIM), F32), pltpu.VMEM((8, 128), F32),
                        pltpu.VMEM((8, 128), F32), pltpu.VMEM((CONV_HALO + CHUNK, 1024), F32),
                        pltpu.VMEM((POOL_HALO + CHUNK, D_POOL), F32)],
        compiler_params=_params(("arbitrary",)),
    )(proj, proj, proj, proj, proj, proj, proj, gates, bg_pad, conv_w8, conv_b, w_pool, ls_pool, mh_g)


def _out_fwd_bwd(mix, x, tgt, w_out_b, gate, final_g):
    seq = x.shape[0]
    tm = min(256, seq)

    def body(mix_ref, x_ref, t_ref, w_ref, gate_ref, fg_ref, dx2_ref, dmix_ref, dwo_ref, dgate_ref, dfg_ref,
             loss_ref):
        @pl.when(pl.program_id(0) == 0)
        def _():
            dwo_ref[...] = jnp.zeros_like(dwo_ref)
            dgate_ref[...] = jnp.zeros_like(dgate_ref)
            dfg_ref[...] = jnp.zeros_like(dfg_ref)
            loss_ref[...] = jnp.zeros_like(loss_ref)

        mixb = mix_ref[...]
        w = w_ref[...]
        gate_v = gate_ref[...]
        fg = fg_ref[...]
        o2 = _dot(mixb, w)
        x2 = x_ref[...] + gate_v * o2
        r2 = lax.rsqrt(jnp.mean(x2 * x2, axis=-1, keepdims=True) + EPS)
        x2n = x2 * r2
        err = x2n * fg - t_ref[...]
        part = 0.5 * jnp.sum(jnp.sum(err * err, axis=-1, keepdims=True), axis=0, keepdims=True) / D_MODEL
        loss_ref[...] += jnp.broadcast_to(part, loss_ref.shape)
        dy = err / D_MODEL
        dfg_ref[...] += jnp.sum(dy * x2n, axis=0, keepdims=True)
        gdy = dy * fg
        dx2 = r2 * (gdy - x2n * jnp.mean(gdy * x2n, axis=-1, keepdims=True))
        dx2_ref[...] = dx2
        dgate_ref[...] += jnp.sum(dx2 * o2, axis=0, keepdims=True)
        do2 = (dx2 * gate_v).astype(BF16)
        dmix_ref[...] = _dot_nt(do2, w)
        dwo_ref[...] += _dot_tn(mixb, do2)

    tile = pl.BlockSpec((tm, D_MODEL), lambda i: (i, 0))
    vec = _full((1, D_MODEL))
    return pl.pallas_call(
        body, name="out_fwd_bwd", grid=(seq // tm,),
        out_shape=(jax.ShapeDtypeStruct((seq, D_MODEL), F32), jax.ShapeDtypeStruct((seq, D_MODEL), F32),
                   jax.ShapeDtypeStruct((D_MODEL, D_MODEL), F32), jax.ShapeDtypeStruct((1, D_MODEL), F32),
                   jax.ShapeDtypeStruct((1, D_MODEL), F32), jax.ShapeDtypeStruct((1, 128), F32)),
        in_specs=[tile, tile, tile, _full((D_MODEL, D_MODEL)), vec, vec],
        out_specs=(tile, tile, _full((D_MODEL, D_MODEL)), vec, vec, _full((1, 128))),
        compiler_params=_params(("arbitrary",)),
    )(mix, x, tgt, w_out_b, gate, final_g)


def _mix_bwd(proj, gates, dmix, cst, nst, mst, bg_pad, conv_w8, conv_b, w_pool, ls_pool, mh_g):
    seq = proj.shape[0]
    n_chunks = seq // CHUNK

    def body(uz_ref, qk_ref, v_ref, o_ref, zm_ref, uh_ref, qkh_ref, g_ref, dmix_ref, cst_ref, nst_ref, mst_ref,
             mnx_ref, bg_ref, cw_ref, cb_ref, wp_ref, ls_ref, mhg_ref,
             dp_ref, dcw_ref, dcb_ref, dwp_ref, dls_ref, dmhg_ref, dbg_ref,
             dc_scr, dn_scr, xpad, upad, dapad, dpipad):
        i = pl.program_id(0)
        ci = n_chunks - 1 - i

        @pl.when(i == 0)
        def _():
            for ref in (dc_scr, dn_scr, dcw_ref, dcb_ref, dwp_ref, dls_ref, dmhg_ref, dbg_ref):
                ref[...] = jnp.zeros_like(ref)
            dapad[CHUNK:CHUNK + CONV_HALO, :] = jnp.zeros((CONV_HALO, 1024), F32)
            dpipad[CHUNK:CHUNK + POOL_HALO, :] = jnp.zeros((POOL_HALO, D_POOL), F32)

        first = ci == 0

        upad[0:POOL_HALO, :] = jnp.where(first, 0.0, uh_ref[...])
        upad[POOL_HALO:POOL_HALO + CHUNK, :] = uz_ref[:, 0:D_POOL]
        dpooled = []
        for g, w in enumerate(POOL_WINDOWS):
            lanes = slice(g * POOL_GROUP_DIM, (g + 1) * POOL_GROUP_DIM)
            zlanes = slice(D_POOL + g * POOL_GROUP_DIM, D_POOL + (g + 1) * POOL_GROUP_DIM)
            inv = _pool_inv_count(ci, w)
            pooled = _pool_window_sum(upad, g, w) * inv - uz_ref[:, lanes]
            pb = pooled.astype(BF16)
            wpb = wp_ref[g].astype(BF16)
            yw = _dot(pb, wpb)
            ls = ls_ref[:, lanes]
            zp = uz_ref[:, zlanes]
            sg = _sigmoid(zp)
            dpo = dmix_ref[:, lanes]
            dp_ref[:, zlanes] = (dpo * (yw * ls) * (sg * (1.0 + zp * (1.0 - sg)))).astype(BF16)
            dy = dpo * (zp * sg)
            dls_ref[:, lanes] += jnp.sum(dy * yw, axis=0, keepdims=True)
            dyw = (dy * ls).astype(BF16)
            dwp_ref[g] += _dot_tn(pb, dyw)
            dpl = _dot_nt(dyw, wpb)
            dpooled.append(dpl)
            dpipad[0:CHUNK, lanes] = dpl * inv
        for g, w in enumerate(POOL_WINDOWS):
            lanes = slice(g * POOL_GROUP_DIM, (g + 1) * POOL_GROUP_DIM)
            du = -dpooled[g]
            for j in range(w):
                du = du + dpipad[j:j + CHUNK, lanes]
            dp_ref[:, lanes] = du.astype(BF16)
        dpipad[CHUNK:CHUNK + POOL_HALO, :] = dpipad[0:POOL_HALO, :]

        xpad[0:CONV_HALO, :] = jnp.where(first, 0.0, qkh_ref[...])
        xpad[CONV_HALO:CONV_HALO + CHUNK, :] = qk_ref[...]
        a = _conv_pre(xpad, cw_ref, cb_ref)
        sga = _sigmoid(a)
        qk = a * sga
        dsilu_a = sga * (1.0 + a * (1.0 - sga))

        gpre = g_ref[...] + bg_ref[...]
        causal, utri, bcol, gt8, brow = _gate_forms(gpre)
        lane = lax.broadcasted_iota(jnp.int32, (CHUNK, 128), 1)
        row = lax.broadcasted_iota(jnp.int32, (CHUNK, 128), 0)
        colsum_rows = jnp.zeros((CHUNK, 128), F32)
        dig_cols = jnp.zeros((CHUNK, 128), F32)
        db_cols = jnp.zeros((CHUNK, 128), F32)
        scale_k = HEAD_DIM ** -0.5

        for h in range(N_HEADS):
            lanes = slice(h * HEAD_DIM, (h + 1) * HEAD_DIM)
            klanes = slice(D_MLSTM + h * HEAD_DIM, D_MLSTM + (h + 1) * HEAD_DIM)
            qh = qk[:, lanes]
            kh = qk[:, klanes] * scale_k
            vh = v_ref[:, lanes]
            bc = bcol[:, N_HEADS + h:N_HEADS + h + 1]
            br = brow[N_HEADS + h:N_HEADS + h + 1, :]
            igr = gt8[h:h + 1, :]
            igc = gpre[:, h:h + 1]
            m_prev = mst_ref[0, h:h + 1, 0:1]
            m_next = mnx_ref[0, h:h + 1, 0:1]
            c_h = cst_ref[0, h]
            n_row = nst_ref[0, h:h + 1, :]
            f = _head_fwd(qh, kh, vh, bc, br, igr, m_prev, c_h, n_row, causal)
            qb, kb, vb, cb = f["qb"], f["kb"], f["vb"], f["cb"]
            s, dmat, inter, den, dn, hm = f["s"], f["dmat"], f["inter"], f["den"], f["dn"], f["hm"]

            rinv = lax.rsqrt(jnp.mean(hm * hm, axis=-1, keepdims=True) + EPS)
            hmn = hm * rinv
            gh = mhg_ref[:, lanes]
            o_pre = o_ref[:, lanes]
            og = _sigmoid(o_pre)
            zm = zm_ref[:, lanes]
            sgz = _sigmoid(zm)
            sz = zm * sgz
            dout = dmix_ref[:, D_POOL + h * HEAD_DIM:D_POOL + (h + 1) * HEAD_DIM]
            hn = hmn * gh
            dp_ref[:, 2560 + h * HEAD_DIM:2560 + (h + 1) * HEAD_DIM] = (
                dout * hn * sz * og * (1.0 - og)).astype(BF16)
            dp_ref[:, 3072 + h * HEAD_DIM:3072 + (h + 1) * HEAD_DIM] = (
                dout * hn * og * (sgz * (1.0 + zm * (1.0 - sgz)))).astype(BF16)
            dhn = dout * og * sz
            dmhg_ref[:, lanes] += jnp.sum(dhn * hmn, axis=0, keepdims=True)
            dyn = dhn * gh
            dhm = rinv * (dyn - hmn * jnp.mean(dyn * hmn, axis=-1, keepdims=True))

            inv_dn = 1.0 / dn
            dnum = dhm * inv_dn
            hd = jnp.sum(dhm * hm, axis=-1, keepdims=True)
            dden = jnp.where(jnp.abs(den) > f["emt"], -hd * inv_dn * jnp.sign(den), 0.0)
            dnb = dnum.astype(BF16)
            ds = _dot_nt(dnb, vb) + dden
            dv = _dot_tn(s.astype(BF16), dnb)
            dqk = (ds * dmat).astype(BF16)
            dq = _dot(dqk, kb) + inter * (_dot(dnb, cb) + dden * n_row)
            dk = _dot_tn(dqk, qb)
            gmat = ds * s
            row_g = jnp.sum(gmat, axis=-1, keepdims=True)
            col_g = jnp.sum(gmat, axis=0, keepdims=True)
            gcol = inter * (jnp.sum(dnum * f["cq"], axis=-1, keepdims=True) + dden * f["nq"])
            dc_prev = _dot_tn((inter * dnum).astype(BF16), qb)
            dn_prev = jnp.sum((inter * dden) * qh, axis=0, keepdims=True)

            w_c, decay, _, last = _state_weights(bc, igc, m_prev, m_next)
            dcn = dc_scr[h]
            dnn = dn_scr[h:h + 1, :]
            ddecay = (jnp.sum(jnp.sum(dcn * c_h, axis=-1, keepdims=True), axis=0, keepdims=True)
                      + jnp.sum(dnn * n_row, axis=-1, keepdims=True))
            dcnb = dcn.astype(BF16)
            amat = _dot(vb, dcnb) + dnn
            dw = jnp.sum(amat * kh, axis=-1, keepdims=True)
            dv = dv + w_c * _dot_nt(kb, dcnb)
            dk = dk + w_c * amat
            e = dw * w_c
            db_last = ddecay * decay + jnp.sum(e, axis=0, keepdims=True)
            db_c = row_g + gcol - e + jnp.where(last, db_last, 0.0)
            dc_scr[h] = decay * dcn + dc_prev
            dn_scr[h:h + 1, :] = decay * dnn + dn_prev

            colsum_rows = colsum_rows + jnp.where(row == h, col_g, 0.0)
            dig_cols = dig_cols + jnp.where(lane == h, e, 0.0)
            db_cols = db_cols + jnp.where(lane == N_HEADS + h, db_c, 0.0)

            dp_ref[:, 2048 + h * HEAD_DIM:2048 + (h + 1) * HEAD_DIM] = dv.astype(BF16)
            dapad[0:CHUNK, lanes] = dq * dsilu_a[:, lanes]
            dapad[0:CHUNK, klanes] = dk * scale_k * dsilu_a[:, klanes]

        cs_t = colsum_rows.T
        dig_all = dig_cols + cs_t
        shifted = jnp.zeros((CHUNK, 128), F32)
        for h in range(N_HEADS):
            shifted = shifted + jnp.where(lane == N_HEADS + h, cs_t[:, h:h + 1], 0.0)
        dlf = _dot_f32(utri, db_cols - shifted)
        dgates = dig_all + dlf * _sigmoid(-gpre)
        dp_ref[:, N_MAIN:N_MAIN + 128] = dgates.astype(BF16)
        dp_ref[:, N_MAIN + 128:N_PAD] = jnp.zeros((CHUNK, N_PAD - N_MAIN - 128), BF16)
        dbg_ref[...] += jnp.sum(dgates, axis=0, keepdims=True)

        da = dapad[0:CHUNK, :]
        dcb_ref[...] += jnp.sum(da, axis=0, keepdims=True)
        dx = jnp.zeros((CHUNK, 1024), F32)
        for j in range(CONV_WIDTH):
            lo = CONV_HALO - (CONV_WIDTH - 1) + j
            dcw_ref[j:j + 1, :] += jnp.sum(da * xpad[lo:lo + CHUNK, :], axis=0, keepdims=True)
            hi = CONV_WIDTH - 1 - j
            dx = dx + cw_ref[j:j + 1, :] * dapad[hi:hi + CHUNK, :]
        dp_ref[:, 1024:2048] = dx.astype(BF16)
        dapad[CHUNK:CHUNK + CONV_HALO, :] = dapad[0:CONV_HALO, :]

    cmap = lambda i: n_chunks - 1 - i
    in_specs = _mixer_in_specs(cmap, n_chunks) + [
        pl.BlockSpec((CHUNK, 128), lambda i: (cmap(i), 0)),
        pl.BlockSpec((CHUNK, D_MODEL), lambda i: (cmap(i), 0)),
        pl.BlockSpec((1, N_HEADS, HEAD_DIM, HEAD_DIM), lambda i: (cmap(i), 0, 0, 0)),
        pl.BlockSpec((1, 8, 128), lambda i: (cmap(i), 0, 0)),
        pl.BlockSpec((1, 8, 128), lambda i: (cmap(i), 0, 0)),
        pl.BlockSpec((1, 8, 128), lambda i: (jnp.minimum(cmap(i) + 1, n_chunks - 1), 0, 0)),
        _full((1, 128)), _full((8, 1024)), _full((1, 1024)), _full((4, 128, 128)), _full((1, 512)),
        _full((1, 512))]
    return pl.pallas_call(
        body, name="mix_bwd", grid=(n_chunks,),
        out_shape=(jax.ShapeDtypeStruct((seq, N_PAD), BF16), jax.ShapeDtypeStruct((8, 1024), F32),
                   jax.ShapeDtypeStruct((1, 1024), F32), jax.ShapeDtypeStruct((4, 128, 128), F32),
                   jax.ShapeDtypeStruct((1, 512), F32), jax.ShapeDtypeStruct((1, 512), F32),
                   jax.ShapeDtypeStruct((1, 128), F32)),
        in_specs=in_specs,
        out_specs=(pl.BlockSpec((CHUNK, N_PAD), lambda i: (cmap(i), 0)), _full((8, 1024)), _full((1, 1024)),
                   _full((4, 128, 128)), _full((1, 512)), _full((1, 512)), _full((1, 128))),
        scratch_shapes=[pltpu.VMEM((N_HEADS, HEAD_DIM, HEAD_DIM), F32), pltpu.VMEM((8, 128), F32),
                        pltpu.VMEM((CONV_HALO + CHUNK, 1024), F32), pltpu.VMEM((POOL_HALO + CHUNK, D_POOL), F32),
                        pltpu.VMEM((CHUNK + CONV_HALO, 1024), F32), pltpu.VMEM((CHUNK + POOL_HALO, D_POOL), F32)],
        compiler_params=_params(("arbitrary",)),
    )(proj, proj, proj, proj, proj, proj, proj, gates, dmix, cst, nst, mst, mst, bg_pad, conv_w8, conv_b,
      w_pool, ls_pool, mh_g)


def _bwd_in(dproj, w_in_t, x, dx2, norm_g, scale):
    seq = x.shape[0]
    tm = min(256, seq)

    def body(dp_ref, wt_ref, x_ref, dx2_ref, ng_ref, sc_ref, gx_ref, dsh_ref, dsc_ref, dng_ref):
        @pl.when(pl.program_id(0) == 0)
        def _():
            dsh_ref[...] = jnp.zeros_like(dsh_ref)
            dsc_ref[...] = jnp.zeros_like(dsc_ref)
            dng_ref[...] = jnp.zeros_like(dng_ref)

        dh = _dot(dp_ref[...], wt_ref[...])
        xt = x_ref[...]
        r = lax.rsqrt(jnp.mean(xt * xt, axis=-1, keepdims=True) + EPS)
        xn = xt * r
        ng = ng_ref[...]
        one_sc = 1.0 + sc_ref[...]
        dsh_ref[...] += jnp.sum(dh, axis=0, keepdims=True)
        dhxn = dh * xn
        dsc_ref[...] += jnp.sum(dhxn * ng, axis=0, keepdims=True)
        dng_ref[...] += jnp.sum(dhxn * one_sc, axis=0, keepdims=True)
        dxn = dh * (ng * one_sc)
        gx_ref[...] = r * (dxn - xn * jnp.mean(dxn * xn, axis=-1, keepdims=True)) + dx2_ref[...]

    tile = pl.BlockSpec((tm, D_MODEL), lambda i: (i, 0))
    vec = _full((1, D_MODEL))
    return pl.pallas_call(
        body, name="bwd_in", grid=(seq // tm,),
        out_shape=(jax.ShapeDtypeStruct((seq, D_MODEL), F32),) + (jax.ShapeDtypeStruct((1, D_MODEL), F32),) * 3,
        in_specs=[pl.BlockSpec((tm, N_PAD), lambda i: (i, 0)), _full((N_PAD, D_MODEL)), tile, tile, vec, vec],
        out_specs=(tile, vec, vec, vec),
        compiler_params=_params(("arbitrary",)),
    )(dproj, w_in_t, x, dx2, norm_g, scale)


def _dw_in(h_b, dproj):
    seq = h_b.shape[0]
    tk = min(1024, seq)
    tn = 768

    def body(h_ref, dp_ref, dw_ref):
        @pl.when(pl.program_id(1) == 0)
        def _():
            dw_ref[...] = jnp.zeros_like(dw_ref)

        dw_ref[...] += _dot_tn(h_ref[...], dp_ref[...])

    return pl.pallas_call(
        body, name="dw_in", grid=(N_PAD // tn, seq // tk),
        out_shape=jax.ShapeDtypeStruct((D_MODEL, N_PAD), F32),
        in_specs=[pl.BlockSpec((tk, D_MODEL), lambda j, t: (t, 0)), pl.BlockSpec((tk, tn), lambda j, t: (t, j))],
        out_specs=pl.BlockSpec((D_MODEL, tn), lambda j, t: (0, j)),
        compiler_params=_params(("arbitrary", "arbitrary")),
    )(h_b, dproj)


def _adam_update(g, w, m, v, g_ref, d_ref, m_ref, v_ref):
    mn = ADAM_B1 * m + (1.0 - ADAM_B1) * g
    vn = ADAM_B2 * v + (1.0 - ADAM_B2) * (g * g)
    m_hat = mn / (1.0 - ADAM_B1 ** ADAM_STEP)
    v_hat = vn / (1.0 - ADAM_B2 ** ADAM_STEP)
    g_ref[...] = g
    d_ref[...] = -ADAM_LR * (m_hat / (jnp.sqrt(v_hat) + ADAM_EPS) + ADAM_WD * w)
    m_ref[...] = mn
    v_ref[...] = vn


def _adam_sum(name, parts, w, m, v, row_tile):
    rows, cols = w.shape
    n_parts = parts.shape[0]

    def body(p_ref, w_ref, m_ref, v_ref, g_out, d_out, m_out, v_out):
        g = p_ref[0].astype(F32)
        for j in range(1, n_parts):
            g = g + p_ref[j].astype(F32)
        _adam_update(g, w_ref[...], m_ref[...], v_ref[...], g_out, d_out, m_out, v_out)

    tile = pl.BlockSpec((row_tile, cols), lambda i: (i, 0))
    return pl.pallas_call(
        body, name=name, grid=(rows // row_tile,),
        out_shape=(jax.ShapeDtypeStruct((rows, cols), F32),) * 4,
        in_specs=[pl.BlockSpec((n_parts, row_tile, cols), lambda i: (0, i, 0)), tile, tile, tile],
        out_specs=(tile,) * 4,
        compiler_params=_params(("arbitrary",)),
    )(parts, w, m, v)


def _adam_ada(sc_all16, dmod_blk16, w, m, v):
    rows, cols = w.shape

    def body(sc_ref, dm_ref, w_ref, m_ref, v_ref, g_out, d_out, m_out, v_out):
        g = _dot_tn(sc_ref[...].astype(BF16), dm_ref[...].astype(BF16))
        _adam_update(g, w_ref[...], m_ref[...], v_ref[...], g_out, d_out, m_out, v_out)

    return pl.pallas_call(
        body, name="adam_w_ada", grid=(1,),
        out_shape=(jax.ShapeDtypeStruct((rows, cols), F32),) * 4,
        in_specs=[_full(sc_all16.shape), _full(dmod_blk16.shape)] + [_full((rows, cols))] * 3,
        out_specs=(_full((rows, cols)),) * 4,
        compiler_params=_params(("arbitrary",)),
    )(sc_all16, dmod_blk16, w, m, v)


def _pack_small(pieces):
    rows = []
    for name, n in PACK_ROWS:
        a = pieces[name].reshape(-1).astype(F32)
        a = jnp.pad(a, (0, n * 128 - a.shape[0]))
        rows.append(a.reshape(n, 128))
    return jnp.concatenate(rows, axis=0)


def _unpack_small(pack, name, shape):
    off, _ = PACK_OFF[name]
    size = 1
    for s in shape:
        size *= s
    n_rows = -(-size // 128)
    return pack[off:off + n_rows].reshape(-1)[:size].reshape(shape)


def _local_step(x2, tgt2, shift, scale, gate, norm_g, w_full_b, w_out_b, conv_w, conv_b, w_pool, ls_pool,
                mh_norm_g, b_gates, final_g):
    w_main = w_full_b[:, :N_MAIN]
    w_g = jnp.pad(w_full_b[:, N_MAIN:], ((0, 0), (0, 128 - (N_IN - N_MAIN))))
    w_in_t = jnp.pad(w_full_b.T, ((0, N_PAD - N_IN), (0, 0)))
    bg_pad = jnp.pad(b_gates, ((0, 0), (0, 128 - b_gates.shape[1])))
    conv_w8 = jnp.pad(conv_w, ((0, 8 - CONV_WIDTH), (0, 0)))
    fg = final_g.reshape(1, D_MODEL)

    proj, gates, h_b = _fwd_proj(x2, norm_g, scale, shift, w_main, w_g)
    mix, cst, nst, mst = _mix_fwd(proj, gates, bg_pad, conv_w8, conv_b, w_pool, ls_pool, mh_norm_g)
    dx2, dmix, dwo, dgate, dfg, loss = _out_fwd_bwd(mix, x2, tgt2, w_out_b, gate, fg)
    dproj, dcw8, dcb, dwp, dls, dmhg, dbg = _mix_bwd(proj, gates, dmix, cst, nst, mst, bg_pad, conv_w8, conv_b,
                                                      w_pool, ls_pool, mh_norm_g)
    gx, dsh, dsc, dng = _bwd_in(dproj, w_in_t, x2, dx2, norm_g, scale)
    dw_in = _dw_in(h_b, dproj)
    return dict(loss=loss, grad_x=gx, dw_in=dw_in[:, :N_IN], dw_out=dwo, dconv_w=dcw8[:CONV_WIDTH], conv_b=dcb,
                w_pool=dwp, ls_pool=dls, mh_norm_g=dmhg, b_gates=dbg, final_g=dfg, norm_g=dng,
                dmod=jnp.concatenate([dsh, dsc, dgate], axis=1))


def kernel(x, c, norm_g, w_ada, b_ada, w_in, b_gates, conv_w, conv_b, w_pool, ls_pool, mh_norm_g, w_out, final_g, loss_target, m_norm_g, m_w_ada, m_b_ada, m_w_in, m_b_gates, m_conv_w, m_conv_b, m_w_pool, m_ls_pool, m_mh_norm_g, m_w_out, m_final_g, v_norm_g, v_w_ada, v_b_ada, v_w_in, v_b_gates, v_conv_w, v_conv_b, v_w_pool, v_ls_pool, v_mh_norm_g, v_w_out, v_final_g):
    seq = x.shape[1]
    me = 4 * lax.axis_index("x") + 2 * lax.axis_index("y") + lax.axis_index("c")

    g_in, g_out, g_cw, g_c = _gather_weights(w_in[0].astype(BF16), w_out[0].astype(BF16), conv_w[0], c)
    w_full_b = jnp.transpose(g_in, (1, 0, 2)).reshape(D_MODEL, N_IN)
    w_out_b = g_out.reshape(D_MODEL, D_MODEL)
    conv_w_full = jnp.transpose(g_cw, (1, 0, 2)).reshape(CONV_WIDTH, 2 * D_MLSTM)
    c_all16 = jnp.pad(g_c.reshape(N_DEV, D_MODEL), ((0, 8), (0, 0)))

    b_ada_blk = lax.dynamic_slice(b_ada, (0, me * ADA_SHARD), (1, ADA_SHARD))
    mod_all, sc_all16 = _ada_mod(c_all16, w_ada[0], b_ada_blk)
    mod = lax.dynamic_index_in_dim(mod_all, me, axis=1, keepdims=False).reshape(1, 3 * D_MODEL)
    shift, scale, gate = mod[:, :D_MODEL], mod[:, D_MODEL:2 * D_MODEL], mod[:, 2 * D_MODEL:]

    r = _local_step(x[0], loss_target[0], shift, scale, gate, norm_g, w_full_b, w_out_b, conv_w_full, conv_b,
                    w_pool[0], ls_pool, mh_norm_g, b_gates, final_g)

    dw_in_blocks = jnp.transpose(r["dw_in"].reshape(D_MODEL, N_DEV, N_SHARD), (1, 0, 2))
    dw_out_blocks = r["dw_out"].reshape(N_DEV, D_MODEL // N_DEV, D_MODEL)
    dcw_blocks = jnp.transpose(r["dconv_w"].reshape(CONV_WIDTH, N_DEV, 128), (1, 0, 2))
    pack = _pack_small(dict(loss=r["loss"][:, :1], final_g=r["final_g"], norm_g=r["norm_g"], conv_b=r["conv_b"],
                            ls_pool=r["ls_pool"], mh_norm_g=r["mh_norm_g"], b_gates=r["b_gates"][:, :8],
                            b_ada=r["dmod"], w_pool=r["w_pool"]))
    core = lax.axis_index("c").astype(jnp.int32).reshape(1)
    r_in, r_out, r_cw = _pair_swap(dw_in_blocks, dw_out_blocks, dcw_blocks)
    s_in, s_out, s_cw = _pair_add(core, dw_in_blocks, r_in, dw_out_blocks, r_out, dcw_blocks, r_cw)
    p_in, p_out, p_cw, p_pack = _chip_exchange(s_in, s_out, s_cw, pack)

    gi, di, mi, vi = _adam_sum("adam_w_in", p_in, w_in[0], m_w_in[0], v_w_in[0], 256)
    go, do_, mo, vo = _adam_sum("adam_w_out", p_out, w_out[0], m_w_out[0], v_w_out[0], 128)
    gc, dc, mc, vc = _adam_sum("adam_conv_w", p_cw, conv_w[0], m_conv_w[0], v_conv_w[0], CONV_WIDTH)

    def small(loss_like, fg_, ng_, cb_, ls_, mh_, bg_, ba_, wp_):
        return _pack_small(dict(loss=loss_like, final_g=fg_, norm_g=ng_, conv_b=cb_, ls_pool=ls_, mh_norm_g=mh_,
                                b_gates=bg_, b_ada=ba_, w_pool=wp_))

    zero = jnp.zeros((1, 1), F32)
    w_pack = small(zero, final_g, norm_g, conv_b, ls_pool, mh_norm_g, b_gates, b_ada, w_pool)
    m_pack = small(zero, m_final_g, m_norm_g, m_conv_b, m_ls_pool, m_mh_norm_g, m_b_gates, m_b_ada, m_w_pool)
    v_pack = small(zero, v_final_g, v_norm_g, v_conv_b, v_ls_pool, v_mh_norm_g, v_b_gates, v_b_ada, v_w_pool)
    gp, dp, mp, vp = _adam_sum("adam_small", p_pack, w_pack, m_pack, v_pack, PACK_TOTAL)

    off, rows = PACK_OFF["b_ada"]
    dmod_all = p_pack[:, off:off + rows, :].reshape(N_DEV, 3 * D_MODEL)
    dmod_blk16 = jnp.pad(lax.dynamic_slice(dmod_all, (0, me * ADA_SHARD), (N_DEV, ADA_SHARD)), ((0, 8), (0, 0)))
    ga, da, ma, va = _adam_ada(sc_all16, dmod_blk16, w_ada[0], m_w_ada[0], v_w_ada[0])

    names = ("norm_g", "w_ada", "b_ada", "w_in", "b_gates", "conv_w", "conv_b", "w_pool", "ls_pool", "mh_norm_g",
             "w_out", "final_g")
    shapes = dict(norm_g=norm_g.shape, b_ada=b_ada.shape, b_gates=b_gates.shape, conv_b=conv_b.shape,
                  w_pool=w_pool.shape, ls_pool=ls_pool.shape, mh_norm_g=mh_norm_g.shape, final_g=final_g.shape)
    sharded = dict(w_ada=(ga, da, ma, va), w_in=(gi, di, mi, vi), conv_w=(gc, dc, mc, vc), w_out=(go, do_, mo, vo))
    outs = []
    for kind in range(4):
        for nm in names:
            if nm in sharded:
                outs.append(sharded[nm][kind][None])
            else:
                outs.append(_unpack_small((gp, dp, mp, vp)[kind], nm, shapes[nm]))
    loss = gp[0, 0]
    grad_x = r["grad_x"].reshape(1, seq, D_MODEL)
    return (loss, grad_x, *outs)
```

```python
import jax
import jax.numpy as jnp
from jax import lax
from jax.experimental import pallas as pl
from jax.experimental.pallas import tpu as pltpu

F32 = jnp.float32
BF16 = jnp.bfloat16

D_MODEL = 1024
D_POOL = 512
D_MLSTM = 512
N_HEADS = 4
HEAD_DIM = 128
CHUNK = 128
POOL_WINDOWS = (2, 4, 8, 16)
POOL_GROUP_DIM = 128
CONV_WIDTH = 4
EPS = 1e-6
N_MAIN = 3584
N_IN = 3592
N_PAD = 3840
N_SHARD = N_IN // 8
ADA_SHARD = 3 * D_MODEL // 8
N_DEV = 8
CONV_HALO = 8
POOL_HALO = 16
NEG_BIG = -1e30
VMEM_LIMIT_BYTES = 56 * 1024 * 1024

ADAM_LR = 0.001
ADAM_B1 = 0.9
ADAM_B2 = 0.999
ADAM_EPS = 1e-08
ADAM_WD = 0.01
ADAM_STEP = 10

PACK_ROWS = (("loss", 8), ("final_g", 8), ("norm_g", 8), ("conv_b", 8), ("ls_pool", 8),
             ("mh_norm_g", 8), ("b_gates", 8), ("b_ada", 24), ("w_pool", 512))
PACK_TOTAL = sum(r for _, r in PACK_ROWS)


def _pack_offsets():
    off, out = 0, {}
    for name, rows in PACK_ROWS:
        out[name] = (off, rows)
        off += rows
    return out


PACK_OFF = _pack_offsets()


def _dot(a, b):
    return jnp.dot(a, b, preferred_element_type=F32)


def _dot_nt(a, b):
    return lax.dot_general(a, b, (((1,), (1,)), ((), ())), preferred_element_type=F32)


def _dot_tn(a, b):
    return lax.dot_general(a, b, (((0,), (0,)), ((), ())), preferred_element_type=F32)


def _dot_f32(a, b):
    return jnp.dot(a, b, precision=lax.Precision.HIGHEST, preferred_element_type=F32)


def _sigmoid(x):
    return jax.nn.sigmoid(x)


def _log_sigmoid(x):
    return jnp.minimum(x, 0.0) - jnp.log1p(jnp.exp(-jnp.abs(x)))


def _params(sem):
    return pltpu.CompilerParams(dimension_semantics=sem, vmem_limit_bytes=VMEM_LIMIT_BYTES)


def _full(shape):
    n = len(shape)
    return pl.BlockSpec(shape, lambda *_: (0,) * n)


def _mesh_pos():
    return lax.axis_index("x"), lax.axis_index("y"), lax.axis_index("c")


def _peer(k):
    x, y, c = _mesh_pos()
    px = 1 - x if (k >> 2) & 1 else x
    py = 1 - y if (k >> 1) & 1 else y
    pc = 1 - c if k & 1 else c
    return (px, py, pc), 4 * px + 2 * py + pc


def _remote(src, dst, send_sem, recv_sem, to):
    return pltpu.make_async_remote_copy(src_ref=src, dst_ref=dst, send_sem=send_sem, recv_sem=recv_sem, device_id=to,
                                        device_id_type=pl.DeviceIdType.MESH)


def _other_chips():
    x, y, _ = _mesh_pos()
    return [(1 - x, y), (x, 1 - y), (1 - x, 1 - y)]


def _two_level_gather(src, dst, send_sems, recv_sems, local_sems):
    n = len(src)
    x, y, c = _mesh_pos()
    me = 4 * x + 2 * y + c
    sibling = (x, y, 1 - c)
    chips = _other_chips()

    def copy(a, k, block, to, own):
        return _remote(src[a] if own else dst[a].at[block], dst[a].at[block], send_sems.at[a, k], recv_sems.at[a, k], to)

    local = [pltpu.make_async_copy(src[a], dst[a].at[me], local_sems.at[a]) for a in range(n)]
    first = [copy(a, 0, me, sibling, True) for a in range(n)]
    first += [copy(a, 1 + j, me, (*chip, c), True) for j, chip in enumerate(chips) for a in range(n)]
    for cp in local + first:
        cp.start()
    passed = []
    for j, (px, py) in enumerate(chips):
        block = 4 * px + 2 * py + c
        for a in range(n):
            copy(a, 1 + j, block, sibling, False).wait_recv()
            passed.append(copy(a, 4 + j, block, sibling, False))
            passed[-1].start()
    for a in range(n):
        copy(a, 0, 4 * x + 2 * y + (1 - c), sibling, False).wait_recv()
    for j, (px, py) in enumerate(chips):
        for a in range(n):
            copy(a, 4 + j, 4 * px + 2 * py + (1 - c), sibling, False).wait_recv()
    for cp in first + passed:
        cp.wait_send()
    for cp in local:
        cp.wait()


GATHER_COPIES = 7


def _gather_weights(w_in_b, w_out_b, conv_w, c_row):
    shards = (w_in_b, w_out_b, conv_w, c_row)
    n = len(shards)

    def body(*refs):
        _two_level_gather(refs[:n], refs[n:2 * n], *refs[2 * n:])

    hbm = pl.BlockSpec(memory_space=pltpu.HBM)
    return pl.pallas_call(
        body, name="gather_weights",
        out_shape=tuple(jax.ShapeDtypeStruct((N_DEV,) + s.shape, s.dtype) for s in shards),
        in_specs=[hbm] * n, out_specs=tuple([hbm] * n),
        scratch_shapes=[pltpu.SemaphoreType.DMA((n, GATHER_COPIES)), pltpu.SemaphoreType.DMA((n, GATHER_COPIES)),
                        pltpu.SemaphoreType.DMA((n,))],
    )(*shards)


def _ada_mod(c_all16, w_ada_blk, b_ada_blk):
    def body(c_ref, w_ref, b_ref, out_ref, sc_ref, send_sems, recv_sems):
        x, y, c = _mesh_pos()
        me = 4 * x + 2 * y + c
        cv = c_ref[...]
        sc = cv * _sigmoid(cv)
        sc_ref[...] = sc
        blk = _dot(sc.astype(BF16), w_ref[...].astype(BF16)) + b_ref[...]
        out_ref[me] = blk[0:N_DEV, :]
        copies = []
        for k in range(1, N_DEV):
            peer, _ = _peer(k)
            copies.append(pltpu.make_async_remote_copy(
                src_ref=out_ref.at[me], dst_ref=out_ref.at[me], send_sem=send_sems.at[k - 1],
                recv_sem=recv_sems.at[k - 1], device_id=peer, device_id_type=pl.DeviceIdType.MESH))
        for cp in copies:
            cp.start()
        for cp in copies:
            cp.wait()

    vmem = pl.BlockSpec(memory_space=pltpu.VMEM)
    return pl.pallas_call(
        body, name="ada_mod",
        out_shape=(jax.ShapeDtypeStruct((N_DEV, N_DEV, ADA_SHARD), F32),
                   jax.ShapeDtypeStruct(c_all16.shape, F32)),
        in_specs=[vmem] * 3, out_specs=(vmem, vmem),
        scratch_shapes=[pltpu.SemaphoreType.DMA((N_DEV - 1,)), pltpu.SemaphoreType.DMA((N_DEV - 1,))],
    )(c_all16, w_ada_blk, b_ada_blk)


N_CHIPS = 4


def _pair_swap(g_in, g_out, g_cw):
    parts = (g_in, g_out, g_cw)
    n = len(parts)

    def body(*refs):
        src, dst = refs[:n], refs[n:2 * n]
        send_sems, recv_sems = refs[2 * n:]
        x, y, c = _mesh_pos()
        copies = [_remote(src[a].at[2 * q + (1 - c)], dst[a].at[q], send_sems.at[a, q], recv_sems.at[a, q],
                          (x, y, 1 - c)) for a in range(n) for q in range(N_CHIPS)]
        for cp in copies:
            cp.start()
        for cp in copies:
            cp.wait()

    hbm = pl.BlockSpec(memory_space=pltpu.HBM)
    return pl.pallas_call(
        body, name="pair_swap",
        out_shape=tuple(jax.ShapeDtypeStruct((N_CHIPS,) + s.shape[1:], s.dtype) for s in parts),
        in_specs=[hbm] * n, out_specs=tuple([hbm] * n),
        scratch_shapes=[pltpu.SemaphoreType.DMA((n, N_CHIPS)), pltpu.SemaphoreType.DMA((n, N_CHIPS))],
    )(*parts)


def _pair_add(core, g_in, r_in, g_out, r_out, g_cw, r_cw):
    pairs = ((g_in, r_in, BF16), (g_out, r_out, BF16), (g_cw, r_cw, F32))

    def body(core_ref, *refs):
        for a in range(len(pairs)):
            g_ref, r_ref, o_ref = refs[2 * a], refs[2 * a + 1], refs[2 * len(pairs) + a]
            o_ref[0] = (g_ref[0, 0] + r_ref[0]).astype(o_ref.dtype)

    in_specs, out_specs, out_shape, args = [], [], [], []
    for g, r, dt in pairs:
        blk = g.shape[1:]
        zeros = (0,) * len(blk)
        in_specs.append(pl.BlockSpec((1, 1) + blk, lambda q, core_ref, zeros=zeros: (q, core_ref[0]) + zeros))
        in_specs.append(pl.BlockSpec((1,) + blk, lambda q, core_ref, zeros=zeros: (q,) + zeros))
        out_specs.append(pl.BlockSpec((1,) + blk, lambda q, core_ref, zeros=zeros: (q,) + zeros))
        out_shape.append(jax.ShapeDtypeStruct((N_CHIPS,) + blk, dt))
        args += [g.reshape((N_CHIPS, 2) + blk), r]
    return pl.pallas_call(
        body, name="pair_add", out_shape=tuple(out_shape),
        grid_spec=pltpu.PrefetchScalarGridSpec(num_scalar_prefetch=1, grid=(N_CHIPS,), in_specs=in_specs,
                                               out_specs=tuple(out_specs)),
        compiler_params=_params(("arbitrary",)),
    )(core, *args)


def _chip_exchange(s_in, s_out, s_cw, pack):
    parts = (s_in, s_out, s_cw)
    n = len(parts)

    def body(*refs):
        src, pack_ref = refs[:n], refs[n]
        dst, packs_ref = refs[n + 1:2 * n + 1], refs[2 * n + 1]
        send_sems, recv_sems, local_sems, pk_send, pk_recv, pk_local = refs[2 * n + 2:]
        x, y, c = _mesh_pos()
        mine = 2 * x + y
        local = [pltpu.make_async_copy(src[a].at[mine], dst[a].at[mine], local_sems.at[a]) for a in range(n)]
        copies = [_remote(src[a].at[2 * px + py], dst[a].at[mine], send_sems.at[a, j], recv_sems.at[a, j], (px, py, c))
                  for j, (px, py) in enumerate(_other_chips()) for a in range(n)]
        for cp in local + copies:
            cp.start()
        _two_level_gather((pack_ref,), (packs_ref,), pk_send, pk_recv, pk_local)
        for cp in copies:
            cp.wait()
        for cp in local:
            cp.wait()

    hbm = pl.BlockSpec(memory_space=pltpu.HBM)
    out_shape = tuple(jax.ShapeDtypeStruct(s.shape, s.dtype) for s in parts)
    out_shape += (jax.ShapeDtypeStruct((N_DEV,) + pack.shape, pack.dtype),)
    return pl.pallas_call(
        body, name="chip_exchange", out_shape=out_shape,
        in_specs=[hbm] * (n + 1), out_specs=tuple([hbm] * (n + 1)),
        scratch_shapes=[pltpu.SemaphoreType.DMA((n, N_CHIPS - 1)), pltpu.SemaphoreType.DMA((n, N_CHIPS - 1)),
                        pltpu.SemaphoreType.DMA((n,)), pltpu.SemaphoreType.DMA((1, GATHER_COPIES)),
                        pltpu.SemaphoreType.DMA((1, GATHER_COPIES)), pltpu.SemaphoreType.DMA((1,))],
    )(s_in, s_out, s_cw, pack)


def _fwd_proj(x, norm_g, scale, shift, w_main, w_g):
    seq = x.shape[0]
    tm = min(512, seq)
    tn = 512

    def body(x_ref, ng_ref, sc_ref, sh_ref, w_ref, wg_ref, proj_ref, gates_ref, h_ref):
        xt = x_ref[...]
        r = lax.rsqrt(jnp.mean(xt * xt, axis=-1, keepdims=True) + EPS)
        h = ((xt * r) * ng_ref[...]) * (1.0 + sc_ref[...]) + sh_ref[...]
        hb = h.astype(BF16)
        h_ref[...] = hb
        gates_ref[...] = _dot(hb, wg_ref[...])
        for j in range(N_MAIN // tn):
            proj_ref[:, j * tn:(j + 1) * tn] = _dot(hb, w_ref[:, j * tn:(j + 1) * tn])

    vec = _full((1, D_MODEL))
    tile = pl.BlockSpec((tm, D_MODEL), lambda i: (i, 0))
    return pl.pallas_call(
        body, name="fwd_proj", grid=(seq // tm,),
        out_shape=(jax.ShapeDtypeStruct((seq, N_MAIN), F32), jax.ShapeDtypeStruct((seq, 128), F32),
                   jax.ShapeDtypeStruct((seq, D_MODEL), BF16)),
        in_specs=[tile, vec, vec, vec, _full((D_MODEL, N_MAIN)), _full((D_MODEL, 128))],
        out_specs=(pl.BlockSpec((tm, N_MAIN), lambda i: (i, 0)), pl.BlockSpec((tm, 128), lambda i: (i, 0)), tile),
        compiler_params=_params(("arbitrary",)),
    )(x, norm_g, scale, shift, w_main, w_g)


def _gate_forms(gpre):
    r = lax.broadcasted_iota(jnp.int32, (CHUNK, CHUNK), 0)
    c = lax.broadcasted_iota(jnp.int32, (CHUNK, CHUNK), 1)
    causal = c <= r
    ltri = jnp.where(causal, 1.0, 0.0).astype(F32)
    utri = jnp.where(r <= c, 1.0, 0.0).astype(F32)
    bcol = _dot_f32(ltri, _log_sigmoid(gpre))
    gt8 = gpre.T[0:8, :]
    brow = _dot_f32(_log_sigmoid(gt8), utri)
    return causal, utri, bcol, gt8, brow


def _in_lockstep(stages):
    alive = list(stages)
    while alive:
        still = []
        for g in alive:
            try:
                next(g)
                still.append(g)
            except StopIteration:
                pass
        alive = still


def _head_fwd(qh, kh, vh, bc, br, igr, m_prev, c_h, n_row, causal):
    qb, kb, vb, cb = qh.astype(BF16), kh.astype(BF16), vh.astype(BF16), c_h.astype(BF16)
    qk = _dot_nt(qb, kb)
    cq = _dot_nt(qb, cb)
    yield
    dlog = jnp.where(causal, bc - br + igr, NEG_BIG)
    inter_log = bc + m_prev
    m_t = jnp.maximum(inter_log, jnp.max(dlog, axis=-1, keepdims=True))
    yield
    dmat = jnp.exp(dlog - m_t)
    inter = jnp.exp(inter_log - m_t)
    s = qk * dmat
    sv = _dot(s.astype(BF16), vb)
    yield
    nq = jnp.sum(qh * n_row, axis=-1, keepdims=True)
    den = jnp.sum(s, axis=-1, keepdims=True) + inter * nq
    emt = jnp.exp(-m_t)
    yield
    num = sv + inter * cq
    dn = jnp.maximum(jnp.abs(den), emt)
    hm = num / dn
    return dict(dmat=dmat, inter=inter, qb=qb, kb=kb, vb=vb, cb=cb, s=s, cq=cq, nq=nq, den=den, emt=emt,
                dn=dn, hm=hm)


def _state_weights(bc, igc, m_prev, m_new=None):
    last = lax.broadcasted_iota(jnp.int32, (CHUNK, 1), 0) == CHUNK - 1
    b_last = jnp.sum(jnp.where(last, bc, 0.0), axis=0, keepdims=True)
    wlog = b_last - bc + igc
    if m_new is None:
        m_new = jnp.maximum(b_last + m_prev, jnp.max(wlog, axis=0, keepdims=True))
    w_c = jnp.exp(wlog - m_new)
    decay = jnp.exp(b_last + m_prev - m_new)
    return w_c, decay, m_new, last


def _rows_back(x, k):
    return x if k == 0 else pltpu.roll(x, k, 0)


def _rows_ahead(x, k):
    return x if k == 0 else pltpu.roll(x, x.shape[0] - k, 0)


def _conv_taps(xpad):
    return [_rows_back(xpad, CONV_WIDTH - 1 - j)[CONV_HALO:, :] for j in range(CONV_WIDTH)]


def _conv_pre(taps, cw_ref, cb_ref):
    a = cb_ref[...]
    for j in range(CONV_WIDTH):
        a = a + cw_ref[j:j + 1, :] * taps[j]
    return a


def _window_sum(x, w, shift):
    k = 1
    while k < w:
        x = x + shift(x, k)
        k *= 2
    return x


def _pool_window_sum(upad_ref, g, w):
    lanes = slice(g * POOL_GROUP_DIM, (g + 1) * POOL_GROUP_DIM)
    return _window_sum(upad_ref[:, lanes], w, _rows_back)[POOL_HALO:, :]


def _pool_inv_count(chunk_idx, w):
    pos = chunk_idx * CHUNK + lax.broadcasted_iota(jnp.int32, (CHUNK, 1), 0) + 1
    return 1.0 / jnp.minimum(pos, w).astype(F32)


def _mixer_in_specs(cmap, n_chunks):
    def rows(i):
        return cmap(i)
    return [
        pl.BlockSpec((CHUNK, 1024), lambda i: (rows(i), 0)),
        pl.BlockSpec((CHUNK, 1024), lambda i: (rows(i), 1)),
        pl.BlockSpec((CHUNK, 512), lambda i: (rows(i), 4)),
        pl.BlockSpec((CHUNK, 512), lambda i: (rows(i), 5)),
        pl.BlockSpec((CHUNK, 512), lambda i: (rows(i), 6)),
        pl.BlockSpec((POOL_HALO, 512), lambda i: (jnp.maximum(rows(i) * (CHUNK // POOL_HALO) - 1, 0), 0)),
        pl.BlockSpec((CONV_HALO, 1024), lambda i: (jnp.maximum(rows(i) * (CHUNK // CONV_HALO) - 1, 0), 1)),
    ]


def _mix_fwd(proj, gates, bg_pad, conv_w8, conv_b, w_pool, ls_pool, mh_g):
    seq = proj.shape[0]
    n_chunks = seq // CHUNK

    def body(uz_ref, qk_ref, v_ref, o_ref, zm_ref, uh_ref, qkh_ref, g_ref, bg_ref, cw_ref, cb_ref, wp_ref,
             ls_ref, mhg_ref, mix_ref, cst_ref, nst_ref, mst_ref, c_scr, n_scr, m_scr, xpad, upad):
        i = pl.program_id(0)

        @pl.when(i == 0)
        def _():
            c_scr[...] = jnp.zeros_like(c_scr)
            n_scr[...] = jnp.zeros_like(n_scr)
            m_scr[...] = jnp.zeros_like(m_scr)

        cst_ref[0] = c_scr[...]
        nst_ref[0] = n_scr[...]
        mst_ref[0] = m_scr[...]
        first = i == 0

        upad[0:POOL_HALO, :] = jnp.where(first, 0.0, uh_ref[...])
        upad[POOL_HALO:POOL_HALO + CHUNK, :] = uz_ref[:, 0:D_POOL]
        for g, w in enumerate(POOL_WINDOWS):
            lanes = slice(g * POOL_GROUP_DIM, (g + 1) * POOL_GROUP_DIM)
            pooled = _pool_window_sum(upad, g, w) * _pool_inv_count(i, w) - uz_ref[:, lanes]
            y = _dot(pooled.astype(BF16), wp_ref[g].astype(BF16)) * ls_ref[:, lanes]
            zp = uz_ref[:, D_POOL + g * POOL_GROUP_DIM:D_POOL + (g + 1) * POOL_GROUP_DIM]
            mix_ref[:, lanes] = (y * (zp * _sigmoid(zp))).astype(BF16)

        xpad[0:CONV_HALO, :] = jnp.where(first, 0.0, qkh_ref[...])
        xpad[CONV_HALO:CONV_HALO + CHUNK, :] = qk_ref[...]
        a = _conv_pre(_conv_taps(xpad[...]), cw_ref, cb_ref)
        qk = a * _sigmoid(a)

        gpre = g_ref[...] + bg_ref[...]
        causal, _, bcol, gt8, brow = _gate_forms(gpre)
        def head(h):
            lanes = slice(h * HEAD_DIM, (h + 1) * HEAD_DIM)
            qh = qk[:, lanes]
            kh = qk[:, D_MLSTM + h * HEAD_DIM:D_MLSTM + (h + 1) * HEAD_DIM] * (HEAD_DIM ** -0.5)
            vh = v_ref[:, lanes]
            bc = bcol[:, N_HEADS + h:N_HEADS + h + 1]
            br = brow[N_HEADS + h:N_HEADS + h + 1, :]
            igr = gt8[h:h + 1, :]
            igc = gpre[:, h:h + 1]
            m_prev = m_scr[h:h + 1, 0:1]
            c_h = c_scr[h]
            n_row = n_scr[h:h + 1, :]
            w_c, decay, m_new, _ = _state_weights(bc, igc, m_prev)
            c_scr[h] = decay * c_h + _dot_tn((vh * w_c).astype(BF16), kh.astype(BF16))
            n_scr[h:h + 1, :] = decay * n_row + jnp.sum(w_c * kh, axis=0, keepdims=True)
            m_scr[h:h + 1, :] = jnp.broadcast_to(m_new, (1, 128))
            f = yield from _head_fwd(qh, kh, vh, bc, br, igr, m_prev, c_h, n_row, causal)
            yield
            hm = f["hm"]
            hn = hm * lax.rsqrt(jnp.mean(hm * hm, axis=-1, keepdims=True) + EPS) * mhg_ref[:, lanes]
            zm = zm_ref[:, lanes]
            out = hn * _sigmoid(o_ref[:, lanes]) * (zm * _sigmoid(zm))
            mix_ref[:, D_POOL + h * HEAD_DIM:D_POOL + (h + 1) * HEAD_DIM] = out.astype(BF16)

        _in_lockstep(head(h) for h in range(N_HEADS))

    cmap = lambda i: i
    in_specs = _mixer_in_specs(cmap, n_chunks) + [
        pl.BlockSpec((CHUNK, 128), lambda i: (i, 0)),
        _full((1, 128)), _full((8, 1024)), _full((1, 1024)), _full((4, 128, 128)), _full((1, 512)),
        _full((1, 512))]
    return pl.pallas_call(
        body, name="mix_fwd", grid=(n_chunks,),
        out_shape=(jax.ShapeDtypeStruct((seq, D_MODEL), BF16),
                   jax.ShapeDtypeStruct((n_chunks, N_HEADS, HEAD_DIM, HEAD_DIM), F32),
                   jax.ShapeDtypeStruct((n_chunks, 8, 128), F32),
                   jax.ShapeDtypeStruct((n_chunks, 8, 128), F32)),
        in_specs=in_specs,
        out_specs=(pl.BlockSpec((CHUNK, D_MODEL), lambda i: (i, 0)),
                   pl.BlockSpec((1, N_HEADS, HEAD_DIM, HEAD_DIM), lambda i: (i, 0, 0, 0)),
                   pl.BlockSpec((1, 8, 128), lambda i: (i, 0, 0)),
                   pl.BlockSpec((1, 8, 128), lambda i: (i, 0, 0))),
        scratch_shapes=[pltpu.VMEM((N_HEADS, HEAD_DIM, HEAD_DIM), F32), pltpu.VMEM((8, 128), F32),
                        pltpu.VMEM((8, 128), F32), pltpu.VMEM((CONV_HALO + CHUNK, 1024), F32),
                        pltpu.VMEM((POOL_HALO + CHUNK, D_POOL), F32)],
        compiler_params=_params(("arbitrary",)),
    )(proj, proj, proj, proj, proj, proj, proj, gates, bg_pad, conv_w8, conv_b, w_pool, ls_pool, mh_g)


def _out_fwd_bwd(mix, x, tgt, w_out_b, gate, final_g):
    seq = x.shape[0]
    tm = min(256, seq)

    def body(mix_ref, x_ref, t_ref, w_ref, gate_ref, fg_ref, dx2_ref, dmix_ref, dwo_ref, dgate_ref, dfg_ref,
             loss_ref):
        @pl.when(pl.program_id(0) == 0)
        def _():
            dwo_ref[...] = jnp.zeros_like(dwo_ref)
            dgate_ref[...] = jnp.zeros_like(dgate_ref)
            dfg_ref[...] = jnp.zeros_like(dfg_ref)
            loss_ref[...] = jnp.zeros_like(loss_ref)

        mixb = mix_ref[...]
        w = w_ref[...]
        gate_v = gate_ref[...]
        fg = fg_ref[...]
        o2 = _dot(mixb, w)
        x2 = x_ref[...] + gate_v * o2
        r2 = lax.rsqrt(jnp.mean(x2 * x2, axis=-1, keepdims=True) + EPS)
        x2n = x2 * r2
        err = x2n * fg - t_ref[...]
        part = 0.5 * jnp.sum(jnp.sum(err * err, axis=-1, keepdims=True), axis=0, keepdims=True) / D_MODEL
        loss_ref[...] += jnp.broadcast_to(part, loss_ref.shape)
        dy = err / D_MODEL
        dfg_ref[...] += jnp.sum(dy * x2n, axis=0, keepdims=True)
        gdy = dy * fg
        dx2 = r2 * (gdy - x2n * jnp.mean(gdy * x2n, axis=-1, keepdims=True))
        dx2_ref[...] = dx2
        dgate_ref[...] += jnp.sum(dx2 * o2, axis=0, keepdims=True)
        do2 = (dx2 * gate_v).astype(BF16)
        dmix_ref[...] = _dot_nt(do2, w)
        dwo_ref[...] += _dot_tn(mixb, do2)

    tile = pl.BlockSpec((tm, D_MODEL), lambda i: (i, 0))
    vec = _full((1, D_MODEL))
    return pl.pallas_call(
        body, name="out_fwd_bwd", grid=(seq // tm,),
        out_shape=(jax.ShapeDtypeStruct((seq, D_MODEL), F32), jax.ShapeDtypeStruct((seq, D_MODEL), F32),
                   jax.ShapeDtypeStruct((D_MODEL, D_MODEL), F32), jax.ShapeDtypeStruct((1, D_MODEL), F32),
                   jax.ShapeDtypeStruct((1, D_MODEL), F32), jax.ShapeDtypeStruct((1, 128), F32)),
        in_specs=[tile, tile, tile, _full((D_MODEL, D_MODEL)), vec, vec],
        out_specs=(tile, tile, _full((D_MODEL, D_MODEL)), vec, vec, _full((1, 128))),
        compiler_params=_params(("arbitrary",)),
    )(mix, x, tgt, w_out_b, gate, final_g)


def _mix_bwd(proj, gates, dmix, cst, nst, mst, bg_pad, conv_w8, conv_b, w_pool, ls_pool, mh_g):
    seq = proj.shape[0]
    n_chunks = seq // CHUNK

    def body(uz_ref, qk_ref, v_ref, o_ref, zm_ref, uh_ref, qkh_ref, g_ref, dmix_ref, cst_ref, nst_ref, mst_ref,
             mnx_ref, bg_ref, cw_ref, cb_ref, wp_ref, ls_ref, mhg_ref,
             dp_ref, dcw_ref, dcb_ref, dwp_ref, dls_ref, dmhg_ref, dbg_ref,
             dc_scr, dn_scr, xpad, upad, dapad, dpipad):
        i = pl.program_id(0)
        ci = n_chunks - 1 - i

        @pl.when(i == 0)
        def _():
            for ref in (dc_scr, dn_scr, dcw_ref, dcb_ref, dwp_ref, dls_ref, dmhg_ref, dbg_ref):
                ref[...] = jnp.zeros_like(ref)
            dapad[CHUNK:CHUNK + CONV_HALO, :] = jnp.zeros((CONV_HALO, 1024), F32)
            dpipad[CHUNK:CHUNK + POOL_HALO, :] = jnp.zeros((POOL_HALO, D_POOL), F32)

        first = ci == 0

        upad[0:POOL_HALO, :] = jnp.where(first, 0.0, uh_ref[...])
        upad[POOL_HALO:POOL_HALO + CHUNK, :] = uz_ref[:, 0:D_POOL]
        dpooled = []
        for g, w in enumerate(POOL_WINDOWS):
            lanes = slice(g * POOL_GROUP_DIM, (g + 1) * POOL_GROUP_DIM)
            zlanes = slice(D_POOL + g * POOL_GROUP_DIM, D_POOL + (g + 1) * POOL_GROUP_DIM)
            inv = _pool_inv_count(ci, w)
            pooled = _pool_window_sum(upad, g, w) * inv - uz_ref[:, lanes]
            pb = pooled.astype(BF16)
            wpb = wp_ref[g].astype(BF16)
            yw = _dot(pb, wpb)
            ls = ls_ref[:, lanes]
            zp = uz_ref[:, zlanes]
            sg = _sigmoid(zp)
            dpo = dmix_ref[:, lanes]
            dp_ref[:, zlanes] = (dpo * (yw * ls) * (sg * (1.0 + zp * (1.0 - sg)))).astype(BF16)
            dy = dpo * (zp * sg)
            dls_ref[:, lanes] += jnp.sum(dy * yw, axis=0, keepdims=True)
            dyw = (dy * ls).astype(BF16)
            dwp_ref[g] += _dot_tn(pb, dyw)
            dpl = _dot_nt(dyw, wpb)
            dpooled.append(dpl)
            dpipad[0:CHUNK, lanes] = dpl * inv
        for g, w in enumerate(POOL_WINDOWS):
            lanes = slice(g * POOL_GROUP_DIM, (g + 1) * POOL_GROUP_DIM)
            du = _window_sum(dpipad[:, lanes], w, _rows_ahead)[0:CHUNK, :] - dpooled[g]
            dp_ref[:, lanes] = du.astype(BF16)
        dpipad[CHUNK:CHUNK + POOL_HALO, :] = dpipad[0:POOL_HALO, :]

        xpad[0:CONV_HALO, :] = jnp.where(first, 0.0, qkh_ref[...])
        xpad[CONV_HALO:CONV_HALO + CHUNK, :] = qk_ref[...]
        taps = _conv_taps(xpad[...])
        a = _conv_pre(taps, cw_ref, cb_ref)
        sga = _sigmoid(a)
        qk = a * sga
        dsilu_a = sga * (1.0 + a * (1.0 - sga))

        gpre = g_ref[...] + bg_ref[...]
        causal, utri, bcol, gt8, brow = _gate_forms(gpre)
        lane = lax.broadcasted_iota(jnp.int32, (CHUNK, 128), 1)
        row = lax.broadcasted_iota(jnp.int32, (CHUNK, 128), 0)
        col_g_rows, dig_parts, db_parts = [], [], []
        scale_k = HEAD_DIM ** -0.5

        def head(h):
            lanes = slice(h * HEAD_DIM, (h + 1) * HEAD_DIM)
            klanes = slice(D_MLSTM + h * HEAD_DIM, D_MLSTM + (h + 1) * HEAD_DIM)
            qh = qk[:, lanes]
            kh = qk[:, klanes] * scale_k
            vh = v_ref[:, lanes]
            bc = bcol[:, N_HEADS + h:N_HEADS + h + 1]
            br = brow[N_HEADS + h:N_HEADS + h + 1, :]
            igr = gt8[h:h + 1, :]
            igc = gpre[:, h:h + 1]
            m_prev = mst_ref[0, h:h + 1, 0:1]
            m_next = mnx_ref[0, h:h + 1, 0:1]
            c_h = cst_ref[0, h]
            n_row = nst_ref[0, h:h + 1, :]
            w_c, decay, _, last = _state_weights(bc, igc, m_prev, m_next)
            dcn = dc_scr[h]
            dnn = dn_scr[h:h + 1, :]
            dcnb = dcn.astype(BF16)
            vb0, kb0 = vh.astype(BF16), kh.astype(BF16)
            amat = _dot(vb0, dcnb) + dnn
            kdc = _dot_nt(kb0, dcnb)
            ddecay = (jnp.sum(jnp.sum(dcn * c_h, axis=-1, keepdims=True), axis=0, keepdims=True)
                      + jnp.sum(dnn * n_row, axis=-1, keepdims=True))
            f = yield from _head_fwd(qh, kh, vh, bc, br, igr, m_prev, c_h, n_row, causal)
            qb, kb, vb, cb = f["qb"], f["kb"], f["vb"], f["cb"]
            s, dmat, inter, den, dn, hm = f["s"], f["dmat"], f["inter"], f["den"], f["dn"], f["hm"]
            yield

            rinv = lax.rsqrt(jnp.mean(hm * hm, axis=-1, keepdims=True) + EPS)
            hmn = hm * rinv
            gh = mhg_ref[:, lanes]
            o_pre = o_ref[:, lanes]
            og = _sigmoid(o_pre)
            zm = zm_ref[:, lanes]
            sgz = _sigmoid(zm)
            sz = zm * sgz
            dout = dmix_ref[:, D_POOL + h * HEAD_DIM:D_POOL + (h + 1) * HEAD_DIM]
            hn = hmn * gh
            dp_ref[:, 2560 + h * HEAD_DIM:2560 + (h + 1) * HEAD_DIM] = (
                dout * hn * sz * og * (1.0 - og)).astype(BF16)
            dp_ref[:, 3072 + h * HEAD_DIM:3072 + (h + 1) * HEAD_DIM] = (
                dout * hn * og * (sgz * (1.0 + zm * (1.0 - sgz)))).astype(BF16)
            dhn = dout * og * sz
            dmhg_ref[:, lanes] += jnp.sum(dhn * hmn, axis=0, keepdims=True)
            dyn = dhn * gh
            dhm = rinv * (dyn - hmn * jnp.mean(dyn * hmn, axis=-1, keepdims=True))
            yield

            inv_dn = 1.0 / dn
            dnum = dhm * inv_dn
            hd = jnp.sum(dhm * hm, axis=-1, keepdims=True)
            dden = jnp.where(jnp.abs(den) > f["emt"], -hd * inv_dn * jnp.sign(den), 0.0)
            dnb = dnum.astype(BF16)
            dnv = _dot_nt(dnb, vb)
            dv = _dot_tn(s.astype(BF16), dnb)
            dnc = _dot(dnb, cb)
            dc_prev = _dot_tn((inter * dnum).astype(BF16), qb)
            yield
            ds = dnv + dden
            dqk = (ds * dmat).astype(BF16)
            dqk_k = _dot(dqk, kb)
            dk = _dot_tn(dqk, qb)
            yield
            gmat = ds * s
            row_g = jnp.sum(gmat, axis=-1, keepdims=True)
            col_g_rows.append(jnp.where(row == h, jnp.sum(gmat, axis=0, keepdims=True), 0.0))
            gcol = inter * (jnp.sum(dnum * f["cq"], axis=-1, keepdims=True) + dden * f["nq"])
            dn_prev = jnp.sum((inter * dden) * qh, axis=0, keepdims=True)
            dw = jnp.sum(amat * kh, axis=-1, keepdims=True)
            e = dw * w_c
            db_last = ddecay * decay + jnp.sum(e, axis=0, keepdims=True)
            dig_parts.append(jnp.where(lane == h, e, 0.0))
            db_parts.append(jnp.where(lane == N_HEADS + h, row_g + gcol - e + jnp.where(last, db_last, 0.0), 0.0))
            dc_scr[h] = decay * dcn + dc_prev
            dn_scr[h:h + 1, :] = decay * dnn + dn_prev
            yield
            dq = dqk_k + inter * (dnc + dden * n_row)
            dp_ref[:, 2048 + h * HEAD_DIM:2048 + (h + 1) * HEAD_DIM] = (dv + w_c * kdc).astype(BF16)
            dapad[0:CHUNK, lanes] = dq * dsilu_a[:, lanes]
            dapad[0:CHUNK, klanes] = (dk + w_c * amat) * scale_k * dsilu_a[:, klanes]

        _in_lockstep(head(h) for h in range(N_HEADS))

        cs_t = sum(col_g_rows[1:], col_g_rows[0]).T
        dig_all = sum(dig_parts[1:], dig_parts[0]) + cs_t
        db_cols = sum(db_parts[1:], db_parts[0])
        shifted = jnp.zeros((CHUNK, 128), F32)
        for h in range(N_HEADS):
            shifted = shifted + jnp.where(lane == N_HEADS + h, cs_t[:, h:h + 1], 0.0)
        dlf = _dot_f32(utri, db_cols - shifted)
        dgates = dig_all + dlf * _sigmoid(-gpre)
        dp_ref[:, N_MAIN:N_MAIN + 128] = dgates.astype(BF16)
        dp_ref[:, N_MAIN + 128:N_PAD] = jnp.zeros((CHUNK, N_PAD - N_MAIN - 128), BF16)
        dbg_ref[...] += jnp.sum(dgates, axis=0, keepdims=True)

        da_pad = dapad[...]
        da = da_pad[0:CHUNK, :]
        dcb_ref[...] += jnp.sum(da, axis=0, keepdims=True)
        dx = jnp.zeros((CHUNK, 1024), F32)
        for j in range(CONV_WIDTH):
            dcw_ref[j:j + 1, :] += jnp.sum(da * taps[j], axis=0, keepdims=True)
            dx = dx + cw_ref[j:j + 1, :] * _rows_ahead(da_pad, CONV_WIDTH - 1 - j)[0:CHUNK, :]
        dp_ref[:, 1024:2048] = dx.astype(BF16)
        dapad[CHUNK:CHUNK + CONV_HALO, :] = dapad[0:CONV_HALO, :]

    cmap = lambda i: n_chunks - 1 - i
    in_specs = _mixer_in_specs(cmap, n_chunks) + [
        pl.BlockSpec((CHUNK, 128), lambda i: (cmap(i), 0)),
        pl.BlockSpec((CHUNK, D_MODEL), lambda i: (cmap(i), 0)),
        pl.BlockSpec((1, N_HEADS, HEAD_DIM, HEAD_DIM), lambda i: (cmap(i), 0, 0, 0)),
        pl.BlockSpec((1, 8, 128), lambda i: (cmap(i), 0, 0)),
        pl.BlockSpec((1, 8, 128), lambda i: (cmap(i), 0, 0)),
        pl.BlockSpec((1, 8, 128), lambda i: (jnp.minimum(cmap(i) + 1, n_chunks - 1), 0, 0)),
        _full((1, 128)), _full((8, 1024)), _full((1, 1024)), _full((4, 128, 128)), _full((1, 512)),
        _full((1, 512))]
    return pl.pallas_call(
        body, name="mix_bwd", grid=(n_chunks,),
        out_shape=(jax.ShapeDtypeStruct((seq, N_PAD), BF16), jax.ShapeDtypeStruct((8, 1024), F32),
                   jax.ShapeDtypeStruct((1, 1024), F32), jax.ShapeDtypeStruct((4, 128, 128), F32),
                   jax.ShapeDtypeStruct((1, 512), F32), jax.ShapeDtypeStruct((1, 512), F32),
                   jax.ShapeDtypeStruct((1, 128), F32)),
        in_specs=in_specs,
        out_specs=(pl.BlockSpec((CHUNK, N_PAD), lambda i: (cmap(i), 0)), _full((8, 1024)), _full((1, 1024)),
                   _full((4, 128, 128)), _full((1, 512)), _full((1, 512)), _full((1, 128))),
        scratch_shapes=[pltpu.VMEM((N_HEADS, HEAD_DIM, HEAD_DIM), F32), pltpu.VMEM((8, 128), F32),
                        pltpu.VMEM((CONV_HALO + CHUNK, 1024), F32), pltpu.VMEM((POOL_HALO + CHUNK, D_POOL), F32),
                        pltpu.VMEM((CHUNK + CONV_HALO, 1024), F32), pltpu.VMEM((CHUNK + POOL_HALO, D_POOL), F32)],
        compiler_params=_params(("arbitrary",)),
    )(proj, proj, proj, proj, proj, proj, proj, gates, dmix, cst, nst, mst, mst, bg_pad, conv_w8, conv_b,
      w_pool, ls_pool, mh_g)


def _bwd_in(dproj, w_in_t, x, dx2, norm_g, scale):
    seq = x.shape[0]
    tm = min(256, seq)

    def body(dp_ref, wt_ref, x_ref, dx2_ref, ng_ref, sc_ref, gx_ref, dsh_ref, dsc_ref, dng_ref):
        @pl.when(pl.program_id(0) == 0)
        def _():
            dsh_ref[...] = jnp.zeros_like(dsh_ref)
            dsc_ref[...] = jnp.zeros_like(dsc_ref)
            dng_ref[...] = jnp.zeros_like(dng_ref)

        dh = _dot(dp_ref[...], wt_ref[...])
        xt = x_ref[...]
        r = lax.rsqrt(jnp.mean(xt * xt, axis=-1, keepdims=True) + EPS)
        xn = xt * r
        ng = ng_ref[...]
        one_sc = 1.0 + sc_ref[...]
        dsh_ref[...] += jnp.sum(dh, axis=0, keepdims=True)
        dhxn = dh * xn
        dsc_ref[...] += jnp.sum(dhxn * ng, axis=0, keepdims=True)
        dng_ref[...] += jnp.sum(dhxn * one_sc, axis=0, keepdims=True)
        dxn = dh * (ng * one_sc)
        gx_ref[...] = r * (dxn - xn * jnp.mean(dxn * xn, axis=-1, keepdims=True)) + dx2_ref[...]

    tile = pl.BlockSpec((tm, D_MODEL), lambda i: (i, 0))
    vec = _full((1, D_MODEL))
    return pl.pallas_call(
        body, name="bwd_in", grid=(seq // tm,),
        out_shape=(jax.ShapeDtypeStruct((seq, D_MODEL), F32),) + (jax.ShapeDtypeStruct((1, D_MODEL), F32),) * 3,
        in_specs=[pl.BlockSpec((tm, N_PAD), lambda i: (i, 0)), _full((N_PAD, D_MODEL)), tile, tile, vec, vec],
        out_specs=(tile, vec, vec, vec),
        compiler_params=_params(("arbitrary",)),
    )(dproj, w_in_t, x, dx2, norm_g, scale)


def _dw_in(h_b, dproj):
    seq = h_b.shape[0]
    tk = min(1024, seq)
    tn = 768

    def body(h_ref, dp_ref, dw_ref):
        @pl.when(pl.program_id(1) == 0)
        def _():
            dw_ref[...] = jnp.zeros_like(dw_ref)

        dw_ref[...] += _dot_tn(h_ref[...], dp_ref[...])

    return pl.pallas_call(
        body, name="dw_in", grid=(N_PAD // tn, seq // tk),
        out_shape=jax.ShapeDtypeStruct((D_MODEL, N_PAD), F32),
        in_specs=[pl.BlockSpec((tk, D_MODEL), lambda j, t: (t, 0)), pl.BlockSpec((tk, tn), lambda j, t: (t, j))],
        out_specs=pl.BlockSpec((D_MODEL, tn), lambda j, t: (0, j)),
        compiler_params=_params(("arbitrary", "arbitrary")),
    )(h_b, dproj)


def _adam_update(g, w, m, v, g_ref, d_ref, m_ref, v_ref):
    mn = ADAM_B1 * m + (1.0 - ADAM_B1) * g
    vn = ADAM_B2 * v + (1.0 - ADAM_B2) * (g * g)
    m_hat = mn / (1.0 - ADAM_B1 ** ADAM_STEP)
    v_hat = vn / (1.0 - ADAM_B2 ** ADAM_STEP)
    g_ref[...] = g
    d_ref[...] = -ADAM_LR * (m_hat / (jnp.sqrt(v_hat) + ADAM_EPS) + ADAM_WD * w)
    m_ref[...] = mn
    v_ref[...] = vn


def _adam_sum(name, parts, w, m, v, row_tile):
    rows, cols = w.shape
    n_parts = parts.shape[0]

    def body(p_ref, w_ref, m_ref, v_ref, g_out, d_out, m_out, v_out):
        g = p_ref[0].astype(F32)
        for j in range(1, n_parts):
            g = g + p_ref[j].astype(F32)
        _adam_update(g, w_ref[...], m_ref[...], v_ref[...], g_out, d_out, m_out, v_out)

    tile = pl.BlockSpec((row_tile, cols), lambda i: (i, 0))
    return pl.pallas_call(
        body, name=name, grid=(rows // row_tile,),
        out_shape=(jax.ShapeDtypeStruct((rows, cols), F32),) * 4,
        in_specs=[pl.BlockSpec((n_parts, row_tile, cols), lambda i: (0, i, 0)), tile, tile, tile],
        out_specs=(tile,) * 4,
        compiler_params=_params(("arbitrary",)),
    )(parts, w, m, v)


def _adam_ada(sc_all16, dmod_blk16, w, m, v):
    rows, cols = w.shape

    def body(sc_ref, dm_ref, w_ref, m_ref, v_ref, g_out, d_out, m_out, v_out):
        g = _dot_tn(sc_ref[...].astype(BF16), dm_ref[...].astype(BF16))
        _adam_update(g, w_ref[...], m_ref[...], v_ref[...], g_out, d_out, m_out, v_out)

    return pl.pallas_call(
        body, name="adam_w_ada", grid=(1,),
        out_shape=(jax.ShapeDtypeStruct((rows, cols), F32),) * 4,
        in_specs=[_full(sc_all16.shape), _full(dmod_blk16.shape)] + [_full((rows, cols))] * 3,
        out_specs=(_full((rows, cols)),) * 4,
        compiler_params=_params(("arbitrary",)),
    )(sc_all16, dmod_blk16, w, m, v)


def _pack_small(pieces):
    rows = []
    for name, n in PACK_ROWS:
        a = pieces[name].reshape(-1).astype(F32)
        a = jnp.pad(a, (0, n * 128 - a.shape[0]))
        rows.append(a.reshape(n, 128))
    return jnp.concatenate(rows, axis=0)


def _unpack_small(pack, name, shape):
    off, _ = PACK_OFF[name]
    size = 1
    for s in shape:
        size *= s
    n_rows = -(-size // 128)
    return pack[off:off + n_rows].reshape(-1)[:size].reshape(shape)


def _local_step(x2, tgt2, shift, scale, gate, norm_g, w_full_b, w_out_b, conv_w, conv_b, w_pool, ls_pool,
                mh_norm_g, b_gates, final_g):
    w_main = w_full_b[:, :N_MAIN]
    w_g = jnp.pad(w_full_b[:, N_MAIN:], ((0, 0), (0, 128 - (N_IN - N_MAIN))))
    w_in_t = jnp.pad(w_full_b.T, ((0, N_PAD - N_IN), (0, 0)))
    bg_pad = jnp.pad(b_gates, ((0, 0), (0, 128 - b_gates.shape[1])))
    conv_w8 = jnp.pad(conv_w, ((0, 8 - CONV_WIDTH), (0, 0)))
    fg = final_g.reshape(1, D_MODEL)

    proj, gates, h_b = _fwd_proj(x2, norm_g, scale, shift, w_main, w_g)
    mix, cst, nst, mst = _mix_fwd(proj, gates, bg_pad, conv_w8, conv_b, w_pool, ls_pool, mh_norm_g)
    dx2, dmix, dwo, dgate, dfg, loss = _out_fwd_bwd(mix, x2, tgt2, w_out_b, gate, fg)
    dproj, dcw8, dcb, dwp, dls, dmhg, dbg = _mix_bwd(proj, gates, dmix, cst, nst, mst, bg_pad, conv_w8, conv_b,
                                                      w_pool, ls_pool, mh_norm_g)
    gx, dsh, dsc, dng = _bwd_in(dproj, w_in_t, x2, dx2, norm_g, scale)
    dw_in = _dw_in(h_b, dproj)
    return dict(loss=loss, grad_x=gx, dw_in=dw_in[:, :N_IN], dw_out=dwo, dconv_w=dcw8[:CONV_WIDTH], conv_b=dcb,
                w_pool=dwp, ls_pool=dls, mh_norm_g=dmhg, b_gates=dbg, final_g=dfg, norm_g=dng,
                dmod=jnp.concatenate([dsh, dsc, dgate], axis=1))


def kernel(x, c, norm_g, w_ada, b_ada, w_in, b_gates, conv_w, conv_b, w_pool, ls_pool, mh_norm_g, w_out, final_g, loss_target, m_norm_g, m_w_ada, m_b_ada, m_w_in, m_b_gates, m_conv_w, m_conv_b, m_w_pool, m_ls_pool, m_mh_norm_g, m_w_out, m_final_g, v_norm_g, v_w_ada, v_b_ada, v_w_in, v_b_gates, v_conv_w, v_conv_b, v_w_pool, v_ls_pool, v_mh_norm_g, v_w_out, v_final_g):
    seq = x.shape[1]
    me = 4 * lax.axis_index("x") + 2 * lax.axis_index("y") + lax.axis_index("c")

    g_in, g_out, g_cw, g_c = _gather_weights(w_in[0].astype(BF16), w_out[0].astype(BF16), conv_w[0], c)
    w_full_b = jnp.transpose(g_in, (1, 0, 2)).reshape(D_MODEL, N_IN)
    w_out_b = g_out.reshape(D_MODEL, D_MODEL)
    conv_w_full = jnp.transpose(g_cw, (1, 0, 2)).reshape(CONV_WIDTH, 2 * D_MLSTM)
    c_all16 = jnp.pad(g_c.reshape(N_DEV, D_MODEL), ((0, 8), (0, 0)))

    b_ada_blk = lax.dynamic_slice(b_ada, (0, me * ADA_SHARD), (1, ADA_SHARD))
    mod_all, sc_all16 = _ada_mod(c_all16, w_ada[0], b_ada_blk)
    mod = lax.dynamic_index_in_dim(mod_all, me, axis=1, keepdims=False).reshape(1, 3 * D_MODEL)
    shift, scale, gate = mod[:, :D_MODEL], mod[:, D_MODEL:2 * D_MODEL], mod[:, 2 * D_MODEL:]

    r = _local_step(x[0], loss_target[0], shift, scale, gate, norm_g, w_full_b, w_out_b, conv_w_full, conv_b,
                    w_pool[0], ls_pool, mh_norm_g, b_gates, final_g)

    dw_in_blocks = jnp.transpose(r["dw_in"].reshape(D_MODEL, N_DEV, N_SHARD), (1, 0, 2))
    dw_out_blocks = r["dw_out"].reshape(N_DEV, D_MODEL // N_DEV, D_MODEL)
    dcw_blocks = jnp.transpose(r["dconv_w"].reshape(CONV_WIDTH, N_DEV, 128), (1, 0, 2))
    pack = _pack_small(dict(loss=r["loss"][:, :1], final_g=r["final_g"], norm_g=r["norm_g"], conv_b=r["conv_b"],
                            ls_pool=r["ls_pool"], mh_norm_g=r["mh_norm_g"], b_gates=r["b_gates"][:, :8],
                            b_ada=r["dmod"], w_pool=r["w_pool"]))
    core = lax.axis_index("c").astype(jnp.int32).reshape(1)
    r_in, r_out, r_cw = _pair_swap(dw_in_blocks, dw_out_blocks, dcw_blocks)
    s_in, s_out, s_cw = _pair_add(core, dw_in_blocks, r_in, dw_out_blocks, r_out, dcw_blocks, r_cw)
    p_in, p_out, p_cw, p_pack = _chip_exchange(s_in, s_out, s_cw, pack)

    gi, di, mi, vi = _adam_sum("adam_w_in", p_in, w_in[0], m_w_in[0], v_w_in[0], 256)
    go, do_, mo, vo = _adam_sum("adam_w_out", p_out, w_out[0], m_w_out[0], v_w_out[0], 128)
    gc, dc, mc, vc = _adam_sum("adam_conv_w", p_cw, conv_w[0], m_conv_w[0], v_conv_w[0], CONV_WIDTH)

    def small(loss_like, fg_, ng_, cb_, ls_, mh_, bg_, ba_, wp_):
        return _pack_small(dict(loss=loss_like, final_g=fg_, norm_g=ng_, conv_b=cb_, ls_pool=ls_, mh_norm_g=mh_,
                                b_gates=bg_, b_ada=ba_, w_pool=wp_))

    zero = jnp.zeros((1, 1), F32)
    w_pack = small(zero, final_g, norm_g, conv_b, ls_pool, mh_norm_g, b_gates, b_ada, w_pool)
    m_pack = small(zero, m_final_g, m_norm_g, m_conv_b, m_ls_pool, m_mh_norm_g, m_b_gates, m_b_ada, m_w_pool)
    v_pack = small(zero, v_final_g, v_norm_g, v_conv_b, v_ls_pool, v_mh_norm_g, v_b_gates, v_b_ada, v_w_pool)
    gp, dp, mp, vp = _adam_sum("adam_small", p_pack, w_pack, m_pack, v_pack, PACK_TOTAL)

    off, rows = PACK_OFF["b_ada"]
    dmod_all = p_pack[:, off:off + rows, :].reshape(N_DEV, 3 * D_MODEL)
    dmod_blk16 = jnp.pad(lax.dynamic_slice(dmod_all, (0, me * ADA_SHARD), (N_DEV, ADA_SHARD)), ((0, 8), (0, 0)))
    ga, da, ma, va = _adam_ada(sc_all16, dmod_blk16, w_ada[0], m_w_ada[0], v_w_ada[0])

    names = ("norm_g", "w_ada", "b_ada", "w_in", "b_gates", "conv_w", "conv_b", "w_pool", "ls_pool", "mh_norm_g",
             "w_out", "final_g")
    shapes = dict(norm_g=norm_g.shape, b_ada=b_ada.shape, b_gates=b_gates.shape, conv_b=conv_b.shape,
                  w_pool=w_pool.shape, ls_pool=ls_pool.shape, mh_norm_g=mh_norm_g.shape, final_g=final_g.shape)
    sharded = dict(w_ada=(ga, da, ma, va), w_in=(gi, di, mi, vi), conv_w=(gc, dc, mc, vc), w_out=(go, do_, mo, vo))
    outs = []
    for kind in range(4):
        for nm in names:
            if nm in sharded:
                outs.append(sharded[nm][kind][None])
            else:
                outs.append(_unpack_small((gp, dp, mp, vp)[kind], nm, shapes[nm]))
    loss = gp[0, 0]
    grad_x = r["grad_x"].reshape(1, seq, D_MODEL)
    return (loss, grad_x, *outs)
```

```python
import jax
import jax.numpy as jnp
from jax import lax
from jax.experimental import pallas as pl
from jax.experimental.pallas import tpu as pltpu

F32 = jnp.float32
BF16 = jnp.bfloat16

D_MODEL = 1024
D_POOL = 512
D_MLSTM = 512
N_HEADS = 4
HEAD_DIM = 128
CHUNK = 128
POOL_WINDOWS = (2, 4, 8, 16)
POOL_GROUP_DIM = 128
CONV_WIDTH = 4
EPS = 1e-6
N_MAIN = 3584
N_IN = 3592
N_PAD = 3840
N_SHARD = N_IN // 8
ADA_SHARD = 3 * D_MODEL // 8
N_DEV = 8
CONV_HALO = 8
POOL_HALO = 16
NEG_BIG = -1e30
VMEM_LIMIT_BYTES = 56 * 1024 * 1024

ADAM_LR = 0.001
ADAM_B1 = 0.9
ADAM_B2 = 0.999
ADAM_EPS = 1e-08
ADAM_WD = 0.01
ADAM_STEP = 10

PACK_ROWS = (("loss", 8), ("final_g", 8), ("norm_g", 8), ("conv_b", 8), ("ls_pool", 8),
             ("mh_norm_g", 8), ("b_gates", 8), ("b_ada", 24), ("w_pool", 512))
PACK_TOTAL = sum(r for _, r in PACK_ROWS)


def _pack_offsets():
    off, out = 0, {}
    for name, rows in PACK_ROWS:
        out[name] = (off, rows)
        off += rows
    return out


PACK_OFF = _pack_offsets()


def _dot(a, b):
    return jnp.dot(a, b, preferred_element_type=F32)


def _dot_nt(a, b):
    return lax.dot_general(a, b, (((1,), (1,)), ((), ())), preferred_element_type=F32)


def _dot_tn(a, b):
    return lax.dot_general(a, b, (((0,), (0,)), ((), ())), preferred_element_type=F32)


def _dot_f32(a, b):
    return jnp.dot(a, b, precision=lax.Precision.HIGHEST, preferred_element_type=F32)


def _sigmoid(x):
    return jax.nn.sigmoid(x)


def _log_sigmoid(x):
    return jnp.minimum(x, 0.0) - jnp.log1p(jnp.exp(-jnp.abs(x)))


def _params(sem):
    return pltpu.CompilerParams(dimension_semantics=sem, vmem_limit_bytes=VMEM_LIMIT_BYTES)


def _full(shape):
    n = len(shape)
    return pl.BlockSpec(shape, lambda *_: (0,) * n)


def _mesh_pos():
    return lax.axis_index("x"), lax.axis_index("y"), lax.axis_index("c")


def _peer(k):
    x, y, c = _mesh_pos()
    px = 1 - x if (k >> 2) & 1 else x
    py = 1 - y if (k >> 1) & 1 else y
    pc = 1 - c if k & 1 else c
    return (px, py, pc), 4 * px + 2 * py + pc


def _remote(src, dst, send_sem, recv_sem, to):
    return pltpu.make_async_remote_copy(src_ref=src, dst_ref=dst, send_sem=send_sem, recv_sem=recv_sem, device_id=to,
                                        device_id_type=pl.DeviceIdType.MESH)


def _other_chips():
    x, y, _ = _mesh_pos()
    return [(1 - x, y), (x, 1 - y), (1 - x, 1 - y)]


def _two_level_gather(src, dst, send_sems, recv_sems, local_sems):
    n = len(src)
    x, y, c = _mesh_pos()
    me = 4 * x + 2 * y + c
    sibling = (x, y, 1 - c)
    chips = _other_chips()

    def copy(a, k, block, to, own):
        return _remote(src[a] if own else dst[a].at[block], dst[a].at[block], send_sems.at[a, k], recv_sems.at[a, k], to)

    local = [pltpu.make_async_copy(src[a], dst[a].at[me], local_sems.at[a]) for a in range(n)]
    first = [copy(a, 0, me, sibling, True) for a in range(n)]
    first += [copy(a, 1 + j, me, (*chip, c), True) for j, chip in enumerate(chips) for a in range(n)]
    for cp in local + first:
        cp.start()
    passed = []
    for j, (px, py) in enumerate(chips):
        block = 4 * px + 2 * py + c
        for a in range(n):
            copy(a, 1 + j, block, sibling, False).wait_recv()
            passed.append(copy(a, 4 + j, block, sibling, False))
            passed[-1].start()
    for a in range(n):
        copy(a, 0, 4 * x + 2 * y + (1 - c), sibling, False).wait_recv()
    for j, (px, py) in enumerate(chips):
        for a in range(n):
            copy(a, 4 + j, 4 * px + 2 * py + (1 - c), sibling, False).wait_recv()
    for cp in first + passed:
        cp.wait_send()
    for cp in local:
        cp.wait()


GATHER_COPIES = 7


def _all_gather(name, *shards):
    n = len(shards)

    def body(*refs):
        _two_level_gather(refs[:n], refs[n:2 * n], *refs[2 * n:])

    hbm = pl.BlockSpec(memory_space=pltpu.HBM)
    return pl.pallas_call(
        body, name=name,
        out_shape=tuple(jax.ShapeDtypeStruct((N_DEV,) + s.shape, s.dtype) for s in shards),
        in_specs=[hbm] * n, out_specs=tuple([hbm] * n),
        scratch_shapes=[pltpu.SemaphoreType.DMA((n, GATHER_COPIES)), pltpu.SemaphoreType.DMA((n, GATHER_COPIES)),
                        pltpu.SemaphoreType.DMA((n,))],
    )(*shards)


def _ada_mod(c_all16, w_ada_blk, b_ada_blk):
    def body(c_ref, w_ref, b_ref, out_ref, sc_ref, send_sems, recv_sems):
        x, y, c = _mesh_pos()
        me = 4 * x + 2 * y + c
        cv = c_ref[...]
        sc = cv * _sigmoid(cv)
        sc_ref[...] = sc
        blk = _dot(sc.astype(BF16), w_ref[...].astype(BF16)) + b_ref[...]
        out_ref[me] = blk[0:N_DEV, :]
        copies = []
        for k in range(1, N_DEV):
            peer, _ = _peer(k)
            copies.append(pltpu.make_async_remote_copy(
                src_ref=out_ref.at[me], dst_ref=out_ref.at[me], send_sem=send_sems.at[k - 1],
                recv_sem=recv_sems.at[k - 1], device_id=peer, device_id_type=pl.DeviceIdType.MESH))
        for cp in copies:
            cp.start()
        for cp in copies:
            cp.wait()

    vmem = pl.BlockSpec(memory_space=pltpu.VMEM)
    return pl.pallas_call(
        body, name="ada_mod",
        out_shape=(jax.ShapeDtypeStruct((N_DEV, N_DEV, ADA_SHARD), F32),
                   jax.ShapeDtypeStruct(c_all16.shape, F32)),
        in_specs=[vmem] * 3, out_specs=(vmem, vmem),
        scratch_shapes=[pltpu.SemaphoreType.DMA((N_DEV - 1,)), pltpu.SemaphoreType.DMA((N_DEV - 1,))],
    )(c_all16, w_ada_blk, b_ada_blk)


def _scatter_copies(src, land, send_sems, recv_sems):
    x, y, c = _mesh_pos()
    me = 4 * x + 2 * y + c
    copies = []
    for k in range(1, N_DEV):
        peer, p = _peer(k)
        for a in range(len(src)):
            i = a * (N_DEV - 1) + k - 1
            copies.append(_remote(src[a].at[p], land[a].at[me], send_sems.at[i], recv_sems.at[i], peer))
    return copies


def _scatter_start(name, blocks):
    n = len(blocks)

    def body(*refs):
        src, land = refs[:n], refs[n:2 * n]
        send_sems, recv_sems = refs[2 * n], refs[2 * n + 1]
        token_ref = refs[-1]
        for cp in _scatter_copies(src, land, send_sems, recv_sems):
            cp.start()
        token_ref[...] = jnp.zeros_like(token_ref)

    hbm = pl.BlockSpec(memory_space=pltpu.HBM)
    sem = pl.BlockSpec(memory_space=pltpu.SEMAPHORE)
    through = tuple(pltpu.HBM(b.shape, b.dtype) for b in blocks)
    args = [pltpu.with_memory_space_constraint(b, pltpu.HBM) for b in blocks]
    args += [pltpu.with_memory_space_constraint(lax.empty(b.shape, b.dtype), pltpu.HBM) for b in blocks]
    out = pl.pallas_call(
        body, name=name,
        out_shape=(pltpu.SemaphoreType.DMA((n * (N_DEV - 1),)),) * 2 + through + through
        + (jax.ShapeDtypeStruct((8, 128), F32),),
        in_specs=[hbm] * (2 * n),
        out_specs=(sem, sem) + (hbm,) * (2 * n) + (pl.BlockSpec(memory_space=pltpu.VMEM),),
        input_output_aliases={i: 2 + i for i in range(2 * n)},
        compiler_params=pltpu.CompilerParams(has_side_effects=pltpu.SideEffectType.DATAFLOW_SIDE_EFFECTING),
    )(*args)
    return out[:-1], out[-1][0:1, 0:1]


def _scatter_wait(name, state, after):
    n = (len(state) - 2) // 2
    send_sems, recv_sems = state[0], state[1]
    src, land = state[2:2 + n], state[2 + n:]

    def body(*refs):
        src_r, land_r = refs[:n], refs[n:2 * n]
        for cp in _scatter_copies(src_r, land_r, refs[2 * n], refs[2 * n + 1]):
            cp.wait_send()
            cp.wait_recv()

    hbm = pl.BlockSpec(memory_space=pltpu.HBM)
    sem = pl.BlockSpec(memory_space=pltpu.SEMAPHORE)
    out = pl.pallas_call(
        body, name=name,
        out_shape=tuple(pltpu.HBM(b.shape, b.dtype) for b in src + land),
        in_specs=[hbm] * (2 * n) + [sem, sem, pl.BlockSpec(memory_space=pl.ANY)],
        out_specs=(hbm,) * (2 * n),
        input_output_aliases={i: i for i in range(2 * n)},
        compiler_params=pltpu.CompilerParams(has_side_effects=pltpu.SideEffectType.DATAFLOW_SIDE_EFFECTING),
    )(*src, *land, send_sems, recv_sems, after)
    me = 4 * lax.axis_index("x") + 2 * lax.axis_index("y") + lax.axis_index("c")
    landed = []
    for a in range(n):
        own = lax.dynamic_index_in_dim(out[a], me, axis=0, keepdims=True)
        landed.append(lax.dynamic_update_slice_in_dim(out[n + a], own, me, axis=0))
    return landed


def _fwd_proj(x, norm_g, scale, shift, w_main, w_g):
    seq = x.shape[0]
    tm = min(512, seq)
    tn = 512

    def body(x_ref, ng_ref, sc_ref, sh_ref, w_ref, wg_ref, proj_ref, gates_ref, h_ref):
        xt = x_ref[...]
        r = lax.rsqrt(jnp.mean(xt * xt, axis=-1, keepdims=True) + EPS)
        h = ((xt * r) * ng_ref[...]) * (1.0 + sc_ref[...]) + sh_ref[...]
        hb = h.astype(BF16)
        h_ref[...] = hb
        gates_ref[...] = _dot(hb, wg_ref[...])
        for j in range(N_MAIN // tn):
            proj_ref[:, j * tn:(j + 1) * tn] = _dot(hb, w_ref[:, j * tn:(j + 1) * tn])

    vec = _full((1, D_MODEL))
    tile = pl.BlockSpec((tm, D_MODEL), lambda i: (i, 0))
    return pl.pallas_call(
        body, name="fwd_proj", grid=(seq // tm,),
        out_shape=(jax.ShapeDtypeStruct((seq, N_MAIN), F32), jax.ShapeDtypeStruct((seq, 128), F32),
                   jax.ShapeDtypeStruct((seq, D_MODEL), BF16)),
        in_specs=[tile, vec, vec, vec, _full((D_MODEL, N_MAIN)), _full((D_MODEL, 128))],
        out_specs=(pl.BlockSpec((tm, N_MAIN), lambda i: (i, 0)), pl.BlockSpec((tm, 128), lambda i: (i, 0)), tile),
        compiler_params=_params(("arbitrary",)),
    )(x, norm_g, scale, shift, w_main, w_g)


def _gate_forms(gpre):
    r = lax.broadcasted_iota(jnp.int32, (CHUNK, CHUNK), 0)
    c = lax.broadcasted_iota(jnp.int32, (CHUNK, CHUNK), 1)
    causal = c <= r
    ltri = jnp.where(causal, 1.0, 0.0).astype(F32)
    utri = jnp.where(r <= c, 1.0, 0.0).astype(F32)
    bcol = _dot_f32(ltri, _log_sigmoid(gpre))
    gt8 = gpre.T[0:8, :]
    brow = _dot_f32(_log_sigmoid(gt8), utri)
    return causal, utri, bcol, gt8, brow


def _in_lockstep(stages):
    alive = list(stages)
    while alive:
        still = []
        for g in alive:
            try:
                next(g)
                still.append(g)
            except StopIteration:
                pass
        alive = still


def _head_fwd(qh, kh, vh, bc, br, igr, m_prev, c_h, n_row, causal):
    qb, kb, vb, cb = qh.astype(BF16), kh.astype(BF16), vh.astype(BF16), c_h.astype(BF16)
    qk = _dot_nt(qb, kb)
    cq = _dot_nt(qb, cb)
    yield
    dlog = jnp.where(causal, bc - br + igr, NEG_BIG)
    inter_log = bc + m_prev
    m_t = jnp.maximum(inter_log, jnp.max(dlog, axis=-1, keepdims=True))
    yield
    dmat = jnp.exp(dlog - m_t)
    inter = jnp.exp(inter_log - m_t)
    s = qk * dmat
    sv = _dot(s.astype(BF16), vb)
    yield
    nq = jnp.sum(qh * n_row, axis=-1, keepdims=True)
    den = jnp.sum(s, axis=-1, keepdims=True) + inter * nq
    emt = jnp.exp(-m_t)
    yield
    num = sv + inter * cq
    dn = jnp.maximum(jnp.abs(den), emt)
    hm = num / dn
    return dict(dmat=dmat, inter=inter, qb=qb, kb=kb, vb=vb, cb=cb, s=s, cq=cq, nq=nq, den=den, emt=emt,
                dn=dn, hm=hm)


def _state_weights(bc, igc, m_prev, m_new=None):
    last = lax.broadcasted_iota(jnp.int32, (CHUNK, 1), 0) == CHUNK - 1
    b_last = jnp.sum(jnp.where(last, bc, 0.0), axis=0, keepdims=True)
    wlog = b_last - bc + igc
    if m_new is None:
        m_new = jnp.maximum(b_last + m_prev, jnp.max(wlog, axis=0, keepdims=True))
    w_c = jnp.exp(wlog - m_new)
    decay = jnp.exp(b_last + m_prev - m_new)
    return w_c, decay, m_new, last


def _rows_back(x, k):
    return x if k == 0 else pltpu.roll(x, k, 0)


def _rows_ahead(x, k):
    return x if k == 0 else pltpu.roll(x, x.shape[0] - k, 0)


def _conv_taps(xpad):
    return [_rows_back(xpad, CONV_WIDTH - 1 - j)[CONV_HALO:, :] for j in range(CONV_WIDTH)]


def _conv_pre(taps, cw_ref, cb_ref):
    a = cb_ref[...]
    for j in range(CONV_WIDTH):
        a = a + cw_ref[j:j + 1, :] * taps[j]
    return a


def _window_sum(x, w, shift):
    k = 1
    while k < w:
        x = x + shift(x, k)
        k *= 2
    return x


def _pool_window_sum(upad_ref, g, w):
    lanes = slice(g * POOL_GROUP_DIM, (g + 1) * POOL_GROUP_DIM)
    return _window_sum(upad_ref[:, lanes], w, _rows_back)[POOL_HALO:, :]


def _pool_inv_count(chunk_idx, w):
    pos = chunk_idx * CHUNK + lax.broadcasted_iota(jnp.int32, (CHUNK, 1), 0) + 1
    return 1.0 / jnp.minimum(pos, w).astype(F32)


def _mixer_in_specs(cmap, n_chunks):
    def rows(i):
        return cmap(i)
    return [
        pl.BlockSpec((CHUNK, 1024), lambda i: (rows(i), 0)),
        pl.BlockSpec((CHUNK, 1024), lambda i: (rows(i), 1)),
        pl.BlockSpec((CHUNK, 512), lambda i: (rows(i), 4)),
        pl.BlockSpec((CHUNK, 512), lambda i: (rows(i), 5)),
        pl.BlockSpec((CHUNK, 512), lambda i: (rows(i), 6)),
        pl.BlockSpec((POOL_HALO, 512), lambda i: (jnp.maximum(rows(i) * (CHUNK // POOL_HALO) - 1, 0), 0)),
        pl.BlockSpec((CONV_HALO, 1024), lambda i: (jnp.maximum(rows(i) * (CHUNK // CONV_HALO) - 1, 0), 1)),
    ]


def _mix_fwd(proj, gates, bg_pad, conv_w8, conv_b, w_pool, ls_pool, mh_g):
    seq = proj.shape[0]
    n_chunks = seq // CHUNK

    def body(uz_ref, qk_ref, v_ref, o_ref, zm_ref, uh_ref, qkh_ref, g_ref, bg_ref, cw_ref, cb_ref, wp_ref,
             ls_ref, mhg_ref, mix_ref, cst_ref, nst_ref, mst_ref, c_scr, n_scr, m_scr, xpad, upad):
        i = pl.program_id(0)

        @pl.when(i == 0)
        def _():
            c_scr[...] = jnp.zeros_like(c_scr)
            n_scr[...] = jnp.zeros_like(n_scr)
            m_scr[...] = jnp.zeros_like(m_scr)

        cst_ref[0] = c_scr[...]
        nst_ref[0] = n_scr[...]
        mst_ref[0] = m_scr[...]
        first = i == 0

        upad[0:POOL_HALO, :] = jnp.where(first, 0.0, uh_ref[...])
        upad[POOL_HALO:POOL_HALO + CHUNK, :] = uz_ref[:, 0:D_POOL]
        for g, w in enumerate(POOL_WINDOWS):
            lanes = slice(g * POOL_GROUP_DIM, (g + 1) * POOL_GROUP_DIM)
            pooled = _pool_window_sum(upad, g, w) * _pool_inv_count(i, w) - uz_ref[:, lanes]
            y = _dot(pooled.astype(BF16), wp_ref[g].astype(BF16)) * ls_ref[:, lanes]
            zp = uz_ref[:, D_POOL + g * POOL_GROUP_DIM:D_POOL + (g + 1) * POOL_GROUP_DIM]
            mix_ref[:, lanes] = (y * (zp * _sigmoid(zp))).astype(BF16)

        xpad[0:CONV_HALO, :] = jnp.where(first, 0.0, qkh_ref[...])
        xpad[CONV_HALO:CONV_HALO + CHUNK, :] = qk_ref[...]
        a = _conv_pre(_conv_taps(xpad[...]), cw_ref, cb_ref)
        qk = a * _sigmoid(a)

        gpre = g_ref[...] + bg_ref[...]
        causal, _, bcol, gt8, brow = _gate_forms(gpre)
        def head(h):
            lanes = slice(h * HEAD_DIM, (h + 1) * HEAD_DIM)
            qh = qk[:, lanes]
            kh = qk[:, D_MLSTM + h * HEAD_DIM:D_MLSTM + (h + 1) * HEAD_DIM] * (HEAD_DIM ** -0.5)
            vh = v_ref[:, lanes]
            bc = bcol[:, N_HEADS + h:N_HEADS + h + 1]
            br = brow[N_HEADS + h:N_HEADS + h + 1, :]
            igr = gt8[h:h + 1, :]
            igc = gpre[:, h:h + 1]
            m_prev = m_scr[h:h + 1, 0:1]
            c_h = c_scr[h]
            n_row = n_scr[h:h + 1, :]
            w_c, decay, m_new, _ = _state_weights(bc, igc, m_prev)
            c_scr[h] = decay * c_h + _dot_tn((vh * w_c).astype(BF16), kh.astype(BF16))
            n_scr[h:h + 1, :] = decay * n_row + jnp.sum(w_c * kh, axis=0, keepdims=True)
            m_scr[h:h + 1, :] = jnp.broadcast_to(m_new, (1, 128))
            f = yield from _head_fwd(qh, kh, vh, bc, br, igr, m_prev, c_h, n_row, causal)
            yield
            hm = f["hm"]
            hn = hm * lax.rsqrt(jnp.mean(hm * hm, axis=-1, keepdims=True) + EPS) * mhg_ref[:, lanes]
            zm = zm_ref[:, lanes]
            out = hn * _sigmoid(o_ref[:, lanes]) * (zm * _sigmoid(zm))
            mix_ref[:, D_POOL + h * HEAD_DIM:D_POOL + (h + 1) * HEAD_DIM] = out.astype(BF16)

        _in_lockstep(head(h) for h in range(N_HEADS))

    cmap = lambda i: i
    in_specs = _mixer_in_specs(cmap, n_chunks) + [
        pl.BlockSpec((CHUNK, 128), lambda i: (i, 0)),
        _full((1, 128)), _full((8, 1024)), _full((1, 1024)), _full((4, 128, 128)), _full((1, 512)),
        _full((1, 512))]
    return pl.pallas_call(
        body, name="mix_fwd", grid=(n_chunks,),
        out_shape=(jax.ShapeDtypeStruct((seq, D_MODEL), BF16),
                   jax.ShapeDtypeStruct((n_chunks, N_HEADS, HEAD_DIM, HEAD_DIM), F32),
                   jax.ShapeDtypeStruct((n_chunks, 8, 128), F32),
                   jax.ShapeDtypeStruct((n_chunks, 8, 128), F32)),
        in_specs=in_specs,
        out_specs=(pl.BlockSpec((CHUNK, D_MODEL), lambda i: (i, 0)),
                   pl.BlockSpec((1, N_HEADS, HEAD_DIM, HEAD_DIM), lambda i: (i, 0, 0, 0)),
                   pl.BlockSpec((1, 8, 128), lambda i: (i, 0, 0)),
                   pl.BlockSpec((1, 8, 128), lambda i: (i, 0, 0))),
        scratch_shapes=[pltpu.VMEM((N_HEADS, HEAD_DIM, HEAD_DIM), F32), pltpu.VMEM((8, 128), F32),
                        pltpu.VMEM((8, 128), F32), pltpu.VMEM((CONV_HALO + CHUNK, 1024), F32),
                        pltpu.VMEM((POOL_HALO + CHUNK, D_POOL), F32)],
        compiler_params=_params(("arbitrary",)),
    )(proj, proj, proj, proj, proj, proj, proj, gates, bg_pad, conv_w8, conv_b, w_pool, ls_pool, mh_g)


def _out_fwd_bwd(mix, x, tgt, w_out_b, gate, final_g):
    seq = x.shape[0]
    tm = min(256, seq)

    def body(mix_ref, x_ref, t_ref, w_ref, gate_ref, fg_ref, dx2_ref, dmix_ref, dwo_ref, dgate_ref, dfg_ref,
             loss_ref):
        @pl.when(pl.program_id(0) == 0)
        def _():
            dwo_ref[...] = jnp.zeros_like(dwo_ref)
            dgate_ref[...] = jnp.zeros_like(dgate_ref)
            dfg_ref[...] = jnp.zeros_like(dfg_ref)
            loss_ref[...] = jnp.zeros_like(loss_ref)

        mixb = mix_ref[...]
        w = w_ref[...]
        gate_v = gate_ref[...]
        fg = fg_ref[...]
        o2 = _dot(mixb, w)
        x2 = x_ref[...] + gate_v * o2
        r2 = lax.rsqrt(jnp.mean(x2 * x2, axis=-1, keepdims=True) + EPS)
        x2n = x2 * r2
        err = x2n * fg - t_ref[...]
        part = 0.5 * jnp.sum(jnp.sum(err * err, axis=-1, keepdims=True), axis=0, keepdims=True) / D_MODEL
        loss_ref[...] += jnp.broadcast_to(part, loss_ref.shape)
        dy = err / D_MODEL
        dfg_ref[...] += jnp.sum(dy * x2n, axis=0, keepdims=True)
        gdy = dy * fg
        dx2 = r2 * (gdy - x2n * jnp.mean(gdy * x2n, axis=-1, keepdims=True))
        dx2_ref[...] = dx2
        dgate_ref[...] += jnp.sum(dx2 * o2, axis=0, keepdims=True)
        do2 = (dx2 * gate_v).astype(BF16)
        dmix_ref[...] = _dot_nt(do2, w)
        dwo_ref[...] += _dot_tn(mixb, do2)

    tile = pl.BlockSpec((tm, D_MODEL), lambda i: (i, 0))
    vec = _full((1, D_MODEL))
    return pl.pallas_call(
        body, name="out_fwd_bwd", grid=(seq // tm,),
        out_shape=(jax.ShapeDtypeStruct((seq, D_MODEL), F32), jax.ShapeDtypeStruct((seq, D_MODEL), F32),
                   jax.ShapeDtypeStruct((D_MODEL, D_MODEL), F32), jax.ShapeDtypeStruct((1, D_MODEL), F32),
                   jax.ShapeDtypeStruct((1, D_MODEL), F32), jax.ShapeDtypeStruct((1, 128), F32)),
        in_specs=[tile, tile, tile, _full((D_MODEL, D_MODEL)), vec, vec],
        out_specs=(tile, tile, _full((D_MODEL, D_MODEL)), vec, vec, _full((1, 128))),
        compiler_params=_params(("arbitrary",)),
    )(mix, x, tgt, w_out_b, gate, final_g)


def _mix_bwd(proj, gates, dmix, cst, nst, mst, bg_pad, conv_w8, conv_b, w_pool, ls_pool, mh_g):
    seq = proj.shape[0]
    n_chunks = seq // CHUNK

    def body(uz_ref, qk_ref, v_ref, o_ref, zm_ref, uh_ref, qkh_ref, g_ref, dmix_ref, cst_ref, nst_ref, mst_ref,
             mnx_ref, bg_ref, cw_ref, cb_ref, wp_ref, ls_ref, mhg_ref,
             dp_ref, dcw_ref, dcb_ref, dwp_ref, dls_ref, dmhg_ref, dbg_ref,
             dc_scr, dn_scr, xpad, upad, dapad, dpipad):
        i = pl.program_id(0)
        ci = n_chunks - 1 - i

        @pl.when(i == 0)
        def _():
            for ref in (dc_scr, dn_scr, dcw_ref, dcb_ref, dwp_ref, dls_ref, dmhg_ref, dbg_ref):
                ref[...] = jnp.zeros_like(ref)
            dapad[CHUNK:CHUNK + CONV_HALO, :] = jnp.zeros((CONV_HALO, 1024), F32)
            dpipad[CHUNK:CHUNK + POOL_HALO, :] = jnp.zeros((POOL_HALO, D_POOL), F32)

        first = ci == 0

        upad[0:POOL_HALO, :] = jnp.where(first, 0.0, uh_ref[...])
        upad[POOL_HALO:POOL_HALO + CHUNK, :] = uz_ref[:, 0:D_POOL]
        dpooled = []
        for g, w in enumerate(POOL_WINDOWS):
            lanes = slice(g * POOL_GROUP_DIM, (g + 1) * POOL_GROUP_DIM)
            zlanes = slice(D_POOL + g * POOL_GROUP_DIM, D_POOL + (g + 1) * POOL_GROUP_DIM)
            inv = _pool_inv_count(ci, w)
            pooled = _pool_window_sum(upad, g, w) * inv - uz_ref[:, lanes]
            pb = pooled.astype(BF16)
            wpb = wp_ref[g].astype(BF16)
            yw = _dot(pb, wpb)
            ls = ls_ref[:, lanes]
            zp = uz_ref[:, zlanes]
            sg = _sigmoid(zp)
            dpo = dmix_ref[:, lanes]
            dp_ref[:, zlanes] = (dpo * (yw * ls) * (sg * (1.0 + zp * (1.0 - sg)))).astype(BF16)
            dy = dpo * (zp * sg)
            dls_ref[:, lanes] += jnp.sum(dy * yw, axis=0, keepdims=True)
            dyw = (dy * ls).astype(BF16)
            dwp_ref[g] += _dot_tn(pb, dyw)
            dpl = _dot_nt(dyw, wpb)
            dpooled.append(dpl)
            dpipad[0:CHUNK, lanes] = dpl * inv
        for g, w in enumerate(POOL_WINDOWS):
            lanes = slice(g * POOL_GROUP_DIM, (g + 1) * POOL_GROUP_DIM)
            du = _window_sum(dpipad[:, lanes], w, _rows_ahead)[0:CHUNK, :] - dpooled[g]
            dp_ref[:, lanes] = du.astype(BF16)
        dpipad[CHUNK:CHUNK + POOL_HALO, :] = dpipad[0:POOL_HALO, :]

        xpad[0:CONV_HALO, :] = jnp.where(first, 0.0, qkh_ref[...])
        xpad[CONV_HALO:CONV_HALO + CHUNK, :] = qk_ref[...]
        taps = _conv_taps(xpad[...])
        a = _conv_pre(taps, cw_ref, cb_ref)
        sga = _sigmoid(a)
        qk = a * sga
        dsilu_a = sga * (1.0 + a * (1.0 - sga))

        gpre = g_ref[...] + bg_ref[...]
        causal, utri, bcol, gt8, brow = _gate_forms(gpre)
        lane = lax.broadcasted_iota(jnp.int32, (CHUNK, 128), 1)
        row = lax.broadcasted_iota(jnp.int32, (CHUNK, 128), 0)
        col_g_rows, dig_parts, db_parts = [], [], []
        scale_k = HEAD_DIM ** -0.5

        def head(h):
            lanes = slice(h * HEAD_DIM, (h + 1) * HEAD_DIM)
            klanes = slice(D_MLSTM + h * HEAD_DIM, D_MLSTM + (h + 1) * HEAD_DIM)
            qh = qk[:, lanes]
            kh = qk[:, klanes] * scale_k
            vh = v_ref[:, lanes]
            bc = bcol[:, N_HEADS + h:N_HEADS + h + 1]
            br = brow[N_HEADS + h:N_HEADS + h + 1, :]
            igr = gt8[h:h + 1, :]
            igc = gpre[:, h:h + 1]
            m_prev = mst_ref[0, h:h + 1, 0:1]
            m_next = mnx_ref[0, h:h + 1, 0:1]
            c_h = cst_ref[0, h]
            n_row = nst_ref[0, h:h + 1, :]
            w_c, decay, _, last = _state_weights(bc, igc, m_prev, m_next)
            dcn = dc_scr[h]
            dnn = dn_scr[h:h + 1, :]
            dcnb = dcn.astype(BF16)
            vb0, kb0 = vh.astype(BF16), kh.astype(BF16)
            amat = _dot(vb0, dcnb) + dnn
            kdc = _dot_nt(kb0, dcnb)
            ddecay = (jnp.sum(jnp.sum(dcn * c_h, axis=-1, keepdims=True), axis=0, keepdims=True)
                      + jnp.sum(dnn * n_row, axis=-1, keepdims=True))
            f = yield from _head_fwd(qh, kh, vh, bc, br, igr, m_prev, c_h, n_row, causal)
            qb, kb, vb, cb = f["qb"], f["kb"], f["vb"], f["cb"]
            s, dmat, inter, den, dn, hm = f["s"], f["dmat"], f["inter"], f["den"], f["dn"], f["hm"]
            yield

            rinv = lax.rsqrt(jnp.mean(hm * hm, axis=-1, keepdims=True) + EPS)
            hmn = hm * rinv
            gh = mhg_ref[:, lanes]
            o_pre = o_ref[:, lanes]
            og = _sigmoid(o_pre)
            zm = zm_ref[:, lanes]
            sgz = _sigmoid(zm)
            sz = zm * sgz
            dout = dmix_ref[:, D_POOL + h * HEAD_DIM:D_POOL + (h + 1) * HEAD_DIM]
            hn = hmn * gh
            dp_ref[:, 2560 + h * HEAD_DIM:2560 + (h + 1) * HEAD_DIM] = (
                dout * hn * sz * og * (1.0 - og)).astype(BF16)
            dp_ref[:, 3072 + h * HEAD_DIM:3072 + (h + 1) * HEAD_DIM] = (
                dout * hn * og * (sgz * (1.0 + zm * (1.0 - sgz)))).astype(BF16)
            dhn = dout * og * sz
            dmhg_ref[:, lanes] += jnp.sum(dhn * hmn, axis=0, keepdims=True)
            dyn = dhn * gh
            dhm = rinv * (dyn - hmn * jnp.mean(dyn * hmn, axis=-1, keepdims=True))
            yield

            inv_dn = 1.0 / dn
            dnum = dhm * inv_dn
            hd = jnp.sum(dhm * hm, axis=-1, keepdims=True)
            dden = jnp.where(jnp.abs(den) > f["emt"], -hd * inv_dn * jnp.sign(den), 0.0)
            dnb = dnum.astype(BF16)
            dnv = _dot_nt(dnb, vb)
            dv = _dot_tn(s.astype(BF16), dnb)
            dnc = _dot(dnb, cb)
            dc_prev = _dot_tn((inter * dnum).astype(BF16), qb)
            yield
            ds = dnv + dden
            dqk = (ds * dmat).astype(BF16)
            dqk_k = _dot(dqk, kb)
            dk = _dot_tn(dqk, qb)
            yield
            gmat = ds * s
            row_g = jnp.sum(gmat, axis=-1, keepdims=True)
            col_g_rows.append(jnp.where(row == h, jnp.sum(gmat, axis=0, keepdims=True), 0.0))
            gcol = inter * (jnp.sum(dnum * f["cq"], axis=-1, keepdims=True) + dden * f["nq"])
            dn_prev = jnp.sum((inter * dden) * qh, axis=0, keepdims=True)
            dw = jnp.sum(amat * kh, axis=-1, keepdims=True)
            e = dw * w_c
            db_last = ddecay * decay + jnp.sum(e, axis=0, keepdims=True)
            dig_parts.append(jnp.where(lane == h, e, 0.0))
            db_parts.append(jnp.where(lane == N_HEADS + h, row_g + gcol - e + jnp.where(last, db_last, 0.0), 0.0))
            dc_scr[h] = decay * dcn + dc_prev
            dn_scr[h:h + 1, :] = decay * dnn + dn_prev
            yield
            dq = dqk_k + inter * (dnc + dden * n_row)
            dp_ref[:, 2048 + h * HEAD_DIM:2048 + (h + 1) * HEAD_DIM] = (dv + w_c * kdc).astype(BF16)
            dapad[0:CHUNK, lanes] = dq * dsilu_a[:, lanes]
            dapad[0:CHUNK, klanes] = (dk + w_c * amat) * scale_k * dsilu_a[:, klanes]

        _in_lockstep(head(h) for h in range(N_HEADS))

        cs_t = sum(col_g_rows[1:], col_g_rows[0]).T
        dig_all = sum(dig_parts[1:], dig_parts[0]) + cs_t
        db_cols = sum(db_parts[1:], db_parts[0])
        shifted = jnp.zeros((CHUNK, 128), F32)
        for h in range(N_HEADS):
            shifted = shifted + jnp.where(lane == N_HEADS + h, cs_t[:, h:h + 1], 0.0)
        dlf = _dot_f32(utri, db_cols - shifted)
        dgates = dig_all + dlf * _sigmoid(-gpre)
        dp_ref[:, N_MAIN:N_MAIN + 128] = dgates.astype(BF16)
        dp_ref[:, N_MAIN + 128:N_PAD] = jnp.zeros((CHUNK, N_PAD - N_MAIN - 128), BF16)
        dbg_ref[...] += jnp.sum(dgates, axis=0, keepdims=True)

        da_pad = dapad[...]
        da = da_pad[0:CHUNK, :]
        dcb_ref[...] += jnp.sum(da, axis=0, keepdims=True)
        dx = jnp.zeros((CHUNK, 1024), F32)
        for j in range(CONV_WIDTH):
            dcw_ref[j:j + 1, :] += jnp.sum(da * taps[j], axis=0, keepdims=True)
            dx = dx + cw_ref[j:j + 1, :] * _rows_ahead(da_pad, CONV_WIDTH - 1 - j)[0:CHUNK, :]
        dp_ref[:, 1024:2048] = dx.astype(BF16)
        dapad[CHUNK:CHUNK + CONV_HALO, :] = dapad[0:CONV_HALO, :]

    cmap = lambda i: n_chunks - 1 - i
    in_specs = _mixer_in_specs(cmap, n_chunks) + [
        pl.BlockSpec((CHUNK, 128), lambda i: (cmap(i), 0)),
        pl.BlockSpec((CHUNK, D_MODEL), lambda i: (cmap(i), 0)),
        pl.BlockSpec((1, N_HEADS, HEAD_DIM, HEAD_DIM), lambda i: (cmap(i), 0, 0, 0)),
        pl.BlockSpec((1, 8, 128), lambda i: (cmap(i), 0, 0)),
        pl.BlockSpec((1, 8, 128), lambda i: (cmap(i), 0, 0)),
        pl.BlockSpec((1, 8, 128), lambda i: (jnp.minimum(cmap(i) + 1, n_chunks - 1), 0, 0)),
        _full((1, 128)), _full((8, 1024)), _full((1, 1024)), _full((4, 128, 128)), _full((1, 512)),
        _full((1, 512))]
    return pl.pallas_call(
        body, name="mix_bwd", grid=(n_chunks,),
        out_shape=(jax.ShapeDtypeStruct((seq, N_PAD), BF16), jax.ShapeDtypeStruct((8, 1024), F32),
                   jax.ShapeDtypeStruct((1, 1024), F32), jax.ShapeDtypeStruct((4, 128, 128), F32),
                   jax.ShapeDtypeStruct((1, 512), F32), jax.ShapeDtypeStruct((1, 512), F32),
                   jax.ShapeDtypeStruct((1, 128), F32)),
        in_specs=in_specs,
        out_specs=(pl.BlockSpec((CHUNK, N_PAD), lambda i: (cmap(i), 0)), _full((8, 1024)), _full((1, 1024)),
                   _full((4, 128, 128)), _full((1, 512)), _full((1, 512)), _full((1, 128))),
        scratch_shapes=[pltpu.VMEM((N_HEADS, HEAD_DIM, HEAD_DIM), F32), pltpu.VMEM((8, 128), F32),
                        pltpu.VMEM((CONV_HALO + CHUNK, 1024), F32), pltpu.VMEM((POOL_HALO + CHUNK, D_POOL), F32),
                        pltpu.VMEM((CHUNK + CONV_HALO, 1024), F32), pltpu.VMEM((CHUNK + POOL_HALO, D_POOL), F32)],
        compiler_params=_params(("arbitrary",)),
    )(proj, proj, proj, proj, proj, proj, proj, gates, dmix, cst, nst, mst, mst, bg_pad, conv_w8, conv_b,
      w_pool, ls_pool, mh_g)


def _bwd_in(dproj, w_in_t, x, dx2, norm_g, scale):
    seq = x.shape[0]
    tm = min(512, seq)

    def body(dp_ref, wt_ref, x_ref, dx2_ref, ng_ref, sc_ref, gx_ref, dsh_ref, dsc_ref, dng_ref):
        @pl.when(pl.program_id(0) == 0)
        def _():
            dsh_ref[...] = jnp.zeros_like(dsh_ref)
            dsc_ref[...] = jnp.zeros_like(dsc_ref)
            dng_ref[...] = jnp.zeros_like(dng_ref)

        dh = _dot(dp_ref[...], wt_ref[...])
        xt = x_ref[...]
        r = lax.rsqrt(jnp.mean(xt * xt, axis=-1, keepdims=True) + EPS)
        xn = xt * r
        ng = ng_ref[...]
        one_sc = 1.0 + sc_ref[...]
        dsh_ref[...] += jnp.sum(dh, axis=0, keepdims=True)
        dhxn = dh * xn
        dsc_ref[...] += jnp.sum(dhxn * ng, axis=0, keepdims=True)
        dng_ref[...] += jnp.sum(dhxn * one_sc, axis=0, keepdims=True)
        dxn = dh * (ng * one_sc)
        gx_ref[...] = r * (dxn - xn * jnp.mean(dxn * xn, axis=-1, keepdims=True)) + dx2_ref[...]

    tile = pl.BlockSpec((tm, D_MODEL), lambda i: (i, 0))
    vec = _full((1, D_MODEL))
    return pl.pallas_call(
        body, name="bwd_in", grid=(seq // tm,),
        out_shape=(jax.ShapeDtypeStruct((seq, D_MODEL), F32),) + (jax.ShapeDtypeStruct((1, D_MODEL), F32),) * 3,
        in_specs=[pl.BlockSpec((tm, N_PAD), lambda i: (i, 0)), _full((N_PAD, D_MODEL)), tile, tile, vec, vec],
        out_specs=(tile, vec, vec, vec),
        compiler_params=_params(("arbitrary",)),
    )(dproj, w_in_t, x, dx2, norm_g, scale)


def _dw_in(h_b, dproj):
    seq = h_b.shape[0]
    tk = min(1024, seq)
    tn = 768

    def body(h_ref, dp_ref, dw_ref):
        @pl.when(pl.program_id(1) == 0)
        def _():
            dw_ref[...] = jnp.zeros_like(dw_ref)

        dw_ref[...] += _dot_tn(h_ref[...], dp_ref[...])

    return pl.pallas_call(
        body, name="dw_in", grid=(N_PAD // tn, seq // tk),
        out_shape=jax.ShapeDtypeStruct((D_MODEL, N_PAD), F32),
        in_specs=[pl.BlockSpec((tk, D_MODEL), lambda j, t: (t, 0)), pl.BlockSpec((tk, tn), lambda j, t: (t, j))],
        out_specs=pl.BlockSpec((D_MODEL, tn), lambda j, t: (0, j)),
        compiler_params=_params(("arbitrary", "arbitrary")),
    )(h_b, dproj)


def _adam_update(g, w, m, v, g_ref, d_ref, m_ref, v_ref):
    mn = ADAM_B1 * m + (1.0 - ADAM_B1) * g
    vn = ADAM_B2 * v + (1.0 - ADAM_B2) * (g * g)
    m_hat = mn / (1.0 - ADAM_B1 ** ADAM_STEP)
    v_hat = vn / (1.0 - ADAM_B2 ** ADAM_STEP)
    g_ref[...] = g
    d_ref[...] = -ADAM_LR * (m_hat / (jnp.sqrt(v_hat) + ADAM_EPS) + ADAM_WD * w)
    m_ref[...] = mn
    v_ref[...] = vn


def _adam_sum(name, parts, w, m, v, row_tile):
    rows, cols = w.shape
    n_parts = parts.shape[0]

    def body(p_ref, w_ref, m_ref, v_ref, g_out, d_out, m_out, v_out):
        g = p_ref[0].astype(F32)
        for j in range(1, n_parts):
            g = g + p_ref[j].astype(F32)
        _adam_update(g, w_ref[...], m_ref[...], v_ref[...], g_out, d_out, m_out, v_out)

    tile = pl.BlockSpec((row_tile, cols), lambda i: (i, 0))
    return pl.pallas_call(
        body, name=name, grid=(rows // row_tile,),
        out_shape=(jax.ShapeDtypeStruct((rows, cols), F32),) * 4,
        in_specs=[pl.BlockSpec((n_parts, row_tile, cols), lambda i: (0, i, 0)), tile, tile, tile],
        out_specs=(tile,) * 4,
        compiler_params=_params(("arbitrary",)),
    )(parts, w, m, v)


def _adam_ada(sc_all16, dmod_blk16, w, m, v):
    rows, cols = w.shape

    def body(sc_ref, dm_ref, w_ref, m_ref, v_ref, g_out, d_out, m_out, v_out):
        g = _dot_tn(sc_ref[...].astype(BF16), dm_ref[...].astype(BF16))
        _adam_update(g, w_ref[...], m_ref[...], v_ref[...], g_out, d_out, m_out, v_out)

    return pl.pallas_call(
        body, name="adam_w_ada", grid=(1,),
        out_shape=(jax.ShapeDtypeStruct((rows, cols), F32),) * 4,
        in_specs=[_full(sc_all16.shape), _full(dmod_blk16.shape)] + [_full((rows, cols))] * 3,
        out_specs=(_full((rows, cols)),) * 4,
        compiler_params=_params(("arbitrary",)),
    )(sc_all16, dmod_blk16, w, m, v)


def _pack_small(pieces):
    rows = []
    for name, n in PACK_ROWS:
        a = pieces[name].reshape(-1).astype(F32)
        a = jnp.pad(a, (0, n * 128 - a.shape[0]))
        rows.append(a.reshape(n, 128))
    return jnp.concatenate(rows, axis=0)


def _unpack_small(pack, name, shape):
    off, _ = PACK_OFF[name]
    size = 1
    for s in shape:
        size *= s
    n_rows = -(-size // 128)
    return pack[off:off + n_rows].reshape(-1)[:size].reshape(shape)


def _local_step(x2, tgt2, shift, scale, gate, norm_g, w_full_b, w_out_b, conv_w, conv_b, w_pool, ls_pool,
                mh_norm_g, b_gates, final_g, send_dw_out=None, send_dw_in=None):
    w_main = w_full_b[:, :N_MAIN]
    w_g = jnp.pad(w_full_b[:, N_MAIN:], ((0, 0), (0, 128 - (N_IN - N_MAIN))))
    w_in_t = jnp.pad(w_full_b.T, ((0, N_PAD - N_IN), (0, 0)))
    bg_pad = jnp.pad(b_gates, ((0, 0), (0, 128 - b_gates.shape[1])))
    conv_w8 = jnp.pad(conv_w, ((0, 8 - CONV_WIDTH), (0, 0)))
    fg = final_g.reshape(1, D_MODEL)

    proj, gates, h_b = _fwd_proj(x2, norm_g, scale, shift, w_main, w_g)
    mix, cst, nst, mst = _mix_fwd(proj, gates, bg_pad, conv_w8, conv_b, w_pool, ls_pool, mh_norm_g)
    dx2, dmix, dwo, dgate, dfg, loss = _out_fwd_bwd(mix, x2, tgt2, w_out_b, gate, fg)
    if send_dw_out is not None:
        bg_pad = bg_pad + send_dw_out(dwo)
    dproj, dcw8, dcb, dwp, dls, dmhg, dbg = _mix_bwd(proj, gates, dmix, cst, nst, mst, bg_pad, conv_w8, conv_b,
                                                      w_pool, ls_pool, mh_norm_g)
    dw_in = _dw_in(h_b, dproj)
    ng_in = norm_g
    if send_dw_in is not None:
        ng_in = norm_g + send_dw_in(dw_in[:, :N_IN], dcw8[:CONV_WIDTH])
    gx, dsh, dsc, dng = _bwd_in(dproj, w_in_t, x2, dx2, ng_in, scale)
    return dict(loss=loss, grad_x=gx, dw_in=dw_in[:, :N_IN], dw_out=dwo, dconv_w=dcw8[:CONV_WIDTH], conv_b=dcb,
                w_pool=dwp, ls_pool=dls, mh_norm_g=dmhg, b_gates=dbg, final_g=dfg, norm_g=dng,
                dmod=jnp.concatenate([dsh, dsc, dgate], axis=1))


def kernel(x, c, norm_g, w_ada, b_ada, w_in, b_gates, conv_w, conv_b, w_pool, ls_pool, mh_norm_g, w_out, final_g, loss_target, m_norm_g, m_w_ada, m_b_ada, m_w_in, m_b_gates, m_conv_w, m_conv_b, m_w_pool, m_ls_pool, m_mh_norm_g, m_w_out, m_final_g, v_norm_g, v_w_ada, v_b_ada, v_w_in, v_b_gates, v_conv_w, v_conv_b, v_w_pool, v_ls_pool, v_mh_norm_g, v_w_out, v_final_g):
    seq = x.shape[1]
    me = 4 * lax.axis_index("x") + 2 * lax.axis_index("y") + lax.axis_index("c")

    g_in, g_out, g_cw, g_c = _all_gather("gather_weights", w_in[0].astype(BF16), w_out[0].astype(BF16), conv_w[0], c)
    w_full_b = jnp.transpose(g_in, (1, 0, 2)).reshape(D_MODEL, N_IN)
    w_out_b = g_out.reshape(D_MODEL, D_MODEL)
    conv_w_full = jnp.transpose(g_cw, (1, 0, 2)).reshape(CONV_WIDTH, 2 * D_MLSTM)
    c_all16 = jnp.pad(g_c.reshape(N_DEV, D_MODEL), ((0, 8), (0, 0)))

    b_ada_blk = lax.dynamic_slice(b_ada, (0, me * ADA_SHARD), (1, ADA_SHARD))
    mod_all, sc_all16 = _ada_mod(c_all16, w_ada[0], b_ada_blk)
    mod = lax.dynamic_index_in_dim(mod_all, me, axis=1, keepdims=False).reshape(1, 3 * D_MODEL)
    shift, scale, gate = mod[:, :D_MODEL], mod[:, D_MODEL:2 * D_MODEL], mod[:, 2 * D_MODEL:]

    flights = {}

    def send_dw_out(dwo):
        blocks = dwo.astype(BF16).reshape(N_DEV, D_MODEL // N_DEV, D_MODEL)
        flights["out"], token = _scatter_start("send_dw_out", (blocks,))
        return token

    def send_dw_in(dw_in, dcw):
        blocks = jnp.transpose(dw_in.astype(BF16).reshape(D_MODEL, N_DEV, N_SHARD), (1, 0, 2))
        dcw_blocks = jnp.transpose(dcw.reshape(CONV_WIDTH, N_DEV, 128), (1, 0, 2))
        flights["in"], token = _scatter_start("send_dw_in", (blocks, dcw_blocks))
        return token

    r = _local_step(x[0], loss_target[0], shift, scale, gate, norm_g, w_full_b, w_out_b, conv_w_full, conv_b,
                    w_pool[0], ls_pool, mh_norm_g, b_gates, final_g, send_dw_out, send_dw_in)

    pack = _pack_small(dict(loss=r["loss"][:, :1], final_g=r["final_g"], norm_g=r["norm_g"], conv_b=r["conv_b"],
                            ls_pool=r["ls_pool"], mh_norm_g=r["mh_norm_g"], b_gates=r["b_gates"][:, :8],
                            b_ada=r["dmod"], w_pool=r["w_pool"]))
    (p_pack,) = _all_gather("gather_small", pack)
    p_in, p_cw = _scatter_wait("recv_dw_in", flights["in"], p_pack)
    (p_out,) = _scatter_wait("recv_dw_out", flights["out"], p_cw)

    gi, di, mi, vi = _adam_sum("adam_w_in", p_in, w_in[0], m_w_in[0], v_w_in[0], 256)
    go, do_, mo, vo = _adam_sum("adam_w_out", p_out, w_out[0], m_w_out[0], v_w_out[0], 128)
    gc, dc, mc, vc = _adam_sum("adam_conv_w", p_cw, conv_w[0], m_conv_w[0], v_conv_w[0], CONV_WIDTH)

    def small(loss_like, fg_, ng_, cb_, ls_, mh_, bg_, ba_, wp_):
        return _pack_small(dict(loss=loss_like, final_g=fg_, norm_g=ng_, conv_b=cb_, ls_pool=ls_, mh_norm_g=mh_,
                                b_gates=bg_, b_ada=ba_, w_pool=wp_))

    zero = jnp.zeros((1, 1), F32)
    w_pack = small(zero, final_g, norm_g, conv_b, ls_pool, mh_norm_g, b_gates, b_ada, w_pool)
    m_pack = small(zero, m_final_g, m_norm_g, m_conv_b, m_ls_pool, m_mh_norm_g, m_b_gates, m_b_ada, m_w_pool)
    v_pack = small(zero, v_final_g, v_norm_g, v_conv_b, v_ls_pool, v_mh_norm_g, v_b_gates, v_b_ada, v_w_pool)
    gp, dp, mp, vp = _adam_sum("adam_small", p_pack, w_pack, m_pack, v_pack, PACK_TOTAL)

    off, rows = PACK_OFF["b_ada"]
    dmod_all = p_pack[:, off:off + rows, :].reshape(N_DEV, 3 * D_MODEL)
    dmod_blk16 = jnp.pad(lax.dynamic_slice(dmod_all, (0, me * ADA_SHARD), (N_DEV, ADA_SHARD)), ((0, 8), (0, 0)))
    ga, da, ma, va = _adam_ada(sc_all16, dmod_blk16, w_ada[0], m_w_ada[0], v_w_ada[0])

    names = ("norm_g", "w_ada", "b_ada", "w_in", "b_gates", "conv_w", "conv_b", "w_pool", "ls_pool", "mh_norm_g",
             "w_out", "final_g")
    shapes = dict(norm_g=norm_g.shape, b_ada=b_ada.shape, b_gates=b_gates.shape, conv_b=conv_b.shape,
                  w_pool=w_pool.shape, ls_pool=ls_pool.shape, mh_norm_g=mh_norm_g.shape, final_g=final_g.shape)
    sharded = dict(w_ada=(ga, da, ma, va), w_in=(gi, di, mi, vi), conv_w=(gc, dc, mc, vc), w_out=(go, do_, mo, vo))
    outs = []
    for kind in range(4):
        for nm in names:
            if nm in sharded:
                outs.append(sharded[nm][kind][None])
            else:
                outs.append(_unpack_small((gp, dp, mp, vp)[kind], nm, shapes[nm]))
    loss = gp[0, 0]
    grad_x = r["grad_x"].reshape(1, seq, D_MODEL)
    return (loss, grad_x, *outs)
```

```python
import jax
import jax.numpy as jnp
from jax import lax
from jax.experimental import pallas as pl
from jax.experimental.pallas import tpu as pltpu

F32 = jnp.float32
BF16 = jnp.bfloat16

D_MODEL = 1024
D_POOL = 512
D_MLSTM = 512
N_HEADS = 4
HEAD_DIM = 128
CHUNK = 128
POOL_WINDOWS = (2, 4, 8, 16)
POOL_GROUP_DIM = 128
CONV_WIDTH = 4
EPS = 1e-6
N_MAIN = 3584
N_IN = 3592
N_PAD = 3840
N_SHARD = N_IN // 8
ADA_SHARD = 3 * D_MODEL // 8
N_DEV = 8
CONV_HALO = 8
POOL_HALO = 16
NEG_BIG = -1e30
VMEM_LIMIT_BYTES = 56 * 1024 * 1024

ADAM_LR = 0.001
ADAM_B1 = 0.9
ADAM_B2 = 0.999
ADAM_EPS = 1e-08
ADAM_WD = 0.01
ADAM_STEP = 10

PACK_ROWS = (("loss", 8), ("final_g", 8), ("norm_g", 8), ("conv_b", 8), ("ls_pool", 8),
             ("mh_norm_g", 8), ("b_gates", 8), ("b_ada", 24), ("w_pool", 512))
PACK_TOTAL = sum(r for _, r in PACK_ROWS)


def _pack_offsets():
    off, out = 0, {}
    for name, rows in PACK_ROWS:
        out[name] = (off, rows)
        off += rows
    return out


PACK_OFF = _pack_offsets()


def _dot(a, b):
    return jnp.dot(a, b, preferred_element_type=F32)


def _dot_nt(a, b):
    return lax.dot_general(a, b, (((1,), (1,)), ((), ())), preferred_element_type=F32)


def _dot_tn(a, b):
    return lax.dot_general(a, b, (((0,), (0,)), ((), ())), preferred_element_type=F32)


def _dot_f32(a, b):
    return jnp.dot(a, b, precision=lax.Precision.HIGHEST, preferred_element_type=F32)


def _sigmoid(x):
    return jax.nn.sigmoid(x)


def _log_sigmoid(x):
    return jnp.minimum(x, 0.0) - jnp.log1p(jnp.exp(-jnp.abs(x)))


def _params(sem):
    return pltpu.CompilerParams(dimension_semantics=sem, vmem_limit_bytes=VMEM_LIMIT_BYTES)


def _full(shape):
    n = len(shape)
    return pl.BlockSpec(shape, lambda *_: (0,) * n)


def _mesh_pos():
    return lax.axis_index("x"), lax.axis_index("y"), lax.axis_index("c")


def _peer(k):
    x, y, c = _mesh_pos()
    px = 1 - x if (k >> 2) & 1 else x
    py = 1 - y if (k >> 1) & 1 else y
    pc = 1 - c if k & 1 else c
    return (px, py, pc), 4 * px + 2 * py + pc


def _remote(src, dst, send_sem, recv_sem, to):
    return pltpu.make_async_remote_copy(src_ref=src, dst_ref=dst, send_sem=send_sem, recv_sem=recv_sem, device_id=to,
                                        device_id_type=pl.DeviceIdType.MESH)


def _other_chips():
    x, y, _ = _mesh_pos()
    return [(1 - x, y), (x, 1 - y), (1 - x, 1 - y)]


def _two_level_gather(src, dst, send_sems, recv_sems, local_sems):
    n = len(src)
    x, y, c = _mesh_pos()
    me = 4 * x + 2 * y + c
    sibling = (x, y, 1 - c)
    chips = _other_chips()

    def copy(a, k, block, to, own):
        return _remote(src[a] if own else dst[a].at[block], dst[a].at[block], send_sems.at[a, k], recv_sems.at[a, k], to)

    local = [pltpu.make_async_copy(src[a], dst[a].at[me], local_sems.at[a]) for a in range(n)]
    first = [copy(a, 0, me, sibling, True) for a in range(n)]
    first += [copy(a, 1 + j, me, (*chip, c), True) for j, chip in enumerate(chips) for a in range(n)]
    for cp in local + first:
        cp.start()
    passed = []
    for j, (px, py) in enumerate(chips):
        block = 4 * px + 2 * py + c
        for a in range(n):
            copy(a, 1 + j, block, sibling, False).wait_recv()
            passed.append(copy(a, 4 + j, block, sibling, False))
            passed[-1].start()
    for a in range(n):
        copy(a, 0, 4 * x + 2 * y + (1 - c), sibling, False).wait_recv()
    for j, (px, py) in enumerate(chips):
        for a in range(n):
            copy(a, 4 + j, 4 * px + 2 * py + (1 - c), sibling, False).wait_recv()
    for cp in first + passed:
        cp.wait_send()
    for cp in local:
        cp.wait()


GATHER_COPIES = 7


def _all_gather(name, *shards):
    n = len(shards)

    def body(*refs):
        _two_level_gather(refs[:n], refs[n:2 * n], *refs[2 * n:])

    hbm = pl.BlockSpec(memory_space=pltpu.HBM)
    return pl.pallas_call(
        body, name=name,
        out_shape=tuple(jax.ShapeDtypeStruct((N_DEV,) + s.shape, s.dtype) for s in shards),
        in_specs=[hbm] * n, out_specs=tuple([hbm] * n),
        scratch_shapes=[pltpu.SemaphoreType.DMA((n, GATHER_COPIES)), pltpu.SemaphoreType.DMA((n, GATHER_COPIES)),
                        pltpu.SemaphoreType.DMA((n,))],
    )(*shards)


def _ada_mod(c_all16, w_ada_blk, b_ada_blk):
    def body(c_ref, w_ref, b_ref, out_ref, sc_ref, send_sems, recv_sems):
        x, y, c = _mesh_pos()
        me = 4 * x + 2 * y + c
        cv = c_ref[...]
        sc = cv * _sigmoid(cv)
        sc_ref[...] = sc
        blk = _dot(sc.astype(BF16), w_ref[...].astype(BF16)) + b_ref[...]
        out_ref[me] = blk[0:N_DEV, :]
        copies = []
        for k in range(1, N_DEV):
            peer, _ = _peer(k)
            copies.append(pltpu.make_async_remote_copy(
                src_ref=out_ref.at[me], dst_ref=out_ref.at[me], send_sem=send_sems.at[k - 1],
                recv_sem=recv_sems.at[k - 1], device_id=peer, device_id_type=pl.DeviceIdType.MESH))
        for cp in copies:
            cp.start()
        for cp in copies:
            cp.wait()

    vmem = pl.BlockSpec(memory_space=pltpu.VMEM)
    return pl.pallas_call(
        body, name="ada_mod",
        out_shape=(jax.ShapeDtypeStruct((N_DEV, N_DEV, ADA_SHARD), F32),
                   jax.ShapeDtypeStruct(c_all16.shape, F32)),
        in_specs=[vmem] * 3, out_specs=(vmem, vmem),
        scratch_shapes=[pltpu.SemaphoreType.DMA((N_DEV - 1,)), pltpu.SemaphoreType.DMA((N_DEV - 1,))],
    )(c_all16, w_ada_blk, b_ada_blk)


def _scatter_copies(src, land, send_sems, recv_sems):
    x, y, c = _mesh_pos()
    me = 4 * x + 2 * y + c
    copies = []
    for k in range(1, N_DEV):
        peer, p = _peer(k)
        for a in range(len(src)):
            i = a * (N_DEV - 1) + k - 1
            copies.append(_remote(src[a].at[p], land[a].at[me], send_sems.at[i], recv_sems.at[i], peer))
    return copies


def _scatter_start(name, blocks):
    n = len(blocks)

    def body(*refs):
        src, land = refs[:n], refs[n:2 * n]
        send_sems, recv_sems = refs[2 * n], refs[2 * n + 1]
        token_ref = refs[-1]
        for cp in _scatter_copies(src, land, send_sems, recv_sems):
            cp.start()
        token_ref[...] = jnp.zeros_like(token_ref)

    hbm = pl.BlockSpec(memory_space=pltpu.HBM)
    sem = pl.BlockSpec(memory_space=pltpu.SEMAPHORE)
    through = tuple(pltpu.HBM(b.shape, b.dtype) for b in blocks)
    args = [pltpu.with_memory_space_constraint(b, pltpu.HBM) for b in blocks]
    args += [pltpu.with_memory_space_constraint(lax.empty(b.shape, b.dtype), pltpu.HBM) for b in blocks]
    out = pl.pallas_call(
        body, name=name,
        out_shape=(pltpu.SemaphoreType.DMA((n * (N_DEV - 1),)),) * 2 + through + through
        + (jax.ShapeDtypeStruct((8, 128), F32),),
        in_specs=[hbm] * (2 * n),
        out_specs=(sem, sem) + (hbm,) * (2 * n) + (pl.BlockSpec(memory_space=pltpu.VMEM),),
        input_output_aliases={i: 2 + i for i in range(2 * n)},
        compiler_params=pltpu.CompilerParams(has_side_effects=pltpu.SideEffectType.DATAFLOW_SIDE_EFFECTING),
    )(*args)
    return out[:-1], out[-1][0:1, 0:1]


def _scatter_wait(name, state, after):
    n = (len(state) - 2) // 2
    send_sems, recv_sems = state[0], state[1]
    src, land = state[2:2 + n], state[2 + n:]

    def body(*refs):
        src_r, land_r = refs[:n], refs[n:2 * n]
        for cp in _scatter_copies(src_r, land_r, refs[2 * n], refs[2 * n + 1]):
            cp.wait_send()
            cp.wait_recv()

    hbm = pl.BlockSpec(memory_space=pltpu.HBM)
    sem = pl.BlockSpec(memory_space=pltpu.SEMAPHORE)
    out = pl.pallas_call(
        body, name=name,
        out_shape=tuple(pltpu.HBM(b.shape, b.dtype) for b in src + land),
        in_specs=[hbm] * (2 * n) + [sem, sem, pl.BlockSpec(memory_space=pl.ANY)],
        out_specs=(hbm,) * (2 * n),
        input_output_aliases={i: i for i in range(2 * n)},
        compiler_params=pltpu.CompilerParams(has_side_effects=pltpu.SideEffectType.DATAFLOW_SIDE_EFFECTING),
    )(*src, *land, send_sems, recv_sems, after)
    me = 4 * lax.axis_index("x") + 2 * lax.axis_index("y") + lax.axis_index("c")
    landed = []
    for a in range(n):
        own = lax.dynamic_index_in_dim(out[a], me, axis=0, keepdims=True)
        landed.append(lax.dynamic_update_slice_in_dim(out[n + a], own, me, axis=0))
    return landed


def _fwd_proj(x, norm_g, scale, shift, w_in_t):
    seq = x.shape[0]
    tm = min(512, seq)
    tn = 512

    def body(x_ref, ng_ref, sc_ref, sh_ref, wt_ref, proj_ref, gates_ref, h_ref):
        xt = x_ref[...]
        r = lax.rsqrt(jnp.mean(xt * xt, axis=-1, keepdims=True) + EPS)
        h = ((xt * r) * ng_ref[...]) * (1.0 + sc_ref[...]) + sh_ref[...]
        hb = h.astype(BF16)
        h_ref[...] = hb
        gates_ref[...] = _dot_nt(hb, wt_ref[N_MAIN:N_MAIN + 128, :])
        for j in range(N_MAIN // tn):
            proj_ref[:, j * tn:(j + 1) * tn] = _dot_nt(hb, wt_ref[j * tn:(j + 1) * tn, :])

    vec = _full((1, D_MODEL))
    tile = pl.BlockSpec((tm, D_MODEL), lambda i: (i, 0))
    return pl.pallas_call(
        body, name="fwd_proj", grid=(seq // tm,),
        out_shape=(jax.ShapeDtypeStruct((seq, N_MAIN), F32), jax.ShapeDtypeStruct((seq, 128), F32),
                   jax.ShapeDtypeStruct((seq, D_MODEL), BF16)),
        in_specs=[tile, vec, vec, vec, _full((N_PAD, D_MODEL))],
        out_specs=(pl.BlockSpec((tm, N_MAIN), lambda i: (i, 0)), pl.BlockSpec((tm, 128), lambda i: (i, 0)), tile),
        compiler_params=_params(("arbitrary",)),
    )(x, norm_g, scale, shift, w_in_t)


def _gate_forms(gpre):
    r = lax.broadcasted_iota(jnp.int32, (CHUNK, CHUNK), 0)
    c = lax.broadcasted_iota(jnp.int32, (CHUNK, CHUNK), 1)
    causal = c <= r
    ltri = jnp.where(causal, 1.0, 0.0).astype(F32)
    utri = jnp.where(r <= c, 1.0, 0.0).astype(F32)
    bcol = _dot_f32(ltri, _log_sigmoid(gpre))
    gt8 = gpre.T[0:8, :]
    brow = _dot_f32(_log_sigmoid(gt8), utri)
    return causal, utri, bcol, gt8, brow


def _in_lockstep(stages):
    alive = list(stages)
    while alive:
        still = []
        for g in alive:
            try:
                next(g)
                still.append(g)
            except StopIteration:
                pass
        alive = still


def _head_fwd(qh, kh, vh, bc, br, igr, m_prev, c_h, n_row, causal):
    qb, kb, vb, cb = qh.astype(BF16), kh.astype(BF16), vh.astype(BF16), c_h.astype(BF16)
    qk = _dot_nt(qb, kb)
    cq = _dot_nt(qb, cb)
    yield
    dlog = jnp.where(causal, bc - br + igr, NEG_BIG)
    inter_log = bc + m_prev
    m_t = jnp.maximum(inter_log, jnp.max(dlog, axis=-1, keepdims=True))
    yield
    dmat = jnp.exp(dlog - m_t)
    inter = jnp.exp(inter_log - m_t)
    s = qk * dmat
    sv = _dot(s.astype(BF16), vb)
    yield
    nq = jnp.sum(qh * n_row, axis=-1, keepdims=True)
    den = jnp.sum(s, axis=-1, keepdims=True) + inter * nq
    emt = jnp.exp(-m_t)
    yield
    num = sv + inter * cq
    dn = jnp.maximum(jnp.abs(den), emt)
    hm = num / dn
    return dict(dmat=dmat, inter=inter, qb=qb, kb=kb, vb=vb, cb=cb, s=s, cq=cq, nq=nq, den=den, emt=emt,
                dn=dn, hm=hm)


def _state_weights(bc, igc, m_prev, m_new=None):
    last = lax.broadcasted_iota(jnp.int32, (CHUNK, 1), 0) == CHUNK - 1
    b_last = jnp.sum(jnp.where(last, bc, 0.0), axis=0, keepdims=True)
    wlog = b_last - bc + igc
    if m_new is None:
        m_new = jnp.maximum(b_last + m_prev, jnp.max(wlog, axis=0, keepdims=True))
    w_c = jnp.exp(wlog - m_new)
    decay = jnp.exp(b_last + m_prev - m_new)
    return w_c, decay, m_new, last


def _rows_back(x, k):
    return x if k == 0 else pltpu.roll(x, k, 0)


def _rows_ahead(x, k):
    return x if k == 0 else pltpu.roll(x, x.shape[0] - k, 0)


def _conv_taps(xpad):
    return [_rows_back(xpad, CONV_WIDTH - 1 - j)[CONV_HALO:, :] for j in range(CONV_WIDTH)]


def _conv_pre(taps, cw_ref, cb_ref):
    a = cb_ref[...]
    for j in range(CONV_WIDTH):
        a = a + cw_ref[j:j + 1, :] * taps[j]
    return a


def _window_sum(x, w, shift):
    k = 1
    while k < w:
        x = x + shift(x, k)
        k *= 2
    return x


def _pool_window_sum(upad_ref, g, w):
    lanes = slice(g * POOL_GROUP_DIM, (g + 1) * POOL_GROUP_DIM)
    return _window_sum(upad_ref[:, lanes], w, _rows_back)[POOL_HALO:, :]


def _pool_inv_count(chunk_idx, w):
    pos = chunk_idx * CHUNK + lax.broadcasted_iota(jnp.int32, (CHUNK, 1), 0) + 1
    return 1.0 / jnp.minimum(pos, w).astype(F32)


def _mixer_in_specs(cmap, n_chunks):
    def rows(i):
        return cmap(i)
    return [
        pl.BlockSpec((CHUNK, 1024), lambda i: (rows(i), 0)),
        pl.BlockSpec((CHUNK, 1024), lambda i: (rows(i), 1)),
        pl.BlockSpec((CHUNK, 512), lambda i: (rows(i), 4)),
        pl.BlockSpec((CHUNK, 512), lambda i: (rows(i), 5)),
        pl.BlockSpec((CHUNK, 512), lambda i: (rows(i), 6)),
        pl.BlockSpec((POOL_HALO, 512), lambda i: (jnp.maximum(rows(i) * (CHUNK // POOL_HALO) - 1, 0), 0)),
        pl.BlockSpec((CONV_HALO, 1024), lambda i: (jnp.maximum(rows(i) * (CHUNK // CONV_HALO) - 1, 0), 1)),
    ]


def _mix_fwd(proj, gates, bg_pad, conv_w8, conv_b, w_pool, ls_pool, mh_g):
    seq = proj.shape[0]
    n_chunks = seq // CHUNK

    def body(uz_ref, qk_ref, v_ref, o_ref, zm_ref, uh_ref, qkh_ref, g_ref, bg_ref, cw_ref, cb_ref, wp_ref,
             ls_ref, mhg_ref, mix_ref, cst_ref, nst_ref, mst_ref, c_scr, n_scr, m_scr, xpad, upad):
        i = pl.program_id(0)

        @pl.when(i == 0)
        def _():
            c_scr[...] = jnp.zeros_like(c_scr)
            n_scr[...] = jnp.zeros_like(n_scr)
            m_scr[...] = jnp.zeros_like(m_scr)

        cst_ref[0] = c_scr[...]
        nst_ref[0] = n_scr[...]
        mst_ref[0] = m_scr[...]
        first = i == 0

        upad[0:POOL_HALO, :] = jnp.where(first, 0.0, uh_ref[...])
        upad[POOL_HALO:POOL_HALO + CHUNK, :] = uz_ref[:, 0:D_POOL]
        for g, w in enumerate(POOL_WINDOWS):
            lanes = slice(g * POOL_GROUP_DIM, (g + 1) * POOL_GROUP_DIM)
            pooled = _pool_window_sum(upad, g, w) * _pool_inv_count(i, w) - uz_ref[:, lanes]
            y = _dot(pooled.astype(BF16), wp_ref[g].astype(BF16)) * ls_ref[:, lanes]
            zp = uz_ref[:, D_POOL + g * POOL_GROUP_DIM:D_POOL + (g + 1) * POOL_GROUP_DIM]
            mix_ref[:, lanes] = (y * (zp * _sigmoid(zp))).astype(BF16)

        xpad[0:CONV_HALO, :] = jnp.where(first, 0.0, qkh_ref[...])
        xpad[CONV_HALO:CONV_HALO + CHUNK, :] = qk_ref[...]
        a = _conv_pre(_conv_taps(xpad[...]), cw_ref, cb_ref)
        qk = a * _sigmoid(a)

        gpre = g_ref[...] + bg_ref[...]
        causal, _, bcol, gt8, brow = _gate_forms(gpre)
        def head(h):
            lanes = slice(h * HEAD_DIM, (h + 1) * HEAD_DIM)
            qh = qk[:, lanes]
            kh = qk[:, D_MLSTM + h * HEAD_DIM:D_MLSTM + (h + 1) * HEAD_DIM] * (HEAD_DIM ** -0.5)
            vh = v_ref[:, lanes]
            bc = bcol[:, N_HEADS + h:N_HEADS + h + 1]
            br = brow[N_HEADS + h:N_HEADS + h + 1, :]
            igr = gt8[h:h + 1, :]
            igc = gpre[:, h:h + 1]
            m_prev = m_scr[h:h + 1, 0:1]
            c_h = c_scr[h]
            n_row = n_scr[h:h + 1, :]
            w_c, decay, m_new, _ = _state_weights(bc, igc, m_prev)
            c_scr[h] = decay * c_h + _dot_tn((vh * w_c).astype(BF16), kh.astype(BF16))
            n_scr[h:h + 1, :] = decay * n_row + jnp.sum(w_c * kh, axis=0, keepdims=True)
            m_scr[h:h + 1, :] = jnp.broadcast_to(m_new, (1, 128))
            f = yield from _head_fwd(qh, kh, vh, bc, br, igr, m_prev, c_h, n_row, causal)
            yield
            hm = f["hm"]
            hn = hm * lax.rsqrt(jnp.mean(hm * hm, axis=-1, keepdims=True) + EPS) * mhg_ref[:, lanes]
            zm = zm_ref[:, lanes]
            out = hn * _sigmoid(o_ref[:, lanes]) * (zm * _sigmoid(zm))
            mix_ref[:, D_POOL + h * HEAD_DIM:D_POOL + (h + 1) * HEAD_DIM] = out.astype(BF16)

        _in_lockstep(head(h) for h in range(N_HEADS))

    cmap = lambda i: i
    in_specs = _mixer_in_specs(cmap, n_chunks) + [
        pl.BlockSpec((CHUNK, 128), lambda i: (i, 0)),
        _full((1, 128)), _full((8, 1024)), _full((1, 1024)), _full((4, 128, 128)), _full((1, 512)),
        _full((1, 512))]
    return pl.pallas_call(
        body, name="mix_fwd", grid=(n_chunks,),
        out_shape=(jax.ShapeDtypeStruct((seq, D_MODEL), BF16),
                   jax.ShapeDtypeStruct((n_chunks, N_HEADS, HEAD_DIM, HEAD_DIM), F32),
                   jax.ShapeDtypeStruct((n_chunks, 8, 128), F32),
                   jax.ShapeDtypeStruct((n_chunks, 8, 128), F32)),
        in_specs=in_specs,
        out_specs=(pl.BlockSpec((CHUNK, D_MODEL), lambda i: (i, 0)),
                   pl.BlockSpec((1, N_HEADS, HEAD_DIM, HEAD_DIM), lambda i: (i, 0, 0, 0)),
                   pl.BlockSpec((1, 8, 128), lambda i: (i, 0, 0)),
                   pl.BlockSpec((1, 8, 128), lambda i: (i, 0, 0))),
        scratch_shapes=[pltpu.VMEM((N_HEADS, HEAD_DIM, HEAD_DIM), F32), pltpu.VMEM((8, 128), F32),
                        pltpu.VMEM((8, 128), F32), pltpu.VMEM((CONV_HALO + CHUNK, 1024), F32),
                        pltpu.VMEM((POOL_HALO + CHUNK, D_POOL), F32)],
        compiler_params=_params(("arbitrary",)),
    )(proj, proj, proj, proj, proj, proj, proj, gates, bg_pad, conv_w8, conv_b, w_pool, ls_pool, mh_g)


def _out_fwd_bwd(mix, x, tgt, w_out_b, gate, final_g):
    seq = x.shape[0]
    tm = min(256, seq)

    def body(mix_ref, x_ref, t_ref, w_ref, gate_ref, fg_ref, dx2_ref, dmix_ref, dwo_ref, dgate_ref, dfg_ref,
             loss_ref):
        @pl.when(pl.program_id(0) == 0)
        def _():
            dwo_ref[...] = jnp.zeros_like(dwo_ref)
            dgate_ref[...] = jnp.zeros_like(dgate_ref)
            dfg_ref[...] = jnp.zeros_like(dfg_ref)
            loss_ref[...] = jnp.zeros_like(loss_ref)

        mixb = mix_ref[...]
        w = w_ref[...]
        gate_v = gate_ref[...]
        fg = fg_ref[...]
        o2 = _dot(mixb, w)
        x2 = x_ref[...] + gate_v * o2
        r2 = lax.rsqrt(jnp.mean(x2 * x2, axis=-1, keepdims=True) + EPS)
        x2n = x2 * r2
        err = x2n * fg - t_ref[...]
        part = 0.5 * jnp.sum(jnp.sum(err * err, axis=-1, keepdims=True), axis=0, keepdims=True) / D_MODEL
        loss_ref[...] += jnp.broadcast_to(part, loss_ref.shape)
        dy = err / D_MODEL
        dfg_ref[...] += jnp.sum(dy * x2n, axis=0, keepdims=True)
        gdy = dy * fg
        dx2 = r2 * (gdy - x2n * jnp.mean(gdy * x2n, axis=-1, keepdims=True))
        dx2_ref[...] = dx2
        dgate_ref[...] += jnp.sum(dx2 * o2, axis=0, keepdims=True)
        do2 = (dx2 * gate_v).astype(BF16)
        dmix_ref[...] = _dot_nt(do2, w)
        dwo_ref[...] += _dot_tn(mixb, do2)

    tile = pl.BlockSpec((tm, D_MODEL), lambda i: (i, 0))
    vec = _full((1, D_MODEL))
    return pl.pallas_call(
        body, name="out_fwd_bwd", grid=(seq // tm,),
        out_shape=(jax.ShapeDtypeStruct((seq, D_MODEL), F32), jax.ShapeDtypeStruct((seq, D_MODEL), F32),
                   jax.ShapeDtypeStruct((D_MODEL, D_MODEL), F32), jax.ShapeDtypeStruct((1, D_MODEL), F32),
                   jax.ShapeDtypeStruct((1, D_MODEL), F32), jax.ShapeDtypeStruct((1, 128), F32)),
        in_specs=[tile, tile, tile, _full((D_MODEL, D_MODEL)), vec, vec],
        out_specs=(tile, tile, _full((D_MODEL, D_MODEL)), vec, vec, _full((1, 128))),
        compiler_params=_params(("arbitrary",)),
    )(mix, x, tgt, w_out_b, gate, final_g)


def _mix_bwd(proj, gates, dmix, cst, nst, mst, bg_pad, conv_w8, conv_b, w_pool, ls_pool, mh_g):
    seq = proj.shape[0]
    n_chunks = seq // CHUNK

    def body(uz_ref, qk_ref, v_ref, o_ref, zm_ref, uh_ref, qkh_ref, g_ref, dmix_ref, cst_ref, nst_ref, mst_ref,
             mnx_ref, bg_ref, cw_ref, cb_ref, wp_ref, ls_ref, mhg_ref,
             dp_ref, dcw_ref, dcb_ref, dwp_ref, dls_ref, dmhg_ref, dbg_ref,
             dc_scr, dn_scr, xpad, upad, dapad, dpipad):
        i = pl.program_id(0)
        ci = n_chunks - 1 - i

        @pl.when(i == 0)
        def _():
            for ref in (dc_scr, dn_scr, dcw_ref, dcb_ref, dwp_ref, dls_ref, dmhg_ref, dbg_ref):
                ref[...] = jnp.zeros_like(ref)
            dapad[CHUNK:CHUNK + CONV_HALO, :] = jnp.zeros((CONV_HALO, 1024), F32)
            dpipad[CHUNK:CHUNK + POOL_HALO, :] = jnp.zeros((POOL_HALO, D_POOL), F32)

        first = ci == 0

        upad[0:POOL_HALO, :] = jnp.where(first, 0.0, uh_ref[...])
        upad[POOL_HALO:POOL_HALO + CHUNK, :] = uz_ref[:, 0:D_POOL]
        dpooled = []
        for g, w in enumerate(POOL_WINDOWS):
            lanes = slice(g * POOL_GROUP_DIM, (g + 1) * POOL_GROUP_DIM)
            zlanes = slice(D_POOL + g * POOL_GROUP_DIM, D_POOL + (g + 1) * POOL_GROUP_DIM)
            inv = _pool_inv_count(ci, w)
            pooled = _pool_window_sum(upad, g, w) * inv - uz_ref[:, lanes]
            pb = pooled.astype(BF16)
            wpb = wp_ref[g].astype(BF16)
            yw = _dot(pb, wpb)
            ls = ls_ref[:, lanes]
            zp = uz_ref[:, zlanes]
            sg = _sigmoid(zp)
            dpo = dmix_ref[:, lanes]
            dp_ref[:, zlanes] = (dpo * (yw * ls) * (sg * (1.0 + zp * (1.0 - sg)))).astype(BF16)
            dy = dpo * (zp * sg)
            dls_ref[:, lanes] += jnp.sum(dy * yw, axis=0, keepdims=True)
            dyw = (dy * ls).astype(BF16)
            dwp_ref[g] += _dot_tn(pb, dyw)
            dpl = _dot_nt(dyw, wpb)
            dpooled.append(dpl)
            dpipad[0:CHUNK, lanes] = dpl * inv
        for g, w in enumerate(POOL_WINDOWS):
            lanes = slice(g * POOL_GROUP_DIM, (g + 1) * POOL_GROUP_DIM)
            du = _window_sum(dpipad[:, lanes], w, _rows_ahead)[0:CHUNK, :] - dpooled[g]
            dp_ref[:, lanes] = du.astype(BF16)
        dpipad[CHUNK:CHUNK + POOL_HALO, :] = dpipad[0:POOL_HALO, :]

        xpad[0:CONV_HALO, :] = jnp.where(first, 0.0, qkh_ref[...])
        xpad[CONV_HALO:CONV_HALO + CHUNK, :] = qk_ref[...]
        taps = _conv_taps(xpad[...])
        a = _conv_pre(taps, cw_ref, cb_ref)
        sga = _sigmoid(a)
        qk = a * sga
        dsilu_a = sga * (1.0 + a * (1.0 - sga))

        gpre = g_ref[...] + bg_ref[...]
        causal, utri, bcol, gt8, brow = _gate_forms(gpre)
        lane = lax.broadcasted_iota(jnp.int32, (CHUNK, 128), 1)
        row = lax.broadcasted_iota(jnp.int32, (CHUNK, 128), 0)
        col_g_rows, dig_parts, db_parts = [], [], []
        scale_k = HEAD_DIM ** -0.5

        def head(h):
            lanes = slice(h * HEAD_DIM, (h + 1) * HEAD_DIM)
            klanes = slice(D_MLSTM + h * HEAD_DIM, D_MLSTM + (h + 1) * HEAD_DIM)
            qh = qk[:, lanes]
            kh = qk[:, klanes] * scale_k
            vh = v_ref[:, lanes]
            bc = bcol[:, N_HEADS + h:N_HEADS + h + 1]
            br = brow[N_HEADS + h:N_HEADS + h + 1, :]
            igr = gt8[h:h + 1, :]
            igc = gpre[:, h:h + 1]
            m_prev = mst_ref[0, h:h + 1, 0:1]
            m_next = mnx_ref[0, h:h + 1, 0:1]
            c_h = cst_ref[0, h]
            n_row = nst_ref[0, h:h + 1, :]
            w_c, decay, _, last = _state_weights(bc, igc, m_prev, m_next)
            dcn = dc_scr[h]
            dnn = dn_scr[h:h + 1, :]
            dcnb = dcn.astype(BF16)
            vb0, kb0 = vh.astype(BF16), kh.astype(BF16)
            amat = _dot(vb0, dcnb) + dnn
            kdc = _dot_nt(kb0, dcnb)
            ddecay = (jnp.sum(jnp.sum(dcn * c_h, axis=-1, keepdims=True), axis=0, keepdims=True)
                      + jnp.sum(dnn * n_row, axis=-1, keepdims=True))
            f = yield from _head_fwd(qh, kh, vh, bc, br, igr, m_prev, c_h, n_row, causal)
            qb, kb, vb, cb = f["qb"], f["kb"], f["vb"], f["cb"]
            s, dmat, inter, den, dn, hm = f["s"], f["dmat"], f["inter"], f["den"], f["dn"], f["hm"]
            yield

            rinv = lax.rsqrt(jnp.mean(hm * hm, axis=-1, keepdims=True) + EPS)
            hmn = hm * rinv
            gh = mhg_ref[:, lanes]
            o_pre = o_ref[:, lanes]
            og = _sigmoid(o_pre)
            zm = zm_ref[:, lanes]
            sgz = _sigmoid(zm)
            sz = zm * sgz
            dout = dmix_ref[:, D_POOL + h * HEAD_DIM:D_POOL + (h + 1) * HEAD_DIM]
            hn = hmn * gh
            dp_ref[:, 2560 + h * HEAD_DIM:2560 + (h + 1) * HEAD_DIM] = (
                dout * hn * sz * og * (1.0 - og)).astype(BF16)
            dp_ref[:, 3072 + h * HEAD_DIM:3072 + (h + 1) * HEAD_DIM] = (
                dout * hn * og * (sgz * (1.0 + zm * (1.0 - sgz)))).astype(BF16)
            dhn = dout * og * sz
            dmhg_ref[:, lanes] += jnp.sum(dhn * hmn, axis=0, keepdims=True)
            dyn = dhn * gh
            dhm = rinv * (dyn - hmn * jnp.mean(dyn * hmn, axis=-1, keepdims=True))
            yield

            inv_dn = 1.0 / dn
            dnum = dhm * inv_dn
            hd = jnp.sum(dhm * hm, axis=-1, keepdims=True)
            dden = jnp.where(jnp.abs(den) > f["emt"], -hd * inv_dn * jnp.sign(den), 0.0)
            dnb = dnum.astype(BF16)
            dnv = _dot_nt(dnb, vb)
            dv = _dot_tn(s.astype(BF16), dnb)
            dnc = _dot(dnb, cb)
            dc_prev = _dot_tn((inter * dnum).astype(BF16), qb)
            yield
            ds = dnv + dden
            dqk = (ds * dmat).astype(BF16)
            dqk_k = _dot(dqk, kb)
            dk = _dot_tn(dqk, qb)
            yield
            gmat = ds * s
            row_g = jnp.sum(gmat, axis=-1, keepdims=True)
            col_g_rows.append(jnp.where(row == h, jnp.sum(gmat, axis=0, keepdims=True), 0.0))
            gcol = inter * (jnp.sum(dnum * f["cq"], axis=-1, keepdims=True) + dden * f["nq"])
            dn_prev = jnp.sum((inter * dden) * qh, axis=0, keepdims=True)
            dw = jnp.sum(amat * kh, axis=-1, keepdims=True)
            e = dw * w_c
            db_last = ddecay * decay + jnp.sum(e, axis=0, keepdims=True)
            dig_parts.append(jnp.where(lane == h, e, 0.0))
            db_parts.append(jnp.where(lane == N_HEADS + h, row_g + gcol - e + jnp.where(last, db_last, 0.0), 0.0))
            dc_scr[h] = decay * dcn + dc_prev
            dn_scr[h:h + 1, :] = decay * dnn + dn_prev
            yield
            dq = dqk_k + inter * (dnc + dden * n_row)
            dp_ref[:, 2048 + h * HEAD_DIM:2048 + (h + 1) * HEAD_DIM] = (dv + w_c * kdc).astype(BF16)
            dapad[0:CHUNK, lanes] = dq * dsilu_a[:, lanes]
            dapad[0:CHUNK, klanes] = (dk + w_c * amat) * scale_k * dsilu_a[:, klanes]

        _in_lockstep(head(h) for h in range(N_HEADS))

        cs_t = sum(col_g_rows[1:], col_g_rows[0]).T
        dig_all = sum(dig_parts[1:], dig_parts[0]) + cs_t
        db_cols = sum(db_parts[1:], db_parts[0])
        shifted = jnp.zeros((CHUNK, 128), F32)
        for h in range(N_HEADS):
            shifted = shifted + jnp.where(lane == N_HEADS + h, cs_t[:, h:h + 1], 0.0)
        dlf = _dot_f32(utri, db_cols - shifted)
        dgates = dig_all + dlf * _sigmoid(-gpre)
        dp_ref[:, N_MAIN:N_MAIN + 128] = dgates.astype(BF16)
        dp_ref[:, N_MAIN + 128:N_PAD] = jnp.zeros((CHUNK, N_PAD - N_MAIN - 128), BF16)
        dbg_ref[...] += jnp.sum(dgates, axis=0, keepdims=True)

        da_pad = dapad[...]
        da = da_pad[0:CHUNK, :]
        dcb_ref[...] += jnp.sum(da, axis=0, keepdims=True)
        dx = jnp.zeros((CHUNK, 1024), F32)
        for j in range(CONV_WIDTH):
            dcw_ref[j:j + 1, :] += jnp.sum(da * taps[j], axis=0, keepdims=True)
            dx = dx + cw_ref[j:j + 1, :] * _rows_ahead(da_pad, CONV_WIDTH - 1 - j)[0:CHUNK, :]
        dp_ref[:, 1024:2048] = dx.astype(BF16)
        dapad[CHUNK:CHUNK + CONV_HALO, :] = dapad[0:CONV_HALO, :]

    cmap = lambda i: n_chunks - 1 - i
    in_specs = _mixer_in_specs(cmap, n_chunks) + [
        pl.BlockSpec((CHUNK, 128), lambda i: (cmap(i), 0)),
        pl.BlockSpec((CHUNK, D_MODEL), lambda i: (cmap(i), 0)),
        pl.BlockSpec((1, N_HEADS, HEAD_DIM, HEAD_DIM), lambda i: (cmap(i), 0, 0, 0)),
        pl.BlockSpec((1, 8, 128), lambda i: (cmap(i), 0, 0)),
        pl.BlockSpec((1, 8, 128), lambda i: (cmap(i), 0, 0)),
        pl.BlockSpec((1, 8, 128), lambda i: (jnp.minimum(cmap(i) + 1, n_chunks - 1), 0, 0)),
        _full((1, 128)), _full((8, 1024)), _full((1, 1024)), _full((4, 128, 128)), _full((1, 512)),
        _full((1, 512))]
    return pl.pallas_call(
        body, name="mix_bwd", grid=(n_chunks,),
        out_shape=(jax.ShapeDtypeStruct((seq, N_PAD), BF16), jax.ShapeDtypeStruct((8, 1024), F32),
                   jax.ShapeDtypeStruct((1, 1024), F32), jax.ShapeDtypeStruct((4, 128, 128), F32),
                   jax.ShapeDtypeStruct((1, 512), F32), jax.ShapeDtypeStruct((1, 512), F32),
                   jax.ShapeDtypeStruct((1, 128), F32)),
        in_specs=in_specs,
        out_specs=(pl.BlockSpec((CHUNK, N_PAD), lambda i: (cmap(i), 0)), _full((8, 1024)), _full((1, 1024)),
                   _full((4, 128, 128)), _full((1, 512)), _full((1, 512)), _full((1, 128))),
        scratch_shapes=[pltpu.VMEM((N_HEADS, HEAD_DIM, HEAD_DIM), F32), pltpu.VMEM((8, 128), F32),
                        pltpu.VMEM((CONV_HALO + CHUNK, 1024), F32), pltpu.VMEM((POOL_HALO + CHUNK, D_POOL), F32),
                        pltpu.VMEM((CHUNK + CONV_HALO, 1024), F32), pltpu.VMEM((CHUNK + POOL_HALO, D_POOL), F32)],
        compiler_params=_params(("arbitrary",)),
    )(proj, proj, proj, proj, proj, proj, proj, gates, dmix, cst, nst, mst, mst, bg_pad, conv_w8, conv_b,
      w_pool, ls_pool, mh_g)


def _bwd_in(dproj, w_in_t, x, dx2, norm_g, scale):
    seq = x.shape[0]
    tm = min(512, seq)

    def body(dp_ref, wt_ref, x_ref, dx2_ref, ng_ref, sc_ref, gx_ref, dsh_ref, dsc_ref, dng_ref):
        @pl.when(pl.program_id(0) == 0)
        def _():
            dsh_ref[...] = jnp.zeros_like(dsh_ref)
            dsc_ref[...] = jnp.zeros_like(dsc_ref)
            dng_ref[...] = jnp.zeros_like(dng_ref)

        dh = _dot(dp_ref[...], wt_ref[...])
        xt = x_ref[...]
        r = lax.rsqrt(jnp.mean(xt * xt, axis=-1, keepdims=True) + EPS)
        xn = xt * r
        ng = ng_ref[...]
        one_sc = 1.0 + sc_ref[...]
        dsh_ref[...] += jnp.sum(dh, axis=0, keepdims=True)
        dhxn = dh * xn
        dsc_ref[...] += jnp.sum(dhxn * ng, axis=0, keepdims=True)
        dng_ref[...] += jnp.sum(dhxn * one_sc, axis=0, keepdims=True)
        dxn = dh * (ng * one_sc)
        gx_ref[...] = r * (dxn - xn * jnp.mean(dxn * xn, axis=-1, keepdims=True)) + dx2_ref[...]

    tile = pl.BlockSpec((tm, D_MODEL), lambda i: (i, 0))
    vec = _full((1, D_MODEL))
    return pl.pallas_call(
        body, name="bwd_in", grid=(seq // tm,),
        out_shape=(jax.ShapeDtypeStruct((seq, D_MODEL), F32),) + (jax.ShapeDtypeStruct((1, D_MODEL), F32),) * 3,
        in_specs=[pl.BlockSpec((tm, N_PAD), lambda i: (i, 0)), _full((N_PAD, D_MODEL)), tile, tile, vec, vec],
        out_specs=(tile, vec, vec, vec),
        compiler_params=_params(("arbitrary",)),
    )(dproj, w_in_t, x, dx2, norm_g, scale)


def _dw_in(h_b, dproj):
    seq = h_b.shape[0]
    tk = min(1024, seq)
    tn = 768
    n_t = seq // tk

    def body(h_ref, dp_ref, dwt_ref, acc):
        t = pl.program_id(1)

        @pl.when(t == 0)
        def _():
            acc[...] = jnp.zeros_like(acc)

        acc[...] += _dot_tn(dp_ref[...], h_ref[...])

        @pl.when(t == n_t - 1)
        def _():
            dwt_ref[...] = acc[...].astype(BF16)

    return pl.pallas_call(
        body, name="dw_in", grid=(N_PAD // tn, n_t),
        out_shape=jax.ShapeDtypeStruct((N_PAD, D_MODEL), BF16),
        in_specs=[pl.BlockSpec((tk, D_MODEL), lambda j, t: (t, 0)), pl.BlockSpec((tk, tn), lambda j, t: (t, j))],
        out_specs=pl.BlockSpec((tn, D_MODEL), lambda j, t: (j, 0)),
        scratch_shapes=[pltpu.VMEM((tn, D_MODEL), F32)],
        compiler_params=_params(("arbitrary", "arbitrary")),
    )(h_b, dproj)


def _adam_update(g, w, m, v, g_ref, d_ref, m_ref, v_ref):
    mn = ADAM_B1 * m + (1.0 - ADAM_B1) * g
    vn = ADAM_B2 * v + (1.0 - ADAM_B2) * (g * g)
    m_hat = mn / (1.0 - ADAM_B1 ** ADAM_STEP)
    v_hat = vn / (1.0 - ADAM_B2 ** ADAM_STEP)
    g_ref[...] = g
    d_ref[...] = -ADAM_LR * (m_hat / (jnp.sqrt(v_hat) + ADAM_EPS) + ADAM_WD * w)
    m_ref[...] = mn
    v_ref[...] = vn


def _adam_sum(name, parts, w, m, v, row_tile, col_tile=None):
    rows, cols = w.shape
    col_tile = cols if col_tile is None else col_tile
    n_parts = parts.shape[0]

    def body(p_ref, w_ref, m_ref, v_ref, g_out, d_out, m_out, v_out):
        g = p_ref[0].astype(F32)
        for j in range(1, n_parts):
            g = g + p_ref[j].astype(F32)
        _adam_update(g, w_ref[...], m_ref[...], v_ref[...], g_out, d_out, m_out, v_out)

    tile = pl.BlockSpec((row_tile, col_tile), lambda i, j: (i, j))
    return pl.pallas_call(
        body, name=name, grid=(rows // row_tile, cols // col_tile),
        out_shape=(jax.ShapeDtypeStruct((rows, cols), F32),) * 4,
        in_specs=[pl.BlockSpec((n_parts, row_tile, col_tile), lambda i, j: (0, i, j)), tile, tile, tile],
        out_specs=(tile,) * 4,
        compiler_params=_params(("arbitrary", "arbitrary")),
    )(parts, w, m, v)


def _adam_ada(sc_all16, dmod_blk16, w, m, v):
    rows, cols = w.shape

    def body(sc_ref, dm_ref, w_ref, m_ref, v_ref, g_out, d_out, m_out, v_out):
        g = _dot_tn(sc_ref[...].astype(BF16), dm_ref[...].astype(BF16))
        _adam_update(g, w_ref[...], m_ref[...], v_ref[...], g_out, d_out, m_out, v_out)

    return pl.pallas_call(
        body, name="adam_w_ada", grid=(1,),
        out_shape=(jax.ShapeDtypeStruct((rows, cols), F32),) * 4,
        in_specs=[_full(sc_all16.shape), _full(dmod_blk16.shape)] + [_full((rows, cols))] * 3,
        out_specs=(_full((rows, cols)),) * 4,
        compiler_params=_params(("arbitrary",)),
    )(sc_all16, dmod_blk16, w, m, v)


def _pack_small(pieces):
    rows = []
    for name, n in PACK_ROWS:
        a = pieces[name].reshape(-1).astype(F32)
        a = jnp.pad(a, (0, n * 128 - a.shape[0]))
        rows.append(a.reshape(n, 128))
    return jnp.concatenate(rows, axis=0)


def _unpack_small(pack, name, shape):
    off, _ = PACK_OFF[name]
    size = 1
    for s in shape:
        size *= s
    n_rows = -(-size // 128)
    return pack[off:off + n_rows].reshape(-1)[:size].reshape(shape)


def _local_step(x2, tgt2, shift, scale, gate, norm_g, w_in_t, w_out_b, conv_w, conv_b, w_pool, ls_pool,
                mh_norm_g, b_gates, final_g, send_dw_out=None, send_dw_in=None):
    bg_pad = jnp.pad(b_gates, ((0, 0), (0, 128 - b_gates.shape[1])))
    conv_w8 = jnp.pad(conv_w, ((0, 8 - CONV_WIDTH), (0, 0)))
    fg = final_g.reshape(1, D_MODEL)

    proj, gates, h_b = _fwd_proj(x2, norm_g, scale, shift, w_in_t)
    mix, cst, nst, mst = _mix_fwd(proj, gates, bg_pad, conv_w8, conv_b, w_pool, ls_pool, mh_norm_g)
    dx2, dmix, dwo, dgate, dfg, loss = _out_fwd_bwd(mix, x2, tgt2, w_out_b, gate, fg)
    if send_dw_out is not None:
        bg_pad = bg_pad + send_dw_out(dwo)
    dproj, dcw8, dcb, dwp, dls, dmhg, dbg = _mix_bwd(proj, gates, dmix, cst, nst, mst, bg_pad, conv_w8, conv_b,
                                                      w_pool, ls_pool, mh_norm_g)
    dw_in_t = _dw_in(h_b, dproj)[:N_IN]
    ng_in = norm_g
    if send_dw_in is not None:
        ng_in = norm_g + send_dw_in(dw_in_t, dcw8[:CONV_WIDTH])
    gx, dsh, dsc, dng = _bwd_in(dproj, w_in_t, x2, dx2, ng_in, scale)
    return dict(loss=loss, grad_x=gx, dw_in_t=dw_in_t, dw_out=dwo, dconv_w=dcw8[:CONV_WIDTH], conv_b=dcb,
                w_pool=dwp, ls_pool=dls, mh_norm_g=dmhg, b_gates=dbg, final_g=dfg, norm_g=dng,
                dmod=jnp.concatenate([dsh, dsc, dgate], axis=1))


def kernel(x, c, norm_g, w_ada, b_ada, w_in, b_gates, conv_w, conv_b, w_pool, ls_pool, mh_norm_g, w_out, final_g, loss_target, m_norm_g, m_w_ada, m_b_ada, m_w_in, m_b_gates, m_conv_w, m_conv_b, m_w_pool, m_ls_pool, m_mh_norm_g, m_w_out, m_final_g, v_norm_g, v_w_ada, v_b_ada, v_w_in, v_b_gates, v_conv_w, v_conv_b, v_w_pool, v_ls_pool, v_mh_norm_g, v_w_out, v_final_g):
    seq = x.shape[1]
    me = 4 * lax.axis_index("x") + 2 * lax.axis_index("y") + lax.axis_index("c")

    g_in, g_out, g_cw, g_c = _all_gather("gather_weights", w_in[0].astype(BF16).T, w_out[0].astype(BF16), conv_w[0], c)
    w_in_t = jnp.pad(g_in.reshape(N_IN, D_MODEL), ((0, N_PAD - N_IN), (0, 0)))
    w_out_b = g_out.reshape(D_MODEL, D_MODEL)
    conv_w_full = jnp.transpose(g_cw, (1, 0, 2)).reshape(CONV_WIDTH, 2 * D_MLSTM)
    c_all16 = jnp.pad(g_c.reshape(N_DEV, D_MODEL), ((0, 8), (0, 0)))

    b_ada_blk = lax.dynamic_slice(b_ada, (0, me * ADA_SHARD), (1, ADA_SHARD))
    mod_all, sc_all16 = _ada_mod(c_all16, w_ada[0], b_ada_blk)
    mod = lax.dynamic_index_in_dim(mod_all, me, axis=1, keepdims=False).reshape(1, 3 * D_MODEL)
    shift, scale, gate = mod[:, :D_MODEL], mod[:, D_MODEL:2 * D_MODEL], mod[:, 2 * D_MODEL:]

    flights = {}

    def send_dw_out(dwo):
        blocks = dwo.astype(BF16).reshape(N_DEV, D_MODEL // N_DEV, D_MODEL)
        flights["out"], token = _scatter_start("send_dw_out", (blocks,))
        return token

    def send_dw_in(dw_in_t, dcw):
        blocks = dw_in_t.reshape(N_DEV, N_SHARD, D_MODEL)
        dcw_blocks = jnp.transpose(dcw.reshape(CONV_WIDTH, N_DEV, 128), (1, 0, 2))
        flights["in"], token = _scatter_start("send_dw_in", (blocks, dcw_blocks))
        return token

    r = _local_step(x[0], loss_target[0], shift, scale, gate, norm_g, w_in_t, w_out_b, conv_w_full, conv_b,
                    w_pool[0], ls_pool, mh_norm_g, b_gates, final_g, send_dw_out, send_dw_in)

    pack = _pack_small(dict(loss=r["loss"][:, :1], final_g=r["final_g"], norm_g=r["norm_g"], conv_b=r["conv_b"],
                            ls_pool=r["ls_pool"], mh_norm_g=r["mh_norm_g"], b_gates=r["b_gates"][:, :8],
                            b_ada=r["dmod"], w_pool=r["w_pool"]))
    (p_pack,) = _all_gather("gather_small", pack)
    p_in, p_cw = _scatter_wait("recv_dw_in", flights["in"], p_pack)
    (p_out,) = _scatter_wait("recv_dw_out", flights["out"], p_cw)

    in_t = _adam_sum("adam_w_in", p_in, w_in[0].T, m_w_in[0].T, v_w_in[0].T, N_SHARD, 256)
    gi, di, mi, vi = (o.T for o in in_t)
    go, do_, mo, vo = _adam_sum("adam_w_out", p_out, w_out[0], m_w_out[0], v_w_out[0], 128)
    gc, dc, mc, vc = _adam_sum("adam_conv_w", p_cw, conv_w[0], m_conv_w[0], v_conv_w[0], CONV_WIDTH)

    def small(loss_like, fg_, ng_, cb_, ls_, mh_, bg_, ba_, wp_):
        return _pack_small(dict(loss=loss_like, final_g=fg_, norm_g=ng_, conv_b=cb_, ls_pool=ls_, mh_norm_g=mh_,
                                b_gates=bg_, b_ada=ba_, w_pool=wp_))

    zero = jnp.zeros((1, 1), F32)
    w_pack = small(zero, final_g, norm_g, conv_b, ls_pool, mh_norm_g, b_gates, b_ada, w_pool)
    m_pack = small(zero, m_final_g, m_norm_g, m_conv_b, m_ls_pool, m_mh_norm_g, m_b_gates, m_b_ada, m_w_pool)
    v_pack = small(zero, v_final_g, v_norm_g, v_conv_b, v_ls_pool, v_mh_norm_g, v_b_gates, v_b_ada, v_w_pool)
    gp, dp, mp, vp = _adam_sum("adam_small", p_pack, w_pack, m_pack, v_pack, PACK_TOTAL)

    off, rows = PACK_OFF["b_ada"]
    dmod_all = p_pack[:, off:off + rows, :].reshape(N_DEV, 3 * D_MODEL)
    dmod_blk16 = jnp.pad(lax.dynamic_slice(dmod_all, (0, me * ADA_SHARD), (N_DEV, ADA_SHARD)), ((0, 8), (0, 0)))
    ga, da, ma, va = _adam_ada(sc_all16, dmod_blk16, w_ada[0], m_w_ada[0], v_w_ada[0])

    names = ("norm_g", "w_ada", "b_ada", "w_in", "b_gates", "conv_w", "conv_b", "w_pool", "ls_pool", "mh_norm_g",
             "w_out", "final_g")
    shapes = dict(norm_g=norm_g.shape, b_ada=b_ada.shape, b_gates=b_gates.shape, conv_b=conv_b.shape,
                  w_pool=w_pool.shape, ls_pool=ls_pool.shape, mh_norm_g=mh_norm_g.shape, final_g=final_g.shape)
    sharded = dict(w_ada=(ga, da, ma, va), w_in=(gi, di, mi, vi), conv_w=(gc, dc, mc, vc), w_out=(go, do_, mo, vo))
    outs = []
    for kind in range(4):
        for nm in names:
            if nm in sharded:
                outs.append(sharded[nm][kind][None])
            else:
                outs.append(_unpack_small((gp, dp, mp, vp)[kind], nm, shapes[nm]))
    loss = gp[0, 0]
    grad_x = r["grad_x"].reshape(1, seq, D_MODEL)
    return (loss, grad_x, *outs)
```

```python
import jax
import jax.numpy as jnp
from jax import lax
from jax.experimental import pallas as pl
from jax.experimental.pallas import tpu as pltpu

F32 = jnp.float32
BF16 = jnp.bfloat16

D_MODEL = 1024
D_POOL = 512
D_MLSTM = 512
N_HEADS = 4
HEAD_DIM = 128
CHUNK = 128
POOL_WINDOWS = (2, 4, 8, 16)
POOL_GROUP_DIM = 128
CONV_WIDTH = 4
EPS = 1e-6
N_MAIN = 3584
N_IN = 3592
N_PAD = 3840
N_SHARD = N_IN // 8
ADA_SHARD = 3 * D_MODEL // 8
N_DEV = 8
CONV_HALO = 8
POOL_HALO = 16
NEG_BIG = -1e30
VMEM_LIMIT_BYTES = 56 * 1024 * 1024

ADAM_LR = 0.001
ADAM_B1 = 0.9
ADAM_B2 = 0.999
ADAM_EPS = 1e-08
ADAM_WD = 0.01
ADAM_STEP = 10

PACK_ROWS = (("loss", 8), ("final_g", 8), ("norm_g", 8), ("conv_b", 8), ("ls_pool", 8),
             ("mh_norm_g", 8), ("b_gates", 8), ("b_ada", 24), ("w_pool", 512))
PACK_TOTAL = sum(r for _, r in PACK_ROWS)


def _pack_offsets():
    off, out = 0, {}
    for name, rows in PACK_ROWS:
        out[name] = (off, rows)
        off += rows
    return out


PACK_OFF = _pack_offsets()


def _dot(a, b):
    return jnp.dot(a, b, preferred_element_type=F32)


def _dot_nt(a, b):
    return lax.dot_general(a, b, (((1,), (1,)), ((), ())), preferred_element_type=F32)


def _dot_tn(a, b):
    return lax.dot_general(a, b, (((0,), (0,)), ((), ())), preferred_element_type=F32)


def _dot_f32(a, b):
    return jnp.dot(a, b, precision=lax.Precision.HIGHEST, preferred_element_type=F32)


def _sigmoid(x):
    return jax.nn.sigmoid(x)


def _log_sigmoid(x):
    return jnp.minimum(x, 0.0) - jnp.log1p(jnp.exp(-jnp.abs(x)))


def _params(sem):
    return pltpu.CompilerParams(dimension_semantics=sem, vmem_limit_bytes=VMEM_LIMIT_BYTES)


def _full(shape):
    n = len(shape)
    return pl.BlockSpec(shape, lambda *_: (0,) * n)


def _mesh_pos():
    return lax.axis_index("x"), lax.axis_index("y"), lax.axis_index("c")


def _peer(k):
    x, y, c = _mesh_pos()
    px = 1 - x if (k >> 2) & 1 else x
    py = 1 - y if (k >> 1) & 1 else y
    pc = 1 - c if k & 1 else c
    return (px, py, pc), 4 * px + 2 * py + pc


def _remote(src, dst, send_sem, recv_sem, to):
    return pltpu.make_async_remote_copy(src_ref=src, dst_ref=dst, send_sem=send_sem, recv_sem=recv_sem, device_id=to,
                                        device_id_type=pl.DeviceIdType.MESH)


def _other_chips():
    x, y, _ = _mesh_pos()
    return [(1 - x, y), (x, 1 - y), (1 - x, 1 - y)]


def _two_level_gather(src, dst, send_sems, recv_sems, local_sems):
    n = len(src)
    x, y, c = _mesh_pos()
    me = 4 * x + 2 * y + c
    sibling = (x, y, 1 - c)
    chips = _other_chips()

    def copy(a, k, block, to, own):
        return _remote(src[a] if own else dst[a].at[block], dst[a].at[block], send_sems.at[a, k], recv_sems.at[a, k], to)

    local = [pltpu.make_async_copy(src[a], dst[a].at[me], local_sems.at[a]) for a in range(n)]
    first = [copy(a, 0, me, sibling, True) for a in range(n)]
    first += [copy(a, 1 + j, me, (*chip, c), True) for j, chip in enumerate(chips) for a in range(n)]
    for cp in local + first:
        cp.start()
    passed = []
    for j, (px, py) in enumerate(chips):
        block = 4 * px + 2 * py + c
        for a in range(n):
            copy(a, 1 + j, block, sibling, False).wait_recv()
            passed.append(copy(a, 4 + j, block, sibling, False))
            passed[-1].start()
    for a in range(n):
        copy(a, 0, 4 * x + 2 * y + (1 - c), sibling, False).wait_recv()
    for j, (px, py) in enumerate(chips):
        for a in range(n):
            copy(a, 4 + j, 4 * px + 2 * py + (1 - c), sibling, False).wait_recv()
    for cp in first + passed:
        cp.wait_send()
    for cp in local:
        cp.wait()


GATHER_COPIES = 7


def _all_gather(name, *shards):
    n = len(shards)

    def body(*refs):
        _two_level_gather(refs[:n], refs[n:2 * n], *refs[2 * n:])

    hbm = pl.BlockSpec(memory_space=pltpu.HBM)
    return pl.pallas_call(
        body, name=name,
        out_shape=tuple(jax.ShapeDtypeStruct((N_DEV,) + s.shape, s.dtype) for s in shards),
        in_specs=[hbm] * n, out_specs=tuple([hbm] * n),
        scratch_shapes=[pltpu.SemaphoreType.DMA((n, GATHER_COPIES)), pltpu.SemaphoreType.DMA((n, GATHER_COPIES)),
                        pltpu.SemaphoreType.DMA((n,))],
    )(*shards)


def _ada_mod(c_all16, w_ada_blk, b_ada_blk):
    def body(c_ref, w_ref, b_ref, out_ref, sc_ref, send_sems, recv_sems):
        x, y, c = _mesh_pos()
        me = 4 * x + 2 * y + c
        cv = c_ref[...]
        sc = cv * _sigmoid(cv)
        sc_ref[...] = sc
        blk = _dot(sc.astype(BF16), w_ref[...].astype(BF16)) + b_ref[...]
        out_ref[me] = blk[0:N_DEV, :]
        copies = []
        for k in range(1, N_DEV):
            peer, _ = _peer(k)
            copies.append(pltpu.make_async_remote_copy(
                src_ref=out_ref.at[me], dst_ref=out_ref.at[me], send_sem=send_sems.at[k - 1],
                recv_sem=recv_sems.at[k - 1], device_id=peer, device_id_type=pl.DeviceIdType.MESH))
        for cp in copies:
            cp.start()
        for cp in copies:
            cp.wait()

    vmem = pl.BlockSpec(memory_space=pltpu.VMEM)
    return pl.pallas_call(
        body, name="ada_mod",
        out_shape=(jax.ShapeDtypeStruct((N_DEV, N_DEV, ADA_SHARD), F32),
                   jax.ShapeDtypeStruct(c_all16.shape, F32)),
        in_specs=[vmem] * 3, out_specs=(vmem, vmem),
        scratch_shapes=[pltpu.SemaphoreType.DMA((N_DEV - 1,)), pltpu.SemaphoreType.DMA((N_DEV - 1,))],
    )(c_all16, w_ada_blk, b_ada_blk)


def _scatter_copies(src, land, send_sems, recv_sems):
    x, y, c = _mesh_pos()
    me = 4 * x + 2 * y + c
    copies = []
    for k in range(1, N_DEV):
        peer, p = _peer(k)
        for a in range(len(src)):
            i = a * (N_DEV - 1) + k - 1
            copies.append(_remote(src[a].at[p], land[a].at[me], send_sems.at[i], recv_sems.at[i], peer))
    return copies


def _scatter_start(name, blocks):
    n = len(blocks)

    def body(*refs):
        src, land = refs[:n], refs[n:2 * n]
        send_sems, recv_sems = refs[2 * n], refs[2 * n + 1]
        token_ref = refs[-1]
        for cp in _scatter_copies(src, land, send_sems, recv_sems):
            cp.start()
        token_ref[...] = jnp.zeros_like(token_ref)

    hbm = pl.BlockSpec(memory_space=pltpu.HBM)
    sem = pl.BlockSpec(memory_space=pltpu.SEMAPHORE)
    through = tuple(pltpu.HBM(b.shape, b.dtype) for b in blocks)
    args = [pltpu.with_memory_space_constraint(b, pltpu.HBM) for b in blocks]
    args += [pltpu.with_memory_space_constraint(lax.empty(b.shape, b.dtype), pltpu.HBM) for b in blocks]
    out = pl.pallas_call(
        body, name=name,
        out_shape=(pltpu.SemaphoreType.DMA((n * (N_DEV - 1),)),) * 2 + through + through
        + (jax.ShapeDtypeStruct((8, 128), F32),),
        in_specs=[hbm] * (2 * n),
        out_specs=(sem, sem) + (hbm,) * (2 * n) + (pl.BlockSpec(memory_space=pltpu.VMEM),),
        input_output_aliases={i: 2 + i for i in range(2 * n)},
        compiler_params=pltpu.CompilerParams(has_side_effects=pltpu.SideEffectType.DATAFLOW_SIDE_EFFECTING),
    )(*args)
    return out[:-1], out[-1][0:1, 0:1]


def _scatter_wait(name, state, after):
    n = (len(state) - 2) // 2
    send_sems, recv_sems = state[0], state[1]
    src, land = state[2:2 + n], state[2 + n:]

    def body(*refs):
        src_r, land_r = refs[:n], refs[n:2 * n]
        for cp in _scatter_copies(src_r, land_r, refs[2 * n], refs[2 * n + 1]):
            cp.wait_send()
            cp.wait_recv()

    hbm = pl.BlockSpec(memory_space=pltpu.HBM)
    sem = pl.BlockSpec(memory_space=pltpu.SEMAPHORE)
    out = pl.pallas_call(
        body, name=name,
        out_shape=tuple(pltpu.HBM(b.shape, b.dtype) for b in src + land),
        in_specs=[hbm] * (2 * n) + [sem, sem, pl.BlockSpec(memory_space=pl.ANY)],
        out_specs=(hbm,) * (2 * n),
        input_output_aliases={i: i for i in range(2 * n)},
        compiler_params=pltpu.CompilerParams(has_side_effects=pltpu.SideEffectType.DATAFLOW_SIDE_EFFECTING),
    )(*src, *land, send_sems, recv_sems, after)
    me = 4 * lax.axis_index("x") + 2 * lax.axis_index("y") + lax.axis_index("c")
    landed = []
    for a in range(n):
        own = lax.dynamic_index_in_dim(out[a], me, axis=0, keepdims=True)
        landed.append(lax.dynamic_update_slice_in_dim(out[n + a], own, me, axis=0))
    return landed


def _fwd_proj(x, norm_g, scale, shift, w_in_t):
    seq = x.shape[0]
    tm = min(512, seq)
    sub = min(256, tm)
    tn = 512

    def body(x_ref, ng_ref, sc_ref, sh_ref, wt_ref, proj_ref, gates_ref, h_ref):
        def chain(n):
            for _ in range(n):
                yield
            rows = slice(n * sub, (n + 1) * sub)
            xt = x_ref[rows, :]
            r = lax.rsqrt(jnp.mean(xt * xt, axis=-1, keepdims=True) + EPS)
            h = ((xt * r) * ng_ref[...]) * (1.0 + sc_ref[...]) + sh_ref[...]
            hb = h.astype(BF16)
            h_ref[rows, :] = hb
            yield
            gates_ref[rows, :] = _dot_nt(hb, wt_ref[N_MAIN:N_MAIN + 128, :])
            for j in range(N_MAIN // tn):
                proj_ref[rows, j * tn:(j + 1) * tn] = _dot_nt(hb, wt_ref[j * tn:(j + 1) * tn, :])

        _in_lockstep(chain(n) for n in range(tm // sub))

    vec = _full((1, D_MODEL))
    tile = pl.BlockSpec((tm, D_MODEL), lambda i: (i, 0))
    return pl.pallas_call(
        body, name="fwd_proj", grid=(seq // tm,),
        out_shape=(jax.ShapeDtypeStruct((seq, N_MAIN), F32), jax.ShapeDtypeStruct((seq, 128), F32),
                   jax.ShapeDtypeStruct((seq, D_MODEL), BF16)),
        in_specs=[tile, vec, vec, vec, _full((N_PAD, D_MODEL))],
        out_specs=(pl.BlockSpec((tm, N_MAIN), lambda i: (i, 0)), pl.BlockSpec((tm, 128), lambda i: (i, 0)), tile),
        compiler_params=_params(("arbitrary",)),
    )(x, norm_g, scale, shift, w_in_t)


def _gate_forms(gpre):
    r = lax.broadcasted_iota(jnp.int32, (CHUNK, CHUNK), 0)
    c = lax.broadcasted_iota(jnp.int32, (CHUNK, CHUNK), 1)
    causal = c <= r
    ltri = jnp.where(causal, 1.0, 0.0).astype(F32)
    utri = jnp.where(r <= c, 1.0, 0.0).astype(F32)
    bcol = _dot_f32(ltri, _log_sigmoid(gpre))
    gt8 = gpre.T[0:8, :]
    brow = _dot_f32(_log_sigmoid(gt8), utri)
    return causal, utri, bcol, gt8, brow


def _in_lockstep(stages):
    alive = list(stages)
    while alive:
        still = []
        for g in alive:
            try:
                next(g)
                still.append(g)
            except StopIteration:
                pass
        alive = still


def _head_fwd(qh, kh, vh, bc, br, igr, m_prev, c_h, n_row, causal):
    qb, kb, vb, cb = qh.astype(BF16), kh.astype(BF16), vh.astype(BF16), c_h.astype(BF16)
    qk = _dot_nt(qb, kb)
    cq = _dot_nt(qb, cb)
    yield
    dlog = jnp.where(causal, bc - br + igr, NEG_BIG)
    inter_log = bc + m_prev
    m_t = jnp.maximum(inter_log, jnp.max(dlog, axis=-1, keepdims=True))
    yield
    dmat = jnp.exp(dlog - m_t)
    inter = jnp.exp(inter_log - m_t)
    s = qk * dmat
    sv = _dot(s.astype(BF16), vb)
    yield
    nq = jnp.sum(qh * n_row, axis=-1, keepdims=True)
    den = jnp.sum(s, axis=-1, keepdims=True) + inter * nq
    emt = jnp.exp(-m_t)
    yield
    num = sv + inter * cq
    dn = jnp.maximum(jnp.abs(den), emt)
    hm = num / dn
    return dict(dmat=dmat, inter=inter, qb=qb, kb=kb, vb=vb, cb=cb, s=s, cq=cq, nq=nq, den=den, emt=emt,
                dn=dn, hm=hm)


def _state_weights(bc, igc, m_prev, m_new=None):
    last = lax.broadcasted_iota(jnp.int32, (CHUNK, 1), 0) == CHUNK - 1
    b_last = jnp.sum(jnp.where(last, bc, 0.0), axis=0, keepdims=True)
    wlog = b_last - bc + igc
    if m_new is None:
        m_new = jnp.maximum(b_last + m_prev, jnp.max(wlog, axis=0, keepdims=True))
    w_c = jnp.exp(wlog - m_new)
    decay = jnp.exp(b_last + m_prev - m_new)
    return w_c, decay, m_new, last


def _rows_back(x, k):
    return x if k == 0 else pltpu.roll(x, k, 0)


def _rows_ahead(x, k):
    return x if k == 0 else pltpu.roll(x, x.shape[0] - k, 0)


def _conv_taps(xpad):
    return [_rows_back(xpad, CONV_WIDTH - 1 - j)[CONV_HALO:, :] for j in range(CONV_WIDTH)]


def _conv_pre(taps, cw_ref, cb_ref):
    a = cb_ref[...]
    for j in range(CONV_WIDTH):
        a = a + cw_ref[j:j + 1, :] * taps[j]
    return a


def _window_sum(x, w, shift):
    k = 1
    while k < w:
        x = x + shift(x, k)
        k *= 2
    return x


def _pool_window_sum(upad_ref, g, w):
    lanes = slice(g * POOL_GROUP_DIM, (g + 1) * POOL_GROUP_DIM)
    return _window_sum(upad_ref[:, lanes], w, _rows_back)[POOL_HALO:, :]


def _pool_inv_count(chunk_idx, w):
    pos = chunk_idx * CHUNK + lax.broadcasted_iota(jnp.int32, (CHUNK, 1), 0) + 1
    return 1.0 / jnp.minimum(pos, w).astype(F32)


def _mixer_in_specs(cmap, n_chunks):
    def rows(i):
        return cmap(i)
    return [
        pl.BlockSpec((CHUNK, 1024), lambda i: (rows(i), 0)),
        pl.BlockSpec((CHUNK, 1024), lambda i: (rows(i), 1)),
        pl.BlockSpec((CHUNK, 512), lambda i: (rows(i), 4)),
        pl.BlockSpec((CHUNK, 512), lambda i: (rows(i), 5)),
        pl.BlockSpec((CHUNK, 512), lambda i: (rows(i), 6)),
        pl.BlockSpec((POOL_HALO, 512), lambda i: (jnp.maximum(rows(i) * (CHUNK // POOL_HALO) - 1, 0), 0)),
        pl.BlockSpec((CONV_HALO, 1024), lambda i: (jnp.maximum(rows(i) * (CHUNK // CONV_HALO) - 1, 0), 1)),
    ]


def _mix_fwd(proj, gates, bg_pad, conv_w8, conv_b, w_pool, ls_pool, mh_g):
    seq = proj.shape[0]
    n_chunks = seq // CHUNK

    def body(uz_ref, qk_ref, v_ref, o_ref, zm_ref, uh_ref, qkh_ref, g_ref, bg_ref, cw_ref, cb_ref, wp_ref,
             ls_ref, mhg_ref, mix_ref, cst_ref, nst_ref, mst_ref, a_ref, pooled_ref, c_scr, n_scr, m_scr, xpad, upad):
        i = pl.program_id(0)

        @pl.when(i == 0)
        def _():
            c_scr[...] = jnp.zeros_like(c_scr)
            n_scr[...] = jnp.zeros_like(n_scr)
            m_scr[...] = jnp.zeros_like(m_scr)

        cst_ref[0] = c_scr[...]
        nst_ref[0] = n_scr[...]
        mst_ref[0] = m_scr[...]
        first = i == 0

        upad[0:POOL_HALO, :] = jnp.where(first, 0.0, uh_ref[...])
        upad[POOL_HALO:POOL_HALO + CHUNK, :] = uz_ref[:, 0:D_POOL]
        for g, w in enumerate(POOL_WINDOWS):
            lanes = slice(g * POOL_GROUP_DIM, (g + 1) * POOL_GROUP_DIM)
            pooled = (_pool_window_sum(upad, g, w) * _pool_inv_count(i, w) - uz_ref[:, lanes]).astype(BF16)
            pooled_ref[:, lanes] = pooled
            y = _dot(pooled, wp_ref[g].astype(BF16)) * ls_ref[:, lanes]
            zp = uz_ref[:, D_POOL + g * POOL_GROUP_DIM:D_POOL + (g + 1) * POOL_GROUP_DIM]
            mix_ref[:, lanes] = (y * (zp * _sigmoid(zp))).astype(BF16)

        xpad[0:CONV_HALO, :] = jnp.where(first, 0.0, qkh_ref[...])
        xpad[CONV_HALO:CONV_HALO + CHUNK, :] = qk_ref[...]
        a = _conv_pre(_conv_taps(xpad[...]), cw_ref, cb_ref)
        a_ref[...] = a
        qk = a * _sigmoid(a)

        gpre = g_ref[...] + bg_ref[...]
        causal, _, bcol, gt8, brow = _gate_forms(gpre)
        def head(h):
            lanes = slice(h * HEAD_DIM, (h + 1) * HEAD_DIM)
            qh = qk[:, lanes]
            kh = qk[:, D_MLSTM + h * HEAD_DIM:D_MLSTM + (h + 1) * HEAD_DIM] * (HEAD_DIM ** -0.5)
            vh = v_ref[:, lanes]
            bc = bcol[:, N_HEADS + h:N_HEADS + h + 1]
            br = brow[N_HEADS + h:N_HEADS + h + 1, :]
            igr = gt8[h:h + 1, :]
            igc = gpre[:, h:h + 1]
            m_prev = m_scr[h:h + 1, 0:1]
            c_h = c_scr[h]
            n_row = n_scr[h:h + 1, :]
            w_c, decay, m_new, _ = _state_weights(bc, igc, m_prev)
            c_scr[h] = decay * c_h + _dot_tn((vh * w_c).astype(BF16), kh.astype(BF16))
            n_scr[h:h + 1, :] = decay * n_row + jnp.sum(w_c * kh, axis=0, keepdims=True)
            m_scr[h:h + 1, :] = jnp.broadcast_to(m_new, (1, 128))
            f = yield from _head_fwd(qh, kh, vh, bc, br, igr, m_prev, c_h, n_row, causal)
            yield
            hm = f["hm"]
            hn = hm * lax.rsqrt(jnp.mean(hm * hm, axis=-1, keepdims=True) + EPS) * mhg_ref[:, lanes]
            zm = zm_ref[:, lanes]
            out = hn * _sigmoid(o_ref[:, lanes]) * (zm * _sigmoid(zm))
            mix_ref[:, D_POOL + h * HEAD_DIM:D_POOL + (h + 1) * HEAD_DIM] = out.astype(BF16)

        _in_lockstep(head(h) for h in range(N_HEADS))

    cmap = lambda i: i
    in_specs = _mixer_in_specs(cmap, n_chunks) + [
        pl.BlockSpec((CHUNK, 128), lambda i: (i, 0)),
        _full((1, 128)), _full((8, 1024)), _full((1, 1024)), _full((4, 128, 128)), _full((1, 512)),
        _full((1, 512))]
    return pl.pallas_call(
        body, name="mix_fwd", grid=(n_chunks,),
        out_shape=(jax.ShapeDtypeStruct((seq, D_MODEL), BF16),
                   jax.ShapeDtypeStruct((n_chunks, N_HEADS, HEAD_DIM, HEAD_DIM), F32),
                   jax.ShapeDtypeStruct((n_chunks, 8, 128), F32),
                   jax.ShapeDtypeStruct((n_chunks, 8, 128), F32),
                   jax.ShapeDtypeStruct((seq, 2 * D_MLSTM), F32),
                   jax.ShapeDtypeStruct((seq, D_POOL), BF16)),
        in_specs=in_specs,
        out_specs=(pl.BlockSpec((CHUNK, D_MODEL), lambda i: (i, 0)),
                   pl.BlockSpec((1, N_HEADS, HEAD_DIM, HEAD_DIM), lambda i: (i, 0, 0, 0)),
                   pl.BlockSpec((1, 8, 128), lambda i: (i, 0, 0)),
                   pl.BlockSpec((1, 8, 128), lambda i: (i, 0, 0)),
                   pl.BlockSpec((CHUNK, 2 * D_MLSTM), lambda i: (i, 0)),
                   pl.BlockSpec((CHUNK, D_POOL), lambda i: (i, 0))),
        scratch_shapes=[pltpu.VMEM((N_HEADS, HEAD_DIM, HEAD_DIM), F32), pltpu.VMEM((8, 128), F32),
                        pltpu.VMEM((8, 128), F32), pltpu.VMEM((CONV_HALO + CHUNK, 1024), F32),
                        pltpu.VMEM((POOL_HALO + CHUNK, D_POOL), F32)],
        compiler_params=_params(("arbitrary",)),
    )(proj, proj, proj, proj, proj, proj, proj, gates, bg_pad, conv_w8, conv_b, w_pool, ls_pool, mh_g)


def _out_fwd_bwd(mix, x, tgt, w_out_b, gate, final_g):
    seq = x.shape[0]
    tm = min(512, seq)
    sub = min(256, tm)

    def body(mix_ref, x_ref, t_ref, w_ref, gate_ref, fg_ref, dx2_ref, dmix_ref, dwo_ref, dgate_ref, dfg_ref,
             loss_ref):
        @pl.when(pl.program_id(0) == 0)
        def _():
            dwo_ref[...] = jnp.zeros_like(dwo_ref)
            dgate_ref[...] = jnp.zeros_like(dgate_ref)
            dfg_ref[...] = jnp.zeros_like(dfg_ref)
            loss_ref[...] = jnp.zeros_like(loss_ref)

        w = w_ref[...]
        gate_v = gate_ref[...]
        fg = fg_ref[...]
        do2_parts = [None] * (tm // sub)

        def chain(n):
            rows = slice(n * sub, (n + 1) * sub)
            o2 = _dot(mix_ref[rows, :], w)
            yield
            x2 = x_ref[rows, :] + gate_v * o2
            r2 = lax.rsqrt(jnp.mean(x2 * x2, axis=-1, keepdims=True) + EPS)
            x2n = x2 * r2
            err = x2n * fg - t_ref[rows, :]
            part = 0.5 * jnp.sum(jnp.sum(err * err, axis=-1, keepdims=True), axis=0, keepdims=True) / D_MODEL
            loss_ref[...] += jnp.broadcast_to(part, loss_ref.shape)
            dy = err / D_MODEL
            dfg_ref[...] += jnp.sum(dy * x2n, axis=0, keepdims=True)
            gdy = dy * fg
            dx2 = r2 * (gdy - x2n * jnp.mean(gdy * x2n, axis=-1, keepdims=True))
            dx2_ref[rows, :] = dx2
            dgate_ref[...] += jnp.sum(dx2 * o2, axis=0, keepdims=True)
            do2 = (dx2 * gate_v).astype(BF16)
            dmix_ref[rows, :] = _dot_nt(do2, w)
            do2_parts[n] = do2

        _in_lockstep(chain(n) for n in range(tm // sub))
        dwo_ref[...] += _dot_tn(mix_ref[...], jnp.concatenate(do2_parts, axis=0))

    tile = pl.BlockSpec((tm, D_MODEL), lambda i: (i, 0))
    vec = _full((1, D_MODEL))
    return pl.pallas_call(
        body, name="out_fwd_bwd", grid=(seq // tm,),
        out_shape=(jax.ShapeDtypeStruct((seq, D_MODEL), F32), jax.ShapeDtypeStruct((seq, D_MODEL), F32),
                   jax.ShapeDtypeStruct((D_MODEL, D_MODEL), F32), jax.ShapeDtypeStruct((1, D_MODEL), F32),
                   jax.ShapeDtypeStruct((1, D_MODEL), F32), jax.ShapeDtypeStruct((1, 128), F32)),
        in_specs=[tile, tile, tile, _full((D_MODEL, D_MODEL)), vec, vec],
        out_specs=(tile, tile, _full((D_MODEL, D_MODEL)), vec, vec, _full((1, 128))),
        compiler_params=_params(("arbitrary",)),
    )(mix, x, tgt, w_out_b, gate, final_g)


def _mix_bwd(proj, gates, dmix, conv_a, pooled, cst, nst, mst, bg_pad, conv_w8, w_pool, ls_pool, mh_g):
    seq = proj.shape[0]
    n_chunks = seq // CHUNK

    def body(zp_ref, qk_ref, v_ref, o_ref, zm_ref, g_ref, dmix_ref, a_ref, pooled_ref, cst_ref, nst_ref, mst_ref,
             mnx_ref, bg_ref, cw_ref, wp_ref, ls_ref, mhg_ref,
             dp_ref, dcw_ref, dcb_ref, dwp_ref, dls_ref, dmhg_ref, dbg_ref,
             dc_scr, dn_scr, dapad, dpipad):
        i = pl.program_id(0)
        ci = n_chunks - 1 - i

        @pl.when(i == 0)
        def _():
            for ref in (dc_scr, dn_scr, dcw_ref, dcb_ref, dwp_ref, dls_ref, dmhg_ref, dbg_ref):
                ref[...] = jnp.zeros_like(ref)
            dapad[CHUNK:CHUNK + CONV_HALO, :] = jnp.zeros((CONV_HALO, 1024), F32)
            dpipad[CHUNK:CHUNK + POOL_HALO, :] = jnp.zeros((POOL_HALO, D_POOL), F32)

        dpooled = []
        for g, w in enumerate(POOL_WINDOWS):
            lanes = slice(g * POOL_GROUP_DIM, (g + 1) * POOL_GROUP_DIM)
            zlanes = slice(D_POOL + g * POOL_GROUP_DIM, D_POOL + (g + 1) * POOL_GROUP_DIM)
            inv = _pool_inv_count(ci, w)
            pb = pooled_ref[:, lanes]
            wpb = wp_ref[g].astype(BF16)
            yw = _dot(pb, wpb)
            ls = ls_ref[:, lanes]
            zp = zp_ref[:, lanes]
            sg = _sigmoid(zp)
            dpo = dmix_ref[:, lanes]
            dp_ref[:, zlanes] = (dpo * (yw * ls) * (sg * (1.0 + zp * (1.0 - sg)))).astype(BF16)
            dy = dpo * (zp * sg)
            dls_ref[:, lanes] += jnp.sum(dy * yw, axis=0, keepdims=True)
            dyw = (dy * ls).astype(BF16)
            dwp_ref[g] += _dot_tn(pb, dyw)
            dpl = _dot_nt(dyw, wpb)
            dpooled.append(dpl)
            dpipad[0:CHUNK, lanes] = dpl * inv
        for g, w in enumerate(POOL_WINDOWS):
            lanes = slice(g * POOL_GROUP_DIM, (g + 1) * POOL_GROUP_DIM)
            du = _window_sum(dpipad[:, lanes], w, _rows_ahead)[0:CHUNK, :] - dpooled[g]
            dp_ref[:, lanes] = du.astype(BF16)
        dpipad[CHUNK:CHUNK + POOL_HALO, :] = dpipad[0:POOL_HALO, :]

        a = a_ref[...]
        sga = _sigmoid(a)
        qk = a * sga
        dsilu_a = sga * (1.0 + a * (1.0 - sga))

        gpre = g_ref[...] + bg_ref[...]
        causal, utri, bcol, gt8, brow = _gate_forms(gpre)
        lane = lax.broadcasted_iota(jnp.int32, (CHUNK, 128), 1)
        row = lax.broadcasted_iota(jnp.int32, (CHUNK, 128), 0)
        col_g_rows, dig_parts, db_parts = [], [], []
        scale_k = HEAD_DIM ** -0.5

        def head(h):
            lanes = slice(h * HEAD_DIM, (h + 1) * HEAD_DIM)
            klanes = slice(D_MLSTM + h * HEAD_DIM, D_MLSTM + (h + 1) * HEAD_DIM)
            qh = qk[:, lanes]
            kh = qk[:, klanes] * scale_k
            vh = v_ref[:, lanes]
            bc = bcol[:, N_HEADS + h:N_HEADS + h + 1]
            br = brow[N_HEADS + h:N_HEADS + h + 1, :]
            igr = gt8[h:h + 1, :]
            igc = gpre[:, h:h + 1]
            m_prev = mst_ref[0, h:h + 1, 0:1]
            m_next = mnx_ref[0, h:h + 1, 0:1]
            c_h = cst_ref[0, h]
            n_row = nst_ref[0, h:h + 1, :]
            w_c, decay, _, last = _state_weights(bc, igc, m_prev, m_next)
            dcn = dc_scr[h]
            dnn = dn_scr[h:h + 1, :]
            dcnb = dcn.astype(BF16)
            vb0, kb0 = vh.astype(BF16), kh.astype(BF16)
            amat = _dot(vb0, dcnb) + dnn
            kdc = _dot_nt(kb0, dcnb)
            ddecay = (jnp.sum(jnp.sum(dcn * c_h, axis=-1, keepdims=True), axis=0, keepdims=True)
                      + jnp.sum(dnn * n_row, axis=-1, keepdims=True))
            f = yield from _head_fwd(qh, kh, vh, bc, br, igr, m_prev, c_h, n_row, causal)
            qb, kb, vb, cb = f["qb"], f["kb"], f["vb"], f["cb"]
            s, dmat, inter, den, dn, hm = f["s"], f["dmat"], f["inter"], f["den"], f["dn"], f["hm"]
            yield

            rinv = lax.rsqrt(jnp.mean(hm * hm, axis=-1, keepdims=True) + EPS)
            hmn = hm * rinv
            gh = mhg_ref[:, lanes]
            o_pre = o_ref[:, lanes]
            og = _sigmoid(o_pre)
            zm = zm_ref[:, lanes]
            sgz = _sigmoid(zm)
            sz = zm * sgz
            dout = dmix_ref[:, D_POOL + h * HEAD_DIM:D_POOL + (h + 1) * HEAD_DIM]
            hn = hmn * gh
            dp_ref[:, 2560 + h * HEAD_DIM:2560 + (h + 1) * HEAD_DIM] = (
                dout * hn * sz * og * (1.0 - og)).astype(BF16)
            dp_ref[:, 3072 + h * HEAD_DIM:3072 + (h + 1) * HEAD_DIM] = (
                dout * hn * og * (sgz * (1.0 + zm * (1.0 - sgz)))).astype(BF16)
            dhn = dout * og * sz
            dmhg_ref[:, lanes] += jnp.sum(dhn * hmn, axis=0, keepdims=True)
            dyn = dhn * gh
            dhm = rinv * (dyn - hmn * jnp.mean(dyn * hmn, axis=-1, keepdims=True))
            yield

            inv_dn = 1.0 / dn
            dnum = dhm * inv_dn
            hd = jnp.sum(dhm * hm, axis=-1, keepdims=True)
            dden = jnp.where(jnp.abs(den) > f["emt"], -hd * inv_dn * jnp.sign(den), 0.0)
            dnb = dnum.astype(BF16)
            dnv = _dot_nt(dnb, vb)
            dv = _dot_tn(s.astype(BF16), dnb)
            dnc = _dot(dnb, cb)
            dc_prev = _dot_tn((inter * dnum).astype(BF16), qb)
            yield
            ds = dnv + dden
            dqk = (ds * dmat).astype(BF16)
            dqk_k = _dot(dqk, kb)
            dk = _dot_tn(dqk, qb)
            yield
            gmat = ds * s
            row_g = jnp.sum(gmat, axis=-1, keepdims=True)
            col_g_rows.append(jnp.where(row == h, jnp.sum(gmat, axis=0, keepdims=True), 0.0))
            gcol = inter * (jnp.sum(dnum * f["cq"], axis=-1, keepdims=True) + dden * f["nq"])
            dn_prev = jnp.sum((inter * dden) * qh, axis=0, keepdims=True)
            dw = jnp.sum(amat * kh, axis=-1, keepdims=True)
            e = dw * w_c
            db_last = ddecay * decay + jnp.sum(e, axis=0, keepdims=True)
            dig_parts.append(jnp.where(lane == h, e, 0.0))
            db_parts.append(jnp.where(lane == N_HEADS + h, row_g + gcol - e + jnp.where(last, db_last, 0.0), 0.0))
            dc_scr[h] = decay * dcn + dc_prev
            dn_scr[h:h + 1, :] = decay * dnn + dn_prev
            yield
            dq = dqk_k + inter * (dnc + dden * n_row)
            dp_ref[:, 2048 + h * HEAD_DIM:2048 + (h + 1) * HEAD_DIM] = (dv + w_c * kdc).astype(BF16)
            dapad[0:CHUNK, lanes] = dq * dsilu_a[:, lanes]
            dapad[0:CHUNK, klanes] = (dk + w_c * amat) * scale_k * dsilu_a[:, klanes]

        _in_lockstep(head(h) for h in range(N_HEADS))

        cs_t = sum(col_g_rows[1:], col_g_rows[0]).T
        dig_all = sum(dig_parts[1:], dig_parts[0]) + cs_t
        db_cols = sum(db_parts[1:], db_parts[0])
        shifted = jnp.zeros((CHUNK, 128), F32)
        for h in range(N_HEADS):
            shifted = shifted + jnp.where(lane == N_HEADS + h, cs_t[:, h:h + 1], 0.0)
        dlf = _dot_f32(utri, db_cols - shifted)
        dgates = dig_all + dlf * _sigmoid(-gpre)
        dp_ref[:, N_MAIN:N_MAIN + 128] = dgates.astype(BF16)
        dp_ref[:, N_MAIN + 128:N_PAD] = jnp.zeros((CHUNK, N_PAD - N_MAIN - 128), BF16)
        dbg_ref[...] += jnp.sum(dgates, axis=0, keepdims=True)

        da_pad = dapad[...]
        da = da_pad[0:CHUNK, :]
        dcb_ref[...] += jnp.sum(da, axis=0, keepdims=True)
        x = qk_ref[...]
        dx = jnp.zeros((CHUNK, 1024), F32)
        for j in range(CONV_WIDTH):
            da_j = _rows_ahead(da_pad, CONV_WIDTH - 1 - j)[0:CHUNK, :]
            dcw_ref[j:j + 1, :] += jnp.sum(da_j * x, axis=0, keepdims=True)
            dx = dx + cw_ref[j:j + 1, :] * da_j
        dp_ref[:, 1024:2048] = dx.astype(BF16)
        dapad[CHUNK:CHUNK + CONV_HALO, :] = dapad[0:CONV_HALO, :]

    cmap = lambda i: n_chunks - 1 - i
    wide = pl.BlockSpec((CHUNK, 1024), lambda i: (cmap(i), 0))
    state = pl.BlockSpec((1, 8, 128), lambda i: (cmap(i), 0, 0))
    in_specs = [
        pl.BlockSpec((CHUNK, 512), lambda i: (cmap(i), 1)),
        pl.BlockSpec((CHUNK, 1024), lambda i: (cmap(i), 1)),
        pl.BlockSpec((CHUNK, 512), lambda i: (cmap(i), 4)),
        pl.BlockSpec((CHUNK, 512), lambda i: (cmap(i), 5)),
        pl.BlockSpec((CHUNK, 512), lambda i: (cmap(i), 6)),
        pl.BlockSpec((CHUNK, 128), lambda i: (cmap(i), 0)),
        wide, wide,
        pl.BlockSpec((CHUNK, D_POOL), lambda i: (cmap(i), 0)),
        pl.BlockSpec((1, N_HEADS, HEAD_DIM, HEAD_DIM), lambda i: (cmap(i), 0, 0, 0)),
        state, state,
        pl.BlockSpec((1, 8, 128), lambda i: (jnp.minimum(cmap(i) + 1, n_chunks - 1), 0, 0)),
        _full((1, 128)), _full((8, 1024)), _full((4, 128, 128)), _full((1, 512)), _full((1, 512))]
    return pl.pallas_call(
        body, name="mix_bwd", grid=(n_chunks,),
        out_shape=(jax.ShapeDtypeStruct((seq, N_PAD), BF16), jax.ShapeDtypeStruct((8, 1024), F32),
                   jax.ShapeDtypeStruct((1, 1024), F32), jax.ShapeDtypeStruct((4, 128, 128), F32),
                   jax.ShapeDtypeStruct((1, 512), F32), jax.ShapeDtypeStruct((1, 512), F32),
                   jax.ShapeDtypeStruct((1, 128), F32)),
        in_specs=in_specs,
        out_specs=(pl.BlockSpec((CHUNK, N_PAD), lambda i: (cmap(i), 0)), _full((8, 1024)), _full((1, 1024)),
                   _full((4, 128, 128)), _full((1, 512)), _full((1, 512)), _full((1, 128))),
        scratch_shapes=[pltpu.VMEM((N_HEADS, HEAD_DIM, HEAD_DIM), F32), pltpu.VMEM((8, 128), F32),
                        pltpu.VMEM((CHUNK + CONV_HALO, 1024), F32), pltpu.VMEM((CHUNK + POOL_HALO, D_POOL), F32)],
        compiler_params=_params(("arbitrary",)),
    )(proj, proj, proj, proj, proj, gates, dmix, conv_a, pooled, cst, nst, mst, mst, bg_pad, conv_w8,
      w_pool, ls_pool, mh_g)


def _bwd_in(dproj, w_in_t, x, dx2, norm_g, scale):
    seq = x.shape[0]
    tm = min(512, seq)
    sub = min(256, tm)

    def body(dp_ref, wt_ref, x_ref, dx2_ref, ng_ref, sc_ref, gx_ref, dsh_ref, dsc_ref, dng_ref):
        @pl.when(pl.program_id(0) == 0)
        def _():
            dsh_ref[...] = jnp.zeros_like(dsh_ref)
            dsc_ref[...] = jnp.zeros_like(dsc_ref)
            dng_ref[...] = jnp.zeros_like(dng_ref)

        ng = ng_ref[...]
        one_sc = 1.0 + sc_ref[...]

        def chain(n):
            rows = slice(n * sub, (n + 1) * sub)
            dh = _dot(dp_ref[rows, :], wt_ref[...])
            yield
            xt = x_ref[rows, :]
            r = lax.rsqrt(jnp.mean(xt * xt, axis=-1, keepdims=True) + EPS)
            xn = xt * r
            dsh_ref[...] += jnp.sum(dh, axis=0, keepdims=True)
            dhxn = dh * xn
            dsc_ref[...] += jnp.sum(dhxn * ng, axis=0, keepdims=True)
            dng_ref[...] += jnp.sum(dhxn * one_sc, axis=0, keepdims=True)
            dxn = dh * (ng * one_sc)
            gx_ref[rows, :] = r * (dxn - xn * jnp.mean(dxn * xn, axis=-1, keepdims=True)) + dx2_ref[rows, :]

        _in_lockstep(chain(n) for n in range(tm // sub))

    tile = pl.BlockSpec((tm, D_MODEL), lambda i: (i, 0))
    vec = _full((1, D_MODEL))
    return pl.pallas_call(
        body, name="bwd_in", grid=(seq // tm,),
        out_shape=(jax.ShapeDtypeStruct((seq, D_MODEL), F32),) + (jax.ShapeDtypeStruct((1, D_MODEL), F32),) * 3,
        in_specs=[pl.BlockSpec((tm, N_PAD), lambda i: (i, 0)), _full((N_PAD, D_MODEL)), tile, tile, vec, vec],
        out_specs=(tile, vec, vec, vec),
        compiler_params=_params(("arbitrary",)),
    )(dproj, w_in_t, x, dx2, norm_g, scale)


def _dw_in(h_b, dproj):
    seq = h_b.shape[0]
    tk = min(1024, seq)
    tn = 768
    n_t = seq // tk

    def body(h_ref, dp_ref, dwt_ref, acc):
        t = pl.program_id(1)

        @pl.when(t == 0)
        def _():
            acc[...] = jnp.zeros_like(acc)

        acc[...] += _dot_tn(dp_ref[...], h_ref[...])

        @pl.when(t == n_t - 1)
        def _():
            dwt_ref[...] = acc[...].astype(BF16)

    return pl.pallas_call(
        body, name="dw_in", grid=(N_PAD // tn, n_t),
        out_shape=jax.ShapeDtypeStruct((N_PAD, D_MODEL), BF16),
        in_specs=[pl.BlockSpec((tk, D_MODEL), lambda j, t: (t, 0)), pl.BlockSpec((tk, tn), lambda j, t: (t, j))],
        out_specs=pl.BlockSpec((tn, D_MODEL), lambda j, t: (j, 0)),
        scratch_shapes=[pltpu.VMEM((tn, D_MODEL), F32)],
        compiler_params=_params(("arbitrary", "arbitrary")),
    )(h_b, dproj)


def _adam_update(g, w, m, v, g_ref, d_ref, m_ref, v_ref):
    mn = ADAM_B1 * m + (1.0 - ADAM_B1) * g
    vn = ADAM_B2 * v + (1.0 - ADAM_B2) * (g * g)
    m_hat = mn / (1.0 - ADAM_B1 ** ADAM_STEP)
    v_hat = vn / (1.0 - ADAM_B2 ** ADAM_STEP)
    g_ref[...] = g
    d_ref[...] = -ADAM_LR * (m_hat / (jnp.sqrt(v_hat) + ADAM_EPS) + ADAM_WD * w)
    m_ref[...] = mn
    v_ref[...] = vn


def _adam_sum(name, parts, w, m, v, row_tile, col_tile=None):
    rows, cols = w.shape
    col_tile = cols if col_tile is None else col_tile
    n_parts = parts.shape[0]

    def body(p_ref, w_ref, m_ref, v_ref, g_out, d_out, m_out, v_out):
        g = p_ref[0].astype(F32)
        for j in range(1, n_parts):
            g = g + p_ref[j].astype(F32)
        _adam_update(g, w_ref[...], m_ref[...], v_ref[...], g_out, d_out, m_out, v_out)

    tile = pl.BlockSpec((row_tile, col_tile), lambda i, j: (i, j))
    return pl.pallas_call(
        body, name=name, grid=(rows // row_tile, cols // col_tile),
        out_shape=(jax.ShapeDtypeStruct((rows, cols), F32),) * 4,
        in_specs=[pl.BlockSpec((n_parts, row_tile, col_tile), lambda i, j: (0, i, j)), tile, tile, tile],
        out_specs=(tile,) * 4,
        compiler_params=_params(("arbitrary", "arbitrary")),
    )(parts, w, m, v)


def _adam_ada(sc_all16, dmod_blk16, w, m, v):
    rows, cols = w.shape

    def body(sc_ref, dm_ref, w_ref, m_ref, v_ref, g_out, d_out, m_out, v_out):
        g = _dot_tn(sc_ref[...].astype(BF16), dm_ref[...].astype(BF16))
        _adam_update(g, w_ref[...], m_ref[...], v_ref[...], g_out, d_out, m_out, v_out)

    return pl.pallas_call(
        body, name="adam_w_ada", grid=(1,),
        out_shape=(jax.ShapeDtypeStruct((rows, cols), F32),) * 4,
        in_specs=[_full(sc_all16.shape), _full(dmod_blk16.shape)] + [_full((rows, cols))] * 3,
        out_specs=(_full((rows, cols)),) * 4,
        compiler_params=_params(("arbitrary",)),
    )(sc_all16, dmod_blk16, w, m, v)


def _pack_small(pieces):
    rows = []
    for name, n in PACK_ROWS:
        a = pieces[name].reshape(-1).astype(F32)
        a = jnp.pad(a, (0, n * 128 - a.shape[0]))
        rows.append(a.reshape(n, 128))
    return jnp.concatenate(rows, axis=0)


def _unpack_small(pack, name, shape):
    off, _ = PACK_OFF[name]
    size = 1
    for s in shape:
        size *= s
    n_rows = -(-size // 128)
    return pack[off:off + n_rows].reshape(-1)[:size].reshape(shape)


def _local_step(x2, tgt2, shift, scale, gate, norm_g, w_in_t, w_out_b, conv_w, conv_b, w_pool, ls_pool,
                mh_norm_g, b_gates, final_g, send_dw_out=None, send_dw_in=None):
    bg_pad = jnp.pad(b_gates, ((0, 0), (0, 128 - b_gates.shape[1])))
    conv_w8 = jnp.pad(conv_w, ((0, 8 - CONV_WIDTH), (0, 0)))
    fg = final_g.reshape(1, D_MODEL)

    proj, gates, h_b = _fwd_proj(x2, norm_g, scale, shift, w_in_t)
    mix, cst, nst, mst, conv_a, pooled = _mix_fwd(proj, gates, bg_pad, conv_w8, conv_b, w_pool, ls_pool, mh_norm_g)
    dx2, dmix, dwo, dgate, dfg, loss = _out_fwd_bwd(mix, x2, tgt2, w_out_b, gate, fg)
    if send_dw_out is not None:
        bg_pad = bg_pad + send_dw_out(dwo)
    dproj, dcw8, dcb, dwp, dls, dmhg, dbg = _mix_bwd(proj, gates, dmix, conv_a, pooled, cst, nst, mst, bg_pad,
                                                      conv_w8, w_pool, ls_pool, mh_norm_g)
    dw_in_t = _dw_in(h_b, dproj)[:N_IN]
    ng_in = norm_g
    if send_dw_in is not None:
        ng_in = norm_g + send_dw_in(dw_in_t, dcw8[:CONV_WIDTH])
    gx, dsh, dsc, dng = _bwd_in(dproj, w_in_t, x2, dx2, ng_in, scale)
    return dict(loss=loss, grad_x=gx, dw_in_t=dw_in_t, dw_out=dwo, dconv_w=dcw8[:CONV_WIDTH], conv_b=dcb,
                w_pool=dwp, ls_pool=dls, mh_norm_g=dmhg, b_gates=dbg, final_g=dfg, norm_g=dng,
                dmod=jnp.concatenate([dsh, dsc, dgate], axis=1))


def kernel(x, c, norm_g, w_ada, b_ada, w_in, b_gates, conv_w, conv_b, w_pool, ls_pool, mh_norm_g, w_out, final_g, loss_target, m_norm_g, m_w_ada, m_b_ada, m_w_in, m_b_gates, m_conv_w, m_conv_b, m_w_pool, m_ls_pool, m_mh_norm_g, m_w_out, m_final_g, v_norm_g, v_w_ada, v_b_ada, v_w_in, v_b_gates, v_conv_w, v_conv_b, v_w_pool, v_ls_pool, v_mh_norm_g, v_w_out, v_final_g):
    seq = x.shape[1]
    me = 4 * lax.axis_index("x") + 2 * lax.axis_index("y") + lax.axis_index("c")

    g_in, g_out, g_cw, g_c = _all_gather("gather_weights", w_in[0].astype(BF16).T, w_out[0].astype(BF16), conv_w[0], c)
    w_in_t = jnp.pad(g_in.reshape(N_IN, D_MODEL), ((0, N_PAD - N_IN), (0, 0)))
    w_out_b = g_out.reshape(D_MODEL, D_MODEL)
    conv_w_full = jnp.transpose(g_cw, (1, 0, 2)).reshape(CONV_WIDTH, 2 * D_MLSTM)
    c_all16 = jnp.pad(g_c.reshape(N_DEV, D_MODEL), ((0, 8), (0, 0)))

    b_ada_blk = lax.dynamic_slice(b_ada, (0, me * ADA_SHARD), (1, ADA_SHARD))
    mod_all, sc_all16 = _ada_mod(c_all16, w_ada[0], b_ada_blk)
    mod = lax.dynamic_index_in_dim(mod_all, me, axis=1, keepdims=False).reshape(1, 3 * D_MODEL)
    shift, scale, gate = mod[:, :D_MODEL], mod[:, D_MODEL:2 * D_MODEL], mod[:, 2 * D_MODEL:]

    flights = {}

    def send_dw_out(dwo):
        blocks = dwo.astype(BF16).reshape(N_DEV, D_MODEL // N_DEV, D_MODEL)
        flights["out"], token = _scatter_start("send_dw_out", (blocks,))
        return token

    def send_dw_in(dw_in_t, dcw):
        blocks = dw_in_t.reshape(N_DEV, N_SHARD, D_MODEL)
        dcw_blocks = jnp.transpose(dcw.reshape(CONV_WIDTH, N_DEV, 128), (1, 0, 2))
        flights["in"], token = _scatter_start("send_dw_in", (blocks, dcw_blocks))
        return token

    r = _local_step(x[0], loss_target[0], shift, scale, gate, norm_g, w_in_t, w_out_b, conv_w_full, conv_b,
                    w_pool[0], ls_pool, mh_norm_g, b_gates, final_g, send_dw_out, send_dw_in)

    pack = _pack_small(dict(loss=r["loss"][:, :1], final_g=r["final_g"], norm_g=r["norm_g"], conv_b=r["conv_b"],
                            ls_pool=r["ls_pool"], mh_norm_g=r["mh_norm_g"], b_gates=r["b_gates"][:, :8],
                            b_ada=r["dmod"], w_pool=r["w_pool"]))
    (p_pack,) = _all_gather("gather_small", pack)
    p_in, p_cw = _scatter_wait("recv_dw_in", flights["in"], p_pack)
    (p_out,) = _scatter_wait("recv_dw_out", flights["out"], p_cw)

    in_t = _adam_sum("adam_w_in", p_in, w_in[0].T, m_w_in[0].T, v_w_in[0].T, N_SHARD, 256)
    gi, di, mi, vi = (o.T for o in in_t)
    go, do_, mo, vo = _adam_sum("adam_w_out", p_out, w_out[0], m_w_out[0], v_w_out[0], 128)
    gc, dc, mc, vc = _adam_sum("adam_conv_w", p_cw, conv_w[0], m_conv_w[0], v_conv_w[0], CONV_WIDTH)

    def small(loss_like, fg_, ng_, cb_, ls_, mh_, bg_, ba_, wp_):
        return _pack_small(dict(loss=loss_like, final_g=fg_, norm_g=ng_, conv_b=cb_, ls_pool=ls_, mh_norm_g=mh_,
                                b_gates=bg_, b_ada=ba_, w_pool=wp_))

    zero = jnp.zeros((1, 1), F32)
    w_pack = small(zero, final_g, norm_g, conv_b, ls_pool, mh_norm_g, b_gates, b_ada, w_pool)
    m_pack = small(zero, m_final_g, m_norm_g, m_conv_b, m_ls_pool, m_mh_norm_g, m_b_gates, m_b_ada, m_w_pool)
    v_pack = small(zero, v_final_g, v_norm_g, v_conv_b, v_ls_pool, v_mh_norm_g, v_b_gates, v_b_ada, v_w_pool)
    gp, dp, mp, vp = _adam_sum("adam_small", p_pack, w_pack, m_pack, v_pack, PACK_TOTAL)

    off, rows = PACK_OFF["b_ada"]
    dmod_all = p_pack[:, off:off + rows, :].reshape(N_DEV, 3 * D_MODEL)
    dmod_blk16 = jnp.pad(lax.dynamic_slice(dmod_all, (0, me * ADA_SHARD), (N_DEV, ADA_SHARD)), ((0, 8), (0, 0)))
    ga, da, ma, va = _adam_ada(sc_all16, dmod_blk16, w_ada[0], m_w_ada[0], v_w_ada[0])

    names = ("norm_g", "w_ada", "b_ada", "w_in", "b_gates", "conv_w", "conv_b", "w_pool", "ls_pool", "mh_norm_g",
             "w_out", "final_g")
    shapes = dict(norm_g=norm_g.shape, b_ada=b_ada.shape, b_gates=b_gates.shape, conv_b=conv_b.shape,
                  w_pool=w_pool.shape, ls_pool=ls_pool.shape, mh_norm_g=mh_norm_g.shape, final_g=final_g.shape)
    sharded = dict(w_ada=(ga, da, ma, va), w_in=(gi, di, mi, vi), conv_w=(gc, dc, mc, vc), w_out=(go, do_, mo, vo))
    outs = []
    for kind in range(4):
        for nm in names:
            if nm in sharded:
                outs.append(sharded[nm][kind][None])
            else:
                outs.append(_unpack_small((gp, dp, mp, vp)[kind], nm, shapes[nm]))
    loss = gp[0, 0]
    grad_x = r["grad_x"].reshape(1, seq, D_MODEL)
    return (loss, grad_x, *outs)
```

```python
import jax
import jax.numpy as jnp
from jax import lax
from jax.experimental import pallas as pl
from jax.experimental.pallas import tpu as pltpu

F32 = jnp.float32
BF16 = jnp.bfloat16

D_MODEL = 1024
D_POOL = 512
D_MLSTM = 512
N_HEADS = 4
HEAD_DIM = 128
CHUNK = 128
POOL_WINDOWS = (2, 4, 8, 16)
POOL_GROUP_DIM = 128
CONV_WIDTH = 4
EPS = 1e-6
N_MAIN = 3584
N_IN = 3592
N_PAD = 3840
N_SHARD = N_IN // 8
ADA_SHARD = 3 * D_MODEL // 8
N_DEV = 8
CONV_HALO = 8
POOL_HALO = 16
NEG_BIG = -1e30
VMEM_LIMIT_BYTES = 56 * 1024 * 1024

ADAM_LR = 0.001
ADAM_B1 = 0.9
ADAM_B2 = 0.999
ADAM_EPS = 1e-08
ADAM_WD = 0.01
ADAM_STEP = 10

PACK_ROWS = (("loss", 8), ("final_g", 8), ("norm_g", 8), ("conv_b", 8), ("ls_pool", 8),
             ("mh_norm_g", 8), ("b_gates", 8), ("b_ada", 24), ("w_pool", 512))
PACK_TOTAL = sum(r for _, r in PACK_ROWS)


def _pack_offsets():
    off, out = 0, {}
    for name, rows in PACK_ROWS:
        out[name] = (off, rows)
        off += rows
    return out


PACK_OFF = _pack_offsets()


def _dot(a, b):
    return jnp.dot(a, b, preferred_element_type=F32)


def _dot_nt(a, b):
    return lax.dot_general(a, b, (((1,), (1,)), ((), ())), preferred_element_type=F32)


def _dot_tn(a, b):
    return lax.dot_general(a, b, (((0,), (0,)), ((), ())), preferred_element_type=F32)


def _dot_f32(a, b):
    return jnp.dot(a, b, precision=lax.Precision.HIGHEST, preferred_element_type=F32)


def _sigmoid(x):
    return jax.nn.sigmoid(x)


def _log_sigmoid(x):
    return jnp.minimum(x, 0.0) - jnp.log1p(jnp.exp(-jnp.abs(x)))


def _params(sem):
    return pltpu.CompilerParams(dimension_semantics=sem, vmem_limit_bytes=VMEM_LIMIT_BYTES)


def _full(shape):
    n = len(shape)
    return pl.BlockSpec(shape, lambda *_: (0,) * n)


def _mesh_pos():
    return lax.axis_index("x"), lax.axis_index("y"), lax.axis_index("c")


def _peer(k):
    x, y, c = _mesh_pos()
    px = 1 - x if (k >> 2) & 1 else x
    py = 1 - y if (k >> 1) & 1 else y
    pc = 1 - c if k & 1 else c
    return (px, py, pc), 4 * px + 2 * py + pc


def _remote(src, dst, send_sem, recv_sem, to):
    return pltpu.make_async_remote_copy(src_ref=src, dst_ref=dst, send_sem=send_sem, recv_sem=recv_sem, device_id=to,
                                        device_id_type=pl.DeviceIdType.MESH)


def _other_chips():
    x, y, _ = _mesh_pos()
    return [(1 - x, y), (x, 1 - y), (1 - x, 1 - y)]


def _two_level_gather(src, dst, send_sems, recv_sems, local_sems):
    n = len(src)
    x, y, c = _mesh_pos()
    me = 4 * x + 2 * y + c
    sibling = (x, y, 1 - c)
    chips = _other_chips()

    def copy(a, k, block, to, own):
        return _remote(src[a] if own else dst[a].at[block], dst[a].at[block], send_sems.at[a, k], recv_sems.at[a, k], to)

    local = [pltpu.make_async_copy(src[a], dst[a].at[me], local_sems.at[a]) for a in range(n)]
    first = [copy(a, 0, me, sibling, True) for a in range(n)]
    first += [copy(a, 1 + j, me, (*chip, c), True) for j, chip in enumerate(chips) for a in range(n)]
    for cp in local + first:
        cp.start()
    passed = []
    for j, (px, py) in enumerate(chips):
        block = 4 * px + 2 * py + c
        for a in range(n):
            copy(a, 1 + j, block, sibling, False).wait_recv()
            passed.append(copy(a, 4 + j, block, sibling, False))
            passed[-1].start()
    for a in range(n):
        copy(a, 0, 4 * x + 2 * y + (1 - c), sibling, False).wait_recv()
    for j, (px, py) in enumerate(chips):
        for a in range(n):
            copy(a, 4 + j, 4 * px + 2 * py + (1 - c), sibling, False).wait_recv()
    for cp in first + passed:
        cp.wait_send()
    for cp in local:
        cp.wait()


GATHER_COPIES = 7


def _all_gather(name, *shards):
    n = len(shards)

    def body(*refs):
        _two_level_gather(refs[:n], refs[n:2 * n], *refs[2 * n:])

    hbm = pl.BlockSpec(memory_space=pltpu.HBM)
    return pl.pallas_call(
        body, name=name,
        out_shape=tuple(jax.ShapeDtypeStruct((N_DEV,) + s.shape, s.dtype) for s in shards),
        in_specs=[hbm] * n, out_specs=tuple([hbm] * n),
        scratch_shapes=[pltpu.SemaphoreType.DMA((n, GATHER_COPIES)), pltpu.SemaphoreType.DMA((n, GATHER_COPIES)),
                        pltpu.SemaphoreType.DMA((n,))],
    )(*shards)


def _ada_mod(c_all16, w_ada_blk, b_ada_blk):
    def body(c_ref, w_ref, b_ref, out_ref, sc_ref, send_sems, recv_sems):
        x, y, c = _mesh_pos()
        me = 4 * x + 2 * y + c
        cv = c_ref[...]
        sc = cv * _sigmoid(cv)
        sc_ref[...] = sc
        blk = _dot(sc.astype(BF16), w_ref[...].astype(BF16)) + b_ref[...]
        out_ref[me] = blk[0:N_DEV, :]
        copies = []
        for k in range(1, N_DEV):
            peer, _ = _peer(k)
            copies.append(pltpu.make_async_remote_copy(
                src_ref=out_ref.at[me], dst_ref=out_ref.at[me], send_sem=send_sems.at[k - 1],
                recv_sem=recv_sems.at[k - 1], device_id=peer, device_id_type=pl.DeviceIdType.MESH))
        for cp in copies:
            cp.start()
        for cp in copies:
            cp.wait()

    vmem = pl.BlockSpec(memory_space=pltpu.VMEM)
    return pl.pallas_call(
        body, name="ada_mod",
        out_shape=(jax.ShapeDtypeStruct((N_DEV, N_DEV, ADA_SHARD), F32),
                   jax.ShapeDtypeStruct(c_all16.shape, F32)),
        in_specs=[vmem] * 3, out_specs=(vmem, vmem),
        scratch_shapes=[pltpu.SemaphoreType.DMA((N_DEV - 1,)), pltpu.SemaphoreType.DMA((N_DEV - 1,))],
    )(c_all16, w_ada_blk, b_ada_blk)


def _scatter_copies(src, land, send_sems, recv_sems):
    x, y, c = _mesh_pos()
    me = 4 * x + 2 * y + c
    copies = []
    for k in range(1, N_DEV):
        peer, p = _peer(k)
        for a in range(len(src)):
            i = a * (N_DEV - 1) + k - 1
            copies.append(_remote(src[a].at[p], land[a].at[me], send_sems.at[i], recv_sems.at[i], peer))
    return copies


def _scatter_start(name, blocks):
    n = len(blocks)

    def body(*refs):
        src, land = refs[:n], refs[n:2 * n]
        send_sems, recv_sems = refs[2 * n], refs[2 * n + 1]
        token_ref = refs[-1]
        for cp in _scatter_copies(src, land, send_sems, recv_sems):
            cp.start()
        token_ref[...] = jnp.zeros_like(token_ref)

    hbm = pl.BlockSpec(memory_space=pltpu.HBM)
    sem = pl.BlockSpec(memory_space=pltpu.SEMAPHORE)
    through = tuple(pltpu.HBM(b.shape, b.dtype) for b in blocks)
    args = [pltpu.with_memory_space_constraint(b, pltpu.HBM) for b in blocks]
    args += [pltpu.with_memory_space_constraint(lax.empty(b.shape, b.dtype), pltpu.HBM) for b in blocks]
    out = pl.pallas_call(
        body, name=name,
        out_shape=(pltpu.SemaphoreType.DMA((n * (N_DEV - 1),)),) * 2 + through + through
        + (jax.ShapeDtypeStruct((8, 128), F32),),
        in_specs=[hbm] * (2 * n),
        out_specs=(sem, sem) + (hbm,) * (2 * n) + (pl.BlockSpec(memory_space=pltpu.VMEM),),
        input_output_aliases={i: 2 + i for i in range(2 * n)},
        compiler_params=pltpu.CompilerParams(has_side_effects=pltpu.SideEffectType.DATAFLOW_SIDE_EFFECTING),
    )(*args)
    return out[:-1], out[-1][0:1, 0:1]


def _scatter_wait(name, state, after):
    n = (len(state) - 2) // 2
    send_sems, recv_sems = state[0], state[1]
    src, land = state[2:2 + n], state[2 + n:]

    def body(*refs):
        src_r, land_r = refs[:n], refs[n:2 * n]
        for cp in _scatter_copies(src_r, land_r, refs[2 * n], refs[2 * n + 1]):
            cp.wait_send()
            cp.wait_recv()

    hbm = pl.BlockSpec(memory_space=pltpu.HBM)
    sem = pl.BlockSpec(memory_space=pltpu.SEMAPHORE)
    out = pl.pallas_call(
        body, name=name,
        out_shape=tuple(pltpu.HBM(b.shape, b.dtype) for b in src + land),
        in_specs=[hbm] * (2 * n) + [sem, sem, pl.BlockSpec(memory_space=pl.ANY)],
        out_specs=(hbm,) * (2 * n),
        input_output_aliases={i: i for i in range(2 * n)},
        compiler_params=pltpu.CompilerParams(has_side_effects=pltpu.SideEffectType.DATAFLOW_SIDE_EFFECTING),
    )(*src, *land, send_sems, recv_sems, after)
    me = 4 * lax.axis_index("x") + 2 * lax.axis_index("y") + lax.axis_index("c")
    landed = []
    for a in range(n):
        own = lax.dynamic_index_in_dim(out[a], me, axis=0, keepdims=True)
        landed.append(lax.dynamic_update_slice_in_dim(out[n + a], own, me, axis=0))
    return landed


def _fwd_proj(x, norm_g, scale, shift, w_in_t):
    seq = x.shape[0]
    tm = min(512, seq)
    sub = min(256, tm)
    tn = 512

    def body(x_ref, ng_ref, sc_ref, sh_ref, wt_ref, proj_ref, gates_ref, h_ref):
        def chain(n):
            for _ in range(n):
                yield
            rows = slice(n * sub, (n + 1) * sub)
            xt = x_ref[rows, :]
            r = lax.rsqrt(jnp.mean(xt * xt, axis=-1, keepdims=True) + EPS)
            h = ((xt * r) * ng_ref[...]) * (1.0 + sc_ref[...]) + sh_ref[...]
            hb = h.astype(BF16)
            h_ref[rows, :] = hb
            yield
            gates_ref[rows, :] = _dot_nt(hb, wt_ref[N_MAIN:N_MAIN + 128, :])
            for j in range(N_MAIN // tn):
                proj_ref[rows, j * tn:(j + 1) * tn] = _dot_nt(hb, wt_ref[j * tn:(j + 1) * tn, :])

        _in_lockstep(chain(n) for n in range(tm // sub))

    vec = _full((1, D_MODEL))
    tile = pl.BlockSpec((tm, D_MODEL), lambda i: (i, 0))
    return pl.pallas_call(
        body, name="fwd_proj", grid=(seq // tm,),
        out_shape=(jax.ShapeDtypeStruct((seq, N_MAIN), F32), jax.ShapeDtypeStruct((seq, 128), F32),
                   jax.ShapeDtypeStruct((seq, D_MODEL), BF16)),
        in_specs=[tile, vec, vec, vec, _full((N_PAD, D_MODEL))],
        out_specs=(pl.BlockSpec((tm, N_MAIN), lambda i: (i, 0)), pl.BlockSpec((tm, 128), lambda i: (i, 0)), tile),
        compiler_params=_params(("arbitrary",)),
    )(x, norm_g, scale, shift, w_in_t)


def _gate_forms(gpre):
    r = lax.broadcasted_iota(jnp.int32, (CHUNK, CHUNK), 0)
    c = lax.broadcasted_iota(jnp.int32, (CHUNK, CHUNK), 1)
    causal = c <= r
    ltri = jnp.where(causal, 1.0, 0.0).astype(F32)
    utri = jnp.where(r <= c, 1.0, 0.0).astype(F32)
    bcol = _dot_f32(ltri, _log_sigmoid(gpre))
    gt8 = gpre.T[0:8, :]
    brow = _dot_f32(_log_sigmoid(gt8), utri)
    return causal, utri, bcol, gt8, brow


def _in_lockstep(stages):
    alive = list(stages)
    while alive:
        still = []
        for g in alive:
            try:
                next(g)
                still.append(g)
            except StopIteration:
                pass
        alive = still


def _head_fwd(qh, kh, vh, bc, br, igr, m_prev, c_h, n_row, causal):
    qb, kb, vb, cb = qh.astype(BF16), kh.astype(BF16), vh.astype(BF16), c_h.astype(BF16)
    qk = _dot_nt(qb, kb)
    cq = _dot_nt(qb, cb)
    yield
    dlog = jnp.where(causal, bc - br + igr, NEG_BIG)
    inter_log = bc + m_prev
    m_t = jnp.maximum(inter_log, jnp.max(dlog, axis=-1, keepdims=True))
    yield
    dmat = jnp.exp(dlog - m_t)
    inter = jnp.exp(inter_log - m_t)
    s = qk * dmat
    sv = _dot(s.astype(BF16), vb)
    yield
    nq = jnp.sum(qh * n_row, axis=-1, keepdims=True)
    den = jnp.sum(s, axis=-1, keepdims=True) + inter * nq
    emt = jnp.exp(-m_t)
    yield
    num = sv + inter * cq
    dn = jnp.maximum(jnp.abs(den), emt)
    hm = num / dn
    return dict(dmat=dmat, inter=inter, qb=qb, kb=kb, vb=vb, cb=cb, s=s, cq=cq, nq=nq, den=den, emt=emt,
                dn=dn, hm=hm)


def _state_weights(bc, igc, m_prev, m_new=None):
    last = lax.broadcasted_iota(jnp.int32, (CHUNK, 1), 0) == CHUNK - 1
    b_last = jnp.sum(jnp.where(last, bc, 0.0), axis=0, keepdims=True)
    wlog = b_last - bc + igc
    if m_new is None:
        m_new = jnp.maximum(b_last + m_prev, jnp.max(wlog, axis=0, keepdims=True))
    w_c = jnp.exp(wlog - m_new)
    decay = jnp.exp(b_last + m_prev - m_new)
    return w_c, decay, m_new, last


def _rows_back(x, k):
    return x if k == 0 else pltpu.roll(x, k, 0)


def _rows_ahead(x, k):
    return x if k == 0 else pltpu.roll(x, x.shape[0] - k, 0)


def _conv_taps(xpad):
    return [_rows_back(xpad, CONV_WIDTH - 1 - j)[CONV_HALO:, :] for j in range(CONV_WIDTH)]


def _conv_pre(taps, cw_ref, cb_ref):
    a = cb_ref[...]
    for j in range(CONV_WIDTH):
        a = a + cw_ref[j:j + 1, :] * taps[j]
    return a


def _window_sum(x, w, shift):
    k = 1
    while k < w:
        x = x + shift(x, k)
        k *= 2
    return x


def _pool_window_sum(upad_ref, g, w):
    lanes = slice(g * POOL_GROUP_DIM, (g + 1) * POOL_GROUP_DIM)
    return _window_sum(upad_ref[:, lanes], w, _rows_back)[POOL_HALO:, :]


def _pool_inv_count(row0, rows, w):
    pos = row0 + lax.broadcasted_iota(jnp.int32, (rows, 1), 0) + 1
    return 1.0 / jnp.minimum(pos, w).astype(F32)


FWD_CHUNKS = 2


def _mix_fwd(proj, gates, bg_pad, conv_w8, conv_b, w_pool, ls_pool, mh_g):
    seq = proj.shape[0]
    n_chunks = seq // CHUNK
    per_step = FWD_CHUNKS
    blk = per_step * CHUNK

    def body(uz_ref, qk_ref, v_ref, o_ref, zm_ref, uh_ref, qkh_ref, g_ref, bg_ref, cw_ref, cb_ref, wp_ref,
             ls_ref, mhg_ref, mix_ref, cst_ref, nst_ref, mst_ref, a_ref, pooled_ref, c_scr, n_scr, m_scr, xpad, upad):
        i = pl.program_id(0)

        @pl.when(i == 0)
        def _():
            c_scr[...] = jnp.zeros_like(c_scr)
            n_scr[...] = jnp.zeros_like(n_scr)
            m_scr[...] = jnp.zeros_like(m_scr)

        first = i == 0

        upad[0:POOL_HALO, :] = jnp.where(first, 0.0, uh_ref[...])
        upad[POOL_HALO:POOL_HALO + blk, :] = uz_ref[:, 0:D_POOL]
        for g, w in enumerate(POOL_WINDOWS):
            lanes = slice(g * POOL_GROUP_DIM, (g + 1) * POOL_GROUP_DIM)
            pooled = (_pool_window_sum(upad, g, w) * _pool_inv_count(i * blk, blk, w) - uz_ref[:, lanes]).astype(BF16)
            pooled_ref[:, lanes] = pooled
            y = _dot(pooled, wp_ref[g].astype(BF16)) * ls_ref[:, lanes]
            zp = uz_ref[:, D_POOL + g * POOL_GROUP_DIM:D_POOL + (g + 1) * POOL_GROUP_DIM]
            mix_ref[:, lanes] = (y * (zp * _sigmoid(zp))).astype(BF16)

        xpad[0:CONV_HALO, :] = jnp.where(first, 0.0, qkh_ref[...])
        xpad[CONV_HALO:CONV_HALO + blk, :] = qk_ref[...]
        a = _conv_pre(_conv_taps(xpad[...]), cw_ref, cb_ref)
        a_ref[...] = a
        qk = a * _sigmoid(a)

        def head(rows, h, qh, kh, vh, bc, br, igr, m_prev, c_h, n_row, causal):
            lanes = slice(h * HEAD_DIM, (h + 1) * HEAD_DIM)
            f = yield from _head_fwd(qh, kh, vh, bc, br, igr, m_prev, c_h, n_row, causal)
            yield
            hm = f["hm"]
            hn = hm * lax.rsqrt(jnp.mean(hm * hm, axis=-1, keepdims=True) + EPS) * mhg_ref[:, lanes]
            zm = zm_ref[rows, lanes]
            out = hn * _sigmoid(o_ref[rows, lanes]) * (zm * _sigmoid(zm))
            mix_ref[rows, D_POOL + h * HEAD_DIM:D_POOL + (h + 1) * HEAD_DIM] = out.astype(BF16)

        c_cur = [c_scr[h] for h in range(N_HEADS)]
        n_cur = [n_scr[h:h + 1, :] for h in range(N_HEADS)]
        m_cur = [m_scr[h:h + 1, 0:1] for h in range(N_HEADS)]
        chains = []
        for s in range(per_step):
            rows = slice(s * CHUNK, (s + 1) * CHUNK)
            gpre = g_ref[rows, :] + bg_ref[...]
            causal, _, bcol, gt8, brow = _gate_forms(gpre)
            nst_ref[s] = jnp.zeros((8, 128), F32)
            mst_ref[s] = jnp.zeros((8, 128), F32)
            for h in range(N_HEADS):
                lanes = slice(h * HEAD_DIM, (h + 1) * HEAD_DIM)
                cst_ref[s, h] = c_cur[h]
                nst_ref[s, h:h + 1, :] = n_cur[h]
                mst_ref[s, h:h + 1, :] = jnp.broadcast_to(m_cur[h], (1, 128))
                qh = qk[rows, lanes]
                kh = qk[rows, D_MLSTM + h * HEAD_DIM:D_MLSTM + (h + 1) * HEAD_DIM] * (HEAD_DIM ** -0.5)
                vh = v_ref[rows, lanes]
                bc = bcol[:, N_HEADS + h:N_HEADS + h + 1]
                br = brow[N_HEADS + h:N_HEADS + h + 1, :]
                igr = gt8[h:h + 1, :]
                igc = gpre[:, h:h + 1]
                chains.append(head(rows, h, qh, kh, vh, bc, br, igr, m_cur[h], c_cur[h], n_cur[h], causal))
                w_c, decay, m_new, _ = _state_weights(bc, igc, m_cur[h])
                c_cur[h] = decay * c_cur[h] + _dot_tn((vh * w_c).astype(BF16), kh.astype(BF16))
                n_cur[h] = decay * n_cur[h] + jnp.sum(w_c * kh, axis=0, keepdims=True)
                m_cur[h] = m_new
        for h in range(N_HEADS):
            c_scr[h] = c_cur[h]
            n_scr[h:h + 1, :] = n_cur[h]
            m_scr[h:h + 1, :] = jnp.broadcast_to(m_cur[h], (1, 128))
        _in_lockstep(chains)

    in_specs = [
        pl.BlockSpec((blk, 1024), lambda i: (i, 0)),
        pl.BlockSpec((blk, 1024), lambda i: (i, 1)),
        pl.BlockSpec((blk, 512), lambda i: (i, 4)),
        pl.BlockSpec((blk, 512), lambda i: (i, 5)),
        pl.BlockSpec((blk, 512), lambda i: (i, 6)),
        pl.BlockSpec((POOL_HALO, 512), lambda i: (jnp.maximum(i * (blk // POOL_HALO) - 1, 0), 0)),
        pl.BlockSpec((CONV_HALO, 1024), lambda i: (jnp.maximum(i * (blk // CONV_HALO) - 1, 0), 1)),
        pl.BlockSpec((blk, 128), lambda i: (i, 0)),
        _full((1, 128)), _full((8, 1024)), _full((1, 1024)), _full((4, 128, 128)), _full((1, 512)),
        _full((1, 512))]
    return pl.pallas_call(
        body, name="mix_fwd", grid=(n_chunks // per_step,),
        out_shape=(jax.ShapeDtypeStruct((seq, D_MODEL), BF16),
                   jax.ShapeDtypeStruct((n_chunks, N_HEADS, HEAD_DIM, HEAD_DIM), F32),
                   jax.ShapeDtypeStruct((n_chunks, 8, 128), F32),
                   jax.ShapeDtypeStruct((n_chunks, 8, 128), F32),
                   jax.ShapeDtypeStruct((seq, 2 * D_MLSTM), F32),
                   jax.ShapeDtypeStruct((seq, D_POOL), BF16)),
        in_specs=in_specs,
        out_specs=(pl.BlockSpec((blk, D_MODEL), lambda i: (i, 0)),
                   pl.BlockSpec((per_step, N_HEADS, HEAD_DIM, HEAD_DIM), lambda i: (i, 0, 0, 0)),
                   pl.BlockSpec((per_step, 8, 128), lambda i: (i, 0, 0)),
                   pl.BlockSpec((per_step, 8, 128), lambda i: (i, 0, 0)),
                   pl.BlockSpec((blk, 2 * D_MLSTM), lambda i: (i, 0)),
                   pl.BlockSpec((blk, D_POOL), lambda i: (i, 0))),
        scratch_shapes=[pltpu.VMEM((N_HEADS, HEAD_DIM, HEAD_DIM), F32), pltpu.VMEM((8, 128), F32),
                        pltpu.VMEM((8, 128), F32), pltpu.VMEM((CONV_HALO + blk, 1024), F32),
                        pltpu.VMEM((POOL_HALO + blk, D_POOL), F32)],
        compiler_params=_params(("arbitrary",)),
    )(proj, proj, proj, proj, proj, proj, proj, gates, bg_pad, conv_w8, conv_b, w_pool, ls_pool, mh_g)


def _out_fwd_bwd(mix, x, tgt, w_out_b, gate, final_g):
    seq = x.shape[0]
    tm = min(512, seq)
    sub = min(256, tm)

    def body(mix_ref, x_ref, t_ref, w_ref, gate_ref, fg_ref, dx2_ref, dmix_ref, dwo_ref, dgate_ref, dfg_ref,
             loss_ref):
        @pl.when(pl.program_id(0) == 0)
        def _():
            dwo_ref[...] = jnp.zeros_like(dwo_ref)
            dgate_ref[...] = jnp.zeros_like(dgate_ref)
            dfg_ref[...] = jnp.zeros_like(dfg_ref)
            loss_ref[...] = jnp.zeros_like(loss_ref)

        w = w_ref[...]
        gate_v = gate_ref[...]
        fg = fg_ref[...]
        do2_parts = [None] * (tm // sub)

        def chain(n):
            rows = slice(n * sub, (n + 1) * sub)
            o2 = _dot(mix_ref[rows, :], w)
            yield
            x2 = x_ref[rows, :] + gate_v * o2
            r2 = lax.rsqrt(jnp.mean(x2 * x2, axis=-1, keepdims=True) + EPS)
            x2n = x2 * r2
            err = x2n * fg - t_ref[rows, :]
            part = 0.5 * jnp.sum(jnp.sum(err * err, axis=-1, keepdims=True), axis=0, keepdims=True) / D_MODEL
            loss_ref[...] += jnp.broadcast_to(part, loss_ref.shape)
            dy = err / D_MODEL
            dfg_ref[...] += jnp.sum(dy * x2n, axis=0, keepdims=True)
            gdy = dy * fg
            dx2 = r2 * (gdy - x2n * jnp.mean(gdy * x2n, axis=-1, keepdims=True))
            dx2_ref[rows, :] = dx2
            dgate_ref[...] += jnp.sum(dx2 * o2, axis=0, keepdims=True)
            do2 = (dx2 * gate_v).astype(BF16)
            dmix_ref[rows, :] = _dot_nt(do2, w)
            do2_parts[n] = do2

        _in_lockstep(chain(n) for n in range(tm // sub))
        dwo_ref[...] += _dot_tn(mix_ref[...], jnp.concatenate(do2_parts, axis=0))

    tile = pl.BlockSpec((tm, D_MODEL), lambda i: (i, 0))
    vec = _full((1, D_MODEL))
    return pl.pallas_call(
        body, name="out_fwd_bwd", grid=(seq // tm,),
        out_shape=(jax.ShapeDtypeStruct((seq, D_MODEL), F32), jax.ShapeDtypeStruct((seq, D_MODEL), F32),
                   jax.ShapeDtypeStruct((D_MODEL, D_MODEL), F32), jax.ShapeDtypeStruct((1, D_MODEL), F32),
                   jax.ShapeDtypeStruct((1, D_MODEL), F32), jax.ShapeDtypeStruct((1, 128), F32)),
        in_specs=[tile, tile, tile, _full((D_MODEL, D_MODEL)), vec, vec],
        out_specs=(tile, tile, _full((D_MODEL, D_MODEL)), vec, vec, _full((1, 128))),
        compiler_params=_params(("arbitrary",)),
    )(mix, x, tgt, w_out_b, gate, final_g)


def _mix_bwd(proj, gates, dmix, conv_a, pooled, cst, nst, mst, bg_pad, conv_w8, w_pool, ls_pool, mh_g):
    seq = proj.shape[0]
    n_chunks = seq // CHUNK

    def body(zp_ref, qk_ref, v_ref, o_ref, zm_ref, g_ref, dmix_ref, a_ref, pooled_ref, cst_ref, nst_ref, mst_ref,
             mnx_ref, bg_ref, cw_ref, wp_ref, ls_ref, mhg_ref,
             dp_ref, dcw_ref, dcb_ref, dwp_ref, dls_ref, dmhg_ref, dbg_ref,
             dc_scr, dn_scr, dapad, dpipad):
        i = pl.program_id(0)
        ci = n_chunks - 1 - i

        @pl.when(i == 0)
        def _():
            for ref in (dc_scr, dn_scr, dcw_ref, dcb_ref, dwp_ref, dls_ref, dmhg_ref, dbg_ref):
                ref[...] = jnp.zeros_like(ref)
            dapad[CHUNK:CHUNK + CONV_HALO, :] = jnp.zeros((CONV_HALO, 1024), F32)
            dpipad[CHUNK:CHUNK + POOL_HALO, :] = jnp.zeros((POOL_HALO, D_POOL), F32)

        dpooled = []
        for g, w in enumerate(POOL_WINDOWS):
            lanes = slice(g * POOL_GROUP_DIM, (g + 1) * POOL_GROUP_DIM)
            zlanes = slice(D_POOL + g * POOL_GROUP_DIM, D_POOL + (g + 1) * POOL_GROUP_DIM)
            inv = _pool_inv_count(ci * CHUNK, CHUNK, w)
            pb = pooled_ref[:, lanes]
            wpb = wp_ref[g].astype(BF16)
            yw = _dot(pb, wpb)
            ls = ls_ref[:, lanes]
            zp = zp_ref[:, lanes]
            sg = _sigmoid(zp)
            dpo = dmix_ref[:, lanes]
            dp_ref[:, zlanes] = (dpo * (yw * ls) * (sg * (1.0 + zp * (1.0 - sg)))).astype(BF16)
            dy = dpo * (zp * sg)
            dls_ref[:, lanes] += jnp.sum(dy * yw, axis=0, keepdims=True)
            dyw = (dy * ls).astype(BF16)
            dwp_ref[g] += _dot_tn(pb, dyw)
            dpl = _dot_nt(dyw, wpb)
            dpooled.append(dpl)
            dpipad[0:CHUNK, lanes] = dpl * inv
        for g, w in enumerate(POOL_WINDOWS):
            lanes = slice(g * POOL_GROUP_DIM, (g + 1) * POOL_GROUP_DIM)
            du = _window_sum(dpipad[:, lanes], w, _rows_ahead)[0:CHUNK, :] - dpooled[g]
            dp_ref[:, lanes] = du.astype(BF16)
        dpipad[CHUNK:CHUNK + POOL_HALO, :] = dpipad[0:POOL_HALO, :]

        a = a_ref[...]
        sga = _sigmoid(a)
        qk = a * sga
        dsilu_a = sga * (1.0 + a * (1.0 - sga))

        gpre = g_ref[...] + bg_ref[...]
        causal, utri, bcol, gt8, brow = _gate_forms(gpre)
        lane = lax.broadcasted_iota(jnp.int32, (CHUNK, 128), 1)
        row = lax.broadcasted_iota(jnp.int32, (CHUNK, 128), 0)
        col_g_rows, dig_parts, db_parts = [], [], []
        scale_k = HEAD_DIM ** -0.5

        def head(h):
            lanes = slice(h * HEAD_DIM, (h + 1) * HEAD_DIM)
            klanes = slice(D_MLSTM + h * HEAD_DIM, D_MLSTM + (h + 1) * HEAD_DIM)
            qh = qk[:, lanes]
            kh = qk[:, klanes] * scale_k
            vh = v_ref[:, lanes]
            bc = bcol[:, N_HEADS + h:N_HEADS + h + 1]
            br = brow[N_HEADS + h:N_HEADS + h + 1, :]
            igr = gt8[h:h + 1, :]
            igc = gpre[:, h:h + 1]
            m_prev = mst_ref[0, h:h + 1, 0:1]
            m_next = mnx_ref[0, h:h + 1, 0:1]
            c_h = cst_ref[0, h]
            n_row = nst_ref[0, h:h + 1, :]
            w_c, decay, _, last = _state_weights(bc, igc, m_prev, m_next)
            dcn = dc_scr[h]
            dnn = dn_scr[h:h + 1, :]
            dcnb = dcn.astype(BF16)
            vb0, kb0 = vh.astype(BF16), kh.astype(BF16)
            amat = _dot(vb0, dcnb) + dnn
            kdc = _dot_nt(kb0, dcnb)
            ddecay = (jnp.sum(jnp.sum(dcn * c_h, axis=-1, keepdims=True), axis=0, keepdims=True)
                      + jnp.sum(dnn * n_row, axis=-1, keepdims=True))
            f = yield from _head_fwd(qh, kh, vh, bc, br, igr, m_prev, c_h, n_row, causal)
            qb, kb, vb, cb = f["qb"], f["kb"], f["vb"], f["cb"]
            s, dmat, inter, den, dn, hm = f["s"], f["dmat"], f["inter"], f["den"], f["dn"], f["hm"]
            yield

            rinv = lax.rsqrt(jnp.mean(hm * hm, axis=-1, keepdims=True) + EPS)
            hmn = hm * rinv
            gh = mhg_ref[:, lanes]
            o_pre = o_ref[:, lanes]
            og = _sigmoid(o_pre)
            zm = zm_ref[:, lanes]
            sgz = _sigmoid(zm)
            sz = zm * sgz
            dout = dmix_ref[:, D_POOL + h * HEAD_DIM:D_POOL + (h + 1) * HEAD_DIM]
            hn = hmn * gh
            dp_ref[:, 2560 + h * HEAD_DIM:2560 + (h + 1) * HEAD_DIM] = (
                dout * hn * sz * og * (1.0 - og)).astype(BF16)
            dp_ref[:, 3072 + h * HEAD_DIM:3072 + (h + 1) * HEAD_DIM] = (
                dout * hn * og * (sgz * (1.0 + zm * (1.0 - sgz)))).astype(BF16)
            dhn = dout * og * sz
            dmhg_ref[:, lanes] += jnp.sum(dhn * hmn, axis=0, keepdims=True)
            dyn = dhn * gh
            dhm = rinv * (dyn - hmn * jnp.mean(dyn * hmn, axis=-1, keepdims=True))
            yield

            inv_dn = 1.0 / dn
            dnum = dhm * inv_dn
            hd = jnp.sum(dhm * hm, axis=-1, keepdims=True)
            dden = jnp.where(jnp.abs(den) > f["emt"], -hd * inv_dn * jnp.sign(den), 0.0)
            dnb = dnum.astype(BF16)
            dnv = _dot_nt(dnb, vb)
            dv = _dot_tn(s.astype(BF16), dnb)
            dnc = _dot(dnb, cb)
            dc_prev = _dot_tn((inter * dnum).astype(BF16), qb)
            yield
            ds = dnv + dden
            dqk = (ds * dmat).astype(BF16)
            dqk_k = _dot(dqk, kb)
            dk = _dot_tn(dqk, qb)
            yield
            gmat = ds * s
            row_g = jnp.sum(gmat, axis=-1, keepdims=True)
            col_g_rows.append(jnp.where(row == h, jnp.sum(gmat, axis=0, keepdims=True), 0.0))
            gcol = inter * (jnp.sum(dnum * f["cq"], axis=-1, keepdims=True) + dden * f["nq"])
            dn_prev = jnp.sum((inter * dden) * qh, axis=0, keepdims=True)
            dw = jnp.sum(amat * kh, axis=-1, keepdims=True)
            e = dw * w_c
            db_last = ddecay * decay + jnp.sum(e, axis=0, keepdims=True)
            dig_parts.append(jnp.where(lane == h, e, 0.0))
            db_parts.append(jnp.where(lane == N_HEADS + h, row_g + gcol - e + jnp.where(last, db_last, 0.0), 0.0))
            dc_scr[h] = decay * dcn + dc_prev
            dn_scr[h:h + 1, :] = decay * dnn + dn_prev
            yield
            dq = dqk_k + inter * (dnc + dden * n_row)
            dp_ref[:, 2048 + h * HEAD_DIM:2048 + (h + 1) * HEAD_DIM] = (dv + w_c * kdc).astype(BF16)
            dapad[0:CHUNK, lanes] = dq * dsilu_a[:, lanes]
            dapad[0:CHUNK, klanes] = (dk + w_c * amat) * scale_k * dsilu_a[:, klanes]

        _in_lockstep(head(h) for h in range(N_HEADS))

        cs_t = sum(col_g_rows[1:], col_g_rows[0]).T
        dig_all = sum(dig_parts[1:], dig_parts[0]) + cs_t
        db_cols = sum(db_parts[1:], db_parts[0])
        shifted = jnp.zeros((CHUNK, 128), F32)
        for h in range(N_HEADS):
            shifted = shifted + jnp.where(lane == N_HEADS + h, cs_t[:, h:h + 1], 0.0)
        dlf = _dot_f32(utri, db_cols - shifted)
        dgates = dig_all + dlf * _sigmoid(-gpre)
        dp_ref[:, N_MAIN:N_MAIN + 128] = dgates.astype(BF16)
        dp_ref[:, N_MAIN + 128:N_PAD] = jnp.zeros((CHUNK, N_PAD - N_MAIN - 128), BF16)
        dbg_ref[...] += jnp.sum(dgates, axis=0, keepdims=True)

        da_pad = dapad[...]
        da = da_pad[0:CHUNK, :]
        dcb_ref[...] += jnp.sum(da, axis=0, keepdims=True)
        x = qk_ref[...]
        dx = jnp.zeros((CHUNK, 1024), F32)
        for j in range(CONV_WIDTH):
            da_j = _rows_ahead(da_pad, CONV_WIDTH - 1 - j)[0:CHUNK, :]
            dcw_ref[j:j + 1, :] += jnp.sum(da_j * x, axis=0, keepdims=True)
            dx = dx + cw_ref[j:j + 1, :] * da_j
        dp_ref[:, 1024:2048] = dx.astype(BF16)
        dapad[CHUNK:CHUNK + CONV_HALO, :] = dapad[0:CONV_HALO, :]

    cmap = lambda i: n_chunks - 1 - i
    wide = pl.BlockSpec((CHUNK, 1024), lambda i: (cmap(i), 0))
    state = pl.BlockSpec((1, 8, 128), lambda i: (cmap(i), 0, 0))
    in_specs = [
        pl.BlockSpec((CHUNK, 512), lambda i: (cmap(i), 1)),
        pl.BlockSpec((CHUNK, 1024), lambda i: (cmap(i), 1)),
        pl.BlockSpec((CHUNK, 512), lambda i: (cmap(i), 4)),
        pl.BlockSpec((CHUNK, 512), lambda i: (cmap(i), 5)),
        pl.BlockSpec((CHUNK, 512), lambda i: (cmap(i), 6)),
        pl.BlockSpec((CHUNK, 128), lambda i: (cmap(i), 0)),
        wide, wide,
        pl.BlockSpec((CHUNK, D_POOL), lambda i: (cmap(i), 0)),
        pl.BlockSpec((1, N_HEADS, HEAD_DIM, HEAD_DIM), lambda i: (cmap(i), 0, 0, 0)),
        state, state,
        pl.BlockSpec((1, 8, 128), lambda i: (jnp.minimum(cmap(i) + 1, n_chunks - 1), 0, 0)),
        _full((1, 128)), _full((8, 1024)), _full((4, 128, 128)), _full((1, 512)), _full((1, 512))]
    return pl.pallas_call(
        body, name="mix_bwd", grid=(n_chunks,),
        out_shape=(jax.ShapeDtypeStruct((seq, N_PAD), BF16), jax.ShapeDtypeStruct((8, 1024), F32),
                   jax.ShapeDtypeStruct((1, 1024), F32), jax.ShapeDtypeStruct((4, 128, 128), F32),
                   jax.ShapeDtypeStruct((1, 512), F32), jax.ShapeDtypeStruct((1, 512), F32),
                   jax.ShapeDtypeStruct((1, 128), F32)),
        in_specs=in_specs,
        out_specs=(pl.BlockSpec((CHUNK, N_PAD), lambda i: (cmap(i), 0)), _full((8, 1024)), _full((1, 1024)),
                   _full((4, 128, 128)), _full((1, 512)), _full((1, 512)), _full((1, 128))),
        scratch_shapes=[pltpu.VMEM((N_HEADS, HEAD_DIM, HEAD_DIM), F32), pltpu.VMEM((8, 128), F32),
                        pltpu.VMEM((CHUNK + CONV_HALO, 1024), F32), pltpu.VMEM((CHUNK + POOL_HALO, D_POOL), F32)],
        compiler_params=_params(("arbitrary",)),
    )(proj, proj, proj, proj, proj, gates, dmix, conv_a, pooled, cst, nst, mst, mst, bg_pad, conv_w8,
      w_pool, ls_pool, mh_g)


def _bwd_in(dproj, w_in_t, x, dx2, norm_g, scale):
    seq = x.shape[0]
    tm = min(512, seq)
    sub = min(256, tm)

    def body(dp_ref, wt_ref, x_ref, dx2_ref, ng_ref, sc_ref, gx_ref, dsh_ref, dsc_ref, dng_ref):
        @pl.when(pl.program_id(0) == 0)
        def _():
            dsh_ref[...] = jnp.zeros_like(dsh_ref)
            dsc_ref[...] = jnp.zeros_like(dsc_ref)
            dng_ref[...] = jnp.zeros_like(dng_ref)

        ng = ng_ref[...]
        one_sc = 1.0 + sc_ref[...]

        def chain(n):
            rows = slice(n * sub, (n + 1) * sub)
            dh = _dot(dp_ref[rows, :], wt_ref[...])
            yield
            xt = x_ref[rows, :]
            r = lax.rsqrt(jnp.mean(xt * xt, axis=-1, keepdims=True) + EPS)
            xn = xt * r
            dsh_ref[...] += jnp.sum(dh, axis=0, keepdims=True)
            dhxn = dh * xn
            dsc_ref[...] += jnp.sum(dhxn * ng, axis=0, keepdims=True)
            dng_ref[...] += jnp.sum(dhxn * one_sc, axis=0, keepdims=True)
            dxn = dh * (ng * one_sc)
            gx_ref[rows, :] = r * (dxn - xn * jnp.mean(dxn * xn, axis=-1, keepdims=True)) + dx2_ref[rows, :]

        _in_lockstep(chain(n) for n in range(tm // sub))

    tile = pl.BlockSpec((tm, D_MODEL), lambda i: (i, 0))
    vec = _full((1, D_MODEL))
    return pl.pallas_call(
        body, name="bwd_in", grid=(seq // tm,),
        out_shape=(jax.ShapeDtypeStruct((seq, D_MODEL), F32),) + (jax.ShapeDtypeStruct((1, D_MODEL), F32),) * 3,
        in_specs=[pl.BlockSpec((tm, N_PAD), lambda i: (i, 0)), _full((N_PAD, D_MODEL)), tile, tile, vec, vec],
        out_specs=(tile, vec, vec, vec),
        compiler_params=_params(("arbitrary",)),
    )(dproj, w_in_t, x, dx2, norm_g, scale)


def _dw_in(h_b, dproj):
    seq = h_b.shape[0]
    tk = min(4096, seq)
    tn = 768
    n_t = seq // tk

    def body(h_ref, dp_ref, dwt_ref, acc):
        t = pl.program_id(1)

        @pl.when(t == 0)
        def _():
            acc[...] = jnp.zeros_like(acc)

        acc[...] += _dot_tn(dp_ref[...], h_ref[...])

        @pl.when(t == n_t - 1)
        def _():
            dwt_ref[...] = acc[...].astype(BF16)

    return pl.pallas_call(
        body, name="dw_in", grid=(N_PAD // tn, n_t),
        out_shape=jax.ShapeDtypeStruct((N_PAD, D_MODEL), BF16),
        in_specs=[pl.BlockSpec((tk, D_MODEL), lambda j, t: (t, 0)), pl.BlockSpec((tk, tn), lambda j, t: (t, j))],
        out_specs=pl.BlockSpec((tn, D_MODEL), lambda j, t: (j, 0)),
        scratch_shapes=[pltpu.VMEM((tn, D_MODEL), F32)],
        compiler_params=_params(("arbitrary", "arbitrary")),
    )(h_b, dproj)


def _adam_update(g, w, m, v, g_ref, d_ref, m_ref, v_ref):
    mn = ADAM_B1 * m + (1.0 - ADAM_B1) * g
    vn = ADAM_B2 * v + (1.0 - ADAM_B2) * (g * g)
    m_hat = mn / (1.0 - ADAM_B1 ** ADAM_STEP)
    v_hat = vn / (1.0 - ADAM_B2 ** ADAM_STEP)
    g_ref[...] = g
    d_ref[...] = -ADAM_LR * (m_hat / (jnp.sqrt(v_hat) + ADAM_EPS) + ADAM_WD * w)
    m_ref[...] = mn
    v_ref[...] = vn


def _adam_sum(name, parts, w, m, v, row_tile, col_tile=None):
    rows, cols = w.shape
    col_tile = cols if col_tile is None else col_tile
    n_parts = parts.shape[0]

    def body(p_ref, w_ref, m_ref, v_ref, g_out, d_out, m_out, v_out):
        g = p_ref[0].astype(F32)
        for j in range(1, n_parts):
            g = g + p_ref[j].astype(F32)
        _adam_update(g, w_ref[...], m_ref[...], v_ref[...], g_out, d_out, m_out, v_out)

    tile = pl.BlockSpec((row_tile, col_tile), lambda i, j: (i, j))
    return pl.pallas_call(
        body, name=name, grid=(rows // row_tile, cols // col_tile),
        out_shape=(jax.ShapeDtypeStruct((rows, cols), F32),) * 4,
        in_specs=[pl.BlockSpec((n_parts, row_tile, col_tile), lambda i, j: (0, i, j)), tile, tile, tile],
        out_specs=(tile,) * 4,
        compiler_params=_params(("arbitrary", "arbitrary")),
    )(parts, w, m, v)


def _adam_ada(sc_all16, dmod_blk16, w, m, v):
    rows, cols = w.shape

    def body(sc_ref, dm_ref, w_ref, m_ref, v_ref, g_out, d_out, m_out, v_out):
        g = _dot_tn(sc_ref[...].astype(BF16), dm_ref[...].astype(BF16))
        _adam_update(g, w_ref[...], m_ref[...], v_ref[...], g_out, d_out, m_out, v_out)

    return pl.pallas_call(
        body, name="adam_w_ada", grid=(1,),
        out_shape=(jax.ShapeDtypeStruct((rows, cols), F32),) * 4,
        in_specs=[_full(sc_all16.shape), _full(dmod_blk16.shape)] + [_full((rows, cols))] * 3,
        out_specs=(_full((rows, cols)),) * 4,
        compiler_params=_params(("arbitrary",)),
    )(sc_all16, dmod_blk16, w, m, v)


def _pack_small(pieces):
    rows = []
    for name, n in PACK_ROWS:
        a = pieces[name].reshape(-1).astype(F32)
        a = jnp.pad(a, (0, n * 128 - a.shape[0]))
        rows.append(a.reshape(n, 128))
    return jnp.concatenate(rows, axis=0)


def _unpack_small(pack, name, shape):
    off, _ = PACK_OFF[name]
    size = 1
    for s in shape:
        size *= s
    n_rows = -(-size // 128)
    return pack[off:off + n_rows].reshape(-1)[:size].reshape(shape)


def _local_step(x2, tgt2, shift, scale, gate, norm_g, w_in_t, w_out_b, conv_w, conv_b, w_pool, ls_pool,
                mh_norm_g, b_gates, final_g, send_dw_out=None, send_dw_in=None):
    bg_pad = jnp.pad(b_gates, ((0, 0), (0, 128 - b_gates.shape[1])))
    conv_w8 = jnp.pad(conv_w, ((0, 8 - CONV_WIDTH), (0, 0)))
    fg = final_g.reshape(1, D_MODEL)

    proj, gates, h_b = _fwd_proj(x2, norm_g, scale, shift, w_in_t)
    mix, cst, nst, mst, conv_a, pooled = _mix_fwd(proj, gates, bg_pad, conv_w8, conv_b, w_pool, ls_pool, mh_norm_g)
    dx2, dmix, dwo, dgate, dfg, loss = _out_fwd_bwd(mix, x2, tgt2, w_out_b, gate, fg)
    if send_dw_out is not None:
        bg_pad = bg_pad + send_dw_out(dwo)
    dproj, dcw8, dcb, dwp, dls, dmhg, dbg = _mix_bwd(proj, gates, dmix, conv_a, pooled, cst, nst, mst, bg_pad,
                                                      conv_w8, w_pool, ls_pool, mh_norm_g)
    dw_in_t = _dw_in(h_b, dproj)[:N_IN]
    ng_in = norm_g
    if send_dw_in is not None:
        ng_in = norm_g + send_dw_in(dw_in_t, dcw8[:CONV_WIDTH])
    gx, dsh, dsc, dng = _bwd_in(dproj, w_in_t, x2, dx2, ng_in, scale)
    return dict(loss=loss, grad_x=gx, dw_in_t=dw_in_t, dw_out=dwo, dconv_w=dcw8[:CONV_WIDTH], conv_b=dcb,
                w_pool=dwp, ls_pool=dls, mh_norm_g=dmhg, b_gates=dbg, final_g=dfg, norm_g=dng,
                dmod=jnp.concatenate([dsh, dsc, dgate], axis=1))


def kernel(x, c, norm_g, w_ada, b_ada, w_in, b_gates, conv_w, conv_b, w_pool, ls_pool, mh_norm_g, w_out, final_g, loss_target, m_norm_g, m_w_ada, m_b_ada, m_w_in, m_b_gates, m_conv_w, m_conv_b, m_w_pool, m_ls_pool, m_mh_norm_g, m_w_out, m_final_g, v_norm_g, v_w_ada, v_b_ada, v_w_in, v_b_gates, v_conv_w, v_conv_b, v_w_pool, v_ls_pool, v_mh_norm_g, v_w_out, v_final_g):
    seq = x.shape[1]
    me = 4 * lax.axis_index("x") + 2 * lax.axis_index("y") + lax.axis_index("c")

    g_in, g_out, g_cw, g_c = _all_gather("gather_weights", w_in[0].astype(BF16).T, w_out[0].astype(BF16), conv_w[0], c)
    w_in_t = jnp.pad(g_in.reshape(N_IN, D_MODEL), ((0, N_PAD - N_IN), (0, 0)))
    w_out_b = g_out.reshape(D_MODEL, D_MODEL)
    conv_w_full = jnp.transpose(g_cw, (1, 0, 2)).reshape(CONV_WIDTH, 2 * D_MLSTM)
    c_all16 = jnp.pad(g_c.reshape(N_DEV, D_MODEL), ((0, 8), (0, 0)))

    b_ada_blk = lax.dynamic_slice(b_ada, (0, me * ADA_SHARD), (1, ADA_SHARD))
    mod_all, sc_all16 = _ada_mod(c_all16, w_ada[0], b_ada_blk)
    mod = lax.dynamic_index_in_dim(mod_all, me, axis=1, keepdims=False).reshape(1, 3 * D_MODEL)
    shift, scale, gate = mod[:, :D_MODEL], mod[:, D_MODEL:2 * D_MODEL], mod[:, 2 * D_MODEL:]

    flights = {}

    def send_dw_out(dwo):
        blocks = dwo.astype(BF16).reshape(N_DEV, D_MODEL // N_DEV, D_MODEL)
        flights["out"], token = _scatter_start("send_dw_out", (blocks,))
        return token

    def send_dw_in(dw_in_t, dcw):
        blocks = dw_in_t.reshape(N_DEV, N_SHARD, D_MODEL)
        dcw_blocks = jnp.transpose(dcw.reshape(CONV_WIDTH, N_DEV, 128), (1, 0, 2))
        flights["in"], token = _scatter_start("send_dw_in", (blocks, dcw_blocks))
        return token

    r = _local_step(x[0], loss_target[0], shift, scale, gate, norm_g, w_in_t, w_out_b, conv_w_full, conv_b,
                    w_pool[0], ls_pool, mh_norm_g, b_gates, final_g, send_dw_out, send_dw_in)

    pack = _pack_small(dict(loss=r["loss"][:, :1], final_g=r["final_g"], norm_g=r["norm_g"], conv_b=r["conv_b"],
                            ls_pool=r["ls_pool"], mh_norm_g=r["mh_norm_g"], b_gates=r["b_gates"][:, :8],
                            b_ada=r["dmod"], w_pool=r["w_pool"]))
    (p_pack,) = _all_gather("gather_small", pack)
    p_in, p_cw = _scatter_wait("recv_dw_in", flights["in"], p_pack)
    (p_out,) = _scatter_wait("recv_dw_out", flights["out"], p_cw)

    in_t = _adam_sum("adam_w_in", p_in, w_in[0].T, m_w_in[0].T, v_w_in[0].T, N_SHARD, 256)
    gi, di, mi, vi = (o.T for o in in_t)
    go, do_, mo, vo = _adam_sum("adam_w_out", p_out, w_out[0], m_w_out[0], v_w_out[0], 128)
    gc, dc, mc, vc = _adam_sum("adam_conv_w", p_cw, conv_w[0], m_conv_w[0], v_conv_w[0], CONV_WIDTH)

    def small(loss_like, fg_, ng_, cb_, ls_, mh_, bg_, ba_, wp_):
        return _pack_small(dict(loss=loss_like, final_g=fg_, norm_g=ng_, conv_b=cb_, ls_pool=ls_, mh_norm_g=mh_,
                                b_gates=bg_, b_ada=ba_, w_pool=wp_))

    zero = jnp.zeros((1, 1), F32)
    w_pack = small(zero, final_g, norm_g, conv_b, ls_pool, mh_norm_g, b_gates, b_ada, w_pool)
    m_pack = small(zero, m_final_g, m_norm_g, m_conv_b, m_ls_pool, m_mh_norm_g, m_b_gates, m_b_ada, m_w_pool)
    v_pack = small(zero, v_final_g, v_norm_g, v_conv_b, v_ls_pool, v_mh_norm_g, v_b_gates, v_b_ada, v_w_pool)
    gp, dp, mp, vp = _adam_sum("adam_small", p_pack, w_pack, m_pack, v_pack, PACK_TOTAL)

    off, rows = PACK_OFF["b_ada"]
    dmod_all = p_pack[:, off:off + rows, :].reshape(N_DEV, 3 * D_MODEL)
    dmod_blk16 = jnp.pad(lax.dynamic_slice(dmod_all, (0, me * ADA_SHARD), (N_DEV, ADA_SHARD)), ((0, 8), (0, 0)))
    ga, da, ma, va = _adam_ada(sc_all16, dmod_blk16, w_ada[0], m_w_ada[0], v_w_ada[0])

    names = ("norm_g", "w_ada", "b_ada", "w_in", "b_gates", "conv_w", "conv_b", "w_pool", "ls_pool", "mh_norm_g",
             "w_out", "final_g")
    shapes = dict(norm_g=norm_g.shape, b_ada=b_ada.shape, b_gates=b_gates.shape, conv_b=conv_b.shape,
                  w_pool=w_pool.shape, ls_pool=ls_pool.shape, mh_norm_g=mh_norm_g.shape, final_g=final_g.shape)
    sharded = dict(w_ada=(ga, da, ma, va), w_in=(gi, di, mi, vi), conv_w=(gc, dc, mc, vc), w_out=(go, do_, mo, vo))
    outs = []
    for kind in range(4):
        for nm in names:
            if nm in sharded:
                outs.append(sharded[nm][kind][None])
            else:
                outs.append(_unpack_small((gp, dp, mp, vp)[kind], nm, shapes[nm]))
    loss = gp[0, 0]
    grad_x = r["grad_x"].reshape(1, seq, D_MODEL)
    return (loss, grad_x, *outs)
```

```python
import jax
import jax.numpy as jnp
from jax import lax
from jax.experimental import pallas as pl
from jax.experimental.pallas import tpu as pltpu

F32 = jnp.float32
BF16 = jnp.bfloat16

D_MODEL = 1024
D_POOL = 512
D_MLSTM = 512
N_HEADS = 4
HEAD_DIM = 128
CHUNK = 128
POOL_WINDOWS = (2, 4, 8, 16)
POOL_GROUP_DIM = 128
CONV_WIDTH = 4
EPS = 1e-6
N_MAIN = 3584
N_IN = 3592
N_PAD = 3840
N_SHARD = N_IN // 8
ADA_SHARD = 3 * D_MODEL // 8
N_DEV = 8
CONV_HALO = 8
POOL_HALO = 16
NEG_BIG = -1e30
VMEM_LIMIT_BYTES = 56 * 1024 * 1024

ADAM_LR = 0.001
ADAM_B1 = 0.9
ADAM_B2 = 0.999
ADAM_EPS = 1e-08
ADAM_WD = 0.01
ADAM_STEP = 10

PACK_ROWS = (("loss", 8), ("final_g", 8), ("norm_g", 8), ("conv_b", 8), ("ls_pool", 8),
             ("mh_norm_g", 8), ("b_gates", 8), ("b_ada", 24), ("w_pool", 512))
PACK_TOTAL = sum(r for _, r in PACK_ROWS)


def _pack_offsets():
    off, out = 0, {}
    for name, rows in PACK_ROWS:
        out[name] = (off, rows)
        off += rows
    return out


PACK_OFF = _pack_offsets()


def _dot(a, b):
    return jnp.dot(a, b, preferred_element_type=F32)


def _dot_nt(a, b):
    return lax.dot_general(a, b, (((1,), (1,)), ((), ())), preferred_element_type=F32)


def _dot_tn(a, b):
    return lax.dot_general(a, b, (((0,), (0,)), ((), ())), preferred_element_type=F32)


def _dot_f32(a, b):
    return jnp.dot(a, b, precision=lax.Precision.HIGHEST, preferred_element_type=F32)


def _sigmoid(x):
    return jax.nn.sigmoid(x)


def _log_sigmoid(x):
    return jnp.minimum(x, 0.0) - jnp.log1p(jnp.exp(-jnp.abs(x)))


def _params(sem):
    return pltpu.CompilerParams(dimension_semantics=sem, vmem_limit_bytes=VMEM_LIMIT_BYTES)


def _full(shape):
    n = len(shape)
    return pl.BlockSpec(shape, lambda *_: (0,) * n)


def _mesh_pos():
    return lax.axis_index("x"), lax.axis_index("y"), lax.axis_index("c")


def _peer(k):
    x, y, c = _mesh_pos()
    px = 1 - x if (k >> 2) & 1 else x
    py = 1 - y if (k >> 1) & 1 else y
    pc = 1 - c if k & 1 else c
    return (px, py, pc), 4 * px + 2 * py + pc


def _remote(src, dst, send_sem, recv_sem, to):
    return pltpu.make_async_remote_copy(src_ref=src, dst_ref=dst, send_sem=send_sem, recv_sem=recv_sem, device_id=to,
                                        device_id_type=pl.DeviceIdType.MESH)


def _other_chips():
    x, y, _ = _mesh_pos()
    return [(1 - x, y), (x, 1 - y), (1 - x, 1 - y)]


def _two_level_gather(src, dst, send_sems, recv_sems, local_sems):
    n = len(src)
    x, y, c = _mesh_pos()
    me = 4 * x + 2 * y + c
    sibling = (x, y, 1 - c)
    chips = _other_chips()

    def copy(a, k, block, to, own):
        return _remote(src[a] if own else dst[a].at[block], dst[a].at[block], send_sems.at[a, k], recv_sems.at[a, k], to)

    local = [pltpu.make_async_copy(src[a], dst[a].at[me], local_sems.at[a]) for a in range(n)]
    first = [copy(a, 0, me, sibling, True) for a in range(n)]
    first += [copy(a, 1 + j, me, (*chip, c), True) for j, chip in enumerate(chips) for a in range(n)]
    for cp in local + first:
        cp.start()
    passed = []
    for j, (px, py) in enumerate(chips):
        block = 4 * px + 2 * py + c
        for a in range(n):
            copy(a, 1 + j, block, sibling, False).wait_recv()
            passed.append(copy(a, 4 + j, block, sibling, False))
            passed[-1].start()
    for a in range(n):
        copy(a, 0, 4 * x + 2 * y + (1 - c), sibling, False).wait_recv()
    for j, (px, py) in enumerate(chips):
        for a in range(n):
            copy(a, 4 + j, 4 * px + 2 * py + (1 - c), sibling, False).wait_recv()
    for cp in first + passed:
        cp.wait_send()
    for cp in local:
        cp.wait()


GATHER_COPIES = 7


def _all_gather(name, *shards):
    n = len(shards)

    def body(*refs):
        _two_level_gather(refs[:n], refs[n:2 * n], *refs[2 * n:])

    hbm = pl.BlockSpec(memory_space=pltpu.HBM)
    return pl.pallas_call(
        body, name=name,
        out_shape=tuple(jax.ShapeDtypeStruct((N_DEV,) + s.shape, s.dtype) for s in shards),
        in_specs=[hbm] * n, out_specs=tuple([hbm] * n),
        scratch_shapes=[pltpu.SemaphoreType.DMA((n, GATHER_COPIES)), pltpu.SemaphoreType.DMA((n, GATHER_COPIES)),
                        pltpu.SemaphoreType.DMA((n,))],
    )(*shards)


def _ada_mod(c_all16, w_ada_blk, b_ada_blk):
    def body(c_ref, w_ref, b_ref, out_ref, sc_ref, send_sems, recv_sems):
        x, y, c = _mesh_pos()
        me = 4 * x + 2 * y + c
        cv = c_ref[...]
        sc = cv * _sigmoid(cv)
        sc_ref[...] = sc
        blk = _dot(sc.astype(BF16), w_ref[...].astype(BF16)) + b_ref[...]
        out_ref[me] = blk[0:N_DEV, :]
        copies = []
        for k in range(1, N_DEV):
            peer, _ = _peer(k)
            copies.append(pltpu.make_async_remote_copy(
                src_ref=out_ref.at[me], dst_ref=out_ref.at[me], send_sem=send_sems.at[k - 1],
                recv_sem=recv_sems.at[k - 1], device_id=peer, device_id_type=pl.DeviceIdType.MESH))
        for cp in copies:
            cp.start()
        for cp in copies:
            cp.wait()

    vmem = pl.BlockSpec(memory_space=pltpu.VMEM)
    return pl.pallas_call(
        body, name="ada_mod",
        out_shape=(jax.ShapeDtypeStruct((N_DEV, N_DEV, ADA_SHARD), F32),
                   jax.ShapeDtypeStruct(c_all16.shape, F32)),
        in_specs=[vmem] * 3, out_specs=(vmem, vmem),
        scratch_shapes=[pltpu.SemaphoreType.DMA((N_DEV - 1,)), pltpu.SemaphoreType.DMA((N_DEV - 1,))],
    )(c_all16, w_ada_blk, b_ada_blk)


def _scatter_copies(src, land, send_sems, recv_sems):
    x, y, c = _mesh_pos()
    me = 4 * x + 2 * y + c
    copies = []
    for k in range(1, N_DEV):
        peer, p = _peer(k)
        for a in range(len(src)):
            i = a * (N_DEV - 1) + k - 1
            copies.append(_remote(src[a].at[p], land[a].at[me], send_sems.at[i], recv_sems.at[i], peer))
    return copies


def _scatter_start(name, blocks):
    n = len(blocks)

    def body(*refs):
        src, land = refs[:n], refs[n:2 * n]
        send_sems, recv_sems = refs[2 * n], refs[2 * n + 1]
        token_ref = refs[-1]
        for cp in _scatter_copies(src, land, send_sems, recv_sems):
            cp.start()
        token_ref[...] = jnp.zeros_like(token_ref)

    hbm = pl.BlockSpec(memory_space=pltpu.HBM)
    sem = pl.BlockSpec(memory_space=pltpu.SEMAPHORE)
    through = tuple(pltpu.HBM(b.shape, b.dtype) for b in blocks)
    args = [pltpu.with_memory_space_constraint(b, pltpu.HBM) for b in blocks]
    args += [pltpu.with_memory_space_constraint(lax.empty(b.shape, b.dtype), pltpu.HBM) for b in blocks]
    out = pl.pallas_call(
        body, name=name,
        out_shape=(pltpu.SemaphoreType.DMA((n * (N_DEV - 1),)),) * 2 + through + through
        + (jax.ShapeDtypeStruct((8, 128), F32),),
        in_specs=[hbm] * (2 * n),
        out_specs=(sem, sem) + (hbm,) * (2 * n) + (pl.BlockSpec(memory_space=pltpu.VMEM),),
        input_output_aliases={i: 2 + i for i in range(2 * n)},
        compiler_params=pltpu.CompilerParams(has_side_effects=pltpu.SideEffectType.DATAFLOW_SIDE_EFFECTING),
    )(*args)
    return out[:-1], out[-1][0:1, 0:1]


def _scatter_wait(name, state, after):
    n = (len(state) - 2) // 2
    send_sems, recv_sems = state[0], state[1]
    src, land = state[2:2 + n], state[2 + n:]

    def body(*refs):
        src_r, land_r = refs[:n], refs[n:2 * n]
        for cp in _scatter_copies(src_r, land_r, refs[2 * n], refs[2 * n + 1]):
            cp.wait_send()
            cp.wait_recv()

    hbm = pl.BlockSpec(memory_space=pltpu.HBM)
    sem = pl.BlockSpec(memory_space=pltpu.SEMAPHORE)
    out = pl.pallas_call(
        body, name=name,
        out_shape=tuple(pltpu.HBM(b.shape, b.dtype) for b in src + land),
        in_specs=[hbm] * (2 * n) + [sem, sem, pl.BlockSpec(memory_space=pl.ANY)],
        out_specs=(hbm,) * (2 * n),
        input_output_aliases={i: i for i in range(2 * n)},
        compiler_params=pltpu.CompilerParams(has_side_effects=pltpu.SideEffectType.DATAFLOW_SIDE_EFFECTING),
    )(*src, *land, send_sems, recv_sems, after)
    me = 4 * lax.axis_index("x") + 2 * lax.axis_index("y") + lax.axis_index("c")
    landed = []
    for a in range(n):
        own = lax.dynamic_index_in_dim(out[a], me, axis=0, keepdims=True)
        landed.append(lax.dynamic_update_slice_in_dim(out[n + a], own, me, axis=0))
    return landed


def _fwd_proj(x, norm_g, scale, shift, w_in_t):
    seq = x.shape[0]
    tm = min(512, seq)
    sub = min(256, tm)
    tn = 512

    def body(x_ref, ng_ref, sc_ref, sh_ref, wt_ref, proj_ref, gates_ref, h_ref):
        def chain(n):
            for _ in range(n):
                yield
            rows = slice(n * sub, (n + 1) * sub)
            xt = x_ref[rows, :]
            r = lax.rsqrt(jnp.mean(xt * xt, axis=-1, keepdims=True) + EPS)
            h = ((xt * r) * ng_ref[...]) * (1.0 + sc_ref[...]) + sh_ref[...]
            hb = h.astype(BF16)
            h_ref[rows, :] = hb
            yield
            gates_ref[rows, :] = _dot_nt(hb, wt_ref[N_MAIN:N_MAIN + 128, :])
            for j in range(N_MAIN // tn):
                proj_ref[rows, j * tn:(j + 1) * tn] = _dot_nt(hb, wt_ref[j * tn:(j + 1) * tn, :])

        _in_lockstep(chain(n) for n in range(tm // sub))

    vec = _full((1, D_MODEL))
    tile = pl.BlockSpec((tm, D_MODEL), lambda i: (i, 0))
    return pl.pallas_call(
        body, name="fwd_proj", grid=(seq // tm,),
        out_shape=(jax.ShapeDtypeStruct((seq, N_MAIN), F32), jax.ShapeDtypeStruct((seq, 128), F32),
                   jax.ShapeDtypeStruct((seq, D_MODEL), BF16)),
        in_specs=[tile, vec, vec, vec, _full((N_PAD, D_MODEL))],
        out_specs=(pl.BlockSpec((tm, N_MAIN), lambda i: (i, 0)), pl.BlockSpec((tm, 128), lambda i: (i, 0)), tile),
        compiler_params=_params(("arbitrary",)),
    )(x, norm_g, scale, shift, w_in_t)


def _gate_forms(gpre):
    r = lax.broadcasted_iota(jnp.int32, (CHUNK, CHUNK), 0)
    c = lax.broadcasted_iota(jnp.int32, (CHUNK, CHUNK), 1)
    causal = c <= r
    ltri = jnp.where(causal, 1.0, 0.0).astype(F32)
    utri = jnp.where(r <= c, 1.0, 0.0).astype(F32)
    bcol = _dot_f32(ltri, _log_sigmoid(gpre))
    gt8 = gpre.T[0:8, :]
    brow = _dot_f32(_log_sigmoid(gt8), utri)
    return causal, utri, bcol, gt8, brow


def _in_lockstep(stages):
    alive = list(stages)
    while alive:
        still = []
        for g in alive:
            try:
                next(g)
                still.append(g)
            except StopIteration:
                pass
        alive = still


def _head_fwd(qh, kh, vh, bc, br, igr, m_prev, c_h, n_row, causal):
    qb, kb, vb, cb = qh.astype(BF16), kh.astype(BF16), vh.astype(BF16), c_h.astype(BF16)
    qk = _dot_nt(qb, kb)
    cq = _dot_nt(qb, cb)
    yield
    dlog = jnp.where(causal, bc - br + igr, NEG_BIG)
    inter_log = bc + m_prev
    m_t = jnp.maximum(inter_log, jnp.max(dlog, axis=-1, keepdims=True))
    yield
    dmat = jnp.exp(dlog - m_t)
    inter = jnp.exp(inter_log - m_t)
    s = qk * dmat
    sv = _dot(s.astype(BF16), vb)
    yield
    nq = jnp.sum(qh * n_row, axis=-1, keepdims=True)
    den = jnp.sum(s, axis=-1, keepdims=True) + inter * nq
    emt = jnp.exp(-m_t)
    yield
    num = sv + inter * cq
    dn = jnp.maximum(jnp.abs(den), emt)
    hm = num / dn
    return dict(dmat=dmat, inter=inter, qb=qb, kb=kb, vb=vb, cb=cb, s=s, cq=cq, nq=nq, den=den, emt=emt,
                dn=dn, hm=hm)


def _state_weights(bc, igc, m_prev, m_new=None):
    last = lax.broadcasted_iota(jnp.int32, (CHUNK, 1), 0) == CHUNK - 1
    b_last = jnp.sum(jnp.where(last, bc, 0.0), axis=0, keepdims=True)
    wlog = b_last - bc + igc
    if m_new is None:
        m_new = jnp.maximum(b_last + m_prev, jnp.max(wlog, axis=0, keepdims=True))
    w_c = jnp.exp(wlog - m_new)
    decay = jnp.exp(b_last + m_prev - m_new)
    return w_c, decay, m_new, last


def _rows_back(x, k):
    return x if k == 0 else pltpu.roll(x, k, 0)


def _rows_ahead(x, k):
    return x if k == 0 else pltpu.roll(x, x.shape[0] - k, 0)


def _conv_taps(xpad):
    return [_rows_back(xpad, CONV_WIDTH - 1 - j)[CONV_HALO:, :] for j in range(CONV_WIDTH)]


def _conv_pre(taps, cw_ref, cb_ref):
    a = cb_ref[...]
    for j in range(CONV_WIDTH):
        a = a + cw_ref[j:j + 1, :] * taps[j]
    return a


def _window_sum(x, w, shift):
    k = 1
    while k < w:
        x = x + shift(x, k)
        k *= 2
    return x


def _pool_window_sum(upad_ref, g, w):
    lanes = slice(g * POOL_GROUP_DIM, (g + 1) * POOL_GROUP_DIM)
    return _window_sum(upad_ref[:, lanes], w, _rows_back)[POOL_HALO:, :]


def _pool_inv_count(row0, rows, w):
    pos = row0 + lax.broadcasted_iota(jnp.int32, (rows, 1), 0) + 1
    return 1.0 / jnp.minimum(pos, w).astype(F32)


FWD_CHUNKS = 2
BWD_CHUNKS = 2


def _mix_fwd(proj, gates, bg_pad, conv_w8, conv_b, w_pool, ls_pool, mh_g):
    seq = proj.shape[0]
    n_chunks = seq // CHUNK
    per_step = FWD_CHUNKS
    blk = per_step * CHUNK

    def body(uz_ref, qk_ref, v_ref, o_ref, zm_ref, uh_ref, qkh_ref, g_ref, bg_ref, cw_ref, cb_ref, wp_ref,
             ls_ref, mhg_ref, mix_ref, cst_ref, nst_ref, mst_ref, a_ref, pooled_ref, c_scr, n_scr, m_scr, xpad, upad):
        i = pl.program_id(0)

        @pl.when(i == 0)
        def _():
            c_scr[...] = jnp.zeros_like(c_scr)
            n_scr[...] = jnp.zeros_like(n_scr)
            m_scr[...] = jnp.zeros_like(m_scr)

        first = i == 0

        upad[0:POOL_HALO, :] = jnp.where(first, 0.0, uh_ref[...])
        upad[POOL_HALO:POOL_HALO + blk, :] = uz_ref[:, 0:D_POOL]
        for g, w in enumerate(POOL_WINDOWS):
            lanes = slice(g * POOL_GROUP_DIM, (g + 1) * POOL_GROUP_DIM)
            pooled = (_pool_window_sum(upad, g, w) * _pool_inv_count(i * blk, blk, w) - uz_ref[:, lanes]).astype(BF16)
            pooled_ref[:, lanes] = pooled
            y = _dot(pooled, wp_ref[g].astype(BF16)) * ls_ref[:, lanes]
            zp = uz_ref[:, D_POOL + g * POOL_GROUP_DIM:D_POOL + (g + 1) * POOL_GROUP_DIM]
            mix_ref[:, lanes] = (y * (zp * _sigmoid(zp))).astype(BF16)

        xpad[0:CONV_HALO, :] = jnp.where(first, 0.0, qkh_ref[...])
        xpad[CONV_HALO:CONV_HALO + blk, :] = qk_ref[...]
        a = _conv_pre(_conv_taps(xpad[...]), cw_ref, cb_ref)
        a_ref[...] = a
        qk = a * _sigmoid(a)

        def head(rows, h, qh, kh, vh, bc, br, igr, m_prev, c_h, n_row, causal):
            lanes = slice(h * HEAD_DIM, (h + 1) * HEAD_DIM)
            f = yield from _head_fwd(qh, kh, vh, bc, br, igr, m_prev, c_h, n_row, causal)
            yield
            hm = f["hm"]
            hn = hm * lax.rsqrt(jnp.mean(hm * hm, axis=-1, keepdims=True) + EPS) * mhg_ref[:, lanes]
            zm = zm_ref[rows, lanes]
            out = hn * _sigmoid(o_ref[rows, lanes]) * (zm * _sigmoid(zm))
            mix_ref[rows, D_POOL + h * HEAD_DIM:D_POOL + (h + 1) * HEAD_DIM] = out.astype(BF16)

        c_cur = [c_scr[h] for h in range(N_HEADS)]
        n_cur = [n_scr[h:h + 1, :] for h in range(N_HEADS)]
        m_cur = [m_scr[h:h + 1, 0:1] for h in range(N_HEADS)]
        chains = []
        for s in range(per_step):
            rows = slice(s * CHUNK, (s + 1) * CHUNK)
            gpre = g_ref[rows, :] + bg_ref[...]
            causal, _, bcol, gt8, brow = _gate_forms(gpre)
            nst_ref[s] = jnp.zeros((8, 128), F32)
            mst_ref[s] = jnp.zeros((8, 128), F32)
            for h in range(N_HEADS):
                lanes = slice(h * HEAD_DIM, (h + 1) * HEAD_DIM)
                cst_ref[s, h] = c_cur[h]
                nst_ref[s, h:h + 1, :] = n_cur[h]
                mst_ref[s, h:h + 1, :] = jnp.broadcast_to(m_cur[h], (1, 128))
                qh = qk[rows, lanes]
                kh = qk[rows, D_MLSTM + h * HEAD_DIM:D_MLSTM + (h + 1) * HEAD_DIM] * (HEAD_DIM ** -0.5)
                vh = v_ref[rows, lanes]
                bc = bcol[:, N_HEADS + h:N_HEADS + h + 1]
                br = brow[N_HEADS + h:N_HEADS + h + 1, :]
                igr = gt8[h:h + 1, :]
                igc = gpre[:, h:h + 1]
                chains.append(head(rows, h, qh, kh, vh, bc, br, igr, m_cur[h], c_cur[h], n_cur[h], causal))
                w_c, decay, m_new, _ = _state_weights(bc, igc, m_cur[h])
                c_cur[h] = decay * c_cur[h] + _dot_tn((vh * w_c).astype(BF16), kh.astype(BF16))
                n_cur[h] = decay * n_cur[h] + jnp.sum(w_c * kh, axis=0, keepdims=True)
                m_cur[h] = m_new
        for h in range(N_HEADS):
            c_scr[h] = c_cur[h]
            n_scr[h:h + 1, :] = n_cur[h]
            m_scr[h:h + 1, :] = jnp.broadcast_to(m_cur[h], (1, 128))
        _in_lockstep(chains)

    in_specs = [
        pl.BlockSpec((blk, 1024), lambda i: (i, 0)),
        pl.BlockSpec((blk, 1024), lambda i: (i, 1)),
        pl.BlockSpec((blk, 512), lambda i: (i, 4)),
        pl.BlockSpec((blk, 512), lambda i: (i, 5)),
        pl.BlockSpec((blk, 512), lambda i: (i, 6)),
        pl.BlockSpec((POOL_HALO, 512), lambda i: (jnp.maximum(i * (blk // POOL_HALO) - 1, 0), 0)),
        pl.BlockSpec((CONV_HALO, 1024), lambda i: (jnp.maximum(i * (blk // CONV_HALO) - 1, 0), 1)),
        pl.BlockSpec((blk, 128), lambda i: (i, 0)),
        _full((1, 128)), _full((8, 1024)), _full((1, 1024)), _full((4, 128, 128)), _full((1, 512)),
        _full((1, 512))]
    return pl.pallas_call(
        body, name="mix_fwd", grid=(n_chunks // per_step,),
        out_shape=(jax.ShapeDtypeStruct((seq, D_MODEL), BF16),
                   jax.ShapeDtypeStruct((n_chunks, N_HEADS, HEAD_DIM, HEAD_DIM), F32),
                   jax.ShapeDtypeStruct((n_chunks, 8, 128), F32),
                   jax.ShapeDtypeStruct((n_chunks, 8, 128), F32),
                   jax.ShapeDtypeStruct((seq, 2 * D_MLSTM), F32),
                   jax.ShapeDtypeStruct((seq, D_POOL), BF16)),
        in_specs=in_specs,
        out_specs=(pl.BlockSpec((blk, D_MODEL), lambda i: (i, 0)),
                   pl.BlockSpec((per_step, N_HEADS, HEAD_DIM, HEAD_DIM), lambda i: (i, 0, 0, 0)),
                   pl.BlockSpec((per_step, 8, 128), lambda i: (i, 0, 0)),
                   pl.BlockSpec((per_step, 8, 128), lambda i: (i, 0, 0)),
                   pl.BlockSpec((blk, 2 * D_MLSTM), lambda i: (i, 0)),
                   pl.BlockSpec((blk, D_POOL), lambda i: (i, 0))),
        scratch_shapes=[pltpu.VMEM((N_HEADS, HEAD_DIM, HEAD_DIM), F32), pltpu.VMEM((8, 128), F32),
                        pltpu.VMEM((8, 128), F32), pltpu.VMEM((CONV_HALO + blk, 1024), F32),
                        pltpu.VMEM((POOL_HALO + blk, D_POOL), F32)],
        compiler_params=_params(("arbitrary",)),
    )(proj, proj, proj, proj, proj, proj, proj, gates, bg_pad, conv_w8, conv_b, w_pool, ls_pool, mh_g)


def _out_fwd_bwd(mix, x, tgt, w_out_b, gate, final_g):
    seq = x.shape[0]
    tm = min(512, seq)
    sub = min(256, tm)

    def body(mix_ref, x_ref, t_ref, w_ref, gate_ref, fg_ref, dx2_ref, dmix_ref, dwo_ref, dgate_ref, dfg_ref,
             loss_ref):
        @pl.when(pl.program_id(0) == 0)
        def _():
            dwo_ref[...] = jnp.zeros_like(dwo_ref)
            dgate_ref[...] = jnp.zeros_like(dgate_ref)
            dfg_ref[...] = jnp.zeros_like(dfg_ref)
            loss_ref[...] = jnp.zeros_like(loss_ref)

        w = w_ref[...]
        gate_v = gate_ref[...]
        fg = fg_ref[...]
        do2_parts = [None] * (tm // sub)

        def chain(n):
            rows = slice(n * sub, (n + 1) * sub)
            o2 = _dot(mix_ref[rows, :], w)
            yield
            x2 = x_ref[rows, :] + gate_v * o2
            r2 = lax.rsqrt(jnp.mean(x2 * x2, axis=-1, keepdims=True) + EPS)
            x2n = x2 * r2
            err = x2n * fg - t_ref[rows, :]
            part = 0.5 * jnp.sum(jnp.sum(err * err, axis=-1, keepdims=True), axis=0, keepdims=True) / D_MODEL
            loss_ref[...] += jnp.broadcast_to(part, loss_ref.shape)
            dy = err / D_MODEL
            dfg_ref[...] += jnp.sum(dy * x2n, axis=0, keepdims=True)
            gdy = dy * fg
            dx2 = r2 * (gdy - x2n * jnp.mean(gdy * x2n, axis=-1, keepdims=True))
            dx2_ref[rows, :] = dx2
            dgate_ref[...] += jnp.sum(dx2 * o2, axis=0, keepdims=True)
            do2 = (dx2 * gate_v).astype(BF16)
            dmix_ref[rows, :] = _dot_nt(do2, w)
            do2_parts[n] = do2

        _in_lockstep(chain(n) for n in range(tm // sub))
        dwo_ref[...] += _dot_tn(mix_ref[...], jnp.concatenate(do2_parts, axis=0))

    tile = pl.BlockSpec((tm, D_MODEL), lambda i: (i, 0))
    vec = _full((1, D_MODEL))
    return pl.pallas_call(
        body, name="out_fwd_bwd", grid=(seq // tm,),
        out_shape=(jax.ShapeDtypeStruct((seq, D_MODEL), F32), jax.ShapeDtypeStruct((seq, D_MODEL), F32),
                   jax.ShapeDtypeStruct((D_MODEL, D_MODEL), F32), jax.ShapeDtypeStruct((1, D_MODEL), F32),
                   jax.ShapeDtypeStruct((1, D_MODEL), F32), jax.ShapeDtypeStruct((1, 128), F32)),
        in_specs=[tile, tile, tile, _full((D_MODEL, D_MODEL)), vec, vec],
        out_specs=(tile, tile, _full((D_MODEL, D_MODEL)), vec, vec, _full((1, 128))),
        compiler_params=_params(("arbitrary",)),
    )(mix, x, tgt, w_out_b, gate, final_g)


def _mix_bwd(proj, gates, dmix, conv_a, pooled, cst, nst, mst, bg_pad, conv_w8, w_pool, ls_pool, mh_g):
    seq = proj.shape[0]
    n_chunks = seq // CHUNK
    per_step = BWD_CHUNKS
    blk = per_step * CHUNK
    n_blocks = n_chunks // per_step

    def body(zp_ref, qk_ref, v_ref, o_ref, zm_ref, g_ref, dmix_ref, a_ref, pooled_ref, cst_ref, nst_ref, mst_ref,
             mnx_ref, bg_ref, cw_ref, wp_ref, ls_ref, mhg_ref,
             dp_ref, dcw_ref, dcb_ref, dwp_ref, dls_ref, dmhg_ref, dbg_ref,
             dc_scr, dn_scr, dapad, dpipad):
        i = pl.program_id(0)
        bi = n_blocks - 1 - i

        @pl.when(i == 0)
        def _():
            for ref in (dc_scr, dn_scr, dcw_ref, dcb_ref, dwp_ref, dls_ref, dmhg_ref, dbg_ref):
                ref[...] = jnp.zeros_like(ref)
            dapad[blk:blk + CONV_HALO, :] = jnp.zeros((CONV_HALO, 1024), F32)
            dpipad[blk:blk + POOL_HALO, :] = jnp.zeros((POOL_HALO, D_POOL), F32)

        dpooled = []
        for g, w in enumerate(POOL_WINDOWS):
            lanes = slice(g * POOL_GROUP_DIM, (g + 1) * POOL_GROUP_DIM)
            zlanes = slice(D_POOL + g * POOL_GROUP_DIM, D_POOL + (g + 1) * POOL_GROUP_DIM)
            inv = _pool_inv_count(bi * blk, blk, w)
            pb = pooled_ref[:, lanes]
            wpb = wp_ref[g].astype(BF16)
            yw = _dot(pb, wpb)
            ls = ls_ref[:, lanes]
            zp = zp_ref[:, lanes]
            sg = _sigmoid(zp)
            dpo = dmix_ref[:, lanes]
            dp_ref[:, zlanes] = (dpo * (yw * ls) * (sg * (1.0 + zp * (1.0 - sg)))).astype(BF16)
            dy = dpo * (zp * sg)
            dls_ref[:, lanes] += jnp.sum(dy * yw, axis=0, keepdims=True)
            dyw = (dy * ls).astype(BF16)
            dwp_ref[g] += _dot_tn(pb, dyw)
            dpl = _dot_nt(dyw, wpb)
            dpooled.append(dpl)
            dpipad[0:blk, lanes] = dpl * inv
        for g, w in enumerate(POOL_WINDOWS):
            lanes = slice(g * POOL_GROUP_DIM, (g + 1) * POOL_GROUP_DIM)
            du = _window_sum(dpipad[:, lanes], w, _rows_ahead)[0:blk, :] - dpooled[g]
            dp_ref[:, lanes] = du.astype(BF16)
        dpipad[blk:blk + POOL_HALO, :] = dpipad[0:POOL_HALO, :]

        a = a_ref[...]
        sga = _sigmoid(a)
        qk = a * sga
        dsilu_a = sga * (1.0 + a * (1.0 - sga))

        lane = lax.broadcasted_iota(jnp.int32, (CHUNK, 128), 1)
        row = lax.broadcasted_iota(jnp.int32, (CHUNK, 128), 0)
        scale_k = HEAD_DIM ** -0.5
        forms = [None] * per_step
        col_g_rows = [[] for _ in range(per_step)]
        dig_parts = [[] for _ in range(per_step)]
        db_parts = [[] for _ in range(per_step)]
        d_state = [[None] * N_HEADS for _ in range(per_step)]

        def state_terms(s, h, c_h, n_row, vb, kb):
            dcn, dnn = d_state[s][h]
            dcnb = dcn.astype(BF16)
            amat = _dot(vb, dcnb) + dnn
            kdc = _dot_nt(kb, dcnb)
            ddecay = (jnp.sum(jnp.sum(dcn * c_h, axis=-1, keepdims=True), axis=0, keepdims=True)
                      + jnp.sum(dnn * n_row, axis=-1, keepdims=True))
            return dcn, dnn, amat, kdc, ddecay

        def head(s, h):
            rows = slice(s * CHUNK, (s + 1) * CHUNK)
            lanes = slice(h * HEAD_DIM, (h + 1) * HEAD_DIM)
            klanes = slice(D_MLSTM + h * HEAD_DIM, D_MLSTM + (h + 1) * HEAD_DIM)
            gpre, causal, utri, bcol, gt8, brow = forms[s]
            qh = qk[rows, lanes]
            kh = qk[rows, klanes] * scale_k
            vh = v_ref[rows, lanes]
            bc = bcol[:, N_HEADS + h:N_HEADS + h + 1]
            br = brow[N_HEADS + h:N_HEADS + h + 1, :]
            igr = gt8[h:h + 1, :]
            igc = gpre[:, h:h + 1]
            m_prev = mst_ref[s, h:h + 1, 0:1]
            m_next = mnx_ref[0, h:h + 1, 0:1] if s == per_step - 1 else mst_ref[s + 1, h:h + 1, 0:1]
            c_h = cst_ref[s, h]
            n_row = nst_ref[s, h:h + 1, :]
            w_c, decay, _, last = _state_weights(bc, igc, m_prev, m_next)
            terms = None
            if s == per_step - 1:
                terms = state_terms(s, h, c_h, n_row, vh.astype(BF16), kh.astype(BF16))
            f = yield from _head_fwd(qh, kh, vh, bc, br, igr, m_prev, c_h, n_row, causal)
            qb, kb, vb, cb = f["qb"], f["kb"], f["vb"], f["cb"]
            sm, dmat, inter, den, dn, hm = f["s"], f["dmat"], f["inter"], f["den"], f["dn"], f["hm"]
            yield

            rinv = lax.rsqrt(jnp.mean(hm * hm, axis=-1, keepdims=True) + EPS)
            hmn = hm * rinv
            gh = mhg_ref[:, lanes]
            o_pre = o_ref[rows, lanes]
            og = _sigmoid(o_pre)
            zm = zm_ref[rows, lanes]
            sgz = _sigmoid(zm)
            sz = zm * sgz
            dout = dmix_ref[rows, D_POOL + h * HEAD_DIM:D_POOL + (h + 1) * HEAD_DIM]
            hn = hmn * gh
            dp_ref[rows, 2560 + h * HEAD_DIM:2560 + (h + 1) * HEAD_DIM] = (
                dout * hn * sz * og * (1.0 - og)).astype(BF16)
            dp_ref[rows, 3072 + h * HEAD_DIM:3072 + (h + 1) * HEAD_DIM] = (
                dout * hn * og * (sgz * (1.0 + zm * (1.0 - sgz)))).astype(BF16)
            dhn = dout * og * sz
            dmhg_ref[:, lanes] += jnp.sum(dhn * hmn, axis=0, keepdims=True)
            dyn = dhn * gh
            dhm = rinv * (dyn - hmn * jnp.mean(dyn * hmn, axis=-1, keepdims=True))
            yield

            inv_dn = 1.0 / dn
            dnum = dhm * inv_dn
            hd = jnp.sum(dhm * hm, axis=-1, keepdims=True)
            dden = jnp.where(jnp.abs(den) > f["emt"], -hd * inv_dn * jnp.sign(den), 0.0)
            dnb = dnum.astype(BF16)
            dnv = _dot_nt(dnb, vb)
            dv = _dot_tn(sm.astype(BF16), dnb)
            dnc = _dot(dnb, cb)
            dc_prev = _dot_tn((inter * dnum).astype(BF16), qb)
            dn_prev = jnp.sum((inter * dden) * qh, axis=0, keepdims=True)
            yield
            ds = dnv + dden
            dqk = (ds * dmat).astype(BF16)
            dqk_k = _dot(dqk, kb)
            dk = _dot_tn(dqk, qb)
            for _ in range(per_step - 1 - s):
                yield
            if terms is None:
                terms = state_terms(s, h, c_h, n_row, vb, kb)
            dcn, dnn, amat, kdc, ddecay = terms
            d_start = (decay * dcn + dc_prev, decay * dnn + dn_prev)
            if s > 0:
                d_state[s - 1][h] = d_start
            else:
                dc_scr[h] = d_start[0]
                dn_scr[h:h + 1, :] = d_start[1]
            yield
            gmat = ds * sm
            row_g = jnp.sum(gmat, axis=-1, keepdims=True)
            col_g_rows[s].append(jnp.where(row == h, jnp.sum(gmat, axis=0, keepdims=True), 0.0))
            gcol = inter * (jnp.sum(dnum * f["cq"], axis=-1, keepdims=True) + dden * f["nq"])
            dw = jnp.sum(amat * kh, axis=-1, keepdims=True)
            e = dw * w_c
            db_last = ddecay * decay + jnp.sum(e, axis=0, keepdims=True)
            dig_parts[s].append(jnp.where(lane == h, e, 0.0))
            db_parts[s].append(
                jnp.where(lane == N_HEADS + h, row_g + gcol - e + jnp.where(last, db_last, 0.0), 0.0))
            yield
            dq = dqk_k + inter * (dnc + dden * n_row)
            dp_ref[rows, 2048 + h * HEAD_DIM:2048 + (h + 1) * HEAD_DIM] = (dv + w_c * kdc).astype(BF16)
            dapad[rows, lanes] = dq * dsilu_a[rows, lanes]
            dapad[rows, klanes] = (dk + w_c * amat) * scale_k * dsilu_a[rows, klanes]

        chains = []
        for s in reversed(range(per_step)):
            gpre = g_ref[s * CHUNK:(s + 1) * CHUNK, :] + bg_ref[...]
            forms[s] = (gpre,) + _gate_forms(gpre)
            for h in range(N_HEADS):
                if s == per_step - 1:
                    d_state[s][h] = (dc_scr[h], dn_scr[h:h + 1, :])
                chains.append(head(s, h))
        _in_lockstep(chains)

        for s in range(per_step):
            rows = slice(s * CHUNK, (s + 1) * CHUNK)
            gpre, utri = forms[s][0], forms[s][2]
            cs_t = sum(col_g_rows[s][1:], col_g_rows[s][0]).T
            dig_all = sum(dig_parts[s][1:], dig_parts[s][0]) + cs_t
            db_cols = sum(db_parts[s][1:], db_parts[s][0])
            shifted = jnp.zeros((CHUNK, 128), F32)
            for h in range(N_HEADS):
                shifted = shifted + jnp.where(lane == N_HEADS + h, cs_t[:, h:h + 1], 0.0)
            dlf = _dot_f32(utri, db_cols - shifted)
            dgates = dig_all + dlf * _sigmoid(-gpre)
            dp_ref[rows, N_MAIN:N_MAIN + 128] = dgates.astype(BF16)
            dbg_ref[...] += jnp.sum(dgates, axis=0, keepdims=True)
        dp_ref[:, N_MAIN + 128:N_PAD] = jnp.zeros((blk, N_PAD - N_MAIN - 128), BF16)

        da_pad = dapad[...]
        da = da_pad[0:blk, :]
        dcb_ref[...] += jnp.sum(da, axis=0, keepdims=True)
        x = qk_ref[...]
        dx = jnp.zeros((blk, 1024), F32)
        for j in range(CONV_WIDTH):
            da_j = _rows_ahead(da_pad, CONV_WIDTH - 1 - j)[0:blk, :]
            dcw_ref[j:j + 1, :] += jnp.sum(da_j * x, axis=0, keepdims=True)
            dx = dx + cw_ref[j:j + 1, :] * da_j
        dp_ref[:, 1024:2048] = dx.astype(BF16)
        dapad[blk:blk + CONV_HALO, :] = dapad[0:CONV_HALO, :]

    bmap = lambda i: n_blocks - 1 - i
    wide = pl.BlockSpec((blk, 1024), lambda i: (bmap(i), 0))
    state = pl.BlockSpec((per_step, 8, 128), lambda i: (bmap(i), 0, 0))
    in_specs = [
        pl.BlockSpec((blk, 512), lambda i: (bmap(i), 1)),
        pl.BlockSpec((blk, 1024), lambda i: (bmap(i), 1)),
        pl.BlockSpec((blk, 512), lambda i: (bmap(i), 4)),
        pl.BlockSpec((blk, 512), lambda i: (bmap(i), 5)),
        pl.BlockSpec((blk, 512), lambda i: (bmap(i), 6)),
        pl.BlockSpec((blk, 128), lambda i: (bmap(i), 0)),
        wide, wide,
        pl.BlockSpec((blk, D_POOL), lambda i: (bmap(i), 0)),
        pl.BlockSpec((per_step, N_HEADS, HEAD_DIM, HEAD_DIM), lambda i: (bmap(i), 0, 0, 0)),
        state, state,
        pl.BlockSpec((1, 8, 128), lambda i: (jnp.minimum((bmap(i) + 1) * per_step, n_chunks - 1), 0, 0)),
        _full((1, 128)), _full((8, 1024)), _full((4, 128, 128)), _full((1, 512)), _full((1, 512))]
    return pl.pallas_call(
        body, name="mix_bwd", grid=(n_blocks,),
        out_shape=(jax.ShapeDtypeStruct((seq, N_PAD), BF16), jax.ShapeDtypeStruct((8, 1024), F32),
                   jax.ShapeDtypeStruct((1, 1024), F32), jax.ShapeDtypeStruct((4, 128, 128), F32),
                   jax.ShapeDtypeStruct((1, 512), F32), jax.ShapeDtypeStruct((1, 512), F32),
                   jax.ShapeDtypeStruct((1, 128), F32)),
        in_specs=in_specs,
        out_specs=(pl.BlockSpec((blk, N_PAD), lambda i: (bmap(i), 0)), _full((8, 1024)), _full((1, 1024)),
                   _full((4, 128, 128)), _full((1, 512)), _full((1, 512)), _full((1, 128))),
        scratch_shapes=[pltpu.VMEM((N_HEADS, HEAD_DIM, HEAD_DIM), F32), pltpu.VMEM((8, 128), F32),
                        pltpu.VMEM((blk + CONV_HALO, 1024), F32), pltpu.VMEM((blk + POOL_HALO, D_POOL), F32)],
        compiler_params=_params(("arbitrary",)),
    )(proj, proj, proj, proj, proj, gates, dmix, conv_a, pooled, cst, nst, mst, mst, bg_pad, conv_w8,
      w_pool, ls_pool, mh_g)


def _bwd_in(dproj, w_in_t, x, dx2, norm_g, scale):
    seq = x.shape[0]
    tm = min(512, seq)
    sub = min(256, tm)

    def body(dp_ref, wt_ref, x_ref, dx2_ref, ng_ref, sc_ref, gx_ref, dsh_ref, dsc_ref, dng_ref):
        @pl.when(pl.program_id(0) == 0)
        def _():
            dsh_ref[...] = jnp.zeros_like(dsh_ref)
            dsc_ref[...] = jnp.zeros_like(dsc_ref)
            dng_ref[...] = jnp.zeros_like(dng_ref)

        ng = ng_ref[...]
        one_sc = 1.0 + sc_ref[...]

        def chain(n):
            rows = slice(n * sub, (n + 1) * sub)
            dh = _dot(dp_ref[rows, :], wt_ref[...])
            yield
            xt = x_ref[rows, :]
            r = lax.rsqrt(jnp.mean(xt * xt, axis=-1, keepdims=True) + EPS)
            xn = xt * r
            dsh_ref[...] += jnp.sum(dh, axis=0, keepdims=True)
            dhxn = dh * xn
            dsc_ref[...] += jnp.sum(dhxn * ng, axis=0, keepdims=True)
            dng_ref[...] += jnp.sum(dhxn * one_sc, axis=0, keepdims=True)
            dxn = dh * (ng * one_sc)
            gx_ref[rows, :] = r * (dxn - xn * jnp.mean(dxn * xn, axis=-1, keepdims=True)) + dx2_ref[rows, :]

        _in_lockstep(chain(n) for n in range(tm // sub))

    tile = pl.BlockSpec((tm, D_MODEL), lambda i: (i, 0))
    vec = _full((1, D_MODEL))
    return pl.pallas_call(
        body, name="bwd_in", grid=(seq // tm,),
        out_shape=(jax.ShapeDtypeStruct((seq, D_MODEL), F32),) + (jax.ShapeDtypeStruct((1, D_MODEL), F32),) * 3,
        in_specs=[pl.BlockSpec((tm, N_PAD), lambda i: (i, 0)), _full((N_PAD, D_MODEL)), tile, tile, vec, vec],
        out_specs=(tile, vec, vec, vec),
        compiler_params=_params(("arbitrary",)),
    )(dproj, w_in_t, x, dx2, norm_g, scale)


def _dw_in(h_b, dproj):
    seq = h_b.shape[0]
    tk = min(4096, seq)
    tn = 768
    n_t = seq // tk

    def body(h_ref, dp_ref, dwt_ref, acc):
        t = pl.program_id(1)

        @pl.when(t == 0)
        def _():
            acc[...] = jnp.zeros_like(acc)

        acc[...] += _dot_tn(dp_ref[...], h_ref[...])

        @pl.when(t == n_t - 1)
        def _():
            dwt_ref[...] = acc[...].astype(BF16)

    return pl.pallas_call(
        body, name="dw_in", grid=(N_PAD // tn, n_t),
        out_shape=jax.ShapeDtypeStruct((N_PAD, D_MODEL), BF16),
        in_specs=[pl.BlockSpec((tk, D_MODEL), lambda j, t: (t, 0)), pl.BlockSpec((tk, tn), lambda j, t: (t, j))],
        out_specs=pl.BlockSpec((tn, D_MODEL), lambda j, t: (j, 0)),
        scratch_shapes=[pltpu.VMEM((tn, D_MODEL), F32)],
        compiler_params=_params(("arbitrary", "arbitrary")),
    )(h_b, dproj)


def _adam_update(g, w, m, v, g_ref, d_ref, m_ref, v_ref):
    mn = ADAM_B1 * m + (1.0 - ADAM_B1) * g
    vn = ADAM_B2 * v + (1.0 - ADAM_B2) * (g * g)
    m_hat = mn / (1.0 - ADAM_B1 ** ADAM_STEP)
    v_hat = vn / (1.0 - ADAM_B2 ** ADAM_STEP)
    g_ref[...] = g
    d_ref[...] = -ADAM_LR * (m_hat / (jnp.sqrt(v_hat) + ADAM_EPS) + ADAM_WD * w)
    m_ref[...] = mn
    v_ref[...] = vn


def _adam_sum(name, parts, w, m, v, row_tile, col_tile=None):
    rows, cols = w.shape
    col_tile = cols if col_tile is None else col_tile
    n_parts = parts.shape[0]

    def body(p_ref, w_ref, m_ref, v_ref, g_out, d_out, m_out, v_out):
        g = p_ref[0].astype(F32)
        for j in range(1, n_parts):
            g = g + p_ref[j].astype(F32)
        _adam_update(g, w_ref[...], m_ref[...], v_ref[...], g_out, d_out, m_out, v_out)

    tile = pl.BlockSpec((row_tile, col_tile), lambda i, j: (i, j))
    return pl.pallas_call(
        body, name=name, grid=(rows // row_tile, cols // col_tile),
        out_shape=(jax.ShapeDtypeStruct((rows, cols), F32),) * 4,
        in_specs=[pl.BlockSpec((n_parts, row_tile, col_tile), lambda i, j: (0, i, j)), tile, tile, tile],
        out_specs=(tile,) * 4,
        compiler_params=_params(("arbitrary", "arbitrary")),
    )(parts, w, m, v)


def _adam_ada(sc_all16, dmod_blk16, w, m, v):
    rows, cols = w.shape

    def body(sc_ref, dm_ref, w_ref, m_ref, v_ref, g_out, d_out, m_out, v_out):
        g = _dot_tn(sc_ref[...].astype(BF16), dm_ref[...].astype(BF16))
        _adam_update(g, w_ref[...], m_ref[...], v_ref[...], g_out, d_out, m_out, v_out)

    return pl.pallas_call(
        body, name="adam_w_ada", grid=(1,),
        out_shape=(jax.ShapeDtypeStruct((rows, cols), F32),) * 4,
        in_specs=[_full(sc_all16.shape), _full(dmod_blk16.shape)] + [_full((rows, cols))] * 3,
        out_specs=(_full((rows, cols)),) * 4,
        compiler_params=_params(("arbitrary",)),
    )(sc_all16, dmod_blk16, w, m, v)


def _pack_small(pieces):
    rows = []
    for name, n in PACK_ROWS:
        a = pieces[name].reshape(-1).astype(F32)
        a = jnp.pad(a, (0, n * 128 - a.shape[0]))
        rows.append(a.reshape(n, 128))
    return jnp.concatenate(rows, axis=0)


def _unpack_small(pack, name, shape):
    off, _ = PACK_OFF[name]
    size = 1
    for s in shape:
        size *= s
    n_rows = -(-size // 128)
    return pack[off:off + n_rows].reshape(-1)[:size].reshape(shape)


def _local_step(x2, tgt2, shift, scale, gate, norm_g, w_in_t, w_out_b, conv_w, conv_b, w_pool, ls_pool,
                mh_norm_g, b_gates, final_g, send_dw_out=None, send_dw_in=None):
    bg_pad = jnp.pad(b_gates, ((0, 0), (0, 128 - b_gates.shape[1])))
    conv_w8 = jnp.pad(conv_w, ((0, 8 - CONV_WIDTH), (0, 0)))
    fg = final_g.reshape(1, D_MODEL)

    proj, gates, h_b = _fwd_proj(x2, norm_g, scale, shift, w_in_t)
    mix, cst, nst, mst, conv_a, pooled = _mix_fwd(proj, gates, bg_pad, conv_w8, conv_b, w_pool, ls_pool, mh_norm_g)
    dx2, dmix, dwo, dgate, dfg, loss = _out_fwd_bwd(mix, x2, tgt2, w_out_b, gate, fg)
    if send_dw_out is not None:
        bg_pad = bg_pad + send_dw_out(dwo)
    dproj, dcw8, dcb, dwp, dls, dmhg, dbg = _mix_bwd(proj, gates, dmix, conv_a, pooled, cst, nst, mst, bg_pad,
                                                      conv_w8, w_pool, ls_pool, mh_norm_g)
    dw_in_t = _dw_in(h_b, dproj)[:N_IN]
    ng_in = norm_g
    if send_dw_in is not None:
        ng_in = norm_g + send_dw_in(dw_in_t, dcw8[:CONV_WIDTH])
    gx, dsh, dsc, dng = _bwd_in(dproj, w_in_t, x2, dx2, ng_in, scale)
    return dict(loss=loss, grad_x=gx, dw_in_t=dw_in_t, dw_out=dwo, dconv_w=dcw8[:CONV_WIDTH], conv_b=dcb,
                w_pool=dwp, ls_pool=dls, mh_norm_g=dmhg, b_gates=dbg, final_g=dfg, norm_g=dng,
                dmod=jnp.concatenate([dsh, dsc, dgate], axis=1))


def kernel(x, c, norm_g, w_ada, b_ada, w_in, b_gates, conv_w, conv_b, w_pool, ls_pool, mh_norm_g, w_out, final_g, loss_target, m_norm_g, m_w_ada, m_b_ada, m_w_in, m_b_gates, m_conv_w, m_conv_b, m_w_pool, m_ls_pool, m_mh_norm_g, m_w_out, m_final_g, v_norm_g, v_w_ada, v_b_ada, v_w_in, v_b_gates, v_conv_w, v_conv_b, v_w_pool, v_ls_pool, v_mh_norm_g, v_w_out, v_final_g):
    seq = x.shape[1]
    me = 4 * lax.axis_index("x") + 2 * lax.axis_index("y") + lax.axis_index("c")

    g_in, g_out, g_cw, g_c = _all_gather("gather_weights", w_in[0].astype(BF16).T, w_out[0].astype(BF16), conv_w[0], c)
    w_in_t = jnp.pad(g_in.reshape(N_IN, D_MODEL), ((0, N_PAD - N_IN), (0, 0)))
    w_out_b = g_out.reshape(D_MODEL, D_MODEL)
    conv_w_full = jnp.transpose(g_cw, (1, 0, 2)).reshape(CONV_WIDTH, 2 * D_MLSTM)
    c_all16 = jnp.pad(g_c.reshape(N_DEV, D_MODEL), ((0, 8), (0, 0)))

    b_ada_blk = lax.dynamic_slice(b_ada, (0, me * ADA_SHARD), (1, ADA_SHARD))
    mod_all, sc_all16 = _ada_mod(c_all16, w_ada[0], b_ada_blk)
    mod = lax.dynamic_index_in_dim(mod_all, me, axis=1, keepdims=False).reshape(1, 3 * D_MODEL)
    shift, scale, gate = mod[:, :D_MODEL], mod[:, D_MODEL:2 * D_MODEL], mod[:, 2 * D_MODEL:]

    flights = {}

    def send_dw_out(dwo):
        blocks = dwo.astype(BF16).reshape(N_DEV, D_MODEL // N_DEV, D_MODEL)
        flights["out"], token = _scatter_start("send_dw_out", (blocks,))
        return token

    def send_dw_in(dw_in_t, dcw):
        blocks = dw_in_t.reshape(N_DEV, N_SHARD, D_MODEL)
        dcw_blocks = jnp.transpose(dcw.reshape(CONV_WIDTH, N_DEV, 128), (1, 0, 2))
        flights["in"], token = _scatter_start("send_dw_in", (blocks, dcw_blocks))
        return token

    r = _local_step(x[0], loss_target[0], shift, scale, gate, norm_g, w_in_t, w_out_b, conv_w_full, conv_b,
                    w_pool[0], ls_pool, mh_norm_g, b_gates, final_g, send_dw_out, send_dw_in)

    pack = _pack_small(dict(loss=r["loss"][:, :1], final_g=r["final_g"], norm_g=r["norm_g"], conv_b=r["conv_b"],
                            ls_pool=r["ls_pool"], mh_norm_g=r["mh_norm_g"], b_gates=r["b_gates"][:, :8],
                            b_ada=r["dmod"], w_pool=r["w_pool"]))
    (p_pack,) = _all_gather("gather_small", pack)
    p_in, p_cw = _scatter_wait("recv_dw_in", flights["in"], p_pack)
    (p_out,) = _scatter_wait("recv_dw_out", flights["out"], p_cw)

    in_t = _adam_sum("adam_w_in", p_in, w_in[0].T, m_w_in[0].T, v_w_in[0].T, N_SHARD, 256)
    gi, di, mi, vi = (o.T for o in in_t)
    go, do_, mo, vo = _adam_sum("adam_w_out", p_out, w_out[0], m_w_out[0], v_w_out[0], 128)
    gc, dc, mc, vc = _adam_sum("adam_conv_w", p_cw, conv_w[0], m_conv_w[0], v_conv_w[0], CONV_WIDTH)

    def small(loss_like, fg_, ng_, cb_, ls_, mh_, bg_, ba_, wp_):
        return _pack_small(dict(loss=loss_like, final_g=fg_, norm_g=ng_, conv_b=cb_, ls_pool=ls_, mh_norm_g=mh_,
                                b_gates=bg_, b_ada=ba_, w_pool=wp_))

    zero = jnp.zeros((1, 1), F32)
    w_pack = small(zero, final_g, norm_g, conv_b, ls_pool, mh_norm_g, b_gates, b_ada, w_pool)
    m_pack = small(zero, m_final_g, m_norm_g, m_conv_b, m_ls_pool, m_mh_norm_g, m_b_gates, m_b_ada, m_w_pool)
    v_pack = small(zero, v_final_g, v_norm_g, v_conv_b, v_ls_pool, v_mh_norm_g, v_b_gates, v_b_ada, v_w_pool)
    gp, dp, mp, vp = _adam_sum("adam_small", p_pack, w_pack, m_pack, v_pack, PACK_TOTAL)

    off, rows = PACK_OFF["b_ada"]
    dmod_all = p_pack[:, off:off + rows, :].reshape(N_DEV, 3 * D_MODEL)
    dmod_blk16 = jnp.pad(lax.dynamic_slice(dmod_all, (0, me * ADA_SHARD), (N_DEV, ADA_SHARD)), ((0, 8), (0, 0)))
    ga, da, ma, va = _adam_ada(sc_all16, dmod_blk16, w_ada[0], m_w_ada[0], v_w_ada[0])

    names = ("norm_g", "w_ada", "b_ada", "w_in", "b_gates", "conv_w", "conv_b", "w_pool", "ls_pool", "mh_norm_g",
             "w_out", "final_g")
    shapes = dict(norm_g=norm_g.shape, b_ada=b_ada.shape, b_gates=b_gates.shape, conv_b=conv_b.shape,
                  w_pool=w_pool.shape, ls_pool=ls_pool.shape, mh_norm_g=mh_norm_g.shape, final_g=final_g.shape)
    sharded = dict(w_ada=(ga, da, ma, va), w_in=(gi, di, mi, vi), conv_w=(gc, dc, mc, vc), w_out=(go, do_, mo, vo))
    outs = []
    for kind in range(4):
        for nm in names:
            if nm in sharded:
                outs.append(sharded[nm][kind][None])
            else:
                outs.append(_unpack_small((gp, dp, mp, vp)[kind], nm, shapes[nm]))
    loss = gp[0, 0]
    grad_x = r["grad_x"].reshape(1, seq, D_MODEL)
    return (loss, grad_x, *outs)
```

```python
import jax
import jax.numpy as jnp
from jax import lax
from jax.experimental import pallas as pl
from jax.experimental.pallas import tpu as pltpu

F32 = jnp.float32
BF16 = jnp.bfloat16

D_MODEL = 1024
D_POOL = 512
D_MLSTM = 512
N_HEADS = 4
HEAD_DIM = 128
CHUNK = 128
POOL_WINDOWS = (2, 4, 8, 16)
POOL_GROUP_DIM = 128
CONV_WIDTH = 4
EPS = 1e-6
N_MAIN = 3584
N_IN = 3592
N_PAD = 3840
N_SHARD = N_IN // 8
ADA_SHARD = 3 * D_MODEL // 8
N_DEV = 8
CONV_HALO = 8
POOL_HALO = 16
NEG_BIG = -1e30
VMEM_LIMIT_BYTES = 56 * 1024 * 1024

ADAM_LR = 0.001
ADAM_B1 = 0.9
ADAM_B2 = 0.999
ADAM_EPS = 1e-08
ADAM_WD = 0.01
ADAM_STEP = 10

PACK_ROWS = (("loss", 8), ("final_g", 8), ("norm_g", 8), ("conv_b", 8), ("ls_pool", 8),
             ("mh_norm_g", 8), ("b_gates", 8), ("b_ada", 24), ("w_pool", 512))
PACK_TOTAL = sum(r for _, r in PACK_ROWS)


def _pack_offsets():
    off, out = 0, {}
    for name, rows in PACK_ROWS:
        out[name] = (off, rows)
        off += rows
    return out


PACK_OFF = _pack_offsets()


def _dot(a, b):
    return jnp.dot(a, b, preferred_element_type=F32)


def _dot_nt(a, b):
    return lax.dot_general(a, b, (((1,), (1,)), ((), ())), preferred_element_type=F32)


def _dot_tn(a, b):
    return lax.dot_general(a, b, (((0,), (0,)), ((), ())), preferred_element_type=F32)


def _dot_f32(a, b):
    return jnp.dot(a, b, precision=lax.Precision.HIGHEST, preferred_element_type=F32)


def _sigmoid(x):
    return jax.nn.sigmoid(x)


def _log_sigmoid(x):
    return jnp.minimum(x, 0.0) - jnp.log1p(jnp.exp(-jnp.abs(x)))


def _params(sem):
    return pltpu.CompilerParams(dimension_semantics=sem, vmem_limit_bytes=VMEM_LIMIT_BYTES)


def _full(shape):
    n = len(shape)
    return pl.BlockSpec(shape, lambda *_: (0,) * n)


def _mesh_pos():
    return lax.axis_index("x"), lax.axis_index("y"), lax.axis_index("c")


def _peer(k):
    x, y, c = _mesh_pos()
    px = 1 - x if (k >> 2) & 1 else x
    py = 1 - y if (k >> 1) & 1 else y
    pc = 1 - c if k & 1 else c
    return (px, py, pc), 4 * px + 2 * py + pc


def _remote(src, dst, send_sem, recv_sem, to):
    return pltpu.make_async_remote_copy(src_ref=src, dst_ref=dst, send_sem=send_sem, recv_sem=recv_sem, device_id=to,
                                        device_id_type=pl.DeviceIdType.MESH)


def _other_chips():
    x, y, _ = _mesh_pos()
    return [(1 - x, y), (x, 1 - y), (1 - x, 1 - y)]


def _two_level_gather(src, dst, send_sems, recv_sems, local_sems):
    n = len(src)
    x, y, c = _mesh_pos()
    me = 4 * x + 2 * y + c
    sibling = (x, y, 1 - c)
    chips = _other_chips()

    def copy(a, k, block, to, own):
        return _remote(src[a] if own else dst[a].at[block], dst[a].at[block], send_sems.at[a, k], recv_sems.at[a, k], to)

    local = [pltpu.make_async_copy(src[a], dst[a].at[me], local_sems.at[a]) for a in range(n)]
    first = [copy(a, 0, me, sibling, True) for a in range(n)]
    first += [copy(a, 1 + j, me, (*chip, c), True) for j, chip in enumerate(chips) for a in range(n)]
    for cp in local + first:
        cp.start()
    passed = []
    for j, (px, py) in enumerate(chips):
        block = 4 * px + 2 * py + c
        for a in range(n):
            copy(a, 1 + j, block, sibling, False).wait_recv()
            passed.append(copy(a, 4 + j, block, sibling, False))
            passed[-1].start()
    for a in range(n):
        copy(a, 0, 4 * x + 2 * y + (1 - c), sibling, False).wait_recv()
    for j, (px, py) in enumerate(chips):
        for a in range(n):
            copy(a, 4 + j, 4 * px + 2 * py + (1 - c), sibling, False).wait_recv()
    for cp in first + passed:
        cp.wait_send()
    for cp in local:
        cp.wait()


GATHER_COPIES = 7


def _all_gather(name, *shards):
    n = len(shards)

    def body(*refs):
        _two_level_gather(refs[:n], refs[n:2 * n], *refs[2 * n:])

    hbm = pl.BlockSpec(memory_space=pltpu.HBM)
    return pl.pallas_call(
        body, name=name,
        out_shape=tuple(jax.ShapeDtypeStruct((N_DEV,) + s.shape, s.dtype) for s in shards),
        in_specs=[hbm] * n, out_specs=tuple([hbm] * n),
        scratch_shapes=[pltpu.SemaphoreType.DMA((n, GATHER_COPIES)), pltpu.SemaphoreType.DMA((n, GATHER_COPIES)),
                        pltpu.SemaphoreType.DMA((n,))],
    )(*shards)


def _ada_mod(c_all16, w_ada_blk, b_ada_blk):
    def body(c_ref, w_ref, b_ref, out_ref, sc_ref, send_sems, recv_sems):
        x, y, c = _mesh_pos()
        me = 4 * x + 2 * y + c
        cv = c_ref[...]
        sc = cv * _sigmoid(cv)
        sc_ref[...] = sc
        blk = _dot(sc.astype(BF16), w_ref[...].astype(BF16)) + b_ref[...]
        out_ref[me] = blk[0:N_DEV, :]
        copies = []
        for k in range(1, N_DEV):
            peer, _ = _peer(k)
            copies.append(pltpu.make_async_remote_copy(
                src_ref=out_ref.at[me], dst_ref=out_ref.at[me], send_sem=send_sems.at[k - 1],
                recv_sem=recv_sems.at[k - 1], device_id=peer, device_id_type=pl.DeviceIdType.MESH))
        for cp in copies:
            cp.start()
        for cp in copies:
            cp.wait()

    vmem = pl.BlockSpec(memory_space=pltpu.VMEM)
    return pl.pallas_call(
        body, name="ada_mod",
        out_shape=(jax.ShapeDtypeStruct((N_DEV, N_DEV, ADA_SHARD), F32),
                   jax.ShapeDtypeStruct(c_all16.shape, F32)),
        in_specs=[vmem] * 3, out_specs=(vmem, vmem),
        scratch_shapes=[pltpu.SemaphoreType.DMA((N_DEV - 1,)), pltpu.SemaphoreType.DMA((N_DEV - 1,))],
    )(c_all16, w_ada_blk, b_ada_blk)


def _scatter_copies(src, land, send_sems, recv_sems):
    x, y, c = _mesh_pos()
    me = 4 * x + 2 * y + c
    copies = []
    for k in range(1, N_DEV):
        peer, p = _peer(k)
        for a in range(len(src)):
            i = a * (N_DEV - 1) + k - 1
            copies.append(_remote(src[a].at[p], land[a].at[me], send_sems.at[i], recv_sems.at[i], peer))
    return copies


def _scatter_start(name, blocks):
    n = len(blocks)

    def body(*refs):
        src, land = refs[:n], refs[n:2 * n]
        send_sems, recv_sems = refs[2 * n], refs[2 * n + 1]
        token_ref = refs[-1]
        for cp in _scatter_copies(src, land, send_sems, recv_sems):
            cp.start()
        token_ref[...] = jnp.zeros_like(token_ref)

    hbm = pl.BlockSpec(memory_space=pltpu.HBM)
    sem = pl.BlockSpec(memory_space=pltpu.SEMAPHORE)
    through = tuple(pltpu.HBM(b.shape, b.dtype) for b in blocks)
    args = [pltpu.with_memory_space_constraint(b, pltpu.HBM) for b in blocks]
    args += [pltpu.with_memory_space_constraint(lax.empty(b.shape, b.dtype), pltpu.HBM) for b in blocks]
    out = pl.pallas_call(
        body, name=name,
        out_shape=(pltpu.SemaphoreType.DMA((n * (N_DEV - 1),)),) * 2 + through + through
        + (jax.ShapeDtypeStruct((8, 128), F32),),
        in_specs=[hbm] * (2 * n),
        out_specs=(sem, sem) + (hbm,) * (2 * n) + (pl.BlockSpec(memory_space=pltpu.VMEM),),
        input_output_aliases={i: 2 + i for i in range(2 * n)},
        compiler_params=pltpu.CompilerParams(has_side_effects=pltpu.SideEffectType.DATAFLOW_SIDE_EFFECTING),
    )(*args)
    return out[:-1], out[-1][0:1, 0:1]


def _scatter_wait(name, state, after):
    n = (len(state) - 2) // 2
    send_sems, recv_sems = state[0], state[1]
    src, land = state[2:2 + n], state[2 + n:]

    def body(*refs):
        src_r, land_r = refs[:n], refs[n:2 * n]
        for cp in _scatter_copies(src_r, land_r, refs[2 * n], refs[2 * n + 1]):
            cp.wait_send()
            cp.wait_recv()

    hbm = pl.BlockSpec(memory_space=pltpu.HBM)
    sem = pl.BlockSpec(memory_space=pltpu.SEMAPHORE)
    out = pl.pallas_call(
        body, name=name,
        out_shape=tuple(pltpu.HBM(b.shape, b.dtype) for b in src + land),
        in_specs=[hbm] * (2 * n) + [sem, sem, pl.BlockSpec(memory_space=pl.ANY)],
        out_specs=(hbm,) * (2 * n),
        input_output_aliases={i: i for i in range(2 * n)},
        compiler_params=pltpu.CompilerParams(has_side_effects=pltpu.SideEffectType.DATAFLOW_SIDE_EFFECTING),
    )(*src, *land, send_sems, recv_sems, after)
    me = 4 * lax.axis_index("x") + 2 * lax.axis_index("y") + lax.axis_index("c")
    landed = []
    for a in range(n):
        own = lax.dynamic_index_in_dim(out[a], me, axis=0, keepdims=True)
        landed.append(lax.dynamic_update_slice_in_dim(out[n + a], own, me, axis=0))
    return landed


def _fwd_proj(x, norm_g, scale, shift, w_in_t):
    seq = x.shape[0]
    tm = min(512, seq)
    sub = min(256, tm)
    tn = 512

    def body(x_ref, ng_ref, sc_ref, sh_ref, wt_ref, proj_ref, gates_ref, h_ref):
        def chain(n):
            for _ in range(n):
                yield
            rows = slice(n * sub, (n + 1) * sub)
            xt = x_ref[rows, :]
            r = lax.rsqrt(jnp.mean(xt * xt, axis=-1, keepdims=True) + EPS)
            h = ((xt * r) * ng_ref[...]) * (1.0 + sc_ref[...]) + sh_ref[...]
            hb = h.astype(BF16)
            h_ref[rows, :] = hb
            yield
            gates_ref[rows, :] = _dot_nt(hb, wt_ref[N_MAIN:N_MAIN + 128, :])
            for j in range(N_MAIN // tn):
                proj_ref[rows, j * tn:(j + 1) * tn] = _dot_nt(hb, wt_ref[j * tn:(j + 1) * tn, :])

        _in_lockstep(chain(n) for n in range(tm // sub))

    vec = _full((1, D_MODEL))
    tile = pl.BlockSpec((tm, D_MODEL), lambda i: (i, 0))
    return pl.pallas_call(
        body, name="fwd_proj", grid=(seq // tm,),
        out_shape=(jax.ShapeDtypeStruct((seq, N_MAIN), F32), jax.ShapeDtypeStruct((seq, 128), F32),
                   jax.ShapeDtypeStruct((seq, D_MODEL), BF16)),
        in_specs=[tile, vec, vec, vec, _full((N_PAD, D_MODEL))],
        out_specs=(pl.BlockSpec((tm, N_MAIN), lambda i: (i, 0)), pl.BlockSpec((tm, 128), lambda i: (i, 0)), tile),
        compiler_params=_params(("arbitrary",)),
    )(x, norm_g, scale, shift, w_in_t)


def _gate_forms(gpre):
    r = lax.broadcasted_iota(jnp.int32, (CHUNK, CHUNK), 0)
    c = lax.broadcasted_iota(jnp.int32, (CHUNK, CHUNK), 1)
    causal = c <= r
    ltri = jnp.where(causal, 1.0, 0.0).astype(F32)
    utri = jnp.where(r <= c, 1.0, 0.0).astype(F32)
    bcol = _dot_f32(ltri, _log_sigmoid(gpre))
    gt8 = gpre.T[0:8, :]
    brow = _dot_f32(_log_sigmoid(gt8), utri)
    return causal, utri, bcol, gt8, brow


def _in_lockstep(stages):
    alive = list(stages)
    while alive:
        still = []
        for g in alive:
            try:
                next(g)
                still.append(g)
            except StopIteration:
                pass
        alive = still


def _head_fwd(qh, kh, vh, bc, br, igr, m_prev, c_h, n_row, causal):
    qb, kb, vb, cb = qh.astype(BF16), kh.astype(BF16), vh.astype(BF16), c_h.astype(BF16)
    qk = _dot_nt(qb, kb)
    cq = _dot_nt(qb, cb)
    nq = _dot_nt(qb, jnp.broadcast_to(n_row.astype(BF16), (HEAD_DIM, HEAD_DIM)))
    yield
    dlog = jnp.where(causal, bc - br + igr, NEG_BIG)
    inter_log = bc + m_prev
    m_t = jnp.maximum(inter_log, jnp.max(dlog, axis=-1, keepdims=True))
    yield
    dmat = jnp.exp(dlog - m_t)
    inter = jnp.exp(inter_log - m_t)
    s = qk * dmat
    sv = _dot(s.astype(BF16), vb)
    yield
    den = jnp.sum(s, axis=-1, keepdims=True) + inter * nq
    emt = jnp.exp(-m_t)
    yield
    num = sv + inter * cq
    dn = jnp.maximum(jnp.abs(den), emt)
    hm = num / dn
    return dict(dmat=dmat, inter=inter, qb=qb, kb=kb, vb=vb, cb=cb, s=s, cq=cq, nq=nq, den=den, emt=emt,
                dn=dn, hm=hm)


def _state_weights(bc, igc, m_prev, m_new=None):
    last = lax.broadcasted_iota(jnp.int32, (CHUNK, 1), 0) == CHUNK - 1
    b_last = jnp.sum(jnp.where(last, bc, 0.0), axis=0, keepdims=True)
    wlog = b_last - bc + igc
    if m_new is None:
        m_new = jnp.maximum(b_last + m_prev, jnp.max(wlog, axis=0, keepdims=True))
    w_c = jnp.exp(wlog - m_new)
    decay = jnp.exp(b_last + m_prev - m_new)
    return w_c, decay, m_new, last


def _rows_back(x, k):
    return x if k == 0 else pltpu.roll(x, k, 0)


def _rows_ahead(x, k):
    return x if k == 0 else pltpu.roll(x, x.shape[0] - k, 0)


def _conv_taps(xpad):
    return [_rows_back(xpad, CONV_WIDTH - 1 - j)[CONV_HALO:, :] for j in range(CONV_WIDTH)]


def _conv_pre(taps, cw_ref, cb_ref):
    a = cb_ref[...]
    for j in range(CONV_WIDTH):
        a = a + cw_ref[j:j + 1, :] * taps[j]
    return a


def _window_sum(x, w, shift):
    k = 1
    while k < w:
        x = x + shift(x, k)
        k *= 2
    return x


def _pool_window_sum(upad_ref, g, w):
    lanes = slice(g * POOL_GROUP_DIM, (g + 1) * POOL_GROUP_DIM)
    return _window_sum(upad_ref[:, lanes], w, _rows_back)[POOL_HALO:, :]


def _pool_inv_count(row0, rows, w):
    pos = row0 + lax.broadcasted_iota(jnp.int32, (rows, 1), 0) + 1
    return 1.0 / jnp.minimum(pos, w).astype(F32)


FWD_CHUNKS = 2
BWD_CHUNKS = 2


def _mix_fwd(proj, gates, bg_pad, conv_w8, conv_b, w_pool, ls_pool, mh_g):
    seq = proj.shape[0]
    n_chunks = seq // CHUNK
    per_step = FWD_CHUNKS
    blk = per_step * CHUNK

    def body(uz_ref, qk_ref, v_ref, o_ref, zm_ref, uh_ref, qkh_ref, g_ref, bg_ref, cw_ref, cb_ref, wp_ref,
             ls_ref, mhg_ref, mix_ref, cst_ref, nst_ref, mst_ref, a_ref, pooled_ref, c_scr, n_scr, m_scr, xpad, upad):
        i = pl.program_id(0)

        @pl.when(i == 0)
        def _():
            c_scr[...] = jnp.zeros_like(c_scr)
            n_scr[...] = jnp.zeros_like(n_scr)
            m_scr[...] = jnp.zeros_like(m_scr)

        first = i == 0

        upad[0:POOL_HALO, :] = jnp.where(first, 0.0, uh_ref[...])
        upad[POOL_HALO:POOL_HALO + blk, :] = uz_ref[:, 0:D_POOL]
        for g, w in enumerate(POOL_WINDOWS):
            lanes = slice(g * POOL_GROUP_DIM, (g + 1) * POOL_GROUP_DIM)
            pooled = (_pool_window_sum(upad, g, w) * _pool_inv_count(i * blk, blk, w) - uz_ref[:, lanes]).astype(BF16)
            pooled_ref[:, lanes] = pooled
            y = _dot(pooled, wp_ref[g].astype(BF16)) * ls_ref[:, lanes]
            zp = uz_ref[:, D_POOL + g * POOL_GROUP_DIM:D_POOL + (g + 1) * POOL_GROUP_DIM]
            mix_ref[:, lanes] = (y * (zp * _sigmoid(zp))).astype(BF16)

        xpad[0:CONV_HALO, :] = jnp.where(first, 0.0, qkh_ref[...])
        xpad[CONV_HALO:CONV_HALO + blk, :] = qk_ref[...]
        a = _conv_pre(_conv_taps(xpad[...]), cw_ref, cb_ref)
        a_ref[...] = a
        qk = a * _sigmoid(a)

        def head(rows, h, qh, kh, vh, bc, br, igr, m_prev, c_h, n_row, causal):
            lanes = slice(h * HEAD_DIM, (h + 1) * HEAD_DIM)
            f = yield from _head_fwd(qh, kh, vh, bc, br, igr, m_prev, c_h, n_row, causal)
            yield
            hm = f["hm"]
            hn = hm * lax.rsqrt(jnp.mean(hm * hm, axis=-1, keepdims=True) + EPS) * mhg_ref[:, lanes]
            zm = zm_ref[rows, lanes]
            out = hn * _sigmoid(o_ref[rows, lanes]) * (zm * _sigmoid(zm))
            mix_ref[rows, D_POOL + h * HEAD_DIM:D_POOL + (h + 1) * HEAD_DIM] = out.astype(BF16)

        c_cur = [c_scr[h] for h in range(N_HEADS)]
        n_cur = [n_scr[h:h + 1, :] for h in range(N_HEADS)]
        m_cur = [m_scr[h:h + 1, 0:1] for h in range(N_HEADS)]
        chains = []
        for s in range(per_step):
            rows = slice(s * CHUNK, (s + 1) * CHUNK)
            gpre = g_ref[rows, :] + bg_ref[...]
            causal, _, bcol, gt8, brow = _gate_forms(gpre)
            nst_ref[s] = jnp.zeros((8, 128), F32)
            mst_ref[s] = jnp.zeros((8, 128), F32)
            for h in range(N_HEADS):
                lanes = slice(h * HEAD_DIM, (h + 1) * HEAD_DIM)
                cst_ref[s, h] = c_cur[h]
                nst_ref[s, h:h + 1, :] = n_cur[h]
                mst_ref[s, h:h + 1, :] = jnp.broadcast_to(m_cur[h], (1, 128))
                qh = qk[rows, lanes]
                kh = qk[rows, D_MLSTM + h * HEAD_DIM:D_MLSTM + (h + 1) * HEAD_DIM] * (HEAD_DIM ** -0.5)
                vh = v_ref[rows, lanes]
                bc = bcol[:, N_HEADS + h:N_HEADS + h + 1]
                br = brow[N_HEADS + h:N_HEADS + h + 1, :]
                igr = gt8[h:h + 1, :]
                igc = gpre[:, h:h + 1]
                chains.append(head(rows, h, qh, kh, vh, bc, br, igr, m_cur[h], c_cur[h], n_cur[h], causal))
                w_c, decay, m_new, _ = _state_weights(bc, igc, m_cur[h])
                c_cur[h] = decay * c_cur[h] + _dot_tn((vh * w_c).astype(BF16), kh.astype(BF16))
                n_cur[h] = decay * n_cur[h] + jnp.sum(w_c * kh, axis=0, keepdims=True)
                m_cur[h] = m_new
        for h in range(N_HEADS):
            c_scr[h] = c_cur[h]
            n_scr[h:h + 1, :] = n_cur[h]
            m_scr[h:h + 1, :] = jnp.broadcast_to(m_cur[h], (1, 128))
        _in_lockstep(chains)

    in_specs = [
        pl.BlockSpec((blk, 1024), lambda i: (i, 0)),
        pl.BlockSpec((blk, 1024), lambda i: (i, 1)),
        pl.BlockSpec((blk, 512), lambda i: (i, 4)),
        pl.BlockSpec((blk, 512), lambda i: (i, 5)),
        pl.BlockSpec((blk, 512), lambda i: (i, 6)),
        pl.BlockSpec((POOL_HALO, 512), lambda i: (jnp.maximum(i * (blk // POOL_HALO) - 1, 0), 0)),
        pl.BlockSpec((CONV_HALO, 1024), lambda i: (jnp.maximum(i * (blk // CONV_HALO) - 1, 0), 1)),
        pl.BlockSpec((blk, 128), lambda i: (i, 0)),
        _full((1, 128)), _full((8, 1024)), _full((1, 1024)), _full((4, 128, 128)), _full((1, 512)),
        _full((1, 512))]
    return pl.pallas_call(
        body, name="mix_fwd", grid=(n_chunks // per_step,),
        out_shape=(jax.ShapeDtypeStruct((seq, D_MODEL), BF16),
                   jax.ShapeDtypeStruct((n_chunks, N_HEADS, HEAD_DIM, HEAD_DIM), F32),
                   jax.ShapeDtypeStruct((n_chunks, 8, 128), F32),
                   jax.ShapeDtypeStruct((n_chunks, 8, 128), F32),
                   jax.ShapeDtypeStruct((seq, 2 * D_MLSTM), F32),
                   jax.ShapeDtypeStruct((seq, D_POOL), BF16)),
        in_specs=in_specs,
        out_specs=(pl.BlockSpec((blk, D_MODEL), lambda i: (i, 0)),
                   pl.BlockSpec((per_step, N_HEADS, HEAD_DIM, HEAD_DIM), lambda i: (i, 0, 0, 0)),
                   pl.BlockSpec((per_step, 8, 128), lambda i: (i, 0, 0)),
                   pl.BlockSpec((per_step, 8, 128), lambda i: (i, 0, 0)),
                   pl.BlockSpec((blk, 2 * D_MLSTM), lambda i: (i, 0)),
                   pl.BlockSpec((blk, D_POOL), lambda i: (i, 0))),
        scratch_shapes=[pltpu.VMEM((N_HEADS, HEAD_DIM, HEAD_DIM), F32), pltpu.VMEM((8, 128), F32),
                        pltpu.VMEM((8, 128), F32), pltpu.VMEM((CONV_HALO + blk, 1024), F32),
                        pltpu.VMEM((POOL_HALO + blk, D_POOL), F32)],
        compiler_params=_params(("arbitrary",)),
    )(proj, proj, proj, proj, proj, proj, proj, gates, bg_pad, conv_w8, conv_b, w_pool, ls_pool, mh_g)


def _out_fwd_bwd(mix, x, tgt, w_out_b, gate, final_g):
    seq = x.shape[0]
    tm = min(512, seq)
    sub = min(256, tm)

    def body(mix_ref, x_ref, t_ref, w_ref, gate_ref, fg_ref, dx2_ref, dmix_ref, dwo_ref, dgate_ref, dfg_ref,
             loss_ref, dwo_acc):
        @pl.when(pl.program_id(0) == 0)
        def _():
            dwo_acc[...] = jnp.zeros_like(dwo_acc)
            dgate_ref[...] = jnp.zeros_like(dgate_ref)
            dfg_ref[...] = jnp.zeros_like(dfg_ref)
            loss_ref[...] = jnp.zeros_like(loss_ref)

        w = w_ref[...]
        gate_v = gate_ref[...]
        fg = fg_ref[...]
        do2_parts = [None] * (tm // sub)

        def chain(n):
            rows = slice(n * sub, (n + 1) * sub)
            o2 = _dot(mix_ref[rows, :], w)
            yield
            x2 = x_ref[rows, :] + gate_v * o2
            r2 = lax.rsqrt(jnp.mean(x2 * x2, axis=-1, keepdims=True) + EPS)
            x2n = x2 * r2
            err = x2n * fg - t_ref[rows, :]
            part = 0.5 * jnp.sum(jnp.sum(err * err, axis=-1, keepdims=True), axis=0, keepdims=True) / D_MODEL
            loss_ref[...] += jnp.broadcast_to(part, loss_ref.shape)
            dy = err / D_MODEL
            dfg_ref[...] += jnp.sum(dy * x2n, axis=0, keepdims=True)
            gdy = dy * fg
            dx2 = r2 * (gdy - x2n * jnp.mean(gdy * x2n, axis=-1, keepdims=True))
            dx2_ref[rows, :] = dx2
            dgate_ref[...] += jnp.sum(dx2 * o2, axis=0, keepdims=True)
            do2 = (dx2 * gate_v).astype(BF16)
            dmix_ref[rows, :] = _dot_nt(do2, w)
            do2_parts[n] = do2

        _in_lockstep(chain(n) for n in range(tm // sub))
        dwo_acc[...] += _dot_tn(mix_ref[...], jnp.concatenate(do2_parts, axis=0))

        @pl.when(pl.program_id(0) == seq // tm - 1)
        def _():
            dwo_ref[...] = dwo_acc[...].astype(BF16)

    tile = pl.BlockSpec((tm, D_MODEL), lambda i: (i, 0))
    vec = _full((1, D_MODEL))
    return pl.pallas_call(
        body, name="out_fwd_bwd", grid=(seq // tm,),
        out_shape=(jax.ShapeDtypeStruct((seq, D_MODEL), F32), jax.ShapeDtypeStruct((seq, D_MODEL), F32),
                   jax.ShapeDtypeStruct((D_MODEL, D_MODEL), BF16), jax.ShapeDtypeStruct((1, D_MODEL), F32),
                   jax.ShapeDtypeStruct((1, D_MODEL), F32), jax.ShapeDtypeStruct((1, 128), F32)),
        in_specs=[tile, tile, tile, _full((D_MODEL, D_MODEL)), vec, vec],
        out_specs=(tile, tile, _full((D_MODEL, D_MODEL)), vec, vec, _full((1, 128))),
        scratch_shapes=[pltpu.VMEM((D_MODEL, D_MODEL), F32)],
        compiler_params=_params(("arbitrary",)),
    )(mix, x, tgt, w_out_b, gate, final_g)


def _mix_bwd(proj, gates, dmix, conv_a, pooled, cst, nst, mst, bg_pad, conv_w8, w_pool, ls_pool, mh_g):
    seq = proj.shape[0]
    n_chunks = seq // CHUNK
    per_step = BWD_CHUNKS
    blk = per_step * CHUNK
    n_blocks = n_chunks // per_step

    def body(zp_ref, qk_ref, v_ref, o_ref, zm_ref, g_ref, dmix_ref, a_ref, pooled_ref, cst_ref, nst_ref, mst_ref,
             mnx_ref, bg_ref, cw_ref, wp_ref, ls_ref, mhg_ref,
             dp_ref, dcw_ref, dcb_ref, dwp_ref, dls_ref, dmhg_ref, dbg_ref,
             dc_scr, dn_scr, dapad, dpipad):
        i = pl.program_id(0)
        bi = n_blocks - 1 - i

        @pl.when(i == 0)
        def _():
            for ref in (dc_scr, dn_scr, dcw_ref, dcb_ref, dwp_ref, dls_ref, dmhg_ref, dbg_ref):
                ref[...] = jnp.zeros_like(ref)
            dapad[blk:blk + CONV_HALO, :] = jnp.zeros((CONV_HALO, 1024), F32)
            dpipad[blk:blk + POOL_HALO, :] = jnp.zeros((POOL_HALO, D_POOL), F32)

        dpooled = []
        for g, w in enumerate(POOL_WINDOWS):
            lanes = slice(g * POOL_GROUP_DIM, (g + 1) * POOL_GROUP_DIM)
            zlanes = slice(D_POOL + g * POOL_GROUP_DIM, D_POOL + (g + 1) * POOL_GROUP_DIM)
            inv = _pool_inv_count(bi * blk, blk, w)
            pb = pooled_ref[:, lanes]
            wpb = wp_ref[g].astype(BF16)
            yw = _dot(pb, wpb)
            ls = ls_ref[:, lanes]
            zp = zp_ref[:, lanes]
            sg = _sigmoid(zp)
            dpo = dmix_ref[:, lanes]
            dp_ref[:, zlanes] = (dpo * (yw * ls) * (sg * (1.0 + zp * (1.0 - sg)))).astype(BF16)
            dy = dpo * (zp * sg)
            dls_ref[:, lanes] += jnp.sum(dy * yw, axis=0, keepdims=True)
            dyw = (dy * ls).astype(BF16)
            dwp_ref[g] += _dot_tn(pb, dyw)
            dpl = _dot_nt(dyw, wpb)
            dpooled.append(dpl)
            dpipad[0:blk, lanes] = dpl * inv
        for g, w in enumerate(POOL_WINDOWS):
            lanes = slice(g * POOL_GROUP_DIM, (g + 1) * POOL_GROUP_DIM)
            du = _window_sum(dpipad[:, lanes], w, _rows_ahead)[0:blk, :] - dpooled[g]
            dp_ref[:, lanes] = du.astype(BF16)
        dpipad[blk:blk + POOL_HALO, :] = dpipad[0:POOL_HALO, :]

        a = a_ref[...]
        sga = _sigmoid(a)
        qk = a * sga
        dsilu_a = sga * (1.0 + a * (1.0 - sga))

        lane = lax.broadcasted_iota(jnp.int32, (CHUNK, 128), 1)
        row = lax.broadcasted_iota(jnp.int32, (CHUNK, 128), 0)
        scale_k = HEAD_DIM ** -0.5
        forms = [None] * per_step
        col_g_rows = [[] for _ in range(per_step)]
        dig_parts = [[] for _ in range(per_step)]
        db_parts = [[] for _ in range(per_step)]
        d_state = [[None] * N_HEADS for _ in range(per_step)]

        def state_terms(s, h, c_h, n_row, vb, kb):
            dcn, dnn = d_state[s][h]
            dcnb = dcn.astype(BF16)
            amat = _dot(vb, dcnb) + dnn
            kdc = _dot_nt(kb, dcnb)
            ddecay = (jnp.sum(jnp.sum(dcn * c_h, axis=-1, keepdims=True), axis=0, keepdims=True)
                      + jnp.sum(dnn * n_row, axis=-1, keepdims=True))
            return dcn, dnn, amat, kdc, ddecay

        def head(s, h):
            rows = slice(s * CHUNK, (s + 1) * CHUNK)
            lanes = slice(h * HEAD_DIM, (h + 1) * HEAD_DIM)
            klanes = slice(D_MLSTM + h * HEAD_DIM, D_MLSTM + (h + 1) * HEAD_DIM)
            gpre, causal, utri, bcol, gt8, brow = forms[s]
            qh = qk[rows, lanes]
            kh = qk[rows, klanes] * scale_k
            vh = v_ref[rows, lanes]
            bc = bcol[:, N_HEADS + h:N_HEADS + h + 1]
            br = brow[N_HEADS + h:N_HEADS + h + 1, :]
            igr = gt8[h:h + 1, :]
            igc = gpre[:, h:h + 1]
            m_prev = mst_ref[s, h:h + 1, 0:1]
            m_next = mnx_ref[0, h:h + 1, 0:1] if s == per_step - 1 else mst_ref[s + 1, h:h + 1, 0:1]
            c_h = cst_ref[s, h]
            n_row = nst_ref[s, h:h + 1, :]
            w_c, decay, _, last = _state_weights(bc, igc, m_prev, m_next)
            terms = None
            if s == per_step - 1:
                terms = state_terms(s, h, c_h, n_row, vh.astype(BF16), kh.astype(BF16))
            f = yield from _head_fwd(qh, kh, vh, bc, br, igr, m_prev, c_h, n_row, causal)
            qb, kb, vb, cb = f["qb"], f["kb"], f["vb"], f["cb"]
            sm, dmat, inter, den, dn, hm = f["s"], f["dmat"], f["inter"], f["den"], f["dn"], f["hm"]
            yield

            rinv = lax.rsqrt(jnp.mean(hm * hm, axis=-1, keepdims=True) + EPS)
            hmn = hm * rinv
            gh = mhg_ref[:, lanes]
            o_pre = o_ref[rows, lanes]
            og = _sigmoid(o_pre)
            zm = zm_ref[rows, lanes]
            sgz = _sigmoid(zm)
            sz = zm * sgz
            dout = dmix_ref[rows, D_POOL + h * HEAD_DIM:D_POOL + (h + 1) * HEAD_DIM]
            hn = hmn * gh
            dp_ref[rows, 2560 + h * HEAD_DIM:2560 + (h + 1) * HEAD_DIM] = (
                dout * hn * sz * og * (1.0 - og)).astype(BF16)
            dp_ref[rows, 3072 + h * HEAD_DIM:3072 + (h + 1) * HEAD_DIM] = (
                dout * hn * og * (sgz * (1.0 + zm * (1.0 - sgz)))).astype(BF16)
            dhn = dout * og * sz
            dmhg_ref[:, lanes] += jnp.sum(dhn * hmn, axis=0, keepdims=True)
            dyn = dhn * gh
            dhm = rinv * (dyn - hmn * jnp.mean(dyn * hmn, axis=-1, keepdims=True))
            yield

            inv_dn = 1.0 / dn
            dnum = dhm * inv_dn
            hd = jnp.sum(dhm * hm, axis=-1, keepdims=True)
            dden = jnp.where(jnp.abs(den) > f["emt"], -hd * inv_dn * jnp.sign(den), 0.0)
            dnb = dnum.astype(BF16)
            dnv = _dot_nt(dnb, vb)
            dv = _dot_tn(sm.astype(BF16), dnb)
            dnc = _dot(dnb, cb)
            dc_prev = _dot_tn((inter * dnum).astype(BF16), qb)
            dn_prev = jnp.sum((inter * dden) * qh, axis=0, keepdims=True)
            yield
            ds = dnv + dden
            dqk = (ds * dmat).astype(BF16)
            dqk_k = _dot(dqk, kb)
            dk = _dot_tn(dqk, qb)
            for _ in range(per_step - 1 - s):
                yield
            if terms is None:
                terms = state_terms(s, h, c_h, n_row, vb, kb)
            dcn, dnn, amat, kdc, ddecay = terms
            d_start = (decay * dcn + dc_prev, decay * dnn + dn_prev)
            if s > 0:
                d_state[s - 1][h] = d_start
            else:
                dc_scr[h] = d_start[0]
                dn_scr[h:h + 1, :] = d_start[1]
            yield
            gmat = ds * sm
            row_g = jnp.sum(gmat, axis=-1, keepdims=True)
            col_g_rows[s].append(jnp.where(row == h, jnp.sum(gmat, axis=0, keepdims=True), 0.0))
            gcol = inter * (jnp.sum(dnum * f["cq"], axis=-1, keepdims=True) + dden * f["nq"])
            dw = jnp.sum(amat * kh, axis=-1, keepdims=True)
            e = dw * w_c
            db_last = ddecay * decay + jnp.sum(e, axis=0, keepdims=True)
            dig_parts[s].append(jnp.where(lane == h, e, 0.0))
            db_parts[s].append(
                jnp.where(lane == N_HEADS + h, row_g + gcol - e + jnp.where(last, db_last, 0.0), 0.0))
            yield
            dq = dqk_k + inter * (dnc + dden * n_row)
            dp_ref[rows, 2048 + h * HEAD_DIM:2048 + (h + 1) * HEAD_DIM] = (dv + w_c * kdc).astype(BF16)
            dapad[rows, lanes] = dq * dsilu_a[rows, lanes]
            dapad[rows, klanes] = (dk + w_c * amat) * scale_k * dsilu_a[rows, klanes]

        chains = []
        for s in reversed(range(per_step)):
            gpre = g_ref[s * CHUNK:(s + 1) * CHUNK, :] + bg_ref[...]
            forms[s] = (gpre,) + _gate_forms(gpre)
            for h in range(N_HEADS):
                if s == per_step - 1:
                    d_state[s][h] = (dc_scr[h], dn_scr[h:h + 1, :])
                chains.append(head(s, h))
        _in_lockstep(chains)

        for s in range(per_step):
            rows = slice(s * CHUNK, (s + 1) * CHUNK)
            gpre, utri = forms[s][0], forms[s][2]
            cs_t = sum(col_g_rows[s][1:], col_g_rows[s][0]).T
            dig_all = sum(dig_parts[s][1:], dig_parts[s][0]) + cs_t
            db_cols = sum(db_parts[s][1:], db_parts[s][0])
            shifted = jnp.zeros((CHUNK, 128), F32)
            for h in range(N_HEADS):
                shifted = shifted + jnp.where(lane == N_HEADS + h, cs_t[:, h:h + 1], 0.0)
            dlf = _dot_f32(utri, db_cols - shifted)
            dgates = dig_all + dlf * _sigmoid(-gpre)
            dp_ref[rows, N_MAIN:N_MAIN + 128] = dgates.astype(BF16)
            dbg_ref[...] += jnp.sum(dgates, axis=0, keepdims=True)
        dp_ref[:, N_MAIN + 128:N_PAD] = jnp.zeros((blk, N_PAD - N_MAIN - 128), BF16)

        da_pad = dapad[...]
        da = da_pad[0:blk, :]
        dcb_ref[...] += jnp.sum(da, axis=0, keepdims=True)
        x = qk_ref[...]
        dx = jnp.zeros((blk, 1024), F32)
        for j in range(CONV_WIDTH):
            da_j = _rows_ahead(da_pad, CONV_WIDTH - 1 - j)[0:blk, :]
            dcw_ref[j:j + 1, :] += jnp.sum(da_j * x, axis=0, keepdims=True)
            dx = dx + cw_ref[j:j + 1, :] * da_j
        dp_ref[:, 1024:2048] = dx.astype(BF16)
        dapad[blk:blk + CONV_HALO, :] = dapad[0:CONV_HALO, :]

    bmap = lambda i: n_blocks - 1 - i
    wide = pl.BlockSpec((blk, 1024), lambda i: (bmap(i), 0))
    state = pl.BlockSpec((per_step, 8, 128), lambda i: (bmap(i), 0, 0))
    in_specs = [
        pl.BlockSpec((blk, 512), lambda i: (bmap(i), 1)),
        pl.BlockSpec((blk, 1024), lambda i: (bmap(i), 1)),
        pl.BlockSpec((blk, 512), lambda i: (bmap(i), 4)),
        pl.BlockSpec((blk, 512), lambda i: (bmap(i), 5)),
        pl.BlockSpec((blk, 512), lambda i: (bmap(i), 6)),
        pl.BlockSpec((blk, 128), lambda i: (bmap(i), 0)),
        wide, wide,
        pl.BlockSpec((blk, D_POOL), lambda i: (bmap(i), 0)),
        pl.BlockSpec((per_step, N_HEADS, HEAD_DIM, HEAD_DIM), lambda i: (bmap(i), 0, 0, 0)),
        state, state,
        pl.BlockSpec((1, 8, 128), lambda i: (jnp.minimum((bmap(i) + 1) * per_step, n_chunks - 1), 0, 0)),
        _full((1, 128)), _full((8, 1024)), _full((4, 128, 128)), _full((1, 512)), _full((1, 512))]
    return pl.pallas_call(
        body, name="mix_bwd", grid=(n_blocks,),
        out_shape=(jax.ShapeDtypeStruct((seq, N_PAD), BF16), jax.ShapeDtypeStruct((8, 1024), F32),
                   jax.ShapeDtypeStruct((1, 1024), F32), jax.ShapeDtypeStruct((4, 128, 128), F32),
                   jax.ShapeDtypeStruct((1, 512), F32), jax.ShapeDtypeStruct((1, 512), F32),
                   jax.ShapeDtypeStruct((1, 128), F32)),
        in_specs=in_specs,
        out_specs=(pl.BlockSpec((blk, N_PAD), lambda i: (bmap(i), 0)), _full((8, 1024)), _full((1, 1024)),
                   _full((4, 128, 128)), _full((1, 512)), _full((1, 512)), _full((1, 128))),
        scratch_shapes=[pltpu.VMEM((N_HEADS, HEAD_DIM, HEAD_DIM), F32), pltpu.VMEM((8, 128), F32),
                        pltpu.VMEM((blk + CONV_HALO, 1024), F32), pltpu.VMEM((blk + POOL_HALO, D_POOL), F32)],
        compiler_params=_params(("arbitrary",)),
    )(proj, proj, proj, proj, proj, gates, dmix, conv_a, pooled, cst, nst, mst, mst, bg_pad, conv_w8,
      w_pool, ls_pool, mh_g)


def _bwd_in(dproj, w_in_t, x, dx2, norm_g, scale):
    seq = x.shape[0]
    tm = min(512, seq)
    sub = min(256, tm)

    def body(dp_ref, wt_ref, x_ref, dx2_ref, ng_ref, sc_ref, gx_ref, dsh_ref, dsc_ref, dng_ref):
        @pl.when(pl.program_id(0) == 0)
        def _():
            dsh_ref[...] = jnp.zeros_like(dsh_ref)
            dsc_ref[...] = jnp.zeros_like(dsc_ref)
            dng_ref[...] = jnp.zeros_like(dng_ref)

        ng = ng_ref[...]
        one_sc = 1.0 + sc_ref[...]

        def chain(n):
            rows = slice(n * sub, (n + 1) * sub)
            dh = _dot(dp_ref[rows, :], wt_ref[...])
            yield
            xt = x_ref[rows, :]
            r = lax.rsqrt(jnp.mean(xt * xt, axis=-1, keepdims=True) + EPS)
            xn = xt * r
            dsh_ref[...] += jnp.sum(dh, axis=0, keepdims=True)
            dhxn = dh * xn
            dsc_ref[...] += jnp.sum(dhxn * ng, axis=0, keepdims=True)
            dng_ref[...] += jnp.sum(dhxn * one_sc, axis=0, keepdims=True)
            dxn = dh * (ng * one_sc)
            gx_ref[rows, :] = r * (dxn - xn * jnp.mean(dxn * xn, axis=-1, keepdims=True)) + dx2_ref[rows, :]

        _in_lockstep(chain(n) for n in range(tm // sub))

    tile = pl.BlockSpec((tm, D_MODEL), lambda i: (i, 0))
    vec = _full((1, D_MODEL))
    return pl.pallas_call(
        body, name="bwd_in", grid=(seq // tm,),
        out_shape=(jax.ShapeDtypeStruct((seq, D_MODEL), F32),) + (jax.ShapeDtypeStruct((1, D_MODEL), F32),) * 3,
        in_specs=[pl.BlockSpec((tm, N_PAD), lambda i: (i, 0)), _full((N_PAD, D_MODEL)), tile, tile, vec, vec],
        out_specs=(tile, vec, vec, vec),
        compiler_params=_params(("arbitrary",)),
    )(dproj, w_in_t, x, dx2, norm_g, scale)


def _dw_in(h_b, dproj):
    seq = h_b.shape[0]
    tk = min(4096, seq)
    tn = 768
    n_t = seq // tk

    def body(h_ref, dp_ref, dwt_ref, acc):
        t = pl.program_id(1)

        @pl.when(t == 0)
        def _():
            acc[...] = jnp.zeros_like(acc)

        acc[...] += _dot_tn(dp_ref[...], h_ref[...])

        @pl.when(t == n_t - 1)
        def _():
            dwt_ref[...] = acc[...].astype(BF16)

    return pl.pallas_call(
        body, name="dw_in", grid=(N_PAD // tn, n_t),
        out_shape=jax.ShapeDtypeStruct((N_PAD, D_MODEL), BF16),
        in_specs=[pl.BlockSpec((tk, D_MODEL), lambda j, t: (t, 0)), pl.BlockSpec((tk, tn), lambda j, t: (t, j))],
        out_specs=pl.BlockSpec((tn, D_MODEL), lambda j, t: (j, 0)),
        scratch_shapes=[pltpu.VMEM((tn, D_MODEL), F32)],
        compiler_params=_params(("arbitrary", "arbitrary")),
    )(h_b, dproj)


def _adam_update(g, w, m, v, g_ref, d_ref, m_ref, v_ref):
    mn = ADAM_B1 * m + (1.0 - ADAM_B1) * g
    vn = ADAM_B2 * v + (1.0 - ADAM_B2) * (g * g)
    m_hat = mn / (1.0 - ADAM_B1 ** ADAM_STEP)
    v_hat = vn / (1.0 - ADAM_B2 ** ADAM_STEP)
    g_ref[...] = g
    d_ref[...] = -ADAM_LR * (m_hat / (jnp.sqrt(v_hat) + ADAM_EPS) + ADAM_WD * w)
    m_ref[...] = mn
    v_ref[...] = vn


def _adam_sum(name, parts, w, m, v, row_tile, col_tile=None):
    rows, cols = w.shape
    col_tile = cols if col_tile is None else col_tile
    n_parts = parts.shape[0]

    def body(p_ref, w_ref, m_ref, v_ref, g_out, d_out, m_out, v_out):
        g = p_ref[0].astype(F32)
        for j in range(1, n_parts):
            g = g + p_ref[j].astype(F32)
        _adam_update(g, w_ref[...], m_ref[...], v_ref[...], g_out, d_out, m_out, v_out)

    tile = pl.BlockSpec((row_tile, col_tile), lambda i, j: (i, j))
    return pl.pallas_call(
        body, name=name, grid=(rows // row_tile, cols // col_tile),
        out_shape=(jax.ShapeDtypeStruct((rows, cols), F32),) * 4,
        in_specs=[pl.BlockSpec((n_parts, row_tile, col_tile), lambda i, j: (0, i, j)), tile, tile, tile],
        out_specs=(tile,) * 4,
        compiler_params=_params(("arbitrary", "arbitrary")),
    )(parts, w, m, v)


def _adam_ada(sc_all16, dmod_blk16, w, m, v):
    rows, cols = w.shape

    def body(sc_ref, dm_ref, w_ref, m_ref, v_ref, g_out, d_out, m_out, v_out):
        g = _dot_tn(sc_ref[...].astype(BF16), dm_ref[...].astype(BF16))
        _adam_update(g, w_ref[...], m_ref[...], v_ref[...], g_out, d_out, m_out, v_out)

    return pl.pallas_call(
        body, name="adam_w_ada", grid=(1,),
        out_shape=(jax.ShapeDtypeStruct((rows, cols), F32),) * 4,
        in_specs=[_full(sc_all16.shape), _full(dmod_blk16.shape)] + [_full((rows, cols))] * 3,
        out_specs=(_full((rows, cols)),) * 4,
        compiler_params=_params(("arbitrary",)),
    )(sc_all16, dmod_blk16, w, m, v)


def _pack_small(pieces):
    rows = []
    for name, n in PACK_ROWS:
        a = pieces[name].reshape(-1).astype(F32)
        a = jnp.pad(a, (0, n * 128 - a.shape[0]))
        rows.append(a.reshape(n, 128))
    return jnp.concatenate(rows, axis=0)


def _unpack_small(pack, name, shape):
    off, _ = PACK_OFF[name]
    size = 1
    for s in shape:
        size *= s
    n_rows = -(-size // 128)
    return pack[off:off + n_rows].reshape(-1)[:size].reshape(shape)


def _local_step(x2, tgt2, shift, scale, gate, norm_g, w_in_t, w_out_b, conv_w, conv_b, w_pool, ls_pool,
                mh_norm_g, b_gates, final_g, send_dw_out=None, send_dw_in=None):
    bg_pad = jnp.pad(b_gates, ((0, 0), (0, 128 - b_gates.shape[1])))
    conv_w8 = jnp.pad(conv_w, ((0, 8 - CONV_WIDTH), (0, 0)))
    fg = final_g.reshape(1, D_MODEL)

    proj, gates, h_b = _fwd_proj(x2, norm_g, scale, shift, w_in_t)
    mix, cst, nst, mst, conv_a, pooled = _mix_fwd(proj, gates, bg_pad, conv_w8, conv_b, w_pool, ls_pool, mh_norm_g)
    dx2, dmix, dwo, dgate, dfg, loss = _out_fwd_bwd(mix, x2, tgt2, w_out_b, gate, fg)
    if send_dw_out is not None:
        bg_pad = bg_pad + send_dw_out(dwo)
    dproj, dcw8, dcb, dwp, dls, dmhg, dbg = _mix_bwd(proj, gates, dmix, conv_a, pooled, cst, nst, mst, bg_pad,
                                                      conv_w8, w_pool, ls_pool, mh_norm_g)
    dw_in_t = _dw_in(h_b, dproj)[:N_IN]
    ng_in = norm_g
    if send_dw_in is not None:
        ng_in = norm_g + send_dw_in(dw_in_t, dcw8[:CONV_WIDTH])
    gx, dsh, dsc, dng = _bwd_in(dproj, w_in_t, x2, dx2, ng_in, scale)
    return dict(loss=loss, grad_x=gx, dw_in_t=dw_in_t, dw_out=dwo, dconv_w=dcw8[:CONV_WIDTH], conv_b=dcb,
                w_pool=dwp, ls_pool=dls, mh_norm_g=dmhg, b_gates=dbg, final_g=dfg, norm_g=dng,
                dmod=jnp.concatenate([dsh, dsc, dgate], axis=1))


def kernel(x, c, norm_g, w_ada, b_ada, w_in, b_gates, conv_w, conv_b, w_pool, ls_pool, mh_norm_g, w_out, final_g, loss_target, m_norm_g, m_w_ada, m_b_ada, m_w_in, m_b_gates, m_conv_w, m_conv_b, m_w_pool, m_ls_pool, m_mh_norm_g, m_w_out, m_final_g, v_norm_g, v_w_ada, v_b_ada, v_w_in, v_b_gates, v_conv_w, v_conv_b, v_w_pool, v_ls_pool, v_mh_norm_g, v_w_out, v_final_g):
    seq = x.shape[1]
    me = 4 * lax.axis_index("x") + 2 * lax.axis_index("y") + lax.axis_index("c")

    g_in, g_out, g_cw, g_c = _all_gather("gather_weights", w_in[0].astype(BF16).T, w_out[0].astype(BF16), conv_w[0], c)
    w_in_t = jnp.pad(g_in.reshape(N_IN, D_MODEL), ((0, N_PAD - N_IN), (0, 0)))
    w_out_b = g_out.reshape(D_MODEL, D_MODEL)
    conv_w_full = jnp.transpose(g_cw, (1, 0, 2)).reshape(CONV_WIDTH, 2 * D_MLSTM)
    c_all16 = jnp.pad(g_c.reshape(N_DEV, D_MODEL), ((0, 8), (0, 0)))

    b_ada_blk = lax.dynamic_slice(b_ada, (0, me * ADA_SHARD), (1, ADA_SHARD))
    mod_all, sc_all16 = _ada_mod(c_all16, w_ada[0], b_ada_blk)
    mod = lax.dynamic_index_in_dim(mod_all, me, axis=1, keepdims=False).reshape(1, 3 * D_MODEL)
    shift, scale, gate = mod[:, :D_MODEL], mod[:, D_MODEL:2 * D_MODEL], mod[:, 2 * D_MODEL:]

    flights = {}

    def send_dw_out(dwo):
        blocks = dwo.astype(BF16).reshape(N_DEV, D_MODEL // N_DEV, D_MODEL)
        flights["out"], token = _scatter_start("send_dw_out", (blocks,))
        return token

    def send_dw_in(dw_in_t, dcw):
        blocks = dw_in_t.reshape(N_DEV, N_SHARD, D_MODEL)
        dcw_blocks = jnp.transpose(dcw.reshape(CONV_WIDTH, N_DEV, 128), (1, 0, 2))
        flights["in"], token = _scatter_start("send_dw_in", (blocks, dcw_blocks))
        return token

    r = _local_step(x[0], loss_target[0], shift, scale, gate, norm_g, w_in_t, w_out_b, conv_w_full, conv_b,
                    w_pool[0], ls_pool, mh_norm_g, b_gates, final_g, send_dw_out, send_dw_in)

    pack = _pack_small(dict(loss=r["loss"][:, :1], final_g=r["final_g"], norm_g=r["norm_g"], conv_b=r["conv_b"],
                            ls_pool=r["ls_pool"], mh_norm_g=r["mh_norm_g"], b_gates=r["b_gates"][:, :8],
                            b_ada=r["dmod"], w_pool=r["w_pool"]))
    (p_pack,) = _all_gather("gather_small", pack)
    p_in, p_cw = _scatter_wait("recv_dw_in", flights["in"], p_pack)
    (p_out,) = _scatter_wait("recv_dw_out", flights["out"], p_cw)

    in_t = _adam_sum("adam_w_in", p_in, w_in[0].T, m_w_in[0].T, v_w_in[0].T, N_SHARD, 256)
    gi, di, mi, vi = (o.T for o in in_t)
    go, do_, mo, vo = _adam_sum("adam_w_out", p_out, w_out[0], m_w_out[0], v_w_out[0], 128)
    gc, dc, mc, vc = _adam_sum("adam_conv_w", p_cw, conv_w[0], m_conv_w[0], v_conv_w[0], CONV_WIDTH)

    def small(loss_like, fg_, ng_, cb_, ls_, mh_, bg_, ba_, wp_):
        return _pack_small(dict(loss=loss_like, final_g=fg_, norm_g=ng_, conv_b=cb_, ls_pool=ls_, mh_norm_g=mh_,
                                b_gates=bg_, b_ada=ba_, w_pool=wp_))

    zero = jnp.zeros((1, 1), F32)
    w_pack = small(zero, final_g, norm_g, conv_b, ls_pool, mh_norm_g, b_gates, b_ada, w_pool)
    m_pack = small(zero, m_final_g, m_norm_g, m_conv_b, m_ls_pool, m_mh_norm_g, m_b_gates, m_b_ada, m_w_pool)
    v_pack = small(zero, v_final_g, v_norm_g, v_conv_b, v_ls_pool, v_mh_norm_g, v_b_gates, v_b_ada, v_w_pool)
    gp, dp, mp, vp = _adam_sum("adam_small", p_pack, w_pack, m_pack, v_pack, PACK_TOTAL)

    off, rows = PACK_OFF["b_ada"]
    dmod_all = p_pack[:, off:off + rows, :].reshape(N_DEV, 3 * D_MODEL)
    dmod_blk16 = jnp.pad(lax.dynamic_slice(dmod_all, (0, me * ADA_SHARD), (N_DEV, ADA_SHARD)), ((0, 8), (0, 0)))
    ga, da, ma, va = _adam_ada(sc_all16, dmod_blk16, w_ada[0], m_w_ada[0], v_w_ada[0])

    names = ("norm_g", "w_ada", "b_ada", "w_in", "b_gates", "conv_w", "conv_b", "w_pool", "ls_pool", "mh_norm_g",
             "w_out", "final_g")
    shapes = dict(norm_g=norm_g.shape, b_ada=b_ada.shape, b_gates=b_gates.shape, conv_b=conv_b.shape,
                  w_pool=w_pool.shape, ls_pool=ls_pool.shape, mh_norm_g=mh_norm_g.shape, final_g=final_g.shape)
    sharded = dict(w_ada=(ga, da, ma, va), w_in=(gi, di, mi, vi), conv_w=(gc, dc, mc, vc), w_out=(go, do_, mo, vo))
    outs = []
    for kind in range(4):
        for nm in names:
            if nm in sharded:
                outs.append(sharded[nm][kind][None])
            else:
                outs.append(_unpack_small((gp, dp, mp, vp)[kind], nm, shapes[nm]))
    loss = gp[0, 0]
    grad_x = r["grad_x"].reshape(1, seq, D_MODEL)
    return (loss, grad_x, *outs)
```

```python
import jax
import jax.numpy as jnp
from jax import lax
from jax.experimental import pallas as pl
from jax.experimental.pallas import tpu as pltpu

F32 = jnp.float32
BF16 = jnp.bfloat16

D_MODEL = 1024
D_POOL = 512
D_MLSTM = 512
N_HEADS = 4
HEAD_DIM = 128
CHUNK = 128
POOL_WINDOWS = (2, 4, 8, 16)
POOL_GROUP_DIM = 128
CONV_WIDTH = 4
EPS = 1e-6
N_MAIN = 3584
N_IN = 3592
N_PAD = 3840
N_SHARD = N_IN // 8
ADA_SHARD = 3 * D_MODEL // 8
N_DEV = 8
CONV_HALO = 8
POOL_HALO = 16
NEG_BIG = -1e30
VMEM_LIMIT_BYTES = 56 * 1024 * 1024

ADAM_LR = 0.001
ADAM_B1 = 0.9
ADAM_B2 = 0.999
ADAM_EPS = 1e-08
ADAM_WD = 0.01
ADAM_STEP = 10

def _dot(a, b):
    return jnp.dot(a, b, preferred_element_type=F32)


def _dot_nt(a, b):
    return lax.dot_general(a, b, (((1,), (1,)), ((), ())), preferred_element_type=F32)


def _dot_tn(a, b):
    return lax.dot_general(a, b, (((0,), (0,)), ((), ())), preferred_element_type=F32)


def _dot_f32(a, b):
    return jnp.dot(a, b, precision=lax.Precision.HIGHEST, preferred_element_type=F32)


def _row_mean_mxu(x):
    return _dot(x.astype(BF16), jnp.full((HEAD_DIM, HEAD_DIM), 1.0 / HEAD_DIM, BF16))


def _sigmoid(x):
    return jax.nn.sigmoid(x)


def _log_sigmoid(x):
    return jnp.minimum(x, 0.0) - jnp.log1p(jnp.exp(-jnp.abs(x)))


def _params(sem):
    return pltpu.CompilerParams(dimension_semantics=sem, vmem_limit_bytes=VMEM_LIMIT_BYTES)


def _full(shape):
    n = len(shape)
    return pl.BlockSpec(shape, lambda *_: (0,) * n)


def _mesh_pos():
    return lax.axis_index("x"), lax.axis_index("y"), lax.axis_index("c")


def _peer(k):
    x, y, c = _mesh_pos()
    px = 1 - x if (k >> 2) & 1 else x
    py = 1 - y if (k >> 1) & 1 else y
    pc = 1 - c if k & 1 else c
    return (px, py, pc), 4 * px + 2 * py + pc


def _remote(src, dst, send_sem, recv_sem, to):
    return pltpu.make_async_remote_copy(src_ref=src, dst_ref=dst, send_sem=send_sem, recv_sem=recv_sem, device_id=to,
                                        device_id_type=pl.DeviceIdType.MESH)


def _other_chips():
    x, y, _ = _mesh_pos()
    return [(1 - x, y), (x, 1 - y), (1 - x, 1 - y)]


def _two_level_gather(src, dst, send_sems, recv_sems, local_sems):
    n = len(src)
    x, y, c = _mesh_pos()
    me = 4 * x + 2 * y + c
    sibling = (x, y, 1 - c)
    chips = _other_chips()

    def copy(a, k, block, to, own):
        return _remote(src[a] if own else dst[a].at[block], dst[a].at[block], send_sems.at[a, k], recv_sems.at[a, k], to)

    local = [pltpu.make_async_copy(src[a], dst[a].at[me], local_sems.at[a]) for a in range(n)]
    first = [copy(a, 0, me, sibling, True) for a in range(n)]
    first += [copy(a, 1 + j, me, (*chip, c), True) for j, chip in enumerate(chips) for a in range(n)]
    for cp in local + first:
        cp.start()
    passed = []
    for j, (px, py) in enumerate(chips):
        block = 4 * px + 2 * py + c
        for a in range(n):
            copy(a, 1 + j, block, sibling, False).wait_recv()
            passed.append(copy(a, 4 + j, block, sibling, False))
            passed[-1].start()
    for a in range(n):
        copy(a, 0, 4 * x + 2 * y + (1 - c), sibling, False).wait_recv()
    for j, (px, py) in enumerate(chips):
        for a in range(n):
            copy(a, 4 + j, 4 * px + 2 * py + (1 - c), sibling, False).wait_recv()
    for cp in first + passed:
        cp.wait_send()
    for cp in local:
        cp.wait()


GATHER_COPIES = 7


def _all_gather(name, *shards):
    n = len(shards)

    def body(*refs):
        _two_level_gather(refs[:n], refs[n:2 * n], *refs[2 * n:])

    hbm = pl.BlockSpec(memory_space=pltpu.HBM)
    return pl.pallas_call(
        body, name=name,
        out_shape=tuple(jax.ShapeDtypeStruct((N_DEV,) + s.shape, s.dtype) for s in shards),
        in_specs=[hbm] * n, out_specs=tuple([hbm] * n),
        scratch_shapes=[pltpu.SemaphoreType.DMA((n, GATHER_COPIES)), pltpu.SemaphoreType.DMA((n, GATHER_COPIES)),
                        pltpu.SemaphoreType.DMA((n,))],
    )(*shards)


def _ada_mod(c_all16, w_ada_blk, b_ada_blk):
    def body(c_ref, w_ref, b_ref, out_ref, sc_ref, send_sems, recv_sems):
        x, y, c = _mesh_pos()
        me = 4 * x + 2 * y + c
        cv = c_ref[...]
        sc = cv * _sigmoid(cv)
        sc_ref[...] = sc
        blk = _dot(sc.astype(BF16), w_ref[...].astype(BF16)) + b_ref[...]
        out_ref[me] = blk[0:N_DEV, :]
        copies = []
        for k in range(1, N_DEV):
            peer, _ = _peer(k)
            copies.append(pltpu.make_async_remote_copy(
                src_ref=out_ref.at[me], dst_ref=out_ref.at[me], send_sem=send_sems.at[k - 1],
                recv_sem=recv_sems.at[k - 1], device_id=peer, device_id_type=pl.DeviceIdType.MESH))
        for cp in copies:
            cp.start()
        for cp in copies:
            cp.wait()

    vmem = pl.BlockSpec(memory_space=pltpu.VMEM)
    return pl.pallas_call(
        body, name="ada_mod",
        out_shape=(jax.ShapeDtypeStruct((N_DEV, N_DEV, ADA_SHARD), F32),
                   jax.ShapeDtypeStruct(c_all16.shape, F32)),
        in_specs=[vmem] * 3, out_specs=(vmem, vmem),
        scratch_shapes=[pltpu.SemaphoreType.DMA((N_DEV - 1,)), pltpu.SemaphoreType.DMA((N_DEV - 1,))],
    )(c_all16, w_ada_blk, b_ada_blk)


def _scatter_copies(src, land, send_sems, recv_sems):
    x, y, c = _mesh_pos()
    me = 4 * x + 2 * y + c
    copies = []
    for k in range(1, N_DEV):
        peer, p = _peer(k)
        for a in range(len(src)):
            i = a * (N_DEV - 1) + k - 1
            copies.append(_remote(src[a].at[p], land[a].at[me], send_sems.at[i], recv_sems.at[i], peer))
    return copies


def _scatter_start(name, blocks):
    n = len(blocks)

    def body(*refs):
        src, land = refs[:n], refs[n:2 * n]
        send_sems, recv_sems = refs[2 * n], refs[2 * n + 1]
        token_ref = refs[-1]
        for cp in _scatter_copies(src, land, send_sems, recv_sems):
            cp.start()
        token_ref[...] = jnp.zeros_like(token_ref)

    hbm = pl.BlockSpec(memory_space=pltpu.HBM)
    sem = pl.BlockSpec(memory_space=pltpu.SEMAPHORE)
    through = tuple(pltpu.HBM(b.shape, b.dtype) for b in blocks)
    args = [pltpu.with_memory_space_constraint(b, pltpu.HBM) for b in blocks]
    args += [pltpu.with_memory_space_constraint(lax.empty(b.shape, b.dtype), pltpu.HBM) for b in blocks]
    out = pl.pallas_call(
        body, name=name,
        out_shape=(pltpu.SemaphoreType.DMA((n * (N_DEV - 1),)),) * 2 + through + through
        + (jax.ShapeDtypeStruct((8, 128), F32),),
        in_specs=[hbm] * (2 * n),
        out_specs=(sem, sem) + (hbm,) * (2 * n) + (pl.BlockSpec(memory_space=pltpu.VMEM),),
        input_output_aliases={i: 2 + i for i in range(2 * n)},
        compiler_params=pltpu.CompilerParams(has_side_effects=pltpu.SideEffectType.DATAFLOW_SIDE_EFFECTING),
    )(*args)
    return out[:-1], out[-1][0:1, 0:1]


def _scatter_wait(name, state, after):
    n = (len(state) - 2) // 2
    send_sems, recv_sems = state[0], state[1]
    src, land = state[2:2 + n], state[2 + n:]

    def body(*refs):
        src_r, land_r = refs[:n], refs[n:2 * n]
        for cp in _scatter_copies(src_r, land_r, refs[2 * n], refs[2 * n + 1]):
            cp.wait_send()
            cp.wait_recv()

    hbm = pl.BlockSpec(memory_space=pltpu.HBM)
    sem = pl.BlockSpec(memory_space=pltpu.SEMAPHORE)
    out = pl.pallas_call(
        body, name=name,
        out_shape=tuple(pltpu.HBM(b.shape, b.dtype) for b in src + land),
        in_specs=[hbm] * (2 * n) + [sem, sem, pl.BlockSpec(memory_space=pl.ANY)],
        out_specs=(hbm,) * (2 * n),
        input_output_aliases={i: i for i in range(2 * n)},
        compiler_params=pltpu.CompilerParams(has_side_effects=pltpu.SideEffectType.DATAFLOW_SIDE_EFFECTING),
    )(*src, *land, send_sems, recv_sems, after)
    me = 4 * lax.axis_index("x") + 2 * lax.axis_index("y") + lax.axis_index("c")
    landed = []
    for a in range(n):
        own = lax.dynamic_index_in_dim(out[a], me, axis=0, keepdims=True)
        landed.append(lax.dynamic_update_slice_in_dim(out[n + a], own, me, axis=0))
    return landed


def _fwd_proj(x, norm_g, scale, shift, w_in_t):
    seq = x.shape[0]
    tm = min(512, seq)
    sub = min(256, tm)
    tn = 512

    def body(x_ref, ng_ref, sc_ref, sh_ref, wt_ref, proj_ref, gates_ref, h_ref):
        def chain(n):
            for _ in range(n):
                yield
            rows = slice(n * sub, (n + 1) * sub)
            xt = x_ref[rows, :]
            r = lax.rsqrt(jnp.mean(xt * xt, axis=-1, keepdims=True) + EPS)
            h = ((xt * r) * ng_ref[...]) * (1.0 + sc_ref[...]) + sh_ref[...]
            hb = h.astype(BF16)
            h_ref[rows, :] = hb
            yield
            gates_ref[rows, :] = _dot_nt(hb, wt_ref[N_MAIN:N_MAIN + 128, :])
            for j in range(N_MAIN // tn):
                proj_ref[rows, j * tn:(j + 1) * tn] = _dot_nt(hb, wt_ref[j * tn:(j + 1) * tn, :])

        _in_lockstep(chain(n) for n in range(tm // sub))

    vec = _full((1, D_MODEL))
    tile = pl.BlockSpec((tm, D_MODEL), lambda i: (i, 0))
    return pl.pallas_call(
        body, name="fwd_proj", grid=(seq // tm,),
        out_shape=(jax.ShapeDtypeStruct((seq, N_MAIN), F32), jax.ShapeDtypeStruct((seq, 128), F32),
                   jax.ShapeDtypeStruct((seq, D_MODEL), BF16)),
        in_specs=[tile, vec, vec, vec, _full((N_PAD, D_MODEL))],
        out_specs=(pl.BlockSpec((tm, N_MAIN), lambda i: (i, 0)), pl.BlockSpec((tm, 128), lambda i: (i, 0)), tile),
        compiler_params=_params(("arbitrary",)),
    )(x, norm_g, scale, shift, w_in_t)


def _gate_forms(gpre):
    r = lax.broadcasted_iota(jnp.int32, (CHUNK, CHUNK), 0)
    c = lax.broadcasted_iota(jnp.int32, (CHUNK, CHUNK), 1)
    causal = c <= r
    ltri = jnp.where(causal, 1.0, 0.0).astype(F32)
    utri = jnp.where(r <= c, 1.0, 0.0).astype(F32)
    bcol = _dot_f32(ltri, _log_sigmoid(gpre))
    gt8 = gpre.T[0:8, :]
    brow = _dot_f32(_log_sigmoid(gt8), utri)
    return causal, utri, bcol, gt8, brow


def _in_lockstep(stages):
    alive = list(stages)
    while alive:
        still = []
        for g in alive:
            try:
                next(g)
                still.append(g)
            except StopIteration:
                pass
        alive = still


def _head_fwd(qh, kh, vh, bc, br, igr, m_prev, c_h, n_row, causal):
    qb, kb, vb, cb = qh.astype(BF16), kh.astype(BF16), vh.astype(BF16), c_h.astype(BF16)
    qk = _dot_nt(qb, kb)
    cq = _dot_nt(qb, cb)
    nq = _dot_nt(qb, jnp.broadcast_to(n_row.astype(BF16), (HEAD_DIM, HEAD_DIM)))
    yield
    dlog = jnp.where(causal, bc - br + igr, NEG_BIG)
    inter_log = bc + m_prev
    m_t = jnp.maximum(inter_log, jnp.max(dlog, axis=-1, keepdims=True))
    yield
    dmat = jnp.exp(dlog - m_t)
    inter = jnp.exp(inter_log - m_t)
    s = qk * dmat
    sv = _dot(s.astype(BF16), vb)
    yield
    den = jnp.sum(s, axis=-1, keepdims=True) + inter * nq
    emt = jnp.exp(-m_t)
    yield
    num = sv + inter * cq
    dn = jnp.maximum(jnp.abs(den), emt)
    hm = num / dn
    return dict(dmat=dmat, inter=inter, qb=qb, kb=kb, vb=vb, cb=cb, s=s, cq=cq, nq=nq, den=den, emt=emt,
                dn=dn, hm=hm)


def _state_weights(bc, igc, m_prev, m_new=None):
    last = lax.broadcasted_iota(jnp.int32, (CHUNK, 1), 0) == CHUNK - 1
    b_last = jnp.sum(jnp.where(last, bc, 0.0), axis=0, keepdims=True)
    wlog = b_last - bc + igc
    if m_new is None:
        m_new = jnp.maximum(b_last + m_prev, jnp.max(wlog, axis=0, keepdims=True))
    w_c = jnp.exp(wlog - m_new)
    decay = jnp.exp(b_last + m_prev - m_new)
    return w_c, decay, m_new, last


def _rows_back(x, k):
    return x if k == 0 else pltpu.roll(x, k, 0)


def _rows_ahead(x, k):
    return x if k == 0 else pltpu.roll(x, x.shape[0] - k, 0)


def _conv_taps(xpad):
    return [_rows_back(xpad, CONV_WIDTH - 1 - j)[CONV_HALO:, :] for j in range(CONV_WIDTH)]


def _conv_pre(taps, cw_ref, cb_ref):
    a = cb_ref[...]
    for j in range(CONV_WIDTH):
        a = a + cw_ref[j:j + 1, :] * taps[j]
    return a


def _window_sum(x, w, shift):
    k = 1
    while k < w:
        x = x + shift(x, k)
        k *= 2
    return x


def _pool_window_sum(upad_ref, g, w):
    lanes = slice(g * POOL_GROUP_DIM, (g + 1) * POOL_GROUP_DIM)
    return _window_sum(upad_ref[:, lanes], w, _rows_back)[POOL_HALO:, :]


def _pool_inv_count(row0, rows, w):
    pos = row0 + lax.broadcasted_iota(jnp.int32, (rows, 1), 0) + 1
    return 1.0 / jnp.minimum(pos, w).astype(F32)


FWD_CHUNKS = 2
BWD_CHUNKS = 2


def _mix_fwd(proj, gates, bg_pad, conv_w8, conv_b, w_pool, ls_pool, mh_g):
    seq = proj.shape[0]
    n_chunks = seq // CHUNK
    per_step = FWD_CHUNKS
    blk = per_step * CHUNK

    def body(uz_ref, qk_ref, v_ref, o_ref, zm_ref, uh_ref, qkh_ref, g_ref, bg_ref, cw_ref, cb_ref, wp_ref,
             ls_ref, mhg_ref, mix_ref, cst_ref, nst_ref, mst_ref, a_ref, pooled_ref, c_scr, n_scr, m_scr, xpad, upad):
        i = pl.program_id(0)

        @pl.when(i == 0)
        def _():
            c_scr[...] = jnp.zeros_like(c_scr)
            n_scr[...] = jnp.zeros_like(n_scr)
            m_scr[...] = jnp.zeros_like(m_scr)

        first = i == 0

        upad[0:POOL_HALO, :] = jnp.where(first, 0.0, uh_ref[...])
        upad[POOL_HALO:POOL_HALO + blk, :] = uz_ref[:, 0:D_POOL]
        for g, w in enumerate(POOL_WINDOWS):
            lanes = slice(g * POOL_GROUP_DIM, (g + 1) * POOL_GROUP_DIM)
            pooled = (_pool_window_sum(upad, g, w) * _pool_inv_count(i * blk, blk, w) - uz_ref[:, lanes]).astype(BF16)
            pooled_ref[:, lanes] = pooled
            y = _dot(pooled, wp_ref[g].astype(BF16)) * ls_ref[:, lanes]
            zp = uz_ref[:, D_POOL + g * POOL_GROUP_DIM:D_POOL + (g + 1) * POOL_GROUP_DIM]
            mix_ref[:, lanes] = (y * (zp * _sigmoid(zp))).astype(BF16)

        xpad[0:CONV_HALO, :] = jnp.where(first, 0.0, qkh_ref[...])
        xpad[CONV_HALO:CONV_HALO + blk, :] = qk_ref[...]
        a = _conv_pre(_conv_taps(xpad[...]), cw_ref, cb_ref)
        a_ref[...] = a
        qk = a * _sigmoid(a)

        def head(rows, h, qh, kh, vh, bc, br, igr, m_prev, c_h, n_row, causal):
            lanes = slice(h * HEAD_DIM, (h + 1) * HEAD_DIM)
            f = yield from _head_fwd(qh, kh, vh, bc, br, igr, m_prev, c_h, n_row, causal)
            yield
            hm = f["hm"]
            hn = hm * lax.rsqrt(_row_mean_mxu(hm * hm) + EPS) * mhg_ref[:, lanes]
            zm = zm_ref[rows, lanes]
            out = hn * _sigmoid(o_ref[rows, lanes]) * (zm * _sigmoid(zm))
            mix_ref[rows, D_POOL + h * HEAD_DIM:D_POOL + (h + 1) * HEAD_DIM] = out.astype(BF16)

        c_cur = [c_scr[h] for h in range(N_HEADS)]
        n_cur = [n_scr[h:h + 1, :] for h in range(N_HEADS)]
        m_cur = [m_scr[h:h + 1, 0:1] for h in range(N_HEADS)]
        chains = []
        for s in range(per_step):
            rows = slice(s * CHUNK, (s + 1) * CHUNK)
            gpre = g_ref[rows, :] + bg_ref[...]
            causal, _, bcol, gt8, brow = _gate_forms(gpre)
            nst_ref[s] = jnp.zeros((8, 128), F32)
            mst_ref[s] = jnp.zeros((8, 128), F32)
            for h in range(N_HEADS):
                lanes = slice(h * HEAD_DIM, (h + 1) * HEAD_DIM)
                cst_ref[s, h] = c_cur[h]
                nst_ref[s, h:h + 1, :] = n_cur[h]
                mst_ref[s, h:h + 1, :] = jnp.broadcast_to(m_cur[h], (1, 128))
                qh = qk[rows, lanes]
                kh = qk[rows, D_MLSTM + h * HEAD_DIM:D_MLSTM + (h + 1) * HEAD_DIM] * (HEAD_DIM ** -0.5)
                vh = v_ref[rows, lanes]
                bc = bcol[:, N_HEADS + h:N_HEADS + h + 1]
                br = brow[N_HEADS + h:N_HEADS + h + 1, :]
                igr = gt8[h:h + 1, :]
                igc = gpre[:, h:h + 1]
                chains.append(head(rows, h, qh, kh, vh, bc, br, igr, m_cur[h], c_cur[h], n_cur[h], causal))
                w_c, decay, m_new, _ = _state_weights(bc, igc, m_cur[h])
                c_cur[h] = decay * c_cur[h] + _dot_tn((vh * w_c).astype(BF16), kh.astype(BF16))
                n_cur[h] = decay * n_cur[h] + jnp.sum(w_c * kh, axis=0, keepdims=True)
                m_cur[h] = m_new
        for h in range(N_HEADS):
            c_scr[h] = c_cur[h]
            n_scr[h:h + 1, :] = n_cur[h]
            m_scr[h:h + 1, :] = jnp.broadcast_to(m_cur[h], (1, 128))
        _in_lockstep(chains)

    in_specs = [
        pl.BlockSpec((blk, 1024), lambda i: (i, 0)),
        pl.BlockSpec((blk, 1024), lambda i: (i, 1)),
        pl.BlockSpec((blk, 512), lambda i: (i, 4)),
        pl.BlockSpec((blk, 512), lambda i: (i, 5)),
        pl.BlockSpec((blk, 512), lambda i: (i, 6)),
        pl.BlockSpec((POOL_HALO, 512), lambda i: (jnp.maximum(i * (blk // POOL_HALO) - 1, 0), 0)),
        pl.BlockSpec((CONV_HALO, 1024), lambda i: (jnp.maximum(i * (blk // CONV_HALO) - 1, 0), 1)),
        pl.BlockSpec((blk, 128), lambda i: (i, 0)),
        _full((1, 128)), _full((8, 1024)), _full((1, 1024)), _full((4, 128, 128)), _full((1, 512)),
        _full((1, 512))]
    return pl.pallas_call(
        body, name="mix_fwd", grid=(n_chunks // per_step,),
        out_shape=(jax.ShapeDtypeStruct((seq, D_MODEL), BF16),
                   jax.ShapeDtypeStruct((n_chunks, N_HEADS, HEAD_DIM, HEAD_DIM), F32),
                   jax.ShapeDtypeStruct((n_chunks, 8, 128), F32),
                   jax.ShapeDtypeStruct((n_chunks, 8, 128), F32),
                   jax.ShapeDtypeStruct((seq, 2 * D_MLSTM), F32),
                   jax.ShapeDtypeStruct((seq, D_POOL), BF16)),
        in_specs=in_specs,
        out_specs=(pl.BlockSpec((blk, D_MODEL), lambda i: (i, 0)),
                   pl.BlockSpec((per_step, N_HEADS, HEAD_DIM, HEAD_DIM), lambda i: (i, 0, 0, 0)),
                   pl.BlockSpec((per_step, 8, 128), lambda i: (i, 0, 0)),
                   pl.BlockSpec((per_step, 8, 128), lambda i: (i, 0, 0)),
                   pl.BlockSpec((blk, 2 * D_MLSTM), lambda i: (i, 0)),
                   pl.BlockSpec((blk, D_POOL), lambda i: (i, 0))),
        scratch_shapes=[pltpu.VMEM((N_HEADS, HEAD_DIM, HEAD_DIM), F32), pltpu.VMEM((8, 128), F32),
                        pltpu.VMEM((8, 128), F32), pltpu.VMEM((CONV_HALO + blk, 1024), F32),
                        pltpu.VMEM((POOL_HALO + blk, D_POOL), F32)],
        compiler_params=_params(("arbitrary",)),
    )(proj, proj, proj, proj, proj, proj, proj, gates, bg_pad, conv_w8, conv_b, w_pool, ls_pool, mh_g)


def _out_fwd_bwd(mix, x, tgt, w_out_b, gate, final_g):
    seq = x.shape[0]
    tm = min(512, seq)
    sub = min(256, tm)

    def body(mix_ref, x_ref, t_ref, w_ref, gate_ref, fg_ref, dx2_ref, dmix_ref, dwo_ref, dgate_ref, dfg_ref,
             loss_ref, dwo_acc):
        @pl.when(pl.program_id(0) == 0)
        def _():
            dwo_acc[...] = jnp.zeros_like(dwo_acc)
            dgate_ref[...] = jnp.zeros_like(dgate_ref)
            dfg_ref[...] = jnp.zeros_like(dfg_ref)
            loss_ref[...] = jnp.zeros_like(loss_ref)

        w = w_ref[...]
        gate_v = gate_ref[...]
        fg = fg_ref[...]
        do2_parts = [None] * (tm // sub)

        def chain(n):
            rows = slice(n * sub, (n + 1) * sub)
            o2 = _dot(mix_ref[rows, :], w)
            yield
            x2 = x_ref[rows, :] + gate_v * o2
            r2 = lax.rsqrt(jnp.mean(x2 * x2, axis=-1, keepdims=True) + EPS)
            x2n = x2 * r2
            err = x2n * fg - t_ref[rows, :]
            part = 0.5 * jnp.sum(jnp.sum(err * err, axis=-1, keepdims=True), axis=0, keepdims=True) / D_MODEL
            loss_ref[...] += jnp.broadcast_to(part, loss_ref.shape)
            dy = err / D_MODEL
            dfg_ref[...] += jnp.sum(dy * x2n, axis=0, keepdims=True)
            gdy = dy * fg
            dx2 = r2 * (gdy - x2n * jnp.mean(gdy * x2n, axis=-1, keepdims=True))
            dx2_ref[rows, :] = dx2
            dgate_ref[...] += jnp.sum(dx2 * o2, axis=0, keepdims=True)
            do2 = (dx2 * gate_v).astype(BF16)
            dmix_ref[rows, :] = _dot_nt(do2, w)
            do2_parts[n] = do2

        _in_lockstep(chain(n) for n in range(tm // sub))
        dwo_acc[...] += _dot_tn(mix_ref[...], jnp.concatenate(do2_parts, axis=0))

        @pl.when(pl.program_id(0) == seq // tm - 1)
        def _():
            dwo_ref[...] = dwo_acc[...].astype(BF16)

    tile = pl.BlockSpec((tm, D_MODEL), lambda i: (i, 0))
    vec = _full((1, D_MODEL))
    return pl.pallas_call(
        body, name="out_fwd_bwd", grid=(seq // tm,),
        out_shape=(jax.ShapeDtypeStruct((seq, D_MODEL), F32), jax.ShapeDtypeStruct((seq, D_MODEL), F32),
                   jax.ShapeDtypeStruct((D_MODEL, D_MODEL), BF16), jax.ShapeDtypeStruct((1, D_MODEL), F32),
                   jax.ShapeDtypeStruct((1, D_MODEL), F32), jax.ShapeDtypeStruct((1, 128), F32)),
        in_specs=[tile, tile, tile, _full((D_MODEL, D_MODEL)), vec, vec],
        out_specs=(tile, tile, _full((D_MODEL, D_MODEL)), vec, vec, _full((1, 128))),
        scratch_shapes=[pltpu.VMEM((D_MODEL, D_MODEL), F32)],
        compiler_params=_params(("arbitrary",)),
    )(mix, x, tgt, w_out_b, gate, final_g)


def _mix_bwd(proj, gates, dmix, conv_a, pooled, cst, nst, mst, bg_pad, conv_w8, w_pool, ls_pool, mh_g):
    seq = proj.shape[0]
    n_chunks = seq // CHUNK
    per_step = BWD_CHUNKS
    blk = per_step * CHUNK
    n_blocks = n_chunks // per_step

    def body(zp_ref, qk_ref, v_ref, o_ref, zm_ref, g_ref, dmix_ref, a_ref, pooled_ref, cst_ref, nst_ref, mst_ref,
             mnx_ref, bg_ref, cw_ref, wp_ref, ls_ref, mhg_ref,
             dp_ref, dcw_ref, dcb_ref, dwp_ref, dls_ref, dmhg_ref, dbg_ref,
             dc_scr, dn_scr, dapad, dpipad):
        i = pl.program_id(0)
        bi = n_blocks - 1 - i

        @pl.when(i == 0)
        def _():
            for ref in (dc_scr, dn_scr, dcw_ref, dcb_ref, dwp_ref, dls_ref, dmhg_ref, dbg_ref):
                ref[...] = jnp.zeros_like(ref)
            dapad[blk:blk + CONV_HALO, :] = jnp.zeros((CONV_HALO, 1024), F32)
            dpipad[blk:blk + POOL_HALO, :] = jnp.zeros((POOL_HALO, D_POOL), F32)

        dpooled = []
        for g, w in enumerate(POOL_WINDOWS):
            lanes = slice(g * POOL_GROUP_DIM, (g + 1) * POOL_GROUP_DIM)
            zlanes = slice(D_POOL + g * POOL_GROUP_DIM, D_POOL + (g + 1) * POOL_GROUP_DIM)
            inv = _pool_inv_count(bi * blk, blk, w)
            pb = pooled_ref[:, lanes]
            wpb = wp_ref[g].astype(BF16)
            yw = _dot(pb, wpb)
            ls = ls_ref[:, lanes]
            zp = zp_ref[:, lanes]
            sg = _sigmoid(zp)
            dpo = dmix_ref[:, lanes]
            dp_ref[:, zlanes] = (dpo * (yw * ls) * (sg * (1.0 + zp * (1.0 - sg)))).astype(BF16)
            dy = dpo * (zp * sg)
            dls_ref[:, lanes] += jnp.sum(dy * yw, axis=0, keepdims=True)
            dyw = (dy * ls).astype(BF16)
            dwp_ref[g] += _dot_tn(pb, dyw)
            dpl = _dot_nt(dyw, wpb)
            dpooled.append(dpl)
            dpipad[0:blk, lanes] = dpl * inv
        for g, w in enumerate(POOL_WINDOWS):
            lanes = slice(g * POOL_GROUP_DIM, (g + 1) * POOL_GROUP_DIM)
            du = _window_sum(dpipad[:, lanes], w, _rows_ahead)[0:blk, :] - dpooled[g]
            dp_ref[:, lanes] = du.astype(BF16)
        dpipad[blk:blk + POOL_HALO, :] = dpipad[0:POOL_HALO, :]

        a = a_ref[...]
        sga = _sigmoid(a)
        qk = a * sga
        dsilu_a = sga * (1.0 + a * (1.0 - sga))

        lane = lax.broadcasted_iota(jnp.int32, (CHUNK, 128), 1)
        row = lax.broadcasted_iota(jnp.int32, (CHUNK, 128), 0)
        scale_k = HEAD_DIM ** -0.5
        forms = [None] * per_step
        col_g_rows = [[] for _ in range(per_step)]
        dig_parts = [[] for _ in range(per_step)]
        db_parts = [[] for _ in range(per_step)]
        d_state = [[None] * N_HEADS for _ in range(per_step)]

        def state_terms(s, h, c_h, n_row, vb, kb):
            dcn, dnn = d_state[s][h]
            dcnb = dcn.astype(BF16)
            amat = _dot(vb, dcnb) + dnn
            kdc = _dot_nt(kb, dcnb)
            ddecay = (jnp.sum(jnp.sum(dcn * c_h, axis=-1, keepdims=True), axis=0, keepdims=True)
                      + jnp.sum(dnn * n_row, axis=-1, keepdims=True))
            return dcn, dnn, amat, kdc, ddecay

        def head(s, h):
            rows = slice(s * CHUNK, (s + 1) * CHUNK)
            lanes = slice(h * HEAD_DIM, (h + 1) * HEAD_DIM)
            klanes = slice(D_MLSTM + h * HEAD_DIM, D_MLSTM + (h + 1) * HEAD_DIM)
            gpre, causal, utri, bcol, gt8, brow = forms[s]
            qh = qk[rows, lanes]
            kh = qk[rows, klanes] * scale_k
            vh = v_ref[rows, lanes]
            bc = bcol[:, N_HEADS + h:N_HEADS + h + 1]
            br = brow[N_HEADS + h:N_HEADS + h + 1, :]
            igr = gt8[h:h + 1, :]
            igc = gpre[:, h:h + 1]
            m_prev = mst_ref[s, h:h + 1, 0:1]
            m_next = mnx_ref[0, h:h + 1, 0:1] if s == per_step - 1 else mst_ref[s + 1, h:h + 1, 0:1]
            c_h = cst_ref[s, h]
            n_row = nst_ref[s, h:h + 1, :]
            w_c, decay, _, last = _state_weights(bc, igc, m_prev, m_next)
            terms = None
            if s == per_step - 1:
                terms = state_terms(s, h, c_h, n_row, vh.astype(BF16), kh.astype(BF16))
            f = yield from _head_fwd(qh, kh, vh, bc, br, igr, m_prev, c_h, n_row, causal)
            qb, kb, vb, cb = f["qb"], f["kb"], f["vb"], f["cb"]
            sm, dmat, inter, den, dn, hm = f["s"], f["dmat"], f["inter"], f["den"], f["dn"], f["hm"]
            yield

            rinv = lax.rsqrt(_row_mean_mxu(hm * hm) + EPS)
            hmn = hm * rinv
            gh = mhg_ref[:, lanes]
            o_pre = o_ref[rows, lanes]
            og = _sigmoid(o_pre)
            zm = zm_ref[rows, lanes]
            sgz = _sigmoid(zm)
            sz = zm * sgz
            dout = dmix_ref[rows, D_POOL + h * HEAD_DIM:D_POOL + (h + 1) * HEAD_DIM]
            hn = hmn * gh
            dp_ref[rows, 2560 + h * HEAD_DIM:2560 + (h + 1) * HEAD_DIM] = (
                dout * hn * sz * og * (1.0 - og)).astype(BF16)
            dp_ref[rows, 3072 + h * HEAD_DIM:3072 + (h + 1) * HEAD_DIM] = (
                dout * hn * og * (sgz * (1.0 + zm * (1.0 - sgz)))).astype(BF16)
            dhn = dout * og * sz
            dmhg_ref[:, lanes] += jnp.sum(dhn * hmn, axis=0, keepdims=True)
            dyn = dhn * gh
            dhm = rinv * (dyn - hmn * _row_mean_mxu(dyn * hmn))
            yield

            inv_dn = 1.0 / dn
            dnum = dhm * inv_dn
            hd = jnp.sum(dhm * hm, axis=-1, keepdims=True)
            dden = jnp.where(jnp.abs(den) > f["emt"], -hd * inv_dn * jnp.sign(den), 0.0)
            dnb = dnum.astype(BF16)
            dnv = _dot_nt(dnb, vb)
            dv = _dot_tn(sm.astype(BF16), dnb)
            dnc = _dot(dnb, cb)
            dc_prev = _dot_tn((inter * dnum).astype(BF16), qb)
            dn_prev = jnp.sum((inter * dden) * qh, axis=0, keepdims=True)
            yield
            ds = dnv + dden
            dqk = (ds * dmat).astype(BF16)
            dqk_k = _dot(dqk, kb)
            dk = _dot_tn(dqk, qb)
            for _ in range(per_step - 1 - s):
                yield
            if terms is None:
                terms = state_terms(s, h, c_h, n_row, vb, kb)
            dcn, dnn, amat, kdc, ddecay = terms
            d_start = (decay * dcn + dc_prev, decay * dnn + dn_prev)
            if s > 0:
                d_state[s - 1][h] = d_start
            else:
                dc_scr[h] = d_start[0]
                dn_scr[h:h + 1, :] = d_start[1]
            yield
            gmat = ds * sm
            row_g = jnp.sum(gmat, axis=-1, keepdims=True)
            col_g_rows[s].append(jnp.where(row == h, jnp.sum(gmat, axis=0, keepdims=True), 0.0))
            gcol = inter * (jnp.sum(dnum * f["cq"], axis=-1, keepdims=True) + dden * f["nq"])
            dw = jnp.sum(amat * kh, axis=-1, keepdims=True)
            e = dw * w_c
            db_last = ddecay * decay + jnp.sum(e, axis=0, keepdims=True)
            dig_parts[s].append(jnp.where(lane == h, e, 0.0))
            db_parts[s].append(
                jnp.where(lane == N_HEADS + h, row_g + gcol - e + jnp.where(last, db_last, 0.0), 0.0))
            yield
            dq = dqk_k + inter * (dnc + dden * n_row)
            dp_ref[rows, 2048 + h * HEAD_DIM:2048 + (h + 1) * HEAD_DIM] = (dv + w_c * kdc).astype(BF16)
            dapad[rows, lanes] = dq * dsilu_a[rows, lanes]
            dapad[rows, klanes] = (dk + w_c * amat) * scale_k * dsilu_a[rows, klanes]

        chains = []
        for s in reversed(range(per_step)):
            gpre = g_ref[s * CHUNK:(s + 1) * CHUNK, :] + bg_ref[...]
            forms[s] = (gpre,) + _gate_forms(gpre)
            for h in range(N_HEADS):
                if s == per_step - 1:
                    d_state[s][h] = (dc_scr[h], dn_scr[h:h + 1, :])
                chains.append(head(s, h))
        _in_lockstep(chains)

        for s in range(per_step):
            rows = slice(s * CHUNK, (s + 1) * CHUNK)
            gpre, utri = forms[s][0], forms[s][2]
            cs_t = sum(col_g_rows[s][1:], col_g_rows[s][0]).T
            dig_all = sum(dig_parts[s][1:], dig_parts[s][0]) + cs_t
            db_cols = sum(db_parts[s][1:], db_parts[s][0])
            shifted = jnp.zeros((CHUNK, 128), F32)
            for h in range(N_HEADS):
                shifted = shifted + jnp.where(lane == N_HEADS + h, cs_t[:, h:h + 1], 0.0)
            dlf = _dot_f32(utri, db_cols - shifted)
            dgates = dig_all + dlf * _sigmoid(-gpre)
            dp_ref[rows, N_MAIN:N_MAIN + 128] = dgates.astype(BF16)
            dbg_ref[...] += jnp.sum(dgates, axis=0, keepdims=True)
        dp_ref[:, N_MAIN + 128:N_PAD] = jnp.zeros((blk, N_PAD - N_MAIN - 128), BF16)

        da_pad = dapad[...]
        da = da_pad[0:blk, :]
        dcb_ref[...] += jnp.sum(da, axis=0, keepdims=True)
        x = qk_ref[...]
        dx = jnp.zeros((blk, 1024), F32)
        for j in range(CONV_WIDTH):
            da_j = _rows_ahead(da_pad, CONV_WIDTH - 1 - j)[0:blk, :]
            dcw_ref[j:j + 1, :] += jnp.sum(da_j * x, axis=0, keepdims=True)
            dx = dx + cw_ref[j:j + 1, :] * da_j
        dp_ref[:, 1024:2048] = dx.astype(BF16)
        dapad[blk:blk + CONV_HALO, :] = dapad[0:CONV_HALO, :]

    bmap = lambda i: n_blocks - 1 - i
    wide = pl.BlockSpec((blk, 1024), lambda i: (bmap(i), 0))
    state = pl.BlockSpec((per_step, 8, 128), lambda i: (bmap(i), 0, 0))
    in_specs = [
        pl.BlockSpec((blk, 512), lambda i: (bmap(i), 1)),
        pl.BlockSpec((blk, 1024), lambda i: (bmap(i), 1)),
        pl.BlockSpec((blk, 512), lambda i: (bmap(i), 4)),
        pl.BlockSpec((blk, 512), lambda i: (bmap(i), 5)),
        pl.BlockSpec((blk, 512), lambda i: (bmap(i), 6)),
        pl.BlockSpec((blk, 128), lambda i: (bmap(i), 0)),
        wide, wide,
        pl.BlockSpec((blk, D_POOL), lambda i: (bmap(i), 0)),
        pl.BlockSpec((per_step, N_HEADS, HEAD_DIM, HEAD_DIM), lambda i: (bmap(i), 0, 0, 0)),
        state, state,
        pl.BlockSpec((1, 8, 128), lambda i: (jnp.minimum((bmap(i) + 1) * per_step, n_chunks - 1), 0, 0)),
        _full((1, 128)), _full((8, 1024)), _full((4, 128, 128)), _full((1, 512)), _full((1, 512))]
    return pl.pallas_call(
        body, name="mix_bwd", grid=(n_blocks,),
        out_shape=(jax.ShapeDtypeStruct((seq, N_PAD), BF16), jax.ShapeDtypeStruct((8, 1024), F32),
                   jax.ShapeDtypeStruct((1, 1024), F32), jax.ShapeDtypeStruct((4, 128, 128), F32),
                   jax.ShapeDtypeStruct((1, 512), F32), jax.ShapeDtypeStruct((1, 512), F32),
                   jax.ShapeDtypeStruct((1, 128), F32)),
        in_specs=in_specs,
        out_specs=(pl.BlockSpec((blk, N_PAD), lambda i: (bmap(i), 0)), _full((8, 1024)), _full((1, 1024)),
                   _full((4, 128, 128)), _full((1, 512)), _full((1, 512)), _full((1, 128))),
        scratch_shapes=[pltpu.VMEM((N_HEADS, HEAD_DIM, HEAD_DIM), F32), pltpu.VMEM((8, 128), F32),
                        pltpu.VMEM((blk + CONV_HALO, 1024), F32), pltpu.VMEM((blk + POOL_HALO, D_POOL), F32)],
        compiler_params=_params(("arbitrary",)),
    )(proj, proj, proj, proj, proj, gates, dmix, conv_a, pooled, cst, nst, mst, mst, bg_pad, conv_w8,
      w_pool, ls_pool, mh_g)


def _bwd_in(dproj, w_in_t, x, dx2, norm_g, scale):
    seq = x.shape[0]
    tm = min(512, seq)
    sub = min(256, tm)

    def body(dp_ref, wt_ref, x_ref, dx2_ref, ng_ref, sc_ref, gx_ref, dsh_ref, dsc_ref, dng_ref):
        @pl.when(pl.program_id(0) == 0)
        def _():
            dsh_ref[...] = jnp.zeros_like(dsh_ref)
            dsc_ref[...] = jnp.zeros_like(dsc_ref)
            dng_ref[...] = jnp.zeros_like(dng_ref)

        ng = ng_ref[...]
        one_sc = 1.0 + sc_ref[...]

        def chain(n):
            rows = slice(n * sub, (n + 1) * sub)
            dh = _dot(dp_ref[rows, :], wt_ref[...])
            yield
            xt = x_ref[rows, :]
            r = lax.rsqrt(jnp.mean(xt * xt, axis=-1, keepdims=True) + EPS)
            xn = xt * r
            dsh_ref[...] += jnp.sum(dh, axis=0, keepdims=True)
            dhxn = dh * xn
            dsc_ref[...] += jnp.sum(dhxn * ng, axis=0, keepdims=True)
            dng_ref[...] += jnp.sum(dhxn * one_sc, axis=0, keepdims=True)
            dxn = dh * (ng * one_sc)
            gx_ref[rows, :] = r * (dxn - xn * jnp.mean(dxn * xn, axis=-1, keepdims=True)) + dx2_ref[rows, :]

        _in_lockstep(chain(n) for n in range(tm // sub))

    tile = pl.BlockSpec((tm, D_MODEL), lambda i: (i, 0))
    vec = _full((1, D_MODEL))
    return pl.pallas_call(
        body, name="bwd_in", grid=(seq // tm,),
        out_shape=(jax.ShapeDtypeStruct((seq, D_MODEL), F32),) + (jax.ShapeDtypeStruct((1, D_MODEL), F32),) * 3,
        in_specs=[pl.BlockSpec((tm, N_PAD), lambda i: (i, 0)), _full((N_PAD, D_MODEL)), tile, tile, vec, vec],
        out_specs=(tile, vec, vec, vec),
        compiler_params=_params(("arbitrary",)),
    )(dproj, w_in_t, x, dx2, norm_g, scale)


def _dw_in(h_b, dproj):
    seq = h_b.shape[0]
    tk = min(4096, seq)
    tn = 768
    n_t = seq // tk

    def body(h_ref, dp_ref, dwt_ref, acc):
        t = pl.program_id(1)

        @pl.when(t == 0)
        def _():
            acc[...] = jnp.zeros_like(acc)

        acc[...] += _dot_tn(dp_ref[...], h_ref[...])

        @pl.when(t == n_t - 1)
        def _():
            dwt_ref[...] = acc[...].astype(BF16)

    return pl.pallas_call(
        body, name="dw_in", grid=(N_PAD // tn, n_t),
        out_shape=jax.ShapeDtypeStruct((N_PAD, D_MODEL), BF16),
        in_specs=[pl.BlockSpec((tk, D_MODEL), lambda j, t: (t, 0)), pl.BlockSpec((tk, tn), lambda j, t: (t, j))],
        out_specs=pl.BlockSpec((tn, D_MODEL), lambda j, t: (j, 0)),
        scratch_shapes=[pltpu.VMEM((tn, D_MODEL), F32)],
        compiler_params=_params(("arbitrary", "arbitrary")),
    )(h_b, dproj)


def _adam_update(g, w, m, v, g_ref, d_ref, m_ref, v_ref):
    mn = ADAM_B1 * m + (1.0 - ADAM_B1) * g
    vn = ADAM_B2 * v + (1.0 - ADAM_B2) * (g * g)
    m_hat = mn / (1.0 - ADAM_B1 ** ADAM_STEP)
    v_hat = vn / (1.0 - ADAM_B2 ** ADAM_STEP)
    g_ref[...] = g
    d_ref[...] = -ADAM_LR * (m_hat / (jnp.sqrt(v_hat) + ADAM_EPS) + ADAM_WD * w)
    m_ref[...] = mn
    v_ref[...] = vn


def _adam_sum(name, parts, w, m, v, row_tile, col_tile=None):
    rows, cols = w.shape
    col_tile = cols if col_tile is None else col_tile
    n_parts = parts.shape[0]

    def body(p_ref, w_ref, m_ref, v_ref, g_out, d_out, m_out, v_out):
        g = p_ref[0].astype(F32)
        for j in range(1, n_parts):
            g = g + p_ref[j].astype(F32)
        _adam_update(g, w_ref[...], m_ref[...], v_ref[...], g_out, d_out, m_out, v_out)

    tile = pl.BlockSpec((row_tile, col_tile), lambda i, j: (i, j))
    return pl.pallas_call(
        body, name=name, grid=(rows // row_tile, cols // col_tile),
        out_shape=(jax.ShapeDtypeStruct((rows, cols), F32),) * 4,
        in_specs=[pl.BlockSpec((n_parts, row_tile, col_tile), lambda i, j: (0, i, j)), tile, tile, tile],
        out_specs=(tile,) * 4,
        compiler_params=_params(("arbitrary", "arbitrary")),
    )(parts, w, m, v)


def _adam_ada(sc_all16, dmod_blk16, w, m, v):
    rows, cols = w.shape

    def body(sc_ref, dm_ref, w_ref, m_ref, v_ref, g_out, d_out, m_out, v_out):
        g = _dot_tn(sc_ref[...].astype(BF16), dm_ref[...].astype(BF16))
        _adam_update(g, w_ref[...], m_ref[...], v_ref[...], g_out, d_out, m_out, v_out)

    return pl.pallas_call(
        body, name="adam_w_ada", grid=(1,),
        out_shape=(jax.ShapeDtypeStruct((rows, cols), F32),) * 4,
        in_specs=[_full(sc_all16.shape), _full(dmod_blk16.shape)] + [_full((rows, cols))] * 3,
        out_specs=(_full((rows, cols)),) * 4,
        compiler_params=_params(("arbitrary",)),
    )(sc_all16, dmod_blk16, w, m, v)


def _adam_small(parts, loss_parts, w, m, v):
    names = list(w)
    n = len(names)

    def body(*refs):
        p_refs, loss_ref = refs[:n], refs[n]
        w_refs, m_refs, v_refs = (refs[n + 1 + k * n:n + 1 + (k + 1) * n] for k in range(3))
        outs = refs[3 * n + n + 1:]
        for a in range(n):
            g = p_refs[a][0]
            for j in range(1, N_DEV):
                g = g + p_refs[a][j]
            width = w_refs[a].shape[-1]
            if g.shape[-1] != width:
                g = g[..., 0:width]
            _adam_update(g, w_refs[a][...], m_refs[a][...], v_refs[a][...], *outs[4 * a:4 * a + 4])
        total = loss_ref[0]
        for j in range(1, N_DEV):
            total = total + loss_ref[j]
        outs[4 * n][...] = total

    args = [parts[k] for k in names] + [loss_parts] + [d[k] for d in (w, m, v) for k in names]
    out_shape = tuple(jax.ShapeDtypeStruct(w[k].shape, F32) for k in names for _ in range(4))
    out_shape += (jax.ShapeDtypeStruct(loss_parts.shape[1:], F32),)
    out = pl.pallas_call(
        body, name="adam_small", grid=(1,), out_shape=out_shape,
        in_specs=[_full(a.shape) for a in args], out_specs=tuple(_full(s.shape) for s in out_shape),
        compiler_params=_params(("arbitrary",)),
    )(*args)
    return {k: out[4 * a:4 * a + 4] for a, k in enumerate(names)}, out[4 * n]


def _local_step(x2, tgt2, shift, scale, gate, norm_g, w_in_t, w_out_b, conv_w, conv_b, w_pool, ls_pool,
                mh_norm_g, b_gates, final_g, send_dw_out=None, send_dw_in=None):
    bg_pad = jnp.pad(b_gates, ((0, 0), (0, 128 - b_gates.shape[1])))
    conv_w8 = jnp.pad(conv_w, ((0, 8 - CONV_WIDTH), (0, 0)))
    fg = final_g.reshape(1, D_MODEL)

    proj, gates, h_b = _fwd_proj(x2, norm_g, scale, shift, w_in_t)
    mix, cst, nst, mst, conv_a, pooled = _mix_fwd(proj, gates, bg_pad, conv_w8, conv_b, w_pool, ls_pool, mh_norm_g)
    dx2, dmix, dwo, dgate, dfg, loss = _out_fwd_bwd(mix, x2, tgt2, w_out_b, gate, fg)
    if send_dw_out is not None:
        bg_pad = bg_pad + send_dw_out(dwo)
    dproj, dcw8, dcb, dwp, dls, dmhg, dbg = _mix_bwd(proj, gates, dmix, conv_a, pooled, cst, nst, mst, bg_pad,
                                                      conv_w8, w_pool, ls_pool, mh_norm_g)
    dw_in_t = _dw_in(h_b, dproj)[:N_IN]
    ng_in = norm_g
    if send_dw_in is not None:
        ng_in = norm_g + send_dw_in(dw_in_t, dcw8[:CONV_WIDTH])
    gx, dsh, dsc, dng = _bwd_in(dproj, w_in_t, x2, dx2, ng_in, scale)
    return dict(loss=loss, grad_x=gx, dw_in_t=dw_in_t, dw_out=dwo, dconv_w=dcw8[:CONV_WIDTH], conv_b=dcb,
                w_pool=dwp, ls_pool=dls, mh_norm_g=dmhg, b_gates=dbg, final_g=dfg, norm_g=dng,
                dmod=jnp.concatenate([dsh, dsc, dgate], axis=1))


def kernel(x, c, norm_g, w_ada, b_ada, w_in, b_gates, conv_w, conv_b, w_pool, ls_pool, mh_norm_g, w_out, final_g, loss_target, m_norm_g, m_w_ada, m_b_ada, m_w_in, m_b_gates, m_conv_w, m_conv_b, m_w_pool, m_ls_pool, m_mh_norm_g, m_w_out, m_final_g, v_norm_g, v_w_ada, v_b_ada, v_w_in, v_b_gates, v_conv_w, v_conv_b, v_w_pool, v_ls_pool, v_mh_norm_g, v_w_out, v_final_g):
    seq = x.shape[1]
    me = 4 * lax.axis_index("x") + 2 * lax.axis_index("y") + lax.axis_index("c")

    g_in, g_out, g_cw, g_c = _all_gather("gather_weights", w_in[0].astype(BF16).T, w_out[0].astype(BF16), conv_w[0], c)
    w_in_t = jnp.pad(g_in.reshape(N_IN, D_MODEL), ((0, N_PAD - N_IN), (0, 0)))
    w_out_b = g_out.reshape(D_MODEL, D_MODEL)
    conv_w_full = jnp.transpose(g_cw, (1, 0, 2)).reshape(CONV_WIDTH, 2 * D_MLSTM)
    c_all16 = jnp.pad(g_c.reshape(N_DEV, D_MODEL), ((0, 8), (0, 0)))

    b_ada_blk = lax.dynamic_slice(b_ada, (0, me * ADA_SHARD), (1, ADA_SHARD))
    mod_all, sc_all16 = _ada_mod(c_all16, w_ada[0], b_ada_blk)
    mod = lax.dynamic_index_in_dim(mod_all, me, axis=1, keepdims=False).reshape(1, 3 * D_MODEL)
    shift, scale, gate = mod[:, :D_MODEL], mod[:, D_MODEL:2 * D_MODEL], mod[:, 2 * D_MODEL:]

    flights = {}

    def send_dw_out(dwo):
        blocks = dwo.reshape(N_DEV, D_MODEL // N_DEV, D_MODEL)
        flights["out"], token = _scatter_start("send_dw_out", (blocks,))
        return token

    def send_dw_in(dw_in_t, dcw):
        blocks = dw_in_t.reshape(N_DEV, N_SHARD, D_MODEL)
        dcw_blocks = jnp.transpose(dcw.reshape(CONV_WIDTH, N_DEV, 128), (1, 0, 2))
        flights["in"], token = _scatter_start("send_dw_in", (blocks, dcw_blocks))
        return token

    r = _local_step(x[0], loss_target[0], shift, scale, gate, norm_g, w_in_t, w_out_b, conv_w_full, conv_b,
                    w_pool[0], ls_pool, mh_norm_g, b_gates, final_g, send_dw_out, send_dw_in)

    small_names = ("norm_g", "b_ada", "b_gates", "conv_b", "w_pool", "ls_pool", "mh_norm_g", "final_g")
    small_grads = dict(norm_g=r["norm_g"], b_ada=r["dmod"], b_gates=r["b_gates"], conv_b=r["conv_b"],
                       w_pool=r["w_pool"], ls_pool=r["ls_pool"], mh_norm_g=r["mh_norm_g"], final_g=r["final_g"])
    gathered = _all_gather("gather_small", r["loss"], *(small_grads[k] for k in small_names))
    p_loss, p_small = gathered[0], dict(zip(small_names, gathered[1:]))
    p_in, p_cw = _scatter_wait("recv_dw_in", flights["in"], p_loss)
    (p_out,) = _scatter_wait("recv_dw_out", flights["out"], p_cw)

    in_t = _adam_sum("adam_w_in", p_in, w_in[0].T, m_w_in[0].T, v_w_in[0].T, N_SHARD, 256)
    gi, di, mi, vi = (o.T for o in in_t)
    go, do_, mo, vo = _adam_sum("adam_w_out", p_out, w_out[0], m_w_out[0], v_w_out[0], 128)
    gc, dc, mc, vc = _adam_sum("adam_conv_w", p_cw, conv_w[0], m_conv_w[0], v_conv_w[0], CONV_WIDTH)

    def plain(norm_g_, b_ada_, b_gates_, conv_b_, w_pool_, ls_pool_, mh_norm_g_, final_g_):
        return dict(norm_g=norm_g_, b_ada=b_ada_, b_gates=b_gates_, conv_b=conv_b_, w_pool=w_pool_[0],
                    ls_pool=ls_pool_, mh_norm_g=mh_norm_g_, final_g=final_g_.reshape(1, D_MODEL))

    small, loss_row = _adam_small(
        p_small, p_loss,
        plain(norm_g, b_ada, b_gates, conv_b, w_pool, ls_pool, mh_norm_g, final_g),
        plain(m_norm_g, m_b_ada, m_b_gates, m_conv_b, m_w_pool, m_ls_pool, m_mh_norm_g, m_final_g),
        plain(v_norm_g, v_b_ada, v_b_gates, v_conv_b, v_w_pool, v_ls_pool, v_mh_norm_g, v_final_g))

    dmod_all = p_small["b_ada"].reshape(N_DEV, 3 * D_MODEL)
    dmod_blk16 = jnp.pad(lax.dynamic_slice(dmod_all, (0, me * ADA_SHARD), (N_DEV, ADA_SHARD)), ((0, 8), (0, 0)))
    ga, da, ma, va = _adam_ada(sc_all16, dmod_blk16, w_ada[0], m_w_ada[0], v_w_ada[0])

    names = ("norm_g", "w_ada", "b_ada", "w_in", "b_gates", "conv_w", "conv_b", "w_pool", "ls_pool", "mh_norm_g",
             "w_out", "final_g")
    shapes = dict(norm_g=norm_g.shape, b_ada=b_ada.shape, b_gates=b_gates.shape, conv_b=conv_b.shape,
                  w_pool=w_pool.shape, ls_pool=ls_pool.shape, mh_norm_g=mh_norm_g.shape, final_g=final_g.shape)
    sharded = dict(w_ada=(ga, da, ma, va), w_in=(gi, di, mi, vi), conv_w=(gc, dc, mc, vc), w_out=(go, do_, mo, vo))
    outs = []
    for kind in range(4):
        for nm in names:
            if nm in sharded:
                outs.append(sharded[nm][kind][None])
            else:
                outs.append(small[nm][kind].reshape(shapes[nm]))
    loss = loss_row[0, 0]
    grad_x = r["grad_x"].reshape(1, seq, D_MODEL)
    return (loss, grad_x, *outs)
```

```python
import jax
import jax.numpy as jnp
from jax import lax
from jax.experimental import pallas as pl
from jax.experimental.pallas import tpu as pltpu

F32 = jnp.float32
BF16 = jnp.bfloat16

D_MODEL = 1024
D_POOL = 512
D_MLSTM = 512
N_HEADS = 4
HEAD_DIM = 128
CHUNK = 128
POOL_WINDOWS = (2, 4, 8, 16)
POOL_GROUP_DIM = 128
CONV_WIDTH = 4
EPS = 1e-6
N_MAIN = 3584
N_IN = 3592
N_PAD = 3840
N_SHARD = N_IN // 8
ADA_SHARD = 3 * D_MODEL // 8
N_DEV = 8
CONV_HALO = 8
POOL_HALO = 16
NEG_BIG = -1e30
VMEM_LIMIT_BYTES = 56 * 1024 * 1024

ADAM_LR = 0.001
ADAM_B1 = 0.9
ADAM_B2 = 0.999
ADAM_EPS = 1e-08
ADAM_WD = 0.01
ADAM_STEP = 10

def _dot(a, b):
    return jnp.dot(a, b, preferred_element_type=F32)


def _dot_nt(a, b):
    return lax.dot_general(a, b, (((1,), (1,)), ((), ())), preferred_element_type=F32)


def _dot_tn(a, b):
    return lax.dot_general(a, b, (((0,), (0,)), ((), ())), preferred_element_type=F32)


def _dot_f32(a, b):
    return jnp.dot(a, b, precision=lax.Precision.HIGHEST, preferred_element_type=F32)


def _row_mean_mxu(x):
    return _dot(x.astype(BF16), jnp.full((HEAD_DIM, HEAD_DIM), 1.0 / HEAD_DIM, BF16))


def _sigmoid(x):
    return jax.nn.sigmoid(x)


def _log_sigmoid(x):
    return jnp.minimum(x, 0.0) - jnp.log1p(jnp.exp(-jnp.abs(x)))


def _params(sem):
    return pltpu.CompilerParams(dimension_semantics=sem, vmem_limit_bytes=VMEM_LIMIT_BYTES)


def _full(shape):
    n = len(shape)
    return pl.BlockSpec(shape, lambda *_: (0,) * n)


def _mesh_pos():
    return lax.axis_index("x"), lax.axis_index("y"), lax.axis_index("c")


def _peer(k):
    x, y, c = _mesh_pos()
    px = 1 - x if (k >> 2) & 1 else x
    py = 1 - y if (k >> 1) & 1 else y
    pc = 1 - c if k & 1 else c
    return (px, py, pc), 4 * px + 2 * py + pc


def _remote(src, dst, send_sem, recv_sem, to):
    return pltpu.make_async_remote_copy(src_ref=src, dst_ref=dst, send_sem=send_sem, recv_sem=recv_sem, device_id=to,
                                        device_id_type=pl.DeviceIdType.MESH)


def _two_level_gather(src, dst, send_sems, recv_sems, local_sems):
    n = len(src)
    x, y, c = _mesh_pos()
    me = 4 * x + 2 * y + c
    sibling = (x, y, 1 - c)
    south = c == 0
    near = (jnp.where(south, 1 - x, x), jnp.where(south, y, 1 - y))
    far = (jnp.where(south, x, 1 - x), jnp.where(south, 1 - y, y))
    diag = (1 - x, 1 - y)

    def block_of(chip, core):
        return 4 * chip[0] + 2 * chip[1] + core

    def copy(a, k, block, to, own=False):
        return _remote(src[a] if own else dst[a].at[block], dst[a].at[block], send_sems.at[a, k], recv_sems.at[a, k], to)

    local = [pltpu.make_async_copy(src[a], dst[a].at[me], local_sems.at[a]) for a in range(n)]
    sent = [copy(a, 0, me, sibling, True) for a in range(n)]
    sent += [copy(a, 1, me, (*near, c), True) for a in range(n)]
    sent += [copy(a, 2, me, (*far, c), True) for a in range(n)]
    for cp in local + sent:
        cp.start()
    for a in range(n):
        copy(a, 1, block_of(near, c), sibling).wait_recv()
        sent += [copy(a, 3, block_of(near, c), (*far, c)), copy(a, 4, block_of(near, c), sibling)]
        sent[-2].start()
        sent[-1].start()
    for k, chip in ((2, far), (3, diag)):
        for a in range(n):
            copy(a, k, block_of(chip, c), sibling).wait_recv()
            sent.append(copy(a, 3 + k, block_of(chip, c), sibling))
            sent[-1].start()
    for k, chip in ((0, (x, y)), (4, far), (5, near), (6, diag)):
        for a in range(n):
            copy(a, k, block_of(chip, 1 - c), sibling).wait_recv()
    for cp in sent:
        cp.wait_send()
    for cp in local:
        cp.wait()


GATHER_COPIES = 7


def _all_gather(name, *shards):
    n = len(shards)

    def body(*refs):
        _two_level_gather(refs[:n], refs[n:2 * n], *refs[2 * n:])

    hbm = pl.BlockSpec(memory_space=pltpu.HBM)
    return pl.pallas_call(
        body, name=name,
        out_shape=tuple(jax.ShapeDtypeStruct((N_DEV,) + s.shape, s.dtype) for s in shards),
        in_specs=[hbm] * n, out_specs=tuple([hbm] * n),
        scratch_shapes=[pltpu.SemaphoreType.DMA((n, GATHER_COPIES)), pltpu.SemaphoreType.DMA((n, GATHER_COPIES)),
                        pltpu.SemaphoreType.DMA((n,))],
    )(*shards)


def _ada_mod(c_all16, w_ada_blk, b_ada_blk):
    def body(c_ref, w_ref, b_ref, out_ref, sc_ref, send_sems, recv_sems):
        x, y, c = _mesh_pos()
        me = 4 * x + 2 * y + c
        cv = c_ref[...]
        sc = cv * _sigmoid(cv)
        sc_ref[...] = sc
        blk = _dot(sc.astype(BF16), w_ref[...].astype(BF16)) + b_ref[...]
        out_ref[me] = blk[0:N_DEV, :]
        copies = []
        for k in range(1, N_DEV):
            peer, _ = _peer(k)
            copies.append(pltpu.make_async_remote_copy(
                src_ref=out_ref.at[me], dst_ref=out_ref.at[me], send_sem=send_sems.at[k - 1],
                recv_sem=recv_sems.at[k - 1], device_id=peer, device_id_type=pl.DeviceIdType.MESH))
        for cp in copies:
            cp.start()
        for cp in copies:
            cp.wait()

    vmem = pl.BlockSpec(memory_space=pltpu.VMEM)
    return pl.pallas_call(
        body, name="ada_mod",
        out_shape=(jax.ShapeDtypeStruct((N_DEV, N_DEV, ADA_SHARD), F32),
                   jax.ShapeDtypeStruct(c_all16.shape, F32)),
        in_specs=[vmem] * 3, out_specs=(vmem, vmem),
        scratch_shapes=[pltpu.SemaphoreType.DMA((N_DEV - 1,)), pltpu.SemaphoreType.DMA((N_DEV - 1,))],
    )(c_all16, w_ada_blk, b_ada_blk)


def _scatter_copies(src, land, send_sems, recv_sems):
    x, y, c = _mesh_pos()
    me = 4 * x + 2 * y + c
    copies = []
    for k in range(1, N_DEV):
        peer, p = _peer(k)
        for a in range(len(src)):
            i = a * (N_DEV - 1) + k - 1
            copies.append(_remote(src[a].at[p], land[a].at[me], send_sems.at[i], recv_sems.at[i], peer))
    return copies


def _scatter_start(name, blocks):
    n = len(blocks)

    def body(*refs):
        src, land = refs[:n], refs[n:2 * n]
        send_sems, recv_sems = refs[2 * n], refs[2 * n + 1]
        token_ref = refs[-1]
        for cp in _scatter_copies(src, land, send_sems, recv_sems):
            cp.start()
        token_ref[...] = jnp.zeros_like(token_ref)

    hbm = pl.BlockSpec(memory_space=pltpu.HBM)
    sem = pl.BlockSpec(memory_space=pltpu.SEMAPHORE)
    through = tuple(pltpu.HBM(b.shape, b.dtype) for b in blocks)
    args = [pltpu.with_memory_space_constraint(b, pltpu.HBM) for b in blocks]
    args += [pltpu.with_memory_space_constraint(lax.empty(b.shape, b.dtype), pltpu.HBM) for b in blocks]
    out = pl.pallas_call(
        body, name=name,
        out_shape=(pltpu.SemaphoreType.DMA((n * (N_DEV - 1),)),) * 2 + through + through
        + (jax.ShapeDtypeStruct((8, 128), F32),),
        in_specs=[hbm] * (2 * n),
        out_specs=(sem, sem) + (hbm,) * (2 * n) + (pl.BlockSpec(memory_space=pltpu.VMEM),),
        input_output_aliases={i: 2 + i for i in range(2 * n)},
        compiler_params=pltpu.CompilerParams(has_side_effects=pltpu.SideEffectType.DATAFLOW_SIDE_EFFECTING),
    )(*args)
    return out[:-1], out[-1][0:1, 0:1]


def _scatter_wait(name, state, after):
    n = (len(state) - 2) // 2
    send_sems, recv_sems = state[0], state[1]
    src, land = state[2:2 + n], state[2 + n:]

    def body(*refs):
        src_r, land_r = refs[:n], refs[n:2 * n]
        for cp in _scatter_copies(src_r, land_r, refs[2 * n], refs[2 * n + 1]):
            cp.wait_send()
            cp.wait_recv()

    hbm = pl.BlockSpec(memory_space=pltpu.HBM)
    sem = pl.BlockSpec(memory_space=pltpu.SEMAPHORE)
    out = pl.pallas_call(
        body, name=name,
        out_shape=tuple(pltpu.HBM(b.shape, b.dtype) for b in src + land),
        in_specs=[hbm] * (2 * n) + [sem, sem, pl.BlockSpec(memory_space=pl.ANY)],
        out_specs=(hbm,) * (2 * n),
        input_output_aliases={i: i for i in range(2 * n)},
        compiler_params=pltpu.CompilerParams(has_side_effects=pltpu.SideEffectType.DATAFLOW_SIDE_EFFECTING),
    )(*src, *land, send_sems, recv_sems, after)
    me = 4 * lax.axis_index("x") + 2 * lax.axis_index("y") + lax.axis_index("c")
    landed = []
    for a in range(n):
        own = lax.dynamic_index_in_dim(out[a], me, axis=0, keepdims=True)
        landed.append(lax.dynamic_update_slice_in_dim(out[n + a], own, me, axis=0))
    return landed


def _fwd_proj(x, norm_g, scale, shift, w_in_t):
    seq = x.shape[0]
    tm = min(512, seq)
    sub = min(256, tm)
    tn = 512

    def body(x_ref, ng_ref, sc_ref, sh_ref, wt_ref, proj_ref, gates_ref, h_ref):
        def chain(n):
            for _ in range(n):
                yield
            rows = slice(n * sub, (n + 1) * sub)
            xt = x_ref[rows, :]
            r = lax.rsqrt(jnp.mean(xt * xt, axis=-1, keepdims=True) + EPS)
            h = ((xt * r) * ng_ref[...]) * (1.0 + sc_ref[...]) + sh_ref[...]
            hb = h.astype(BF16)
            h_ref[rows, :] = hb
            yield
            gates_ref[rows, :] = _dot_nt(hb, wt_ref[N_MAIN:N_MAIN + 128, :])
            for j in range(N_MAIN // tn):
                proj_ref[rows, j * tn:(j + 1) * tn] = _dot_nt(hb, wt_ref[j * tn:(j + 1) * tn, :])

        _in_lockstep(chain(n) for n in range(tm // sub))

    vec = _full((1, D_MODEL))
    tile = pl.BlockSpec((tm, D_MODEL), lambda i: (i, 0))
    return pl.pallas_call(
        body, name="fwd_proj", grid=(seq // tm,),
        out_shape=(jax.ShapeDtypeStruct((seq, N_MAIN), F32), jax.ShapeDtypeStruct((seq, 128), F32),
                   jax.ShapeDtypeStruct((seq, D_MODEL), BF16)),
        in_specs=[tile, vec, vec, vec, _full((N_PAD, D_MODEL))],
        out_specs=(pl.BlockSpec((tm, N_MAIN), lambda i: (i, 0)), pl.BlockSpec((tm, 128), lambda i: (i, 0)), tile),
        compiler_params=_params(("arbitrary",)),
    )(x, norm_g, scale, shift, w_in_t)


def _gate_forms(gpre):
    r = lax.broadcasted_iota(jnp.int32, (CHUNK, CHUNK), 0)
    c = lax.broadcasted_iota(jnp.int32, (CHUNK, CHUNK), 1)
    causal = c <= r
    ltri = jnp.where(causal, 1.0, 0.0).astype(F32)
    utri = jnp.where(r <= c, 1.0, 0.0).astype(F32)
    bcol = _dot_f32(ltri, _log_sigmoid(gpre))
    gt8 = gpre.T[0:8, :]
    brow = _dot_f32(_log_sigmoid(gt8), utri)
    return causal, utri, bcol, gt8, brow


def _in_lockstep(stages):
    alive = list(stages)
    while alive:
        still = []
        for g in alive:
            try:
                next(g)
                still.append(g)
            except StopIteration:
                pass
        alive = still


def _head_fwd(qh, kh, vh, bc, br, igr, m_prev, c_h, n_row, causal):
    qb, kb, vb, cb = qh.astype(BF16), kh.astype(BF16), vh.astype(BF16), c_h.astype(BF16)
    qk = _dot_nt(qb, kb)
    cq = _dot_nt(qb, cb)
    nq = _dot_nt(qb, jnp.broadcast_to(n_row.astype(BF16), (HEAD_DIM, HEAD_DIM)))
    yield
    dlog = jnp.where(causal, bc - br + igr, NEG_BIG)
    inter_log = bc + m_prev
    m_t = jnp.maximum(inter_log, jnp.max(dlog, axis=-1, keepdims=True))
    yield
    dmat = jnp.exp(dlog - m_t)
    inter = jnp.exp(inter_log - m_t)
    s = qk * dmat
    sv = _dot(s.astype(BF16), vb)
    yield
    den = jnp.sum(s, axis=-1, keepdims=True) + inter * nq
    emt = jnp.exp(-m_t)
    yield
    num = sv + inter * cq
    dn = jnp.maximum(jnp.abs(den), emt)
    hm = num / dn
    return dict(dmat=dmat, inter=inter, qb=qb, kb=kb, vb=vb, cb=cb, s=s, cq=cq, nq=nq, den=den, emt=emt,
                dn=dn, hm=hm)


def _state_weights(bc, igc, m_prev, m_new=None):
    last = lax.broadcasted_iota(jnp.int32, (CHUNK, 1), 0) == CHUNK - 1
    b_last = jnp.sum(jnp.where(last, bc, 0.0), axis=0, keepdims=True)
    wlog = b_last - bc + igc
    if m_new is None:
        m_new = jnp.maximum(b_last + m_prev, jnp.max(wlog, axis=0, keepdims=True))
    w_c = jnp.exp(wlog - m_new)
    decay = jnp.exp(b_last + m_prev - m_new)
    return w_c, decay, m_new, last


def _rows_back(x, k):
    return x if k == 0 else pltpu.roll(x, k, 0)


def _rows_ahead(x, k):
    return x if k == 0 else pltpu.roll(x, x.shape[0] - k, 0)


def _conv_taps(xpad):
    return [_rows_back(xpad, CONV_WIDTH - 1 - j)[CONV_HALO:, :] for j in range(CONV_WIDTH)]


def _conv_pre(taps, cw_ref, cb_ref):
    a = cb_ref[...]
    for j in range(CONV_WIDTH):
        a = a + cw_ref[j:j + 1, :] * taps[j]
    return a


def _window_sum(x, w, shift):
    k = 1
    while k < w:
        x = x + shift(x, k)
        k *= 2
    return x


def _pool_window_sum(upad_ref, g, w):
    lanes = slice(g * POOL_GROUP_DIM, (g + 1) * POOL_GROUP_DIM)
    return _window_sum(upad_ref[:, lanes], w, _rows_back)[POOL_HALO:, :]


def _pool_inv_count(row0, rows, w):
    pos = row0 + lax.broadcasted_iota(jnp.int32, (rows, 1), 0) + 1
    return 1.0 / jnp.minimum(pos, w).astype(F32)


FWD_CHUNKS = 2
BWD_CHUNKS = 2


def _mix_fwd(proj, gates, bg_pad, conv_w8, conv_b, w_pool, ls_pool, mh_g):
    seq = proj.shape[0]
    n_chunks = seq // CHUNK
    per_step = FWD_CHUNKS
    blk = per_step * CHUNK

    def body(uz_ref, qk_ref, v_ref, o_ref, zm_ref, uh_ref, qkh_ref, g_ref, bg_ref, cw_ref, cb_ref, wp_ref,
             ls_ref, mhg_ref, mix_ref, cst_ref, nst_ref, mst_ref, a_ref, pooled_ref, c_scr, n_scr, m_scr, xpad, upad):
        i = pl.program_id(0)

        @pl.when(i == 0)
        def _():
            c_scr[...] = jnp.zeros_like(c_scr)
            n_scr[...] = jnp.zeros_like(n_scr)
            m_scr[...] = jnp.zeros_like(m_scr)

        first = i == 0

        upad[0:POOL_HALO, :] = jnp.where(first, 0.0, uh_ref[...])
        upad[POOL_HALO:POOL_HALO + blk, :] = uz_ref[:, 0:D_POOL]
        for g, w in enumerate(POOL_WINDOWS):
            lanes = slice(g * POOL_GROUP_DIM, (g + 1) * POOL_GROUP_DIM)
            pooled = (_pool_window_sum(upad, g, w) * _pool_inv_count(i * blk, blk, w) - uz_ref[:, lanes]).astype(BF16)
            pooled_ref[:, lanes] = pooled
            y = _dot(pooled, wp_ref[g].astype(BF16)) * ls_ref[:, lanes]
            zp = uz_ref[:, D_POOL + g * POOL_GROUP_DIM:D_POOL + (g + 1) * POOL_GROUP_DIM]
            mix_ref[:, lanes] = (y * (zp * _sigmoid(zp))).astype(BF16)

        xpad[0:CONV_HALO, :] = jnp.where(first, 0.0, qkh_ref[...])
        xpad[CONV_HALO:CONV_HALO + blk, :] = qk_ref[...]
        a = _conv_pre(_conv_taps(xpad[...]), cw_ref, cb_ref)
        a_ref[...] = a
        qk = a * _sigmoid(a)

        def head(rows, h, qh, kh, vh, bc, br, igr, m_prev, c_h, n_row, causal):
            lanes = slice(h * HEAD_DIM, (h + 1) * HEAD_DIM)
            f = yield from _head_fwd(qh, kh, vh, bc, br, igr, m_prev, c_h, n_row, causal)
            yield
            hm = f["hm"]
            hn = hm * lax.rsqrt(_row_mean_mxu(hm * hm) + EPS) * mhg_ref[:, lanes]
            zm = zm_ref[rows, lanes]
            out = hn * _sigmoid(o_ref[rows, lanes]) * (zm * _sigmoid(zm))
            mix_ref[rows, D_POOL + h * HEAD_DIM:D_POOL + (h + 1) * HEAD_DIM] = out.astype(BF16)

        c_cur = [c_scr[h] for h in range(N_HEADS)]
        n_cur = [n_scr[h:h + 1, :] for h in range(N_HEADS)]
        m_cur = [m_scr[h:h + 1, 0:1] for h in range(N_HEADS)]
        chains = []
        for s in range(per_step):
            rows = slice(s * CHUNK, (s + 1) * CHUNK)
            gpre = g_ref[rows, :] + bg_ref[...]
            causal, _, bcol, gt8, brow = _gate_forms(gpre)
            nst_ref[s] = jnp.zeros((8, 128), F32)
            mst_ref[s] = jnp.zeros((8, 128), F32)
            for h in range(N_HEADS):
                lanes = slice(h * HEAD_DIM, (h + 1) * HEAD_DIM)
                cst_ref[s, h] = c_cur[h]
                nst_ref[s, h:h + 1, :] = n_cur[h]
                mst_ref[s, h:h + 1, :] = jnp.broadcast_to(m_cur[h], (1, 128))
                qh = qk[rows, lanes]
                kh = qk[rows, D_MLSTM + h * HEAD_DIM:D_MLSTM + (h + 1) * HEAD_DIM] * (HEAD_DIM ** -0.5)
                vh = v_ref[rows, lanes]
                bc = bcol[:, N_HEADS + h:N_HEADS + h + 1]
                br = brow[N_HEADS + h:N_HEADS + h + 1, :]
                igr = gt8[h:h + 1, :]
                igc = gpre[:, h:h + 1]
                chains.append(head(rows, h, qh, kh, vh, bc, br, igr, m_cur[h], c_cur[h], n_cur[h], causal))
                w_c, decay, m_new, _ = _state_weights(bc, igc, m_cur[h])
                c_cur[h] = decay * c_cur[h] + _dot_tn((vh * w_c).astype(BF16), kh.astype(BF16))
                n_cur[h] = decay * n_cur[h] + jnp.sum(w_c * kh, axis=0, keepdims=True)
                m_cur[h] = m_new
        for h in range(N_HEADS):
            c_scr[h] = c_cur[h]
            n_scr[h:h + 1, :] = n_cur[h]
            m_scr[h:h + 1, :] = jnp.broadcast_to(m_cur[h], (1, 128))
        _in_lockstep(chains)

    in_specs = [
        pl.BlockSpec((blk, 1024), lambda i: (i, 0)),
        pl.BlockSpec((blk, 1024), lambda i: (i, 1)),
        pl.BlockSpec((blk, 512), lambda i: (i, 4)),
        pl.BlockSpec((blk, 512), lambda i: (i, 5)),
        pl.BlockSpec((blk, 512), lambda i: (i, 6)),
        pl.BlockSpec((POOL_HALO, 512), lambda i: (jnp.maximum(i * (blk // POOL_HALO) - 1, 0), 0)),
        pl.BlockSpec((CONV_HALO, 1024), lambda i: (jnp.maximum(i * (blk // CONV_HALO) - 1, 0), 1)),
        pl.BlockSpec((blk, 128), lambda i: (i, 0)),
        _full((1, 128)), _full((8, 1024)), _full((1, 1024)), _full((4, 128, 128)), _full((1, 512)),
        _full((1, 512))]
    return pl.pallas_call(
        body, name="mix_fwd", grid=(n_chunks // per_step,),
        out_shape=(jax.ShapeDtypeStruct((seq, D_MODEL), BF16),
                   jax.ShapeDtypeStruct((n_chunks, N_HEADS, HEAD_DIM, HEAD_DIM), F32),
                   jax.ShapeDtypeStruct((n_chunks, 8, 128), F32),
                   jax.ShapeDtypeStruct((n_chunks, 8, 128), F32),
                   jax.ShapeDtypeStruct((seq, 2 * D_MLSTM), F32),
                   jax.ShapeDtypeStruct((seq, D_POOL), BF16)),
        in_specs=in_specs,
        out_specs=(pl.BlockSpec((blk, D_MODEL), lambda i: (i, 0)),
                   pl.BlockSpec((per_step, N_HEADS, HEAD_DIM, HEAD_DIM), lambda i: (i, 0, 0, 0)),
                   pl.BlockSpec((per_step, 8, 128), lambda i: (i, 0, 0)),
                   pl.BlockSpec((per_step, 8, 128), lambda i: (i, 0, 0)),
                   pl.BlockSpec((blk, 2 * D_MLSTM), lambda i: (i, 0)),
                   pl.BlockSpec((blk, D_POOL), lambda i: (i, 0))),
        scratch_shapes=[pltpu.VMEM((N_HEADS, HEAD_DIM, HEAD_DIM), F32), pltpu.VMEM((8, 128), F32),
                        pltpu.VMEM((8, 128), F32), pltpu.VMEM((CONV_HALO + blk, 1024), F32),
                        pltpu.VMEM((POOL_HALO + blk, D_POOL), F32)],
        compiler_params=_params(("arbitrary",)),
    )(proj, proj, proj, proj, proj, proj, proj, gates, bg_pad, conv_w8, conv_b, w_pool, ls_pool, mh_g)


def _out_fwd_bwd(mix, x, tgt, w_out_b, gate, final_g):
    seq = x.shape[0]
    tm = min(512, seq)
    sub = min(256, tm)

    def body(mix_ref, x_ref, t_ref, w_ref, gate_ref, fg_ref, dx2_ref, dmix_ref, dwo_ref, dgate_ref, dfg_ref,
             loss_ref, dwo_acc):
        @pl.when(pl.program_id(0) == 0)
        def _():
            dwo_acc[...] = jnp.zeros_like(dwo_acc)
            dgate_ref[...] = jnp.zeros_like(dgate_ref)
            dfg_ref[...] = jnp.zeros_like(dfg_ref)
            loss_ref[...] = jnp.zeros_like(loss_ref)

        w = w_ref[...]
        gate_v = gate_ref[...]
        fg = fg_ref[...]
        do2_parts = [None] * (tm // sub)

        def chain(n):
            rows = slice(n * sub, (n + 1) * sub)
            o2 = _dot(mix_ref[rows, :], w)
            yield
            x2 = x_ref[rows, :] + gate_v * o2
            r2 = lax.rsqrt(jnp.mean(x2 * x2, axis=-1, keepdims=True) + EPS)
            x2n = x2 * r2
            err = x2n * fg - t_ref[rows, :]
            part = 0.5 * jnp.sum(jnp.sum(err * err, axis=-1, keepdims=True), axis=0, keepdims=True) / D_MODEL
            loss_ref[...] += jnp.broadcast_to(part, loss_ref.shape)
            dy = err / D_MODEL
            dfg_ref[...] += jnp.sum(dy * x2n, axis=0, keepdims=True)
            gdy = dy * fg
            dx2 = r2 * (gdy - x2n * jnp.mean(gdy * x2n, axis=-1, keepdims=True))
            dx2_ref[rows, :] = dx2
            dgate_ref[...] += jnp.sum(dx2 * o2, axis=0, keepdims=True)
            do2 = (dx2 * gate_v).astype(BF16)
            dmix_ref[rows, :] = _dot_nt(do2, w)
            do2_parts[n] = do2

        _in_lockstep(chain(n) for n in range(tm // sub))
        dwo_acc[...] += _dot_tn(mix_ref[...], jnp.concatenate(do2_parts, axis=0))

        @pl.when(pl.program_id(0) == seq // tm - 1)
        def _():
            dwo_ref[...] = dwo_acc[...].astype(BF16)

    tile = pl.BlockSpec((tm, D_MODEL), lambda i: (i, 0))
    vec = _full((1, D_MODEL))
    return pl.pallas_call(
        body, name="out_fwd_bwd", grid=(seq // tm,),
        out_shape=(jax.ShapeDtypeStruct((seq, D_MODEL), F32), jax.ShapeDtypeStruct((seq, D_MODEL), F32),
                   jax.ShapeDtypeStruct((D_MODEL, D_MODEL), BF16), jax.ShapeDtypeStruct((1, D_MODEL), F32),
                   jax.ShapeDtypeStruct((1, D_MODEL), F32), jax.ShapeDtypeStruct((1, 128), F32)),
        in_specs=[tile, tile, tile, _full((D_MODEL, D_MODEL)), vec, vec],
        out_specs=(tile, tile, _full((D_MODEL, D_MODEL)), vec, vec, _full((1, 128))),
        scratch_shapes=[pltpu.VMEM((D_MODEL, D_MODEL), F32)],
        compiler_params=_params(("arbitrary",)),
    )(mix, x, tgt, w_out_b, gate, final_g)


def _mix_bwd(proj, gates, dmix, conv_a, pooled, cst, nst, mst, bg_pad, conv_w8, w_pool, ls_pool, mh_g):
    seq = proj.shape[0]
    n_chunks = seq // CHUNK
    per_step = BWD_CHUNKS
    blk = per_step * CHUNK
    n_blocks = n_chunks // per_step

    def body(zp_ref, qk_ref, v_ref, o_ref, zm_ref, g_ref, dmix_ref, a_ref, pooled_ref, cst_ref, nst_ref, mst_ref,
             mnx_ref, bg_ref, cw_ref, wp_ref, ls_ref, mhg_ref,
             dp_ref, dcw_ref, dcb_ref, dwp_ref, dls_ref, dmhg_ref, dbg_ref,
             dc_scr, dn_scr, dapad, dpipad):
        i = pl.program_id(0)
        bi = n_blocks - 1 - i

        @pl.when(i == 0)
        def _():
            for ref in (dc_scr, dn_scr, dcw_ref, dcb_ref, dwp_ref, dls_ref, dmhg_ref, dbg_ref):
                ref[...] = jnp.zeros_like(ref)
            dapad[blk:blk + CONV_HALO, :] = jnp.zeros((CONV_HALO, 1024), F32)
            dpipad[blk:blk + POOL_HALO, :] = jnp.zeros((POOL_HALO, D_POOL), F32)

        dpooled = []
        for g, w in enumerate(POOL_WINDOWS):
            lanes = slice(g * POOL_GROUP_DIM, (g + 1) * POOL_GROUP_DIM)
            zlanes = slice(D_POOL + g * POOL_GROUP_DIM, D_POOL + (g + 1) * POOL_GROUP_DIM)
            inv = _pool_inv_count(bi * blk, blk, w)
            pb = pooled_ref[:, lanes]
            wpb = wp_ref[g].astype(BF16)
            yw = _dot(pb, wpb)
            ls = ls_ref[:, lanes]
            zp = zp_ref[:, lanes]
            sg = _sigmoid(zp)
            dpo = dmix_ref[:, lanes]
            dp_ref[:, zlanes] = (dpo * (yw * ls) * (sg * (1.0 + zp * (1.0 - sg)))).astype(BF16)
            dy = dpo * (zp * sg)
            dls_ref[:, lanes] += jnp.sum(dy * yw, axis=0, keepdims=True)
            dyw = (dy * ls).astype(BF16)
            dwp_ref[g] += _dot_tn(pb, dyw)
            dpl = _dot_nt(dyw, wpb)
            dpooled.append(dpl)
            dpipad[0:blk, lanes] = dpl * inv
        for g, w in enumerate(POOL_WINDOWS):
            lanes = slice(g * POOL_GROUP_DIM, (g + 1) * POOL_GROUP_DIM)
            du = _window_sum(dpipad[:, lanes], w, _rows_ahead)[0:blk, :] - dpooled[g]
            dp_ref[:, lanes] = du.astype(BF16)
        dpipad[blk:blk + POOL_HALO, :] = dpipad[0:POOL_HALO, :]

        def silu_and_slope(rows, cols):
            a = a_ref[rows, cols]
            sg = _sigmoid(a)
            return a * sg, sg * (1.0 + a * (1.0 - sg))

        lane = lax.broadcasted_iota(jnp.int32, (CHUNK, 128), 1)
        row = lax.broadcasted_iota(jnp.int32, (CHUNK, 128), 0)
        scale_k = HEAD_DIM ** -0.5
        forms = [None] * per_step
        col_g_rows = [[] for _ in range(per_step)]
        dig_parts = [[] for _ in range(per_step)]
        db_parts = [[] for _ in range(per_step)]
        d_state = [[None] * N_HEADS for _ in range(per_step)]

        def state_terms(s, h, c_h, n_row, vb, kb):
            dcn, dnn = d_state[s][h]
            dcnb = dcn.astype(BF16)
            amat = _dot(vb, dcnb) + dnn
            kdc = _dot_nt(kb, dcnb)
            ddecay = (jnp.sum(jnp.sum(dcn * c_h, axis=-1, keepdims=True), axis=0, keepdims=True)
                      + jnp.sum(dnn * n_row, axis=-1, keepdims=True))
            return dcn, dnn, amat, kdc, ddecay

        def head(s, h):
            rows = slice(s * CHUNK, (s + 1) * CHUNK)
            lanes = slice(h * HEAD_DIM, (h + 1) * HEAD_DIM)
            klanes = slice(D_MLSTM + h * HEAD_DIM, D_MLSTM + (h + 1) * HEAD_DIM)
            gpre, causal, utri, bcol, gt8, brow = forms[s]
            qh, dsilu_q = silu_and_slope(rows, lanes)
            kh, dsilu_k = silu_and_slope(rows, klanes)
            kh = kh * scale_k
            vh = v_ref[rows, lanes]
            bc = bcol[:, N_HEADS + h:N_HEADS + h + 1]
            br = brow[N_HEADS + h:N_HEADS + h + 1, :]
            igr = gt8[h:h + 1, :]
            igc = gpre[:, h:h + 1]
            m_prev = mst_ref[s, h:h + 1, 0:1]
            m_next = mnx_ref[0, h:h + 1, 0:1] if s == per_step - 1 else mst_ref[s + 1, h:h + 1, 0:1]
            c_h = cst_ref[s, h]
            n_row = nst_ref[s, h:h + 1, :]
            w_c, decay, _, last = _state_weights(bc, igc, m_prev, m_next)
            terms = None
            if s == per_step - 1:
                terms = state_terms(s, h, c_h, n_row, vh.astype(BF16), kh.astype(BF16))
            f = yield from _head_fwd(qh, kh, vh, bc, br, igr, m_prev, c_h, n_row, causal)
            qb, kb, vb, cb = f["qb"], f["kb"], f["vb"], f["cb"]
            sm, dmat, inter, den, dn, hm = f["s"], f["dmat"], f["inter"], f["den"], f["dn"], f["hm"]
            yield

            rinv = lax.rsqrt(_row_mean_mxu(hm * hm) + EPS)
            hmn = hm * rinv
            gh = mhg_ref[:, lanes]
            o_pre = o_ref[rows, lanes]
            og = _sigmoid(o_pre)
            zm = zm_ref[rows, lanes]
            sgz = _sigmoid(zm)
            sz = zm * sgz
            dout = dmix_ref[rows, D_POOL + h * HEAD_DIM:D_POOL + (h + 1) * HEAD_DIM]
            hn = hmn * gh
            dp_ref[rows, 2560 + h * HEAD_DIM:2560 + (h + 1) * HEAD_DIM] = (
                dout * hn * sz * og * (1.0 - og)).astype(BF16)
            dp_ref[rows, 3072 + h * HEAD_DIM:3072 + (h + 1) * HEAD_DIM] = (
                dout * hn * og * (sgz * (1.0 + zm * (1.0 - sgz)))).astype(BF16)
            dhn = dout * og * sz
            dmhg_ref[:, lanes] += jnp.sum(dhn * hmn, axis=0, keepdims=True)
            dyn = dhn * gh
            dhm = rinv * (dyn - hmn * _row_mean_mxu(dyn * hmn))
            yield

            inv_dn = 1.0 / dn
            dnum = dhm * inv_dn
            hd = jnp.sum(dhm * hm, axis=-1, keepdims=True)
            dden = jnp.where(jnp.abs(den) > f["emt"], -hd * inv_dn * jnp.sign(den), 0.0)
            dnb = dnum.astype(BF16)
            dnv = _dot_nt(dnb, vb)
            dv = _dot_tn(sm.astype(BF16), dnb)
            dnc = _dot(dnb, cb)
            dc_prev = _dot_tn((inter * dnum).astype(BF16), qb)
            dn_prev = jnp.sum((inter * dden) * qh, axis=0, keepdims=True)
            yield
            ds = dnv + dden
            dqk = (ds * dmat).astype(BF16)
            dqk_k = _dot(dqk, kb)
            dk = _dot_tn(dqk, qb)
            for _ in range(per_step - 1 - s):
                yield
            if terms is None:
                terms = state_terms(s, h, c_h, n_row, vb, kb)
            dcn, dnn, amat, kdc, ddecay = terms
            d_start = (decay * dcn + dc_prev, decay * dnn + dn_prev)
            if s > 0:
                d_state[s - 1][h] = d_start
            else:
                dc_scr[h] = d_start[0]
                dn_scr[h:h + 1, :] = d_start[1]
            yield
            gmat = ds * sm
            row_g = jnp.sum(gmat, axis=-1, keepdims=True)
            col_g_rows[s].append(jnp.where(row == h, jnp.sum(gmat, axis=0, keepdims=True), 0.0))
            gcol = inter * (jnp.sum(dnum * f["cq"], axis=-1, keepdims=True) + dden * f["nq"])
            dw = jnp.sum(amat * kh, axis=-1, keepdims=True)
            e = dw * w_c
            db_last = ddecay * decay + jnp.sum(e, axis=0, keepdims=True)
            dig_parts[s].append(jnp.where(lane == h, e, 0.0))
            db_parts[s].append(
                jnp.where(lane == N_HEADS + h, row_g + gcol - e + jnp.where(last, db_last, 0.0), 0.0))
            yield
            dq = dqk_k + inter * (dnc + dden * n_row)
            dp_ref[rows, 2048 + h * HEAD_DIM:2048 + (h + 1) * HEAD_DIM] = (dv + w_c * kdc).astype(BF16)
            dapad[rows, lanes] = dq * dsilu_q
            dapad[rows, klanes] = (dk + w_c * amat) * scale_k * dsilu_k

        chains = []
        for s in reversed(range(per_step)):
            gpre = g_ref[s * CHUNK:(s + 1) * CHUNK, :] + bg_ref[...]
            forms[s] = (gpre,) + _gate_forms(gpre)
            for h in range(N_HEADS):
                if s == per_step - 1:
                    d_state[s][h] = (dc_scr[h], dn_scr[h:h + 1, :])
                chains.append(head(s, h))
        _in_lockstep(chains)

        for s in range(per_step):
            rows = slice(s * CHUNK, (s + 1) * CHUNK)
            gpre, utri = forms[s][0], forms[s][2]
            cs_t = sum(col_g_rows[s][1:], col_g_rows[s][0]).T
            dig_all = sum(dig_parts[s][1:], dig_parts[s][0]) + cs_t
            db_cols = sum(db_parts[s][1:], db_parts[s][0])
            shifted = jnp.zeros((CHUNK, 128), F32)
            for h in range(N_HEADS):
                shifted = shifted + jnp.where(lane == N_HEADS + h, cs_t[:, h:h + 1], 0.0)
            dlf = _dot_f32(utri, db_cols - shifted)
            dgates = dig_all + dlf * _sigmoid(-gpre)
            dp_ref[rows, N_MAIN:N_MAIN + 128] = dgates.astype(BF16)
            dbg_ref[...] += jnp.sum(dgates, axis=0, keepdims=True)
        dp_ref[:, N_MAIN + 128:N_PAD] = jnp.zeros((blk, N_PAD - N_MAIN - 128), BF16)

        da_pad = dapad[...]
        da = da_pad[0:blk, :]
        dcb_ref[...] += jnp.sum(da, axis=0, keepdims=True)
        x = qk_ref[...]
        dx = jnp.zeros((blk, 1024), F32)
        for j in range(CONV_WIDTH):
            da_j = _rows_ahead(da_pad, CONV_WIDTH - 1 - j)[0:blk, :]
            dcw_ref[j:j + 1, :] += jnp.sum(da_j * x, axis=0, keepdims=True)
            dx = dx + cw_ref[j:j + 1, :] * da_j
        dp_ref[:, 1024:2048] = dx.astype(BF16)
        dapad[blk:blk + CONV_HALO, :] = dapad[0:CONV_HALO, :]

    bmap = lambda i: n_blocks - 1 - i
    wide = pl.BlockSpec((blk, 1024), lambda i: (bmap(i), 0))
    state = pl.BlockSpec((per_step, 8, 128), lambda i: (bmap(i), 0, 0))
    in_specs = [
        pl.BlockSpec((blk, 512), lambda i: (bmap(i), 1)),
        pl.BlockSpec((blk, 1024), lambda i: (bmap(i), 1)),
        pl.BlockSpec((blk, 512), lambda i: (bmap(i), 4)),
        pl.BlockSpec((blk, 512), lambda i: (bmap(i), 5)),
        pl.BlockSpec((blk, 512), lambda i: (bmap(i), 6)),
        pl.BlockSpec((blk, 128), lambda i: (bmap(i), 0)),
        wide, wide,
        pl.BlockSpec((blk, D_POOL), lambda i: (bmap(i), 0)),
        pl.BlockSpec((per_step, N_HEADS, HEAD_DIM, HEAD_DIM), lambda i: (bmap(i), 0, 0, 0)),
        state, state,
        pl.BlockSpec((1, 8, 128), lambda i: (jnp.minimum((bmap(i) + 1) * per_step, n_chunks - 1), 0, 0)),
        _full((1, 128)), _full((8, 1024)), _full((4, 128, 128)), _full((1, 512)), _full((1, 512))]
    return pl.pallas_call(
        body, name="mix_bwd", grid=(n_blocks,),
        out_shape=(jax.ShapeDtypeStruct((seq, N_PAD), BF16), jax.ShapeDtypeStruct((8, 1024), F32),
                   jax.ShapeDtypeStruct((1, 1024), F32), jax.ShapeDtypeStruct((4, 128, 128), F32),
                   jax.ShapeDtypeStruct((1, 512), F32), jax.ShapeDtypeStruct((1, 512), F32),
                   jax.ShapeDtypeStruct((1, 128), F32)),
        in_specs=in_specs,
        out_specs=(pl.BlockSpec((blk, N_PAD), lambda i: (bmap(i), 0)), _full((8, 1024)), _full((1, 1024)),
                   _full((4, 128, 128)), _full((1, 512)), _full((1, 512)), _full((1, 128))),
        scratch_shapes=[pltpu.VMEM((N_HEADS, HEAD_DIM, HEAD_DIM), F32), pltpu.VMEM((8, 128), F32),
                        pltpu.VMEM((blk + CONV_HALO, 1024), F32), pltpu.VMEM((blk + POOL_HALO, D_POOL), F32)],
        compiler_params=_params(("arbitrary",)),
    )(proj, proj, proj, proj, proj, gates, dmix, conv_a, pooled, cst, nst, mst, mst, bg_pad, conv_w8,
      w_pool, ls_pool, mh_g)


def _bwd_in(dproj, w_in_t, x, dx2, norm_g, scale):
    seq = x.shape[0]
    tm = min(512, seq)
    sub = min(256, tm)

    def body(dp_ref, wt_ref, x_ref, dx2_ref, ng_ref, sc_ref, gx_ref, dsh_ref, dsc_ref, dng_ref):
        @pl.when(pl.program_id(0) == 0)
        def _():
            dsh_ref[...] = jnp.zeros_like(dsh_ref)
            dsc_ref[...] = jnp.zeros_like(dsc_ref)
            dng_ref[...] = jnp.zeros_like(dng_ref)

        ng = ng_ref[...]
        one_sc = 1.0 + sc_ref[...]

        def chain(n):
            rows = slice(n * sub, (n + 1) * sub)
            dh = _dot(dp_ref[rows, :], wt_ref[...])
            yield
            xt = x_ref[rows, :]
            r = lax.rsqrt(jnp.mean(xt * xt, axis=-1, keepdims=True) + EPS)
            xn = xt * r
            dsh_ref[...] += jnp.sum(dh, axis=0, keepdims=True)
            dhxn = dh * xn
            dsc_ref[...] += jnp.sum(dhxn * ng, axis=0, keepdims=True)
            dng_ref[...] += jnp.sum(dhxn * one_sc, axis=0, keepdims=True)
            dxn = dh * (ng * one_sc)
            gx_ref[rows, :] = r * (dxn - xn * jnp.mean(dxn * xn, axis=-1, keepdims=True)) + dx2_ref[rows, :]

        _in_lockstep(chain(n) for n in range(tm // sub))

    tile = pl.BlockSpec((tm, D_MODEL), lambda i: (i, 0))
    vec = _full((1, D_MODEL))
    return pl.pallas_call(
        body, name="bwd_in", grid=(seq // tm,),
        out_shape=(jax.ShapeDtypeStruct((seq, D_MODEL), F32),) + (jax.ShapeDtypeStruct((1, D_MODEL), F32),) * 3,
        in_specs=[pl.BlockSpec((tm, N_PAD), lambda i: (i, 0)), _full((N_PAD, D_MODEL)), tile, tile, vec, vec],
        out_specs=(tile, vec, vec, vec),
        compiler_params=_params(("arbitrary",)),
    )(dproj, w_in_t, x, dx2, norm_g, scale)


def _dw_in(h_b, dproj):
    seq = h_b.shape[0]
    tk = min(4096, seq)
    tn = 768
    n_t = seq // tk

    def body(h_ref, dp_ref, dwt_ref, acc):
        t = pl.program_id(1)

        @pl.when(t == 0)
        def _():
            acc[...] = jnp.zeros_like(acc)

        acc[...] += _dot_tn(dp_ref[...], h_ref[...])

        @pl.when(t == n_t - 1)
        def _():
            dwt_ref[...] = acc[...].astype(BF16)

    return pl.pallas_call(
        body, name="dw_in", grid=(N_PAD // tn, n_t),
        out_shape=jax.ShapeDtypeStruct((N_PAD, D_MODEL), BF16),
        in_specs=[pl.BlockSpec((tk, D_MODEL), lambda j, t: (t, 0)), pl.BlockSpec((tk, tn), lambda j, t: (t, j))],
        out_specs=pl.BlockSpec((tn, D_MODEL), lambda j, t: (j, 0)),
        scratch_shapes=[pltpu.VMEM((tn, D_MODEL), F32)],
        compiler_params=_params(("arbitrary", "arbitrary")),
    )(h_b, dproj)


def _adam_update(g, w, m, v, g_ref, d_ref, m_ref, v_ref):
    mn = ADAM_B1 * m + (1.0 - ADAM_B1) * g
    vn = ADAM_B2 * v + (1.0 - ADAM_B2) * (g * g)
    m_hat = mn / (1.0 - ADAM_B1 ** ADAM_STEP)
    v_hat = vn / (1.0 - ADAM_B2 ** ADAM_STEP)
    g_ref[...] = g
    d_ref[...] = -ADAM_LR * (m_hat / (jnp.sqrt(v_hat) + ADAM_EPS) + ADAM_WD * w)
    m_ref[...] = mn
    v_ref[...] = vn


def _adam_sum(name, parts, w, m, v, row_tile, col_tile=None):
    rows, cols = w.shape
    col_tile = cols if col_tile is None else col_tile
    n_parts = parts.shape[0]

    def body(p_ref, w_ref, m_ref, v_ref, g_out, d_out, m_out, v_out):
        g = p_ref[0].astype(F32)
        for j in range(1, n_parts):
            g = g + p_ref[j].astype(F32)
        _adam_update(g, w_ref[...], m_ref[...], v_ref[...], g_out, d_out, m_out, v_out)

    tile = pl.BlockSpec((row_tile, col_tile), lambda i, j: (i, j))
    return pl.pallas_call(
        body, name=name, grid=(rows // row_tile, cols // col_tile),
        out_shape=(jax.ShapeDtypeStruct((rows, cols), F32),) * 4,
        in_specs=[pl.BlockSpec((n_parts, row_tile, col_tile), lambda i, j: (0, i, j)), tile, tile, tile],
        out_specs=(tile,) * 4,
        compiler_params=_params(("arbitrary", "arbitrary")),
    )(parts, w, m, v)


def _adam_ada(sc_all16, dmod_blk16, w, m, v):
    rows, cols = w.shape

    def body(sc_ref, dm_ref, w_ref, m_ref, v_ref, g_out, d_out, m_out, v_out):
        g = _dot_tn(sc_ref[...].astype(BF16), dm_ref[...].astype(BF16))
        _adam_update(g, w_ref[...], m_ref[...], v_ref[...], g_out, d_out, m_out, v_out)

    return pl.pallas_call(
        body, name="adam_w_ada", grid=(1,),
        out_shape=(jax.ShapeDtypeStruct((rows, cols), F32),) * 4,
        in_specs=[_full(sc_all16.shape), _full(dmod_blk16.shape)] + [_full((rows, cols))] * 3,
        out_specs=(_full((rows, cols)),) * 4,
        compiler_params=_params(("arbitrary",)),
    )(sc_all16, dmod_blk16, w, m, v)


def _adam_small(parts, loss_parts, w, m, v):
    names = list(w)
    n = len(names)

    def body(*refs):
        p_refs, loss_ref = refs[:n], refs[n]
        w_refs, m_refs, v_refs = (refs[n + 1 + k * n:n + 1 + (k + 1) * n] for k in range(3))
        outs = refs[3 * n + n + 1:]
        for a in range(n):
            g = p_refs[a][0]
            for j in range(1, N_DEV):
                g = g + p_refs[a][j]
            width = w_refs[a].shape[-1]
            if g.shape[-1] != width:
                g = g[..., 0:width]
            _adam_update(g, w_refs[a][...], m_refs[a][...], v_refs[a][...], *outs[4 * a:4 * a + 4])
        total = loss_ref[0]
        for j in range(1, N_DEV):
            total = total + loss_ref[j]
        outs[4 * n][...] = total

    args = [parts[k] for k in names] + [loss_parts] + [d[k] for d in (w, m, v) for k in names]
    out_shape = tuple(jax.ShapeDtypeStruct(w[k].shape, F32) for k in names for _ in range(4))
    out_shape += (jax.ShapeDtypeStruct(loss_parts.shape[1:], F32),)
    out = pl.pallas_call(
        body, name="adam_small", grid=(1,), out_shape=out_shape,
        in_specs=[_full(a.shape) for a in args], out_specs=tuple(_full(s.shape) for s in out_shape),
        compiler_params=_params(("arbitrary",)),
    )(*args)
    return {k: out[4 * a:4 * a + 4] for a, k in enumerate(names)}, out[4 * n]


def _local_step(x2, tgt2, shift, scale, gate, norm_g, w_in_t, w_out_b, conv_w, conv_b, w_pool, ls_pool,
                mh_norm_g, b_gates, final_g, send_dw_out=None, send_dw_in=None):
    bg_pad = jnp.pad(b_gates, ((0, 0), (0, 128 - b_gates.shape[1])))
    conv_w8 = jnp.pad(conv_w, ((0, 8 - CONV_WIDTH), (0, 0)))
    fg = final_g.reshape(1, D_MODEL)

    proj, gates, h_b = _fwd_proj(x2, norm_g, scale, shift, w_in_t)
    mix, cst, nst, mst, conv_a, pooled = _mix_fwd(proj, gates, bg_pad, conv_w8, conv_b, w_pool, ls_pool, mh_norm_g)
    dx2, dmix, dwo, dgate, dfg, loss = _out_fwd_bwd(mix, x2, tgt2, w_out_b, gate, fg)
    if send_dw_out is not None:
        bg_pad = bg_pad + send_dw_out(dwo)
    dproj, dcw8, dcb, dwp, dls, dmhg, dbg = _mix_bwd(proj, gates, dmix, conv_a, pooled, cst, nst, mst, bg_pad,
                                                      conv_w8, w_pool, ls_pool, mh_norm_g)
    dw_in_t = _dw_in(h_b, dproj)[:N_IN]
    ng_in = norm_g
    if send_dw_in is not None:
        ng_in = norm_g + send_dw_in(dw_in_t, dcw8[:CONV_WIDTH])
    gx, dsh, dsc, dng = _bwd_in(dproj, w_in_t, x2, dx2, ng_in, scale)
    return dict(loss=loss, grad_x=gx, dw_in_t=dw_in_t, dw_out=dwo, dconv_w=dcw8[:CONV_WIDTH], conv_b=dcb,
                w_pool=dwp, ls_pool=dls, mh_norm_g=dmhg, b_gates=dbg, final_g=dfg, norm_g=dng,
                dmod=jnp.concatenate([dsh, dsc, dgate], axis=1))


def kernel(x, c, norm_g, w_ada, b_ada, w_in, b_gates, conv_w, conv_b, w_pool, ls_pool, mh_norm_g, w_out, final_g, loss_target, m_norm_g, m_w_ada, m_b_ada, m_w_in, m_b_gates, m_conv_w, m_conv_b, m_w_pool, m_ls_pool, m_mh_norm_g, m_w_out, m_final_g, v_norm_g, v_w_ada, v_b_ada, v_w_in, v_b_gates, v_conv_w, v_conv_b, v_w_pool, v_ls_pool, v_mh_norm_g, v_w_out, v_final_g):
    seq = x.shape[1]
    me = 4 * lax.axis_index("x") + 2 * lax.axis_index("y") + lax.axis_index("c")

    g_in, g_out, g_cw, g_c = _all_gather("gather_weights", w_in[0].astype(BF16).T, w_out[0].astype(BF16), conv_w[0], c)
    w_in_t = jnp.pad(g_in.reshape(N_IN, D_MODEL), ((0, N_PAD - N_IN), (0, 0)))
    w_out_b = g_out.reshape(D_MODEL, D_MODEL)
    conv_w_full = jnp.transpose(g_cw, (1, 0, 2)).reshape(CONV_WIDTH, 2 * D_MLSTM)
    c_all16 = jnp.pad(g_c.reshape(N_DEV, D_MODEL), ((0, 8), (0, 0)))

    b_ada_blk = lax.dynamic_slice(b_ada, (0, me * ADA_SHARD), (1, ADA_SHARD))
    mod_all, sc_all16 = _ada_mod(c_all16, w_ada[0], b_ada_blk)
    mod = lax.dynamic_index_in_dim(mod_all, me, axis=1, keepdims=False).reshape(1, 3 * D_MODEL)
    shift, scale, gate = mod[:, :D_MODEL], mod[:, D_MODEL:2 * D_MODEL], mod[:, 2 * D_MODEL:]

    flights = {}

    def send_dw_out(dwo):
        blocks = dwo.reshape(N_DEV, D_MODEL // N_DEV, D_MODEL)
        flights["out"], token = _scatter_start("send_dw_out", (blocks,))
        return token

    def send_dw_in(dw_in_t, dcw):
        blocks = dw_in_t.reshape(N_DEV, N_SHARD, D_MODEL)
        dcw_blocks = jnp.transpose(dcw.reshape(CONV_WIDTH, N_DEV, 128), (1, 0, 2))
        flights["in"], token = _scatter_start("send_dw_in", (blocks, dcw_blocks))
        return token

    r = _local_step(x[0], loss_target[0], shift, scale, gate, norm_g, w_in_t, w_out_b, conv_w_full, conv_b,
                    w_pool[0], ls_pool, mh_norm_g, b_gates, final_g, send_dw_out, send_dw_in)

    small_names = ("norm_g", "b_ada", "b_gates", "conv_b", "w_pool", "ls_pool", "mh_norm_g", "final_g")
    small_grads = dict(norm_g=r["norm_g"], b_ada=r["dmod"], b_gates=r["b_gates"], conv_b=r["conv_b"],
                       w_pool=r["w_pool"], ls_pool=r["ls_pool"], mh_norm_g=r["mh_norm_g"], final_g=r["final_g"])
    gathered = _all_gather("gather_small", r["loss"], *(small_grads[k] for k in small_names))
    p_loss, p_small = gathered[0], dict(zip(small_names, gathered[1:]))
    p_in, p_cw = _scatter_wait("recv_dw_in", flights["in"], p_loss)
    (p_out,) = _scatter_wait("recv_dw_out", flights["out"], p_cw)

    in_t = _adam_sum("adam_w_in", p_in, w_in[0].T, m_w_in[0].T, v_w_in[0].T, N_SHARD, 256)
    gi, di, mi, vi = (o.T for o in in_t)
    go, do_, mo, vo = _adam_sum("adam_w_out", p_out, w_out[0], m_w_out[0], v_w_out[0], 128)
    gc, dc, mc, vc = _adam_sum("adam_conv_w", p_cw, conv_w[0], m_conv_w[0], v_conv_w[0], CONV_WIDTH)

    def plain(norm_g_, b_ada_, b_gates_, conv_b_, w_pool_, ls_pool_, mh_norm_g_, final_g_):
        return dict(norm_g=norm_g_, b_ada=b_ada_, b_gates=b_gates_, conv_b=conv_b_, w_pool=w_pool_[0],
                    ls_pool=ls_pool_, mh_norm_g=mh_norm_g_, final_g=final_g_.reshape(1, D_MODEL))

    small, loss_row = _adam_small(
        p_small, p_loss,
        plain(norm_g, b_ada, b_gates, conv_b, w_pool, ls_pool, mh_norm_g, final_g),
        plain(m_norm_g, m_b_ada, m_b_gates, m_conv_b, m_w_pool, m_ls_pool, m_mh_norm_g, m_final_g),
        plain(v_norm_g, v_b_ada, v_b_gates, v_conv_b, v_w_pool, v_ls_pool, v_mh_norm_g, v_final_g))

    dmod_all = p_small["b_ada"].reshape(N_DEV, 3 * D_MODEL)
    dmod_blk16 = jnp.pad(lax.dynamic_slice(dmod_all, (0, me * ADA_SHARD), (N_DEV, ADA_SHARD)), ((0, 8), (0, 0)))
    ga, da, ma, va = _adam_ada(sc_all16, dmod_blk16, w_ada[0], m_w_ada[0], v_w_ada[0])

    names = ("norm_g", "w_ada", "b_ada", "w_in", "b_gates", "conv_w", "conv_b", "w_pool", "ls_pool", "mh_norm_g",
             "w_out", "final_g")
    shapes = dict(norm_g=norm_g.shape, b_ada=b_ada.shape, b_gates=b_gates.shape, conv_b=conv_b.shape,
                  w_pool=w_pool.shape, ls_pool=ls_pool.shape, mh_norm_g=mh_norm_g.shape, final_g=final_g.shape)
    sharded = dict(w_ada=(ga, da, ma, va), w_in=(gi, di, mi, vi), conv_w=(gc, dc, mc, vc), w_out=(go, do_, mo, vo))
    outs = []
    for kind in range(4):
        for nm in names:
            if nm in sharded:
                outs.append(sharded[nm][kind][None])
            else:
                outs.append(small[nm][kind].reshape(shapes[nm]))
    loss = loss_row[0, 0]
    grad_x = r["grad_x"].reshape(1, seq, D_MODEL)
    return (loss, grad_x, *outs)
```

```python
import jax
import jax.numpy as jnp
from jax import lax
from jax.experimental import pallas as pl
from jax.experimental.pallas import tpu as pltpu

F32 = jnp.float32
BF16 = jnp.bfloat16

D_MODEL = 1024
D_POOL = 512
D_MLSTM = 512
N_HEADS = 4
HEAD_DIM = 128
CHUNK = 128
POOL_WINDOWS = (2, 4, 8, 16)
POOL_GROUP_DIM = 128
CONV_WIDTH = 4
EPS = 1e-6
N_MAIN = 3584
N_IN = 3592
N_PAD = 3840
N_SHARD = N_IN // 8
ADA_SHARD = 3 * D_MODEL // 8
N_DEV = 8
CONV_HALO = 8
POOL_HALO = 16
NEG_BIG = -1e30
VMEM_LIMIT_BYTES = 56 * 1024 * 1024

ADAM_LR = 0.001
ADAM_B1 = 0.9
ADAM_B2 = 0.999
ADAM_EPS = 1e-08
ADAM_WD = 0.01
ADAM_STEP = 10

def _dot(a, b):
    return jnp.dot(a, b, preferred_element_type=F32)


def _dot_nt(a, b):
    return lax.dot_general(a, b, (((1,), (1,)), ((), ())), preferred_element_type=F32)


def _dot_tn(a, b):
    return lax.dot_general(a, b, (((0,), (0,)), ((), ())), preferred_element_type=F32)


def _dot_f32(a, b):
    return jnp.dot(a, b, precision=lax.Precision.HIGHEST, preferred_element_type=F32)


def _row_mean_mxu(x):
    return _dot(x.astype(BF16), jnp.full((HEAD_DIM, HEAD_DIM), 1.0 / HEAD_DIM, BF16))


def _sigmoid(x):
    return jax.nn.sigmoid(x)


def _log_sigmoid(x):
    return jnp.minimum(x, 0.0) - jnp.log1p(jnp.exp(-jnp.abs(x)))


def _params(sem):
    return pltpu.CompilerParams(dimension_semantics=sem, vmem_limit_bytes=VMEM_LIMIT_BYTES)


def _full(shape):
    n = len(shape)
    return pl.BlockSpec(shape, lambda *_: (0,) * n)


def _mesh_pos():
    return lax.axis_index("x"), lax.axis_index("y"), lax.axis_index("c")


def _peer(k):
    x, y, c = _mesh_pos()
    px = 1 - x if (k >> 2) & 1 else x
    py = 1 - y if (k >> 1) & 1 else y
    pc = 1 - c if k & 1 else c
    return (px, py, pc), 4 * px + 2 * py + pc


def _remote(src, dst, send_sem, recv_sem, to):
    return pltpu.make_async_remote_copy(src_ref=src, dst_ref=dst, send_sem=send_sem, recv_sem=recv_sem, device_id=to,
                                        device_id_type=pl.DeviceIdType.MESH)


def _two_level_gather(src, dst, send_sems, recv_sems, local_sems):
    n = len(src)
    x, y, c = _mesh_pos()
    me = 4 * x + 2 * y + c
    sibling = (x, y, 1 - c)
    south = c == 0
    near = (jnp.where(south, 1 - x, x), jnp.where(south, y, 1 - y))
    far = (jnp.where(south, x, 1 - x), jnp.where(south, 1 - y, y))
    diag = (1 - x, 1 - y)

    def block_of(chip, core):
        return 4 * chip[0] + 2 * chip[1] + core

    def copy(a, k, block, to, own=False):
        return _remote(src[a] if own else dst[a].at[block], dst[a].at[block], send_sems.at[a, k], recv_sems.at[a, k], to)

    local = [pltpu.make_async_copy(src[a], dst[a].at[me], local_sems.at[a]) for a in range(n)]
    sent = [copy(a, 0, me, sibling, True) for a in range(n)]
    sent += [copy(a, 1, me, (*near, c), True) for a in range(n)]
    sent += [copy(a, 2, me, (*far, c), True) for a in range(n)]
    for cp in local + sent:
        cp.start()
    for a in range(n):
        copy(a, 1, block_of(near, c), sibling).wait_recv()
        sent += [copy(a, 3, block_of(near, c), (*far, c)), copy(a, 4, block_of(near, c), sibling)]
        sent[-2].start()
        sent[-1].start()
    for k, chip in ((2, far), (3, diag)):
        for a in range(n):
            copy(a, k, block_of(chip, c), sibling).wait_recv()
            sent.append(copy(a, 3 + k, block_of(chip, c), sibling))
            sent[-1].start()
    for k, chip in ((0, (x, y)), (4, far), (5, near), (6, diag)):
        for a in range(n):
            copy(a, k, block_of(chip, 1 - c), sibling).wait_recv()
    for cp in sent:
        cp.wait_send()
    for cp in local:
        cp.wait()


GATHER_COPIES = 7


def _all_gather(name, *shards):
    n = len(shards)

    def body(*refs):
        _two_level_gather(refs[:n], refs[n:2 * n], *refs[2 * n:])

    hbm = pl.BlockSpec(memory_space=pltpu.HBM)
    return pl.pallas_call(
        body, name=name,
        out_shape=tuple(jax.ShapeDtypeStruct((N_DEV,) + s.shape, s.dtype) for s in shards),
        in_specs=[hbm] * n, out_specs=tuple([hbm] * n),
        scratch_shapes=[pltpu.SemaphoreType.DMA((n, GATHER_COPIES)), pltpu.SemaphoreType.DMA((n, GATHER_COPIES)),
                        pltpu.SemaphoreType.DMA((n,))],
    )(*shards)


def _ada_mod(c_all16, w_ada_blk, b_ada_blk):
    def body(c_ref, w_ref, b_ref, out_ref, sc_ref, send_sems, recv_sems):
        x, y, c = _mesh_pos()
        me = 4 * x + 2 * y + c
        cv = c_ref[...]
        sc = cv * _sigmoid(cv)
        sc_ref[...] = sc
        blk = _dot(sc.astype(BF16), w_ref[...].astype(BF16)) + b_ref[...]
        out_ref[me] = blk[0:N_DEV, :]
        copies = []
        for k in range(1, N_DEV):
            peer, _ = _peer(k)
            copies.append(pltpu.make_async_remote_copy(
                src_ref=out_ref.at[me], dst_ref=out_ref.at[me], send_sem=send_sems.at[k - 1],
                recv_sem=recv_sems.at[k - 1], device_id=peer, device_id_type=pl.DeviceIdType.MESH))
        for cp in copies:
            cp.start()
        for cp in copies:
            cp.wait()

    vmem = pl.BlockSpec(memory_space=pltpu.VMEM)
    return pl.pallas_call(
        body, name="ada_mod",
        out_shape=(jax.ShapeDtypeStruct((N_DEV, N_DEV, ADA_SHARD), F32),
                   jax.ShapeDtypeStruct(c_all16.shape, F32)),
        in_specs=[vmem] * 3, out_specs=(vmem, vmem),
        scratch_shapes=[pltpu.SemaphoreType.DMA((N_DEV - 1,)), pltpu.SemaphoreType.DMA((N_DEV - 1,))],
    )(c_all16, w_ada_blk, b_ada_blk)


def _scatter_copies(src, land, send_sems, recv_sems, whole=False):
    x, y, c = _mesh_pos()
    me = 4 * x + 2 * y + c
    copies = []
    for k in range(1, N_DEV):
        peer, p = _peer(k)
        for a in range(len(src)):
            i = a * (N_DEV - 1) + k - 1
            copies.append(_remote(src[a] if whole else src[a].at[p], land[a].at[me], send_sems.at[i],
                                  recv_sems.at[i], peer))
    return copies


def _scatter_start(name, blocks, whole=False):
    n = len(blocks)

    def body(*refs):
        src, land = refs[:n], refs[n:2 * n]
        send_sems, recv_sems = refs[2 * n], refs[2 * n + 1]
        token_ref = refs[-1]
        for cp in _scatter_copies(src, land, send_sems, recv_sems, whole):
            cp.start()
        token_ref[...] = jnp.zeros_like(token_ref)

    hbm = pl.BlockSpec(memory_space=pltpu.HBM)
    sem = pl.BlockSpec(memory_space=pltpu.SEMAPHORE)
    landing = [((N_DEV,) + b.shape if whole else b.shape, b.dtype) for b in blocks]
    through = tuple(pltpu.HBM(b.shape, b.dtype) for b in blocks) + tuple(pltpu.HBM(s, d) for s, d in landing)
    args = [pltpu.with_memory_space_constraint(b, pltpu.HBM) for b in blocks]
    args += [pltpu.with_memory_space_constraint(lax.empty(s, d), pltpu.HBM) for s, d in landing]
    out = pl.pallas_call(
        body, name=name,
        out_shape=(pltpu.SemaphoreType.DMA((n * (N_DEV - 1),)),) * 2 + through
        + (jax.ShapeDtypeStruct((8, 128), F32),),
        in_specs=[hbm] * (2 * n),
        out_specs=(sem, sem) + (hbm,) * (2 * n) + (pl.BlockSpec(memory_space=pltpu.VMEM),),
        input_output_aliases={i: 2 + i for i in range(2 * n)},
        compiler_params=pltpu.CompilerParams(has_side_effects=pltpu.SideEffectType.DATAFLOW_SIDE_EFFECTING),
    )(*args)
    return out[:-1], out[-1][0:1, 0:1]


def _scatter_wait(name, state, after, whole=False):
    n = (len(state) - 2) // 2
    send_sems, recv_sems = state[0], state[1]
    src, land = state[2:2 + n], state[2 + n:]

    def body(*refs):
        src_r, land_r = refs[:n], refs[n:2 * n]
        for cp in _scatter_copies(src_r, land_r, refs[2 * n], refs[2 * n + 1], whole):
            cp.wait_send()
            cp.wait_recv()

    hbm = pl.BlockSpec(memory_space=pltpu.HBM)
    sem = pl.BlockSpec(memory_space=pltpu.SEMAPHORE)
    out = pl.pallas_call(
        body, name=name,
        out_shape=tuple(pltpu.HBM(b.shape, b.dtype) for b in src + land),
        in_specs=[hbm] * (2 * n) + [sem, sem, pl.BlockSpec(memory_space=pl.ANY)],
        out_specs=(hbm,) * (2 * n),
        input_output_aliases={i: i for i in range(2 * n)},
        compiler_params=pltpu.CompilerParams(has_side_effects=pltpu.SideEffectType.DATAFLOW_SIDE_EFFECTING),
    )(*src, *land, send_sems, recv_sems, after)
    me = 4 * lax.axis_index("x") + 2 * lax.axis_index("y") + lax.axis_index("c")
    landed = []
    for a in range(n):
        own = out[a][None] if whole else lax.dynamic_index_in_dim(out[a], me, axis=0, keepdims=True)
        landed.append(lax.dynamic_update_slice_in_dim(out[n + a], own, me, axis=0))
    return landed


def _fwd_proj(x, norm_g, scale, shift, w_in_t):
    seq = x.shape[0]
    tm = min(512, seq)
    sub = min(256, tm)
    tn = 512

    def body(x_ref, ng_ref, sc_ref, sh_ref, wt_ref, proj_ref, gates_ref, h_ref):
        def chain(n):
            for _ in range(n):
                yield
            rows = slice(n * sub, (n + 1) * sub)
            xt = x_ref[rows, :]
            r = lax.rsqrt(jnp.mean(xt * xt, axis=-1, keepdims=True) + EPS)
            h = ((xt * r) * ng_ref[...]) * (1.0 + sc_ref[...]) + sh_ref[...]
            hb = h.astype(BF16)
            h_ref[rows, :] = hb
            yield
            gates_ref[rows, :] = _dot_nt(hb, wt_ref[N_MAIN:N_MAIN + 128, :])
            for j in range(N_MAIN // tn):
                proj_ref[rows, j * tn:(j + 1) * tn] = _dot_nt(hb, wt_ref[j * tn:(j + 1) * tn, :])

        _in_lockstep(chain(n) for n in range(tm // sub))

    vec = _full((1, D_MODEL))
    tile = pl.BlockSpec((tm, D_MODEL), lambda i: (i, 0))
    return pl.pallas_call(
        body, name="fwd_proj", grid=(seq // tm,),
        out_shape=(jax.ShapeDtypeStruct((seq, N_MAIN), F32), jax.ShapeDtypeStruct((seq, 128), F32),
                   jax.ShapeDtypeStruct((seq, D_MODEL), BF16)),
        in_specs=[tile, vec, vec, vec, _full((N_PAD, D_MODEL))],
        out_specs=(pl.BlockSpec((tm, N_MAIN), lambda i: (i, 0)), pl.BlockSpec((tm, 128), lambda i: (i, 0)), tile),
        compiler_params=_params(("arbitrary",)),
    )(x, norm_g, scale, shift, w_in_t)


def _gate_forms(gpre):
    r = lax.broadcasted_iota(jnp.int32, (CHUNK, CHUNK), 0)
    c = lax.broadcasted_iota(jnp.int32, (CHUNK, CHUNK), 1)
    causal = c <= r
    ltri = jnp.where(causal, 1.0, 0.0).astype(F32)
    utri = jnp.where(r <= c, 1.0, 0.0).astype(F32)
    bcol = _dot_f32(ltri, _log_sigmoid(gpre))
    gt8 = gpre.T[0:8, :]
    brow = _dot_f32(_log_sigmoid(gt8), utri)
    return causal, utri, bcol, gt8, brow


def _in_lockstep(stages):
    alive = list(stages)
    while alive:
        still = []
        for g in alive:
            try:
                next(g)
                still.append(g)
            except StopIteration:
                pass
        alive = still


def _head_fwd(qh, kh, vh, bc, br, igr, m_prev, c_h, n_row, causal):
    qb, kb, vb, cb = qh.astype(BF16), kh.astype(BF16), vh.astype(BF16), c_h.astype(BF16)
    qk = _dot_nt(qb, kb)
    cq = _dot_nt(qb, cb)
    nq = _dot_nt(qb, jnp.broadcast_to(n_row.astype(BF16), (HEAD_DIM, HEAD_DIM)))
    yield
    dlog = jnp.where(causal, bc - br + igr, NEG_BIG)
    inter_log = bc + m_prev
    m_t = jnp.maximum(inter_log, jnp.max(dlog, axis=-1, keepdims=True))
    yield
    dmat = jnp.exp(dlog - m_t)
    inter = jnp.exp(inter_log - m_t)
    s = qk * dmat
    sv = _dot(s.astype(BF16), vb)
    yield
    den = jnp.sum(s, axis=-1, keepdims=True) + inter * nq
    emt = jnp.exp(-m_t)
    yield
    num = sv + inter * cq
    dn = jnp.maximum(jnp.abs(den), emt)
    hm = num / dn
    return dict(dmat=dmat, inter=inter, qb=qb, kb=kb, vb=vb, cb=cb, s=s, cq=cq, nq=nq, den=den, emt=emt,
                dn=dn, hm=hm)


def _state_weights(bc, igc, m_prev, m_new=None):
    last = lax.broadcasted_iota(jnp.int32, (CHUNK, 1), 0) == CHUNK - 1
    b_last = jnp.sum(jnp.where(last, bc, 0.0), axis=0, keepdims=True)
    wlog = b_last - bc + igc
    if m_new is None:
        m_new = jnp.maximum(b_last + m_prev, jnp.max(wlog, axis=0, keepdims=True))
    w_c = jnp.exp(wlog - m_new)
    decay = jnp.exp(b_last + m_prev - m_new)
    return w_c, decay, m_new, last


def _rows_back(x, k):
    return x if k == 0 else pltpu.roll(x, k, 0)


def _rows_ahead(x, k):
    return x if k == 0 else pltpu.roll(x, x.shape[0] - k, 0)


def _conv_taps(xpad):
    return [_rows_back(xpad, CONV_WIDTH - 1 - j)[CONV_HALO:, :] for j in range(CONV_WIDTH)]


def _conv_pre(taps, cw_ref, cb_ref):
    a = cb_ref[...]
    for j in range(CONV_WIDTH):
        a = a + cw_ref[j:j + 1, :] * taps[j]
    return a


def _window_sum(x, w, shift):
    k = 1
    while k < w:
        x = x + shift(x, k)
        k *= 2
    return x


def _pool_window_sum(upad_ref, g, w):
    lanes = slice(g * POOL_GROUP_DIM, (g + 1) * POOL_GROUP_DIM)
    return _window_sum(upad_ref[:, lanes], w, _rows_back)[POOL_HALO:, :]


def _pool_inv_count(row0, rows, w):
    pos = row0 + lax.broadcasted_iota(jnp.int32, (rows, 1), 0) + 1
    return 1.0 / jnp.minimum(pos, w).astype(F32)


FWD_CHUNKS = 2
BWD_CHUNKS = 4


def _mix_fwd(proj, gates, bg_pad, conv_w8, conv_b, w_pool, ls_pool, mh_g):
    seq = proj.shape[0]
    n_chunks = seq // CHUNK
    per_step = FWD_CHUNKS
    blk = per_step * CHUNK

    def body(uz_ref, qk_ref, v_ref, o_ref, zm_ref, uh_ref, qkh_ref, g_ref, bg_ref, cw_ref, cb_ref, wp_ref,
             ls_ref, mhg_ref, mix_ref, cst_ref, nst_ref, mst_ref, a_ref, pooled_ref, c_scr, n_scr, m_scr, xpad, upad):
        i = pl.program_id(0)

        @pl.when(i == 0)
        def _():
            c_scr[...] = jnp.zeros_like(c_scr)
            n_scr[...] = jnp.zeros_like(n_scr)
            m_scr[...] = jnp.zeros_like(m_scr)

        first = i == 0

        upad[0:POOL_HALO, :] = jnp.where(first, 0.0, uh_ref[...])
        upad[POOL_HALO:POOL_HALO + blk, :] = uz_ref[:, 0:D_POOL]
        for g, w in enumerate(POOL_WINDOWS):
            lanes = slice(g * POOL_GROUP_DIM, (g + 1) * POOL_GROUP_DIM)
            pooled = (_pool_window_sum(upad, g, w) * _pool_inv_count(i * blk, blk, w) - uz_ref[:, lanes]).astype(BF16)
            pooled_ref[:, lanes] = pooled
            y = _dot(pooled, wp_ref[g].astype(BF16)) * ls_ref[:, lanes]
            zp = uz_ref[:, D_POOL + g * POOL_GROUP_DIM:D_POOL + (g + 1) * POOL_GROUP_DIM]
            mix_ref[:, lanes] = (y * (zp * _sigmoid(zp))).astype(BF16)

        xpad[0:CONV_HALO, :] = jnp.where(first, 0.0, qkh_ref[...])
        xpad[CONV_HALO:CONV_HALO + blk, :] = qk_ref[...]
        a = _conv_pre(_conv_taps(xpad[...]), cw_ref, cb_ref)
        a_ref[...] = a
        qk = a * _sigmoid(a)

        def head(rows, h, qh, kh, vh, bc, br, igr, m_prev, c_h, n_row, causal):
            lanes = slice(h * HEAD_DIM, (h + 1) * HEAD_DIM)
            f = yield from _head_fwd(qh, kh, vh, bc, br, igr, m_prev, c_h, n_row, causal)
            yield
            hm = f["hm"]
            hn = hm * lax.rsqrt(_row_mean_mxu(hm * hm) + EPS) * mhg_ref[:, lanes]
            zm = zm_ref[rows, lanes]
            out = hn * _sigmoid(o_ref[rows, lanes]) * (zm * _sigmoid(zm))
            mix_ref[rows, D_POOL + h * HEAD_DIM:D_POOL + (h + 1) * HEAD_DIM] = out.astype(BF16)

        c_cur = [c_scr[h] for h in range(N_HEADS)]
        n_cur = [n_scr[h:h + 1, :] for h in range(N_HEADS)]
        m_cur = [m_scr[h:h + 1, 0:1] for h in range(N_HEADS)]
        chains = []
        for s in range(per_step):
            rows = slice(s * CHUNK, (s + 1) * CHUNK)
            gpre = g_ref[rows, :] + bg_ref[...]
            causal, _, bcol, gt8, brow = _gate_forms(gpre)
            nst_ref[s] = jnp.zeros((8, 128), F32)
            mst_ref[s] = jnp.zeros((8, 128), F32)
            for h in range(N_HEADS):
                lanes = slice(h * HEAD_DIM, (h + 1) * HEAD_DIM)
                cst_ref[s, h] = c_cur[h]
                nst_ref[s, h:h + 1, :] = n_cur[h]
                mst_ref[s, h:h + 1, :] = jnp.broadcast_to(m_cur[h], (1, 128))
                qh = qk[rows, lanes]
                kh = qk[rows, D_MLSTM + h * HEAD_DIM:D_MLSTM + (h + 1) * HEAD_DIM] * (HEAD_DIM ** -0.5)
                vh = v_ref[rows, lanes]
                bc = bcol[:, N_HEADS + h:N_HEADS + h + 1]
                br = brow[N_HEADS + h:N_HEADS + h + 1, :]
                igr = gt8[h:h + 1, :]
                igc = gpre[:, h:h + 1]
                chains.append(head(rows, h, qh, kh, vh, bc, br, igr, m_cur[h], c_cur[h], n_cur[h], causal))
                w_c, decay, m_new, _ = _state_weights(bc, igc, m_cur[h])
                c_cur[h] = decay * c_cur[h] + _dot_tn((vh * w_c).astype(BF16), kh.astype(BF16))
                n_cur[h] = decay * n_cur[h] + jnp.sum(w_c * kh, axis=0, keepdims=True)
                m_cur[h] = m_new
        for h in range(N_HEADS):
            c_scr[h] = c_cur[h]
            n_scr[h:h + 1, :] = n_cur[h]
            m_scr[h:h + 1, :] = jnp.broadcast_to(m_cur[h], (1, 128))
        _in_lockstep(chains)

    in_specs = [
        pl.BlockSpec((blk, 1024), lambda i: (i, 0)),
        pl.BlockSpec((blk, 1024), lambda i: (i, 1)),
        pl.BlockSpec((blk, 512), lambda i: (i, 4)),
        pl.BlockSpec((blk, 512), lambda i: (i, 5)),
        pl.BlockSpec((blk, 512), lambda i: (i, 6)),
        pl.BlockSpec((POOL_HALO, 512), lambda i: (jnp.maximum(i * (blk // POOL_HALO) - 1, 0), 0)),
        pl.BlockSpec((CONV_HALO, 1024), lambda i: (jnp.maximum(i * (blk // CONV_HALO) - 1, 0), 1)),
        pl.BlockSpec((blk, 128), lambda i: (i, 0)),
        _full((1, 128)), _full((8, 1024)), _full((1, 1024)), _full((4, 128, 128)), _full((1, 512)),
        _full((1, 512))]
    return pl.pallas_call(
        body, name="mix_fwd", grid=(n_chunks // per_step,),
        out_shape=(jax.ShapeDtypeStruct((seq, D_MODEL), BF16),
                   jax.ShapeDtypeStruct((n_chunks, N_HEADS, HEAD_DIM, HEAD_DIM), F32),
                   jax.ShapeDtypeStruct((n_chunks, 8, 128), F32),
                   jax.ShapeDtypeStruct((n_chunks, 8, 128), F32),
                   jax.ShapeDtypeStruct((seq, 2 * D_MLSTM), F32),
                   jax.ShapeDtypeStruct((seq, D_POOL), BF16)),
        in_specs=in_specs,
        out_specs=(pl.BlockSpec((blk, D_MODEL), lambda i: (i, 0)),
                   pl.BlockSpec((per_step, N_HEADS, HEAD_DIM, HEAD_DIM), lambda i: (i, 0, 0, 0)),
                   pl.BlockSpec((per_step, 8, 128), lambda i: (i, 0, 0)),
                   pl.BlockSpec((per_step, 8, 128), lambda i: (i, 0, 0)),
                   pl.BlockSpec((blk, 2 * D_MLSTM), lambda i: (i, 0)),
                   pl.BlockSpec((blk, D_POOL), lambda i: (i, 0))),
        scratch_shapes=[pltpu.VMEM((N_HEADS, HEAD_DIM, HEAD_DIM), F32), pltpu.VMEM((8, 128), F32),
                        pltpu.VMEM((8, 128), F32), pltpu.VMEM((CONV_HALO + blk, 1024), F32),
                        pltpu.VMEM((POOL_HALO + blk, D_POOL), F32)],
        compiler_params=_params(("arbitrary",)),
    )(proj, proj, proj, proj, proj, proj, proj, gates, bg_pad, conv_w8, conv_b, w_pool, ls_pool, mh_g)


def _out_fwd_bwd(mix, x, tgt, w_out_b, gate, final_g):
    seq = x.shape[0]
    tm = min(512, seq)
    sub = min(256, tm)

    def body(mix_ref, x_ref, t_ref, w_ref, gate_ref, fg_ref, dx2_ref, dmix_ref, dwo_ref, dgate_ref, dfg_ref,
             loss_ref, dwo_acc):
        @pl.when(pl.program_id(0) == 0)
        def _():
            dwo_acc[...] = jnp.zeros_like(dwo_acc)
            dgate_ref[...] = jnp.zeros_like(dgate_ref)
            dfg_ref[...] = jnp.zeros_like(dfg_ref)
            loss_ref[...] = jnp.zeros_like(loss_ref)

        w = w_ref[...]
        gate_v = gate_ref[...]
        fg = fg_ref[...]
        do2_parts = [None] * (tm // sub)

        def chain(n):
            rows = slice(n * sub, (n + 1) * sub)
            o2 = _dot(mix_ref[rows, :], w)
            yield
            x2 = x_ref[rows, :] + gate_v * o2
            r2 = lax.rsqrt(jnp.mean(x2 * x2, axis=-1, keepdims=True) + EPS)
            x2n = x2 * r2
            err = x2n * fg - t_ref[rows, :]
            part = 0.5 * jnp.sum(jnp.sum(err * err, axis=-1, keepdims=True), axis=0, keepdims=True) / D_MODEL
            loss_ref[...] += jnp.broadcast_to(part, loss_ref.shape)
            dy = err / D_MODEL
            dfg_ref[...] += jnp.sum(dy * x2n, axis=0, keepdims=True)
            gdy = dy * fg
            dx2 = r2 * (gdy - x2n * jnp.mean(gdy * x2n, axis=-1, keepdims=True))
            dx2_ref[rows, :] = dx2
            dgate_ref[...] += jnp.sum(dx2 * o2, axis=0, keepdims=True)
            do2 = (dx2 * gate_v).astype(BF16)
            dmix_ref[rows, :] = _dot_nt(do2, w)
            do2_parts[n] = do2

        _in_lockstep(chain(n) for n in range(tm // sub))
        dwo_acc[...] += _dot_tn(mix_ref[...], jnp.concatenate(do2_parts, axis=0))

        @pl.when(pl.program_id(0) == seq // tm - 1)
        def _():
            dwo_ref[...] = dwo_acc[...].astype(BF16)

    tile = pl.BlockSpec((tm, D_MODEL), lambda i: (i, 0))
    vec = _full((1, D_MODEL))
    return pl.pallas_call(
        body, name="out_fwd_bwd", grid=(seq // tm,),
        out_shape=(jax.ShapeDtypeStruct((seq, D_MODEL), F32), jax.ShapeDtypeStruct((seq, D_MODEL), F32),
                   jax.ShapeDtypeStruct((D_MODEL, D_MODEL), BF16), jax.ShapeDtypeStruct((1, D_MODEL), F32),
                   jax.ShapeDtypeStruct((1, D_MODEL), F32), jax.ShapeDtypeStruct((1, 128), F32)),
        in_specs=[tile, tile, tile, _full((D_MODEL, D_MODEL)), vec, vec],
        out_specs=(tile, tile, _full((D_MODEL, D_MODEL)), vec, vec, _full((1, 128))),
        scratch_shapes=[pltpu.VMEM((D_MODEL, D_MODEL), F32)],
        compiler_params=_params(("arbitrary",)),
    )(mix, x, tgt, w_out_b, gate, final_g)


def _mix_bwd(proj, gates, dmix, conv_a, pooled, cst, nst, mst, bg_pad, conv_w8, w_pool, ls_pool, mh_g):
    seq = proj.shape[0]
    n_chunks = seq // CHUNK
    per_step = BWD_CHUNKS
    blk = per_step * CHUNK
    n_blocks = n_chunks // per_step

    def body(zp_ref, qk_ref, v_ref, o_ref, zm_ref, g_ref, dmix_ref, a_ref, pooled_ref, cst_ref, nst_ref, mst_ref,
             mnx_ref, bg_ref, cw_ref, wp_ref, ls_ref, mhg_ref,
             dp_ref, dcw_ref, dcb_ref, dwp_ref, dls_ref, dmhg_ref, dbg_ref,
             dc_scr, dn_scr, dapad, dpipad):
        i = pl.program_id(0)
        bi = n_blocks - 1 - i

        @pl.when(i == 0)
        def _():
            for ref in (dc_scr, dn_scr, dcw_ref, dcb_ref, dwp_ref, dls_ref, dmhg_ref, dbg_ref):
                ref[...] = jnp.zeros_like(ref)
            dapad[blk:blk + CONV_HALO, :] = jnp.zeros((CONV_HALO, 1024), F32)
            dpipad[blk:blk + POOL_HALO, :] = jnp.zeros((POOL_HALO, D_POOL), F32)

        dpooled = []
        for g, w in enumerate(POOL_WINDOWS):
            lanes = slice(g * POOL_GROUP_DIM, (g + 1) * POOL_GROUP_DIM)
            zlanes = slice(D_POOL + g * POOL_GROUP_DIM, D_POOL + (g + 1) * POOL_GROUP_DIM)
            inv = _pool_inv_count(bi * blk, blk, w)
            pb = pooled_ref[:, lanes]
            wpb = wp_ref[g].astype(BF16)
            yw = _dot(pb, wpb)
            ls = ls_ref[:, lanes]
            zp = zp_ref[:, lanes]
            sg = _sigmoid(zp)
            dpo = dmix_ref[:, lanes]
            dp_ref[:, zlanes] = (dpo * (yw * ls) * (sg * (1.0 + zp * (1.0 - sg)))).astype(BF16)
            dy = dpo * (zp * sg)
            dls_ref[:, lanes] += jnp.sum(dy * yw, axis=0, keepdims=True)
            dyw = (dy * ls).astype(BF16)
            dwp_ref[g] += _dot_tn(pb, dyw)
            dpl = _dot_nt(dyw, wpb)
            dpooled.append(dpl)
            dpipad[0:blk, lanes] = dpl * inv
        for g, w in enumerate(POOL_WINDOWS):
            lanes = slice(g * POOL_GROUP_DIM, (g + 1) * POOL_GROUP_DIM)
            du = _window_sum(dpipad[:, lanes], w, _rows_ahead)[0:blk, :] - dpooled[g]
            dp_ref[:, lanes] = du.astype(BF16)
        dpipad[blk:blk + POOL_HALO, :] = dpipad[0:POOL_HALO, :]

        def silu_and_slope(rows, cols):
            a = a_ref[rows, cols]
            sg = _sigmoid(a)
            return a * sg, sg * (1.0 + a * (1.0 - sg))

        lane = lax.broadcasted_iota(jnp.int32, (CHUNK, 128), 1)
        row = lax.broadcasted_iota(jnp.int32, (CHUNK, 128), 0)
        scale_k = HEAD_DIM ** -0.5
        forms = [None] * per_step
        col_g_rows = [[] for _ in range(per_step)]
        dig_parts = [[] for _ in range(per_step)]
        db_parts = [[] for _ in range(per_step)]
        d_state = [[None] * N_HEADS for _ in range(per_step)]

        def state_terms(s, h, c_h, n_row, vb, kb):
            dcn, dnn = d_state[s][h]
            dcnb = dcn.astype(BF16)
            amat = _dot(vb, dcnb) + dnn
            kdc = _dot_nt(kb, dcnb)
            ddecay = (jnp.sum(jnp.sum(dcn * c_h, axis=-1, keepdims=True), axis=0, keepdims=True)
                      + jnp.sum(dnn * n_row, axis=-1, keepdims=True))
            return dcn, dnn, amat, kdc, ddecay

        def head(s, h):
            rows = slice(s * CHUNK, (s + 1) * CHUNK)
            lanes = slice(h * HEAD_DIM, (h + 1) * HEAD_DIM)
            klanes = slice(D_MLSTM + h * HEAD_DIM, D_MLSTM + (h + 1) * HEAD_DIM)
            gpre, causal, utri, bcol, gt8, brow = forms[s]
            qh, dsilu_q = silu_and_slope(rows, lanes)
            kh, dsilu_k = silu_and_slope(rows, klanes)
            kh = kh * scale_k
            vh = v_ref[rows, lanes]
            bc = bcol[:, N_HEADS + h:N_HEADS + h + 1]
            br = brow[N_HEADS + h:N_HEADS + h + 1, :]
            igr = gt8[h:h + 1, :]
            igc = gpre[:, h:h + 1]
            m_prev = mst_ref[s, h:h + 1, 0:1]
            m_next = mnx_ref[0, h:h + 1, 0:1] if s == per_step - 1 else mst_ref[s + 1, h:h + 1, 0:1]
            c_h = cst_ref[s, h]
            n_row = nst_ref[s, h:h + 1, :]
            w_c, decay, _, last = _state_weights(bc, igc, m_prev, m_next)
            terms = None
            if s == per_step - 1:
                terms = state_terms(s, h, c_h, n_row, vh.astype(BF16), kh.astype(BF16))
            f = yield from _head_fwd(qh, kh, vh, bc, br, igr, m_prev, c_h, n_row, causal)
            qb, kb, vb, cb = f["qb"], f["kb"], f["vb"], f["cb"]
            sm, dmat, inter, den, dn, hm = f["s"], f["dmat"], f["inter"], f["den"], f["dn"], f["hm"]
            yield

            rinv = lax.rsqrt(_row_mean_mxu(hm * hm) + EPS)
            hmn = hm * rinv
            gh = mhg_ref[:, lanes]
            o_pre = o_ref[rows, lanes]
            og = _sigmoid(o_pre)
            zm = zm_ref[rows, lanes]
            sgz = _sigmoid(zm)
            sz = zm * sgz
            dout = dmix_ref[rows, D_POOL + h * HEAD_DIM:D_POOL + (h + 1) * HEAD_DIM]
            hn = hmn * gh
            dp_ref[rows, 2560 + h * HEAD_DIM:2560 + (h + 1) * HEAD_DIM] = (
                dout * hn * sz * og * (1.0 - og)).astype(BF16)
            dp_ref[rows, 3072 + h * HEAD_DIM:3072 + (h + 1) * HEAD_DIM] = (
                dout * hn * og * (sgz * (1.0 + zm * (1.0 - sgz)))).astype(BF16)
            dhn = dout * og * sz
            dmhg_ref[:, lanes] += jnp.sum(dhn * hmn, axis=0, keepdims=True)
            dyn = dhn * gh
            dhm = rinv * (dyn - hmn * _row_mean_mxu(dyn * hmn))
            yield

            inv_dn = 1.0 / dn
            dnum = dhm * inv_dn
            hd = jnp.sum(dhm * hm, axis=-1, keepdims=True)
            dden = jnp.where(jnp.abs(den) > f["emt"], -hd * inv_dn * jnp.sign(den), 0.0)
            dnb = dnum.astype(BF16)
            dnv = _dot_nt(dnb, vb)
            dv = _dot_tn(sm.astype(BF16), dnb)
            dnc = _dot(dnb, cb)
            dc_prev = _dot_tn((inter * dnum).astype(BF16), qb)
            dn_prev = jnp.sum((inter * dden) * qh, axis=0, keepdims=True)
            yield
            ds = dnv + dden
            dqk = (ds * dmat).astype(BF16)
            dqk_k = _dot(dqk, kb)
            dk = _dot_tn(dqk, qb)
            for _ in range(per_step - 1 - s):
                yield
            if terms is None:
                terms = state_terms(s, h, c_h, n_row, vb, kb)
            dcn, dnn, amat, kdc, ddecay = terms
            d_start = (decay * dcn + dc_prev, decay * dnn + dn_prev)
            if s > 0:
                d_state[s - 1][h] = d_start
            else:
                dc_scr[h] = d_start[0]
                dn_scr[h:h + 1, :] = d_start[1]
            yield
            gmat = ds * sm
            row_g = jnp.sum(gmat, axis=-1, keepdims=True)
            col_g_rows[s].append(jnp.where(row == h, jnp.sum(gmat, axis=0, keepdims=True), 0.0))
            gcol = inter * (jnp.sum(dnum * f["cq"], axis=-1, keepdims=True) + dden * f["nq"])
            dw = jnp.sum(amat * kh, axis=-1, keepdims=True)
            e = dw * w_c
            db_last = ddecay * decay + jnp.sum(e, axis=0, keepdims=True)
            dig_parts[s].append(jnp.where(lane == h, e, 0.0))
            db_parts[s].append(
                jnp.where(lane == N_HEADS + h, row_g + gcol - e + jnp.where(last, db_last, 0.0), 0.0))
            yield
            dq = dqk_k + inter * (dnc + dden * n_row)
            dp_ref[rows, 2048 + h * HEAD_DIM:2048 + (h + 1) * HEAD_DIM] = (dv + w_c * kdc).astype(BF16)
            dapad[rows, lanes] = dq * dsilu_q
            dapad[rows, klanes] = (dk + w_c * amat) * scale_k * dsilu_k

        chains = []
        for s in reversed(range(per_step)):
            gpre = g_ref[s * CHUNK:(s + 1) * CHUNK, :] + bg_ref[...]
            forms[s] = (gpre,) + _gate_forms(gpre)
            for h in range(N_HEADS):
                if s == per_step - 1:
                    d_state[s][h] = (dc_scr[h], dn_scr[h:h + 1, :])
                chains.append(head(s, h))
        _in_lockstep(chains)

        for s in range(per_step):
            rows = slice(s * CHUNK, (s + 1) * CHUNK)
            gpre, utri = forms[s][0], forms[s][2]
            cs_t = sum(col_g_rows[s][1:], col_g_rows[s][0]).T
            dig_all = sum(dig_parts[s][1:], dig_parts[s][0]) + cs_t
            db_cols = sum(db_parts[s][1:], db_parts[s][0])
            shifted = jnp.zeros((CHUNK, 128), F32)
            for h in range(N_HEADS):
                shifted = shifted + jnp.where(lane == N_HEADS + h, cs_t[:, h:h + 1], 0.0)
            dlf = _dot_f32(utri, db_cols - shifted)
            dgates = dig_all + dlf * _sigmoid(-gpre)
            dp_ref[rows, N_MAIN:N_MAIN + 128] = dgates.astype(BF16)
            dbg_ref[...] += jnp.sum(dgates, axis=0, keepdims=True)
        dp_ref[:, N_MAIN + 128:N_PAD] = jnp.zeros((blk, N_PAD - N_MAIN - 128), BF16)

        da_pad = dapad[...]
        da = da_pad[0:blk, :]
        dcb_ref[...] += jnp.sum(da, axis=0, keepdims=True)
        x = qk_ref[...]
        dx = jnp.zeros((blk, 1024), F32)
        for j in range(CONV_WIDTH):
            da_j = _rows_ahead(da_pad, CONV_WIDTH - 1 - j)[0:blk, :]
            dcw_ref[j:j + 1, :] += jnp.sum(da_j * x, axis=0, keepdims=True)
            dx = dx + cw_ref[j:j + 1, :] * da_j
        dp_ref[:, 1024:2048] = dx.astype(BF16)
        dapad[blk:blk + CONV_HALO, :] = dapad[0:CONV_HALO, :]

    bmap = lambda i: n_blocks - 1 - i
    wide = pl.BlockSpec((blk, 1024), lambda i: (bmap(i), 0))
    state = pl.BlockSpec((per_step, 8, 128), lambda i: (bmap(i), 0, 0))
    in_specs = [
        pl.BlockSpec((blk, 512), lambda i: (bmap(i), 1)),
        pl.BlockSpec((blk, 1024), lambda i: (bmap(i), 1)),
        pl.BlockSpec((blk, 512), lambda i: (bmap(i), 4)),
        pl.BlockSpec((blk, 512), lambda i: (bmap(i), 5)),
        pl.BlockSpec((blk, 512), lambda i: (bmap(i), 6)),
        pl.BlockSpec((blk, 128), lambda i: (bmap(i), 0)),
        wide, wide,
        pl.BlockSpec((blk, D_POOL), lambda i: (bmap(i), 0)),
        pl.BlockSpec((per_step, N_HEADS, HEAD_DIM, HEAD_DIM), lambda i: (bmap(i), 0, 0, 0)),
        state, state,
        pl.BlockSpec((1, 8, 128), lambda i: (jnp.minimum((bmap(i) + 1) * per_step, n_chunks - 1), 0, 0)),
        _full((1, 128)), _full((8, 1024)), _full((4, 128, 128)), _full((1, 512)), _full((1, 512))]
    return pl.pallas_call(
        body, name="mix_bwd", grid=(n_blocks,),
        out_shape=(jax.ShapeDtypeStruct((seq, N_PAD), BF16), jax.ShapeDtypeStruct((8, 1024), F32),
                   jax.ShapeDtypeStruct((1, 1024), F32), jax.ShapeDtypeStruct((4, 128, 128), F32),
                   jax.ShapeDtypeStruct((1, 512), F32), jax.ShapeDtypeStruct((1, 512), F32),
                   jax.ShapeDtypeStruct((1, 128), F32)),
        in_specs=in_specs,
        out_specs=(pl.BlockSpec((blk, N_PAD), lambda i: (bmap(i), 0)), _full((8, 1024)), _full((1, 1024)),
                   _full((4, 128, 128)), _full((1, 512)), _full((1, 512)), _full((1, 128))),
        scratch_shapes=[pltpu.VMEM((N_HEADS, HEAD_DIM, HEAD_DIM), F32), pltpu.VMEM((8, 128), F32),
                        pltpu.VMEM((blk + CONV_HALO, 1024), F32), pltpu.VMEM((blk + POOL_HALO, D_POOL), F32)],
        compiler_params=_params(("arbitrary",)),
    )(proj, proj, proj, proj, proj, gates, dmix, conv_a, pooled, cst, nst, mst, mst, bg_pad, conv_w8,
      w_pool, ls_pool, mh_g)


def _bwd_in(dproj, w_in_t, x, dx2, norm_g, scale):
    seq = x.shape[0]
    tm = min(512, seq)
    sub = min(256, tm)

    def body(dp_ref, wt_ref, x_ref, dx2_ref, ng_ref, sc_ref, gx_ref, dsh_ref, dsc_ref, dng_ref):
        @pl.when(pl.program_id(0) == 0)
        def _():
            dsh_ref[...] = jnp.zeros_like(dsh_ref)
            dsc_ref[...] = jnp.zeros_like(dsc_ref)
            dng_ref[...] = jnp.zeros_like(dng_ref)

        ng = ng_ref[...]
        one_sc = 1.0 + sc_ref[...]

        def chain(n):
            rows = slice(n * sub, (n + 1) * sub)
            dh = _dot(dp_ref[rows, :], wt_ref[...])
            yield
            xt = x_ref[rows, :]
            r = lax.rsqrt(jnp.mean(xt * xt, axis=-1, keepdims=True) + EPS)
            xn = xt * r
            dsh_ref[...] += jnp.sum(dh, axis=0, keepdims=True)
            dhxn = dh * xn
            dsc_ref[...] += jnp.sum(dhxn * ng, axis=0, keepdims=True)
            dng_ref[...] += jnp.sum(dhxn * one_sc, axis=0, keepdims=True)
            dxn = dh * (ng * one_sc)
            gx_ref[rows, :] = r * (dxn - xn * jnp.mean(dxn * xn, axis=-1, keepdims=True)) + dx2_ref[rows, :]

        _in_lockstep(chain(n) for n in range(tm // sub))

    tile = pl.BlockSpec((tm, D_MODEL), lambda i: (i, 0))
    vec = _full((1, D_MODEL))
    return pl.pallas_call(
        body, name="bwd_in", grid=(seq // tm,),
        out_shape=(jax.ShapeDtypeStruct((seq, D_MODEL), F32),) + (jax.ShapeDtypeStruct((1, D_MODEL), F32),) * 3,
        in_specs=[pl.BlockSpec((tm, N_PAD), lambda i: (i, 0)), _full((N_PAD, D_MODEL)), tile, tile, vec, vec],
        out_specs=(tile, vec, vec, vec),
        compiler_params=_params(("arbitrary",)),
    )(dproj, w_in_t, x, dx2, norm_g, scale)


def _dw_in(h_b, dproj):
    seq = h_b.shape[0]
    tk = min(4096, seq)
    tn = 768
    n_t = seq // tk

    def body(h_ref, dp_ref, dwt_ref, acc):
        t = pl.program_id(1)

        @pl.when(t == 0)
        def _():
            acc[...] = jnp.zeros_like(acc)

        acc[...] += _dot_tn(dp_ref[...], h_ref[...])

        @pl.when(t == n_t - 1)
        def _():
            dwt_ref[...] = acc[...].astype(BF16)

    return pl.pallas_call(
        body, name="dw_in", grid=(N_PAD // tn, n_t),
        out_shape=jax.ShapeDtypeStruct((N_PAD, D_MODEL), BF16),
        in_specs=[pl.BlockSpec((tk, D_MODEL), lambda j, t: (t, 0)), pl.BlockSpec((tk, tn), lambda j, t: (t, j))],
        out_specs=pl.BlockSpec((tn, D_MODEL), lambda j, t: (j, 0)),
        scratch_shapes=[pltpu.VMEM((tn, D_MODEL), F32)],
        compiler_params=_params(("arbitrary", "arbitrary")),
    )(h_b, dproj)


def _adam_update(g, w, m, v, g_ref, d_ref, m_ref, v_ref):
    mn = ADAM_B1 * m + (1.0 - ADAM_B1) * g
    vn = ADAM_B2 * v + (1.0 - ADAM_B2) * (g * g)
    m_hat = mn / (1.0 - ADAM_B1 ** ADAM_STEP)
    v_hat = vn / (1.0 - ADAM_B2 ** ADAM_STEP)
    g_ref[...] = g
    d_ref[...] = -ADAM_LR * (m_hat / (jnp.sqrt(v_hat) + ADAM_EPS) + ADAM_WD * w)
    m_ref[...] = mn
    v_ref[...] = vn


def _adam_sum(name, parts, w, m, v, row_tile, col_tile=None):
    rows, cols = w.shape
    col_tile = cols if col_tile is None else col_tile
    n_parts = parts.shape[0]

    def body(p_ref, w_ref, m_ref, v_ref, g_out, d_out, m_out, v_out):
        g = p_ref[0].astype(F32)
        for j in range(1, n_parts):
            g = g + p_ref[j].astype(F32)
        _adam_update(g, w_ref[...], m_ref[...], v_ref[...], g_out, d_out, m_out, v_out)

    tile = pl.BlockSpec((row_tile, col_tile), lambda i, j: (i, j))
    return pl.pallas_call(
        body, name=name, grid=(rows // row_tile, cols // col_tile),
        out_shape=(jax.ShapeDtypeStruct((rows, cols), F32),) * 4,
        in_specs=[pl.BlockSpec((n_parts, row_tile, col_tile), lambda i, j: (0, i, j)), tile, tile, tile],
        out_specs=(tile,) * 4,
        compiler_params=_params(("arbitrary", "arbitrary")),
    )(parts, w, m, v)


def _adam_ada(sc_all16, dmod_blk16, w, m, v):
    rows, cols = w.shape

    def body(sc_ref, dm_ref, w_ref, m_ref, v_ref, g_out, d_out, m_out, v_out):
        g = _dot_tn(sc_ref[...].astype(BF16), dm_ref[...].astype(BF16))
        _adam_update(g, w_ref[...], m_ref[...], v_ref[...], g_out, d_out, m_out, v_out)

    return pl.pallas_call(
        body, name="adam_w_ada", grid=(1,),
        out_shape=(jax.ShapeDtypeStruct((rows, cols), F32),) * 4,
        in_specs=[_full(sc_all16.shape), _full(dmod_blk16.shape)] + [_full((rows, cols))] * 3,
        out_specs=(_full((rows, cols)),) * 4,
        compiler_params=_params(("arbitrary",)),
    )(sc_all16, dmod_blk16, w, m, v)


def _adam_small(parts, loss_parts, w, m, v):
    names = list(w)
    n = len(names)

    def body(*refs):
        p_refs, loss_ref = refs[:n], refs[n]
        w_refs, m_refs, v_refs = (refs[n + 1 + k * n:n + 1 + (k + 1) * n] for k in range(3))
        outs = refs[3 * n + n + 1:]
        for a in range(n):
            g = p_refs[a][0]
            for j in range(1, N_DEV):
                g = g + p_refs[a][j]
            width = w_refs[a].shape[-1]
            if g.shape[-1] != width:
                g = g[..., 0:width]
            _adam_update(g, w_refs[a][...], m_refs[a][...], v_refs[a][...], *outs[4 * a:4 * a + 4])
        total = loss_ref[0]
        for j in range(1, N_DEV):
            total = total + loss_ref[j]
        outs[4 * n][...] = total

    args = [parts[k] for k in names] + [loss_parts] + [d[k] for d in (w, m, v) for k in names]
    out_shape = tuple(jax.ShapeDtypeStruct(w[k].shape, F32) for k in names for _ in range(4))
    out_shape += (jax.ShapeDtypeStruct(loss_parts.shape[1:], F32),)
    out = pl.pallas_call(
        body, name="adam_small", grid=(1,), out_shape=out_shape,
        in_specs=[_full(a.shape) for a in args], out_specs=tuple(_full(s.shape) for s in out_shape),
        compiler_params=_params(("arbitrary",)),
    )(*args)
    return {k: out[4 * a:4 * a + 4] for a, k in enumerate(names)}, out[4 * n]


def _local_step(x2, tgt2, shift, scale, gate, norm_g, w_in_t, w_out_b, conv_w, conv_b, w_pool, ls_pool,
                mh_norm_g, b_gates, final_g, send_dw_out=None, send_dw_in=None):
    bg_pad = jnp.pad(b_gates, ((0, 0), (0, 128 - b_gates.shape[1])))
    conv_w8 = jnp.pad(conv_w, ((0, 8 - CONV_WIDTH), (0, 0)))
    fg = final_g.reshape(1, D_MODEL)

    proj, gates, h_b = _fwd_proj(x2, norm_g, scale, shift, w_in_t)
    mix, cst, nst, mst, conv_a, pooled = _mix_fwd(proj, gates, bg_pad, conv_w8, conv_b, w_pool, ls_pool, mh_norm_g)
    dx2, dmix, dwo, dgate, dfg, loss = _out_fwd_bwd(mix, x2, tgt2, w_out_b, gate, fg)
    if send_dw_out is not None:
        bg_pad = bg_pad + send_dw_out(dwo)
    dproj, dcw8, dcb, dwp, dls, dmhg, dbg = _mix_bwd(proj, gates, dmix, conv_a, pooled, cst, nst, mst, bg_pad,
                                                      conv_w8, w_pool, ls_pool, mh_norm_g)
    dw_in_t = _dw_in(h_b, dproj)[:N_IN]
    ng_in = norm_g
    if send_dw_in is not None:
        ng_in = norm_g + send_dw_in(dw_in_t, dcw8[:CONV_WIDTH])
    gx, dsh, dsc, dng = _bwd_in(dproj, w_in_t, x2, dx2, ng_in, scale)
    return dict(loss=loss, grad_x=gx, dw_in_t=dw_in_t, dw_out=dwo, dconv_w=dcw8[:CONV_WIDTH], conv_b=dcb,
                w_pool=dwp, ls_pool=dls, mh_norm_g=dmhg, b_gates=dbg, final_g=dfg, norm_g=dng,
                dmod=jnp.concatenate([dsh, dsc, dgate], axis=1))


def kernel(x, c, norm_g, w_ada, b_ada, w_in, b_gates, conv_w, conv_b, w_pool, ls_pool, mh_norm_g, w_out, final_g, loss_target, m_norm_g, m_w_ada, m_b_ada, m_w_in, m_b_gates, m_conv_w, m_conv_b, m_w_pool, m_ls_pool, m_mh_norm_g, m_w_out, m_final_g, v_norm_g, v_w_ada, v_b_ada, v_w_in, v_b_gates, v_conv_w, v_conv_b, v_w_pool, v_ls_pool, v_mh_norm_g, v_w_out, v_final_g):
    seq = x.shape[1]
    me = 4 * lax.axis_index("x") + 2 * lax.axis_index("y") + lax.axis_index("c")

    g_in, g_out, g_cw, g_c = _all_gather("gather_weights", w_in[0].astype(BF16).T, w_out[0].astype(BF16), conv_w[0], c)
    w_in_t = jnp.pad(g_in.reshape(N_IN, D_MODEL), ((0, N_PAD - N_IN), (0, 0)))
    w_out_b = g_out.reshape(D_MODEL, D_MODEL)
    conv_w_full = jnp.transpose(g_cw, (1, 0, 2)).reshape(CONV_WIDTH, 2 * D_MLSTM)
    c_all16 = jnp.pad(g_c.reshape(N_DEV, D_MODEL), ((0, 8), (0, 0)))

    b_ada_blk = lax.dynamic_slice(b_ada, (0, me * ADA_SHARD), (1, ADA_SHARD))
    mod_all, sc_all16 = _ada_mod(c_all16, w_ada[0], b_ada_blk)
    mod = lax.dynamic_index_in_dim(mod_all, me, axis=1, keepdims=False).reshape(1, 3 * D_MODEL)
    shift, scale, gate = mod[:, :D_MODEL], mod[:, D_MODEL:2 * D_MODEL], mod[:, 2 * D_MODEL:]

    flights = {}

    def send_dw_out(dwo):
        blocks = dwo.reshape(N_DEV, D_MODEL // N_DEV, D_MODEL)
        flights["out"], token = _scatter_start("send_dw_out", (blocks,))
        return token

    def send_dw_in(dw_in_t, dcw):
        blocks = dw_in_t.reshape(N_DEV, N_SHARD, D_MODEL)
        dcw_blocks = jnp.transpose(dcw.reshape(CONV_WIDTH, N_DEV, 128), (1, 0, 2))
        flights["in"], token = _scatter_start("send_dw_in", (blocks, dcw_blocks))
        return token

    r = _local_step(x[0], loss_target[0], shift, scale, gate, norm_g, w_in_t, w_out_b, conv_w_full, conv_b,
                    w_pool[0], ls_pool, mh_norm_g, b_gates, final_g, send_dw_out, send_dw_in)

    small_names = ("norm_g", "b_ada", "b_gates", "conv_b", "w_pool", "ls_pool", "mh_norm_g", "final_g")
    small_grads = dict(norm_g=r["norm_g"], b_ada=r["dmod"], b_gates=r["b_gates"], conv_b=r["conv_b"],
                       w_pool=r["w_pool"], ls_pool=r["ls_pool"], mh_norm_g=r["mh_norm_g"], final_g=r["final_g"])
    flights["small"], _ = _scatter_start(
        "send_small", (r["loss"],) + tuple(small_grads[k] for k in small_names), whole=True)
    p_in, p_cw = _scatter_wait("recv_dw_in", flights["in"], flights["small"][2 + 1 + small_names.index("w_pool")])
    (p_out,) = _scatter_wait("recv_dw_out", flights["out"], p_cw)

    in_t = _adam_sum("adam_w_in", p_in, w_in[0].T, m_w_in[0].T, v_w_in[0].T, N_SHARD, 256)
    gi, di, mi, vi = (o.T for o in in_t)
    go, do_, mo, vo = _adam_sum("adam_w_out", p_out, w_out[0], m_w_out[0], v_w_out[0], 128)
    gc, dc, mc, vc = _adam_sum("adam_conv_w", p_cw, conv_w[0], m_conv_w[0], v_conv_w[0], CONV_WIDTH)
    gathered = _scatter_wait("recv_small", flights["small"], go, whole=True)
    p_loss, p_small = gathered[0], dict(zip(small_names, gathered[1:]))

    def plain(norm_g_, b_ada_, b_gates_, conv_b_, w_pool_, ls_pool_, mh_norm_g_, final_g_):
        return dict(norm_g=norm_g_, b_ada=b_ada_, b_gates=b_gates_, conv_b=conv_b_, w_pool=w_pool_[0],
                    ls_pool=ls_pool_, mh_norm_g=mh_norm_g_, final_g=final_g_.reshape(1, D_MODEL))

    small, loss_row = _adam_small(
        p_small, p_loss,
        plain(norm_g, b_ada, b_gates, conv_b, w_pool, ls_pool, mh_norm_g, final_g),
        plain(m_norm_g, m_b_ada, m_b_gates, m_conv_b, m_w_pool, m_ls_pool, m_mh_norm_g, m_final_g),
        plain(v_norm_g, v_b_ada, v_b_gates, v_conv_b, v_w_pool, v_ls_pool, v_mh_norm_g, v_final_g))

    dmod_all = p_small["b_ada"].reshape(N_DEV, 3 * D_MODEL)
    dmod_blk16 = jnp.pad(lax.dynamic_slice(dmod_all, (0, me * ADA_SHARD), (N_DEV, ADA_SHARD)), ((0, 8), (0, 0)))
    ga, da, ma, va = _adam_ada(sc_all16, dmod_blk16, w_ada[0], m_w_ada[0], v_w_ada[0])

    names = ("norm_g", "w_ada", "b_ada", "w_in", "b_gates", "conv_w", "conv_b", "w_pool", "ls_pool", "mh_norm_g",
             "w_out", "final_g")
    shapes = dict(norm_g=norm_g.shape, b_ada=b_ada.shape, b_gates=b_gates.shape, conv_b=conv_b.shape,
                  w_pool=w_pool.shape, ls_pool=ls_pool.shape, mh_norm_g=mh_norm_g.shape, final_g=final_g.shape)
    sharded = dict(w_ada=(ga, da, ma, va), w_in=(gi, di, mi, vi), conv_w=(gc, dc, mc, vc), w_out=(go, do_, mo, vo))
    outs = []
    for kind in range(4):
        for nm in names:
            if nm in sharded:
                outs.append(sharded[nm][kind][None])
            else:
                outs.append(small[nm][kind].reshape(shapes[nm]))
    loss = loss_row[0, 0]
    grad_x = r["grad_x"].reshape(1, seq, D_MODEL)
    return (loss, grad_x, *outs)
```

```python
import jax
import jax.numpy as jnp
from jax import lax
from jax.experimental import pallas as pl
from jax.experimental.pallas import tpu as pltpu

F32 = jnp.float32
BF16 = jnp.bfloat16

D_MODEL = 1024
D_POOL = 512
D_MLSTM = 512
N_HEADS = 4
HEAD_DIM = 128
CHUNK = 128
POOL_WINDOWS = (2, 4, 8, 16)
POOL_GROUP_DIM = 128
CONV_WIDTH = 4
EPS = 1e-6
N_MAIN = 3584
N_IN = 3592
N_PAD = 3840
N_SHARD = N_IN // 8
SHARD_ROWS = 464
ADA_SHARD = 3 * D_MODEL // 8
N_DEV = 8
CONV_HALO = 8
POOL_HALO = 16
NEG_BIG = -1e30
VMEM_LIMIT_BYTES = 56 * 1024 * 1024

ADAM_LR = 0.001
ADAM_B1 = 0.9
ADAM_B2 = 0.999
ADAM_EPS = 1e-08
ADAM_WD = 0.01
ADAM_STEP = 10

def _dot(a, b):
    return jnp.dot(a, b, preferred_element_type=F32)


def _dot_nt(a, b):
    return lax.dot_general(a, b, (((1,), (1,)), ((), ())), preferred_element_type=F32)


def _dot_tn(a, b):
    return lax.dot_general(a, b, (((0,), (0,)), ((), ())), preferred_element_type=F32)


def _dot_f32(a, b):
    return jnp.dot(a, b, precision=lax.Precision.HIGHEST, preferred_element_type=F32)


def _row_mean_mxu(x):
    return _dot(x.astype(BF16), jnp.full((HEAD_DIM, HEAD_DIM), 1.0 / HEAD_DIM, BF16))


def _sigmoid(x):
    return jax.nn.sigmoid(x)


def _log_sigmoid(x):
    return jnp.minimum(x, 0.0) - jnp.log1p(jnp.exp(-jnp.abs(x)))


def _params(sem):
    return pltpu.CompilerParams(dimension_semantics=sem, vmem_limit_bytes=VMEM_LIMIT_BYTES)


def _full(shape):
    n = len(shape)
    return pl.BlockSpec(shape, lambda *_: (0,) * n)


def _mesh_pos():
    return lax.axis_index("x"), lax.axis_index("y"), lax.axis_index("c")


def _peer(k):
    x, y, c = _mesh_pos()
    px = 1 - x if (k >> 2) & 1 else x
    py = 1 - y if (k >> 1) & 1 else y
    pc = 1 - c if k & 1 else c
    return (px, py, pc), 4 * px + 2 * py + pc


def _remote(src, dst, send_sem, recv_sem, to):
    return pltpu.make_async_remote_copy(src_ref=src, dst_ref=dst, send_sem=send_sem, recv_sem=recv_sem, device_id=to,
                                        device_id_type=pl.DeviceIdType.MESH)


def _row_pieces(shards, split):
    pieces = []
    for a, s in enumerate(shards):
        cuts = (0,) + tuple(split.get(a, ())) + (s.shape[0],)
        if len(cuts) == 2:
            pieces.append((a, None))
        else:
            pieces += [(a, pl.ds(lo, hi - lo)) for lo, hi in zip(cuts[:-1], cuts[1:])]
    return pieces


def _two_level_gather(src, dst, send_sems, recv_sems, local_sems, pieces):
    n = len(pieces)
    x, y, c = _mesh_pos()
    me = 4 * x + 2 * y + c
    sibling = (x, y, 1 - c)
    south = c == 0
    near = (jnp.where(south, 1 - x, x), jnp.where(south, y, 1 - y))
    far = (jnp.where(south, x, 1 - x), jnp.where(south, 1 - y, y))
    diag = (1 - x, 1 - y)

    def block_of(chip, core):
        return 4 * chip[0] + 2 * chip[1] + core

    def mine(q):
        a, rows = pieces[q]
        return src[a] if rows is None else src[a].at[rows]

    def slot(q, block):
        a, rows = pieces[q]
        return dst[a].at[block] if rows is None else dst[a].at[block, rows]

    def copy(a, k, block, to, own=False):
        return _remote(mine(a) if own else slot(a, block), slot(a, block), send_sems.at[a, k], recv_sems.at[a, k], to)

    local = [pltpu.make_async_copy(src[a], dst[a].at[me], local_sems.at[a]) for a in range(len(src))]
    sent = [copy(a, 0, me, sibling, True) for a in range(n)]
    sent += [copy(a, 1, me, (*near, c), True) for a in range(n)]
    sent += [copy(a, 2, me, (*far, c), True) for a in range(n)]
    for cp in local + sent:
        cp.start()
    for a in range(n):
        copy(a, 1, block_of(near, c), sibling).wait_recv()
        sent += [copy(a, 3, block_of(near, c), (*far, c)), copy(a, 4, block_of(near, c), sibling)]
        sent[-2].start()
        sent[-1].start()
    for k, chip in ((2, far), (3, diag)):
        for a in range(n):
            copy(a, k, block_of(chip, c), sibling).wait_recv()
            sent.append(copy(a, 3 + k, block_of(chip, c), sibling))
            sent[-1].start()
    for k, chip in ((0, (x, y)), (4, far), (5, near), (6, diag)):
        for a in range(n):
            copy(a, k, block_of(chip, 1 - c), sibling).wait_recv()
    for cp in sent:
        cp.wait_send()
    for cp in local:
        cp.wait()


GATHER_COPIES = 7


def _all_gather(name, *shards, split=None):
    n = len(shards)
    pieces = _row_pieces(shards, split or {})

    def body(*refs):
        _two_level_gather(refs[:n], refs[n:2 * n], *refs[2 * n:], pieces)

    hbm = pl.BlockSpec(memory_space=pltpu.HBM)
    return pl.pallas_call(
        body, name=name,
        out_shape=tuple(jax.ShapeDtypeStruct((N_DEV,) + s.shape, s.dtype) for s in shards),
        in_specs=[hbm] * n, out_specs=tuple([hbm] * n),
        scratch_shapes=[pltpu.SemaphoreType.DMA((len(pieces), GATHER_COPIES)),
                        pltpu.SemaphoreType.DMA((len(pieces), GATHER_COPIES)), pltpu.SemaphoreType.DMA((n,))],
    )(*shards)


def _ada_mod(c_all16, w_ada_blk, b_ada_blk):
    def body(c_ref, w_ref, b_ref, out_ref, sc_ref, send_sems, recv_sems):
        x, y, c = _mesh_pos()
        me = 4 * x + 2 * y + c
        cv = c_ref[...]
        sc = cv * _sigmoid(cv)
        sc_ref[...] = sc
        blk = _dot(sc.astype(BF16), w_ref[...].astype(BF16)) + b_ref[...]
        out_ref[me] = blk[0:N_DEV, :]
        copies = []
        for k in range(1, N_DEV):
            peer, _ = _peer(k)
            copies.append(pltpu.make_async_remote_copy(
                src_ref=out_ref.at[me], dst_ref=out_ref.at[me], send_sem=send_sems.at[k - 1],
                recv_sem=recv_sems.at[k - 1], device_id=peer, device_id_type=pl.DeviceIdType.MESH))
        for cp in copies:
            cp.start()
        for cp in copies:
            cp.wait()

    vmem = pl.BlockSpec(memory_space=pltpu.VMEM)
    return pl.pallas_call(
        body, name="ada_mod",
        out_shape=(jax.ShapeDtypeStruct((N_DEV, N_DEV, ADA_SHARD), F32),
                   jax.ShapeDtypeStruct(c_all16.shape, F32)),
        in_specs=[vmem] * 3, out_specs=(vmem, vmem),
        scratch_shapes=[pltpu.SemaphoreType.DMA((N_DEV - 1,)), pltpu.SemaphoreType.DMA((N_DEV - 1,))],
    )(c_all16, w_ada_blk, b_ada_blk)


def _scatter_copies(src, land, send_sems, recv_sems, whole=False):
    x, y, c = _mesh_pos()
    me = 4 * x + 2 * y + c
    copies = []
    for k in range(1, N_DEV):
        peer, p = _peer(k)
        for a in range(len(src)):
            i = a * (N_DEV - 1) + k - 1
            copies.append(_remote(src[a] if whole else src[a].at[p], land[a].at[me], send_sems.at[i],
                                  recv_sems.at[i], peer))
    return copies


def _scatter_start(name, blocks, whole=False):
    n = len(blocks)

    def body(*refs):
        src, land = refs[:n], refs[n:2 * n]
        send_sems, recv_sems = refs[2 * n], refs[2 * n + 1]
        token_ref = refs[-1]
        for cp in _scatter_copies(src, land, send_sems, recv_sems, whole):
            cp.start()
        token_ref[...] = jnp.zeros_like(token_ref)

    hbm = pl.BlockSpec(memory_space=pltpu.HBM)
    sem = pl.BlockSpec(memory_space=pltpu.SEMAPHORE)
    landing = [((N_DEV,) + b.shape if whole else b.shape, b.dtype) for b in blocks]
    through = tuple(pltpu.HBM(b.shape, b.dtype) for b in blocks) + tuple(pltpu.HBM(s, d) for s, d in landing)
    args = [pltpu.with_memory_space_constraint(b, pltpu.HBM) for b in blocks]
    args += [pltpu.with_memory_space_constraint(lax.empty(s, d), pltpu.HBM) for s, d in landing]
    out = pl.pallas_call(
        body, name=name,
        out_shape=(pltpu.SemaphoreType.DMA((n * (N_DEV - 1),)),) * 2 + through
        + (jax.ShapeDtypeStruct((8, 128), F32),),
        in_specs=[hbm] * (2 * n),
        out_specs=(sem, sem) + (hbm,) * (2 * n) + (pl.BlockSpec(memory_space=pltpu.VMEM),),
        input_output_aliases={i: 2 + i for i in range(2 * n)},
        compiler_params=pltpu.CompilerParams(has_side_effects=pltpu.SideEffectType.DATAFLOW_SIDE_EFFECTING),
    )(*args)
    return out[:-1], out[-1][0:1, 0:1]


def _scatter_wait(name, state, after, whole=False):
    n = (len(state) - 2) // 2
    send_sems, recv_sems = state[0], state[1]
    src, land = state[2:2 + n], state[2 + n:]

    def body(*refs):
        src_r, land_r = refs[:n], refs[n:2 * n]
        for cp in _scatter_copies(src_r, land_r, refs[2 * n], refs[2 * n + 1], whole):
            cp.wait_send()
            cp.wait_recv()

    hbm = pl.BlockSpec(memory_space=pltpu.HBM)
    sem = pl.BlockSpec(memory_space=pltpu.SEMAPHORE)
    out = pl.pallas_call(
        body, name=name,
        out_shape=tuple(pltpu.HBM(b.shape, b.dtype) for b in src + land),
        in_specs=[hbm] * (2 * n) + [sem, sem, pl.BlockSpec(memory_space=pl.ANY)],
        out_specs=(hbm,) * (2 * n),
        input_output_aliases={i: i for i in range(2 * n)},
        compiler_params=pltpu.CompilerParams(has_side_effects=pltpu.SideEffectType.DATAFLOW_SIDE_EFFECTING),
    )(*src, *land, send_sems, recv_sems, after)
    me = 4 * lax.axis_index("x") + 2 * lax.axis_index("y") + lax.axis_index("c")
    landed = []
    for a in range(n):
        own = out[a][None] if whole else lax.dynamic_index_in_dim(out[a], me, axis=0, keepdims=True)
        landed.append(lax.dynamic_update_slice_in_dim(out[n + a], own, me, axis=0))
    return landed


def _fwd_proj(x, norm_g, scale, shift, w_in_t):
    seq = x.shape[0]
    tm = min(512, seq)
    sub = min(256, tm)
    tn = 512

    def body(x_ref, ng_ref, sc_ref, sh_ref, wt_ref, proj_ref, gates_ref, h_ref):
        def chain(n):
            for _ in range(n):
                yield
            rows = slice(n * sub, (n + 1) * sub)
            xt = x_ref[rows, :]
            r = lax.rsqrt(jnp.mean(xt * xt, axis=-1, keepdims=True) + EPS)
            h = ((xt * r) * ng_ref[...]) * (1.0 + sc_ref[...]) + sh_ref[...]
            hb = h.astype(BF16)
            h_ref[rows, :] = hb
            yield
            gates_ref[rows, :] = _dot_nt(hb, wt_ref[N_MAIN:N_MAIN + 128, :])
            for j in range(N_MAIN // tn):
                proj_ref[rows, j * tn:(j + 1) * tn] = _dot_nt(hb, wt_ref[j * tn:(j + 1) * tn, :])

        _in_lockstep(chain(n) for n in range(tm // sub))

    vec = _full((1, D_MODEL))
    tile = pl.BlockSpec((tm, D_MODEL), lambda i: (i, 0))
    return pl.pallas_call(
        body, name="fwd_proj", grid=(seq // tm,),
        out_shape=(jax.ShapeDtypeStruct((seq, N_MAIN), F32), jax.ShapeDtypeStruct((seq, 128), F32),
                   jax.ShapeDtypeStruct((seq, D_MODEL), BF16)),
        in_specs=[tile, vec, vec, vec, _full((N_PAD, D_MODEL))],
        out_specs=(pl.BlockSpec((tm, N_MAIN), lambda i: (i, 0)), pl.BlockSpec((tm, 128), lambda i: (i, 0)), tile),
        compiler_params=_params(("arbitrary",)),
    )(x, norm_g, scale, shift, w_in_t)


def _gate_forms(gpre):
    r = lax.broadcasted_iota(jnp.int32, (CHUNK, CHUNK), 0)
    c = lax.broadcasted_iota(jnp.int32, (CHUNK, CHUNK), 1)
    causal = c <= r
    ltri = jnp.where(causal, 1.0, 0.0).astype(F32)
    utri = jnp.where(r <= c, 1.0, 0.0).astype(F32)
    bcol = _dot_f32(ltri, _log_sigmoid(gpre))
    gt8 = gpre.T[0:8, :]
    brow = _dot_f32(_log_sigmoid(gt8), utri)
    return causal, utri, bcol, gt8, brow


def _in_lockstep(stages):
    alive = list(stages)
    while alive:
        still = []
        for g in alive:
            try:
                next(g)
                still.append(g)
            except StopIteration:
                pass
        alive = still


def _head_fwd(qh, kh, vh, bc, br, igr, m_prev, c_h, n_row, causal):
    qb, kb, vb, cb = qh.astype(BF16), kh.astype(BF16), vh.astype(BF16), c_h.astype(BF16)
    qk = _dot_nt(qb, kb)
    cq = _dot_nt(qb, cb)
    nq = _dot_nt(qb, jnp.broadcast_to(n_row.astype(BF16), (HEAD_DIM, HEAD_DIM)))
    yield
    dlog = jnp.where(causal, bc - br + igr, NEG_BIG)
    inter_log = bc + m_prev
    m_t = jnp.maximum(inter_log, jnp.max(dlog, axis=-1, keepdims=True))
    yield
    dmat = jnp.exp(dlog - m_t)
    inter = jnp.exp(inter_log - m_t)
    s = qk * dmat
    sv = _dot(s.astype(BF16), vb)
    yield
    den = jnp.sum(s, axis=-1, keepdims=True) + inter * nq
    emt = jnp.exp(-m_t)
    yield
    num = sv + inter * cq
    dn = jnp.maximum(jnp.abs(den), emt)
    hm = num / dn
    return dict(dmat=dmat, inter=inter, qb=qb, kb=kb, vb=vb, cb=cb, s=s, cq=cq, nq=nq, den=den, emt=emt,
                dn=dn, hm=hm)


def _state_weights(bc, igc, m_prev, m_new=None):
    last = lax.broadcasted_iota(jnp.int32, (CHUNK, 1), 0) == CHUNK - 1
    b_last = jnp.sum(jnp.where(last, bc, 0.0), axis=0, keepdims=True)
    wlog = b_last - bc + igc
    if m_new is None:
        m_new = jnp.maximum(b_last + m_prev, jnp.max(wlog, axis=0, keepdims=True))
    w_c = jnp.exp(wlog - m_new)
    decay = jnp.exp(b_last + m_prev - m_new)
    return w_c, decay, m_new, last


def _rows_back(x, k):
    return x if k == 0 else pltpu.roll(x, k, 0)


def _rows_ahead(x, k):
    return x if k == 0 else pltpu.roll(x, x.shape[0] - k, 0)


def _conv_taps(xpad):
    return [_rows_back(xpad, CONV_WIDTH - 1 - j)[CONV_HALO:, :] for j in range(CONV_WIDTH)]


def _conv_pre(taps, cw_ref, cb_ref):
    a = cb_ref[...]
    for j in range(CONV_WIDTH):
        a = a + cw_ref[j:j + 1, :] * taps[j]
    return a


def _window_sum(x, w, shift):
    k = 1
    while k < w:
        x = x + shift(x, k)
        k *= 2
    return x


def _pool_window_sum(upad_ref, g, w):
    lanes = slice(g * POOL_GROUP_DIM, (g + 1) * POOL_GROUP_DIM)
    return _window_sum(upad_ref[:, lanes], w, _rows_back)[POOL_HALO:, :]


def _pool_inv_count(row0, rows, w):
    pos = row0 + lax.broadcasted_iota(jnp.int32, (rows, 1), 0) + 1
    return 1.0 / jnp.minimum(pos, w).astype(F32)


FWD_CHUNKS = 4
BWD_CHUNKS = 4


def _mix_fwd(proj, gates, bg_pad, conv_w8, conv_b, w_pool, ls_pool, mh_g):
    seq = proj.shape[0]
    n_chunks = seq // CHUNK
    per_step = FWD_CHUNKS
    blk = per_step * CHUNK

    def body(uz_ref, qk_ref, v_ref, o_ref, zm_ref, uh_ref, qkh_ref, g_ref, bg_ref, cw_ref, cb_ref, wp_ref,
             ls_ref, mhg_ref, mix_ref, cst_ref, nst_ref, mst_ref, a_ref, pooled_ref, c_scr, n_scr, m_scr, xpad, upad):
        i = pl.program_id(0)

        @pl.when(i == 0)
        def _():
            c_scr[...] = jnp.zeros_like(c_scr)
            n_scr[...] = jnp.zeros_like(n_scr)
            m_scr[...] = jnp.zeros_like(m_scr)

        first = i == 0

        upad[0:POOL_HALO, :] = jnp.where(first, 0.0, uh_ref[...])
        upad[POOL_HALO:POOL_HALO + blk, :] = uz_ref[:, 0:D_POOL]
        for g, w in enumerate(POOL_WINDOWS):
            lanes = slice(g * POOL_GROUP_DIM, (g + 1) * POOL_GROUP_DIM)
            pooled = (_pool_window_sum(upad, g, w) * _pool_inv_count(i * blk, blk, w) - uz_ref[:, lanes]).astype(BF16)
            pooled_ref[:, lanes] = pooled
            y = _dot(pooled, wp_ref[g].astype(BF16)) * ls_ref[:, lanes]
            zp = uz_ref[:, D_POOL + g * POOL_GROUP_DIM:D_POOL + (g + 1) * POOL_GROUP_DIM]
            mix_ref[:, lanes] = (y * (zp * _sigmoid(zp))).astype(BF16)

        xpad[0:CONV_HALO, :] = jnp.where(first, 0.0, qkh_ref[...])
        xpad[CONV_HALO:CONV_HALO + blk, :] = qk_ref[...]
        a = _conv_pre(_conv_taps(xpad[...]), cw_ref, cb_ref)
        a_ref[...] = a
        qk = a * _sigmoid(a)

        def head(rows, h, qh, kh, vh, bc, br, igr, m_prev, c_h, n_row, causal):
            lanes = slice(h * HEAD_DIM, (h + 1) * HEAD_DIM)
            f = yield from _head_fwd(qh, kh, vh, bc, br, igr, m_prev, c_h, n_row, causal)
            yield
            hm = f["hm"]
            hn = hm * lax.rsqrt(_row_mean_mxu(hm * hm) + EPS) * mhg_ref[:, lanes]
            zm = zm_ref[rows, lanes]
            out = hn * _sigmoid(o_ref[rows, lanes]) * (zm * _sigmoid(zm))
            mix_ref[rows, D_POOL + h * HEAD_DIM:D_POOL + (h + 1) * HEAD_DIM] = out.astype(BF16)

        c_cur = [c_scr[h] for h in range(N_HEADS)]
        n_cur = [n_scr[h:h + 1, :] for h in range(N_HEADS)]
        m_cur = [m_scr[h:h + 1, 0:1] for h in range(N_HEADS)]
        chains = []
        for s in range(per_step):
            rows = slice(s * CHUNK, (s + 1) * CHUNK)
            gpre = g_ref[rows, :] + bg_ref[...]
            causal, _, bcol, gt8, brow = _gate_forms(gpre)
            nst_ref[s] = jnp.zeros((8, 128), F32)
            mst_ref[s] = jnp.zeros((8, 128), F32)
            for h in range(N_HEADS):
                lanes = slice(h * HEAD_DIM, (h + 1) * HEAD_DIM)
                cst_ref[s, h] = c_cur[h]
                nst_ref[s, h:h + 1, :] = n_cur[h]
                mst_ref[s, h:h + 1, :] = jnp.broadcast_to(m_cur[h], (1, 128))
                qh = qk[rows, lanes]
                kh = qk[rows, D_MLSTM + h * HEAD_DIM:D_MLSTM + (h + 1) * HEAD_DIM] * (HEAD_DIM ** -0.5)
                vh = v_ref[rows, lanes]
                bc = bcol[:, N_HEADS + h:N_HEADS + h + 1]
                br = brow[N_HEADS + h:N_HEADS + h + 1, :]
                igr = gt8[h:h + 1, :]
                igc = gpre[:, h:h + 1]
                chains.append(head(rows, h, qh, kh, vh, bc, br, igr, m_cur[h], c_cur[h], n_cur[h], causal))
                w_c, decay, m_new, _ = _state_weights(bc, igc, m_cur[h])
                c_cur[h] = decay * c_cur[h] + _dot_tn((vh * w_c).astype(BF16), kh.astype(BF16))
                n_cur[h] = decay * n_cur[h] + jnp.sum(w_c * kh, axis=0, keepdims=True)
                m_cur[h] = m_new
        for h in range(N_HEADS):
            c_scr[h] = c_cur[h]
            n_scr[h:h + 1, :] = n_cur[h]
            m_scr[h:h + 1, :] = jnp.broadcast_to(m_cur[h], (1, 128))
        _in_lockstep(chains)

    in_specs = [
        pl.BlockSpec((blk, 1024), lambda i: (i, 0)),
        pl.BlockSpec((blk, 1024), lambda i: (i, 1)),
        pl.BlockSpec((blk, 512), lambda i: (i, 4)),
        pl.BlockSpec((blk, 512), lambda i: (i, 5)),
        pl.BlockSpec((blk, 512), lambda i: (i, 6)),
        pl.BlockSpec((POOL_HALO, 512), lambda i: (jnp.maximum(i * (blk // POOL_HALO) - 1, 0), 0)),
        pl.BlockSpec((CONV_HALO, 1024), lambda i: (jnp.maximum(i * (blk // CONV_HALO) - 1, 0), 1)),
        pl.BlockSpec((blk, 128), lambda i: (i, 0)),
        _full((1, 128)), _full((8, 1024)), _full((1, 1024)), _full((4, 128, 128)), _full((1, 512)),
        _full((1, 512))]
    return pl.pallas_call(
        body, name="mix_fwd", grid=(n_chunks // per_step,),
        out_shape=(jax.ShapeDtypeStruct((seq, D_MODEL), BF16),
                   jax.ShapeDtypeStruct((n_chunks, N_HEADS, HEAD_DIM, HEAD_DIM), F32),
                   jax.ShapeDtypeStruct((n_chunks, 8, 128), F32),
                   jax.ShapeDtypeStruct((n_chunks, 8, 128), F32),
                   jax.ShapeDtypeStruct((seq, 2 * D_MLSTM), F32),
                   jax.ShapeDtypeStruct((seq, D_POOL), BF16)),
        in_specs=in_specs,
        out_specs=(pl.BlockSpec((blk, D_MODEL), lambda i: (i, 0)),
                   pl.BlockSpec((per_step, N_HEADS, HEAD_DIM, HEAD_DIM), lambda i: (i, 0, 0, 0)),
                   pl.BlockSpec((per_step, 8, 128), lambda i: (i, 0, 0)),
                   pl.BlockSpec((per_step, 8, 128), lambda i: (i, 0, 0)),
                   pl.BlockSpec((blk, 2 * D_MLSTM), lambda i: (i, 0)),
                   pl.BlockSpec((blk, D_POOL), lambda i: (i, 0))),
        scratch_shapes=[pltpu.VMEM((N_HEADS, HEAD_DIM, HEAD_DIM), F32), pltpu.VMEM((8, 128), F32),
                        pltpu.VMEM((8, 128), F32), pltpu.VMEM((CONV_HALO + blk, 1024), F32),
                        pltpu.VMEM((POOL_HALO + blk, D_POOL), F32)],
        compiler_params=_params(("arbitrary",)),
    )(proj, proj, proj, proj, proj, proj, proj, gates, bg_pad, conv_w8, conv_b, w_pool, ls_pool, mh_g)


def _out_fwd_bwd(mix, x, tgt, w_out_b, gate, final_g):
    seq = x.shape[0]
    tm = min(512, seq)
    sub = min(256, tm)

    def body(mix_ref, x_ref, t_ref, w_ref, gate_ref, fg_ref, dx2_ref, dmix_ref, dwo_ref, dgate_ref, dfg_ref,
             loss_ref, dwo_acc):
        @pl.when(pl.program_id(0) == 0)
        def _():
            dwo_acc[...] = jnp.zeros_like(dwo_acc)
            dgate_ref[...] = jnp.zeros_like(dgate_ref)
            dfg_ref[...] = jnp.zeros_like(dfg_ref)
            loss_ref[...] = jnp.zeros_like(loss_ref)

        w = w_ref[...]
        gate_v = gate_ref[...]
        fg = fg_ref[...]
        do2_parts = [None] * (tm // sub)

        def chain(n):
            rows = slice(n * sub, (n + 1) * sub)
            o2 = _dot(mix_ref[rows, :], w)
            yield
            x2 = x_ref[rows, :] + gate_v * o2
            r2 = lax.rsqrt(jnp.mean(x2 * x2, axis=-1, keepdims=True) + EPS)
            x2n = x2 * r2
            err = x2n * fg - t_ref[rows, :]
            part = 0.5 * jnp.sum(jnp.sum(err * err, axis=-1, keepdims=True), axis=0, keepdims=True) / D_MODEL
            loss_ref[...] += jnp.broadcast_to(part, loss_ref.shape)
            dy = err / D_MODEL
            dfg_ref[...] += jnp.sum(dy * x2n, axis=0, keepdims=True)
            gdy = dy * fg
            dx2 = r2 * (gdy - x2n * jnp.mean(gdy * x2n, axis=-1, keepdims=True))
            dx2_ref[rows, :] = dx2
            dgate_ref[...] += jnp.sum(dx2 * o2, axis=0, keepdims=True)
            do2 = (dx2 * gate_v).astype(BF16)
            dmix_ref[rows, :] = _dot_nt(do2, w)
            do2_parts[n] = do2

        _in_lockstep(chain(n) for n in range(tm // sub))
        dwo_acc[...] += _dot_tn(mix_ref[...], jnp.concatenate(do2_parts, axis=0))

        @pl.when(pl.program_id(0) == seq // tm - 1)
        def _():
            dwo_ref[...] = dwo_acc[...].astype(BF16)

    tile = pl.BlockSpec((tm, D_MODEL), lambda i: (i, 0))
    vec = _full((1, D_MODEL))
    return pl.pallas_call(
        body, name="out_fwd_bwd", grid=(seq // tm,),
        out_shape=(jax.ShapeDtypeStruct((seq, D_MODEL), F32), jax.ShapeDtypeStruct((seq, D_MODEL), F32),
                   jax.ShapeDtypeStruct((D_MODEL, D_MODEL), BF16), jax.ShapeDtypeStruct((1, D_MODEL), F32),
                   jax.ShapeDtypeStruct((1, D_MODEL), F32), jax.ShapeDtypeStruct((1, 128), F32)),
        in_specs=[tile, tile, tile, _full((D_MODEL, D_MODEL)), vec, vec],
        out_specs=(tile, tile, _full((D_MODEL, D_MODEL)), vec, vec, _full((1, 128))),
        scratch_shapes=[pltpu.VMEM((D_MODEL, D_MODEL), F32)],
        compiler_params=_params(("arbitrary",)),
    )(mix, x, tgt, w_out_b, gate, final_g)


def _mix_bwd(proj, gates, dmix, conv_a, pooled, cst, nst, mst, bg_pad, conv_w8, w_pool, ls_pool, mh_g):
    seq = proj.shape[0]
    n_chunks = seq // CHUNK
    per_step = BWD_CHUNKS
    blk = per_step * CHUNK
    n_blocks = n_chunks // per_step

    def body(zp_ref, qk_ref, v_ref, o_ref, zm_ref, g_ref, dmix_ref, a_ref, pooled_ref, cst_ref, nst_ref, mst_ref,
             mnx_ref, bg_ref, cw_ref, wp_ref, ls_ref, mhg_ref,
             dp_ref, dcw_ref, dcb_ref, dwp_ref, dls_ref, dmhg_ref, dbg_ref,
             dc_scr, dn_scr, dapad, dpipad):
        i = pl.program_id(0)
        bi = n_blocks - 1 - i

        @pl.when(i == 0)
        def _():
            for ref in (dc_scr, dn_scr, dcw_ref, dcb_ref, dwp_ref, dls_ref, dmhg_ref, dbg_ref):
                ref[...] = jnp.zeros_like(ref)
            dapad[blk:blk + CONV_HALO, :] = jnp.zeros((CONV_HALO, 1024), F32)
            dpipad[blk:blk + POOL_HALO, :] = jnp.zeros((POOL_HALO, D_POOL), F32)

        dpooled = []
        for g, w in enumerate(POOL_WINDOWS):
            lanes = slice(g * POOL_GROUP_DIM, (g + 1) * POOL_GROUP_DIM)
            zlanes = slice(D_POOL + g * POOL_GROUP_DIM, D_POOL + (g + 1) * POOL_GROUP_DIM)
            inv = _pool_inv_count(bi * blk, blk, w)
            pb = pooled_ref[:, lanes]
            wpb = wp_ref[g].astype(BF16)
            yw = _dot(pb, wpb)
            ls = ls_ref[:, lanes]
            zp = zp_ref[:, lanes]
            sg = _sigmoid(zp)
            dpo = dmix_ref[:, lanes]
            dp_ref[:, zlanes] = (dpo * (yw * ls) * (sg * (1.0 + zp * (1.0 - sg)))).astype(BF16)
            dy = dpo * (zp * sg)
            dls_ref[:, lanes] += jnp.sum(dy * yw, axis=0, keepdims=True)
            dyw = (dy * ls).astype(BF16)
            dwp_ref[g] += _dot_tn(pb, dyw)
            dpl = _dot_nt(dyw, wpb)
            dpooled.append(dpl)
            dpipad[0:blk, lanes] = dpl * inv
        for g, w in enumerate(POOL_WINDOWS):
            lanes = slice(g * POOL_GROUP_DIM, (g + 1) * POOL_GROUP_DIM)
            du = _window_sum(dpipad[:, lanes], w, _rows_ahead)[0:blk, :] - dpooled[g]
            dp_ref[:, lanes] = du.astype(BF16)
        dpipad[blk:blk + POOL_HALO, :] = dpipad[0:POOL_HALO, :]

        def silu_and_slope(rows, cols):
            a = a_ref[rows, cols]
            sg = _sigmoid(a)
            return a * sg, sg * (1.0 + a * (1.0 - sg))

        lane = lax.broadcasted_iota(jnp.int32, (CHUNK, 128), 1)
        row = lax.broadcasted_iota(jnp.int32, (CHUNK, 128), 0)
        scale_k = HEAD_DIM ** -0.5
        forms = [None] * per_step
        col_g_rows = [[] for _ in range(per_step)]
        dig_parts = [[] for _ in range(per_step)]
        db_parts = [[] for _ in range(per_step)]
        d_state = [[None] * N_HEADS for _ in range(per_step)]

        def state_terms(s, h, c_h, n_row, vb, kb):
            dcn, dnn = d_state[s][h]
            dcnb = dcn.astype(BF16)
            amat = _dot(vb, dcnb) + dnn
            kdc = _dot_nt(kb, dcnb)
            ddecay = (jnp.sum(jnp.sum(dcn * c_h, axis=-1, keepdims=True), axis=0, keepdims=True)
                      + jnp.sum(dnn * n_row, axis=-1, keepdims=True))
            return dcn, dnn, amat, kdc, ddecay

        def head(s, h):
            rows = slice(s * CHUNK, (s + 1) * CHUNK)
            lanes = slice(h * HEAD_DIM, (h + 1) * HEAD_DIM)
            klanes = slice(D_MLSTM + h * HEAD_DIM, D_MLSTM + (h + 1) * HEAD_DIM)
            gpre, causal, utri, bcol, gt8, brow = forms[s]
            qh, dsilu_q = silu_and_slope(rows, lanes)
            kh, dsilu_k = silu_and_slope(rows, klanes)
            kh = kh * scale_k
            vh = v_ref[rows, lanes]
            bc = bcol[:, N_HEADS + h:N_HEADS + h + 1]
            br = brow[N_HEADS + h:N_HEADS + h + 1, :]
            igr = gt8[h:h + 1, :]
            igc = gpre[:, h:h + 1]
            m_prev = mst_ref[s, h:h + 1, 0:1]
            m_next = mnx_ref[0, h:h + 1, 0:1] if s == per_step - 1 else mst_ref[s + 1, h:h + 1, 0:1]
            c_h = cst_ref[s, h]
            n_row = nst_ref[s, h:h + 1, :]
            w_c, decay, _, last = _state_weights(bc, igc, m_prev, m_next)
            terms = None
            if s == per_step - 1:
                terms = state_terms(s, h, c_h, n_row, vh.astype(BF16), kh.astype(BF16))
            f = yield from _head_fwd(qh, kh, vh, bc, br, igr, m_prev, c_h, n_row, causal)
            qb, kb, vb, cb = f["qb"], f["kb"], f["vb"], f["cb"]
            sm, dmat, inter, den, dn, hm = f["s"], f["dmat"], f["inter"], f["den"], f["dn"], f["hm"]
            yield

            rinv = lax.rsqrt(_row_mean_mxu(hm * hm) + EPS)
            hmn = hm * rinv
            gh = mhg_ref[:, lanes]
            o_pre = o_ref[rows, lanes]
            og = _sigmoid(o_pre)
            zm = zm_ref[rows, lanes]
            sgz = _sigmoid(zm)
            sz = zm * sgz
            dout = dmix_ref[rows, D_POOL + h * HEAD_DIM:D_POOL + (h + 1) * HEAD_DIM]
            hn = hmn * gh
            dp_ref[rows, 2560 + h * HEAD_DIM:2560 + (h + 1) * HEAD_DIM] = (
                dout * hn * sz * og * (1.0 - og)).astype(BF16)
            dp_ref[rows, 3072 + h * HEAD_DIM:3072 + (h + 1) * HEAD_DIM] = (
                dout * hn * og * (sgz * (1.0 + zm * (1.0 - sgz)))).astype(BF16)
            dhn = dout * og * sz
            dmhg_ref[:, lanes] += jnp.sum(dhn * hmn, axis=0, keepdims=True)
            dyn = dhn * gh
            dhm = rinv * (dyn - hmn * _row_mean_mxu(dyn * hmn))
            yield

            inv_dn = 1.0 / dn
            dnum = dhm * inv_dn
            hd = jnp.sum(dhm * hm, axis=-1, keepdims=True)
            dden = jnp.where(jnp.abs(den) > f["emt"], -hd * inv_dn * jnp.sign(den), 0.0)
            dnb = dnum.astype(BF16)
            dnv = _dot_nt(dnb, vb)
            dv = _dot_tn(sm.astype(BF16), dnb)
            dnc = _dot(dnb, cb)
            dc_prev = _dot_tn((inter * dnum).astype(BF16), qb)
            dn_prev = jnp.sum((inter * dden) * qh, axis=0, keepdims=True)
            yield
            ds = dnv + dden
            dqk = (ds * dmat).astype(BF16)
            dqk_k = _dot(dqk, kb)
            dk = _dot_tn(dqk, qb)
            for _ in range(per_step - 1 - s):
                yield
            if terms is None:
                terms = state_terms(s, h, c_h, n_row, vb, kb)
            dcn, dnn, amat, kdc, ddecay = terms
            d_start = (decay * dcn + dc_prev, decay * dnn + dn_prev)
            if s > 0:
                d_state[s - 1][h] = d_start
            else:
                dc_scr[h] = d_start[0]
                dn_scr[h:h + 1, :] = d_start[1]
            yield
            gmat = ds * sm
            row_g = jnp.sum(gmat, axis=-1, keepdims=True)
            col_g_rows[s].append(jnp.where(row == h, jnp.sum(gmat, axis=0, keepdims=True), 0.0))
            gcol = inter * (jnp.sum(dnum * f["cq"], axis=-1, keepdims=True) + dden * f["nq"])
            dw = jnp.sum(amat * kh, axis=-1, keepdims=True)
            e = dw * w_c
            db_last = ddecay * decay + jnp.sum(e, axis=0, keepdims=True)
            dig_parts[s].append(jnp.where(lane == h, e, 0.0))
            db_parts[s].append(
                jnp.where(lane == N_HEADS + h, row_g + gcol - e + jnp.where(last, db_last, 0.0), 0.0))
            yield
            dq = dqk_k + inter * (dnc + dden * n_row)
            dp_ref[rows, 2048 + h * HEAD_DIM:2048 + (h + 1) * HEAD_DIM] = (dv + w_c * kdc).astype(BF16)
            dapad[rows, lanes] = dq * dsilu_q
            dapad[rows, klanes] = (dk + w_c * amat) * scale_k * dsilu_k

        chains = []
        for s in reversed(range(per_step)):
            gpre = g_ref[s * CHUNK:(s + 1) * CHUNK, :] + bg_ref[...]
            forms[s] = (gpre,) + _gate_forms(gpre)
            for h in range(N_HEADS):
                if s == per_step - 1:
                    d_state[s][h] = (dc_scr[h], dn_scr[h:h + 1, :])
                chains.append(head(s, h))
        _in_lockstep(chains)

        for s in range(per_step):
            rows = slice(s * CHUNK, (s + 1) * CHUNK)
            gpre, utri = forms[s][0], forms[s][2]
            cs_t = sum(col_g_rows[s][1:], col_g_rows[s][0]).T
            dig_all = sum(dig_parts[s][1:], dig_parts[s][0]) + cs_t
            db_cols = sum(db_parts[s][1:], db_parts[s][0])
            shifted = jnp.zeros((CHUNK, 128), F32)
            for h in range(N_HEADS):
                shifted = shifted + jnp.where(lane == N_HEADS + h, cs_t[:, h:h + 1], 0.0)
            dlf = _dot_f32(utri, db_cols - shifted)
            dgates = dig_all + dlf * _sigmoid(-gpre)
            dp_ref[rows, N_MAIN:N_MAIN + 128] = dgates.astype(BF16)
            dbg_ref[...] += jnp.sum(dgates, axis=0, keepdims=True)
        dp_ref[:, N_MAIN + 128:N_PAD] = jnp.zeros((blk, N_PAD - N_MAIN - 128), BF16)

        da_pad = dapad[...]
        da = da_pad[0:blk, :]
        dcb_ref[...] += jnp.sum(da, axis=0, keepdims=True)
        x = qk_ref[...]
        dx = jnp.zeros((blk, 1024), F32)
        for j in range(CONV_WIDTH):
            da_j = _rows_ahead(da_pad, CONV_WIDTH - 1 - j)[0:blk, :]
            dcw_ref[j:j + 1, :] += jnp.sum(da_j * x, axis=0, keepdims=True)
            dx = dx + cw_ref[j:j + 1, :] * da_j
        dp_ref[:, 1024:2048] = dx.astype(BF16)
        dapad[blk:blk + CONV_HALO, :] = dapad[0:CONV_HALO, :]

    bmap = lambda i: n_blocks - 1 - i
    wide = pl.BlockSpec((blk, 1024), lambda i: (bmap(i), 0))
    state = pl.BlockSpec((per_step, 8, 128), lambda i: (bmap(i), 0, 0))
    in_specs = [
        pl.BlockSpec((blk, 512), lambda i: (bmap(i), 1)),
        pl.BlockSpec((blk, 1024), lambda i: (bmap(i), 1)),
        pl.BlockSpec((blk, 512), lambda i: (bmap(i), 4)),
        pl.BlockSpec((blk, 512), lambda i: (bmap(i), 5)),
        pl.BlockSpec((blk, 512), lambda i: (bmap(i), 6)),
        pl.BlockSpec((blk, 128), lambda i: (bmap(i), 0)),
        wide, wide,
        pl.BlockSpec((blk, D_POOL), lambda i: (bmap(i), 0)),
        pl.BlockSpec((per_step, N_HEADS, HEAD_DIM, HEAD_DIM), lambda i: (bmap(i), 0, 0, 0)),
        state, state,
        pl.BlockSpec((1, 8, 128), lambda i: (jnp.minimum((bmap(i) + 1) * per_step, n_chunks - 1), 0, 0)),
        _full((1, 128)), _full((8, 1024)), _full((4, 128, 128)), _full((1, 512)), _full((1, 512))]
    return pl.pallas_call(
        body, name="mix_bwd", grid=(n_blocks,),
        out_shape=(jax.ShapeDtypeStruct((seq, N_PAD), BF16), jax.ShapeDtypeStruct((8, 1024), F32),
                   jax.ShapeDtypeStruct((1, 1024), F32), jax.ShapeDtypeStruct((4, 128, 128), F32),
                   jax.ShapeDtypeStruct((1, 512), F32), jax.ShapeDtypeStruct((1, 512), F32),
                   jax.ShapeDtypeStruct((1, 128), F32)),
        in_specs=in_specs,
        out_specs=(pl.BlockSpec((blk, N_PAD), lambda i: (bmap(i), 0)), _full((8, 1024)), _full((1, 1024)),
                   _full((4, 128, 128)), _full((1, 512)), _full((1, 512)), _full((1, 128))),
        scratch_shapes=[pltpu.VMEM((N_HEADS, HEAD_DIM, HEAD_DIM), F32), pltpu.VMEM((8, 128), F32),
                        pltpu.VMEM((blk + CONV_HALO, 1024), F32), pltpu.VMEM((blk + POOL_HALO, D_POOL), F32)],
        compiler_params=_params(("arbitrary",)),
    )(proj, proj, proj, proj, proj, gates, dmix, conv_a, pooled, cst, nst, mst, mst, bg_pad, conv_w8,
      w_pool, ls_pool, mh_g)


def _bwd_in(dproj, w_in_t, x, dx2, norm_g, scale):
    seq = x.shape[0]
    tm = min(512, seq)
    sub = min(256, tm)

    def body(dp_ref, wt_ref, x_ref, dx2_ref, ng_ref, sc_ref, gx_ref, dsh_ref, dsc_ref, dng_ref):
        @pl.when(pl.program_id(0) == 0)
        def _():
            dsh_ref[...] = jnp.zeros_like(dsh_ref)
            dsc_ref[...] = jnp.zeros_like(dsc_ref)
            dng_ref[...] = jnp.zeros_like(dng_ref)

        ng = ng_ref[...]
        one_sc = 1.0 + sc_ref[...]

        def chain(n):
            rows = slice(n * sub, (n + 1) * sub)
            dh = _dot(dp_ref[rows, :], wt_ref[...])
            yield
            xt = x_ref[rows, :]
            r = lax.rsqrt(jnp.mean(xt * xt, axis=-1, keepdims=True) + EPS)
            xn = xt * r
            dsh_ref[...] += jnp.sum(dh, axis=0, keepdims=True)
            dhxn = dh * xn
            dsc_ref[...] += jnp.sum(dhxn * ng, axis=0, keepdims=True)
            dng_ref[...] += jnp.sum(dhxn * one_sc, axis=0, keepdims=True)
            dxn = dh * (ng * one_sc)
            gx_ref[rows, :] = r * (dxn - xn * jnp.mean(dxn * xn, axis=-1, keepdims=True)) + dx2_ref[rows, :]

        _in_lockstep(chain(n) for n in range(tm // sub))

    tile = pl.BlockSpec((tm, D_MODEL), lambda i: (i, 0))
    vec = _full((1, D_MODEL))
    return pl.pallas_call(
        body, name="bwd_in", grid=(seq // tm,),
        out_shape=(jax.ShapeDtypeStruct((seq, D_MODEL), F32),) + (jax.ShapeDtypeStruct((1, D_MODEL), F32),) * 3,
        in_specs=[pl.BlockSpec((tm, N_PAD), lambda i: (i, 0)), _full((N_PAD, D_MODEL)), tile, tile, vec, vec],
        out_specs=(tile, vec, vec, vec),
        compiler_params=_params(("arbitrary",)),
    )(dproj, w_in_t, x, dx2, norm_g, scale)


def _dw_in(h_b, dproj):
    seq = h_b.shape[0]
    tk = min(4096, seq)
    tn = 768
    n_t = seq // tk

    def body(h_ref, dp_ref, dwt_ref, acc):
        t = pl.program_id(1)

        @pl.when(t == 0)
        def _():
            acc[...] = jnp.zeros_like(acc)

        acc[...] += _dot_tn(dp_ref[...], h_ref[...])

        @pl.when(t == n_t - 1)
        def _():
            dwt_ref[...] = acc[...].astype(BF16)

    return pl.pallas_call(
        body, name="dw_in", grid=(N_PAD // tn, n_t),
        out_shape=jax.ShapeDtypeStruct((N_PAD, D_MODEL), BF16),
        in_specs=[pl.BlockSpec((tk, D_MODEL), lambda j, t: (t, 0)), pl.BlockSpec((tk, tn), lambda j, t: (t, j))],
        out_specs=pl.BlockSpec((tn, D_MODEL), lambda j, t: (j, 0)),
        scratch_shapes=[pltpu.VMEM((tn, D_MODEL), F32)],
        compiler_params=_params(("arbitrary", "arbitrary")),
    )(h_b, dproj)


def _adam_update(g, w, m, v, g_ref, d_ref, m_ref, v_ref):
    mn = ADAM_B1 * m + (1.0 - ADAM_B1) * g
    vn = ADAM_B2 * v + (1.0 - ADAM_B2) * (g * g)
    m_hat = mn / (1.0 - ADAM_B1 ** ADAM_STEP)
    v_hat = vn / (1.0 - ADAM_B2 ** ADAM_STEP)
    g_ref[...] = g
    d_ref[...] = -ADAM_LR * (m_hat / (jnp.sqrt(v_hat) + ADAM_EPS) + ADAM_WD * w)
    m_ref[...] = mn
    v_ref[...] = vn


def _adam_sum(name, parts, w, m, v, row_tile, col_tile=None):
    rows, cols = w.shape
    col_tile = cols if col_tile is None else col_tile
    n_parts = parts.shape[0]

    def body(p_ref, w_ref, m_ref, v_ref, g_out, d_out, m_out, v_out):
        g = p_ref[0].astype(F32)
        for j in range(1, n_parts):
            g = g + p_ref[j].astype(F32)
        _adam_update(g, w_ref[...], m_ref[...], v_ref[...], g_out, d_out, m_out, v_out)

    tile = pl.BlockSpec((row_tile, col_tile), lambda i, j: (i, j))
    return pl.pallas_call(
        body, name=name, grid=(rows // row_tile, cols // col_tile),
        out_shape=(jax.ShapeDtypeStruct((rows, cols), F32),) * 4,
        in_specs=[pl.BlockSpec((n_parts, row_tile, col_tile), lambda i, j: (0, i, j)), tile, tile, tile],
        out_specs=(tile,) * 4,
        compiler_params=_params(("arbitrary", "arbitrary")),
    )(parts, w, m, v)


def _adam_ada(sc_all16, dmod_blk16, w, m, v):
    rows, cols = w.shape

    def body(sc_ref, dm_ref, w_ref, m_ref, v_ref, g_out, d_out, m_out, v_out):
        g = _dot_tn(sc_ref[...].astype(BF16), dm_ref[...].astype(BF16))
        _adam_update(g, w_ref[...], m_ref[...], v_ref[...], g_out, d_out, m_out, v_out)

    return pl.pallas_call(
        body, name="adam_w_ada", grid=(1,),
        out_shape=(jax.ShapeDtypeStruct((rows, cols), F32),) * 4,
        in_specs=[_full(sc_all16.shape), _full(dmod_blk16.shape)] + [_full((rows, cols))] * 3,
        out_specs=(_full((rows, cols)),) * 4,
        compiler_params=_params(("arbitrary",)),
    )(sc_all16, dmod_blk16, w, m, v)


def _adam_small(parts, loss_parts, w, m, v):
    names = list(w)
    n = len(names)

    def body(*refs):
        p_refs, loss_ref = refs[:n], refs[n]
        w_refs, m_refs, v_refs = (refs[n + 1 + k * n:n + 1 + (k + 1) * n] for k in range(3))
        outs = refs[3 * n + n + 1:]
        for a in range(n):
            g = p_refs[a][0]
            for j in range(1, N_DEV):
                g = g + p_refs[a][j]
            width = w_refs[a].shape[-1]
            if g.shape[-1] != width:
                g = g[..., 0:width]
            _adam_update(g, w_refs[a][...], m_refs[a][...], v_refs[a][...], *outs[4 * a:4 * a + 4])
        total = loss_ref[0]
        for j in range(1, N_DEV):
            total = total + loss_ref[j]
        outs[4 * n][...] = total

    args = [parts[k] for k in names] + [loss_parts] + [d[k] for d in (w, m, v) for k in names]
    out_shape = tuple(jax.ShapeDtypeStruct(w[k].shape, F32) for k in names for _ in range(4))
    out_shape += (jax.ShapeDtypeStruct(loss_parts.shape[1:], F32),)
    out = pl.pallas_call(
        body, name="adam_small", grid=(1,), out_shape=out_shape,
        in_specs=[_full(a.shape) for a in args], out_specs=tuple(_full(s.shape) for s in out_shape),
        compiler_params=_params(("arbitrary",)),
    )(*args)
    return {k: out[4 * a:4 * a + 4] for a, k in enumerate(names)}, out[4 * n]


def _local_step(x2, tgt2, shift, scale, gate, norm_g, w_in_t, w_out_b, conv_w, conv_b, w_pool, ls_pool,
                mh_norm_g, b_gates, final_g, send_dw_out=None, send_dw_in=None):
    bg_pad = jnp.pad(b_gates, ((0, 0), (0, 128 - b_gates.shape[1])))
    conv_w8 = jnp.pad(conv_w, ((0, 8 - CONV_WIDTH), (0, 0)))
    fg = final_g.reshape(1, D_MODEL)

    proj, gates, h_b = _fwd_proj(x2, norm_g, scale, shift, w_in_t)
    mix, cst, nst, mst, conv_a, pooled = _mix_fwd(proj, gates, bg_pad, conv_w8, conv_b, w_pool, ls_pool, mh_norm_g)
    dx2, dmix, dwo, dgate, dfg, loss = _out_fwd_bwd(mix, x2, tgt2, w_out_b, gate, fg)
    if send_dw_out is not None:
        bg_pad = bg_pad + send_dw_out(dwo)
    dproj, dcw8, dcb, dwp, dls, dmhg, dbg = _mix_bwd(proj, gates, dmix, conv_a, pooled, cst, nst, mst, bg_pad,
                                                      conv_w8, w_pool, ls_pool, mh_norm_g)
    dw_in_t = _dw_in(h_b, dproj)[:N_IN]
    ng_in = norm_g
    if send_dw_in is not None:
        ng_in = norm_g + send_dw_in(dw_in_t, dcw8[:CONV_WIDTH])
    gx, dsh, dsc, dng = _bwd_in(dproj, w_in_t, x2, dx2, ng_in, scale)
    return dict(loss=loss, grad_x=gx, dw_in_t=dw_in_t, dw_out=dwo, dconv_w=dcw8[:CONV_WIDTH], conv_b=dcb,
                w_pool=dwp, ls_pool=dls, mh_norm_g=dmhg, b_gates=dbg, final_g=dfg, norm_g=dng,
                dmod=jnp.concatenate([dsh, dsc, dgate], axis=1))


def kernel(x, c, norm_g, w_ada, b_ada, w_in, b_gates, conv_w, conv_b, w_pool, ls_pool, mh_norm_g, w_out, final_g, loss_target, m_norm_g, m_w_ada, m_b_ada, m_w_in, m_b_gates, m_conv_w, m_conv_b, m_w_pool, m_ls_pool, m_mh_norm_g, m_w_out, m_final_g, v_norm_g, v_w_ada, v_b_ada, v_w_in, v_b_gates, v_conv_w, v_conv_b, v_w_pool, v_ls_pool, v_mh_norm_g, v_w_out, v_final_g):
    seq = x.shape[1]
    me = 4 * lax.axis_index("x") + 2 * lax.axis_index("y") + lax.axis_index("c")

    shard_t = jnp.pad(w_in[0].astype(BF16).T, ((0, SHARD_ROWS - N_SHARD), (0, 0)))
    g_in, g_out, g_cw, g_c = _all_gather("gather_weights", shard_t, w_out[0].astype(BF16), conv_w[0], c,
                                         split={0: (SHARD_ROWS // 32 * 16,)})
    w_in_t = jnp.pad(g_in[:, :N_SHARD].reshape(N_IN, D_MODEL), ((0, N_PAD - N_IN), (0, 0)))
    w_out_b = g_out.reshape(D_MODEL, D_MODEL)
    conv_w_full = jnp.transpose(g_cw, (1, 0, 2)).reshape(CONV_WIDTH, 2 * D_MLSTM)
    c_all16 = jnp.pad(g_c.reshape(N_DEV, D_MODEL), ((0, 8), (0, 0)))

    b_ada_blk = lax.dynamic_slice(b_ada, (0, me * ADA_SHARD), (1, ADA_SHARD))
    mod_all, sc_all16 = _ada_mod(c_all16, w_ada[0], b_ada_blk)
    mod = lax.dynamic_index_in_dim(mod_all, me, axis=1, keepdims=False).reshape(1, 3 * D_MODEL)
    shift, scale, gate = mod[:, :D_MODEL], mod[:, D_MODEL:2 * D_MODEL], mod[:, 2 * D_MODEL:]

    flights = {}

    def send_dw_out(dwo):
        blocks = dwo.reshape(N_DEV, D_MODEL // N_DEV, D_MODEL)
        flights["out"], token = _scatter_start("send_dw_out", (blocks,))
        return token

    def send_dw_in(dw_in_t, dcw):
        blocks = dw_in_t.reshape(N_DEV, N_SHARD, D_MODEL)
        dcw_blocks = jnp.transpose(dcw.reshape(CONV_WIDTH, N_DEV, 128), (1, 0, 2))
        flights["in"], token = _scatter_start("send_dw_in", (blocks, dcw_blocks))
        return token

    r = _local_step(x[0], loss_target[0], shift, scale, gate, norm_g, w_in_t, w_out_b, conv_w_full, conv_b,
                    w_pool[0], ls_pool, mh_norm_g, b_gates, final_g, send_dw_out, send_dw_in)

    small_names = ("norm_g", "b_ada", "b_gates", "conv_b", "w_pool", "ls_pool", "mh_norm_g", "final_g")
    small_grads = dict(norm_g=r["norm_g"], b_ada=r["dmod"], b_gates=r["b_gates"], conv_b=r["conv_b"],
                       w_pool=r["w_pool"], ls_pool=r["ls_pool"], mh_norm_g=r["mh_norm_g"], final_g=r["final_g"])
    flights["small"], _ = _scatter_start(
        "send_small", (r["loss"],) + tuple(small_grads[k] for k in small_names), whole=True)
    p_in, p_cw = _scatter_wait("recv_dw_in", flights["in"], flights["small"][2 + 1 + small_names.index("w_pool")])
    (p_out,) = _scatter_wait("recv_dw_out", flights["out"], p_cw)

    in_t = _adam_sum("adam_w_in", p_in, w_in[0].T, m_w_in[0].T, v_w_in[0].T, N_SHARD, 256)
    gi, di, mi, vi = (o.T for o in in_t)
    go, do_, mo, vo = _adam_sum("adam_w_out", p_out, w_out[0], m_w_out[0], v_w_out[0], 128)
    gc, dc, mc, vc = _adam_sum("adam_conv_w", p_cw, conv_w[0], m_conv_w[0], v_conv_w[0], CONV_WIDTH)
    gathered = _scatter_wait("recv_small", flights["small"], go, whole=True)
    p_loss, p_small = gathered[0], dict(zip(small_names, gathered[1:]))

    def plain(norm_g_, b_ada_, b_gates_, conv_b_, w_pool_, ls_pool_, mh_norm_g_, final_g_):
        return dict(norm_g=norm_g_, b_ada=b_ada_, b_gates=b_gates_, conv_b=conv_b_, w_pool=w_pool_[0],
                    ls_pool=ls_pool_, mh_norm_g=mh_norm_g_, final_g=final_g_.reshape(1, D_MODEL))

    small, loss_row = _adam_small(
        p_small, p_loss,
        plain(norm_g, b_ada, b_gates, conv_b, w_pool, ls_pool, mh_norm_g, final_g),
        plain(m_norm_g, m_b_ada, m_b_gates, m_conv_b, m_w_pool, m_ls_pool, m_mh_norm_g, m_final_g),
        plain(v_norm_g, v_b_ada, v_b_gates, v_conv_b, v_w_pool, v_ls_pool, v_mh_norm_g, v_final_g))

    dmod_all = p_small["b_ada"].reshape(N_DEV, 3 * D_MODEL)
    dmod_blk16 = jnp.pad(lax.dynamic_slice(dmod_all, (0, me * ADA_SHARD), (N_DEV, ADA_SHARD)), ((0, 8), (0, 0)))
    ga, da, ma, va = _adam_ada(sc_all16, dmod_blk16, w_ada[0], m_w_ada[0], v_w_ada[0])

    names = ("norm_g", "w_ada", "b_ada", "w_in", "b_gates", "conv_w", "conv_b", "w_pool", "ls_pool", "mh_norm_g",
             "w_out", "final_g")
    shapes = dict(norm_g=norm_g.shape, b_ada=b_ada.shape, b_gates=b_gates.shape, conv_b=conv_b.shape,
                  w_pool=w_pool.shape, ls_pool=ls_pool.shape, mh_norm_g=mh_norm_g.shape, final_g=final_g.shape)
    sharded = dict(w_ada=(ga, da, ma, va), w_in=(gi, di, mi, vi), conv_w=(gc, dc, mc, vc), w_out=(go, do_, mo, vo))
    outs = []
    for kind in range(4):
        for nm in names:
            if nm in sharded:
                outs.append(sharded[nm][kind][None])
            else:
                outs.append(small[nm][kind].reshape(shapes[nm]))
    loss = loss_row[0, 0]
    grad_x = r["grad_x"].reshape(1, seq, D_MODEL)
    return (loss, grad_x, *outs)
```

```python
import jax
import jax.numpy as jnp
from jax import lax
from jax.experimental import pallas as pl
from jax.experimental.pallas import tpu as pltpu

F32 = jnp.float32
BF16 = jnp.bfloat16

D_MODEL = 1024
D_POOL = 512
D_MLSTM = 512
N_HEADS = 4
HEAD_DIM = 128
CHUNK = 128
POOL_WINDOWS = (2, 4, 8, 16)
POOL_GROUP_DIM = 128
CONV_WIDTH = 4
EPS = 1e-6
N_MAIN = 3584
N_IN = 3592
N_PAD = 3840
N_SHARD = N_IN // 8
ADA_SHARD = 3 * D_MODEL // 8
N_DEV = 8
CONV_HALO = 8
POOL_HALO = 16
NEG_BIG = -1e30
VMEM_LIMIT_BYTES = 56 * 1024 * 1024

ADAM_LR = 0.001
ADAM_B1 = 0.9
ADAM_B2 = 0.999
ADAM_EPS = 1e-08
ADAM_WD = 0.01
ADAM_STEP = 10

def _dot(a, b):
    return jnp.dot(a, b, preferred_element_type=F32)


def _dot_nt(a, b):
    return lax.dot_general(a, b, (((1,), (1,)), ((), ())), preferred_element_type=F32)


def _dot_tn(a, b):
    return lax.dot_general(a, b, (((0,), (0,)), ((), ())), preferred_element_type=F32)


def _dot_f32(a, b):
    return jnp.dot(a, b, precision=lax.Precision.HIGHEST, preferred_element_type=F32)


def _row_mean_mxu(x):
    return _dot(x.astype(BF16), jnp.full((HEAD_DIM, HEAD_DIM), 1.0 / HEAD_DIM, BF16))


def _sigmoid(x):
    return jax.nn.sigmoid(x)


def _log_sigmoid(x):
    return jnp.minimum(x, 0.0) - jnp.log1p(jnp.exp(-jnp.abs(x)))


def _params(sem):
    return pltpu.CompilerParams(dimension_semantics=sem, vmem_limit_bytes=VMEM_LIMIT_BYTES)


def _full(shape):
    n = len(shape)
    return pl.BlockSpec(shape, lambda *_: (0,) * n)


def _mesh_pos():
    return lax.axis_index("x"), lax.axis_index("y"), lax.axis_index("c")


def _peer(k):
    x, y, c = _mesh_pos()
    px = 1 - x if (k >> 2) & 1 else x
    py = 1 - y if (k >> 1) & 1 else y
    pc = 1 - c if k & 1 else c
    return (px, py, pc), 4 * px + 2 * py + pc


def _remote(src, dst, send_sem, recv_sem, to):
    return pltpu.make_async_remote_copy(src_ref=src, dst_ref=dst, send_sem=send_sem, recv_sem=recv_sem, device_id=to,
                                        device_id_type=pl.DeviceIdType.MESH)


def _two_level_gather(src, dst, send_sems, recv_sems, local_sems):
    n = len(src)
    x, y, c = _mesh_pos()
    me = 4 * x + 2 * y + c
    sibling = (x, y, 1 - c)
    south = c == 0
    near = (jnp.where(south, 1 - x, x), jnp.where(south, y, 1 - y))
    far = (jnp.where(south, x, 1 - x), jnp.where(south, 1 - y, y))
    diag = (1 - x, 1 - y)

    def block_of(chip, core):
        return 4 * chip[0] + 2 * chip[1] + core

    def copy(a, k, block, to, own=False):
        return _remote(src[a] if own else dst[a].at[block], dst[a].at[block], send_sems.at[a, k], recv_sems.at[a, k], to)

    local = [pltpu.make_async_copy(src[a], dst[a].at[me], local_sems.at[a]) for a in range(n)]
    sent = [copy(a, 0, me, sibling, True) for a in range(n)]
    sent += [copy(a, 1, me, (*near, c), True) for a in range(n)]
    sent += [copy(a, 2, me, (*far, c), True) for a in range(n)]
    for cp in local + sent:
        cp.start()
    yield
    for a in range(n):
        copy(a, 1, block_of(near, c), sibling).wait_recv()
        sent += [copy(a, 3, block_of(near, c), (*far, c)), copy(a, 4, block_of(near, c), sibling)]
        sent[-2].start()
        sent[-1].start()
    for k, chip in ((2, far), (3, diag)):
        for a in range(n):
            copy(a, k, block_of(chip, c), sibling).wait_recv()
            sent.append(copy(a, 3 + k, block_of(chip, c), sibling))
            sent[-1].start()
    for k, chip in ((0, (x, y)), (4, far), (5, near), (6, diag)):
        for a in range(n):
            copy(a, k, block_of(chip, 1 - c), sibling).wait_recv()
    for cp in sent:
        cp.wait_send()
    for cp in local:
        cp.wait()


GATHER_COPIES = 7


def _swap_with_all(buf, send_sems, recv_sems):
    x, y, c = _mesh_pos()
    me = 4 * x + 2 * y + c
    copies = [_remote(buf.at[me], buf.at[me], send_sems.at[k - 1], recv_sems.at[k - 1], _peer(k)[0])
              for k in range(1, N_DEV)]
    for cp in copies:
        cp.start()
    for cp in copies:
        cp.wait()


def _gather_weights_and_mod(shards, c_row, w_ada_blk, b_ada_blk):
    n = len(shards)

    def body(*refs):
        src, (c_ref, w_ref, b_ref) = refs[:n], refs[n:n + 3]
        dst, (mod_ref, sc_ref) = refs[n + 3:2 * n + 3], refs[2 * n + 3:2 * n + 5]
        c_all, g_send, g_recv, g_local, c_send, c_recv, m_send, m_recv = refs[2 * n + 5:]
        x, y, c = _mesh_pos()
        me = 4 * x + 2 * y + c
        gather = _two_level_gather(src, dst, g_send, g_recv, g_local)
        next(gather)
        c_all[me] = c_ref[...]
        _swap_with_all(c_all, c_send, c_recv)
        cv = jnp.concatenate([c_all[j] for j in range(N_DEV)] + [jnp.zeros((N_DEV, D_MODEL), F32)], axis=0)
        sc = cv * _sigmoid(cv)
        sc_ref[...] = sc
        blk = _dot(sc.astype(BF16), w_ref[...].astype(BF16)) + b_ref[...]
        mod_ref[me] = blk[0:N_DEV, :]
        _swap_with_all(mod_ref, m_send, m_recv)
        for _ in gather:
            pass

    hbm = pl.BlockSpec(memory_space=pltpu.HBM)
    vmem = pl.BlockSpec(memory_space=pltpu.VMEM)
    peers = pltpu.SemaphoreType.DMA((N_DEV - 1,))
    return pl.pallas_call(
        body, name="gather_weights",
        out_shape=tuple(jax.ShapeDtypeStruct((N_DEV,) + s.shape, s.dtype) for s in shards)
        + (jax.ShapeDtypeStruct((N_DEV, N_DEV, ADA_SHARD), F32), jax.ShapeDtypeStruct((2 * N_DEV, D_MODEL), F32)),
        in_specs=[hbm] * n + [vmem] * 3, out_specs=tuple([hbm] * n + [vmem] * 2),
        scratch_shapes=[pltpu.VMEM((N_DEV, 1, D_MODEL), F32),
                        pltpu.SemaphoreType.DMA((n, GATHER_COPIES)), pltpu.SemaphoreType.DMA((n, GATHER_COPIES)),
                        pltpu.SemaphoreType.DMA((n,)), peers, peers, peers, peers],
    )(*shards, c_row, w_ada_blk, b_ada_blk)


def _scatter_copies(src, land, send_sems, recv_sems, whole=False):
    x, y, c = _mesh_pos()
    me = 4 * x + 2 * y + c
    copies = []
    for k in range(1, N_DEV):
        peer, p = _peer(k)
        for a in range(len(src)):
            i = a * (N_DEV - 1) + k - 1
            copies.append(_remote(src[a] if whole else src[a].at[p], land[a].at[me], send_sems.at[i],
                                  recv_sems.at[i], peer))
    return copies


def _scatter_start(name, blocks, whole=False):
    n = len(blocks)

    def body(*refs):
        src, land = refs[:n], refs[n:2 * n]
        send_sems, recv_sems = refs[2 * n], refs[2 * n + 1]
        token_ref = refs[-1]
        for cp in _scatter_copies(src, land, send_sems, recv_sems, whole):
            cp.start()
        token_ref[...] = jnp.zeros_like(token_ref)

    hbm = pl.BlockSpec(memory_space=pltpu.HBM)
    sem = pl.BlockSpec(memory_space=pltpu.SEMAPHORE)
    landing = [((N_DEV,) + b.shape if whole else b.shape, b.dtype) for b in blocks]
    through = tuple(pltpu.HBM(b.shape, b.dtype) for b in blocks) + tuple(pltpu.HBM(s, d) for s, d in landing)
    args = [pltpu.with_memory_space_constraint(b, pltpu.HBM) for b in blocks]
    args += [pltpu.with_memory_space_constraint(lax.empty(s, d), pltpu.HBM) for s, d in landing]
    out = pl.pallas_call(
        body, name=name,
        out_shape=(pltpu.SemaphoreType.DMA((n * (N_DEV - 1),)),) * 2 + through
        + (jax.ShapeDtypeStruct((8, 128), F32),),
        in_specs=[hbm] * (2 * n),
        out_specs=(sem, sem) + (hbm,) * (2 * n) + (pl.BlockSpec(memory_space=pltpu.VMEM),),
        input_output_aliases={i: 2 + i for i in range(2 * n)},
        compiler_params=pltpu.CompilerParams(has_side_effects=pltpu.SideEffectType.DATAFLOW_SIDE_EFFECTING),
    )(*args)
    return out[:-1], out[-1][0:1, 0:1]


def _scatter_wait(name, state, after, whole=False):
    n = (len(state) - 2) // 2
    send_sems, recv_sems = state[0], state[1]
    src, land = state[2:2 + n], state[2 + n:]

    def body(*refs):
        src_r, land_r = refs[:n], refs[n:2 * n]
        for cp in _scatter_copies(src_r, land_r, refs[2 * n], refs[2 * n + 1], whole):
            cp.wait_send()
            cp.wait_recv()

    hbm = pl.BlockSpec(memory_space=pltpu.HBM)
    sem = pl.BlockSpec(memory_space=pltpu.SEMAPHORE)
    out = pl.pallas_call(
        body, name=name,
        out_shape=tuple(pltpu.HBM(b.shape, b.dtype) for b in src + land),
        in_specs=[hbm] * (2 * n) + [sem, sem, pl.BlockSpec(memory_space=pl.ANY)],
        out_specs=(hbm,) * (2 * n),
        input_output_aliases={i: i for i in range(2 * n)},
        compiler_params=pltpu.CompilerParams(has_side_effects=pltpu.SideEffectType.DATAFLOW_SIDE_EFFECTING),
    )(*src, *land, send_sems, recv_sems, after)
    me = 4 * lax.axis_index("x") + 2 * lax.axis_index("y") + lax.axis_index("c")
    landed = []
    for a in range(n):
        own = out[a][None] if whole else lax.dynamic_index_in_dim(out[a], me, axis=0, keepdims=True)
        landed.append(lax.dynamic_update_slice_in_dim(out[n + a], own, me, axis=0))
    return landed


def _fwd_proj(x, norm_g, scale, shift, w_in_t):
    seq = x.shape[0]
    tm = min(512, seq)
    sub = min(256, tm)
    tn = 512

    def body(x_ref, ng_ref, sc_ref, sh_ref, wt_ref, proj_ref, gates_ref, h_ref):
        def chain(n):
            for _ in range(n):
                yield
            rows = slice(n * sub, (n + 1) * sub)
            xt = x_ref[rows, :]
            r = lax.rsqrt(jnp.mean(xt * xt, axis=-1, keepdims=True) + EPS)
            h = ((xt * r) * ng_ref[...]) * (1.0 + sc_ref[...]) + sh_ref[...]
            hb = h.astype(BF16)
            h_ref[rows, :] = hb
            yield
            gates_ref[rows, :] = _dot_nt(hb, wt_ref[N_MAIN:N_MAIN + 128, :])
            for j in range(N_MAIN // tn):
                proj_ref[rows, j * tn:(j + 1) * tn] = _dot_nt(hb, wt_ref[j * tn:(j + 1) * tn, :])

        _in_lockstep(chain(n) for n in range(tm // sub))

    vec = _full((1, D_MODEL))
    tile = pl.BlockSpec((tm, D_MODEL), lambda i: (i, 0))
    return pl.pallas_call(
        body, name="fwd_proj", grid=(seq // tm,),
        out_shape=(jax.ShapeDtypeStruct((seq, N_MAIN), F32), jax.ShapeDtypeStruct((seq, 128), F32),
                   jax.ShapeDtypeStruct((seq, D_MODEL), BF16)),
        in_specs=[tile, vec, vec, vec, _full((N_PAD, D_MODEL))],
        out_specs=(pl.BlockSpec((tm, N_MAIN), lambda i: (i, 0)), pl.BlockSpec((tm, 128), lambda i: (i, 0)), tile),
        compiler_params=_params(("arbitrary",)),
    )(x, norm_g, scale, shift, w_in_t)


def _gate_forms(gpre):
    r = lax.broadcasted_iota(jnp.int32, (CHUNK, CHUNK), 0)
    c = lax.broadcasted_iota(jnp.int32, (CHUNK, CHUNK), 1)
    causal = c <= r
    ltri = jnp.where(causal, 1.0, 0.0).astype(F32)
    utri = jnp.where(r <= c, 1.0, 0.0).astype(F32)
    bcol = _dot_f32(ltri, _log_sigmoid(gpre))
    gt8 = gpre.T[0:8, :]
    brow = _dot_f32(_log_sigmoid(gt8), utri)
    return causal, utri, bcol, gt8, brow


def _in_lockstep(stages):
    alive = list(stages)
    while alive:
        still = []
        for g in alive:
            try:
                next(g)
                still.append(g)
            except StopIteration:
                pass
        alive = still


def _head_fwd(qh, kh, vh, bc, br, igr, m_prev, c_h, n_row, causal):
    qb, kb, vb, cb = qh.astype(BF16), kh.astype(BF16), vh.astype(BF16), c_h.astype(BF16)
    qk = _dot_nt(qb, kb)
    cq = _dot_nt(qb, cb)
    nq = _dot_nt(qb, jnp.broadcast_to(n_row.astype(BF16), (HEAD_DIM, HEAD_DIM)))
    yield
    dlog = jnp.where(causal, bc - br + igr, NEG_BIG)
    inter_log = bc + m_prev
    m_t = jnp.maximum(inter_log, jnp.max(dlog, axis=-1, keepdims=True))
    yield
    dmat = jnp.exp(dlog - m_t)
    inter = jnp.exp(inter_log - m_t)
    s = qk * dmat
    sv = _dot(s.astype(BF16), vb)
    yield
    den = jnp.sum(s, axis=-1, keepdims=True) + inter * nq
    emt = jnp.exp(-m_t)
    yield
    num = sv + inter * cq
    dn = jnp.maximum(jnp.abs(den), emt)
    hm = num / dn
    return dict(dmat=dmat, inter=inter, qb=qb, kb=kb, vb=vb, cb=cb, s=s, cq=cq, nq=nq, den=den, emt=emt,
                dn=dn, hm=hm)


def _state_weights(bc, igc, m_prev, m_new=None):
    last = lax.broadcasted_iota(jnp.int32, (CHUNK, 1), 0) == CHUNK - 1
    b_last = jnp.sum(jnp.where(last, bc, 0.0), axis=0, keepdims=True)
    wlog = b_last - bc + igc
    if m_new is None:
        m_new = jnp.maximum(b_last + m_prev, jnp.max(wlog, axis=0, keepdims=True))
    w_c = jnp.exp(wlog - m_new)
    decay = jnp.exp(b_last + m_prev - m_new)
    return w_c, decay, m_new, last


def _rows_back(x, k):
    return x if k == 0 else pltpu.roll(x, k, 0)


def _rows_ahead(x, k):
    return x if k == 0 else pltpu.roll(x, x.shape[0] - k, 0)


def _conv_taps(xpad):
    return [_rows_back(xpad, CONV_WIDTH - 1 - j)[CONV_HALO:, :] for j in range(CONV_WIDTH)]


def _conv_pre(taps, cw_ref, cb_ref):
    a = cb_ref[...]
    for j in range(CONV_WIDTH):
        a = a + cw_ref[j:j + 1, :] * taps[j]
    return a


def _window_sum(x, w, shift):
    k = 1
    while k < w:
        x = x + shift(x, k)
        k *= 2
    return x


def _pool_window_sum(upad_ref, g, w):
    lanes = slice(g * POOL_GROUP_DIM, (g + 1) * POOL_GROUP_DIM)
    return _window_sum(upad_ref[:, lanes], w, _rows_back)[POOL_HALO:, :]


def _pool_inv_count(row0, rows, w):
    pos = row0 + lax.broadcasted_iota(jnp.int32, (rows, 1), 0) + 1
    return 1.0 / jnp.minimum(pos, w).astype(F32)


FWD_CHUNKS = 2
BWD_CHUNKS = 4


def _mix_fwd(proj, gates, bg_pad, conv_w8, conv_b, w_pool, ls_pool, mh_g):
    seq = proj.shape[0]
    n_chunks = seq // CHUNK
    per_step = FWD_CHUNKS
    blk = per_step * CHUNK

    def body(uz_ref, qk_ref, v_ref, o_ref, zm_ref, uh_ref, qkh_ref, g_ref, bg_ref, cw_ref, cb_ref, wp_ref,
             ls_ref, mhg_ref, mix_ref, cst_ref, nst_ref, mst_ref, a_ref, pooled_ref, c_scr, n_scr, m_scr, xpad, upad):
        i = pl.program_id(0)

        @pl.when(i == 0)
        def _():
            c_scr[...] = jnp.zeros_like(c_scr)
            n_scr[...] = jnp.zeros_like(n_scr)
            m_scr[...] = jnp.zeros_like(m_scr)

        first = i == 0

        upad[0:POOL_HALO, :] = jnp.where(first, 0.0, uh_ref[...])
        upad[POOL_HALO:POOL_HALO + blk, :] = uz_ref[:, 0:D_POOL]
        for g, w in enumerate(POOL_WINDOWS):
            lanes = slice(g * POOL_GROUP_DIM, (g + 1) * POOL_GROUP_DIM)
            pooled = (_pool_window_sum(upad, g, w) * _pool_inv_count(i * blk, blk, w) - uz_ref[:, lanes]).astype(BF16)
            pooled_ref[:, lanes] = pooled
            y = _dot(pooled, wp_ref[g].astype(BF16)) * ls_ref[:, lanes]
            zp = uz_ref[:, D_POOL + g * POOL_GROUP_DIM:D_POOL + (g + 1) * POOL_GROUP_DIM]
            mix_ref[:, lanes] = (y * (zp * _sigmoid(zp))).astype(BF16)

        xpad[0:CONV_HALO, :] = jnp.where(first, 0.0, qkh_ref[...])
        xpad[CONV_HALO:CONV_HALO + blk, :] = qk_ref[...]
        a = _conv_pre(_conv_taps(xpad[...]), cw_ref, cb_ref)
        a_ref[...] = a
        qk = a * _sigmoid(a)

        def head(rows, h, qh, kh, vh, bc, br, igr, m_prev, c_h, n_row, causal):
            lanes = slice(h * HEAD_DIM, (h + 1) * HEAD_DIM)
            f = yield from _head_fwd(qh, kh, vh, bc, br, igr, m_prev, c_h, n_row, causal)
            yield
            hm = f["hm"]
            hn = hm * lax.rsqrt(_row_mean_mxu(hm * hm) + EPS) * mhg_ref[:, lanes]
            zm = zm_ref[rows, lanes]
            out = hn * _sigmoid(o_ref[rows, lanes]) * (zm * _sigmoid(zm))
            mix_ref[rows, D_POOL + h * HEAD_DIM:D_POOL + (h + 1) * HEAD_DIM] = out.astype(BF16)

        c_cur = [c_scr[h] for h in range(N_HEADS)]
        n_cur = [n_scr[h:h + 1, :] for h in range(N_HEADS)]
        m_cur = [m_scr[h:h + 1, 0:1] for h in range(N_HEADS)]
        chains = []
        for s in range(per_step):
            rows = slice(s * CHUNK, (s + 1) * CHUNK)
            gpre = g_ref[rows, :] + bg_ref[...]
            causal, _, bcol, gt8, brow = _gate_forms(gpre)
            nst_ref[s] = jnp.zeros((8, 128), F32)
            mst_ref[s] = jnp.zeros((8, 128), F32)
            for h in range(N_HEADS):
                lanes = slice(h * HEAD_DIM, (h + 1) * HEAD_DIM)
                cst_ref[s, h] = c_cur[h]
                nst_ref[s, h:h + 1, :] = n_cur[h]
                mst_ref[s, h:h + 1, :] = jnp.broadcast_to(m_cur[h], (1, 128))
                qh = qk[rows, lanes]
                kh = qk[rows, D_MLSTM + h * HEAD_DIM:D_MLSTM + (h + 1) * HEAD_DIM] * (HEAD_DIM ** -0.5)
                vh = v_ref[rows, lanes]
                bc = bcol[:, N_HEADS + h:N_HEADS + h + 1]
                br = brow[N_HEADS + h:N_HEADS + h + 1, :]
                igr = gt8[h:h + 1, :]
                igc = gpre[:, h:h + 1]
                chains.append(head(rows, h, qh, kh, vh, bc, br, igr, m_cur[h], c_cur[h], n_cur[h], causal))
                w_c, decay, m_new, _ = _state_weights(bc, igc, m_cur[h])
                c_cur[h] = decay * c_cur[h] + _dot_tn((vh * w_c).astype(BF16), kh.astype(BF16))
                n_cur[h] = decay * n_cur[h] + jnp.sum(w_c * kh, axis=0, keepdims=True)
                m_cur[h] = m_new
        for h in range(N_HEADS):
            c_scr[h] = c_cur[h]
            n_scr[h:h + 1, :] = n_cur[h]
            m_scr[h:h + 1, :] = jnp.broadcast_to(m_cur[h], (1, 128))
        _in_lockstep(chains)

    in_specs = [
        pl.BlockSpec((blk, 1024), lambda i: (i, 0)),
        pl.BlockSpec((blk, 1024), lambda i: (i, 1)),
        pl.BlockSpec((blk, 512), lambda i: (i, 4)),
        pl.BlockSpec((blk, 512), lambda i: (i, 5)),
        pl.BlockSpec((blk, 512), lambda i: (i, 6)),
        pl.BlockSpec((POOL_HALO, 512), lambda i: (jnp.maximum(i * (blk // POOL_HALO) - 1, 0), 0)),
        pl.BlockSpec((CONV_HALO, 1024), lambda i: (jnp.maximum(i * (blk // CONV_HALO) - 1, 0), 1)),
        pl.BlockSpec((blk, 128), lambda i: (i, 0)),
        _full((1, 128)), _full((8, 1024)), _full((1, 1024)), _full((4, 128, 128)), _full((1, 512)),
        _full((1, 512))]
    return pl.pallas_call(
        body, name="mix_fwd", grid=(n_chunks // per_step,),
        out_shape=(jax.ShapeDtypeStruct((seq, D_MODEL), BF16),
                   jax.ShapeDtypeStruct((n_chunks, N_HEADS, HEAD_DIM, HEAD_DIM), F32),
                   jax.ShapeDtypeStruct((n_chunks, 8, 128), F32),
                   jax.ShapeDtypeStruct((n_chunks, 8, 128), F32),
                   jax.ShapeDtypeStruct((seq, 2 * D_MLSTM), F32),
                   jax.ShapeDtypeStruct((seq, D_POOL), BF16)),
        in_specs=in_specs,
        out_specs=(pl.BlockSpec((blk, D_MODEL), lambda i: (i, 0)),
                   pl.BlockSpec((per_step, N_HEADS, HEAD_DIM, HEAD_DIM), lambda i: (i, 0, 0, 0)),
                   pl.BlockSpec((per_step, 8, 128), lambda i: (i, 0, 0)),
                   pl.BlockSpec((per_step, 8, 128), lambda i: (i, 0, 0)),
                   pl.BlockSpec((blk, 2 * D_MLSTM), lambda i: (i, 0)),
                   pl.BlockSpec((blk, D_POOL), lambda i: (i, 0))),
        scratch_shapes=[pltpu.VMEM((N_HEADS, HEAD_DIM, HEAD_DIM), F32), pltpu.VMEM((8, 128), F32),
                        pltpu.VMEM((8, 128), F32), pltpu.VMEM((CONV_HALO + blk, 1024), F32),
                        pltpu.VMEM((POOL_HALO + blk, D_POOL), F32)],
        compiler_params=_params(("arbitrary",)),
    )(proj, proj, proj, proj, proj, proj, proj, gates, bg_pad, conv_w8, conv_b, w_pool, ls_pool, mh_g)


def _out_fwd_bwd(mix, x, tgt, w_out_b, gate, final_g):
    seq = x.shape[0]
    tm = min(512, seq)
    sub = min(256, tm)

    def body(mix_ref, x_ref, t_ref, w_ref, gate_ref, fg_ref, dx2_ref, dmix_ref, dwo_ref, dgate_ref, dfg_ref,
             loss_ref, dwo_acc):
        @pl.when(pl.program_id(0) == 0)
        def _():
            dwo_acc[...] = jnp.zeros_like(dwo_acc)
            dgate_ref[...] = jnp.zeros_like(dgate_ref)
            dfg_ref[...] = jnp.zeros_like(dfg_ref)
            loss_ref[...] = jnp.zeros_like(loss_ref)

        w = w_ref[...]
        gate_v = gate_ref[...]
        fg = fg_ref[...]
        do2_parts = [None] * (tm // sub)

        def chain(n):
            rows = slice(n * sub, (n + 1) * sub)
            o2 = _dot(mix_ref[rows, :], w)
            yield
            x2 = x_ref[rows, :] + gate_v * o2
            r2 = lax.rsqrt(jnp.mean(x2 * x2, axis=-1, keepdims=True) + EPS)
            x2n = x2 * r2
            err = x2n * fg - t_ref[rows, :]
            part = 0.5 * jnp.sum(jnp.sum(err * err, axis=-1, keepdims=True), axis=0, keepdims=True) / D_MODEL
            loss_ref[...] += jnp.broadcast_to(part, loss_ref.shape)
            dy = err / D_MODEL
            dfg_ref[...] += jnp.sum(dy * x2n, axis=0, keepdims=True)
            gdy = dy * fg
            dx2 = r2 * (gdy - x2n * jnp.mean(gdy * x2n, axis=-1, keepdims=True))
            dx2_ref[rows, :] = dx2
            dgate_ref[...] += jnp.sum(dx2 * o2, axis=0, keepdims=True)
            do2 = (dx2 * gate_v).astype(BF16)
            dmix_ref[rows, :] = _dot_nt(do2, w)
            do2_parts[n] = do2

        _in_lockstep(chain(n) for n in range(tm // sub))
        dwo_acc[...] += _dot_tn(mix_ref[...], jnp.concatenate(do2_parts, axis=0))

        @pl.when(pl.program_id(0) == seq // tm - 1)
        def _():
            dwo_ref[...] = dwo_acc[...].astype(BF16)

    tile = pl.BlockSpec((tm, D_MODEL), lambda i: (i, 0))
    vec = _full((1, D_MODEL))
    return pl.pallas_call(
        body, name="out_fwd_bwd", grid=(seq // tm,),
        out_shape=(jax.ShapeDtypeStruct((seq, D_MODEL), F32), jax.ShapeDtypeStruct((seq, D_MODEL), F32),
                   jax.ShapeDtypeStruct((D_MODEL, D_MODEL), BF16), jax.ShapeDtypeStruct((1, D_MODEL), F32),
                   jax.ShapeDtypeStruct((1, D_MODEL), F32), jax.ShapeDtypeStruct((1, 128), F32)),
        in_specs=[tile, tile, tile, _full((D_MODEL, D_MODEL)), vec, vec],
        out_specs=(tile, tile, _full((D_MODEL, D_MODEL)), vec, vec, _full((1, 128))),
        scratch_shapes=[pltpu.VMEM((D_MODEL, D_MODEL), F32)],
        compiler_params=_params(("arbitrary",)),
    )(mix, x, tgt, w_out_b, gate, final_g)


def _mix_bwd(proj, gates, dmix, conv_a, pooled, cst, nst, mst, bg_pad, conv_w8, w_pool, ls_pool, mh_g):
    seq = proj.shape[0]
    n_chunks = seq // CHUNK
    per_step = BWD_CHUNKS
    blk = per_step * CHUNK
    n_blocks = n_chunks // per_step

    def body(zp_ref, qk_ref, v_ref, o_ref, zm_ref, g_ref, dmix_ref, a_ref, pooled_ref, cst_ref, nst_ref, mst_ref,
             mnx_ref, bg_ref, cw_ref, wp_ref, ls_ref, mhg_ref,
             dp_ref, dcw_ref, dcb_ref, dwp_ref, dls_ref, dmhg_ref, dbg_ref,
             dc_scr, dn_scr, dapad, dpipad):
        i = pl.program_id(0)
        bi = n_blocks - 1 - i

        @pl.when(i == 0)
        def _():
            for ref in (dc_scr, dn_scr, dcw_ref, dcb_ref, dwp_ref, dls_ref, dmhg_ref, dbg_ref):
                ref[...] = jnp.zeros_like(ref)
            dapad[blk:blk + CONV_HALO, :] = jnp.zeros((CONV_HALO, 1024), F32)
            dpipad[blk:blk + POOL_HALO, :] = jnp.zeros((POOL_HALO, D_POOL), F32)

        dpooled = []
        for g, w in enumerate(POOL_WINDOWS):
            lanes = slice(g * POOL_GROUP_DIM, (g + 1) * POOL_GROUP_DIM)
            zlanes = slice(D_POOL + g * POOL_GROUP_DIM, D_POOL + (g + 1) * POOL_GROUP_DIM)
            inv = _pool_inv_count(bi * blk, blk, w)
            pb = pooled_ref[:, lanes]
            wpb = wp_ref[g].astype(BF16)
            yw = _dot(pb, wpb)
            ls = ls_ref[:, lanes]
            zp = zp_ref[:, lanes]
            sg = _sigmoid(zp)
            dpo = dmix_ref[:, lanes]
            dp_ref[:, zlanes] = (dpo * (yw * ls) * (sg * (1.0 + zp * (1.0 - sg)))).astype(BF16)
            dy = dpo * (zp * sg)
            dls_ref[:, lanes] += jnp.sum(dy * yw, axis=0, keepdims=True)
            dyw = (dy * ls).astype(BF16)
            dwp_ref[g] += _dot_tn(pb, dyw)
            dpl = _dot_nt(dyw, wpb)
            dpooled.append(dpl)
            dpipad[0:blk, lanes] = dpl * inv
        for g, w in enumerate(POOL_WINDOWS):
            lanes = slice(g * POOL_GROUP_DIM, (g + 1) * POOL_GROUP_DIM)
            du = _window_sum(dpipad[:, lanes], w, _rows_ahead)[0:blk, :] - dpooled[g]
            dp_ref[:, lanes] = du.astype(BF16)
        dpipad[blk:blk + POOL_HALO, :] = dpipad[0:POOL_HALO, :]

        def silu_and_slope(rows, cols):
            a = a_ref[rows, cols]
            sg = _sigmoid(a)
            return a * sg, sg * (1.0 + a * (1.0 - sg))

        lane = lax.broadcasted_iota(jnp.int32, (CHUNK, 128), 1)
        row = lax.broadcasted_iota(jnp.int32, (CHUNK, 128), 0)
        scale_k = HEAD_DIM ** -0.5
        forms = [None] * per_step
        col_g_rows = [[] for _ in range(per_step)]
        dig_parts = [[] for _ in range(per_step)]
        db_parts = [[] for _ in range(per_step)]
        d_state = [[None] * N_HEADS for _ in range(per_step)]

        def state_terms(s, h, c_h, n_row, vb, kb):
            dcn, dnn = d_state[s][h]
            dcnb = dcn.astype(BF16)
            amat = _dot(vb, dcnb) + dnn
            kdc = _dot_nt(kb, dcnb)
            ddecay = (jnp.sum(jnp.sum(dcn * c_h, axis=-1, keepdims=True), axis=0, keepdims=True)
                      + jnp.sum(dnn * n_row, axis=-1, keepdims=True))
            return dcn, dnn, amat, kdc, ddecay

        def head(s, h):
            rows = slice(s * CHUNK, (s + 1) * CHUNK)
            lanes = slice(h * HEAD_DIM, (h + 1) * HEAD_DIM)
            klanes = slice(D_MLSTM + h * HEAD_DIM, D_MLSTM + (h + 1) * HEAD_DIM)
            gpre, causal, utri, bcol, gt8, brow = forms[s]
            qh, dsilu_q = silu_and_slope(rows, lanes)
            kh, dsilu_k = silu_and_slope(rows, klanes)
            kh = kh * scale_k
            vh = v_ref[rows, lanes]
            bc = bcol[:, N_HEADS + h:N_HEADS + h + 1]
            br = brow[N_HEADS + h:N_HEADS + h + 1, :]
            igr = gt8[h:h + 1, :]
            igc = gpre[:, h:h + 1]
            m_prev = mst_ref[s, h:h + 1, 0:1]
            m_next = mnx_ref[0, h:h + 1, 0:1] if s == per_step - 1 else mst_ref[s + 1, h:h + 1, 0:1]
            c_h = cst_ref[s, h]
            n_row = nst_ref[s, h:h + 1, :]
            w_c, decay, _, last = _state_weights(bc, igc, m_prev, m_next)
            terms = None
            if s == per_step - 1:
                terms = state_terms(s, h, c_h, n_row, vh.astype(BF16), kh.astype(BF16))
            f = yield from _head_fwd(qh, kh, vh, bc, br, igr, m_prev, c_h, n_row, causal)
            qb, kb, vb, cb = f["qb"], f["kb"], f["vb"], f["cb"]
            sm, dmat, inter, den, dn, hm = f["s"], f["dmat"], f["inter"], f["den"], f["dn"], f["hm"]
            yield

            rinv = lax.rsqrt(_row_mean_mxu(hm * hm) + EPS)
            hmn = hm * rinv
            gh = mhg_ref[:, lanes]
            o_pre = o_ref[rows, lanes]
            og = _sigmoid(o_pre)
            zm = zm_ref[rows, lanes]
            sgz = _sigmoid(zm)
            sz = zm * sgz
            dout = dmix_ref[rows, D_POOL + h * HEAD_DIM:D_POOL + (h + 1) * HEAD_DIM]
            hn = hmn * gh
            dp_ref[rows, 2560 + h * HEAD_DIM:2560 + (h + 1) * HEAD_DIM] = (
                dout * hn * sz * og * (1.0 - og)).astype(BF16)
            dp_ref[rows, 3072 + h * HEAD_DIM:3072 + (h + 1) * HEAD_DIM] = (
                dout * hn * og * (sgz * (1.0 + zm * (1.0 - sgz)))).astype(BF16)
            dhn = dout * og * sz
            dmhg_ref[:, lanes] += jnp.sum(dhn * hmn, axis=0, keepdims=True)
            dyn = dhn * gh
            dhm = rinv * (dyn - hmn * _row_mean_mxu(dyn * hmn))
            yield

            inv_dn = 1.0 / dn
            dnum = dhm * inv_dn
            hd = jnp.sum(dhm * hm, axis=-1, keepdims=True)
            dden = jnp.where(jnp.abs(den) > f["emt"], -hd * inv_dn * jnp.sign(den), 0.0)
            dnb = dnum.astype(BF16)
            dnv = _dot_nt(dnb, vb)
            dv = _dot_tn(sm.astype(BF16), dnb)
            dnc = _dot(dnb, cb)
            dc_prev = _dot_tn((inter * dnum).astype(BF16), qb)
            dn_prev = jnp.sum((inter * dden) * qh, axis=0, keepdims=True)
            yield
            ds = dnv + dden
            dqk = (ds * dmat).astype(BF16)
            dqk_k = _dot(dqk, kb)
            dk = _dot_tn(dqk, qb)
            for _ in range(per_step - 1 - s):
                yield
            if terms is None:
                terms = state_terms(s, h, c_h, n_row, vb, kb)
            dcn, dnn, amat, kdc, ddecay = terms
            d_start = (decay * dcn + dc_prev, decay * dnn + dn_prev)
            if s > 0:
                d_state[s - 1][h] = d_start
            else:
                dc_scr[h] = d_start[0]
                dn_scr[h:h + 1, :] = d_start[1]
            yield
            gmat = ds * sm
            row_g = jnp.sum(gmat, axis=-1, keepdims=True)
            col_g_rows[s].append(jnp.where(row == h, jnp.sum(gmat, axis=0, keepdims=True), 0.0))
            gcol = inter * (jnp.sum(dnum * f["cq"], axis=-1, keepdims=True) + dden * f["nq"])
            dw = jnp.sum(amat * kh, axis=-1, keepdims=True)
            e = dw * w_c
            db_last = ddecay * decay + jnp.sum(e, axis=0, keepdims=True)
            dig_parts[s].append(jnp.where(lane == h, e, 0.0))
            db_parts[s].append(
                jnp.where(lane == N_HEADS + h, row_g + gcol - e + jnp.where(last, db_last, 0.0), 0.0))
            yield
            dq = dqk_k + inter * (dnc + dden * n_row)
            dp_ref[rows, 2048 + h * HEAD_DIM:2048 + (h + 1) * HEAD_DIM] = (dv + w_c * kdc).astype(BF16)
            dapad[rows, lanes] = dq * dsilu_q
            dapad[rows, klanes] = (dk + w_c * amat) * scale_k * dsilu_k

        chains = []
        for s in reversed(range(per_step)):
            gpre = g_ref[s * CHUNK:(s + 1) * CHUNK, :] + bg_ref[...]
            forms[s] = (gpre,) + _gate_forms(gpre)
            for h in range(N_HEADS):
                if s == per_step - 1:
                    d_state[s][h] = (dc_scr[h], dn_scr[h:h + 1, :])
                chains.append(head(s, h))
        _in_lockstep(chains)

        for s in range(per_step):
            rows = slice(s * CHUNK, (s + 1) * CHUNK)
            gpre, utri = forms[s][0], forms[s][2]
            cs_t = sum(col_g_rows[s][1:], col_g_rows[s][0]).T
            dig_all = sum(dig_parts[s][1:], dig_parts[s][0]) + cs_t
            db_cols = sum(db_parts[s][1:], db_parts[s][0])
            shifted = jnp.zeros((CHUNK, 128), F32)
            for h in range(N_HEADS):
                shifted = shifted + jnp.where(lane == N_HEADS + h, cs_t[:, h:h + 1], 0.0)
            dlf = _dot_f32(utri, db_cols - shifted)
            dgates = dig_all + dlf * _sigmoid(-gpre)
            dp_ref[rows, N_MAIN:N_MAIN + 128] = dgates.astype(BF16)
            dbg_ref[...] += jnp.sum(dgates, axis=0, keepdims=True)
        dp_ref[:, N_MAIN + 128:N_PAD] = jnp.zeros((blk, N_PAD - N_MAIN - 128), BF16)

        da_pad = dapad[...]
        da = da_pad[0:blk, :]
        dcb_ref[...] += jnp.sum(da, axis=0, keepdims=True)
        x = qk_ref[...]
        dx = jnp.zeros((blk, 1024), F32)
        for j in range(CONV_WIDTH):
            da_j = _rows_ahead(da_pad, CONV_WIDTH - 1 - j)[0:blk, :]
            dcw_ref[j:j + 1, :] += jnp.sum(da_j * x, axis=0, keepdims=True)
            dx = dx + cw_ref[j:j + 1, :] * da_j
        dp_ref[:, 1024:2048] = dx.astype(BF16)
        dapad[blk:blk + CONV_HALO, :] = dapad[0:CONV_HALO, :]

    bmap = lambda i: n_blocks - 1 - i
    wide = pl.BlockSpec((blk, 1024), lambda i: (bmap(i), 0))
    state = pl.BlockSpec((per_step, 8, 128), lambda i: (bmap(i), 0, 0))
    in_specs = [
        pl.BlockSpec((blk, 512), lambda i: (bmap(i), 1)),
        pl.BlockSpec((blk, 1024), lambda i: (bmap(i), 1)),
        pl.BlockSpec((blk, 512), lambda i: (bmap(i), 4)),
        pl.BlockSpec((blk, 512), lambda i: (bmap(i), 5)),
        pl.BlockSpec((blk, 512), lambda i: (bmap(i), 6)),
        pl.BlockSpec((blk, 128), lambda i: (bmap(i), 0)),
        wide, wide,
        pl.BlockSpec((blk, D_POOL), lambda i: (bmap(i), 0)),
        pl.BlockSpec((per_step, N_HEADS, HEAD_DIM, HEAD_DIM), lambda i: (bmap(i), 0, 0, 0)),
        state, state,
        pl.BlockSpec((1, 8, 128), lambda i: (jnp.minimum((bmap(i) + 1) * per_step, n_chunks - 1), 0, 0)),
        _full((1, 128)), _full((8, 1024)), _full((4, 128, 128)), _full((1, 512)), _full((1, 512))]
    return pl.pallas_call(
        body, name="mix_bwd", grid=(n_blocks,),
        out_shape=(jax.ShapeDtypeStruct((seq, N_PAD), BF16), jax.ShapeDtypeStruct((8, 1024), F32),
                   jax.ShapeDtypeStruct((1, 1024), F32), jax.ShapeDtypeStruct((4, 128, 128), F32),
                   jax.ShapeDtypeStruct((1, 512), F32), jax.ShapeDtypeStruct((1, 512), F32),
                   jax.ShapeDtypeStruct((1, 128), F32)),
        in_specs=in_specs,
        out_specs=(pl.BlockSpec((blk, N_PAD), lambda i: (bmap(i), 0)), _full((8, 1024)), _full((1, 1024)),
                   _full((4, 128, 128)), _full((1, 512)), _full((1, 512)), _full((1, 128))),
        scratch_shapes=[pltpu.VMEM((N_HEADS, HEAD_DIM, HEAD_DIM), F32), pltpu.VMEM((8, 128), F32),
                        pltpu.VMEM((blk + CONV_HALO, 1024), F32), pltpu.VMEM((blk + POOL_HALO, D_POOL), F32)],
        compiler_params=_params(("arbitrary",)),
    )(proj, proj, proj, proj, proj, gates, dmix, conv_a, pooled, cst, nst, mst, mst, bg_pad, conv_w8,
      w_pool, ls_pool, mh_g)


def _bwd_in(dproj, w_in_t, x, dx2, norm_g, scale):
    seq = x.shape[0]
    tm = min(512, seq)
    sub = min(256, tm)

    def body(dp_ref, wt_ref, x_ref, dx2_ref, ng_ref, sc_ref, gx_ref, dsh_ref, dsc_ref, dng_ref):
        @pl.when(pl.program_id(0) == 0)
        def _():
            dsh_ref[...] = jnp.zeros_like(dsh_ref)
            dsc_ref[...] = jnp.zeros_like(dsc_ref)
            dng_ref[...] = jnp.zeros_like(dng_ref)

        ng = ng_ref[...]
        one_sc = 1.0 + sc_ref[...]

        def chain(n):
            rows = slice(n * sub, (n + 1) * sub)
            dh = _dot(dp_ref[rows, :], wt_ref[...])
            yield
            xt = x_ref[rows, :]
            r = lax.rsqrt(jnp.mean(xt * xt, axis=-1, keepdims=True) + EPS)
            xn = xt * r
            dsh_ref[...] += jnp.sum(dh, axis=0, keepdims=True)
            dhxn = dh * xn
            dsc_ref[...] += jnp.sum(dhxn * ng, axis=0, keepdims=True)
            dng_ref[...] += jnp.sum(dhxn * one_sc, axis=0, keepdims=True)
            dxn = dh * (ng * one_sc)
            gx_ref[rows, :] = r * (dxn - xn * jnp.mean(dxn * xn, axis=-1, keepdims=True)) + dx2_ref[rows, :]

        _in_lockstep(chain(n) for n in range(tm // sub))

    tile = pl.BlockSpec((tm, D_MODEL), lambda i: (i, 0))
    vec = _full((1, D_MODEL))
    return pl.pallas_call(
        body, name="bwd_in", grid=(seq // tm,),
        out_shape=(jax.ShapeDtypeStruct((seq, D_MODEL), F32),) + (jax.ShapeDtypeStruct((1, D_MODEL), F32),) * 3,
        in_specs=[pl.BlockSpec((tm, N_PAD), lambda i: (i, 0)), _full((N_PAD, D_MODEL)), tile, tile, vec, vec],
        out_specs=(tile, vec, vec, vec),
        compiler_params=_params(("arbitrary",)),
    )(dproj, w_in_t, x, dx2, norm_g, scale)


def _dw_in(h_b, dproj):
    seq = h_b.shape[0]
    tk = min(4096, seq)
    tn = 768
    n_t = seq // tk

    def body(h_ref, dp_ref, dwt_ref, acc):
        t = pl.program_id(1)

        @pl.when(t == 0)
        def _():
            acc[...] = jnp.zeros_like(acc)

        acc[...] += _dot_tn(dp_ref[...], h_ref[...])

        @pl.when(t == n_t - 1)
        def _():
            dwt_ref[...] = acc[...].astype(BF16)

    return pl.pallas_call(
        body, name="dw_in", grid=(N_PAD // tn, n_t),
        out_shape=jax.ShapeDtypeStruct((N_PAD, D_MODEL), BF16),
        in_specs=[pl.BlockSpec((tk, D_MODEL), lambda j, t: (t, 0)), pl.BlockSpec((tk, tn), lambda j, t: (t, j))],
        out_specs=pl.BlockSpec((tn, D_MODEL), lambda j, t: (j, 0)),
        scratch_shapes=[pltpu.VMEM((tn, D_MODEL), F32)],
        compiler_params=_params(("arbitrary", "arbitrary")),
    )(h_b, dproj)


def _adam_update(g, w, m, v, g_ref, d_ref, m_ref, v_ref):
    mn = ADAM_B1 * m + (1.0 - ADAM_B1) * g
    vn = ADAM_B2 * v + (1.0 - ADAM_B2) * (g * g)
    m_hat = mn / (1.0 - ADAM_B1 ** ADAM_STEP)
    v_hat = vn / (1.0 - ADAM_B2 ** ADAM_STEP)
    g_ref[...] = g
    d_ref[...] = -ADAM_LR * (m_hat / (jnp.sqrt(v_hat) + ADAM_EPS) + ADAM_WD * w)
    m_ref[...] = mn
    v_ref[...] = vn


def _adam_sum(name, parts, w, m, v, row_tile, col_tile=None):
    rows, cols = w.shape
    col_tile = cols if col_tile is None else col_tile
    n_parts = parts.shape[0]

    def body(p_ref, w_ref, m_ref, v_ref, g_out, d_out, m_out, v_out):
        g = p_ref[0].astype(F32)
        for j in range(1, n_parts):
            g = g + p_ref[j].astype(F32)
        _adam_update(g, w_ref[...], m_ref[...], v_ref[...], g_out, d_out, m_out, v_out)

    tile = pl.BlockSpec((row_tile, col_tile), lambda i, j: (i, j))
    return pl.pallas_call(
        body, name=name, grid=(rows // row_tile, cols // col_tile),
        out_shape=(jax.ShapeDtypeStruct((rows, cols), F32),) * 4,
        in_specs=[pl.BlockSpec((n_parts, row_tile, col_tile), lambda i, j: (0, i, j)), tile, tile, tile],
        out_specs=(tile,) * 4,
        compiler_params=_params(("arbitrary", "arbitrary")),
    )(parts, w, m, v)


def _adam_ada(sc_all16, dmod_blk16, w, m, v):
    rows, cols = w.shape

    def body(sc_ref, dm_ref, w_ref, m_ref, v_ref, g_out, d_out, m_out, v_out):
        g = _dot_tn(sc_ref[...].astype(BF16), dm_ref[...].astype(BF16))
        _adam_update(g, w_ref[...], m_ref[...], v_ref[...], g_out, d_out, m_out, v_out)

    return pl.pallas_call(
        body, name="adam_w_ada", grid=(1,),
        out_shape=(jax.ShapeDtypeStruct((rows, cols), F32),) * 4,
        in_specs=[_full(sc_all16.shape), _full(dmod_blk16.shape)] + [_full((rows, cols))] * 3,
        out_specs=(_full((rows, cols)),) * 4,
        compiler_params=_params(("arbitrary",)),
    )(sc_all16, dmod_blk16, w, m, v)


def _adam_small(parts, loss_parts, w, m, v):
    names = list(w)
    n = len(names)

    def body(*refs):
        p_refs, loss_ref = refs[:n], refs[n]
        w_refs, m_refs, v_refs = (refs[n + 1 + k * n:n + 1 + (k + 1) * n] for k in range(3))
        outs = refs[3 * n + n + 1:]
        for a in range(n):
            g = p_refs[a][0]
            for j in range(1, N_DEV):
                g = g + p_refs[a][j]
            width = w_refs[a].shape[-1]
            if g.shape[-1] != width:
                g = g[..., 0:width]
            _adam_update(g, w_refs[a][...], m_refs[a][...], v_refs[a][...], *outs[4 * a:4 * a + 4])
        total = loss_ref[0]
        for j in range(1, N_DEV):
            total = total + loss_ref[j]
        outs[4 * n][...] = total

    args = [parts[k] for k in names] + [loss_parts] + [d[k] for d in (w, m, v) for k in names]
    out_shape = tuple(jax.ShapeDtypeStruct(w[k].shape, F32) for k in names for _ in range(4))
    out_shape += (jax.ShapeDtypeStruct(loss_parts.shape[1:], F32),)
    out = pl.pallas_call(
        body, name="adam_small", grid=(1,), out_shape=out_shape,
        in_specs=[_full(a.shape) for a in args], out_specs=tuple(_full(s.shape) for s in out_shape),
        compiler_params=_params(("arbitrary",)),
    )(*args)
    return {k: out[4 * a:4 * a + 4] for a, k in enumerate(names)}, out[4 * n]


def _local_step(x2, tgt2, shift, scale, gate, norm_g, w_in_t, w_out_b, conv_w, conv_b, w_pool, ls_pool,
                mh_norm_g, b_gates, final_g, send_dw_out=None, send_dw_in=None):
    bg_pad = jnp.pad(b_gates, ((0, 0), (0, 128 - b_gates.shape[1])))
    conv_w8 = jnp.pad(conv_w, ((0, 8 - CONV_WIDTH), (0, 0)))
    fg = final_g.reshape(1, D_MODEL)

    proj, gates, h_b = _fwd_proj(x2, norm_g, scale, shift, w_in_t)
    mix, cst, nst, mst, conv_a, pooled = _mix_fwd(proj, gates, bg_pad, conv_w8, conv_b, w_pool, ls_pool, mh_norm_g)
    dx2, dmix, dwo, dgate, dfg, loss = _out_fwd_bwd(mix, x2, tgt2, w_out_b, gate, fg)
    if send_dw_out is not None:
        bg_pad = bg_pad + send_dw_out(dwo)
    dproj, dcw8, dcb, dwp, dls, dmhg, dbg = _mix_bwd(proj, gates, dmix, conv_a, pooled, cst, nst, mst, bg_pad,
                                                      conv_w8, w_pool, ls_pool, mh_norm_g)
    dw_in_t = _dw_in(h_b, dproj)[:N_IN]
    ng_in = norm_g
    if send_dw_in is not None:
        ng_in = norm_g + send_dw_in(dw_in_t, dcw8[:CONV_WIDTH])
    gx, dsh, dsc, dng = _bwd_in(dproj, w_in_t, x2, dx2, ng_in, scale)
    return dict(loss=loss, grad_x=gx, dw_in_t=dw_in_t, dw_out=dwo, dconv_w=dcw8[:CONV_WIDTH], conv_b=dcb,
                w_pool=dwp, ls_pool=dls, mh_norm_g=dmhg, b_gates=dbg, final_g=dfg, norm_g=dng,
                dmod=jnp.concatenate([dsh, dsc, dgate], axis=1))


def kernel(x, c, norm_g, w_ada, b_ada, w_in, b_gates, conv_w, conv_b, w_pool, ls_pool, mh_norm_g, w_out, final_g, loss_target, m_norm_g, m_w_ada, m_b_ada, m_w_in, m_b_gates, m_conv_w, m_conv_b, m_w_pool, m_ls_pool, m_mh_norm_g, m_w_out, m_final_g, v_norm_g, v_w_ada, v_b_ada, v_w_in, v_b_gates, v_conv_w, v_conv_b, v_w_pool, v_ls_pool, v_mh_norm_g, v_w_out, v_final_g):
    seq = x.shape[1]
    me = 4 * lax.axis_index("x") + 2 * lax.axis_index("y") + lax.axis_index("c")

    b_ada_blk = lax.dynamic_slice(b_ada, (0, me * ADA_SHARD), (1, ADA_SHARD))
    g_in, g_out, g_cw, mod_all, sc_all16 = _gather_weights_and_mod(
        (w_in[0].astype(BF16).T, w_out[0].astype(BF16), conv_w[0]), c, w_ada[0], b_ada_blk)
    w_in_t = jnp.pad(g_in.reshape(N_IN, D_MODEL), ((0, N_PAD - N_IN), (0, 0)))
    w_out_b = g_out.reshape(D_MODEL, D_MODEL)
    conv_w_full = jnp.transpose(g_cw, (1, 0, 2)).reshape(CONV_WIDTH, 2 * D_MLSTM)
    mod = lax.dynamic_index_in_dim(mod_all, me, axis=1, keepdims=False).reshape(1, 3 * D_MODEL)
    shift, scale, gate = mod[:, :D_MODEL], mod[:, D_MODEL:2 * D_MODEL], mod[:, 2 * D_MODEL:]

    flights = {}

    def send_dw_out(dwo):
        blocks = dwo.reshape(N_DEV, D_MODEL // N_DEV, D_MODEL)
        flights["out"], token = _scatter_start("send_dw_out", (blocks,))
        return token

    def send_dw_in(dw_in_t, dcw):
        blocks = dw_in_t.reshape(N_DEV, N_SHARD, D_MODEL)
        dcw_blocks = jnp.transpose(dcw.reshape(CONV_WIDTH, N_DEV, 128), (1, 0, 2))
        flights["in"], token = _scatter_start("send_dw_in", (blocks, dcw_blocks))
        return token

    r = _local_step(x[0], loss_target[0], shift, scale, gate, norm_g, w_in_t, w_out_b, conv_w_full, conv_b,
                    w_pool[0], ls_pool, mh_norm_g, b_gates, final_g, send_dw_out, send_dw_in)

    small_names = ("norm_g", "b_ada", "b_gates", "conv_b", "w_pool", "ls_pool", "mh_norm_g", "final_g")
    small_grads = dict(norm_g=r["norm_g"], b_ada=r["dmod"], b_gates=r["b_gates"], conv_b=r["conv_b"],
                       w_pool=r["w_pool"], ls_pool=r["ls_pool"], mh_norm_g=r["mh_norm_g"], final_g=r["final_g"])
    flights["small"], _ = _scatter_start(
        "send_small", (r["loss"],) + tuple(small_grads[k] for k in small_names), whole=True)
    p_in, p_cw = _scatter_wait("recv_dw_in", flights["in"], flights["small"][2 + 1 + small_names.index("w_pool")])
    (p_out,) = _scatter_wait("recv_dw_out", flights["out"], p_cw)

    in_t = _adam_sum("adam_w_in", p_in, w_in[0].T, m_w_in[0].T, v_w_in[0].T, N_SHARD, 256)
    gi, di, mi, vi = (o.T for o in in_t)
    go, do_, mo, vo = _adam_sum("adam_w_out", p_out, w_out[0], m_w_out[0], v_w_out[0], 128)
    gc, dc, mc, vc = _adam_sum("adam_conv_w", p_cw, conv_w[0], m_conv_w[0], v_conv_w[0], CONV_WIDTH)
    gathered = _scatter_wait("recv_small", flights["small"], go, whole=True)
    p_loss, p_small = gathered[0], dict(zip(small_names, gathered[1:]))

    def plain(norm_g_, b_ada_, b_gates_, conv_b_, w_pool_, ls_pool_, mh_norm_g_, final_g_):
        return dict(norm_g=norm_g_, b_ada=b_ada_, b_gates=b_gates_, conv_b=conv_b_, w_pool=w_pool_[0],
                    ls_pool=ls_pool_, mh_norm_g=mh_norm_g_, final_g=final_g_.reshape(1, D_MODEL))

    small, loss_row = _adam_small(
        p_small, p_loss,
        plain(norm_g, b_ada, b_gates, conv_b, w_pool, ls_pool, mh_norm_g, final_g),
        plain(m_norm_g, m_b_ada, m_b_gates, m_conv_b, m_w_pool, m_ls_pool, m_mh_norm_g, m_final_g),
        plain(v_norm_g, v_b_ada, v_b_gates, v_conv_b, v_w_pool, v_ls_pool, v_mh_norm_g, v_final_g))

    dmod_all = p_small["b_ada"].reshape(N_DEV, 3 * D_MODEL)
    dmod_blk16 = jnp.pad(lax.dynamic_slice(dmod_all, (0, me * ADA_SHARD), (N_DEV, ADA_SHARD)), ((0, 8), (0, 0)))
    ga, da, ma, va = _adam_ada(sc_all16, dmod_blk16, w_ada[0], m_w_ada[0], v_w_ada[0])

    names = ("norm_g", "w_ada", "b_ada", "w_in", "b_gates", "conv_w", "conv_b", "w_pool", "ls_pool", "mh_norm_g",
             "w_out", "final_g")
    shapes = dict(norm_g=norm_g.shape, b_ada=b_ada.shape, b_gates=b_gates.shape, conv_b=conv_b.shape,
                  w_pool=w_pool.shape, ls_pool=ls_pool.shape, mh_norm_g=mh_norm_g.shape, final_g=final_g.shape)
    sharded = dict(w_ada=(ga, da, ma, va), w_in=(gi, di, mi, vi), conv_w=(gc, dc, mc, vc), w_out=(go, do_, mo, vo))
    outs = []
    for kind in range(4):
        for nm in names:
            if nm in sharded:
                outs.append(sharded[nm][kind][None])
            else:
                outs.append(small[nm][kind].reshape(shapes[nm]))
    loss = loss_row[0, 0]
    grad_x = r["grad_x"].reshape(1, seq, D_MODEL)
    return (loss, grad_x, *outs)
```

```python
import jax
import jax.numpy as jnp
from jax import lax
from jax.experimental import pallas as pl
from jax.experimental.pallas import tpu as pltpu

F32 = jnp.float32
BF16 = jnp.bfloat16

D_MODEL = 1024
D_POOL = 512
D_MLSTM = 512
N_HEADS = 4
HEAD_DIM = 128
CHUNK = 128
POOL_WINDOWS = (2, 4, 8, 16)
POOL_GROUP_DIM = 128
CONV_WIDTH = 4
EPS = 1e-6
N_MAIN = 3584
N_IN = 3592
N_PAD = 3840
N_SHARD = N_IN // 8
ADA_SHARD = 3 * D_MODEL // 8
N_DEV = 8
CONV_HALO = 8
POOL_HALO = 16
NEG_BIG = -1e30
VMEM_LIMIT_BYTES = 56 * 1024 * 1024

ADAM_LR = 0.001
ADAM_B1 = 0.9
ADAM_B2 = 0.999
ADAM_EPS = 1e-08
ADAM_WD = 0.01
ADAM_STEP = 10

def _dot(a, b):
    return jnp.dot(a, b, preferred_element_type=F32)


def _dot_nt(a, b):
    return lax.dot_general(a, b, (((1,), (1,)), ((), ())), preferred_element_type=F32)


def _dot_tn(a, b):
    return lax.dot_general(a, b, (((0,), (0,)), ((), ())), preferred_element_type=F32)


def _dot_f32(a, b):
    return jnp.dot(a, b, precision=lax.Precision.HIGHEST, preferred_element_type=F32)


def _row_mean_mxu(x):
    return _dot(x.astype(BF16), jnp.full((HEAD_DIM, HEAD_DIM), 1.0 / HEAD_DIM, BF16))


def _sigmoid(x):
    return jax.nn.sigmoid(x)


def _log_sigmoid(x):
    return jnp.minimum(x, 0.0) - jnp.log1p(jnp.exp(-jnp.abs(x)))


def _params(sem):
    return pltpu.CompilerParams(dimension_semantics=sem, vmem_limit_bytes=VMEM_LIMIT_BYTES)


def _full(shape):
    n = len(shape)
    return pl.BlockSpec(shape, lambda *_: (0,) * n)


def _mesh_pos():
    return lax.axis_index("x"), lax.axis_index("y"), lax.axis_index("c")


def _peer(k):
    x, y, c = _mesh_pos()
    px = 1 - x if (k >> 2) & 1 else x
    py = 1 - y if (k >> 1) & 1 else y
    pc = 1 - c if k & 1 else c
    return (px, py, pc), 4 * px + 2 * py + pc


def _remote(src, dst, send_sem, recv_sem, to):
    return pltpu.make_async_remote_copy(src_ref=src, dst_ref=dst, send_sem=send_sem, recv_sem=recv_sem, device_id=to,
                                        device_id_type=pl.DeviceIdType.MESH)


def _two_level_gather(src, dst, send_sems, recv_sems, local_sems):
    n = len(src)
    x, y, c = _mesh_pos()
    me = 4 * x + 2 * y + c
    sibling = (x, y, 1 - c)
    south = c == 0
    near = (jnp.where(south, 1 - x, x), jnp.where(south, y, 1 - y))
    far = (jnp.where(south, x, 1 - x), jnp.where(south, 1 - y, y))
    diag = (1 - x, 1 - y)

    def block_of(chip, core):
        return 4 * chip[0] + 2 * chip[1] + core

    def copy(a, k, block, to, own=False):
        return _remote(src[a] if own else dst[a].at[block], dst[a].at[block], send_sems.at[a, k], recv_sems.at[a, k], to)

    local = [pltpu.make_async_copy(src[a], dst[a].at[me], local_sems.at[a]) for a in range(n)]
    sent = [copy(a, 0, me, sibling, True) for a in range(n)]
    sent += [copy(a, 1, me, (*near, c), True) for a in range(n)]
    sent += [copy(a, 2, me, (*far, c), True) for a in range(n)]
    for cp in local + sent:
        cp.start()
    yield
    for a in range(n):
        copy(a, 1, block_of(near, c), sibling).wait_recv()
        sent += [copy(a, 3, block_of(near, c), (*far, c)), copy(a, 4, block_of(near, c), sibling)]
        sent[-2].start()
        sent[-1].start()
    for k, chip in ((2, far), (3, diag)):
        for a in range(n):
            copy(a, k, block_of(chip, c), sibling).wait_recv()
            sent.append(copy(a, 3 + k, block_of(chip, c), sibling))
            sent[-1].start()
    for k, chip in ((0, (x, y)), (4, far), (5, near), (6, diag)):
        for a in range(n):
            copy(a, k, block_of(chip, 1 - c), sibling).wait_recv()
    for cp in sent:
        cp.wait_send()
    for cp in local:
        cp.wait()


GATHER_COPIES = 7


def _swap_with_all(buf, send_sems, recv_sems):
    x, y, c = _mesh_pos()
    me = 4 * x + 2 * y + c
    copies = [_remote(buf.at[me], buf.at[me], send_sems.at[k - 1], recv_sems.at[k - 1], _peer(k)[0])
              for k in range(1, N_DEV)]
    for cp in copies:
        cp.start()
    for cp in copies:
        cp.wait()


def _gather_weights_and_mod(shards, c_row, w_ada_blk, b_ada_blk):
    n = len(shards)

    def body(*refs):
        src, (c_ref, w_ref, b_ref) = refs[:n], refs[n:n + 3]
        dst, (mod_ref, sc_ref) = refs[n + 3:2 * n + 3], refs[2 * n + 3:2 * n + 5]
        c_all, g_send, g_recv, g_local, c_send, c_recv, m_send, m_recv = refs[2 * n + 5:]
        x, y, c = _mesh_pos()
        me = 4 * x + 2 * y + c
        gather = _two_level_gather(src, dst, g_send, g_recv, g_local)
        next(gather)
        c_all[me] = c_ref[...]
        _swap_with_all(c_all, c_send, c_recv)
        cv = jnp.concatenate([c_all[j] for j in range(N_DEV)] + [jnp.zeros((N_DEV, D_MODEL), F32)], axis=0)
        sc = cv * _sigmoid(cv)
        sc_ref[...] = sc
        blk = _dot(sc.astype(BF16), w_ref[...].astype(BF16)) + b_ref[...]
        mod_ref[me] = blk[0:N_DEV, :]
        _swap_with_all(mod_ref, m_send, m_recv)
        for _ in gather:
            pass

    hbm = pl.BlockSpec(memory_space=pltpu.HBM)
    vmem = pl.BlockSpec(memory_space=pltpu.VMEM)
    peers = pltpu.SemaphoreType.DMA((N_DEV - 1,))
    return pl.pallas_call(
        body, name="gather_weights",
        out_shape=tuple(jax.ShapeDtypeStruct((N_DEV,) + s.shape, s.dtype) for s in shards)
        + (jax.ShapeDtypeStruct((N_DEV, N_DEV, ADA_SHARD), F32), jax.ShapeDtypeStruct((2 * N_DEV, D_MODEL), F32)),
        in_specs=[hbm] * n + [vmem] * 3, out_specs=tuple([hbm] * n + [vmem] * 2),
        scratch_shapes=[pltpu.VMEM((N_DEV, 1, D_MODEL), F32),
                        pltpu.SemaphoreType.DMA((n, GATHER_COPIES)), pltpu.SemaphoreType.DMA((n, GATHER_COPIES)),
                        pltpu.SemaphoreType.DMA((n,)), peers, peers, peers, peers],
    )(*shards, c_row, w_ada_blk, b_ada_blk)


def _scatter_copies(src, land, send_sems, recv_sems, whole=False):
    x, y, c = _mesh_pos()
    me = 4 * x + 2 * y + c
    copies = []
    for k in range(1, N_DEV):
        peer, p = _peer(k)
        for a in range(len(src)):
            i = a * (N_DEV - 1) + k - 1
            copies.append(_remote(src[a] if whole else src[a].at[p], land[a].at[me], send_sems.at[i],
                                  recv_sems.at[i], peer))
    return copies


def _scatter_start(name, blocks, whole=False, after=None):
    n = len(blocks)
    extra = 0 if after is None else 1

    def body(*refs):
        src, land = refs[:n], refs[n:2 * n]
        send_sems, recv_sems = refs[2 * n + extra], refs[2 * n + extra + 1]
        token_ref = refs[-1]
        for cp in _scatter_copies(src, land, send_sems, recv_sems, whole):
            cp.start()
        token_ref[...] = jnp.zeros_like(token_ref)

    hbm = pl.BlockSpec(memory_space=pltpu.HBM)
    sem = pl.BlockSpec(memory_space=pltpu.SEMAPHORE)
    landing = [((N_DEV,) + b.shape if whole else b.shape, b.dtype) for b in blocks]
    through = tuple(pltpu.HBM(b.shape, b.dtype) for b in blocks) + tuple(pltpu.HBM(s, d) for s, d in landing)
    args = [pltpu.with_memory_space_constraint(b, pltpu.HBM) for b in blocks]
    args += [pltpu.with_memory_space_constraint(lax.empty(s, d), pltpu.HBM) for s, d in landing]
    out = pl.pallas_call(
        body, name=name,
        out_shape=(pltpu.SemaphoreType.DMA((n * (N_DEV - 1),)),) * 2 + through
        + (jax.ShapeDtypeStruct((8, 128), F32),),
        in_specs=[hbm] * (2 * n) + [pl.BlockSpec(memory_space=pl.ANY)] * extra,
        out_specs=(sem, sem) + (hbm,) * (2 * n) + (pl.BlockSpec(memory_space=pltpu.VMEM),),
        input_output_aliases={i: 2 + i for i in range(2 * n)},
        compiler_params=pltpu.CompilerParams(has_side_effects=pltpu.SideEffectType.DATAFLOW_SIDE_EFFECTING),
    )(*args, *([] if after is None else [after]))
    return out[:-1], out[-1][0:1, 0:1]


def _scatter_wait(name, state, after, whole=False):
    n = (len(state) - 2) // 2
    send_sems, recv_sems = state[0], state[1]
    src, land = state[2:2 + n], state[2 + n:]

    def body(*refs):
        src_r, land_r = refs[:n], refs[n:2 * n]
        for cp in _scatter_copies(src_r, land_r, refs[2 * n], refs[2 * n + 1], whole):
            cp.wait_send()
            cp.wait_recv()

    hbm = pl.BlockSpec(memory_space=pltpu.HBM)
    sem = pl.BlockSpec(memory_space=pltpu.SEMAPHORE)
    out = pl.pallas_call(
        body, name=name,
        out_shape=tuple(pltpu.HBM(b.shape, b.dtype) for b in src + land),
        in_specs=[hbm] * (2 * n) + [sem, sem, pl.BlockSpec(memory_space=pl.ANY)],
        out_specs=(hbm,) * (2 * n),
        input_output_aliases={i: i for i in range(2 * n)},
        compiler_params=pltpu.CompilerParams(has_side_effects=pltpu.SideEffectType.DATAFLOW_SIDE_EFFECTING),
    )(*src, *land, send_sems, recv_sems, after)
    me = 4 * lax.axis_index("x") + 2 * lax.axis_index("y") + lax.axis_index("c")
    landed = []
    for a in range(n):
        own = out[a][None] if whole else lax.dynamic_index_in_dim(out[a], me, axis=0, keepdims=True)
        landed.append(lax.dynamic_update_slice_in_dim(out[n + a], own, me, axis=0))
    return landed


def _fwd_proj(x, norm_g, scale, shift, w_in_t):
    seq = x.shape[0]
    tm = min(512, seq)
    sub = min(256, tm)
    tn = 512

    def body(x_ref, ng_ref, sc_ref, sh_ref, wt_ref, proj_ref, gates_ref, h_ref):
        def chain(n):
            for _ in range(n):
                yield
            rows = slice(n * sub, (n + 1) * sub)
            xt = x_ref[rows, :]
            r = lax.rsqrt(jnp.mean(xt * xt, axis=-1, keepdims=True) + EPS)
            h = ((xt * r) * ng_ref[...]) * (1.0 + sc_ref[...]) + sh_ref[...]
            hb = h.astype(BF16)
            h_ref[rows, :] = hb
            yield
            gates_ref[rows, :] = _dot_nt(hb, wt_ref[N_MAIN:N_MAIN + 128, :])
            for j in range(N_MAIN // tn):
                proj_ref[rows, j * tn:(j + 1) * tn] = _dot_nt(hb, wt_ref[j * tn:(j + 1) * tn, :])

        _in_lockstep(chain(n) for n in range(tm // sub))

    vec = _full((1, D_MODEL))
    tile = pl.BlockSpec((tm, D_MODEL), lambda i: (i, 0))
    return pl.pallas_call(
        body, name="fwd_proj", grid=(seq // tm,),
        out_shape=(jax.ShapeDtypeStruct((seq, N_MAIN), F32), jax.ShapeDtypeStruct((seq, 128), F32),
                   jax.ShapeDtypeStruct((seq, D_MODEL), BF16)),
        in_specs=[tile, vec, vec, vec, _full((N_PAD, D_MODEL))],
        out_specs=(pl.BlockSpec((tm, N_MAIN), lambda i: (i, 0)), pl.BlockSpec((tm, 128), lambda i: (i, 0)), tile),
        compiler_params=_params(("arbitrary",)),
    )(x, norm_g, scale, shift, w_in_t)


def _gate_forms(gpre):
    r = lax.broadcasted_iota(jnp.int32, (CHUNK, CHUNK), 0)
    c = lax.broadcasted_iota(jnp.int32, (CHUNK, CHUNK), 1)
    causal = c <= r
    ltri = jnp.where(causal, 1.0, 0.0).astype(F32)
    utri = jnp.where(r <= c, 1.0, 0.0).astype(F32)
    bcol = _dot_f32(ltri, _log_sigmoid(gpre))
    gt8 = gpre.T[0:8, :]
    brow = _dot_f32(_log_sigmoid(gt8), utri)
    return causal, utri, bcol, gt8, brow


def _in_lockstep(stages):
    alive = list(stages)
    while alive:
        still = []
        for g in alive:
            try:
                next(g)
                still.append(g)
            except StopIteration:
                pass
        alive = still


def _head_fwd(qh, kh, vh, bc, br, igr, m_prev, c_h, n_row, causal):
    qb, kb, vb, cb = qh.astype(BF16), kh.astype(BF16), vh.astype(BF16), c_h.astype(BF16)
    qk = _dot_nt(qb, kb)
    cq = _dot_nt(qb, cb)
    nq = _dot_nt(qb, jnp.broadcast_to(n_row.astype(BF16), (HEAD_DIM, HEAD_DIM)))
    yield
    dlog = jnp.where(causal, bc - br + igr, NEG_BIG)
    inter_log = bc + m_prev
    m_t = jnp.maximum(inter_log, jnp.max(dlog, axis=-1, keepdims=True))
    yield
    dmat = jnp.exp(dlog - m_t)
    inter = jnp.exp(inter_log - m_t)
    s = qk * dmat
    sv = _dot(s.astype(BF16), vb)
    yield
    den = jnp.sum(s, axis=-1, keepdims=True) + inter * nq
    emt = jnp.exp(-m_t)
    yield
    num = sv + inter * cq
    dn = jnp.maximum(jnp.abs(den), emt)
    hm = num / dn
    return dict(dmat=dmat, inter=inter, qb=qb, kb=kb, vb=vb, cb=cb, s=s, cq=cq, nq=nq, den=den, emt=emt,
                dn=dn, hm=hm)


def _state_weights(bc, igc, m_prev, m_new=None):
    last = lax.broadcasted_iota(jnp.int32, (CHUNK, 1), 0) == CHUNK - 1
    b_last = jnp.sum(jnp.where(last, bc, 0.0), axis=0, keepdims=True)
    wlog = b_last - bc + igc
    if m_new is None:
        m_new = jnp.maximum(b_last + m_prev, jnp.max(wlog, axis=0, keepdims=True))
    w_c = jnp.exp(wlog - m_new)
    decay = jnp.exp(b_last + m_prev - m_new)
    return w_c, decay, m_new, last


def _rows_back(x, k):
    return x if k == 0 else pltpu.roll(x, k, 0)


def _rows_ahead(x, k):
    return x if k == 0 else pltpu.roll(x, x.shape[0] - k, 0)


def _conv_taps(xpad):
    return [_rows_back(xpad, CONV_WIDTH - 1 - j)[CONV_HALO:, :] for j in range(CONV_WIDTH)]


def _conv_pre(taps, cw_ref, cb_ref):
    a = cb_ref[...]
    for j in range(CONV_WIDTH):
        a = a + cw_ref[j:j + 1, :] * taps[j]
    return a


def _window_sum(x, w, shift):
    k = 1
    while k < w:
        x = x + shift(x, k)
        k *= 2
    return x


def _pool_window_sum(upad_ref, g, w):
    lanes = slice(g * POOL_GROUP_DIM, (g + 1) * POOL_GROUP_DIM)
    return _window_sum(upad_ref[:, lanes], w, _rows_back)[POOL_HALO:, :]


def _pool_inv_count(row0, rows, w):
    pos = row0 + lax.broadcasted_iota(jnp.int32, (rows, 1), 0) + 1
    return 1.0 / jnp.minimum(pos, w).astype(F32)


FWD_CHUNKS = 2
BWD_CHUNKS = 4


def _mix_fwd(proj, gates, bg_pad, conv_w8, conv_b, w_pool, ls_pool, mh_g):
    seq = proj.shape[0]
    n_chunks = seq // CHUNK
    per_step = FWD_CHUNKS
    blk = per_step * CHUNK

    def body(uz_ref, qk_ref, v_ref, o_ref, zm_ref, uh_ref, qkh_ref, g_ref, bg_ref, cw_ref, cb_ref, wp_ref,
             ls_ref, mhg_ref, mix_ref, cst_ref, nst_ref, mst_ref, a_ref, pooled_ref, c_scr, n_scr, m_scr, xpad, upad):
        i = pl.program_id(0)

        @pl.when(i == 0)
        def _():
            c_scr[...] = jnp.zeros_like(c_scr)
            n_scr[...] = jnp.zeros_like(n_scr)
            m_scr[...] = jnp.zeros_like(m_scr)

        first = i == 0

        upad[0:POOL_HALO, :] = jnp.where(first, 0.0, uh_ref[...])
        upad[POOL_HALO:POOL_HALO + blk, :] = uz_ref[:, 0:D_POOL]
        for g, w in enumerate(POOL_WINDOWS):
            lanes = slice(g * POOL_GROUP_DIM, (g + 1) * POOL_GROUP_DIM)
            pooled = (_pool_window_sum(upad, g, w) * _pool_inv_count(i * blk, blk, w) - uz_ref[:, lanes]).astype(BF16)
            pooled_ref[:, lanes] = pooled
            y = _dot(pooled, wp_ref[g].astype(BF16)) * ls_ref[:, lanes]
            zp = uz_ref[:, D_POOL + g * POOL_GROUP_DIM:D_POOL + (g + 1) * POOL_GROUP_DIM]
            mix_ref[:, lanes] = (y * (zp * _sigmoid(zp))).astype(BF16)

        xpad[0:CONV_HALO, :] = jnp.where(first, 0.0, qkh_ref[...])
        xpad[CONV_HALO:CONV_HALO + blk, :] = qk_ref[...]
        a = _conv_pre(_conv_taps(xpad[...]), cw_ref, cb_ref)
        a_ref[...] = a
        qk = a * _sigmoid(a)

        def head(rows, h, qh, kh, vh, bc, br, igr, m_prev, c_h, n_row, causal):
            lanes = slice(h * HEAD_DIM, (h + 1) * HEAD_DIM)
            f = yield from _head_fwd(qh, kh, vh, bc, br, igr, m_prev, c_h, n_row, causal)
            yield
            hm = f["hm"]
            hn = hm * lax.rsqrt(_row_mean_mxu(hm * hm) + EPS) * mhg_ref[:, lanes]
            zm = zm_ref[rows, lanes]
            out = hn * _sigmoid(o_ref[rows, lanes]) * (zm * _sigmoid(zm))
            mix_ref[rows, D_POOL + h * HEAD_DIM:D_POOL + (h + 1) * HEAD_DIM] = out.astype(BF16)

        c_cur = [c_scr[h] for h in range(N_HEADS)]
        n_cur = [n_scr[h:h + 1, :] for h in range(N_HEADS)]
        m_cur = [m_scr[h:h + 1, 0:1] for h in range(N_HEADS)]
        chains = []
        for s in range(per_step):
            rows = slice(s * CHUNK, (s + 1) * CHUNK)
            gpre = g_ref[rows, :] + bg_ref[...]
            causal, _, bcol, gt8, brow = _gate_forms(gpre)
            nst_ref[s] = jnp.zeros((8, 128), F32)
            mst_ref[s] = jnp.zeros((8, 128), F32)
            for h in range(N_HEADS):
                lanes = slice(h * HEAD_DIM, (h + 1) * HEAD_DIM)
                cst_ref[s, h] = c_cur[h]
                nst_ref[s, h:h + 1, :] = n_cur[h]
                mst_ref[s, h:h + 1, :] = jnp.broadcast_to(m_cur[h], (1, 128))
                qh = qk[rows, lanes]
                kh = qk[rows, D_MLSTM + h * HEAD_DIM:D_MLSTM + (h + 1) * HEAD_DIM] * (HEAD_DIM ** -0.5)
                vh = v_ref[rows, lanes]
                bc = bcol[:, N_HEADS + h:N_HEADS + h + 1]
                br = brow[N_HEADS + h:N_HEADS + h + 1, :]
                igr = gt8[h:h + 1, :]
                igc = gpre[:, h:h + 1]
                chains.append(head(rows, h, qh, kh, vh, bc, br, igr, m_cur[h], c_cur[h], n_cur[h], causal))
                w_c, decay, m_new, _ = _state_weights(bc, igc, m_cur[h])
                c_cur[h] = decay * c_cur[h] + _dot_tn((vh * w_c).astype(BF16), kh.astype(BF16))
                n_cur[h] = decay * n_cur[h] + jnp.sum(w_c * kh, axis=0, keepdims=True)
                m_cur[h] = m_new
        for h in range(N_HEADS):
            c_scr[h] = c_cur[h]
            n_scr[h:h + 1, :] = n_cur[h]
            m_scr[h:h + 1, :] = jnp.broadcast_to(m_cur[h], (1, 128))
        _in_lockstep(chains)

    in_specs = [
        pl.BlockSpec((blk, 1024), lambda i: (i, 0)),
        pl.BlockSpec((blk, 1024), lambda i: (i, 1)),
        pl.BlockSpec((blk, 512), lambda i: (i, 4)),
        pl.BlockSpec((blk, 512), lambda i: (i, 5)),
        pl.BlockSpec((blk, 512), lambda i: (i, 6)),
        pl.BlockSpec((POOL_HALO, 512), lambda i: (jnp.maximum(i * (blk // POOL_HALO) - 1, 0), 0)),
        pl.BlockSpec((CONV_HALO, 1024), lambda i: (jnp.maximum(i * (blk // CONV_HALO) - 1, 0), 1)),
        pl.BlockSpec((blk, 128), lambda i: (i, 0)),
        _full((1, 128)), _full((8, 1024)), _full((1, 1024)), _full((4, 128, 128)), _full((1, 512)),
        _full((1, 512))]
    return pl.pallas_call(
        body, name="mix_fwd", grid=(n_chunks // per_step,),
        out_shape=(jax.ShapeDtypeStruct((seq, D_MODEL), BF16),
                   jax.ShapeDtypeStruct((n_chunks, N_HEADS, HEAD_DIM, HEAD_DIM), F32),
                   jax.ShapeDtypeStruct((n_chunks, 8, 128), F32),
                   jax.ShapeDtypeStruct((n_chunks, 8, 128), F32),
                   jax.ShapeDtypeStruct((seq, 2 * D_MLSTM), F32),
                   jax.ShapeDtypeStruct((seq, D_POOL), BF16)),
        in_specs=in_specs,
        out_specs=(pl.BlockSpec((blk, D_MODEL), lambda i: (i, 0)),
                   pl.BlockSpec((per_step, N_HEADS, HEAD_DIM, HEAD_DIM), lambda i: (i, 0, 0, 0)),
                   pl.BlockSpec((per_step, 8, 128), lambda i: (i, 0, 0)),
                   pl.BlockSpec((per_step, 8, 128), lambda i: (i, 0, 0)),
                   pl.BlockSpec((blk, 2 * D_MLSTM), lambda i: (i, 0)),
                   pl.BlockSpec((blk, D_POOL), lambda i: (i, 0))),
        scratch_shapes=[pltpu.VMEM((N_HEADS, HEAD_DIM, HEAD_DIM), F32), pltpu.VMEM((8, 128), F32),
                        pltpu.VMEM((8, 128), F32), pltpu.VMEM((CONV_HALO + blk, 1024), F32),
                        pltpu.VMEM((POOL_HALO + blk, D_POOL), F32)],
        compiler_params=_params(("arbitrary",)),
    )(proj, proj, proj, proj, proj, proj, proj, gates, bg_pad, conv_w8, conv_b, w_pool, ls_pool, mh_g)


def _out_fwd_bwd(mix, x, tgt, w_out_b, gate, final_g):
    seq = x.shape[0]
    tm = min(512, seq)
    sub = min(256, tm)

    def body(mix_ref, x_ref, t_ref, w_ref, gate_ref, fg_ref, dx2_ref, dmix_ref, dwo_ref, dgate_ref, dfg_ref,
             loss_ref, dwo_acc):
        @pl.when(pl.program_id(0) == 0)
        def _():
            dwo_acc[...] = jnp.zeros_like(dwo_acc)
            dgate_ref[...] = jnp.zeros_like(dgate_ref)
            dfg_ref[...] = jnp.zeros_like(dfg_ref)
            loss_ref[...] = jnp.zeros_like(loss_ref)

        w = w_ref[...]
        gate_v = gate_ref[...]
        fg = fg_ref[...]
        do2_parts = [None] * (tm // sub)

        def chain(n):
            rows = slice(n * sub, (n + 1) * sub)
            o2 = _dot(mix_ref[rows, :], w)
            yield
            x2 = x_ref[rows, :] + gate_v * o2
            r2 = lax.rsqrt(jnp.mean(x2 * x2, axis=-1, keepdims=True) + EPS)
            x2n = x2 * r2
            err = x2n * fg - t_ref[rows, :]
            part = 0.5 * jnp.sum(jnp.sum(err * err, axis=-1, keepdims=True), axis=0, keepdims=True) / D_MODEL
            loss_ref[...] += jnp.broadcast_to(part, loss_ref.shape)
            dy = err / D_MODEL
            dfg_ref[...] += jnp.sum(dy * x2n, axis=0, keepdims=True)
            gdy = dy * fg
            dx2 = r2 * (gdy - x2n * jnp.mean(gdy * x2n, axis=-1, keepdims=True))
            dx2_ref[rows, :] = dx2
            dgate_ref[...] += jnp.sum(dx2 * o2, axis=0, keepdims=True)
            do2 = (dx2 * gate_v).astype(BF16)
            dmix_ref[rows, :] = _dot_nt(do2, w)
            do2_parts[n] = do2

        _in_lockstep(chain(n) for n in range(tm // sub))
        dwo_acc[...] += _dot_tn(mix_ref[...], jnp.concatenate(do2_parts, axis=0))

        @pl.when(pl.program_id(0) == seq // tm - 1)
        def _():
            dwo_ref[...] = dwo_acc[...].astype(BF16)

    tile = pl.BlockSpec((tm, D_MODEL), lambda i: (i, 0))
    vec = _full((1, D_MODEL))
    return pl.pallas_call(
        body, name="out_fwd_bwd", grid=(seq // tm,),
        out_shape=(jax.ShapeDtypeStruct((seq, D_MODEL), F32), jax.ShapeDtypeStruct((seq, D_MODEL), F32),
                   jax.ShapeDtypeStruct((D_MODEL, D_MODEL), BF16), jax.ShapeDtypeStruct((1, D_MODEL), F32),
                   jax.ShapeDtypeStruct((1, D_MODEL), F32), jax.ShapeDtypeStruct((1, 128), F32)),
        in_specs=[tile, tile, tile, _full((D_MODEL, D_MODEL)), vec, vec],
        out_specs=(tile, tile, _full((D_MODEL, D_MODEL)), vec, vec, _full((1, 128))),
        scratch_shapes=[pltpu.VMEM((D_MODEL, D_MODEL), F32)],
        compiler_params=_params(("arbitrary",)),
    )(mix, x, tgt, w_out_b, gate, final_g)


def _mix_bwd(proj, gates, dmix, conv_a, pooled, cst, nst, mst, bg_pad, conv_w8, w_pool, ls_pool, mh_g):
    seq = proj.shape[0]
    n_chunks = seq // CHUNK
    per_step = BWD_CHUNKS
    blk = per_step * CHUNK
    n_blocks = n_chunks // per_step

    def body(zp_ref, qk_ref, v_ref, o_ref, zm_ref, g_ref, dmix_ref, a_ref, pooled_ref, cst_ref, nst_ref, mst_ref,
             mnx_ref, bg_ref, cw_ref, wp_ref, ls_ref, mhg_ref,
             dp_ref, dcw_ref, dcb_ref, dwp_ref, dls_ref, dmhg_ref, dbg_ref,
             dc_scr, dn_scr, dapad, dpipad):
        i = pl.program_id(0)
        bi = n_blocks - 1 - i

        @pl.when(i == 0)
        def _():
            for ref in (dc_scr, dn_scr, dcw_ref, dcb_ref, dwp_ref, dls_ref, dmhg_ref, dbg_ref):
                ref[...] = jnp.zeros_like(ref)
            dapad[blk:blk + CONV_HALO, :] = jnp.zeros((CONV_HALO, 1024), F32)
            dpipad[blk:blk + POOL_HALO, :] = jnp.zeros((POOL_HALO, D_POOL), F32)

        dpooled = []
        for g, w in enumerate(POOL_WINDOWS):
            lanes = slice(g * POOL_GROUP_DIM, (g + 1) * POOL_GROUP_DIM)
            zlanes = slice(D_POOL + g * POOL_GROUP_DIM, D_POOL + (g + 1) * POOL_GROUP_DIM)
            inv = _pool_inv_count(bi * blk, blk, w)
            pb = pooled_ref[:, lanes]
            wpb = wp_ref[g].astype(BF16)
            yw = _dot(pb, wpb)
            ls = ls_ref[:, lanes]
            zp = zp_ref[:, lanes]
            sg = _sigmoid(zp)
            dpo = dmix_ref[:, lanes]
            dp_ref[:, zlanes] = (dpo * (yw * ls) * (sg * (1.0 + zp * (1.0 - sg)))).astype(BF16)
            dy = dpo * (zp * sg)
            dls_ref[:, lanes] += jnp.sum(dy * yw, axis=0, keepdims=True)
            dyw = (dy * ls).astype(BF16)
            dwp_ref[g] += _dot_tn(pb, dyw)
            dpl = _dot_nt(dyw, wpb)
            dpooled.append(dpl)
            dpipad[0:blk, lanes] = dpl * inv
        for g, w in enumerate(POOL_WINDOWS):
            lanes = slice(g * POOL_GROUP_DIM, (g + 1) * POOL_GROUP_DIM)
            du = _window_sum(dpipad[:, lanes], w, _rows_ahead)[0:blk, :] - dpooled[g]
            dp_ref[:, lanes] = du.astype(BF16)
        dpipad[blk:blk + POOL_HALO, :] = dpipad[0:POOL_HALO, :]

        def silu_and_slope(rows, cols):
            a = a_ref[rows, cols]
            sg = _sigmoid(a)
            return a * sg, sg * (1.0 + a * (1.0 - sg))

        lane = lax.broadcasted_iota(jnp.int32, (CHUNK, 128), 1)
        row = lax.broadcasted_iota(jnp.int32, (CHUNK, 128), 0)
        scale_k = HEAD_DIM ** -0.5
        forms = [None] * per_step
        col_g_rows = [[] for _ in range(per_step)]
        dig_parts = [[] for _ in range(per_step)]
        db_parts = [[] for _ in range(per_step)]
        d_state = [[None] * N_HEADS for _ in range(per_step)]

        def state_terms(s, h, c_h, n_row, vb, kb):
            dcn, dnn = d_state[s][h]
            dcnb = dcn.astype(BF16)
            amat = _dot(vb, dcnb) + dnn
            kdc = _dot_nt(kb, dcnb)
            ddecay = (jnp.sum(jnp.sum(dcn * c_h, axis=-1, keepdims=True), axis=0, keepdims=True)
                      + jnp.sum(dnn * n_row, axis=-1, keepdims=True))
            return dcn, dnn, amat, kdc, ddecay

        def head(s, h):
            rows = slice(s * CHUNK, (s + 1) * CHUNK)
            lanes = slice(h * HEAD_DIM, (h + 1) * HEAD_DIM)
            klanes = slice(D_MLSTM + h * HEAD_DIM, D_MLSTM + (h + 1) * HEAD_DIM)
            gpre, causal, utri, bcol, gt8, brow = forms[s]
            qh, dsilu_q = silu_and_slope(rows, lanes)
            kh, dsilu_k = silu_and_slope(rows, klanes)
            kh = kh * scale_k
            vh = v_ref[rows, lanes]
            bc = bcol[:, N_HEADS + h:N_HEADS + h + 1]
            br = brow[N_HEADS + h:N_HEADS + h + 1, :]
            igr = gt8[h:h + 1, :]
            igc = gpre[:, h:h + 1]
            m_prev = mst_ref[s, h:h + 1, 0:1]
            m_next = mnx_ref[0, h:h + 1, 0:1] if s == per_step - 1 else mst_ref[s + 1, h:h + 1, 0:1]
            c_h = cst_ref[s, h]
            n_row = nst_ref[s, h:h + 1, :]
            w_c, decay, _, last = _state_weights(bc, igc, m_prev, m_next)
            terms = None
            if s == per_step - 1:
                terms = state_terms(s, h, c_h, n_row, vh.astype(BF16), kh.astype(BF16))
            f = yield from _head_fwd(qh, kh, vh, bc, br, igr, m_prev, c_h, n_row, causal)
            qb, kb, vb, cb = f["qb"], f["kb"], f["vb"], f["cb"]
            sm, dmat, inter, den, dn, hm = f["s"], f["dmat"], f["inter"], f["den"], f["dn"], f["hm"]
            yield

            rinv = lax.rsqrt(_row_mean_mxu(hm * hm) + EPS)
            hmn = hm * rinv
            gh = mhg_ref[:, lanes]
            o_pre = o_ref[rows, lanes]
            og = _sigmoid(o_pre)
            zm = zm_ref[rows, lanes]
            sgz = _sigmoid(zm)
            sz = zm * sgz
            dout = dmix_ref[rows, D_POOL + h * HEAD_DIM:D_POOL + (h + 1) * HEAD_DIM]
            hn = hmn * gh
            dp_ref[rows, 2560 + h * HEAD_DIM:2560 + (h + 1) * HEAD_DIM] = (
                dout * hn * sz * og * (1.0 - og)).astype(BF16)
            dp_ref[rows, 3072 + h * HEAD_DIM:3072 + (h + 1) * HEAD_DIM] = (
                dout * hn * og * (sgz * (1.0 + zm * (1.0 - sgz)))).astype(BF16)
            dhn = dout * og * sz
            dmhg_ref[:, lanes] += jnp.sum(dhn * hmn, axis=0, keepdims=True)
            dyn = dhn * gh
            dhm = rinv * (dyn - hmn * _row_mean_mxu(dyn * hmn))
            yield

            inv_dn = 1.0 / dn
            dnum = dhm * inv_dn
            hd = jnp.sum(dhm * hm, axis=-1, keepdims=True)
            dden = jnp.where(jnp.abs(den) > f["emt"], -hd * inv_dn * jnp.sign(den), 0.0)
            dnb = dnum.astype(BF16)
            dnv = _dot_nt(dnb, vb)
            dv = _dot_tn(sm.astype(BF16), dnb)
            dnc = _dot(dnb, cb)
            dc_prev = _dot_tn((inter * dnum).astype(BF16), qb)
            dn_prev = jnp.sum((inter * dden) * qh, axis=0, keepdims=True)
            yield
            ds = dnv + dden
            dqk = (ds * dmat).astype(BF16)
            dqk_k = _dot(dqk, kb)
            dk = _dot_tn(dqk, qb)
            for _ in range(per_step - 1 - s):
                yield
            if terms is None:
                terms = state_terms(s, h, c_h, n_row, vb, kb)
            dcn, dnn, amat, kdc, ddecay = terms
            d_start = (decay * dcn + dc_prev, decay * dnn + dn_prev)
            if s > 0:
                d_state[s - 1][h] = d_start
            else:
                dc_scr[h] = d_start[0]
                dn_scr[h:h + 1, :] = d_start[1]
            yield
            gmat = ds * sm
            row_g = jnp.sum(gmat, axis=-1, keepdims=True)
            col_g_rows[s].append(jnp.where(row == h, jnp.sum(gmat, axis=0, keepdims=True), 0.0))
            gcol = inter * (jnp.sum(dnum * f["cq"], axis=-1, keepdims=True) + dden * f["nq"])
            dw = jnp.sum(amat * kh, axis=-1, keepdims=True)
            e = dw * w_c
            db_last = ddecay * decay + jnp.sum(e, axis=0, keepdims=True)
            dig_parts[s].append(jnp.where(lane == h, e, 0.0))
            db_parts[s].append(
                jnp.where(lane == N_HEADS + h, row_g + gcol - e + jnp.where(last, db_last, 0.0), 0.0))
            yield
            dq = dqk_k + inter * (dnc + dden * n_row)
            dp_ref[rows, 2048 + h * HEAD_DIM:2048 + (h + 1) * HEAD_DIM] = (dv + w_c * kdc).astype(BF16)
            dapad[rows, lanes] = dq * dsilu_q
            dapad[rows, klanes] = (dk + w_c * amat) * scale_k * dsilu_k

        chains = []
        for s in reversed(range(per_step)):
            gpre = g_ref[s * CHUNK:(s + 1) * CHUNK, :] + bg_ref[...]
            forms[s] = (gpre,) + _gate_forms(gpre)
            for h in range(N_HEADS):
                if s == per_step - 1:
                    d_state[s][h] = (dc_scr[h], dn_scr[h:h + 1, :])
                chains.append(head(s, h))
        _in_lockstep(chains)

        for s in range(per_step):
            rows = slice(s * CHUNK, (s + 1) * CHUNK)
            gpre, utri = forms[s][0], forms[s][2]
            cs_t = sum(col_g_rows[s][1:], col_g_rows[s][0]).T
            dig_all = sum(dig_parts[s][1:], dig_parts[s][0]) + cs_t
            db_cols = sum(db_parts[s][1:], db_parts[s][0])
            shifted = jnp.zeros((CHUNK, 128), F32)
            for h in range(N_HEADS):
                shifted = shifted + jnp.where(lane == N_HEADS + h, cs_t[:, h:h + 1], 0.0)
            dlf = _dot_f32(utri, db_cols - shifted)
            dgates = dig_all + dlf * _sigmoid(-gpre)
            dp_ref[rows, N_MAIN:N_MAIN + 128] = dgates.astype(BF16)
            dbg_ref[...] += jnp.sum(dgates, axis=0, keepdims=True)
        dp_ref[:, N_MAIN + 128:N_PAD] = jnp.zeros((blk, N_PAD - N_MAIN - 128), BF16)

        da_pad = dapad[...]
        da = da_pad[0:blk, :]
        dcb_ref[...] += jnp.sum(da, axis=0, keepdims=True)
        x = qk_ref[...]
        dx = jnp.zeros((blk, 1024), F32)
        for j in range(CONV_WIDTH):
            da_j = _rows_ahead(da_pad, CONV_WIDTH - 1 - j)[0:blk, :]
            dcw_ref[j:j + 1, :] += jnp.sum(da_j * x, axis=0, keepdims=True)
            dx = dx + cw_ref[j:j + 1, :] * da_j
        dp_ref[:, 1024:2048] = dx.astype(BF16)
        dapad[blk:blk + CONV_HALO, :] = dapad[0:CONV_HALO, :]

    bmap = lambda i: n_blocks - 1 - i
    wide = pl.BlockSpec((blk, 1024), lambda i: (bmap(i), 0))
    state = pl.BlockSpec((per_step, 8, 128), lambda i: (bmap(i), 0, 0))
    in_specs = [
        pl.BlockSpec((blk, 512), lambda i: (bmap(i), 1)),
        pl.BlockSpec((blk, 1024), lambda i: (bmap(i), 1)),
        pl.BlockSpec((blk, 512), lambda i: (bmap(i), 4)),
        pl.BlockSpec((blk, 512), lambda i: (bmap(i), 5)),
        pl.BlockSpec((blk, 512), lambda i: (bmap(i), 6)),
        pl.BlockSpec((blk, 128), lambda i: (bmap(i), 0)),
        wide, wide,
        pl.BlockSpec((blk, D_POOL), lambda i: (bmap(i), 0)),
        pl.BlockSpec((per_step, N_HEADS, HEAD_DIM, HEAD_DIM), lambda i: (bmap(i), 0, 0, 0)),
        state, state,
        pl.BlockSpec((1, 8, 128), lambda i: (jnp.minimum((bmap(i) + 1) * per_step, n_chunks - 1), 0, 0)),
        _full((1, 128)), _full((8, 1024)), _full((4, 128, 128)), _full((1, 512)), _full((1, 512))]
    return pl.pallas_call(
        body, name="mix_bwd", grid=(n_blocks,),
        out_shape=(jax.ShapeDtypeStruct((seq, N_PAD), BF16), jax.ShapeDtypeStruct((8, 1024), F32),
                   jax.ShapeDtypeStruct((1, 1024), F32), jax.ShapeDtypeStruct((4, 128, 128), F32),
                   jax.ShapeDtypeStruct((1, 512), F32), jax.ShapeDtypeStruct((1, 512), F32),
                   jax.ShapeDtypeStruct((1, 128), F32)),
        in_specs=in_specs,
        out_specs=(pl.BlockSpec((blk, N_PAD), lambda i: (bmap(i), 0)), _full((8, 1024)), _full((1, 1024)),
                   _full((4, 128, 128)), _full((1, 512)), _full((1, 512)), _full((1, 128))),
        scratch_shapes=[pltpu.VMEM((N_HEADS, HEAD_DIM, HEAD_DIM), F32), pltpu.VMEM((8, 128), F32),
                        pltpu.VMEM((blk + CONV_HALO, 1024), F32), pltpu.VMEM((blk + POOL_HALO, D_POOL), F32)],
        compiler_params=_params(("arbitrary",)),
    )(proj, proj, proj, proj, proj, gates, dmix, conv_a, pooled, cst, nst, mst, mst, bg_pad, conv_w8,
      w_pool, ls_pool, mh_g)


def _bwd_in(dproj, w_in_t, x, dx2, norm_g, scale):
    seq = x.shape[0]
    tm = min(512, seq)
    sub = min(256, tm)

    def body(dp_ref, wt_ref, x_ref, dx2_ref, ng_ref, sc_ref, gx_ref, dsh_ref, dsc_ref, dng_ref):
        @pl.when(pl.program_id(0) == 0)
        def _():
            dsh_ref[...] = jnp.zeros_like(dsh_ref)
            dsc_ref[...] = jnp.zeros_like(dsc_ref)
            dng_ref[...] = jnp.zeros_like(dng_ref)

        ng = ng_ref[...]
        one_sc = 1.0 + sc_ref[...]

        def chain(n):
            rows = slice(n * sub, (n + 1) * sub)
            dh = _dot(dp_ref[rows, :], wt_ref[...])
            yield
            xt = x_ref[rows, :]
            r = lax.rsqrt(jnp.mean(xt * xt, axis=-1, keepdims=True) + EPS)
            xn = xt * r
            dsh_ref[...] += jnp.sum(dh, axis=0, keepdims=True)
            dhxn = dh * xn
            dsc_ref[...] += jnp.sum(dhxn * ng, axis=0, keepdims=True)
            dng_ref[...] += jnp.sum(dhxn * one_sc, axis=0, keepdims=True)
            dxn = dh * (ng * one_sc)
            gx_ref[rows, :] = r * (dxn - xn * jnp.mean(dxn * xn, axis=-1, keepdims=True)) + dx2_ref[rows, :]

        _in_lockstep(chain(n) for n in range(tm // sub))

    tile = pl.BlockSpec((tm, D_MODEL), lambda i: (i, 0))
    vec = _full((1, D_MODEL))
    return pl.pallas_call(
        body, name="bwd_in", grid=(seq // tm,),
        out_shape=(jax.ShapeDtypeStruct((seq, D_MODEL), F32),) + (jax.ShapeDtypeStruct((1, D_MODEL), F32),) * 3,
        in_specs=[pl.BlockSpec((tm, N_PAD), lambda i: (i, 0)), _full((N_PAD, D_MODEL)), tile, tile, vec, vec],
        out_specs=(tile, vec, vec, vec),
        compiler_params=_params(("arbitrary",)),
    )(dproj, w_in_t, x, dx2, norm_g, scale)


def _dw_in(h_b, dproj):
    seq = h_b.shape[0]
    tk = min(4096, seq)
    tn = 768
    n_t = seq // tk

    def body(h_ref, dp_ref, dwt_ref, acc):
        t = pl.program_id(1)

        @pl.when(t == 0)
        def _():
            acc[...] = jnp.zeros_like(acc)

        acc[...] += _dot_tn(dp_ref[...], h_ref[...])

        @pl.when(t == n_t - 1)
        def _():
            dwt_ref[...] = acc[...].astype(BF16)

    return pl.pallas_call(
        body, name="dw_in", grid=(N_PAD // tn, n_t),
        out_shape=jax.ShapeDtypeStruct((N_PAD, D_MODEL), BF16),
        in_specs=[pl.BlockSpec((tk, D_MODEL), lambda j, t: (t, 0)), pl.BlockSpec((tk, tn), lambda j, t: (t, j))],
        out_specs=pl.BlockSpec((tn, D_MODEL), lambda j, t: (j, 0)),
        scratch_shapes=[pltpu.VMEM((tn, D_MODEL), F32)],
        compiler_params=_params(("arbitrary", "arbitrary")),
    )(h_b, dproj)


def _adam_update(g, w, m, v, g_ref, d_ref, m_ref, v_ref):
    mn = ADAM_B1 * m + (1.0 - ADAM_B1) * g
    vn = ADAM_B2 * v + (1.0 - ADAM_B2) * (g * g)
    m_hat = mn / (1.0 - ADAM_B1 ** ADAM_STEP)
    v_hat = vn / (1.0 - ADAM_B2 ** ADAM_STEP)
    g_ref[...] = g
    d_ref[...] = -ADAM_LR * (m_hat / (jnp.sqrt(v_hat) + ADAM_EPS) + ADAM_WD * w)
    m_ref[...] = mn
    v_ref[...] = vn


def _adam_sum(name, parts, w, m, v, row_tile, col_tile=None):
    rows, cols = w.shape
    col_tile = cols if col_tile is None else col_tile
    n_parts = parts.shape[0]

    def body(p_ref, w_ref, m_ref, v_ref, g_out, d_out, m_out, v_out):
        g = p_ref[0].astype(F32)
        for j in range(1, n_parts):
            g = g + p_ref[j].astype(F32)
        _adam_update(g, w_ref[...], m_ref[...], v_ref[...], g_out, d_out, m_out, v_out)

    tile = pl.BlockSpec((row_tile, col_tile), lambda i, j: (i, j))
    return pl.pallas_call(
        body, name=name, grid=(rows // row_tile, cols // col_tile),
        out_shape=(jax.ShapeDtypeStruct((rows, cols), F32),) * 4,
        in_specs=[pl.BlockSpec((n_parts, row_tile, col_tile), lambda i, j: (0, i, j)), tile, tile, tile],
        out_specs=(tile,) * 4,
        compiler_params=_params(("arbitrary", "arbitrary")),
    )(parts, w, m, v)


def _adam_ada(sc_all16, dmod_blk16, w, m, v):
    rows, cols = w.shape

    def body(sc_ref, dm_ref, w_ref, m_ref, v_ref, g_out, d_out, m_out, v_out):
        g = _dot_tn(sc_ref[...].astype(BF16), dm_ref[...].astype(BF16))
        _adam_update(g, w_ref[...], m_ref[...], v_ref[...], g_out, d_out, m_out, v_out)

    return pl.pallas_call(
        body, name="adam_w_ada", grid=(1,),
        out_shape=(jax.ShapeDtypeStruct((rows, cols), F32),) * 4,
        in_specs=[_full(sc_all16.shape), _full(dmod_blk16.shape)] + [_full((rows, cols))] * 3,
        out_specs=(_full((rows, cols)),) * 4,
        compiler_params=_params(("arbitrary",)),
    )(sc_all16, dmod_blk16, w, m, v)


def _adam_small(parts, loss_parts, w, m, v):
    names = list(w)
    n = len(names)

    def body(*refs):
        p_refs, loss_ref = refs[:n], refs[n]
        w_refs, m_refs, v_refs = (refs[n + 1 + k * n:n + 1 + (k + 1) * n] for k in range(3))
        outs = refs[3 * n + n + 1:]
        for a in range(n):
            g = p_refs[a][0]
            for j in range(1, N_DEV):
                g = g + p_refs[a][j]
            width = w_refs[a].shape[-1]
            if g.shape[-1] != width:
                g = g[..., 0:width]
            _adam_update(g, w_refs[a][...], m_refs[a][...], v_refs[a][...], *outs[4 * a:4 * a + 4])
        total = loss_ref[0]
        for j in range(1, N_DEV):
            total = total + loss_ref[j]
        outs[4 * n][...] = total

    args = [parts[k] for k in names] + [loss_parts] + [d[k] for d in (w, m, v) for k in names]
    out_shape = tuple(jax.ShapeDtypeStruct(w[k].shape, F32) for k in names for _ in range(4))
    out_shape += (jax.ShapeDtypeStruct(loss_parts.shape[1:], F32),)
    out = pl.pallas_call(
        body, name="adam_small", grid=(1,), out_shape=out_shape,
        in_specs=[_full(a.shape) for a in args], out_specs=tuple(_full(s.shape) for s in out_shape),
        compiler_params=_params(("arbitrary",)),
    )(*args)
    return {k: out[4 * a:4 * a + 4] for a, k in enumerate(names)}, out[4 * n]


def _local_step(x2, tgt2, shift, scale, gate, norm_g, w_in_t, w_out_b, conv_w, conv_b, w_pool, ls_pool,
                mh_norm_g, b_gates, final_g, send_dw_out=None, send_dw_in=None):
    bg_pad = jnp.pad(b_gates, ((0, 0), (0, 128 - b_gates.shape[1])))
    conv_w8 = jnp.pad(conv_w, ((0, 8 - CONV_WIDTH), (0, 0)))
    fg = final_g.reshape(1, D_MODEL)

    proj, gates, h_b = _fwd_proj(x2, norm_g, scale, shift, w_in_t)
    mix, cst, nst, mst, conv_a, pooled = _mix_fwd(proj, gates, bg_pad, conv_w8, conv_b, w_pool, ls_pool, mh_norm_g)
    if callable(w_out_b):
        w_out_b = w_out_b(mix)
    dx2, dmix, dwo, dgate, dfg, loss = _out_fwd_bwd(mix, x2, tgt2, w_out_b, gate, fg)
    if send_dw_out is not None:
        bg_pad = bg_pad + send_dw_out(dwo)
    dproj, dcw8, dcb, dwp, dls, dmhg, dbg = _mix_bwd(proj, gates, dmix, conv_a, pooled, cst, nst, mst, bg_pad,
                                                      conv_w8, w_pool, ls_pool, mh_norm_g)
    dw_in_t = _dw_in(h_b, dproj)[:N_IN]
    ng_in = norm_g
    if send_dw_in is not None:
        ng_in = norm_g + send_dw_in(dw_in_t, dcw8[:CONV_WIDTH])
    gx, dsh, dsc, dng = _bwd_in(dproj, w_in_t, x2, dx2, ng_in, scale)
    return dict(loss=loss, grad_x=gx, dw_in_t=dw_in_t, dw_out=dwo, dconv_w=dcw8[:CONV_WIDTH], conv_b=dcb,
                w_pool=dwp, ls_pool=dls, mh_norm_g=dmhg, b_gates=dbg, final_g=dfg, norm_g=dng,
                dmod=jnp.concatenate([dsh, dsc, dgate], axis=1))


def kernel(x, c, norm_g, w_ada, b_ada, w_in, b_gates, conv_w, conv_b, w_pool, ls_pool, mh_norm_g, w_out, final_g, loss_target, m_norm_g, m_w_ada, m_b_ada, m_w_in, m_b_gates, m_conv_w, m_conv_b, m_w_pool, m_ls_pool, m_mh_norm_g, m_w_out, m_final_g, v_norm_g, v_w_ada, v_b_ada, v_w_in, v_b_gates, v_conv_w, v_conv_b, v_w_pool, v_ls_pool, v_mh_norm_g, v_w_out, v_final_g):
    seq = x.shape[1]
    me = 4 * lax.axis_index("x") + 2 * lax.axis_index("y") + lax.axis_index("c")

    b_ada_blk = lax.dynamic_slice(b_ada, (0, me * ADA_SHARD), (1, ADA_SHARD))
    g_in, g_cw, mod_all, sc_all16 = _gather_weights_and_mod(
        (w_in[0].astype(BF16).T, conv_w[0]), c, w_ada[0], b_ada_blk)
    w_in_t = jnp.pad(g_in.reshape(N_IN, D_MODEL), ((0, N_PAD - N_IN), (0, 0)))
    w_out_flight, _ = _scatter_start("send_w_out", (w_out[0].astype(BF16),), whole=True, after=g_in)

    def w_out_b(after):
        (g_out,) = _scatter_wait("recv_w_out", w_out_flight, after, whole=True)
        return g_out.reshape(D_MODEL, D_MODEL)

    conv_w_full = jnp.transpose(g_cw, (1, 0, 2)).reshape(CONV_WIDTH, 2 * D_MLSTM)
    mod = lax.dynamic_index_in_dim(mod_all, me, axis=1, keepdims=False).reshape(1, 3 * D_MODEL)
    shift, scale, gate = mod[:, :D_MODEL], mod[:, D_MODEL:2 * D_MODEL], mod[:, 2 * D_MODEL:]

    flights = {}

    def send_dw_out(dwo):
        blocks = dwo.reshape(N_DEV, D_MODEL // N_DEV, D_MODEL)
        flights["out"], token = _scatter_start("send_dw_out", (blocks,))
        return token

    def send_dw_in(dw_in_t, dcw):
        blocks = dw_in_t.reshape(N_DEV, N_SHARD, D_MODEL)
        dcw_blocks = jnp.transpose(dcw.reshape(CONV_WIDTH, N_DEV, 128), (1, 0, 2))
        flights["in"], token = _scatter_start("send_dw_in", (blocks, dcw_blocks))
        return token

    r = _local_step(x[0], loss_target[0], shift, scale, gate, norm_g, w_in_t, w_out_b, conv_w_full, conv_b,
                    w_pool[0], ls_pool, mh_norm_g, b_gates, final_g, send_dw_out, send_dw_in)

    small_names = ("norm_g", "b_ada", "b_gates", "conv_b", "w_pool", "ls_pool", "mh_norm_g", "final_g")
    small_grads = dict(norm_g=r["norm_g"], b_ada=r["dmod"], b_gates=r["b_gates"], conv_b=r["conv_b"],
                       w_pool=r["w_pool"], ls_pool=r["ls_pool"], mh_norm_g=r["mh_norm_g"], final_g=r["final_g"])
    flights["small"], _ = _scatter_start(
        "send_small", (r["loss"],) + tuple(small_grads[k] for k in small_names), whole=True)
    p_in, p_cw = _scatter_wait("recv_dw_in", flights["in"], flights["small"][2 + 1 + small_names.index("w_pool")])
    (p_out,) = _scatter_wait("recv_dw_out", flights["out"], p_cw)

    in_t = _adam_sum("adam_w_in", p_in, w_in[0].T, m_w_in[0].T, v_w_in[0].T, N_SHARD, 256)
    gi, di, mi, vi = (o.T for o in in_t)
    go, do_, mo, vo = _adam_sum("adam_w_out", p_out, w_out[0], m_w_out[0], v_w_out[0], 128)
    gc, dc, mc, vc = _adam_sum("adam_conv_w", p_cw, conv_w[0], m_conv_w[0], v_conv_w[0], CONV_WIDTH)
    gathered = _scatter_wait("recv_small", flights["small"], go, whole=True)
    p_loss, p_small = gathered[0], dict(zip(small_names, gathered[1:]))

    def plain(norm_g_, b_ada_, b_gates_, conv_b_, w_pool_, ls_pool_, mh_norm_g_, final_g_):
        return dict(norm_g=norm_g_, b_ada=b_ada_, b_gates=b_gates_, conv_b=conv_b_, w_pool=w_pool_[0],
                    ls_pool=ls_pool_, mh_norm_g=mh_norm_g_, final_g=final_g_.reshape(1, D_MODEL))

    small, loss_row = _adam_small(
        p_small, p_loss,
        plain(norm_g, b_ada, b_gates, conv_b, w_pool, ls_pool, mh_norm_g, final_g),
        plain(m_norm_g, m_b_ada, m_b_gates, m_conv_b, m_w_pool, m_ls_pool, m_mh_norm_g, m_final_g),
        plain(v_norm_g, v_b_ada, v_b_gates, v_conv_b, v_w_pool, v_ls_pool, v_mh_norm_g, v_final_g))

    dmod_all = p_small["b_ada"].reshape(N_DEV, 3 * D_MODEL)
    dmod_blk16 = jnp.pad(lax.dynamic_slice(dmod_all, (0, me * ADA_SHARD), (N_DEV, ADA_SHARD)), ((0, 8), (0, 0)))
    ga, da, ma, va = _adam_ada(sc_all16, dmod_blk16, w_ada[0], m_w_ada[0], v_w_ada[0])

    names = ("norm_g", "w_ada", "b_ada", "w_in", "b_gates", "conv_w", "conv_b", "w_pool", "ls_pool", "mh_norm_g",
             "w_out", "final_g")
    shapes = dict(norm_g=norm_g.shape, b_ada=b_ada.shape, b_gates=b_gates.shape, conv_b=conv_b.shape,
                  w_pool=w_pool.shape, ls_pool=ls_pool.shape, mh_norm_g=mh_norm_g.shape, final_g=final_g.shape)
    sharded = dict(w_ada=(ga, da, ma, va), w_in=(gi, di, mi, vi), conv_w=(gc, dc, mc, vc), w_out=(go, do_, mo, vo))
    outs = []
    for kind in range(4):
        for nm in names:
            if nm in sharded:
                outs.append(sharded[nm][kind][None])
            else:
                outs.append(small[nm][kind].reshape(shapes[nm]))
    loss = loss_row[0, 0]
    grad_x = r["grad_x"].reshape(1, seq, D_MODEL)
    return (loss, grad_x, *outs)
```

```python
import jax
import jax.numpy as jnp
from jax import lax
from jax.experimental import pallas as pl
from jax.experimental.pallas import tpu as pltpu

F32 = jnp.float32
BF16 = jnp.bfloat16

D_MODEL = 1024
D_POOL = 512
D_MLSTM = 512
N_HEADS = 4
HEAD_DIM = 128
CHUNK = 128
POOL_WINDOWS = (2, 4, 8, 16)
POOL_GROUP_DIM = 128
CONV_WIDTH = 4
EPS = 1e-6
N_MAIN = 3584
N_IN = 3592
N_PAD = 3840
N_SHARD = N_IN // 8
ADA_SHARD = 3 * D_MODEL // 8
N_DEV = 8
CONV_HALO = 8
POOL_HALO = 16
NEG_BIG = -1e30
VMEM_LIMIT_BYTES = 56 * 1024 * 1024

ADAM_LR = 0.001
ADAM_B1 = 0.9
ADAM_B2 = 0.999
ADAM_EPS = 1e-08
ADAM_WD = 0.01
ADAM_STEP = 10

def _dot(a, b):
    return jnp.dot(a, b, preferred_element_type=F32)


def _dot_nt(a, b):
    return lax.dot_general(a, b, (((1,), (1,)), ((), ())), preferred_element_type=F32)


def _dot_tn(a, b):
    return lax.dot_general(a, b, (((0,), (0,)), ((), ())), preferred_element_type=F32)


def _dot_f32(a, b):
    return jnp.dot(a, b, precision=lax.Precision.HIGHEST, preferred_element_type=F32)


def _row_mean_mxu(x):
    return _dot(x.astype(BF16), jnp.full((HEAD_DIM, HEAD_DIM), 1.0 / HEAD_DIM, BF16))


def _sigmoid(x):
    return jax.nn.sigmoid(x)


def _log_sigmoid(x):
    return jnp.minimum(x, 0.0) - jnp.log1p(jnp.exp(-jnp.abs(x)))


def _params(sem):
    return pltpu.CompilerParams(dimension_semantics=sem, vmem_limit_bytes=VMEM_LIMIT_BYTES)


def _full(shape):
    n = len(shape)
    return pl.BlockSpec(shape, lambda *_: (0,) * n)


def _mesh_pos():
    return lax.axis_index("x"), lax.axis_index("y"), lax.axis_index("c")


def _peer(k):
    x, y, c = _mesh_pos()
    px = 1 - x if (k >> 2) & 1 else x
    py = 1 - y if (k >> 1) & 1 else y
    pc = 1 - c if k & 1 else c
    return (px, py, pc), 4 * px + 2 * py + pc


def _remote(src, dst, send_sem, recv_sem, to):
    return pltpu.make_async_remote_copy(src_ref=src, dst_ref=dst, send_sem=send_sem, recv_sem=recv_sem, device_id=to,
                                        device_id_type=pl.DeviceIdType.MESH)


def _two_level_gather(src, dst, send_sems, recv_sems, local_sems):
    n = len(src)
    x, y, c = _mesh_pos()
    me = 4 * x + 2 * y + c
    sibling = (x, y, 1 - c)
    south = c == 0
    near = (jnp.where(south, 1 - x, x), jnp.where(south, y, 1 - y))
    far = (jnp.where(south, x, 1 - x), jnp.where(south, 1 - y, y))
    diag = (1 - x, 1 - y)

    def block_of(chip, core):
        return 4 * chip[0] + 2 * chip[1] + core

    def copy(a, k, block, to, own=False):
        return _remote(src[a] if own else dst[a].at[block], dst[a].at[block], send_sems.at[a, k], recv_sems.at[a, k], to)

    local = [pltpu.make_async_copy(src[a], dst[a].at[me], local_sems.at[a]) for a in range(n)]
    sent = [copy(a, 0, me, sibling, True) for a in range(n)]
    sent += [copy(a, 1, me, (*near, c), True) for a in range(n)]
    sent += [copy(a, 2, me, (*far, c), True) for a in range(n)]
    for cp in local + sent:
        cp.start()
    yield
    for a in range(n):
        copy(a, 1, block_of(near, c), sibling).wait_recv()
        sent += [copy(a, 3, block_of(near, c), (*far, c)), copy(a, 4, block_of(near, c), sibling)]
        sent[-2].start()
        sent[-1].start()
    for k, chip in ((2, far), (3, diag)):
        for a in range(n):
            copy(a, k, block_of(chip, c), sibling).wait_recv()
            sent.append(copy(a, 3 + k, block_of(chip, c), sibling))
            sent[-1].start()
    for k, chip in ((0, (x, y)), (4, far), (5, near), (6, diag)):
        for a in range(n):
            copy(a, k, block_of(chip, 1 - c), sibling).wait_recv()
    for cp in sent:
        cp.wait_send()
    for cp in local:
        cp.wait()


GATHER_COPIES = 7


def _swap_with_all(buf, send_sems, recv_sems):
    x, y, c = _mesh_pos()
    me = 4 * x + 2 * y + c
    copies = [_remote(buf.at[me], buf.at[me], send_sems.at[k - 1], recv_sems.at[k - 1], _peer(k)[0])
              for k in range(1, N_DEV)]
    for cp in copies:
        cp.start()
    for cp in copies:
        cp.wait()


def _gather_weights_and_mod(shards, c_row, w_ada_blk, b_ada_blk):
    n = len(shards)

    def body(*refs):
        src, (c_ref, w_ref, b_ref) = refs[:n], refs[n:n + 3]
        dst, (mod_ref, sc_ref) = refs[n + 3:2 * n + 3], refs[2 * n + 3:2 * n + 5]
        c_all, g_send, g_recv, g_local, c_send, c_recv, m_send, m_recv = refs[2 * n + 5:]
        x, y, c = _mesh_pos()
        me = 4 * x + 2 * y + c
        gather = _two_level_gather(src, dst, g_send, g_recv, g_local)
        next(gather)
        c_all[me] = c_ref[...]
        _swap_with_all(c_all, c_send, c_recv)
        cv = jnp.concatenate([c_all[j] for j in range(N_DEV)] + [jnp.zeros((N_DEV, D_MODEL), F32)], axis=0)
        sc = cv * _sigmoid(cv)
        sc_ref[...] = sc
        blk = _dot(sc.astype(BF16), w_ref[...].astype(BF16)) + b_ref[...]
        mod_ref[me] = blk[0:N_DEV, :]
        _swap_with_all(mod_ref, m_send, m_recv)
        for _ in gather:
            pass

    hbm = pl.BlockSpec(memory_space=pltpu.HBM)
    vmem = pl.BlockSpec(memory_space=pltpu.VMEM)
    peers = pltpu.SemaphoreType.DMA((N_DEV - 1,))
    return pl.pallas_call(
        body, name="gather_weights",
        out_shape=tuple(jax.ShapeDtypeStruct((N_DEV,) + s.shape, s.dtype) for s in shards)
        + (jax.ShapeDtypeStruct((N_DEV, N_DEV, ADA_SHARD), F32), jax.ShapeDtypeStruct((2 * N_DEV, D_MODEL), F32)),
        in_specs=[hbm] * n + [vmem] * 3, out_specs=tuple([hbm] * n + [vmem] * 2),
        scratch_shapes=[pltpu.VMEM((N_DEV, 1, D_MODEL), F32),
                        pltpu.SemaphoreType.DMA((n, GATHER_COPIES)), pltpu.SemaphoreType.DMA((n, GATHER_COPIES)),
                        pltpu.SemaphoreType.DMA((n,)), peers, peers, peers, peers],
    )(*shards, c_row, w_ada_blk, b_ada_blk)


def _scatter_copies(src, land, send_sems, recv_sems, whole=False):
    x, y, c = _mesh_pos()
    me = 4 * x + 2 * y + c
    copies = []
    for k in range(1, N_DEV):
        peer, p = _peer(k)
        for a in range(len(src)):
            i = a * (N_DEV - 1) + k - 1
            copies.append(_remote(src[a] if whole else src[a].at[p], land[a].at[me], send_sems.at[i],
                                  recv_sems.at[i], peer))
    return copies


def _scatter_start(name, blocks, whole=False, after=None):
    n = len(blocks)
    extra = 0 if after is None else 1

    def body(*refs):
        src, land = refs[:n], refs[n:2 * n]
        send_sems, recv_sems = refs[2 * n + extra], refs[2 * n + extra + 1]
        token_ref = refs[-1]
        for cp in _scatter_copies(src, land, send_sems, recv_sems, whole):
            cp.start()
        token_ref[...] = jnp.zeros_like(token_ref)

    hbm = pl.BlockSpec(memory_space=pltpu.HBM)
    sem = pl.BlockSpec(memory_space=pltpu.SEMAPHORE)
    landing = [((N_DEV,) + b.shape if whole else b.shape, b.dtype) for b in blocks]
    through = tuple(pltpu.HBM(b.shape, b.dtype) for b in blocks) + tuple(pltpu.HBM(s, d) for s, d in landing)
    args = [pltpu.with_memory_space_constraint(b, pltpu.HBM) for b in blocks]
    args += [pltpu.with_memory_space_constraint(lax.empty(s, d), pltpu.HBM) for s, d in landing]
    out = pl.pallas_call(
        body, name=name,
        out_shape=(pltpu.SemaphoreType.DMA((n * (N_DEV - 1),)),) * 2 + through
        + (jax.ShapeDtypeStruct((8, 128), F32),),
        in_specs=[hbm] * (2 * n) + [pl.BlockSpec(memory_space=pl.ANY)] * extra,
        out_specs=(sem, sem) + (hbm,) * (2 * n) + (pl.BlockSpec(memory_space=pltpu.VMEM),),
        input_output_aliases={i: 2 + i for i in range(2 * n)},
        compiler_params=pltpu.CompilerParams(has_side_effects=pltpu.SideEffectType.DATAFLOW_SIDE_EFFECTING),
    )(*args, *([] if after is None else [after]))
    return out[:-1], out[-1][0:1, 0:1]


def _scatter_wait(name, state, after, whole=False):
    n = (len(state) - 2) // 2
    send_sems, recv_sems = state[0], state[1]
    src, land = state[2:2 + n], state[2 + n:]

    def body(*refs):
        src_r, land_r = refs[:n], refs[n:2 * n]
        for cp in _scatter_copies(src_r, land_r, refs[2 * n], refs[2 * n + 1], whole):
            cp.wait_send()
            cp.wait_recv()

    hbm = pl.BlockSpec(memory_space=pltpu.HBM)
    sem = pl.BlockSpec(memory_space=pltpu.SEMAPHORE)
    out = pl.pallas_call(
        body, name=name,
        out_shape=tuple(pltpu.HBM(b.shape, b.dtype) for b in src + land),
        in_specs=[hbm] * (2 * n) + [sem, sem, pl.BlockSpec(memory_space=pl.ANY)],
        out_specs=(hbm,) * (2 * n),
        input_output_aliases={i: i for i in range(2 * n)},
        compiler_params=pltpu.CompilerParams(has_side_effects=pltpu.SideEffectType.DATAFLOW_SIDE_EFFECTING),
    )(*src, *land, send_sems, recv_sems, after)
    me = 4 * lax.axis_index("x") + 2 * lax.axis_index("y") + lax.axis_index("c")
    landed = []
    for a in range(n):
        own = out[a][None] if whole else lax.dynamic_index_in_dim(out[a], me, axis=0, keepdims=True)
        landed.append(lax.dynamic_update_slice_in_dim(out[n + a], own, me, axis=0))
    return landed


def _fwd_proj(x, norm_g, scale, shift, w_in_t):
    seq = x.shape[0]
    tm = min(512, seq)
    sub = min(256, tm)
    tn = 512

    def body(x_ref, ng_ref, sc_ref, sh_ref, wt_ref, proj_ref, gates_ref, h_ref):
        def chain(n):
            for _ in range(n):
                yield
            rows = slice(n * sub, (n + 1) * sub)
            xt = x_ref[rows, :]
            r = lax.rsqrt(jnp.mean(xt * xt, axis=-1, keepdims=True) + EPS)
            h = ((xt * r) * ng_ref[...]) * (1.0 + sc_ref[...]) + sh_ref[...]
            hb = h.astype(BF16)
            h_ref[rows, :] = hb
            yield
            gates_ref[rows, :] = _dot_nt(hb, wt_ref[N_MAIN:N_MAIN + 128, :])
            for j in range(N_MAIN // tn):
                proj_ref[rows, j * tn:(j + 1) * tn] = _dot_nt(hb, wt_ref[j * tn:(j + 1) * tn, :])

        _in_lockstep(chain(n) for n in range(tm // sub))

    vec = _full((1, D_MODEL))
    tile = pl.BlockSpec((tm, D_MODEL), lambda i: (i, 0))
    return pl.pallas_call(
        body, name="fwd_proj", grid=(seq // tm,),
        out_shape=(jax.ShapeDtypeStruct((seq, N_MAIN), F32), jax.ShapeDtypeStruct((seq, 128), F32),
                   jax.ShapeDtypeStruct((seq, D_MODEL), BF16)),
        in_specs=[tile, vec, vec, vec, _full((N_PAD, D_MODEL))],
        out_specs=(pl.BlockSpec((tm, N_MAIN), lambda i: (i, 0)), pl.BlockSpec((tm, 128), lambda i: (i, 0)), tile),
        compiler_params=_params(("arbitrary",)),
    )(x, norm_g, scale, shift, w_in_t)


def _gate_forms(gpre):
    r = lax.broadcasted_iota(jnp.int32, (CHUNK, CHUNK), 0)
    c = lax.broadcasted_iota(jnp.int32, (CHUNK, CHUNK), 1)
    causal = c <= r
    ltri = jnp.where(causal, 1.0, 0.0).astype(F32)
    utri = jnp.where(r <= c, 1.0, 0.0).astype(F32)
    bcol = _dot_f32(ltri, _log_sigmoid(gpre))
    gt8 = gpre.T[0:8, :]
    brow = _dot_f32(_log_sigmoid(gt8), utri)
    return causal, utri, bcol, gt8, brow


def _in_lockstep(stages):
    alive = list(stages)
    while alive:
        still = []
        for g in alive:
            try:
                next(g)
                still.append(g)
            except StopIteration:
                pass
        alive = still


def _head_fwd(qh, kh, vh, bc, br, igr, m_prev, c_h, n_row, causal):
    qb, kb, vb, cb = qh.astype(BF16), kh.astype(BF16), vh.astype(BF16), c_h.astype(BF16)
    qk = _dot_nt(qb, kb)
    cq = _dot_nt(qb, cb)
    nq = _dot_nt(qb, jnp.broadcast_to(n_row.astype(BF16), (HEAD_DIM, HEAD_DIM)))
    yield
    dlog = jnp.where(causal, bc - br + igr, NEG_BIG)
    inter_log = bc + m_prev
    m_t = jnp.maximum(inter_log, jnp.max(dlog, axis=-1, keepdims=True))
    yield
    dmat = jnp.exp(dlog - m_t)
    inter = jnp.exp(inter_log - m_t)
    s = qk * dmat
    sv = _dot(s.astype(BF16), vb)
    yield
    den = jnp.sum(s, axis=-1, keepdims=True) + inter * nq
    emt = jnp.exp(-m_t)
    yield
    num = sv + inter * cq
    dn = jnp.maximum(jnp.abs(den), emt)
    hm = num / dn
    return dict(dmat=dmat, inter=inter, qb=qb, kb=kb, vb=vb, cb=cb, s=s, cq=cq, nq=nq, den=den, emt=emt,
                dn=dn, hm=hm)


def _state_weights(bc, igc, m_prev, m_new=None):
    last = lax.broadcasted_iota(jnp.int32, (CHUNK, 1), 0) == CHUNK - 1
    b_last = jnp.sum(jnp.where(last, bc, 0.0), axis=0, keepdims=True)
    wlog = b_last - bc + igc
    if m_new is None:
        m_new = jnp.maximum(b_last + m_prev, jnp.max(wlog, axis=0, keepdims=True))
    w_c = jnp.exp(wlog - m_new)
    decay = jnp.exp(b_last + m_prev - m_new)
    return w_c, decay, m_new, last


def _rows_back(x, k):
    return x if k == 0 else pltpu.roll(x, k, 0)


def _rows_ahead(x, k):
    return x if k == 0 else pltpu.roll(x, x.shape[0] - k, 0)


def _conv_taps(xpad):
    return [_rows_back(xpad, CONV_WIDTH - 1 - j)[CONV_HALO:, :] for j in range(CONV_WIDTH)]


def _conv_pre(taps, cw_ref, cb_ref):
    a = cb_ref[...]
    for j in range(CONV_WIDTH):
        a = a + cw_ref[j:j + 1, :] * taps[j]
    return a


def _window_sum(x, w, shift):
    k = 1
    while k < w:
        x = x + shift(x, k)
        k *= 2
    return x


def _pool_window_sum(upad_ref, g, w):
    lanes = slice(g * POOL_GROUP_DIM, (g + 1) * POOL_GROUP_DIM)
    return _window_sum(upad_ref[:, lanes], w, _rows_back)[POOL_HALO:, :]


def _pool_inv_count(row0, rows, w):
    pos = row0 + lax.broadcasted_iota(jnp.int32, (rows, 1), 0) + 1
    return 1.0 / jnp.minimum(pos, w).astype(F32)


FWD_CHUNKS = 2
BWD_CHUNKS = 4


def _mix_fwd(proj, gates, bg_pad, conv_w8, conv_b, w_pool, ls_pool, mh_g):
    seq = proj.shape[0]
    n_chunks = seq // CHUNK
    per_step = FWD_CHUNKS
    blk = per_step * CHUNK

    def body(uz_ref, qk_ref, v_ref, o_ref, zm_ref, uh_ref, qkh_ref, g_ref, bg_ref, cw_ref, cb_ref, wp_ref,
             ls_ref, mhg_ref, mix_ref, cst_ref, nst_ref, mst_ref, a_ref, pooled_ref, c_scr, n_scr, m_scr, xpad, upad):
        i = pl.program_id(0)

        @pl.when(i == 0)
        def _():
            c_scr[...] = jnp.zeros_like(c_scr)
            n_scr[...] = jnp.zeros_like(n_scr)
            m_scr[...] = jnp.zeros_like(m_scr)

        first = i == 0

        upad[0:POOL_HALO, :] = jnp.where(first, 0.0, uh_ref[...])
        upad[POOL_HALO:POOL_HALO + blk, :] = uz_ref[:, 0:D_POOL]
        for g, w in enumerate(POOL_WINDOWS):
            lanes = slice(g * POOL_GROUP_DIM, (g + 1) * POOL_GROUP_DIM)
            pooled = (_pool_window_sum(upad, g, w) * _pool_inv_count(i * blk, blk, w) - uz_ref[:, lanes]).astype(BF16)
            pooled_ref[:, lanes] = pooled
            y = _dot(pooled, wp_ref[g].astype(BF16)) * ls_ref[:, lanes]
            zp = uz_ref[:, D_POOL + g * POOL_GROUP_DIM:D_POOL + (g + 1) * POOL_GROUP_DIM]
            mix_ref[:, lanes] = (y * (zp * _sigmoid(zp))).astype(BF16)

        xpad[0:CONV_HALO, :] = jnp.where(first, 0.0, qkh_ref[...])
        xpad[CONV_HALO:CONV_HALO + blk, :] = qk_ref[...]
        a = _conv_pre(_conv_taps(xpad[...]), cw_ref, cb_ref)
        a_ref[...] = a
        qk = a * _sigmoid(a)

        def head(rows, h, qh, kh, vh, bc, br, igr, m_prev, c_h, n_row, causal):
            lanes = slice(h * HEAD_DIM, (h + 1) * HEAD_DIM)
            f = yield from _head_fwd(qh, kh, vh, bc, br, igr, m_prev, c_h, n_row, causal)
            yield
            hm = f["hm"]
            hn = hm * lax.rsqrt(_row_mean_mxu(hm * hm) + EPS) * mhg_ref[:, lanes]
            zm = zm_ref[rows, lanes]
            out = hn * _sigmoid(o_ref[rows, lanes]) * (zm * _sigmoid(zm))
            mix_ref[rows, D_POOL + h * HEAD_DIM:D_POOL + (h + 1) * HEAD_DIM] = out.astype(BF16)

        c_cur = [c_scr[h] for h in range(N_HEADS)]
        n_cur = [n_scr[h:h + 1, :] for h in range(N_HEADS)]
        m_cur = [m_scr[h:h + 1, 0:1] for h in range(N_HEADS)]
        chains = []
        for s in range(per_step):
            rows = slice(s * CHUNK, (s + 1) * CHUNK)
            gpre = g_ref[rows, :] + bg_ref[...]
            causal, _, bcol, gt8, brow = _gate_forms(gpre)
            nst_ref[s] = jnp.zeros((8, 128), F32)
            mst_ref[s] = jnp.zeros((8, 128), F32)
            for h in range(N_HEADS):
                lanes = slice(h * HEAD_DIM, (h + 1) * HEAD_DIM)
                cst_ref[s, h] = c_cur[h]
                nst_ref[s, h:h + 1, :] = n_cur[h]
                mst_ref[s, h:h + 1, :] = jnp.broadcast_to(m_cur[h], (1, 128))
                qh = qk[rows, lanes]
                kh = qk[rows, D_MLSTM + h * HEAD_DIM:D_MLSTM + (h + 1) * HEAD_DIM] * (HEAD_DIM ** -0.5)
                vh = v_ref[rows, lanes]
                bc = bcol[:, N_HEADS + h:N_HEADS + h + 1]
                br = brow[N_HEADS + h:N_HEADS + h + 1, :]
                igr = gt8[h:h + 1, :]
                igc = gpre[:, h:h + 1]
                chains.append(head(rows, h, qh, kh, vh, bc, br, igr, m_cur[h], c_cur[h], n_cur[h], causal))
                w_c, decay, m_new, _ = _state_weights(bc, igc, m_cur[h])
                c_cur[h] = decay * c_cur[h] + _dot_tn((vh * w_c).astype(BF16), kh.astype(BF16))
                n_cur[h] = decay * n_cur[h] + jnp.sum(w_c * kh, axis=0, keepdims=True)
                m_cur[h] = m_new
        for h in range(N_HEADS):
            c_scr[h] = c_cur[h]
            n_scr[h:h + 1, :] = n_cur[h]
            m_scr[h:h + 1, :] = jnp.broadcast_to(m_cur[h], (1, 128))
        _in_lockstep(chains)

    in_specs = [
        pl.BlockSpec((blk, 1024), lambda i: (i, 0)),
        pl.BlockSpec((blk, 1024), lambda i: (i, 1)),
        pl.BlockSpec((blk, 512), lambda i: (i, 4)),
        pl.BlockSpec((blk, 512), lambda i: (i, 5)),
        pl.BlockSpec((blk, 512), lambda i: (i, 6)),
        pl.BlockSpec((POOL_HALO, 512), lambda i: (jnp.maximum(i * (blk // POOL_HALO) - 1, 0), 0)),
        pl.BlockSpec((CONV_HALO, 1024), lambda i: (jnp.maximum(i * (blk // CONV_HALO) - 1, 0), 1)),
        pl.BlockSpec((blk, 128), lambda i: (i, 0)),
        _full((1, 128)), _full((8, 1024)), _full((1, 1024)), _full((4, 128, 128)), _full((1, 512)),
        _full((1, 512))]
    return pl.pallas_call(
        body, name="mix_fwd", grid=(n_chunks // per_step,),
        out_shape=(jax.ShapeDtypeStruct((seq, D_MODEL), BF16),
                   jax.ShapeDtypeStruct((n_chunks, N_HEADS, HEAD_DIM, HEAD_DIM), F32),
                   jax.ShapeDtypeStruct((n_chunks, 8, 128), F32),
                   jax.ShapeDtypeStruct((n_chunks, 8, 128), F32),
                   jax.ShapeDtypeStruct((seq, 2 * D_MLSTM), F32),
                   jax.ShapeDtypeStruct((seq, D_POOL), BF16)),
        in_specs=in_specs,
        out_specs=(pl.BlockSpec((blk, D_MODEL), lambda i: (i, 0)),
                   pl.BlockSpec((per_step, N_HEADS, HEAD_DIM, HEAD_DIM), lambda i: (i, 0, 0, 0)),
                   pl.BlockSpec((per_step, 8, 128), lambda i: (i, 0, 0)),
                   pl.BlockSpec((per_step, 8, 128), lambda i: (i, 0, 0)),
                   pl.BlockSpec((blk, 2 * D_MLSTM), lambda i: (i, 0)),
                   pl.BlockSpec((blk, D_POOL), lambda i: (i, 0))),
        scratch_shapes=[pltpu.VMEM((N_HEADS, HEAD_DIM, HEAD_DIM), F32), pltpu.VMEM((8, 128), F32),
                        pltpu.VMEM((8, 128), F32), pltpu.VMEM((CONV_HALO + blk, 1024), F32),
                        pltpu.VMEM((POOL_HALO + blk, D_POOL), F32)],
        compiler_params=_params(("arbitrary",)),
    )(proj, proj, proj, proj, proj, proj, proj, gates, bg_pad, conv_w8, conv_b, w_pool, ls_pool, mh_g)


def _out_fwd_bwd(mix, x, tgt, w_out_b, gate, final_g):
    seq = x.shape[0]
    tm = min(512, seq)
    sub = min(256, tm)

    def body(mix_ref, x_ref, t_ref, w_ref, gate_ref, fg_ref, dx2_ref, dmix_ref, dwo_ref, dgate_ref, dfg_ref,
             loss_ref, dwo_acc):
        @pl.when(pl.program_id(0) == 0)
        def _():
            dwo_acc[...] = jnp.zeros_like(dwo_acc)
            dgate_ref[...] = jnp.zeros_like(dgate_ref)
            dfg_ref[...] = jnp.zeros_like(dfg_ref)
            loss_ref[...] = jnp.zeros_like(loss_ref)

        w = w_ref[...]
        gate_v = gate_ref[...]
        fg = fg_ref[...]
        do2_parts = [None] * (tm // sub)

        def chain(n):
            rows = slice(n * sub, (n + 1) * sub)
            o2 = _dot(mix_ref[rows, :], w)
            yield
            x2 = x_ref[rows, :] + gate_v * o2
            r2 = lax.rsqrt(jnp.mean(x2 * x2, axis=-1, keepdims=True) + EPS)
            x2n = x2 * r2
            err = x2n * fg - t_ref[rows, :]
            part = 0.5 * jnp.sum(jnp.sum(err * err, axis=-1, keepdims=True), axis=0, keepdims=True) / D_MODEL
            loss_ref[...] += jnp.broadcast_to(part, loss_ref.shape)
            dy = err / D_MODEL
            dfg_ref[...] += jnp.sum(dy * x2n, axis=0, keepdims=True)
            gdy = dy * fg
            dx2 = r2 * (gdy - x2n * jnp.mean(gdy * x2n, axis=-1, keepdims=True))
            dx2_ref[rows, :] = dx2
            dgate_ref[...] += jnp.sum(dx2 * o2, axis=0, keepdims=True)
            do2 = (dx2 * gate_v).astype(BF16)
            dmix_ref[rows, :] = _dot_nt(do2, w)
            do2_parts[n] = do2

        _in_lockstep(chain(n) for n in range(tm // sub))
        dwo_acc[...] += _dot_tn(mix_ref[...], jnp.concatenate(do2_parts, axis=0))

        @pl.when(pl.program_id(0) == seq // tm - 1)
        def _():
            dwo_ref[...] = dwo_acc[...].astype(BF16)

    tile = pl.BlockSpec((tm, D_MODEL), lambda i: (i, 0))
    vec = _full((1, D_MODEL))
    return pl.pallas_call(
        body, name="out_fwd_bwd", grid=(seq // tm,),
        out_shape=(jax.ShapeDtypeStruct((seq, D_MODEL), F32), jax.ShapeDtypeStruct((seq, D_MODEL), F32),
                   jax.ShapeDtypeStruct((D_MODEL, D_MODEL), BF16), jax.ShapeDtypeStruct((1, D_MODEL), F32),
                   jax.ShapeDtypeStruct((1, D_MODEL), F32), jax.ShapeDtypeStruct((1, 128), F32)),
        in_specs=[tile, tile, tile, _full((D_MODEL, D_MODEL)), vec, vec],
        out_specs=(tile, tile, _full((D_MODEL, D_MODEL)), vec, vec, _full((1, 128))),
        scratch_shapes=[pltpu.VMEM((D_MODEL, D_MODEL), F32)],
        compiler_params=_params(("arbitrary",)),
    )(mix, x, tgt, w_out_b, gate, final_g)


def _mix_bwd(proj, gates, dmix, conv_a, pooled, cst, nst, mst, bg_pad, conv_w8, w_pool, ls_pool, mh_g):
    seq = proj.shape[0]
    n_chunks = seq // CHUNK
    per_step = BWD_CHUNKS
    blk = per_step * CHUNK
    n_blocks = n_chunks // per_step

    def body(zp_ref, qk_ref, v_ref, o_ref, zm_ref, g_ref, dmix_ref, a_ref, pooled_ref, cst_ref, nst_ref, mst_ref,
             mnx_ref, bg_ref, cw_ref, wp_ref, ls_ref, mhg_ref,
             dp_ref, dcw_ref, dcb_ref, dwp_ref, dls_ref, dmhg_ref, dbg_ref,
             dc_scr, dn_scr, dapad, dpipad):
        i = pl.program_id(0)
        bi = n_blocks - 1 - i

        @pl.when(i == 0)
        def _():
            for ref in (dc_scr, dn_scr, dcw_ref, dcb_ref, dwp_ref, dls_ref, dmhg_ref, dbg_ref):
                ref[...] = jnp.zeros_like(ref)
            dapad[blk:blk + CONV_HALO, :] = jnp.zeros((CONV_HALO, 1024), F32)
            dpipad[blk:blk + POOL_HALO, :] = jnp.zeros((POOL_HALO, D_POOL), F32)

        dpooled = []
        for g, w in enumerate(POOL_WINDOWS):
            lanes = slice(g * POOL_GROUP_DIM, (g + 1) * POOL_GROUP_DIM)
            zlanes = slice(D_POOL + g * POOL_GROUP_DIM, D_POOL + (g + 1) * POOL_GROUP_DIM)
            inv = _pool_inv_count(bi * blk, blk, w)
            pb = pooled_ref[:, lanes]
            wpb = wp_ref[g].astype(BF16)
            yw = _dot(pb, wpb)
            ls = ls_ref[:, lanes]
            zp = zp_ref[:, lanes]
            sg = _sigmoid(zp)
            dpo = dmix_ref[:, lanes]
            dp_ref[:, zlanes] = (dpo * (yw * ls) * (sg * (1.0 + zp * (1.0 - sg)))).astype(BF16)
            dy = dpo * (zp * sg)
            dls_ref[:, lanes] += jnp.sum(dy * yw, axis=0, keepdims=True)
            dyw = (dy * ls).astype(BF16)
            dwp_ref[g] += _dot_tn(pb, dyw)
            dpl = _dot_nt(dyw, wpb)
            dpooled.append(dpl)
            dpipad[0:blk, lanes] = dpl * inv
        for g, w in enumerate(POOL_WINDOWS):
            lanes = slice(g * POOL_GROUP_DIM, (g + 1) * POOL_GROUP_DIM)
            du = _window_sum(dpipad[:, lanes], w, _rows_ahead)[0:blk, :] - dpooled[g]
            dp_ref[:, lanes] = du.astype(BF16)
        dpipad[blk:blk + POOL_HALO, :] = dpipad[0:POOL_HALO, :]

        def silu_and_slope(rows, cols):
            a = a_ref[rows, cols]
            sg = _sigmoid(a)
            return a * sg, sg * (1.0 + a * (1.0 - sg))

        lane = lax.broadcasted_iota(jnp.int32, (CHUNK, 128), 1)
        row = lax.broadcasted_iota(jnp.int32, (CHUNK, 128), 0)
        scale_k = HEAD_DIM ** -0.5
        forms = [None] * per_step
        col_g_rows = [[] for _ in range(per_step)]
        dig_parts = [[] for _ in range(per_step)]
        db_parts = [[] for _ in range(per_step)]
        d_state = [[None] * N_HEADS for _ in range(per_step)]

        def state_terms(s, h, c_h, n_row, vb, kb):
            dcn, dnn = d_state[s][h]
            dcnb = dcn.astype(BF16)
            amat = _dot(vb, dcnb) + dnn
            kdc = _dot_nt(kb, dcnb)
            ddecay = (jnp.sum(jnp.sum(dcn * c_h, axis=-1, keepdims=True), axis=0, keepdims=True)
                      + jnp.sum(dnn * n_row, axis=-1, keepdims=True))
            return dcn, dnn, amat, kdc, ddecay

        def head(s, h):
            rows = slice(s * CHUNK, (s + 1) * CHUNK)
            lanes = slice(h * HEAD_DIM, (h + 1) * HEAD_DIM)
            klanes = slice(D_MLSTM + h * HEAD_DIM, D_MLSTM + (h + 1) * HEAD_DIM)
            gpre, causal, utri, bcol, gt8, brow = forms[s]
            qh, dsilu_q = silu_and_slope(rows, lanes)
            kh, dsilu_k = silu_and_slope(rows, klanes)
            kh = kh * scale_k
            vh = v_ref[rows, lanes]
            bc = bcol[:, N_HEADS + h:N_HEADS + h + 1]
            br = brow[N_HEADS + h:N_HEADS + h + 1, :]
            igr = gt8[h:h + 1, :]
            igc = gpre[:, h:h + 1]
            m_prev = mst_ref[s, h:h + 1, 0:1]
            m_next = mnx_ref[0, h:h + 1, 0:1] if s == per_step - 1 else mst_ref[s + 1, h:h + 1, 0:1]
            c_h = cst_ref[s, h]
            n_row = nst_ref[s, h:h + 1, :]
            w_c, decay, _, last = _state_weights(bc, igc, m_prev, m_next)
            terms = None
            if s == per_step - 1:
                terms = state_terms(s, h, c_h, n_row, vh.astype(BF16), kh.astype(BF16))
            f = yield from _head_fwd(qh, kh, vh, bc, br, igr, m_prev, c_h, n_row, causal)
            qb, kb, vb, cb = f["qb"], f["kb"], f["vb"], f["cb"]
            sm, dmat, inter, den, dn, hm = f["s"], f["dmat"], f["inter"], f["den"], f["dn"], f["hm"]
            yield

            rinv = lax.rsqrt(_row_mean_mxu(hm * hm) + EPS)
            hmn = hm * rinv
            gh = mhg_ref[:, lanes]
            o_pre = o_ref[rows, lanes]
            og = _sigmoid(o_pre)
            zm = zm_ref[rows, lanes]
            sgz = _sigmoid(zm)
            sz = zm * sgz
            dout = dmix_ref[rows, D_POOL + h * HEAD_DIM:D_POOL + (h + 1) * HEAD_DIM]
            hn = hmn * gh
            dp_ref[rows, 2560 + h * HEAD_DIM:2560 + (h + 1) * HEAD_DIM] = (
                dout * hn * sz * og * (1.0 - og)).astype(BF16)
            dp_ref[rows, 3072 + h * HEAD_DIM:3072 + (h + 1) * HEAD_DIM] = (
                dout * hn * og * (sgz * (1.0 + zm * (1.0 - sgz)))).astype(BF16)
            dhn = dout * og * sz
            dmhg_ref[:, lanes] += jnp.sum(dhn * hmn, axis=0, keepdims=True)
            dyn = dhn * gh
            dhm = rinv * (dyn - hmn * _row_mean_mxu(dyn * hmn))
            yield

            inv_dn = 1.0 / dn
            dnum = dhm * inv_dn
            hd = jnp.sum(dhm * hm, axis=-1, keepdims=True)
            dden = jnp.where(jnp.abs(den) > f["emt"], -hd * inv_dn * jnp.sign(den), 0.0)
            dnb = dnum.astype(BF16)
            dnv = _dot_nt(dnb, vb)
            dv = _dot_tn(sm.astype(BF16), dnb)
            dnc = _dot(dnb, cb)
            dc_prev = _dot_tn((inter * dnum).astype(BF16), qb)
            dn_prev = jnp.sum((inter * dden) * qh, axis=0, keepdims=True)
            yield
            ds = dnv + dden
            dqk = (ds * dmat).astype(BF16)
            dqk_k = _dot(dqk, kb)
            dk = _dot_tn(dqk, qb)
            for _ in range(per_step - 1 - s):
                yield
            if terms is None:
                terms = state_terms(s, h, c_h, n_row, vb, kb)
            dcn, dnn, amat, kdc, ddecay = terms
            d_start = (decay * dcn + dc_prev, decay * dnn + dn_prev)
            if s > 0:
                d_state[s - 1][h] = d_start
            else:
                dc_scr[h] = d_start[0]
                dn_scr[h:h + 1, :] = d_start[1]
            yield
            gmat = ds * sm
            row_g = jnp.sum(gmat, axis=-1, keepdims=True)
            col_g_rows[s].append(jnp.where(row == h, jnp.sum(gmat, axis=0, keepdims=True), 0.0))
            gcol = inter * (jnp.sum(dnum * f["cq"], axis=-1, keepdims=True) + dden * f["nq"])
            dw = jnp.sum(amat * kh, axis=-1, keepdims=True)
            e = dw * w_c
            db_last = ddecay * decay + jnp.sum(e, axis=0, keepdims=True)
            dig_parts[s].append(jnp.where(lane == h, e, 0.0))
            db_parts[s].append(
                jnp.where(lane == N_HEADS + h, row_g + gcol - e + jnp.where(last, db_last, 0.0), 0.0))
            yield
            dq = dqk_k + inter * (dnc + dden * n_row)
            dp_ref[rows, 2048 + h * HEAD_DIM:2048 + (h + 1) * HEAD_DIM] = (dv + w_c * kdc).astype(BF16)
            dapad[rows, lanes] = dq * dsilu_q
            dapad[rows, klanes] = (dk + w_c * amat) * scale_k * dsilu_k

        chains = []
        for s in reversed(range(per_step)):
            gpre = g_ref[s * CHUNK:(s + 1) * CHUNK, :] + bg_ref[...]
            forms[s] = (gpre,) + _gate_forms(gpre)
            for h in range(N_HEADS):
                if s == per_step - 1:
                    d_state[s][h] = (dc_scr[h], dn_scr[h:h + 1, :])
                chains.append(head(s, h))
        _in_lockstep(chains)

        for s in range(per_step):
            rows = slice(s * CHUNK, (s + 1) * CHUNK)
            gpre, utri = forms[s][0], forms[s][2]
            cs_t = sum(col_g_rows[s][1:], col_g_rows[s][0]).T
            dig_all = sum(dig_parts[s][1:], dig_parts[s][0]) + cs_t
            db_cols = sum(db_parts[s][1:], db_parts[s][0])
            shifted = jnp.zeros((CHUNK, 128), F32)
            for h in range(N_HEADS):
                shifted = shifted + jnp.where(lane == N_HEADS + h, cs_t[:, h:h + 1], 0.0)
            dlf = _dot_f32(utri, db_cols - shifted)
            dgates = dig_all + dlf * _sigmoid(-gpre)
            dp_ref[rows, N_MAIN:N_MAIN + 128] = dgates.astype(BF16)
            dbg_ref[...] += jnp.sum(dgates, axis=0, keepdims=True)
        dp_ref[:, N_MAIN + 128:N_PAD] = jnp.zeros((blk, N_PAD - N_MAIN - 128), BF16)

        da_pad = dapad[...]
        da = da_pad[0:blk, :]
        dcb_ref[...] += jnp.sum(da, axis=0, keepdims=True)
        x = qk_ref[...]
        dx = jnp.zeros((blk, 1024), F32)
        for j in range(CONV_WIDTH):
            da_j = _rows_ahead(da_pad, CONV_WIDTH - 1 - j)[0:blk, :]
            dcw_ref[j:j + 1, :] += jnp.sum(da_j * x, axis=0, keepdims=True)
            dx = dx + cw_ref[j:j + 1, :] * da_j
        dp_ref[:, 1024:2048] = dx.astype(BF16)
        dapad[blk:blk + CONV_HALO, :] = dapad[0:CONV_HALO, :]

    bmap = lambda i: n_blocks - 1 - i
    wide = pl.BlockSpec((blk, 1024), lambda i: (bmap(i), 0))
    state = pl.BlockSpec((per_step, 8, 128), lambda i: (bmap(i), 0, 0))
    in_specs = [
        pl.BlockSpec((blk, 512), lambda i: (bmap(i), 1)),
        pl.BlockSpec((blk, 1024), lambda i: (bmap(i), 1)),
        pl.BlockSpec((blk, 512), lambda i: (bmap(i), 4)),
        pl.BlockSpec((blk, 512), lambda i: (bmap(i), 5)),
        pl.BlockSpec((blk, 512), lambda i: (bmap(i), 6)),
        pl.BlockSpec((blk, 128), lambda i: (bmap(i), 0)),
        wide, wide,
        pl.BlockSpec((blk, D_POOL), lambda i: (bmap(i), 0)),
        pl.BlockSpec((per_step, N_HEADS, HEAD_DIM, HEAD_DIM), lambda i: (bmap(i), 0, 0, 0)),
        state, state,
        pl.BlockSpec((1, 8, 128), lambda i: (jnp.minimum((bmap(i) + 1) * per_step, n_chunks - 1), 0, 0)),
        _full((1, 128)), _full((8, 1024)), _full((4, 128, 128)), _full((1, 512)), _full((1, 512))]
    return pl.pallas_call(
        body, name="mix_bwd", grid=(n_blocks,),
        out_shape=(jax.ShapeDtypeStruct((seq, N_PAD), BF16), jax.ShapeDtypeStruct((8, 1024), F32),
                   jax.ShapeDtypeStruct((1, 1024), F32), jax.ShapeDtypeStruct((4, 128, 128), F32),
                   jax.ShapeDtypeStruct((1, 512), F32), jax.ShapeDtypeStruct((1, 512), F32),
                   jax.ShapeDtypeStruct((1, 128), F32)),
        in_specs=in_specs,
        out_specs=(pl.BlockSpec((blk, N_PAD), lambda i: (bmap(i), 0)), _full((8, 1024)), _full((1, 1024)),
                   _full((4, 128, 128)), _full((1, 512)), _full((1, 512)), _full((1, 128))),
        scratch_shapes=[pltpu.VMEM((N_HEADS, HEAD_DIM, HEAD_DIM), F32), pltpu.VMEM((8, 128), F32),
                        pltpu.VMEM((blk + CONV_HALO, 1024), F32), pltpu.VMEM((blk + POOL_HALO, D_POOL), F32)],
        compiler_params=_params(("arbitrary",)),
    )(proj, proj, proj, proj, proj, gates, dmix, conv_a, pooled, cst, nst, mst, mst, bg_pad, conv_w8,
      w_pool, ls_pool, mh_g)


def _bwd_in(dproj, w_in_t, x, dx2, norm_g, scale):
    seq = x.shape[0]
    tm = min(512, seq)
    sub = min(256, tm)

    def body(dp_ref, wt_ref, x_ref, dx2_ref, ng_ref, sc_ref, gx_ref, dsh_ref, dsc_ref, dng_ref):
        @pl.when(pl.program_id(0) == 0)
        def _():
            dsh_ref[...] = jnp.zeros_like(dsh_ref)
            dsc_ref[...] = jnp.zeros_like(dsc_ref)
            dng_ref[...] = jnp.zeros_like(dng_ref)

        ng = ng_ref[...]
        one_sc = 1.0 + sc_ref[...]

        def chain(n):
            rows = slice(n * sub, (n + 1) * sub)
            dh = _dot(dp_ref[rows, :], wt_ref[...])
            yield
            xt = x_ref[rows, :]
            r = lax.rsqrt(jnp.mean(xt * xt, axis=-1, keepdims=True) + EPS)
            xn = xt * r
            dsh_ref[...] += jnp.sum(dh, axis=0, keepdims=True)
            dhxn = dh * xn
            dsc_ref[...] += jnp.sum(dhxn * ng, axis=0, keepdims=True)
            dng_ref[...] += jnp.sum(dhxn * one_sc, axis=0, keepdims=True)
            dxn = dh * (ng * one_sc)
            gx_ref[rows, :] = r * (dxn - xn * jnp.mean(dxn * xn, axis=-1, keepdims=True)) + dx2_ref[rows, :]

        _in_lockstep(chain(n) for n in range(tm // sub))

    tile = pl.BlockSpec((tm, D_MODEL), lambda i: (i, 0))
    vec = _full((1, D_MODEL))
    return pl.pallas_call(
        body, name="bwd_in", grid=(seq // tm,),
        out_shape=(jax.ShapeDtypeStruct((seq, D_MODEL), F32),) + (jax.ShapeDtypeStruct((1, D_MODEL), F32),) * 3,
        in_specs=[pl.BlockSpec((tm, N_PAD), lambda i: (i, 0)), _full((N_PAD, D_MODEL)), tile, tile, vec, vec],
        out_specs=(tile, vec, vec, vec),
        compiler_params=_params(("arbitrary",)),
    )(dproj, w_in_t, x, dx2, norm_g, scale)


def _dw_in(h_b, dproj):
    seq = h_b.shape[0]
    tk = min(4096, seq)
    tn = 768
    n_t = seq // tk

    def body(h_ref, dp_ref, dwt_ref, acc):
        t = pl.program_id(1)

        @pl.when(t == 0)
        def _():
            acc[...] = jnp.zeros_like(acc)

        acc[...] += _dot_tn(dp_ref[...], h_ref[...])

        @pl.when(t == n_t - 1)
        def _():
            dwt_ref[...] = acc[...].astype(BF16)

    return pl.pallas_call(
        body, name="dw_in", grid=(N_PAD // tn, n_t),
        out_shape=jax.ShapeDtypeStruct((N_PAD, D_MODEL), BF16),
        in_specs=[pl.BlockSpec((tk, D_MODEL), lambda j, t: (t, 0)), pl.BlockSpec((tk, tn), lambda j, t: (t, j))],
        out_specs=pl.BlockSpec((tn, D_MODEL), lambda j, t: (j, 0)),
        scratch_shapes=[pltpu.VMEM((tn, D_MODEL), F32)],
        compiler_params=_params(("arbitrary", "arbitrary")),
    )(h_b, dproj)


def _adam_update(g, w, m, v, g_ref, d_ref, m_ref, v_ref):
    mn = ADAM_B1 * m + (1.0 - ADAM_B1) * g
    vn = ADAM_B2 * v + (1.0 - ADAM_B2) * (g * g)
    m_hat = mn / (1.0 - ADAM_B1 ** ADAM_STEP)
    v_hat = vn / (1.0 - ADAM_B2 ** ADAM_STEP)
    g_ref[...] = g
    d_ref[...] = -ADAM_LR * (m_hat / (jnp.sqrt(v_hat) + ADAM_EPS) + ADAM_WD * w)
    m_ref[...] = mn
    v_ref[...] = vn


def _adam_sum(name, parts, w, m, v, row_tile, col_tile=None):
    rows, cols = w.shape
    col_tile = cols if col_tile is None else col_tile
    n_parts = parts.shape[0]

    def body(p_ref, w_ref, m_ref, v_ref, g_out, d_out, m_out, v_out):
        g = p_ref[0].astype(F32)
        for j in range(1, n_parts):
            g = g + p_ref[j].astype(F32)
        _adam_update(g, w_ref[...], m_ref[...], v_ref[...], g_out, d_out, m_out, v_out)

    tile = pl.BlockSpec((row_tile, col_tile), lambda i, j: (i, j))
    return pl.pallas_call(
        body, name=name, grid=(rows // row_tile, cols // col_tile),
        out_shape=(jax.ShapeDtypeStruct((rows, cols), F32),) * 4,
        in_specs=[pl.BlockSpec((n_parts, row_tile, col_tile), lambda i, j: (0, i, j)), tile, tile, tile],
        out_specs=(tile,) * 4,
        compiler_params=_params(("arbitrary", "arbitrary")),
    )(parts, w, m, v)


def _adam_ada(sc_all16, dmod_blk16, w, m, v):
    rows, cols = w.shape

    def body(sc_ref, dm_ref, w_ref, m_ref, v_ref, g_out, d_out, m_out, v_out):
        g = _dot_tn(sc_ref[...].astype(BF16), dm_ref[...].astype(BF16))
        _adam_update(g, w_ref[...], m_ref[...], v_ref[...], g_out, d_out, m_out, v_out)

    return pl.pallas_call(
        body, name="adam_w_ada", grid=(1,),
        out_shape=(jax.ShapeDtypeStruct((rows, cols), F32),) * 4,
        in_specs=[_full(sc_all16.shape), _full(dmod_blk16.shape)] + [_full((rows, cols))] * 3,
        out_specs=(_full((rows, cols)),) * 4,
        compiler_params=_params(("arbitrary",)),
    )(sc_all16, dmod_blk16, w, m, v)


def _adam_small(parts, loss_parts, w, m, v):
    names = list(w)
    n = len(names)

    def body(*refs):
        p_refs, loss_ref = refs[:n], refs[n]
        w_refs, m_refs, v_refs = (refs[n + 1 + k * n:n + 1 + (k + 1) * n] for k in range(3))
        outs = refs[3 * n + n + 1:]
        for a in range(n):
            g = p_refs[a][0]
            for j in range(1, N_DEV):
                g = g + p_refs[a][j]
            width = w_refs[a].shape[-1]
            if g.shape[-1] != width:
                g = g[..., 0:width]
            _adam_update(g, w_refs[a][...], m_refs[a][...], v_refs[a][...], *outs[4 * a:4 * a + 4])
        total = loss_ref[0]
        for j in range(1, N_DEV):
            total = total + loss_ref[j]
        outs[4 * n][...] = total

    args = [parts[k] for k in names] + [loss_parts] + [d[k] for d in (w, m, v) for k in names]
    out_shape = tuple(jax.ShapeDtypeStruct(w[k].shape, F32) for k in names for _ in range(4))
    out_shape += (jax.ShapeDtypeStruct(loss_parts.shape[1:], F32),)
    out = pl.pallas_call(
        body, name="adam_small", grid=(1,), out_shape=out_shape,
        in_specs=[_full(a.shape) for a in args], out_specs=tuple(_full(s.shape) for s in out_shape),
        compiler_params=_params(("arbitrary",)),
    )(*args)
    return {k: out[4 * a:4 * a + 4] for a, k in enumerate(names)}, out[4 * n]


def _local_step(x2, tgt2, shift, scale, gate, norm_g, w_in_t, w_out_b, conv_w, conv_b, w_pool, ls_pool,
                mh_norm_g, b_gates, final_g, send_dw_out=None, send_dw_in=None):
    bg_pad = jnp.pad(b_gates, ((0, 0), (0, 128 - b_gates.shape[1])))
    conv_w8 = jnp.pad(conv_w, ((0, 8 - CONV_WIDTH), (0, 0)))
    fg = final_g.reshape(1, D_MODEL)

    proj, gates, h_b = _fwd_proj(x2, norm_g, scale, shift, w_in_t)
    mix, cst, nst, mst, conv_a, pooled = _mix_fwd(proj, gates, bg_pad, conv_w8, conv_b, w_pool, ls_pool, mh_norm_g)
    if callable(w_out_b):
        w_out_b = w_out_b(mix)
    dx2, dmix, dwo, dgate, dfg, loss = _out_fwd_bwd(mix, x2, tgt2, w_out_b, gate, fg)
    if send_dw_out is not None:
        bg_pad = bg_pad + send_dw_out(dwo)
    dproj, dcw8, dcb, dwp, dls, dmhg, dbg = _mix_bwd(proj, gates, dmix, conv_a, pooled, cst, nst, mst, bg_pad,
                                                      conv_w8, w_pool, ls_pool, mh_norm_g)
    dw_in_t = _dw_in(h_b, dproj)[:N_IN]
    ng_in = norm_g
    if send_dw_in is not None:
        ng_in = norm_g + send_dw_in(dw_in_t, dcw8[:CONV_WIDTH])
    gx, dsh, dsc, dng = _bwd_in(dproj, w_in_t, x2, dx2, ng_in, scale)
    return dict(loss=loss, grad_x=gx, dw_in_t=dw_in_t, dw_out=dwo, dconv_w=dcw8[:CONV_WIDTH], conv_b=dcb,
                w_pool=dwp, ls_pool=dls, mh_norm_g=dmhg, b_gates=dbg, final_g=dfg, norm_g=dng,
                dmod=jnp.concatenate([dsh, dsc, dgate], axis=1))


def kernel(x, c, norm_g, w_ada, b_ada, w_in, b_gates, conv_w, conv_b, w_pool, ls_pool, mh_norm_g, w_out, final_g, loss_target, m_norm_g, m_w_ada, m_b_ada, m_w_in, m_b_gates, m_conv_w, m_conv_b, m_w_pool, m_ls_pool, m_mh_norm_g, m_w_out, m_final_g, v_norm_g, v_w_ada, v_b_ada, v_w_in, v_b_gates, v_conv_w, v_conv_b, v_w_pool, v_ls_pool, v_mh_norm_g, v_w_out, v_final_g):
    seq = x.shape[1]
    me = 4 * lax.axis_index("x") + 2 * lax.axis_index("y") + lax.axis_index("c")

    b_ada_blk = lax.dynamic_slice(b_ada, (0, me * ADA_SHARD), (1, ADA_SHARD))
    g_in, g_cw, mod_all, sc_all16 = _gather_weights_and_mod(
        (w_in[0].astype(BF16).T, conv_w[0]), c, w_ada[0], b_ada_blk)
    w_in_t = jnp.pad(g_in.reshape(N_IN, D_MODEL), ((0, N_PAD - N_IN), (0, 0)))
    w_out_flight, w_out_token = _scatter_start("send_w_out", (w_out[0].astype(BF16),), whole=True, after=g_in)

    def w_out_b(after):
        (g_out,) = _scatter_wait("recv_w_out", w_out_flight, after, whole=True)
        return g_out.reshape(D_MODEL, D_MODEL)

    conv_w_full = jnp.transpose(g_cw, (1, 0, 2)).reshape(CONV_WIDTH, 2 * D_MLSTM)
    mod = lax.dynamic_index_in_dim(mod_all, me, axis=1, keepdims=False).reshape(1, 3 * D_MODEL)
    shift, scale, gate = mod[:, :D_MODEL], mod[:, D_MODEL:2 * D_MODEL], mod[:, 2 * D_MODEL:]
    shift = shift + w_out_token

    flights = {}

    def send_dw_out(dwo):
        blocks = dwo.reshape(N_DEV, D_MODEL // N_DEV, D_MODEL)
        flights["out"], token = _scatter_start("send_dw_out", (blocks,))
        return token

    def send_dw_in(dw_in_t, dcw):
        blocks = dw_in_t.reshape(N_DEV, N_SHARD, D_MODEL)
        dcw_blocks = jnp.transpose(dcw.reshape(CONV_WIDTH, N_DEV, 128), (1, 0, 2))
        flights["in"], token = _scatter_start("send_dw_in", (blocks, dcw_blocks))
        return token

    r = _local_step(x[0], loss_target[0], shift, scale, gate, norm_g, w_in_t, w_out_b, conv_w_full, conv_b,
                    w_pool[0], ls_pool, mh_norm_g, b_gates, final_g, send_dw_out, send_dw_in)

    small_names = ("norm_g", "b_ada", "b_gates", "conv_b", "w_pool", "ls_pool", "mh_norm_g", "final_g")
    small_grads = dict(norm_g=r["norm_g"], b_ada=r["dmod"], b_gates=r["b_gates"], conv_b=r["conv_b"],
                       w_pool=r["w_pool"], ls_pool=r["ls_pool"], mh_norm_g=r["mh_norm_g"], final_g=r["final_g"])
    flights["small"], _ = _scatter_start(
        "send_small", (r["loss"],) + tuple(small_grads[k] for k in small_names), whole=True)
    p_in, p_cw = _scatter_wait("recv_dw_in", flights["in"], flights["small"][2 + 1 + small_names.index("w_pool")])
    (p_out,) = _scatter_wait("recv_dw_out", flights["out"], p_cw)

    in_t = _adam_sum("adam_w_in", p_in, w_in[0].T, m_w_in[0].T, v_w_in[0].T, N_SHARD, 256)
    gi, di, mi, vi = (o.T for o in in_t)
    go, do_, mo, vo = _adam_sum("adam_w_out", p_out, w_out[0], m_w_out[0], v_w_out[0], 128)
    gc, dc, mc, vc = _adam_sum("adam_conv_w", p_cw, conv_w[0], m_conv_w[0], v_conv_w[0], CONV_WIDTH)
    gathered = _scatter_wait("recv_small", flights["small"], go, whole=True)
    p_loss, p_small = gathered[0], dict(zip(small_names, gathered[1:]))

    def plain(norm_g_, b_ada_, b_gates_, conv_b_, w_pool_, ls_pool_, mh_norm_g_, final_g_):
        return dict(norm_g=norm_g_, b_ada=b_ada_, b_gates=b_gates_, conv_b=conv_b_, w_pool=w_pool_[0],
                    ls_pool=ls_pool_, mh_norm_g=mh_norm_g_, final_g=final_g_.reshape(1, D_MODEL))

    small, loss_row = _adam_small(
        p_small, p_loss,
        plain(norm_g, b_ada, b_gates, conv_b, w_pool, ls_pool, mh_norm_g, final_g),
        plain(m_norm_g, m_b_ada, m_b_gates, m_conv_b, m_w_pool, m_ls_pool, m_mh_norm_g, m_final_g),
        plain(v_norm_g, v_b_ada, v_b_gates, v_conv_b, v_w_pool, v_ls_pool, v_mh_norm_g, v_final_g))

    dmod_all = p_small["b_ada"].reshape(N_DEV, 3 * D_MODEL)
    dmod_blk16 = jnp.pad(lax.dynamic_slice(dmod_all, (0, me * ADA_SHARD), (N_DEV, ADA_SHARD)), ((0, 8), (0, 0)))
    ga, da, ma, va = _adam_ada(sc_all16, dmod_blk16, w_ada[0], m_w_ada[0], v_w_ada[0])

    names = ("norm_g", "w_ada", "b_ada", "w_in", "b_gates", "conv_w", "conv_b", "w_pool", "ls_pool", "mh_norm_g",
             "w_out", "final_g")
    shapes = dict(norm_g=norm_g.shape, b_ada=b_ada.shape, b_gates=b_gates.shape, conv_b=conv_b.shape,
                  w_pool=w_pool.shape, ls_pool=ls_pool.shape, mh_norm_g=mh_norm_g.shape, final_g=final_g.shape)
    sharded = dict(w_ada=(ga, da, ma, va), w_in=(gi, di, mi, vi), conv_w=(gc, dc, mc, vc), w_out=(go, do_, mo, vo))
    outs = []
    for kind in range(4):
        for nm in names:
            if nm in sharded:
                outs.append(sharded[nm][kind][None])
            else:
                outs.append(small[nm][kind].reshape(shapes[nm]))
    loss = loss_row[0, 0]
    grad_x = r["grad_x"].reshape(1, seq, D_MODEL)
    return (loss, grad_x, *outs)
```

```python
import jax
import jax.numpy as jnp
from jax import lax
from jax.experimental import pallas as pl
from jax.experimental.pallas import tpu as pltpu

F32 = jnp.float32
BF16 = jnp.bfloat16

D_MODEL = 1024
D_POOL = 512
D_MLSTM = 512
N_HEADS = 4
HEAD_DIM = 128
CHUNK = 128
POOL_WINDOWS = (2, 4, 8, 16)
POOL_GROUP_DIM = 128
CONV_WIDTH = 4
EPS = 1e-6
N_MAIN = 3584
N_IN = 3592
N_PAD = 3840
N_SHARD = N_IN // 8
ADA_SHARD = 3 * D_MODEL // 8
N_DEV = 8
CONV_HALO = 8
POOL_HALO = 16
NEG_BIG = -1e30
VMEM_LIMIT_BYTES = 56 * 1024 * 1024

ADAM_LR = 0.001
ADAM_B1 = 0.9
ADAM_B2 = 0.999
ADAM_EPS = 1e-08
ADAM_WD = 0.01
ADAM_STEP = 10

def _dot(a, b):
    return jnp.dot(a, b, preferred_element_type=F32)


def _dot_nt(a, b):
    return lax.dot_general(a, b, (((1,), (1,)), ((), ())), preferred_element_type=F32)


def _dot_tn(a, b):
    return lax.dot_general(a, b, (((0,), (0,)), ((), ())), preferred_element_type=F32)


def _dot_f32(a, b):
    return jnp.dot(a, b, precision=lax.Precision.HIGHEST, preferred_element_type=F32)


def _row_mean_mxu(x):
    return _dot(x.astype(BF16), jnp.full((HEAD_DIM, HEAD_DIM), 1.0 / HEAD_DIM, BF16))


def _sigmoid(x):
    return jax.nn.sigmoid(x)


def _log_sigmoid(x):
    return jnp.minimum(x, 0.0) - jnp.log1p(jnp.exp(-jnp.abs(x)))


def _params(sem):
    return pltpu.CompilerParams(dimension_semantics=sem, vmem_limit_bytes=VMEM_LIMIT_BYTES)


def _full(shape):
    n = len(shape)
    return pl.BlockSpec(shape, lambda *_: (0,) * n)


def _mesh_pos():
    return lax.axis_index("x"), lax.axis_index("y"), lax.axis_index("c")


def _peer(k):
    x, y, c = _mesh_pos()
    px = 1 - x if (k >> 2) & 1 else x
    py = 1 - y if (k >> 1) & 1 else y
    pc = 1 - c if k & 1 else c
    return (px, py, pc), 4 * px + 2 * py + pc


def _remote(src, dst, send_sem, recv_sem, to):
    return pltpu.make_async_remote_copy(src_ref=src, dst_ref=dst, send_sem=send_sem, recv_sem=recv_sem, device_id=to,
                                        device_id_type=pl.DeviceIdType.MESH)


def _two_level_gather(src, dst, send_sems, recv_sems, local_sems):
    n = len(src)
    x, y, c = _mesh_pos()
    me = 4 * x + 2 * y + c
    sibling = (x, y, 1 - c)
    south = c == 0
    near = (jnp.where(south, 1 - x, x), jnp.where(south, y, 1 - y))
    far = (jnp.where(south, x, 1 - x), jnp.where(south, 1 - y, y))
    diag = (1 - x, 1 - y)

    def block_of(chip, core):
        return 4 * chip[0] + 2 * chip[1] + core

    def copy(a, k, block, to, own=False):
        return _remote(src[a] if own else dst[a].at[block], dst[a].at[block], send_sems.at[a, k], recv_sems.at[a, k], to)

    local = [pltpu.make_async_copy(src[a], dst[a].at[me], local_sems.at[a]) for a in range(n)]
    sent = [copy(a, 0, me, sibling, True) for a in range(n)]
    sent += [copy(a, 1, me, (*near, c), True) for a in range(n)]
    sent += [copy(a, 2, me, (*far, c), True) for a in range(n)]
    for cp in local + sent:
        cp.start()
    yield
    for a in range(n):
        copy(a, 1, block_of(near, c), sibling).wait_recv()
        sent += [copy(a, 3, block_of(near, c), (*far, c)), copy(a, 4, block_of(near, c), sibling)]
        sent[-2].start()
        sent[-1].start()
    for k, chip in ((2, far), (3, diag)):
        for a in range(n):
            copy(a, k, block_of(chip, c), sibling).wait_recv()
            sent.append(copy(a, 3 + k, block_of(chip, c), sibling))
            sent[-1].start()
    for k, chip in ((0, (x, y)), (4, far), (5, near), (6, diag)):
        for a in range(n):
            copy(a, k, block_of(chip, 1 - c), sibling).wait_recv()
    for cp in sent:
        cp.wait_send()
    for cp in local:
        cp.wait()


GATHER_COPIES = 7


def _swap_with_all(buf, send_sems, recv_sems):
    x, y, c = _mesh_pos()
    me = 4 * x + 2 * y + c
    copies = [_remote(buf.at[me], buf.at[me], send_sems.at[k - 1], recv_sems.at[k - 1], _peer(k)[0])
              for k in range(1, N_DEV)]
    for cp in copies:
        cp.start()
    for cp in copies:
        cp.wait()


def _gather_weights_and_mod(shards, c_row, w_ada_blk, b_ada_blk):
    n = len(shards)

    def body(*refs):
        src, (c_ref, w_ref, b_ref) = refs[:n], refs[n:n + 3]
        dst, (mod_ref, sc_ref) = refs[n + 3:2 * n + 3], refs[2 * n + 3:2 * n + 5]
        c_all, g_send, g_recv, g_local, c_send, c_recv, m_send, m_recv = refs[2 * n + 5:]
        x, y, c = _mesh_pos()
        me = 4 * x + 2 * y + c
        gather = _two_level_gather(src, dst, g_send, g_recv, g_local)
        next(gather)
        c_all[me] = c_ref[...]
        _swap_with_all(c_all, c_send, c_recv)
        cv = jnp.concatenate([c_all[j] for j in range(N_DEV)] + [jnp.zeros((N_DEV, D_MODEL), F32)], axis=0)
        sc = cv * _sigmoid(cv)
        sc_ref[...] = sc
        blk = _dot(sc.astype(BF16), w_ref[...].astype(BF16)) + b_ref[...]
        mod_ref[me] = blk[0:N_DEV, :]
        _swap_with_all(mod_ref, m_send, m_recv)
        for _ in gather:
            pass

    hbm = pl.BlockSpec(memory_space=pltpu.HBM)
    vmem = pl.BlockSpec(memory_space=pltpu.VMEM)
    peers = pltpu.SemaphoreType.DMA((N_DEV - 1,))
    return pl.pallas_call(
        body, name="gather_weights",
        out_shape=tuple(jax.ShapeDtypeStruct((N_DEV,) + s.shape, s.dtype) for s in shards)
        + (jax.ShapeDtypeStruct((N_DEV, N_DEV, ADA_SHARD), F32), jax.ShapeDtypeStruct((2 * N_DEV, D_MODEL), F32)),
        in_specs=[hbm] * n + [vmem] * 3, out_specs=tuple([hbm] * n + [vmem] * 2),
        scratch_shapes=[pltpu.VMEM((N_DEV, 1, D_MODEL), F32),
                        pltpu.SemaphoreType.DMA((n, GATHER_COPIES)), pltpu.SemaphoreType.DMA((n, GATHER_COPIES)),
                        pltpu.SemaphoreType.DMA((n,)), peers, peers, peers, peers],
    )(*shards, c_row, w_ada_blk, b_ada_blk)


def _scatter_copies(src, land, send_sems, recv_sems, whole=False):
    x, y, c = _mesh_pos()
    me = 4 * x + 2 * y + c
    copies = []
    for k in range(1, N_DEV):
        peer, p = _peer(k)
        for a in range(len(src)):
            i = a * (N_DEV - 1) + k - 1
            copies.append(_remote(src[a] if whole else src[a].at[p], land[a].at[me], send_sems.at[i],
                                  recv_sems.at[i], peer))
    return copies


def _scatter_start(name, blocks, whole=False):
    n = len(blocks)

    def body(*refs):
        src, land = refs[:n], refs[n:2 * n]
        send_sems, recv_sems = refs[2 * n], refs[2 * n + 1]
        token_ref = refs[-1]
        for cp in _scatter_copies(src, land, send_sems, recv_sems, whole):
            cp.start()
        token_ref[...] = jnp.zeros_like(token_ref)

    hbm = pl.BlockSpec(memory_space=pltpu.HBM)
    sem = pl.BlockSpec(memory_space=pltpu.SEMAPHORE)
    landing = [((N_DEV,) + b.shape if whole else b.shape, b.dtype) for b in blocks]
    through = tuple(pltpu.HBM(b.shape, b.dtype) for b in blocks) + tuple(pltpu.HBM(s, d) for s, d in landing)
    args = [pltpu.with_memory_space_constraint(b, pltpu.HBM) for b in blocks]
    args += [pltpu.with_memory_space_constraint(lax.empty(s, d), pltpu.HBM) for s, d in landing]
    out = pl.pallas_call(
        body, name=name,
        out_shape=(pltpu.SemaphoreType.DMA((n * (N_DEV - 1),)),) * 2 + through
        + (jax.ShapeDtypeStruct((8, 128), F32),),
        in_specs=[hbm] * (2 * n),
        out_specs=(sem, sem) + (hbm,) * (2 * n) + (pl.BlockSpec(memory_space=pltpu.VMEM),),
        input_output_aliases={i: 2 + i for i in range(2 * n)},
        compiler_params=pltpu.CompilerParams(has_side_effects=pltpu.SideEffectType.DATAFLOW_SIDE_EFFECTING),
    )(*args)
    return out[:-1], out[-1][0:1, 0:1]


def _scatter_wait(name, state, after, whole=False):
    n = (len(state) - 2) // 2
    send_sems, recv_sems = state[0], state[1]
    src, land = state[2:2 + n], state[2 + n:]

    def body(*refs):
        src_r, land_r = refs[:n], refs[n:2 * n]
        for cp in _scatter_copies(src_r, land_r, refs[2 * n], refs[2 * n + 1], whole):
            cp.wait_send()
            cp.wait_recv()

    hbm = pl.BlockSpec(memory_space=pltpu.HBM)
    sem = pl.BlockSpec(memory_space=pltpu.SEMAPHORE)
    out = pl.pallas_call(
        body, name=name,
        out_shape=tuple(pltpu.HBM(b.shape, b.dtype) for b in src + land),
        in_specs=[hbm] * (2 * n) + [sem, sem, pl.BlockSpec(memory_space=pl.ANY)],
        out_specs=(hbm,) * (2 * n),
        input_output_aliases={i: i for i in range(2 * n)},
        compiler_params=pltpu.CompilerParams(has_side_effects=pltpu.SideEffectType.DATAFLOW_SIDE_EFFECTING),
    )(*src, *land, send_sems, recv_sems, after)
    me = 4 * lax.axis_index("x") + 2 * lax.axis_index("y") + lax.axis_index("c")
    landed = []
    for a in range(n):
        own = out[a][None] if whole else lax.dynamic_index_in_dim(out[a], me, axis=0, keepdims=True)
        landed.append(lax.dynamic_update_slice_in_dim(out[n + a], own, me, axis=0))
    return landed


def _fwd_proj(x, norm_g, scale, shift, w_in_t):
    seq = x.shape[0]
    tm = min(512, seq)
    sub = min(256, tm)
    tn = 512

    def body(x_ref, ng_ref, sc_ref, sh_ref, wt_ref, proj_ref, gates_ref, h_ref):
        def chain(n):
            for _ in range(n):
                yield
            rows = slice(n * sub, (n + 1) * sub)
            xt = x_ref[rows, :]
            r = lax.rsqrt(jnp.mean(xt * xt, axis=-1, keepdims=True) + EPS)
            h = ((xt * r) * ng_ref[...]) * (1.0 + sc_ref[...]) + sh_ref[...]
            hb = h.astype(BF16)
            h_ref[rows, :] = hb
            yield
            gates_ref[rows, :] = _dot_nt(hb, wt_ref[N_MAIN:N_MAIN + 128, :])
            for j in range(N_MAIN // tn):
                proj_ref[rows, j * tn:(j + 1) * tn] = _dot_nt(hb, wt_ref[j * tn:(j + 1) * tn, :])

        _in_lockstep(chain(n) for n in range(tm // sub))

    vec = _full((1, D_MODEL))
    tile = pl.BlockSpec((tm, D_MODEL), lambda i: (i, 0))
    return pl.pallas_call(
        body, name="fwd_proj", grid=(seq // tm,),
        out_shape=(jax.ShapeDtypeStruct((seq, N_MAIN), F32), jax.ShapeDtypeStruct((seq, 128), F32),
                   jax.ShapeDtypeStruct((seq, D_MODEL), BF16)),
        in_specs=[tile, vec, vec, vec, _full((N_PAD, D_MODEL))],
        out_specs=(pl.BlockSpec((tm, N_MAIN), lambda i: (i, 0)), pl.BlockSpec((tm, 128), lambda i: (i, 0)), tile),
        compiler_params=_params(("arbitrary",)),
    )(x, norm_g, scale, shift, w_in_t)


def _gate_forms(gpre):
    r = lax.broadcasted_iota(jnp.int32, (CHUNK, CHUNK), 0)
    c = lax.broadcasted_iota(jnp.int32, (CHUNK, CHUNK), 1)
    causal = c <= r
    ltri = jnp.where(causal, 1.0, 0.0).astype(F32)
    utri = jnp.where(r <= c, 1.0, 0.0).astype(F32)
    bcol = _dot_f32(ltri, _log_sigmoid(gpre))
    gt8 = gpre.T[0:8, :]
    brow = _dot_f32(_log_sigmoid(gt8), utri)
    return causal, utri, bcol, gt8, brow


def _in_lockstep(stages):
    alive = list(stages)
    while alive:
        still = []
        for g in alive:
            try:
                next(g)
                still.append(g)
            except StopIteration:
                pass
        alive = still


def _head_fwd(qh, kh, vh, bc, br, igr, m_prev, c_h, n_row, causal):
    qb, kb, vb, cb = qh.astype(BF16), kh.astype(BF16), vh.astype(BF16), c_h.astype(BF16)
    qk = _dot_nt(qb, kb)
    cq = _dot_nt(qb, cb)
    nq = _dot_nt(qb, jnp.broadcast_to(n_row.astype(BF16), (HEAD_DIM, HEAD_DIM)))
    yield
    dlog = jnp.where(causal, bc - br + igr, NEG_BIG)
    inter_log = bc + m_prev
    m_t = jnp.maximum(inter_log, jnp.max(dlog, axis=-1, keepdims=True))
    yield
    dmat = jnp.exp(dlog - m_t)
    inter = jnp.exp(inter_log - m_t)
    s = qk * dmat
    sv = _dot(s.astype(BF16), vb)
    yield
    den = jnp.sum(s, axis=-1, keepdims=True) + inter * nq
    emt = jnp.exp(-m_t)
    yield
    num = sv + inter * cq
    dn = jnp.maximum(jnp.abs(den), emt)
    hm = num / dn
    return dict(dmat=dmat, inter=inter, qb=qb, kb=kb, vb=vb, cb=cb, s=s, cq=cq, nq=nq, den=den, emt=emt,
                dn=dn, hm=hm)


def _state_weights(bc, igc, m_prev, m_new=None):
    last = lax.broadcasted_iota(jnp.int32, (CHUNK, 1), 0) == CHUNK - 1
    b_last = jnp.sum(jnp.where(last, bc, 0.0), axis=0, keepdims=True)
    wlog = b_last - bc + igc
    if m_new is None:
        m_new = jnp.maximum(b_last + m_prev, jnp.max(wlog, axis=0, keepdims=True))
    w_c = jnp.exp(wlog - m_new)
    decay = jnp.exp(b_last + m_prev - m_new)
    return w_c, decay, m_new, last


def _rows_back(x, k):
    return x if k == 0 else pltpu.roll(x, k, 0)


def _rows_ahead(x, k):
    return x if k == 0 else pltpu.roll(x, x.shape[0] - k, 0)


def _conv_taps(xpad):
    return [_rows_back(xpad, CONV_WIDTH - 1 - j)[CONV_HALO:, :] for j in range(CONV_WIDTH)]


def _conv_pre(taps, cw_ref, cb_ref):
    a = cb_ref[...]
    for j in range(CONV_WIDTH):
        a = a + cw_ref[j:j + 1, :] * taps[j]
    return a


def _window_sum(x, w, shift):
    k = 1
    while k < w:
        x = x + shift(x, k)
        k *= 2
    return x


def _pool_window_sum(upad_ref, g, w):
    lanes = slice(g * POOL_GROUP_DIM, (g + 1) * POOL_GROUP_DIM)
    return _window_sum(upad_ref[:, lanes], w, _rows_back)[POOL_HALO:, :]


def _pool_inv_count(row0, rows, w):
    pos = row0 + lax.broadcasted_iota(jnp.int32, (rows, 1), 0) + 1
    return 1.0 / jnp.minimum(pos, w).astype(F32)


FWD_CHUNKS = 4
BWD_CHUNKS = 4


def _mix_fwd(proj, gates, bg_pad, conv_w8, conv_b, w_pool, ls_pool, mh_g):
    seq = proj.shape[0]
    n_chunks = seq // CHUNK
    per_step = FWD_CHUNKS
    blk = per_step * CHUNK

    def body(uz_ref, qk_ref, v_ref, o_ref, zm_ref, uh_ref, qkh_ref, g_ref, bg_ref, cw_ref, cb_ref, wp_ref,
             ls_ref, mhg_ref, mix_ref, cst_ref, nst_ref, mst_ref, a_ref, pooled_ref, c_scr, n_scr, m_scr, xpad, upad):
        i = pl.program_id(0)

        @pl.when(i == 0)
        def _():
            c_scr[...] = jnp.zeros_like(c_scr)
            n_scr[...] = jnp.zeros_like(n_scr)
            m_scr[...] = jnp.zeros_like(m_scr)

        first = i == 0

        upad[0:POOL_HALO, :] = jnp.where(first, 0.0, uh_ref[...])
        upad[POOL_HALO:POOL_HALO + blk, :] = uz_ref[:, 0:D_POOL]
        for g, w in enumerate(POOL_WINDOWS):
            lanes = slice(g * POOL_GROUP_DIM, (g + 1) * POOL_GROUP_DIM)
            pooled = (_pool_window_sum(upad, g, w) * _pool_inv_count(i * blk, blk, w) - uz_ref[:, lanes]).astype(BF16)
            pooled_ref[:, lanes] = pooled
            y = _dot(pooled, wp_ref[g].astype(BF16)) * ls_ref[:, lanes]
            zp = uz_ref[:, D_POOL + g * POOL_GROUP_DIM:D_POOL + (g + 1) * POOL_GROUP_DIM]
            mix_ref[:, lanes] = (y * (zp * _sigmoid(zp))).astype(BF16)

        xpad[0:CONV_HALO, :] = jnp.where(first, 0.0, qkh_ref[...])
        xpad[CONV_HALO:CONV_HALO + blk, :] = qk_ref[...]
        a = _conv_pre(_conv_taps(xpad[...]), cw_ref, cb_ref)
        a_ref[...] = a
        qk = a * _sigmoid(a)

        def head(rows, h, qh, kh, vh, bc, br, igr, m_prev, c_h, n_row, causal):
            lanes = slice(h * HEAD_DIM, (h + 1) * HEAD_DIM)
            f = yield from _head_fwd(qh, kh, vh, bc, br, igr, m_prev, c_h, n_row, causal)
            yield
            hm = f["hm"]
            hn = hm * lax.rsqrt(_row_mean_mxu(hm * hm) + EPS) * mhg_ref[:, lanes]
            zm = zm_ref[rows, lanes]
            out = hn * _sigmoid(o_ref[rows, lanes]) * (zm * _sigmoid(zm))
            mix_ref[rows, D_POOL + h * HEAD_DIM:D_POOL + (h + 1) * HEAD_DIM] = out.astype(BF16)

        c_cur = [c_scr[h] for h in range(N_HEADS)]
        n_cur = [n_scr[h:h + 1, :] for h in range(N_HEADS)]
        m_cur = [m_scr[h:h + 1, 0:1] for h in range(N_HEADS)]
        chains = []
        for s in range(per_step):
            rows = slice(s * CHUNK, (s + 1) * CHUNK)
            gpre = g_ref[rows, :] + bg_ref[...]
            causal, _, bcol, gt8, brow = _gate_forms(gpre)
            nst_ref[s] = jnp.zeros((8, 128), F32)
            mst_ref[s] = jnp.zeros((8, 128), F32)
            for h in range(N_HEADS):
                lanes = slice(h * HEAD_DIM, (h + 1) * HEAD_DIM)
                cst_ref[s, h] = c_cur[h]
                nst_ref[s, h:h + 1, :] = n_cur[h]
                mst_ref[s, h:h + 1, :] = jnp.broadcast_to(m_cur[h], (1, 128))
                qh = qk[rows, lanes]
                kh = qk[rows, D_MLSTM + h * HEAD_DIM:D_MLSTM + (h + 1) * HEAD_DIM] * (HEAD_DIM ** -0.5)
                vh = v_ref[rows, lanes]
                bc = bcol[:, N_HEADS + h:N_HEADS + h + 1]
                br = brow[N_HEADS + h:N_HEADS + h + 1, :]
                igr = gt8[h:h + 1, :]
                igc = gpre[:, h:h + 1]
                chains.append(head(rows, h, qh, kh, vh, bc, br, igr, m_cur[h], c_cur[h], n_cur[h], causal))
                w_c, decay, m_new, _ = _state_weights(bc, igc, m_cur[h])
                c_cur[h] = decay * c_cur[h] + _dot_tn((vh * w_c).astype(BF16), kh.astype(BF16))
                n_cur[h] = decay * n_cur[h] + jnp.sum(w_c * kh, axis=0, keepdims=True)
                m_cur[h] = m_new
        for h in range(N_HEADS):
            c_scr[h] = c_cur[h]
            n_scr[h:h + 1, :] = n_cur[h]
            m_scr[h:h + 1, :] = jnp.broadcast_to(m_cur[h], (1, 128))
        _in_lockstep(chains)

    in_specs = [
        pl.BlockSpec((blk, 1024), lambda i: (i, 0)),
        pl.BlockSpec((blk, 1024), lambda i: (i, 1)),
        pl.BlockSpec((blk, 512), lambda i: (i, 4)),
        pl.BlockSpec((blk, 512), lambda i: (i, 5)),
        pl.BlockSpec((blk, 512), lambda i: (i, 6)),
        pl.BlockSpec((POOL_HALO, 512), lambda i: (jnp.maximum(i * (blk // POOL_HALO) - 1, 0), 0)),
        pl.BlockSpec((CONV_HALO, 1024), lambda i: (jnp.maximum(i * (blk // CONV_HALO) - 1, 0), 1)),
        pl.BlockSpec((blk, 128), lambda i: (i, 0)),
        _full((1, 128)), _full((8, 1024)), _full((1, 1024)), _full((4, 128, 128)), _full((1, 512)),
        _full((1, 512))]
    return pl.pallas_call(
        body, name="mix_fwd", grid=(n_chunks // per_step,),
        out_shape=(jax.ShapeDtypeStruct((seq, D_MODEL), BF16),
                   jax.ShapeDtypeStruct((n_chunks, N_HEADS, HEAD_DIM, HEAD_DIM), F32),
                   jax.ShapeDtypeStruct((n_chunks, 8, 128), F32),
                   jax.ShapeDtypeStruct((n_chunks, 8, 128), F32),
                   jax.ShapeDtypeStruct((seq, 2 * D_MLSTM), F32),
                   jax.ShapeDtypeStruct((seq, D_POOL), BF16)),
        in_specs=in_specs,
        out_specs=(pl.BlockSpec((blk, D_MODEL), lambda i: (i, 0)),
                   pl.BlockSpec((per_step, N_HEADS, HEAD_DIM, HEAD_DIM), lambda i: (i, 0, 0, 0)),
                   pl.BlockSpec((per_step, 8, 128), lambda i: (i, 0, 0)),
                   pl.BlockSpec((per_step, 8, 128), lambda i: (i, 0, 0)),
                   pl.BlockSpec((blk, 2 * D_MLSTM), lambda i: (i, 0)),
                   pl.BlockSpec((blk, D_POOL), lambda i: (i, 0))),
        scratch_shapes=[pltpu.VMEM((N_HEADS, HEAD_DIM, HEAD_DIM), F32), pltpu.VMEM((8, 128), F32),
                        pltpu.VMEM((8, 128), F32), pltpu.VMEM((CONV_HALO + blk, 1024), F32),
                        pltpu.VMEM((POOL_HALO + blk, D_POOL), F32)],
        compiler_params=_params(("arbitrary",)),
    )(proj, proj, proj, proj, proj, proj, proj, gates, bg_pad, conv_w8, conv_b, w_pool, ls_pool, mh_g)


def _out_fwd_bwd(mix, x, tgt, w_out_b, gate, final_g):
    seq = x.shape[0]
    tm = min(512, seq)
    sub = min(256, tm)

    def body(mix_ref, x_ref, t_ref, w_ref, gate_ref, fg_ref, dx2_ref, dmix_ref, dwo_ref, dgate_ref, dfg_ref,
             loss_ref, dwo_acc):
        @pl.when(pl.program_id(0) == 0)
        def _():
            dwo_acc[...] = jnp.zeros_like(dwo_acc)
            dgate_ref[...] = jnp.zeros_like(dgate_ref)
            dfg_ref[...] = jnp.zeros_like(dfg_ref)
            loss_ref[...] = jnp.zeros_like(loss_ref)

        w = w_ref[...]
        gate_v = gate_ref[...]
        fg = fg_ref[...]
        do2_parts = [None] * (tm // sub)

        def chain(n):
            rows = slice(n * sub, (n + 1) * sub)
            o2 = _dot(mix_ref[rows, :], w)
            yield
            x2 = x_ref[rows, :] + gate_v * o2
            r2 = lax.rsqrt(jnp.mean(x2 * x2, axis=-1, keepdims=True) + EPS)
            x2n = x2 * r2
            err = x2n * fg - t_ref[rows, :]
            part = 0.5 * jnp.sum(jnp.sum(err * err, axis=-1, keepdims=True), axis=0, keepdims=True) / D_MODEL
            loss_ref[...] += jnp.broadcast_to(part, loss_ref.shape)
            dy = err / D_MODEL
            dfg_ref[...] += jnp.sum(dy * x2n, axis=0, keepdims=True)
            gdy = dy * fg
            dx2 = r2 * (gdy - x2n * jnp.mean(gdy * x2n, axis=-1, keepdims=True))
            dx2_ref[rows, :] = dx2
            dgate_ref[...] += jnp.sum(dx2 * o2, axis=0, keepdims=True)
            do2 = (dx2 * gate_v).astype(BF16)
            dmix_ref[rows, :] = _dot_nt(do2, w)
            do2_parts[n] = do2

        _in_lockstep(chain(n) for n in range(tm // sub))
        dwo_acc[...] += _dot_tn(mix_ref[...], jnp.concatenate(do2_parts, axis=0))

        @pl.when(pl.program_id(0) == seq // tm - 1)
        def _():
            dwo_ref[...] = dwo_acc[...].astype(BF16)

    tile = pl.BlockSpec((tm, D_MODEL), lambda i: (i, 0))
    vec = _full((1, D_MODEL))
    return pl.pallas_call(
        body, name="out_fwd_bwd", grid=(seq // tm,),
        out_shape=(jax.ShapeDtypeStruct((seq, D_MODEL), F32), jax.ShapeDtypeStruct((seq, D_MODEL), F32),
                   jax.ShapeDtypeStruct((D_MODEL, D_MODEL), BF16), jax.ShapeDtypeStruct((1, D_MODEL), F32),
                   jax.ShapeDtypeStruct((1, D_MODEL), F32), jax.ShapeDtypeStruct((1, 128), F32)),
        in_specs=[tile, tile, tile, _full((D_MODEL, D_MODEL)), vec, vec],
        out_specs=(tile, tile, _full((D_MODEL, D_MODEL)), vec, vec, _full((1, 128))),
        scratch_shapes=[pltpu.VMEM((D_MODEL, D_MODEL), F32)],
        compiler_params=_params(("arbitrary",)),
    )(mix, x, tgt, w_out_b, gate, final_g)


def _mix_bwd(proj, gates, dmix, conv_a, pooled, cst, nst, mst, bg_pad, conv_w8, w_pool, ls_pool, mh_g):
    seq = proj.shape[0]
    n_chunks = seq // CHUNK
    per_step = BWD_CHUNKS
    blk = per_step * CHUNK
    n_blocks = n_chunks // per_step

    def body(zp_ref, qk_ref, v_ref, o_ref, zm_ref, g_ref, dmix_ref, a_ref, pooled_ref, cst_ref, nst_ref, mst_ref,
             mnx_ref, bg_ref, cw_ref, wp_ref, ls_ref, mhg_ref,
             dp_ref, dcw_ref, dcb_ref, dwp_ref, dls_ref, dmhg_ref, dbg_ref,
             dc_scr, dn_scr, dapad, dpipad):
        i = pl.program_id(0)
        bi = n_blocks - 1 - i

        @pl.when(i == 0)
        def _():
            for ref in (dc_scr, dn_scr, dcw_ref, dcb_ref, dwp_ref, dls_ref, dmhg_ref, dbg_ref):
                ref[...] = jnp.zeros_like(ref)
            dapad[blk:blk + CONV_HALO, :] = jnp.zeros((CONV_HALO, 1024), F32)
            dpipad[blk:blk + POOL_HALO, :] = jnp.zeros((POOL_HALO, D_POOL), F32)

        dpooled = []
        for g, w in enumerate(POOL_WINDOWS):
            lanes = slice(g * POOL_GROUP_DIM, (g + 1) * POOL_GROUP_DIM)
            zlanes = slice(D_POOL + g * POOL_GROUP_DIM, D_POOL + (g + 1) * POOL_GROUP_DIM)
            inv = _pool_inv_count(bi * blk, blk, w)
            pb = pooled_ref[:, lanes]
            wpb = wp_ref[g].astype(BF16)
            yw = _dot(pb, wpb)
            ls = ls_ref[:, lanes]
            zp = zp_ref[:, lanes]
            sg = _sigmoid(zp)
            dpo = dmix_ref[:, lanes]
            dp_ref[:, zlanes] = (dpo * (yw * ls) * (sg * (1.0 + zp * (1.0 - sg)))).astype(BF16)
            dy = dpo * (zp * sg)
            dls_ref[:, lanes] += jnp.sum(dy * yw, axis=0, keepdims=True)
            dyw = (dy * ls).astype(BF16)
            dwp_ref[g] += _dot_tn(pb, dyw)
            dpl = _dot_nt(dyw, wpb)
            dpooled.append(dpl)
            dpipad[0:blk, lanes] = dpl * inv
        for g, w in enumerate(POOL_WINDOWS):
            lanes = slice(g * POOL_GROUP_DIM, (g + 1) * POOL_GROUP_DIM)
            du = _window_sum(dpipad[:, lanes], w, _rows_ahead)[0:blk, :] - dpooled[g]
            dp_ref[:, lanes] = du.astype(BF16)
        dpipad[blk:blk + POOL_HALO, :] = dpipad[0:POOL_HALO, :]

        def silu_and_slope(rows, cols):
            a = a_ref[rows, cols]
            sg = _sigmoid(a)
            return a * sg, sg * (1.0 + a * (1.0 - sg))

        lane = lax.broadcasted_iota(jnp.int32, (CHUNK, 128), 1)
        row = lax.broadcasted_iota(jnp.int32, (CHUNK, 128), 0)
        scale_k = HEAD_DIM ** -0.5
        forms = [None] * per_step
        col_g_rows = [[] for _ in range(per_step)]
        dig_parts = [[] for _ in range(per_step)]
        db_parts = [[] for _ in range(per_step)]
        d_state = [[None] * N_HEADS for _ in range(per_step)]

        def state_terms(s, h, c_h, n_row, vb, kb):
            dcn, dnn = d_state[s][h]
            dcnb = dcn.astype(BF16)
            amat = _dot(vb, dcnb) + dnn
            kdc = _dot_nt(kb, dcnb)
            ddecay = (jnp.sum(jnp.sum(dcn * c_h, axis=-1, keepdims=True), axis=0, keepdims=True)
                      + jnp.sum(dnn * n_row, axis=-1, keepdims=True))
            return dcn, dnn, amat, kdc, ddecay

        def head(s, h):
            rows = slice(s * CHUNK, (s + 1) * CHUNK)
            lanes = slice(h * HEAD_DIM, (h + 1) * HEAD_DIM)
            klanes = slice(D_MLSTM + h * HEAD_DIM, D_MLSTM + (h + 1) * HEAD_DIM)
            gpre, causal, utri, bcol, gt8, brow = forms[s]
            qh, dsilu_q = silu_and_slope(rows, lanes)
            kh, dsilu_k = silu_and_slope(rows, klanes)
            kh = kh * scale_k
            vh = v_ref[rows, lanes]
            bc = bcol[:, N_HEADS + h:N_HEADS + h + 1]
            br = brow[N_HEADS + h:N_HEADS + h + 1, :]
            igr = gt8[h:h + 1, :]
            igc = gpre[:, h:h + 1]
            m_prev = mst_ref[s, h:h + 1, 0:1]
            m_next = mnx_ref[0, h:h + 1, 0:1] if s == per_step - 1 else mst_ref[s + 1, h:h + 1, 0:1]
            c_h = cst_ref[s, h]
            n_row = nst_ref[s, h:h + 1, :]
            w_c, decay, _, last = _state_weights(bc, igc, m_prev, m_next)
            terms = None
            if s == per_step - 1:
                terms = state_terms(s, h, c_h, n_row, vh.astype(BF16), kh.astype(BF16))
            f = yield from _head_fwd(qh, kh, vh, bc, br, igr, m_prev, c_h, n_row, causal)
            qb, kb, vb, cb = f["qb"], f["kb"], f["vb"], f["cb"]
            sm, dmat, inter, den, dn, hm = f["s"], f["dmat"], f["inter"], f["den"], f["dn"], f["hm"]
            yield

            rinv = lax.rsqrt(_row_mean_mxu(hm * hm) + EPS)
            hmn = hm * rinv
            gh = mhg_ref[:, lanes]
            o_pre = o_ref[rows, lanes]
            og = _sigmoid(o_pre)
            zm = zm_ref[rows, lanes]
            sgz = _sigmoid(zm)
            sz = zm * sgz
            dout = dmix_ref[rows, D_POOL + h * HEAD_DIM:D_POOL + (h + 1) * HEAD_DIM]
            hn = hmn * gh
            dp_ref[rows, 2560 + h * HEAD_DIM:2560 + (h + 1) * HEAD_DIM] = (
                dout * hn * sz * og * (1.0 - og)).astype(BF16)
            dp_ref[rows, 3072 + h * HEAD_DIM:3072 + (h + 1) * HEAD_DIM] = (
                dout * hn * og * (sgz * (1.0 + zm * (1.0 - sgz)))).astype(BF16)
            dhn = dout * og * sz
            dmhg_ref[:, lanes] += jnp.sum(dhn * hmn, axis=0, keepdims=True)
            dyn = dhn * gh
            dhm = rinv * (dyn - hmn * _row_mean_mxu(dyn * hmn))
            yield

            inv_dn = 1.0 / dn
            dnum = dhm * inv_dn
            hd = jnp.sum(dhm * hm, axis=-1, keepdims=True)
            dden = jnp.where(jnp.abs(den) > f["emt"], -hd / den, 0.0)
            dnb = dnum.astype(BF16)
            dnv = _dot_nt(dnb, vb)
            dv = _dot_tn(sm.astype(BF16), dnb)
            dnc = _dot(dnb, cb)
            dc_prev = _dot_tn((inter * dnum).astype(BF16), qb)
            dn_prev = jnp.sum((inter * dden) * qh, axis=0, keepdims=True)
            yield
            ds = dnv + dden
            dqk = (ds * dmat).astype(BF16)
            dqk_k = _dot(dqk, kb)
            dk = _dot_tn(dqk, qb)
            for _ in range(per_step - 1 - s):
                yield
            if terms is None:
                terms = state_terms(s, h, c_h, n_row, vb, kb)
            dcn, dnn, amat, kdc, ddecay = terms
            d_start = (decay * dcn + dc_prev, decay * dnn + dn_prev)
            if s > 0:
                d_state[s - 1][h] = d_start
            else:
                dc_scr[h] = d_start[0]
                dn_scr[h:h + 1, :] = d_start[1]
            yield
            gmat = ds * sm
            row_g = jnp.sum(gmat, axis=-1, keepdims=True)
            col_g_rows[s].append(jnp.where(row == h, jnp.sum(gmat, axis=0, keepdims=True), 0.0))
            gcol = inter * (jnp.sum(dnum * f["cq"], axis=-1, keepdims=True) + dden * f["nq"])
            dw = jnp.sum(amat * kh, axis=-1, keepdims=True)
            e = dw * w_c
            db_last = ddecay * decay + jnp.sum(e, axis=0, keepdims=True)
            dig_parts[s].append(jnp.where(lane == h, e, 0.0))
            db_parts[s].append(
                jnp.where(lane == N_HEADS + h, row_g + gcol - e + jnp.where(last, db_last, 0.0), 0.0))
            yield
            dq = dqk_k + inter * (dnc + dden * n_row)
            dp_ref[rows, 2048 + h * HEAD_DIM:2048 + (h + 1) * HEAD_DIM] = (dv + w_c * kdc).astype(BF16)
            dapad[rows, lanes] = dq * dsilu_q
            dapad[rows, klanes] = (dk + w_c * amat) * scale_k * dsilu_k

        chains = []
        for s in reversed(range(per_step)):
            gpre = g_ref[s * CHUNK:(s + 1) * CHUNK, :] + bg_ref[...]
            forms[s] = (gpre,) + _gate_forms(gpre)
            for h in range(N_HEADS):
                if s == per_step - 1:
                    d_state[s][h] = (dc_scr[h], dn_scr[h:h + 1, :])
                chains.append(head(s, h))
        _in_lockstep(chains)

        for s in range(per_step):
            rows = slice(s * CHUNK, (s + 1) * CHUNK)
            gpre, utri = forms[s][0], forms[s][2]
            cs_t = sum(col_g_rows[s][1:], col_g_rows[s][0]).T
            dig_all = sum(dig_parts[s][1:], dig_parts[s][0]) + cs_t
            db_cols = sum(db_parts[s][1:], db_parts[s][0])
            shifted = jnp.zeros((CHUNK, 128), F32)
            for h in range(N_HEADS):
                shifted = shifted + jnp.where(lane == N_HEADS + h, cs_t[:, h:h + 1], 0.0)
            dlf = _dot_f32(utri, db_cols - shifted)
            dgates = dig_all + dlf * _sigmoid(-gpre)
            dp_ref[rows, N_MAIN:N_MAIN + 128] = dgates.astype(BF16)
            dbg_ref[...] += jnp.sum(dgates, axis=0, keepdims=True)
        dp_ref[:, N_MAIN + 128:N_PAD] = jnp.zeros((blk, N_PAD - N_MAIN - 128), BF16)

        da_pad = dapad[...]
        da = da_pad[0:blk, :]
        dcb_ref[...] += jnp.sum(da, axis=0, keepdims=True)
        x = qk_ref[...]
        dx = jnp.zeros((blk, 1024), F32)
        for j in range(CONV_WIDTH):
            da_j = _rows_ahead(da_pad, CONV_WIDTH - 1 - j)[0:blk, :]
            dcw_ref[j:j + 1, :] += jnp.sum(da_j * x, axis=0, keepdims=True)
            dx = dx + cw_ref[j:j + 1, :] * da_j
        dp_ref[:, 1024:2048] = dx.astype(BF16)
        dapad[blk:blk + CONV_HALO, :] = dapad[0:CONV_HALO, :]

    bmap = lambda i: n_blocks - 1 - i
    wide = pl.BlockSpec((blk, 1024), lambda i: (bmap(i), 0))
    state = pl.BlockSpec((per_step, 8, 128), lambda i: (bmap(i), 0, 0))
    in_specs = [
        pl.BlockSpec((blk, 512), lambda i: (bmap(i), 1)),
        pl.BlockSpec((blk, 1024), lambda i: (bmap(i), 1)),
        pl.BlockSpec((blk, 512), lambda i: (bmap(i), 4)),
        pl.BlockSpec((blk, 512), lambda i: (bmap(i), 5)),
        pl.BlockSpec((blk, 512), lambda i: (bmap(i), 6)),
        pl.BlockSpec((blk, 128), lambda i: (bmap(i), 0)),
        wide, wide,
        pl.BlockSpec((blk, D_POOL), lambda i: (bmap(i), 0)),
        pl.BlockSpec((per_step, N_HEADS, HEAD_DIM, HEAD_DIM), lambda i: (bmap(i), 0, 0, 0)),
        state, state,
        pl.BlockSpec((1, 8, 128), lambda i: (jnp.minimum((bmap(i) + 1) * per_step, n_chunks - 1), 0, 0)),
        _full((1, 128)), _full((8, 1024)), _full((4, 128, 128)), _full((1, 512)), _full((1, 512))]
    return pl.pallas_call(
        body, name="mix_bwd", grid=(n_blocks,),
        out_shape=(jax.ShapeDtypeStruct((seq, N_PAD), BF16), jax.ShapeDtypeStruct((8, 1024), F32),
                   jax.ShapeDtypeStruct((1, 1024), F32), jax.ShapeDtypeStruct((4, 128, 128), F32),
                   jax.ShapeDtypeStruct((1, 512), F32), jax.ShapeDtypeStruct((1, 512), F32),
                   jax.ShapeDtypeStruct((1, 128), F32)),
        in_specs=in_specs,
        out_specs=(pl.BlockSpec((blk, N_PAD), lambda i: (bmap(i), 0)), _full((8, 1024)), _full((1, 1024)),
                   _full((4, 128, 128)), _full((1, 512)), _full((1, 512)), _full((1, 128))),
        scratch_shapes=[pltpu.VMEM((N_HEADS, HEAD_DIM, HEAD_DIM), F32), pltpu.VMEM((8, 128), F32),
                        pltpu.VMEM((blk + CONV_HALO, 1024), F32), pltpu.VMEM((blk + POOL_HALO, D_POOL), F32)],
        compiler_params=_params(("arbitrary",)),
    )(proj, proj, proj, proj, proj, gates, dmix, conv_a, pooled, cst, nst, mst, mst, bg_pad, conv_w8,
      w_pool, ls_pool, mh_g)


def _bwd_in(dproj, w_in_t, x, dx2, norm_g, scale):
    seq = x.shape[0]
    tm = min(512, seq)
    sub = min(256, tm)

    def body(dp_ref, wt_ref, x_ref, dx2_ref, ng_ref, sc_ref, gx_ref, dsh_ref, dsc_ref, dng_ref):
        @pl.when(pl.program_id(0) == 0)
        def _():
            dsh_ref[...] = jnp.zeros_like(dsh_ref)
            dsc_ref[...] = jnp.zeros_like(dsc_ref)
            dng_ref[...] = jnp.zeros_like(dng_ref)

        ng = ng_ref[...]
        one_sc = 1.0 + sc_ref[...]

        def chain(n):
            rows = slice(n * sub, (n + 1) * sub)
            dh = _dot(dp_ref[rows, :], wt_ref[...])
            yield
            xt = x_ref[rows, :]
            r = lax.rsqrt(jnp.mean(xt * xt, axis=-1, keepdims=True) + EPS)
            xn = xt * r
            dsh_ref[...] += jnp.sum(dh, axis=0, keepdims=True)
            dhxn_sum = jnp.sum(dh * xn, axis=0, keepdims=True)
            dsc_ref[...] += dhxn_sum * ng
            dng_ref[...] += dhxn_sum * one_sc
            dxn = dh * (ng * one_sc)
            gx_ref[rows, :] = r * (dxn - xn * jnp.mean(dxn * xn, axis=-1, keepdims=True)) + dx2_ref[rows, :]

        _in_lockstep(chain(n) for n in range(tm // sub))

    tile = pl.BlockSpec((tm, D_MODEL), lambda i: (i, 0))
    vec = _full((1, D_MODEL))
    return pl.pallas_call(
        body, name="bwd_in", grid=(seq // tm,),
        out_shape=(jax.ShapeDtypeStruct((seq, D_MODEL), F32),) + (jax.ShapeDtypeStruct((1, D_MODEL), F32),) * 3,
        in_specs=[pl.BlockSpec((tm, N_PAD), lambda i: (i, 0)), _full((N_PAD, D_MODEL)), tile, tile, vec, vec],
        out_specs=(tile, vec, vec, vec),
        compiler_params=_params(("arbitrary",)),
    )(dproj, w_in_t, x, dx2, norm_g, scale)


def _dw_in(h_b, dproj):
    seq = h_b.shape[0]
    tk = min(4096, seq)
    tn = 768
    n_t = seq // tk

    def body(h_ref, dp_ref, dwt_ref, acc):
        t = pl.program_id(1)

        @pl.when(t == 0)
        def _():
            acc[...] = jnp.zeros_like(acc)

        acc[...] += _dot_tn(dp_ref[...], h_ref[...])

        @pl.when(t == n_t - 1)
        def _():
            dwt_ref[...] = acc[...].astype(BF16)

    return pl.pallas_call(
        body, name="dw_in", grid=(N_PAD // tn, n_t),
        out_shape=jax.ShapeDtypeStruct((N_PAD, D_MODEL), BF16),
        in_specs=[pl.BlockSpec((tk, D_MODEL), lambda j, t: (t, 0)), pl.BlockSpec((tk, tn), lambda j, t: (t, j))],
        out_specs=pl.BlockSpec((tn, D_MODEL), lambda j, t: (j, 0)),
        scratch_shapes=[pltpu.VMEM((tn, D_MODEL), F32)],
        compiler_params=_params(("arbitrary", "arbitrary")),
    )(h_b, dproj)


def _adam_update(g, w, m, v, g_ref, d_ref, m_ref, v_ref):
    mn = ADAM_B1 * m + (1.0 - ADAM_B1) * g
    vn = ADAM_B2 * v + (1.0 - ADAM_B2) * (g * g)
    m_hat = mn / (1.0 - ADAM_B1 ** ADAM_STEP)
    v_hat = vn / (1.0 - ADAM_B2 ** ADAM_STEP)
    g_ref[...] = g
    d_ref[...] = -ADAM_LR * (m_hat / (jnp.sqrt(v_hat) + ADAM_EPS) + ADAM_WD * w)
    m_ref[...] = mn
    v_ref[...] = vn


def _adam_sum(name, parts, w, m, v, row_tile, col_tile=None):
    rows, cols = w.shape
    col_tile = cols if col_tile is None else col_tile
    n_parts = parts.shape[0]

    def body(p_ref, w_ref, m_ref, v_ref, g_out, d_out, m_out, v_out):
        g = p_ref[0].astype(F32)
        for j in range(1, n_parts):
            g = g + p_ref[j].astype(F32)
        _adam_update(g, w_ref[...], m_ref[...], v_ref[...], g_out, d_out, m_out, v_out)

    tile = pl.BlockSpec((row_tile, col_tile), lambda i, j: (i, j))
    return pl.pallas_call(
        body, name=name, grid=(rows // row_tile, cols // col_tile),
        out_shape=(jax.ShapeDtypeStruct((rows, cols), F32),) * 4,
        in_specs=[pl.BlockSpec((n_parts, row_tile, col_tile), lambda i, j: (0, i, j)), tile, tile, tile],
        out_specs=(tile,) * 4,
        compiler_params=_params(("arbitrary", "arbitrary")),
    )(parts, w, m, v)


def _adam_ada(sc_all16, dmod_blk16, w, m, v):
    rows, cols = w.shape

    def body(sc_ref, dm_ref, w_ref, m_ref, v_ref, g_out, d_out, m_out, v_out):
        g = _dot_tn(sc_ref[...].astype(BF16), dm_ref[...].astype(BF16))
        _adam_update(g, w_ref[...], m_ref[...], v_ref[...], g_out, d_out, m_out, v_out)

    return pl.pallas_call(
        body, name="adam_w_ada", grid=(1,),
        out_shape=(jax.ShapeDtypeStruct((rows, cols), F32),) * 4,
        in_specs=[_full(sc_all16.shape), _full(dmod_blk16.shape)] + [_full((rows, cols))] * 3,
        out_specs=(_full((rows, cols)),) * 4,
        compiler_params=_params(("arbitrary",)),
    )(sc_all16, dmod_blk16, w, m, v)


def _adam_small(parts, loss_parts, w, m, v):
    names = list(w)
    n = len(names)

    def body(*refs):
        p_refs, loss_ref = refs[:n], refs[n]
        w_refs, m_refs, v_refs = (refs[n + 1 + k * n:n + 1 + (k + 1) * n] for k in range(3))
        outs = refs[3 * n + n + 1:]
        for a in range(n):
            g = p_refs[a][0]
            for j in range(1, N_DEV):
                g = g + p_refs[a][j]
            width = w_refs[a].shape[-1]
            if g.shape[-1] != width:
                g = g[..., 0:width]
            _adam_update(g, w_refs[a][...], m_refs[a][...], v_refs[a][...], *outs[4 * a:4 * a + 4])
        total = loss_ref[0]
        for j in range(1, N_DEV):
            total = total + loss_ref[j]
        outs[4 * n][...] = total

    args = [parts[k] for k in names] + [loss_parts] + [d[k] for d in (w, m, v) for k in names]
    out_shape = tuple(jax.ShapeDtypeStruct(w[k].shape, F32) for k in names for _ in range(4))
    out_shape += (jax.ShapeDtypeStruct(loss_parts.shape[1:], F32),)
    out = pl.pallas_call(
        body, name="adam_small", grid=(1,), out_shape=out_shape,
        in_specs=[_full(a.shape) for a in args], out_specs=tuple(_full(s.shape) for s in out_shape),
        compiler_params=_params(("arbitrary",)),
    )(*args)
    return {k: out[4 * a:4 * a + 4] for a, k in enumerate(names)}, out[4 * n]


def _local_step(x2, tgt2, shift, scale, gate, norm_g, w_in_t, w_out_b, conv_w, conv_b, w_pool, ls_pool,
                mh_norm_g, b_gates, final_g, send_dw_out=None, send_dw_in=None):
    bg_pad = jnp.pad(b_gates, ((0, 0), (0, 128 - b_gates.shape[1])))
    conv_w8 = jnp.pad(conv_w, ((0, 8 - CONV_WIDTH), (0, 0)))
    fg = final_g.reshape(1, D_MODEL)

    proj, gates, h_b = _fwd_proj(x2, norm_g, scale, shift, w_in_t)
    mix, cst, nst, mst, conv_a, pooled = _mix_fwd(proj, gates, bg_pad, conv_w8, conv_b, w_pool, ls_pool, mh_norm_g)
    dx2, dmix, dwo, dgate, dfg, loss = _out_fwd_bwd(mix, x2, tgt2, w_out_b, gate, fg)
    if send_dw_out is not None:
        bg_pad = bg_pad + send_dw_out(dwo)
    dproj, dcw8, dcb, dwp, dls, dmhg, dbg = _mix_bwd(proj, gates, dmix, conv_a, pooled, cst, nst, mst, bg_pad,
                                                      conv_w8, w_pool, ls_pool, mh_norm_g)
    dw_in_t = _dw_in(h_b, dproj)[:N_IN]
    ng_in = norm_g
    if send_dw_in is not None:
        ng_in = norm_g + send_dw_in(dw_in_t, dcw8[:CONV_WIDTH])
    gx, dsh, dsc, dng = _bwd_in(dproj, w_in_t, x2, dx2, ng_in, scale)
    return dict(loss=loss, grad_x=gx, dw_in_t=dw_in_t, dw_out=dwo, dconv_w=dcw8[:CONV_WIDTH], conv_b=dcb,
                w_pool=dwp, ls_pool=dls, mh_norm_g=dmhg, b_gates=dbg, final_g=dfg, norm_g=dng,
                dmod=jnp.concatenate([dsh, dsc, dgate], axis=1))


def kernel(x, c, norm_g, w_ada, b_ada, w_in, b_gates, conv_w, conv_b, w_pool, ls_pool, mh_norm_g, w_out, final_g, loss_target, m_norm_g, m_w_ada, m_b_ada, m_w_in, m_b_gates, m_conv_w, m_conv_b, m_w_pool, m_ls_pool, m_mh_norm_g, m_w_out, m_final_g, v_norm_g, v_w_ada, v_b_ada, v_w_in, v_b_gates, v_conv_w, v_conv_b, v_w_pool, v_ls_pool, v_mh_norm_g, v_w_out, v_final_g):
    seq = x.shape[1]
    me = 4 * lax.axis_index("x") + 2 * lax.axis_index("y") + lax.axis_index("c")

    b_ada_blk = lax.dynamic_slice(b_ada, (0, me * ADA_SHARD), (1, ADA_SHARD))
    g_in, g_out, g_cw, mod_all, sc_all16 = _gather_weights_and_mod(
        (w_in[0].astype(BF16).T, w_out[0].astype(BF16), conv_w[0]), c, w_ada[0], b_ada_blk)
    w_in_t = jnp.pad(g_in.reshape(N_IN, D_MODEL), ((0, N_PAD - N_IN), (0, 0)))
    w_out_b = g_out.reshape(D_MODEL, D_MODEL)
    conv_w_full = jnp.transpose(g_cw, (1, 0, 2)).reshape(CONV_WIDTH, 2 * D_MLSTM)
    mod = lax.dynamic_index_in_dim(mod_all, me, axis=1, keepdims=False).reshape(1, 3 * D_MODEL)
    shift, scale, gate = mod[:, :D_MODEL], mod[:, D_MODEL:2 * D_MODEL], mod[:, 2 * D_MODEL:]

    flights = {}

    def send_dw_out(dwo):
        blocks = dwo.reshape(N_DEV, D_MODEL // N_DEV, D_MODEL)
        flights["out"], token = _scatter_start("send_dw_out", (blocks,))
        return token

    def send_dw_in(dw_in_t, dcw):
        blocks = dw_in_t.reshape(N_DEV, N_SHARD, D_MODEL)
        dcw_blocks = jnp.transpose(dcw.reshape(CONV_WIDTH, N_DEV, 128), (1, 0, 2))
        flights["in"], token = _scatter_start("send_dw_in", (blocks, dcw_blocks))
        return token

    r = _local_step(x[0], loss_target[0], shift, scale, gate, norm_g, w_in_t, w_out_b, conv_w_full, conv_b,
                    w_pool[0], ls_pool, mh_norm_g, b_gates, final_g, send_dw_out, send_dw_in)

    small_names = ("norm_g", "b_ada", "b_gates", "conv_b", "w_pool", "ls_pool", "mh_norm_g", "final_g")
    small_grads = dict(norm_g=r["norm_g"], b_ada=r["dmod"], b_gates=r["b_gates"], conv_b=r["conv_b"],
                       w_pool=r["w_pool"], ls_pool=r["ls_pool"], mh_norm_g=r["mh_norm_g"], final_g=r["final_g"])
    flights["small"], _ = _scatter_start(
        "send_small", (r["loss"],) + tuple(small_grads[k] for k in small_names), whole=True)
    p_in, p_cw = _scatter_wait("recv_dw_in", flights["in"], flights["small"][2 + 1 + small_names.index("w_pool")])
    (p_out,) = _scatter_wait("recv_dw_out", flights["out"], p_cw)

    in_t = _adam_sum("adam_w_in", p_in, w_in[0].T, m_w_in[0].T, v_w_in[0].T, N_SHARD, 256)
    gi, di, mi, vi = (o.T for o in in_t)
    go, do_, mo, vo = _adam_sum("adam_w_out", p_out, w_out[0], m_w_out[0], v_w_out[0], 128)
    gc, dc, mc, vc = _adam_sum("adam_conv_w", p_cw, conv_w[0], m_conv_w[0], v_conv_w[0], CONV_WIDTH)
    gathered = _scatter_wait("recv_small", flights["small"], go, whole=True)
    p_loss, p_small = gathered[0], dict(zip(small_names, gathered[1:]))

    def plain(norm_g_, b_ada_, b_gates_, conv_b_, w_pool_, ls_pool_, mh_norm_g_, final_g_):
        return dict(norm_g=norm_g_, b_ada=b_ada_, b_gates=b_gates_, conv_b=conv_b_, w_pool=w_pool_[0],
                    ls_pool=ls_pool_, mh_norm_g=mh_norm_g_, final_g=final_g_.reshape(1, D_MODEL))

    small, loss_row = _adam_small(
        p_small, p_loss,
        plain(norm_g, b_ada, b_gates, conv_b, w_pool, ls_pool, mh_norm_g, final_g),
        plain(m_norm_g, m_b_ada, m_b_gates, m_conv_b, m_w_pool, m_ls_pool, m_mh_norm_g, m_final_g),
        plain(v_norm_g, v_b_ada, v_b_gates, v_conv_b, v_w_pool, v_ls_pool, v_mh_norm_g, v_final_g))

    dmod_all = p_small["b_ada"].reshape(N_DEV, 3 * D_MODEL)
    dmod_blk16 = jnp.pad(lax.dynamic_slice(dmod_all, (0, me * ADA_SHARD), (N_DEV, ADA_SHARD)), ((0, 8), (0, 0)))
    ga, da, ma, va = _adam_ada(sc_all16, dmod_blk16, w_ada[0], m_w_ada[0], v_w_ada[0])

    names = ("norm_g", "w_ada", "b_ada", "w_in", "b_gates", "conv_w", "conv_b", "w_pool", "ls_pool", "mh_norm_g",
             "w_out", "final_g")
    shapes = dict(norm_g=norm_g.shape, b_ada=b_ada.shape, b_gates=b_gates.shape, conv_b=conv_b.shape,
                  w_pool=w_pool.shape, ls_pool=ls_pool.shape, mh_norm_g=mh_norm_g.shape, final_g=final_g.shape)
    sharded = dict(w_ada=(ga, da, ma, va), w_in=(gi, di, mi, vi), conv_w=(gc, dc, mc, vc), w_out=(go, do_, mo, vo))
    outs = []
    for kind in range(4):
        for nm in names:
            if nm in sharded:
                outs.append(sharded[nm][kind][None])
            else:
                outs.append(small[nm][kind].reshape(shapes[nm]))
    loss = loss_row[0, 0]
    grad_x = r["grad_x"].reshape(1, seq, D_MODEL)
    return (loss, grad_x, *outs)
```

```python
import jax
import jax.numpy as jnp
from jax import lax
from jax.experimental import pallas as pl
from jax.experimental.pallas import tpu as pltpu

F32 = jnp.float32
BF16 = jnp.bfloat16

D_MODEL = 1024
D_POOL = 512
D_MLSTM = 512
N_HEADS = 4
HEAD_DIM = 128
CHUNK = 128
POOL_WINDOWS = (2, 4, 8, 16)
POOL_GROUP_DIM = 128
CONV_WIDTH = 4
EPS = 1e-6
N_MAIN = 3584
N_IN = 3592
N_PAD = 3840
N_SHARD = N_IN // 8
ADA_SHARD = 3 * D_MODEL // 8
N_DEV = 8
CONV_HALO = 8
POOL_HALO = 16
NEG_BIG = -1e30
VMEM_LIMIT_BYTES = 56 * 1024 * 1024

ADAM_LR = 0.001
ADAM_B1 = 0.9
ADAM_B2 = 0.999
ADAM_EPS = 1e-08
ADAM_WD = 0.01
ADAM_STEP = 10

def _dot(a, b):
    return jnp.dot(a, b, preferred_element_type=F32)


def _dot_nt(a, b):
    return lax.dot_general(a, b, (((1,), (1,)), ((), ())), preferred_element_type=F32)


def _dot_tn(a, b):
    return lax.dot_general(a, b, (((0,), (0,)), ((), ())), preferred_element_type=F32)


def _dot_f32(a, b):
    return jnp.dot(a, b, precision=lax.Precision.HIGHEST, preferred_element_type=F32)


def _row_mean_mxu(x):
    return _dot(x.astype(BF16), jnp.full((HEAD_DIM, HEAD_DIM), 1.0 / HEAD_DIM, BF16))


def _sigmoid(x):
    return jax.nn.sigmoid(x)


def _log_sigmoid(x):
    return jnp.minimum(x, 0.0) - jnp.log1p(jnp.exp(-jnp.abs(x)))


def _params(sem):
    return pltpu.CompilerParams(dimension_semantics=sem, vmem_limit_bytes=VMEM_LIMIT_BYTES)


def _full(shape):
    n = len(shape)
    return pl.BlockSpec(shape, lambda *_: (0,) * n)


def _mesh_pos():
    return lax.axis_index("x"), lax.axis_index("y"), lax.axis_index("c")


def _peer(k):
    x, y, c = _mesh_pos()
    px = 1 - x if (k >> 2) & 1 else x
    py = 1 - y if (k >> 1) & 1 else y
    pc = 1 - c if k & 1 else c
    return (px, py, pc), 4 * px + 2 * py + pc


def _remote(src, dst, send_sem, recv_sem, to):
    return pltpu.make_async_remote_copy(src_ref=src, dst_ref=dst, send_sem=send_sem, recv_sem=recv_sem, device_id=to,
                                        device_id_type=pl.DeviceIdType.MESH)


def _two_level_gather(src, dst, send_sems, recv_sems, local_sems):
    n = len(src)
    x, y, c = _mesh_pos()
    me = 4 * x + 2 * y + c
    sibling = (x, y, 1 - c)
    south = c == 0
    near = (jnp.where(south, 1 - x, x), jnp.where(south, y, 1 - y))
    far = (jnp.where(south, x, 1 - x), jnp.where(south, 1 - y, y))
    diag = (1 - x, 1 - y)

    def block_of(chip, core):
        return 4 * chip[0] + 2 * chip[1] + core

    def copy(a, k, block, to, own=False):
        return _remote(src[a] if own else dst[a].at[block], dst[a].at[block], send_sems.at[a, k], recv_sems.at[a, k], to)

    local = [pltpu.make_async_copy(src[a], dst[a].at[me], local_sems.at[a]) for a in range(n)]
    sent = [copy(a, 0, me, sibling, True) for a in range(n)]
    sent += [copy(a, 1, me, (*near, c), True) for a in range(n)]
    sent += [copy(a, 2, me, (*far, c), True) for a in range(n)]
    for cp in local + sent:
        cp.start()
    yield
    for a in range(n):
        copy(a, 1, block_of(near, c), sibling).wait_recv()
        sent += [copy(a, 3, block_of(near, c), (*far, c)), copy(a, 4, block_of(near, c), sibling)]
        sent[-2].start()
        sent[-1].start()
    for k, chip in ((2, far), (3, diag)):
        for a in range(n):
            copy(a, k, block_of(chip, c), sibling).wait_recv()
            sent.append(copy(a, 3 + k, block_of(chip, c), sibling))
            sent[-1].start()
    for k, chip in ((0, (x, y)), (4, far), (5, near), (6, diag)):
        for a in range(n):
            copy(a, k, block_of(chip, 1 - c), sibling).wait_recv()
    for cp in sent:
        cp.wait_send()
    for cp in local:
        cp.wait()


GATHER_COPIES = 7


def _swap_with_all(buf, send_sems, recv_sems):
    x, y, c = _mesh_pos()
    me = 4 * x + 2 * y + c
    copies = [_remote(buf.at[me], buf.at[me], send_sems.at[k - 1], recv_sems.at[k - 1], _peer(k)[0])
              for k in range(1, N_DEV)]
    for cp in copies:
        cp.start()
    for cp in copies:
        cp.wait()


def _gather_weights_and_mod(shards, c_row, w_ada_blk, b_ada_blk):
    n = len(shards)

    def body(*refs):
        src, (c_ref, w_ref, b_ref) = refs[:n], refs[n:n + 3]
        dst, (mod_ref, sc_ref) = refs[n + 3:2 * n + 3], refs[2 * n + 3:2 * n + 5]
        c_all, g_send, g_recv, g_local, c_send, c_recv, m_send, m_recv = refs[2 * n + 5:]
        x, y, c = _mesh_pos()
        me = 4 * x + 2 * y + c
        gather = _two_level_gather(src, dst, g_send, g_recv, g_local)
        next(gather)
        c_all[me] = c_ref[...]
        _swap_with_all(c_all, c_send, c_recv)
        cv = jnp.concatenate([c_all[j] for j in range(N_DEV)] + [jnp.zeros((N_DEV, D_MODEL), F32)], axis=0)
        sc = cv * _sigmoid(cv)
        sc_ref[...] = sc
        blk = _dot(sc.astype(BF16), w_ref[...].astype(BF16)) + b_ref[...]
        mod_ref[me] = blk[0:N_DEV, :]
        _swap_with_all(mod_ref, m_send, m_recv)
        for _ in gather:
            pass

    hbm = pl.BlockSpec(memory_space=pltpu.HBM)
    vmem = pl.BlockSpec(memory_space=pltpu.VMEM)
    peers = pltpu.SemaphoreType.DMA((N_DEV - 1,))
    return pl.pallas_call(
        body, name="gather_weights",
        out_shape=tuple(jax.ShapeDtypeStruct((N_DEV,) + s.shape, s.dtype) for s in shards)
        + (jax.ShapeDtypeStruct((N_DEV, N_DEV, ADA_SHARD), F32), jax.ShapeDtypeStruct((2 * N_DEV, D_MODEL), F32)),
        in_specs=[hbm] * n + [vmem] * 3, out_specs=tuple([hbm] * n + [vmem] * 2),
        scratch_shapes=[pltpu.VMEM((N_DEV, 1, D_MODEL), F32),
                        pltpu.SemaphoreType.DMA((n, GATHER_COPIES)), pltpu.SemaphoreType.DMA((n, GATHER_COPIES)),
                        pltpu.SemaphoreType.DMA((n,)), peers, peers, peers, peers],
    )(*shards, c_row, w_ada_blk, b_ada_blk)


def _scatter_copies(src, land, send_sems, recv_sems, whole=False):
    x, y, c = _mesh_pos()
    me = 4 * x + 2 * y + c
    copies = []
    for k in range(1, N_DEV):
        peer, p = _peer(k)
        for a in range(len(src)):
            i = a * (N_DEV - 1) + k - 1
            copies.append(_remote(src[a] if whole else src[a].at[p], land[a].at[me], send_sems.at[i],
                                  recv_sems.at[i], peer))
    return copies


def _scatter_start(name, blocks, whole=False):
    n = len(blocks)

    def body(*refs):
        src, land = refs[:n], refs[n:2 * n]
        send_sems, recv_sems = refs[2 * n], refs[2 * n + 1]
        token_ref = refs[-1]
        for cp in _scatter_copies(src, land, send_sems, recv_sems, whole):
            cp.start()
        token_ref[...] = jnp.zeros_like(token_ref)

    hbm = pl.BlockSpec(memory_space=pltpu.HBM)
    sem = pl.BlockSpec(memory_space=pltpu.SEMAPHORE)
    landing = [((N_DEV,) + b.shape if whole else b.shape, b.dtype) for b in blocks]
    through = tuple(pltpu.HBM(b.shape, b.dtype) for b in blocks) + tuple(pltpu.HBM(s, d) for s, d in landing)
    args = [pltpu.with_memory_space_constraint(b, pltpu.HBM) for b in blocks]
    args += [pltpu.with_memory_space_constraint(lax.empty(s, d), pltpu.HBM) for s, d in landing]
    out = pl.pallas_call(
        body, name=name,
        out_shape=(pltpu.SemaphoreType.DMA((n * (N_DEV - 1),)),) * 2 + through
        + (jax.ShapeDtypeStruct((8, 128), F32),),
        in_specs=[hbm] * (2 * n),
        out_specs=(sem, sem) + (hbm,) * (2 * n) + (pl.BlockSpec(memory_space=pltpu.VMEM),),
        input_output_aliases={i: 2 + i for i in range(2 * n)},
        compiler_params=pltpu.CompilerParams(has_side_effects=pltpu.SideEffectType.DATAFLOW_SIDE_EFFECTING),
    )(*args)
    return out[:-1], out[-1][0:1, 0:1]


def _scatter_wait(name, state, after, whole=False, fill=None):
    n = (len(state) - 2) // 2
    send_sems, recv_sems = state[0], state[1]
    src, land = state[2:2 + n], state[2 + n:]

    def body(*refs):
        src_r, land_r = refs[:n], refs[n:2 * n]
        for cp in _scatter_copies(src_r, land_r, refs[2 * n], refs[2 * n + 1], whole):
            cp.wait_send()
            cp.wait_recv()

    hbm = pl.BlockSpec(memory_space=pltpu.HBM)
    sem = pl.BlockSpec(memory_space=pltpu.SEMAPHORE)
    out = pl.pallas_call(
        body, name=name,
        out_shape=tuple(pltpu.HBM(b.shape, b.dtype) for b in src + land),
        in_specs=[hbm] * (2 * n) + [sem, sem, pl.BlockSpec(memory_space=pl.ANY)],
        out_specs=(hbm,) * (2 * n),
        input_output_aliases={i: i for i in range(2 * n)},
        compiler_params=pltpu.CompilerParams(has_side_effects=pltpu.SideEffectType.DATAFLOW_SIDE_EFFECTING),
    )(*src, *land, send_sems, recv_sems, after)
    me = 4 * lax.axis_index("x") + 2 * lax.axis_index("y") + lax.axis_index("c")
    landed = []
    for a in range(n):
        if fill is not None and not fill[a]:
            landed.append((out[a], out[n + a]))
            continue
        own = out[a][None] if whole else lax.dynamic_index_in_dim(out[a], me, axis=0, keepdims=True)
        landed.append(lax.dynamic_update_slice_in_dim(out[n + a], own, me, axis=0))
    return landed


def _fwd_proj(x, norm_g, scale, shift, w_in_t):
    seq = x.shape[0]
    tm = min(512, seq)
    sub = min(256, tm)
    tn = 512

    def body(x_ref, ng_ref, sc_ref, sh_ref, wt_ref, proj_ref, gates_ref, h_ref):
        def chain(n):
            for _ in range(n):
                yield
            rows = slice(n * sub, (n + 1) * sub)
            xt = x_ref[rows, :]
            r = lax.rsqrt(jnp.mean(xt * xt, axis=-1, keepdims=True) + EPS)
            h = ((xt * r) * ng_ref[...]) * (1.0 + sc_ref[...]) + sh_ref[...]
            hb = h.astype(BF16)
            h_ref[rows, :] = hb
            yield
            gates_ref[rows, :] = _dot_nt(hb, wt_ref[N_MAIN:N_MAIN + 128, :])
            for j in range(N_MAIN // tn):
                proj_ref[rows, j * tn:(j + 1) * tn] = _dot_nt(hb, wt_ref[j * tn:(j + 1) * tn, :])

        _in_lockstep(chain(n) for n in range(tm // sub))

    vec = _full((1, D_MODEL))
    tile = pl.BlockSpec((tm, D_MODEL), lambda i: (i, 0))
    return pl.pallas_call(
        body, name="fwd_proj", grid=(seq // tm,),
        out_shape=(jax.ShapeDtypeStruct((seq, N_MAIN), F32), jax.ShapeDtypeStruct((seq, 128), F32),
                   jax.ShapeDtypeStruct((seq, D_MODEL), BF16)),
        in_specs=[tile, vec, vec, vec, _full((N_PAD, D_MODEL))],
        out_specs=(pl.BlockSpec((tm, N_MAIN), lambda i: (i, 0)), pl.BlockSpec((tm, 128), lambda i: (i, 0)), tile),
        compiler_params=_params(("arbitrary",)),
    )(x, norm_g, scale, shift, w_in_t)


def _gate_forms(gpre):
    r = lax.broadcasted_iota(jnp.int32, (CHUNK, CHUNK), 0)
    c = lax.broadcasted_iota(jnp.int32, (CHUNK, CHUNK), 1)
    causal = c <= r
    ltri = jnp.where(causal, 1.0, 0.0).astype(F32)
    utri = jnp.where(r <= c, 1.0, 0.0).astype(F32)
    bcol = _dot_f32(ltri, _log_sigmoid(gpre))
    gt8 = gpre.T[0:8, :]
    brow = _dot_f32(_log_sigmoid(gt8), utri)
    return causal, utri, bcol, gt8, brow


def _in_lockstep(stages):
    alive = list(stages)
    while alive:
        still = []
        for g in alive:
            try:
                next(g)
                still.append(g)
            except StopIteration:
                pass
        alive = still


def _head_fwd(qh, kh, vh, bc, br, igr, m_prev, c_h, n_row, causal):
    qb, kb, vb, cb = qh.astype(BF16), kh.astype(BF16), vh.astype(BF16), c_h.astype(BF16)
    qk = _dot_nt(qb, kb)
    cq = _dot_nt(qb, cb)
    nq = _dot_nt(qb, jnp.broadcast_to(n_row.astype(BF16), (HEAD_DIM, HEAD_DIM)))
    yield
    dlog = jnp.where(causal, bc - br + igr, NEG_BIG)
    inter_log = bc + m_prev
    m_t = jnp.maximum(inter_log, jnp.max(dlog, axis=-1, keepdims=True))
    yield
    dmat = jnp.exp(dlog - m_t)
    inter = jnp.exp(inter_log - m_t)
    s = qk * dmat
    sv = _dot(s.astype(BF16), vb)
    yield
    den = jnp.sum(s, axis=-1, keepdims=True) + inter * nq
    emt = jnp.exp(-m_t)
    yield
    num = sv + inter * cq
    dn = jnp.maximum(jnp.abs(den), emt)
    hm = num / dn
    return dict(dmat=dmat, inter=inter, qb=qb, kb=kb, vb=vb, cb=cb, s=s, cq=cq, nq=nq, den=den, emt=emt,
                dn=dn, hm=hm)


def _state_weights(bc, igc, m_prev, m_new=None):
    last = lax.broadcasted_iota(jnp.int32, (CHUNK, 1), 0) == CHUNK - 1
    b_last = jnp.sum(jnp.where(last, bc, 0.0), axis=0, keepdims=True)
    wlog = b_last - bc + igc
    if m_new is None:
        m_new = jnp.maximum(b_last + m_prev, jnp.max(wlog, axis=0, keepdims=True))
    w_c = jnp.exp(wlog - m_new)
    decay = jnp.exp(b_last + m_prev - m_new)
    return w_c, decay, m_new, last


def _rows_back(x, k):
    return x if k == 0 else pltpu.roll(x, k, 0)


def _rows_ahead(x, k):
    return x if k == 0 else pltpu.roll(x, x.shape[0] - k, 0)


def _conv_taps(xpad):
    return [_rows_back(xpad, CONV_WIDTH - 1 - j)[CONV_HALO:, :] for j in range(CONV_WIDTH)]


def _conv_pre(taps, cw_ref, cb_ref):
    a = cb_ref[...]
    for j in range(CONV_WIDTH):
        a = a + cw_ref[j:j + 1, :] * taps[j]
    return a


def _window_sum(x, w, shift):
    k = 1
    while k < w:
        x = x + shift(x, k)
        k *= 2
    return x


def _pool_window_sum(upad_ref, g, w):
    lanes = slice(g * POOL_GROUP_DIM, (g + 1) * POOL_GROUP_DIM)
    return _window_sum(upad_ref[:, lanes], w, _rows_back)[POOL_HALO:, :]


def _pool_inv_count(row0, rows, w):
    pos = row0 + lax.broadcasted_iota(jnp.int32, (rows, 1), 0) + 1
    return 1.0 / jnp.minimum(pos, w).astype(F32)


FWD_CHUNKS = 4
BWD_CHUNKS = 4


def _mix_fwd(proj, gates, bg_pad, conv_w8, conv_b, w_pool, ls_pool, mh_g):
    seq = proj.shape[0]
    n_chunks = seq // CHUNK
    per_step = FWD_CHUNKS
    blk = per_step * CHUNK

    def body(uz_ref, qk_ref, v_ref, o_ref, zm_ref, uh_ref, qkh_ref, g_ref, bg_ref, cw_ref, cb_ref, wp_ref,
             ls_ref, mhg_ref, mix_ref, cst_ref, nst_ref, mst_ref, a_ref, pooled_ref, c_scr, n_scr, m_scr, xpad, upad):
        i = pl.program_id(0)

        @pl.when(i == 0)
        def _():
            c_scr[...] = jnp.zeros_like(c_scr)
            n_scr[...] = jnp.zeros_like(n_scr)
            m_scr[...] = jnp.zeros_like(m_scr)

        first = i == 0

        upad[0:POOL_HALO, :] = jnp.where(first, 0.0, uh_ref[...])
        upad[POOL_HALO:POOL_HALO + blk, :] = uz_ref[:, 0:D_POOL]
        for g, w in enumerate(POOL_WINDOWS):
            lanes = slice(g * POOL_GROUP_DIM, (g + 1) * POOL_GROUP_DIM)
            pooled = (_pool_window_sum(upad, g, w) * _pool_inv_count(i * blk, blk, w) - uz_ref[:, lanes]).astype(BF16)
            pooled_ref[:, lanes] = pooled
            y = _dot(pooled, wp_ref[g].astype(BF16)) * ls_ref[:, lanes]
            zp = uz_ref[:, D_POOL + g * POOL_GROUP_DIM:D_POOL + (g + 1) * POOL_GROUP_DIM]
            mix_ref[:, lanes] = (y * (zp * _sigmoid(zp))).astype(BF16)

        xpad[0:CONV_HALO, :] = jnp.where(first, 0.0, qkh_ref[...])
        xpad[CONV_HALO:CONV_HALO + blk, :] = qk_ref[...]
        a = _conv_pre(_conv_taps(xpad[...]), cw_ref, cb_ref)
        a_ref[...] = a
        qk = a * _sigmoid(a)

        def head(rows, h, qh, kh, vh, bc, br, igr, m_prev, c_h, n_row, causal):
            lanes = slice(h * HEAD_DIM, (h + 1) * HEAD_DIM)
            f = yield from _head_fwd(qh, kh, vh, bc, br, igr, m_prev, c_h, n_row, causal)
            yield
            hm = f["hm"]
            hn = hm * lax.rsqrt(_row_mean_mxu(hm * hm) + EPS) * mhg_ref[:, lanes]
            zm = zm_ref[rows, lanes]
            out = hn * _sigmoid(o_ref[rows, lanes]) * (zm * _sigmoid(zm))
            mix_ref[rows, D_POOL + h * HEAD_DIM:D_POOL + (h + 1) * HEAD_DIM] = out.astype(BF16)

        c_cur = [c_scr[h] for h in range(N_HEADS)]
        n_cur = [n_scr[h:h + 1, :] for h in range(N_HEADS)]
        m_cur = [m_scr[h:h + 1, 0:1] for h in range(N_HEADS)]
        chains = []
        for s in range(per_step):
            rows = slice(s * CHUNK, (s + 1) * CHUNK)
            gpre = g_ref[rows, :] + bg_ref[...]
            causal, _, bcol, gt8, brow = _gate_forms(gpre)
            nst_ref[s] = jnp.zeros((8, 128), F32)
            mst_ref[s] = jnp.zeros((8, 128), F32)
            for h in range(N_HEADS):
                lanes = slice(h * HEAD_DIM, (h + 1) * HEAD_DIM)
                cst_ref[s, h] = c_cur[h]
                nst_ref[s, h:h + 1, :] = n_cur[h]
                mst_ref[s, h:h + 1, :] = jnp.broadcast_to(m_cur[h], (1, 128))
                qh = qk[rows, lanes]
                kh = qk[rows, D_MLSTM + h * HEAD_DIM:D_MLSTM + (h + 1) * HEAD_DIM] * (HEAD_DIM ** -0.5)
                vh = v_ref[rows, lanes]
                bc = bcol[:, N_HEADS + h:N_HEADS + h + 1]
                br = brow[N_HEADS + h:N_HEADS + h + 1, :]
                igr = gt8[h:h + 1, :]
                igc = gpre[:, h:h + 1]
                chains.append(head(rows, h, qh, kh, vh, bc, br, igr, m_cur[h], c_cur[h], n_cur[h], causal))
                w_c, decay, m_new, _ = _state_weights(bc, igc, m_cur[h])
                c_cur[h] = decay * c_cur[h] + _dot_tn((vh * w_c).astype(BF16), kh.astype(BF16))
                n_cur[h] = decay * n_cur[h] + jnp.sum(w_c * kh, axis=0, keepdims=True)
                m_cur[h] = m_new
        for h in range(N_HEADS):
            c_scr[h] = c_cur[h]
            n_scr[h:h + 1, :] = n_cur[h]
            m_scr[h:h + 1, :] = jnp.broadcast_to(m_cur[h], (1, 128))
        _in_lockstep(chains)

    in_specs = [
        pl.BlockSpec((blk, 1024), lambda i: (i, 0)),
        pl.BlockSpec((blk, 1024), lambda i: (i, 1)),
        pl.BlockSpec((blk, 512), lambda i: (i, 4)),
        pl.BlockSpec((blk, 512), lambda i: (i, 5)),
        pl.BlockSpec((blk, 512), lambda i: (i, 6)),
        pl.BlockSpec((POOL_HALO, 512), lambda i: (jnp.maximum(i * (blk // POOL_HALO) - 1, 0), 0)),
        pl.BlockSpec((CONV_HALO, 1024), lambda i: (jnp.maximum(i * (blk // CONV_HALO) - 1, 0), 1)),
        pl.BlockSpec((blk, 128), lambda i: (i, 0)),
        _full((1, 128)), _full((8, 1024)), _full((1, 1024)), _full((4, 128, 128)), _full((1, 512)),
        _full((1, 512))]
    return pl.pallas_call(
        body, name="mix_fwd", grid=(n_chunks // per_step,),
        out_shape=(jax.ShapeDtypeStruct((seq, D_MODEL), BF16),
                   jax.ShapeDtypeStruct((n_chunks, N_HEADS, HEAD_DIM, HEAD_DIM), F32),
                   jax.ShapeDtypeStruct((n_chunks, 8, 128), F32),
                   jax.ShapeDtypeStruct((n_chunks, 8, 128), F32),
                   jax.ShapeDtypeStruct((seq, 2 * D_MLSTM), F32),
                   jax.ShapeDtypeStruct((seq, D_POOL), BF16)),
        in_specs=in_specs,
        out_specs=(pl.BlockSpec((blk, D_MODEL), lambda i: (i, 0)),
                   pl.BlockSpec((per_step, N_HEADS, HEAD_DIM, HEAD_DIM), lambda i: (i, 0, 0, 0)),
                   pl.BlockSpec((per_step, 8, 128), lambda i: (i, 0, 0)),
                   pl.BlockSpec((per_step, 8, 128), lambda i: (i, 0, 0)),
                   pl.BlockSpec((blk, 2 * D_MLSTM), lambda i: (i, 0)),
                   pl.BlockSpec((blk, D_POOL), lambda i: (i, 0))),
        scratch_shapes=[pltpu.VMEM((N_HEADS, HEAD_DIM, HEAD_DIM), F32), pltpu.VMEM((8, 128), F32),
                        pltpu.VMEM((8, 128), F32), pltpu.VMEM((CONV_HALO + blk, 1024), F32),
                        pltpu.VMEM((POOL_HALO + blk, D_POOL), F32)],
        compiler_params=_params(("arbitrary",)),
    )(proj, proj, proj, proj, proj, proj, proj, gates, bg_pad, conv_w8, conv_b, w_pool, ls_pool, mh_g)


def _out_fwd_bwd(mix, x, tgt, w_out_b, gate, final_g):
    seq = x.shape[0]
    tm = min(512, seq)
    sub = min(256, tm)

    def body(mix_ref, x_ref, t_ref, w_ref, gate_ref, fg_ref, dx2_ref, dmix_ref, dwo_ref, dgate_ref, dfg_ref,
             loss_ref, dwo_acc):
        @pl.when(pl.program_id(0) == 0)
        def _():
            dwo_acc[...] = jnp.zeros_like(dwo_acc)
            dgate_ref[...] = jnp.zeros_like(dgate_ref)
            dfg_ref[...] = jnp.zeros_like(dfg_ref)
            loss_ref[...] = jnp.zeros_like(loss_ref)

        w = w_ref[...]
        gate_v = gate_ref[...]
        fg = fg_ref[...]
        do2_parts = [None] * (tm // sub)

        def chain(n):
            rows = slice(n * sub, (n + 1) * sub)
            o2 = _dot(mix_ref[rows, :], w)
            yield
            x2 = x_ref[rows, :] + gate_v * o2
            r2 = lax.rsqrt(jnp.mean(x2 * x2, axis=-1, keepdims=True) + EPS)
            x2n = x2 * r2
            err = x2n * fg - t_ref[rows, :]
            part = 0.5 * jnp.sum(jnp.sum(err * err, axis=-1, keepdims=True), axis=0, keepdims=True) / D_MODEL
            loss_ref[...] += jnp.broadcast_to(part, loss_ref.shape)
            dy = err / D_MODEL
            dfg_ref[...] += jnp.sum(dy * x2n, axis=0, keepdims=True)
            gdy = dy * fg
            dx2 = r2 * (gdy - x2n * jnp.mean(gdy * x2n, axis=-1, keepdims=True))
            dx2_ref[rows, :] = dx2
            dgate_ref[...] += jnp.sum(dx2 * o2, axis=0, keepdims=True)
            do2 = (dx2 * gate_v).astype(BF16)
            dmix_ref[rows, :] = _dot_nt(do2, w)
            do2_parts[n] = do2

        _in_lockstep(chain(n) for n in range(tm // sub))
        dwo_acc[...] += _dot_tn(mix_ref[...], jnp.concatenate(do2_parts, axis=0))

        @pl.when(pl.program_id(0) == seq // tm - 1)
        def _():
            dwo_ref[...] = dwo_acc[...].astype(BF16)

    tile = pl.BlockSpec((tm, D_MODEL), lambda i: (i, 0))
    vec = _full((1, D_MODEL))
    return pl.pallas_call(
        body, name="out_fwd_bwd", grid=(seq // tm,),
        out_shape=(jax.ShapeDtypeStruct((seq, D_MODEL), F32), jax.ShapeDtypeStruct((seq, D_MODEL), F32),
                   jax.ShapeDtypeStruct((D_MODEL, D_MODEL), BF16), jax.ShapeDtypeStruct((1, D_MODEL), F32),
                   jax.ShapeDtypeStruct((1, D_MODEL), F32), jax.ShapeDtypeStruct((1, 128), F32)),
        in_specs=[tile, tile, tile, _full((D_MODEL, D_MODEL)), vec, vec],
        out_specs=(tile, tile, _full((D_MODEL, D_MODEL)), vec, vec, _full((1, 128))),
        scratch_shapes=[pltpu.VMEM((D_MODEL, D_MODEL), F32)],
        compiler_params=_params(("arbitrary",)),
    )(mix, x, tgt, w_out_b, gate, final_g)


def _mix_bwd(proj, gates, dmix, conv_a, pooled, cst, nst, mst, bg_pad, conv_w8, w_pool, ls_pool, mh_g):
    seq = proj.shape[0]
    n_chunks = seq // CHUNK
    per_step = BWD_CHUNKS
    blk = per_step * CHUNK
    n_blocks = n_chunks // per_step

    def body(zp_ref, qk_ref, v_ref, o_ref, zm_ref, g_ref, dmix_ref, a_ref, pooled_ref, cst_ref, nst_ref, mst_ref,
             mnx_ref, bg_ref, cw_ref, wp_ref, ls_ref, mhg_ref,
             dp_ref, dcw_ref, dcb_ref, dwp_ref, dls_ref, dmhg_ref, dbg_ref,
             dc_scr, dn_scr, dapad, dpipad):
        i = pl.program_id(0)
        bi = n_blocks - 1 - i

        @pl.when(i == 0)
        def _():
            for ref in (dc_scr, dn_scr, dcw_ref, dcb_ref, dwp_ref, dls_ref, dmhg_ref, dbg_ref):
                ref[...] = jnp.zeros_like(ref)
            dapad[blk:blk + CONV_HALO, :] = jnp.zeros((CONV_HALO, 1024), F32)
            dpipad[blk:blk + POOL_HALO, :] = jnp.zeros((POOL_HALO, D_POOL), F32)

        dpooled = []
        for g, w in enumerate(POOL_WINDOWS):
            lanes = slice(g * POOL_GROUP_DIM, (g + 1) * POOL_GROUP_DIM)
            zlanes = slice(D_POOL + g * POOL_GROUP_DIM, D_POOL + (g + 1) * POOL_GROUP_DIM)
            inv = _pool_inv_count(bi * blk, blk, w)
            pb = pooled_ref[:, lanes]
            wpb = wp_ref[g].astype(BF16)
            yw = _dot(pb, wpb)
            ls = ls_ref[:, lanes]
            zp = zp_ref[:, lanes]
            sg = _sigmoid(zp)
            dpo = dmix_ref[:, lanes]
            dp_ref[:, zlanes] = (dpo * (yw * ls) * (sg * (1.0 + zp * (1.0 - sg)))).astype(BF16)
            dy = dpo * (zp * sg)
            dls_ref[:, lanes] += jnp.sum(dy * yw, axis=0, keepdims=True)
            dyw = (dy * ls).astype(BF16)
            dwp_ref[g] += _dot_tn(pb, dyw)
            dpl = _dot_nt(dyw, wpb)
            dpooled.append(dpl)
            dpipad[0:blk, lanes] = dpl * inv
        for g, w in enumerate(POOL_WINDOWS):
            lanes = slice(g * POOL_GROUP_DIM, (g + 1) * POOL_GROUP_DIM)
            du = _window_sum(dpipad[:, lanes], w, _rows_ahead)[0:blk, :] - dpooled[g]
            dp_ref[:, lanes] = du.astype(BF16)
        dpipad[blk:blk + POOL_HALO, :] = dpipad[0:POOL_HALO, :]

        def silu_and_slope(rows, cols):
            a = a_ref[rows, cols]
            sg = _sigmoid(a)
            return a * sg, sg * (1.0 + a * (1.0 - sg))

        lane = lax.broadcasted_iota(jnp.int32, (CHUNK, 128), 1)
        row = lax.broadcasted_iota(jnp.int32, (CHUNK, 128), 0)
        scale_k = HEAD_DIM ** -0.5
        forms = [None] * per_step
        col_g_rows = [[] for _ in range(per_step)]
        dig_parts = [[] for _ in range(per_step)]
        db_parts = [[] for _ in range(per_step)]
        d_state = [[None] * N_HEADS for _ in range(per_step)]

        def state_terms(s, h, c_h, n_row, vb, kb):
            dcn, dnn = d_state[s][h]
            dcnb = dcn.astype(BF16)
            amat = _dot(vb, dcnb) + dnn
            kdc = _dot_nt(kb, dcnb)
            ddecay = (jnp.sum(jnp.sum(dcn * c_h, axis=-1, keepdims=True), axis=0, keepdims=True)
                      + jnp.sum(dnn * n_row, axis=-1, keepdims=True))
            return dcn, dnn, amat, kdc, ddecay

        def head(s, h):
            rows = slice(s * CHUNK, (s + 1) * CHUNK)
            lanes = slice(h * HEAD_DIM, (h + 1) * HEAD_DIM)
            klanes = slice(D_MLSTM + h * HEAD_DIM, D_MLSTM + (h + 1) * HEAD_DIM)
            gpre, causal, utri, bcol, gt8, brow = forms[s]
            qh, dsilu_q = silu_and_slope(rows, lanes)
            kh, dsilu_k = silu_and_slope(rows, klanes)
            kh = kh * scale_k
            vh = v_ref[rows, lanes]
            bc = bcol[:, N_HEADS + h:N_HEADS + h + 1]
            br = brow[N_HEADS + h:N_HEADS + h + 1, :]
            igr = gt8[h:h + 1, :]
            igc = gpre[:, h:h + 1]
            m_prev = mst_ref[s, h:h + 1, 0:1]
            m_next = mnx_ref[0, h:h + 1, 0:1] if s == per_step - 1 else mst_ref[s + 1, h:h + 1, 0:1]
            c_h = cst_ref[s, h]
            n_row = nst_ref[s, h:h + 1, :]
            w_c, decay, _, last = _state_weights(bc, igc, m_prev, m_next)
            terms = None
            if s == per_step - 1:
                terms = state_terms(s, h, c_h, n_row, vh.astype(BF16), kh.astype(BF16))
            f = yield from _head_fwd(qh, kh, vh, bc, br, igr, m_prev, c_h, n_row, causal)
            qb, kb, vb, cb = f["qb"], f["kb"], f["vb"], f["cb"]
            sm, dmat, inter, den, dn, hm = f["s"], f["dmat"], f["inter"], f["den"], f["dn"], f["hm"]
            yield

            rinv = lax.rsqrt(_row_mean_mxu(hm * hm) + EPS)
            hmn = hm * rinv
            gh = mhg_ref[:, lanes]
            o_pre = o_ref[rows, lanes]
            og = _sigmoid(o_pre)
            zm = zm_ref[rows, lanes]
            sgz = _sigmoid(zm)
            sz = zm * sgz
            dout = dmix_ref[rows, D_POOL + h * HEAD_DIM:D_POOL + (h + 1) * HEAD_DIM]
            hn = hmn * gh
            dp_ref[rows, 2560 + h * HEAD_DIM:2560 + (h + 1) * HEAD_DIM] = (
                dout * hn * sz * og * (1.0 - og)).astype(BF16)
            dp_ref[rows, 3072 + h * HEAD_DIM:3072 + (h + 1) * HEAD_DIM] = (
                dout * hn * og * (sgz * (1.0 + zm * (1.0 - sgz)))).astype(BF16)
            dhn = dout * og * sz
            dmhg_ref[:, lanes] += jnp.sum(dhn * hmn, axis=0, keepdims=True)
            dyn = dhn * gh
            dhm = rinv * (dyn - hmn * _row_mean_mxu(dyn * hmn))
            yield

            inv_dn = 1.0 / dn
            dnum = dhm * inv_dn
            hd = jnp.sum(dhm * hm, axis=-1, keepdims=True)
            dden = jnp.where(jnp.abs(den) > f["emt"], -hd / den, 0.0)
            dnb = dnum.astype(BF16)
            dnv = _dot_nt(dnb, vb)
            dv = _dot_tn(sm.astype(BF16), dnb)
            dnc = _dot(dnb, cb)
            dc_prev = _dot_tn((inter * dnum).astype(BF16), qb)
            dn_prev = jnp.sum((inter * dden) * qh, axis=0, keepdims=True)
            yield
            ds = dnv + dden
            dqk = (ds * dmat).astype(BF16)
            dqk_k = _dot(dqk, kb)
            dk = _dot_tn(dqk, qb)
            for _ in range(per_step - 1 - s):
                yield
            if terms is None:
                terms = state_terms(s, h, c_h, n_row, vb, kb)
            dcn, dnn, amat, kdc, ddecay = terms
            d_start = (decay * dcn + dc_prev, decay * dnn + dn_prev)
            if s > 0:
                d_state[s - 1][h] = d_start
            else:
                dc_scr[h] = d_start[0]
                dn_scr[h:h + 1, :] = d_start[1]
            yield
            gmat = ds * sm
            row_g = jnp.sum(gmat, axis=-1, keepdims=True)
            col_g_rows[s].append(jnp.where(row == h, jnp.sum(gmat, axis=0, keepdims=True), 0.0))
            gcol = inter * (jnp.sum(dnum * f["cq"], axis=-1, keepdims=True) + dden * f["nq"])
            dw = jnp.sum(amat * kh, axis=-1, keepdims=True)
            e = dw * w_c
            db_last = ddecay * decay + jnp.sum(e, axis=0, keepdims=True)
            dig_parts[s].append(jnp.where(lane == h, e, 0.0))
            db_parts[s].append(
                jnp.where(lane == N_HEADS + h, row_g + gcol - e + jnp.where(last, db_last, 0.0), 0.0))
            yield
            dq = dqk_k + inter * (dnc + dden * n_row)
            dp_ref[rows, 2048 + h * HEAD_DIM:2048 + (h + 1) * HEAD_DIM] = (dv + w_c * kdc).astype(BF16)
            dapad[rows, lanes] = dq * dsilu_q
            dapad[rows, klanes] = (dk + w_c * amat) * scale_k * dsilu_k

        chains = []
        for s in reversed(range(per_step)):
            gpre = g_ref[s * CHUNK:(s + 1) * CHUNK, :] + bg_ref[...]
            forms[s] = (gpre,) + _gate_forms(gpre)
            for h in range(N_HEADS):
                if s == per_step - 1:
                    d_state[s][h] = (dc_scr[h], dn_scr[h:h + 1, :])
                chains.append(head(s, h))
        _in_lockstep(chains)

        for s in range(per_step):
            rows = slice(s * CHUNK, (s + 1) * CHUNK)
            gpre, utri = forms[s][0], forms[s][2]
            cs_t = sum(col_g_rows[s][1:], col_g_rows[s][0]).T
            dig_all = sum(dig_parts[s][1:], dig_parts[s][0]) + cs_t
            db_cols = sum(db_parts[s][1:], db_parts[s][0])
            shifted = jnp.zeros((CHUNK, 128), F32)
            for h in range(N_HEADS):
                shifted = shifted + jnp.where(lane == N_HEADS + h, cs_t[:, h:h + 1], 0.0)
            dlf = _dot_f32(utri, db_cols - shifted)
            dgates = dig_all + dlf * _sigmoid(-gpre)
            dp_ref[rows, N_MAIN:N_MAIN + 128] = dgates.astype(BF16)
            dbg_ref[...] += jnp.sum(dgates, axis=0, keepdims=True)
        dp_ref[:, N_MAIN + 128:N_PAD] = jnp.zeros((blk, N_PAD - N_MAIN - 128), BF16)

        da_pad = dapad[...]
        da = da_pad[0:blk, :]
        dcb_ref[...] += jnp.sum(da, axis=0, keepdims=True)
        x = qk_ref[...]
        dx = jnp.zeros((blk, 1024), F32)
        for j in range(CONV_WIDTH):
            da_j = _rows_ahead(da_pad, CONV_WIDTH - 1 - j)[0:blk, :]
            dcw_ref[j:j + 1, :] += jnp.sum(da_j * x, axis=0, keepdims=True)
            dx = dx + cw_ref[j:j + 1, :] * da_j
        dp_ref[:, 1024:2048] = dx.astype(BF16)
        dapad[blk:blk + CONV_HALO, :] = dapad[0:CONV_HALO, :]

    bmap = lambda i: n_blocks - 1 - i
    wide = pl.BlockSpec((blk, 1024), lambda i: (bmap(i), 0))
    state = pl.BlockSpec((per_step, 8, 128), lambda i: (bmap(i), 0, 0))
    in_specs = [
        pl.BlockSpec((blk, 512), lambda i: (bmap(i), 1)),
        pl.BlockSpec((blk, 1024), lambda i: (bmap(i), 1)),
        pl.BlockSpec((blk, 512), lambda i: (bmap(i), 4)),
        pl.BlockSpec((blk, 512), lambda i: (bmap(i), 5)),
        pl.BlockSpec((blk, 512), lambda i: (bmap(i), 6)),
        pl.BlockSpec((blk, 128), lambda i: (bmap(i), 0)),
        wide, wide,
        pl.BlockSpec((blk, D_POOL), lambda i: (bmap(i), 0)),
        pl.BlockSpec((per_step, N_HEADS, HEAD_DIM, HEAD_DIM), lambda i: (bmap(i), 0, 0, 0)),
        state, state,
        pl.BlockSpec((1, 8, 128), lambda i: (jnp.minimum((bmap(i) + 1) * per_step, n_chunks - 1), 0, 0)),
        _full((1, 128)), _full((8, 1024)), _full((4, 128, 128)), _full((1, 512)), _full((1, 512))]
    return pl.pallas_call(
        body, name="mix_bwd", grid=(n_blocks,),
        out_shape=(jax.ShapeDtypeStruct((seq, N_PAD), BF16), jax.ShapeDtypeStruct((8, 1024), F32),
                   jax.ShapeDtypeStruct((1, 1024), F32), jax.ShapeDtypeStruct((4, 128, 128), F32),
                   jax.ShapeDtypeStruct((1, 512), F32), jax.ShapeDtypeStruct((1, 512), F32),
                   jax.ShapeDtypeStruct((1, 128), F32)),
        in_specs=in_specs,
        out_specs=(pl.BlockSpec((blk, N_PAD), lambda i: (bmap(i), 0)), _full((8, 1024)), _full((1, 1024)),
                   _full((4, 128, 128)), _full((1, 512)), _full((1, 512)), _full((1, 128))),
        scratch_shapes=[pltpu.VMEM((N_HEADS, HEAD_DIM, HEAD_DIM), F32), pltpu.VMEM((8, 128), F32),
                        pltpu.VMEM((blk + CONV_HALO, 1024), F32), pltpu.VMEM((blk + POOL_HALO, D_POOL), F32)],
        compiler_params=_params(("arbitrary",)),
    )(proj, proj, proj, proj, proj, gates, dmix, conv_a, pooled, cst, nst, mst, mst, bg_pad, conv_w8,
      w_pool, ls_pool, mh_g)


def _bwd_in(dproj, w_in_t, x, dx2, norm_g, scale):
    seq = x.shape[0]
    tm = min(512, seq)
    sub = min(256, tm)

    def body(dp_ref, wt_ref, x_ref, dx2_ref, ng_ref, sc_ref, gx_ref, dsh_ref, dsc_ref, dng_ref):
        @pl.when(pl.program_id(0) == 0)
        def _():
            dsh_ref[...] = jnp.zeros_like(dsh_ref)
            dsc_ref[...] = jnp.zeros_like(dsc_ref)
            dng_ref[...] = jnp.zeros_like(dng_ref)

        ng = ng_ref[...]
        one_sc = 1.0 + sc_ref[...]

        def chain(n):
            rows = slice(n * sub, (n + 1) * sub)
            dh = _dot(dp_ref[rows, :], wt_ref[...])
            yield
            xt = x_ref[rows, :]
            r = lax.rsqrt(jnp.mean(xt * xt, axis=-1, keepdims=True) + EPS)
            xn = xt * r
            dsh_ref[...] += jnp.sum(dh, axis=0, keepdims=True)
            dhxn_sum = jnp.sum(dh * xn, axis=0, keepdims=True)
            dsc_ref[...] += dhxn_sum * ng
            dng_ref[...] += dhxn_sum * one_sc
            dxn = dh * (ng * one_sc)
            gx_ref[rows, :] = r * (dxn - xn * jnp.mean(dxn * xn, axis=-1, keepdims=True)) + dx2_ref[rows, :]

        _in_lockstep(chain(n) for n in range(tm // sub))

    tile = pl.BlockSpec((tm, D_MODEL), lambda i: (i, 0))
    vec = _full((1, D_MODEL))
    return pl.pallas_call(
        body, name="bwd_in", grid=(seq // tm,),
        out_shape=(jax.ShapeDtypeStruct((seq, D_MODEL), F32),) + (jax.ShapeDtypeStruct((1, D_MODEL), F32),) * 3,
        in_specs=[pl.BlockSpec((tm, N_PAD), lambda i: (i, 0)), _full((N_PAD, D_MODEL)), tile, tile, vec, vec],
        out_specs=(tile, vec, vec, vec),
        compiler_params=_params(("arbitrary",)),
    )(dproj, w_in_t, x, dx2, norm_g, scale)


def _dw_in(h_b, dproj):
    seq = h_b.shape[0]
    tk = min(4096, seq)
    tn = 768
    n_t = seq // tk

    def body(h_ref, dp_ref, dwt_ref, acc):
        t = pl.program_id(1)

        @pl.when(t == 0)
        def _():
            acc[...] = jnp.zeros_like(acc)

        acc[...] += _dot_tn(dp_ref[...], h_ref[...])

        @pl.when(t == n_t - 1)
        def _():
            dwt_ref[...] = acc[...].astype(BF16)

    return pl.pallas_call(
        body, name="dw_in", grid=(N_PAD // tn, n_t),
        out_shape=jax.ShapeDtypeStruct((N_PAD, D_MODEL), BF16),
        in_specs=[pl.BlockSpec((tk, D_MODEL), lambda j, t: (t, 0)), pl.BlockSpec((tk, tn), lambda j, t: (t, j))],
        out_specs=pl.BlockSpec((tn, D_MODEL), lambda j, t: (j, 0)),
        scratch_shapes=[pltpu.VMEM((tn, D_MODEL), F32)],
        compiler_params=_params(("arbitrary", "arbitrary")),
    )(h_b, dproj)


def _adam_update(g, w, m, v, g_ref, d_ref, m_ref, v_ref):
    mn = ADAM_B1 * m + (1.0 - ADAM_B1) * g
    vn = ADAM_B2 * v + (1.0 - ADAM_B2) * (g * g)
    m_hat = mn / (1.0 - ADAM_B1 ** ADAM_STEP)
    v_hat = vn / (1.0 - ADAM_B2 ** ADAM_STEP)
    g_ref[...] = g
    d_ref[...] = -ADAM_LR * (m_hat / (jnp.sqrt(v_hat) + ADAM_EPS) + ADAM_WD * w)
    m_ref[...] = mn
    v_ref[...] = vn


def _adam_sum(name, parts, w, m, v, row_tile, col_tile=None):
    rows, cols = w.shape
    col_tile = cols if col_tile is None else col_tile
    n_parts = parts.shape[0]

    def body(p_ref, w_ref, m_ref, v_ref, g_out, d_out, m_out, v_out):
        g = p_ref[0].astype(F32)
        for j in range(1, n_parts):
            g = g + p_ref[j].astype(F32)
        _adam_update(g, w_ref[...], m_ref[...], v_ref[...], g_out, d_out, m_out, v_out)

    tile = pl.BlockSpec((row_tile, col_tile), lambda i, j: (i, j))
    return pl.pallas_call(
        body, name=name, grid=(rows // row_tile, cols // col_tile),
        out_shape=(jax.ShapeDtypeStruct((rows, cols), F32),) * 4,
        in_specs=[pl.BlockSpec((n_parts, row_tile, col_tile), lambda i, j: (0, i, j)), tile, tile, tile],
        out_specs=(tile,) * 4,
        compiler_params=_params(("arbitrary", "arbitrary")),
    )(parts, w, m, v)


def _adam_sum_own(name, me, blocks, landed, w, m, v, row_tile, col_tile=None):
    rows, cols = w.shape
    col_tile = cols if col_tile is None else col_tile

    def body(me_ref, p_ref, own_ref, w_ref, m_ref, v_ref, g_out, d_out, m_out, v_out):
        g = jnp.zeros((row_tile, col_tile), F32)
        for j in range(N_DEV):
            g = g + jnp.where(me_ref[0] == j, own_ref[0], p_ref[j]).astype(F32)
        _adam_update(g, w_ref[...], m_ref[...], v_ref[...], g_out, d_out, m_out, v_out)

    tile = pl.BlockSpec((row_tile, col_tile), lambda i, j, me_ref: (i, j))
    return pl.pallas_call(
        body, name=name, out_shape=(jax.ShapeDtypeStruct((rows, cols), F32),) * 4,
        grid_spec=pltpu.PrefetchScalarGridSpec(
            num_scalar_prefetch=1, grid=(rows // row_tile, cols // col_tile),
            in_specs=[pl.BlockSpec((N_DEV, row_tile, col_tile), lambda i, j, me_ref: (0, i, j)),
                      pl.BlockSpec((1, row_tile, col_tile), lambda i, j, me_ref: (me_ref[0], i, j)),
                      tile, tile, tile],
            out_specs=(tile,) * 4),
        compiler_params=_params(("arbitrary", "arbitrary")),
    )(me, landed, blocks, w, m, v)


def _adam_ada(sc_all16, dmod_blk16, w, m, v):
    rows, cols = w.shape

    def body(sc_ref, dm_ref, w_ref, m_ref, v_ref, g_out, d_out, m_out, v_out):
        g = _dot_tn(sc_ref[...].astype(BF16), dm_ref[...].astype(BF16))
        _adam_update(g, w_ref[...], m_ref[...], v_ref[...], g_out, d_out, m_out, v_out)

    return pl.pallas_call(
        body, name="adam_w_ada", grid=(1,),
        out_shape=(jax.ShapeDtypeStruct((rows, cols), F32),) * 4,
        in_specs=[_full(sc_all16.shape), _full(dmod_blk16.shape)] + [_full((rows, cols))] * 3,
        out_specs=(_full((rows, cols)),) * 4,
        compiler_params=_params(("arbitrary",)),
    )(sc_all16, dmod_blk16, w, m, v)


def _adam_small(parts, loss_parts, w, m, v):
    names = list(w)
    n = len(names)

    def body(*refs):
        p_refs, loss_ref = refs[:n], refs[n]
        w_refs, m_refs, v_refs = (refs[n + 1 + k * n:n + 1 + (k + 1) * n] for k in range(3))
        outs = refs[3 * n + n + 1:]
        for a in range(n):
            g = p_refs[a][0]
            for j in range(1, N_DEV):
                g = g + p_refs[a][j]
            width = w_refs[a].shape[-1]
            if g.shape[-1] != width:
                g = g[..., 0:width]
            _adam_update(g, w_refs[a][...], m_refs[a][...], v_refs[a][...], *outs[4 * a:4 * a + 4])
        total = loss_ref[0]
        for j in range(1, N_DEV):
            total = total + loss_ref[j]
        outs[4 * n][...] = total

    args = [parts[k] for k in names] + [loss_parts] + [d[k] for d in (w, m, v) for k in names]
    out_shape = tuple(jax.ShapeDtypeStruct(w[k].shape, F32) for k in names for _ in range(4))
    out_shape += (jax.ShapeDtypeStruct(loss_parts.shape[1:], F32),)
    out = pl.pallas_call(
        body, name="adam_small", grid=(1,), out_shape=out_shape,
        in_specs=[_full(a.shape) for a in args], out_specs=tuple(_full(s.shape) for s in out_shape),
        compiler_params=_params(("arbitrary",)),
    )(*args)
    return {k: out[4 * a:4 * a + 4] for a, k in enumerate(names)}, out[4 * n]


def _local_step(x2, tgt2, shift, scale, gate, norm_g, w_in_t, w_out_b, conv_w, conv_b, w_pool, ls_pool,
                mh_norm_g, b_gates, final_g, send_dw_out=None, send_dw_in=None):
    bg_pad = jnp.pad(b_gates, ((0, 0), (0, 128 - b_gates.shape[1])))
    conv_w8 = jnp.pad(conv_w, ((0, 8 - CONV_WIDTH), (0, 0)))
    fg = final_g.reshape(1, D_MODEL)

    proj, gates, h_b = _fwd_proj(x2, norm_g, scale, shift, w_in_t)
    mix, cst, nst, mst, conv_a, pooled = _mix_fwd(proj, gates, bg_pad, conv_w8, conv_b, w_pool, ls_pool, mh_norm_g)
    dx2, dmix, dwo, dgate, dfg, loss = _out_fwd_bwd(mix, x2, tgt2, w_out_b, gate, fg)
    if send_dw_out is not None:
        bg_pad = bg_pad + send_dw_out(dwo)
    dproj, dcw8, dcb, dwp, dls, dmhg, dbg = _mix_bwd(proj, gates, dmix, conv_a, pooled, cst, nst, mst, bg_pad,
                                                      conv_w8, w_pool, ls_pool, mh_norm_g)
    dw_in_t = _dw_in(h_b, dproj)[:N_IN]
    ng_in = norm_g
    if send_dw_in is not None:
        ng_in = norm_g + send_dw_in(dw_in_t, dcw8[:CONV_WIDTH])
    gx, dsh, dsc, dng = _bwd_in(dproj, w_in_t, x2, dx2, ng_in, scale)
    return dict(loss=loss, grad_x=gx, dw_in_t=dw_in_t, dw_out=dwo, dconv_w=dcw8[:CONV_WIDTH], conv_b=dcb,
                w_pool=dwp, ls_pool=dls, mh_norm_g=dmhg, b_gates=dbg, final_g=dfg, norm_g=dng,
                dmod=jnp.concatenate([dsh, dsc, dgate], axis=1))


def kernel(x, c, norm_g, w_ada, b_ada, w_in, b_gates, conv_w, conv_b, w_pool, ls_pool, mh_norm_g, w_out, final_g, loss_target, m_norm_g, m_w_ada, m_b_ada, m_w_in, m_b_gates, m_conv_w, m_conv_b, m_w_pool, m_ls_pool, m_mh_norm_g, m_w_out, m_final_g, v_norm_g, v_w_ada, v_b_ada, v_w_in, v_b_gates, v_conv_w, v_conv_b, v_w_pool, v_ls_pool, v_mh_norm_g, v_w_out, v_final_g):
    seq = x.shape[1]
    me = 4 * lax.axis_index("x") + 2 * lax.axis_index("y") + lax.axis_index("c")

    b_ada_blk = lax.dynamic_slice(b_ada, (0, me * ADA_SHARD), (1, ADA_SHARD))
    g_in, g_out, g_cw, mod_all, sc_all16 = _gather_weights_and_mod(
        (w_in[0].astype(BF16).T, w_out[0].astype(BF16), conv_w[0]), c, w_ada[0], b_ada_blk)
    w_in_t = jnp.pad(g_in.reshape(N_IN, D_MODEL), ((0, N_PAD - N_IN), (0, 0)))
    w_out_b = g_out.reshape(D_MODEL, D_MODEL)
    conv_w_full = jnp.transpose(g_cw, (1, 0, 2)).reshape(CONV_WIDTH, 2 * D_MLSTM)
    mod = lax.dynamic_index_in_dim(mod_all, me, axis=1, keepdims=False).reshape(1, 3 * D_MODEL)
    shift, scale, gate = mod[:, :D_MODEL], mod[:, D_MODEL:2 * D_MODEL], mod[:, 2 * D_MODEL:]

    flights = {}

    def send_dw_out(dwo):
        blocks = dwo.reshape(N_DEV, D_MODEL // N_DEV, D_MODEL)
        flights["out"], token = _scatter_start("send_dw_out", (blocks,))
        return token

    def send_dw_in(dw_in_t, dcw):
        blocks = dw_in_t.reshape(N_DEV, N_SHARD, D_MODEL)
        dcw_blocks = jnp.transpose(dcw.reshape(CONV_WIDTH, N_DEV, 128), (1, 0, 2))
        flights["in"], token = _scatter_start("send_dw_in", (blocks, dcw_blocks))
        return token

    r = _local_step(x[0], loss_target[0], shift, scale, gate, norm_g, w_in_t, w_out_b, conv_w_full, conv_b,
                    w_pool[0], ls_pool, mh_norm_g, b_gates, final_g, send_dw_out, send_dw_in)

    small_names = ("norm_g", "b_ada", "b_gates", "conv_b", "w_pool", "ls_pool", "mh_norm_g", "final_g")
    small_grads = dict(norm_g=r["norm_g"], b_ada=r["dmod"], b_gates=r["b_gates"], conv_b=r["conv_b"],
                       w_pool=r["w_pool"], ls_pool=r["ls_pool"], mh_norm_g=r["mh_norm_g"], final_g=r["final_g"])
    flights["small"], _ = _scatter_start(
        "send_small", (r["loss"],) + tuple(small_grads[k] for k in small_names), whole=True)
    (s_in, p_in), p_cw = _scatter_wait("recv_dw_in", flights["in"],
                                       flights["small"][2 + 1 + small_names.index("w_pool")], fill=(False, True))
    ((s_out, p_out),) = _scatter_wait("recv_dw_out", flights["out"], p_cw, fill=(False,))

    me1 = me.astype(jnp.int32).reshape(1)
    in_t = _adam_sum_own("adam_w_in", me1, s_in, p_in, w_in[0].T, m_w_in[0].T, v_w_in[0].T, N_SHARD, 256)
    gi, di, mi, vi = (o.T for o in in_t)
    go, do_, mo, vo = _adam_sum_own("adam_w_out", me1, s_out, p_out, w_out[0], m_w_out[0], v_w_out[0], 128)
    gc, dc, mc, vc = _adam_sum("adam_conv_w", p_cw, conv_w[0], m_conv_w[0], v_conv_w[0], CONV_WIDTH)
    gathered = _scatter_wait("recv_small", flights["small"], go, whole=True)
    p_loss, p_small = gathered[0], dict(zip(small_names, gathered[1:]))

    def plain(norm_g_, b_ada_, b_gates_, conv_b_, w_pool_, ls_pool_, mh_norm_g_, final_g_):
        return dict(norm_g=norm_g_, b_ada=b_ada_, b_gates=b_gates_, conv_b=conv_b_, w_pool=w_pool_[0],
                    ls_pool=ls_pool_, mh_norm_g=mh_norm_g_, final_g=final_g_.reshape(1, D_MODEL))

    small, loss_row = _adam_small(
        p_small, p_loss,
        plain(norm_g, b_ada, b_gates, conv_b, w_pool, ls_pool, mh_norm_g, final_g),
        plain(m_norm_g, m_b_ada, m_b_gates, m_conv_b, m_w_pool, m_ls_pool, m_mh_norm_g, m_final_g),
        plain(v_norm_g, v_b_ada, v_b_gates, v_conv_b, v_w_pool, v_ls_pool, v_mh_norm_g, v_final_g))

    dmod_all = p_small["b_ada"].reshape(N_DEV, 3 * D_MODEL)
    dmod_blk16 = jnp.pad(lax.dynamic_slice(dmod_all, (0, me * ADA_SHARD), (N_DEV, ADA_SHARD)), ((0, 8), (0, 0)))
    ga, da, ma, va = _adam_ada(sc_all16, dmod_blk16, w_ada[0], m_w_ada[0], v_w_ada[0])

    names = ("norm_g", "w_ada", "b_ada", "w_in", "b_gates", "conv_w", "conv_b", "w_pool", "ls_pool", "mh_norm_g",
             "w_out", "final_g")
    shapes = dict(norm_g=norm_g.shape, b_ada=b_ada.shape, b_gates=b_gates.shape, conv_b=conv_b.shape,
                  w_pool=w_pool.shape, ls_pool=ls_pool.shape, mh_norm_g=mh_norm_g.shape, final_g=final_g.shape)
    sharded = dict(w_ada=(ga, da, ma, va), w_in=(gi, di, mi, vi), conv_w=(gc, dc, mc, vc), w_out=(go, do_, mo, vo))
    outs = []
    for kind in range(4):
        for nm in names:
            if nm in sharded:
                outs.append(sharded[nm][kind][None])
            else:
                outs.append(small[nm][kind].reshape(shapes[nm]))
    loss = loss_row[0, 0]
    grad_x = r["grad_x"].reshape(1, seq, D_MODEL)
    return (loss, grad_x, *outs)
```

```python
import jax
import jax.numpy as jnp
from jax import lax
from jax.experimental import pallas as pl
from jax.experimental.pallas import tpu as pltpu

F32 = jnp.float32
BF16 = jnp.bfloat16

D_MODEL = 1024
D_POOL = 512
D_MLSTM = 512
N_HEADS = 4
HEAD_DIM = 128
CHUNK = 128
POOL_WINDOWS = (2, 4, 8, 16)
POOL_GROUP_DIM = 128
CONV_WIDTH = 4
EPS = 1e-6
N_MAIN = 3584
N_IN = 3592
N_PAD = 3840
N_SHARD = N_IN // 8
ADA_SHARD = 3 * D_MODEL // 8
N_DEV = 8
CONV_HALO = 8
POOL_HALO = 16
NEG_BIG = -1e30
VMEM_LIMIT_BYTES = 56 * 1024 * 1024

ADAM_LR = 0.001
ADAM_B1 = 0.9
ADAM_B2 = 0.999
ADAM_EPS = 1e-08
ADAM_WD = 0.01
ADAM_STEP = 10

def _dot(a, b):
    return jnp.dot(a, b, preferred_element_type=F32)


def _dot_nt(a, b):
    return lax.dot_general(a, b, (((1,), (1,)), ((), ())), preferred_element_type=F32)


def _dot_tn(a, b):
    return lax.dot_general(a, b, (((0,), (0,)), ((), ())), preferred_element_type=F32)


def _dot_f32(a, b):
    return jnp.dot(a, b, precision=lax.Precision.HIGHEST, preferred_element_type=F32)


def _row_mean_mxu(x):
    return _dot(x.astype(BF16), jnp.full((HEAD_DIM, HEAD_DIM), 1.0 / HEAD_DIM, BF16))


def _sigmoid(x):
    return jax.nn.sigmoid(x)


def _log_sigmoid(x):
    return jnp.minimum(x, 0.0) - jnp.log1p(jnp.exp(-jnp.abs(x)))


def _params(sem):
    return pltpu.CompilerParams(dimension_semantics=sem, vmem_limit_bytes=VMEM_LIMIT_BYTES)


def _full(shape):
    n = len(shape)
    return pl.BlockSpec(shape, lambda *_: (0,) * n)


def _mesh_pos():
    return lax.axis_index("x"), lax.axis_index("y"), lax.axis_index("c")


def _peer(k):
    x, y, c = _mesh_pos()
    px = 1 - x if (k >> 2) & 1 else x
    py = 1 - y if (k >> 1) & 1 else y
    pc = 1 - c if k & 1 else c
    return (px, py, pc), 4 * px + 2 * py + pc


def _remote(src, dst, send_sem, recv_sem, to):
    return pltpu.make_async_remote_copy(src_ref=src, dst_ref=dst, send_sem=send_sem, recv_sem=recv_sem, device_id=to,
                                        device_id_type=pl.DeviceIdType.MESH)


def _two_level_gather(src, dst, send_sems, recv_sems, local_sems):
    n = len(src)
    x, y, c = _mesh_pos()
    me = 4 * x + 2 * y + c
    sibling = (x, y, 1 - c)
    south = c == 0
    near = (jnp.where(south, 1 - x, x), jnp.where(south, y, 1 - y))
    far = (jnp.where(south, x, 1 - x), jnp.where(south, 1 - y, y))
    diag = (1 - x, 1 - y)

    def block_of(chip, core):
        return 4 * chip[0] + 2 * chip[1] + core

    def copy(a, k, block, to, own=False):
        return _remote(src[a] if own else dst[a].at[block], dst[a].at[block], send_sems.at[a, k], recv_sems.at[a, k], to)

    local = [pltpu.make_async_copy(src[a], dst[a].at[me], local_sems.at[a]) for a in range(n)]
    sent = [copy(a, 0, me, sibling, True) for a in range(n)]
    sent += [copy(a, 1, me, (*near, c), True) for a in range(n)]
    sent += [copy(a, 2, me, (*far, c), True) for a in range(n)]
    for cp in local + sent:
        cp.start()
    yield
    for a in range(n):
        copy(a, 1, block_of(near, c), sibling).wait_recv()
        sent += [copy(a, 3, block_of(near, c), (*far, c)), copy(a, 4, block_of(near, c), sibling)]
        sent[-2].start()
        sent[-1].start()
    for k, chip in ((2, far), (3, diag)):
        for a in range(n):
            copy(a, k, block_of(chip, c), sibling).wait_recv()
            sent.append(copy(a, 3 + k, block_of(chip, c), sibling))
            sent[-1].start()
    for k, chip in ((0, (x, y)), (4, far), (5, near), (6, diag)):
        for a in range(n):
            copy(a, k, block_of(chip, 1 - c), sibling).wait_recv()
    for cp in sent:
        cp.wait_send()
    for cp in local:
        cp.wait()


GATHER_COPIES = 7


def _swap_with_all(buf, send_sems, recv_sems):
    x, y, c = _mesh_pos()
    me = 4 * x + 2 * y + c
    copies = [_remote(buf.at[me], buf.at[me], send_sems.at[k - 1], recv_sems.at[k - 1], _peer(k)[0])
              for k in range(1, N_DEV)]
    for cp in copies:
        cp.start()
    for cp in copies:
        cp.wait()


def _gather_weights_and_mod(shards, c_row, w_ada_blk, b_ada_blk):
    n = len(shards)

    def body(*refs):
        src, (c_ref, w_ref, b_ref) = refs[:n], refs[n:n + 3]
        dst, (mod_ref, sc_ref) = refs[n + 3:2 * n + 3], refs[2 * n + 3:2 * n + 5]
        c_all, g_send, g_recv, g_local, c_send, c_recv, m_send, m_recv = refs[2 * n + 5:]
        x, y, c = _mesh_pos()
        me = 4 * x + 2 * y + c
        gather = _two_level_gather(src, dst, g_send, g_recv, g_local)
        next(gather)
        c_all[me] = c_ref[...]
        _swap_with_all(c_all, c_send, c_recv)
        cv = jnp.concatenate([c_all[j] for j in range(N_DEV)] + [jnp.zeros((N_DEV, D_MODEL), F32)], axis=0)
        sc = cv * _sigmoid(cv)
        sc_ref[...] = sc
        blk = _dot(sc.astype(BF16), w_ref[...].astype(BF16)) + b_ref[...]
        mod_ref[me] = blk[0:N_DEV, :]
        _swap_with_all(mod_ref, m_send, m_recv)
        for _ in gather:
            pass

    hbm = pl.BlockSpec(memory_space=pltpu.HBM)
    vmem = pl.BlockSpec(memory_space=pltpu.VMEM)
    peers = pltpu.SemaphoreType.DMA((N_DEV - 1,))
    return pl.pallas_call(
        body, name="gather_weights",
        out_shape=tuple(jax.ShapeDtypeStruct((N_DEV,) + s.shape, s.dtype) for s in shards)
        + (jax.ShapeDtypeStruct((N_DEV, N_DEV, ADA_SHARD), F32), jax.ShapeDtypeStruct((2 * N_DEV, D_MODEL), F32)),
        in_specs=[hbm] * n + [vmem] * 3, out_specs=tuple([hbm] * n + [vmem] * 2),
        scratch_shapes=[pltpu.VMEM((N_DEV, 1, D_MODEL), F32),
                        pltpu.SemaphoreType.DMA((n, GATHER_COPIES)), pltpu.SemaphoreType.DMA((n, GATHER_COPIES)),
                        pltpu.SemaphoreType.DMA((n,)), peers, peers, peers, peers],
    )(*shards, c_row, w_ada_blk, b_ada_blk)


def _scatter_copies(src, land, send_sems, recv_sems, whole=False):
    x, y, c = _mesh_pos()
    me = 4 * x + 2 * y + c
    copies = []
    for k in range(1, N_DEV):
        peer, p = _peer(k)
        for a in range(len(src)):
            i = a * (N_DEV - 1) + k - 1
            copies.append(_remote(src[a] if whole else src[a].at[p], land[a].at[me], send_sems.at[i],
                                  recv_sems.at[i], peer))
    return copies


def _scatter_start(name, blocks, whole=False):
    n = len(blocks)

    def body(*refs):
        src, land = refs[:n], refs[n:2 * n]
        send_sems, recv_sems = refs[2 * n], refs[2 * n + 1]
        token_ref = refs[-1]
        for cp in _scatter_copies(src, land, send_sems, recv_sems, whole):
            cp.start()
        token_ref[...] = jnp.zeros_like(token_ref)

    hbm = pl.BlockSpec(memory_space=pltpu.HBM)
    sem = pl.BlockSpec(memory_space=pltpu.SEMAPHORE)
    landing = [((N_DEV,) + b.shape if whole else b.shape, b.dtype) for b in blocks]
    through = tuple(pltpu.HBM(b.shape, b.dtype) for b in blocks) + tuple(pltpu.HBM(s, d) for s, d in landing)
    args = [pltpu.with_memory_space_constraint(b, pltpu.HBM) for b in blocks]
    args += [pltpu.with_memory_space_constraint(lax.empty(s, d), pltpu.HBM) for s, d in landing]
    out = pl.pallas_call(
        body, name=name,
        out_shape=(pltpu.SemaphoreType.DMA((n * (N_DEV - 1),)),) * 2 + through
        + (jax.ShapeDtypeStruct((8, 128), F32),),
        in_specs=[hbm] * (2 * n),
        out_specs=(sem, sem) + (hbm,) * (2 * n) + (pl.BlockSpec(memory_space=pltpu.VMEM),),
        input_output_aliases={i: 2 + i for i in range(2 * n)},
        compiler_params=pltpu.CompilerParams(has_side_effects=pltpu.SideEffectType.DATAFLOW_SIDE_EFFECTING),
    )(*args)
    return out[:-1], out[-1][0:1, 0:1]


def _scatter_wait(name, state, after, whole=False, fill=None):
    n = (len(state) - 2) // 2
    send_sems, recv_sems = state[0], state[1]
    src, land = state[2:2 + n], state[2 + n:]

    def body(*refs):
        src_r, land_r = refs[:n], refs[n:2 * n]
        for cp in _scatter_copies(src_r, land_r, refs[2 * n], refs[2 * n + 1], whole):
            cp.wait_send()
            cp.wait_recv()

    hbm = pl.BlockSpec(memory_space=pltpu.HBM)
    sem = pl.BlockSpec(memory_space=pltpu.SEMAPHORE)
    out = pl.pallas_call(
        body, name=name,
        out_shape=tuple(pltpu.HBM(b.shape, b.dtype) for b in src + land),
        in_specs=[hbm] * (2 * n) + [sem, sem, pl.BlockSpec(memory_space=pl.ANY)],
        out_specs=(hbm,) * (2 * n),
        input_output_aliases={i: i for i in range(2 * n)},
        compiler_params=pltpu.CompilerParams(has_side_effects=pltpu.SideEffectType.DATAFLOW_SIDE_EFFECTING),
    )(*src, *land, send_sems, recv_sems, after)
    me = 4 * lax.axis_index("x") + 2 * lax.axis_index("y") + lax.axis_index("c")
    landed = []
    for a in range(n):
        if fill is not None and not fill[a]:
            landed.append((out[a], out[n + a]))
            continue
        own = out[a][None] if whole else lax.dynamic_index_in_dim(out[a], me, axis=0, keepdims=True)
        landed.append(lax.dynamic_update_slice_in_dim(out[n + a], own, me, axis=0))
    return landed


def _fwd_proj(x, norm_g, scale, shift, w_in_t):
    seq = x.shape[0]
    tm = min(512, seq)
    sub = min(256, tm)
    tn = 512

    def body(x_ref, ng_ref, sc_ref, sh_ref, wt_ref, proj_ref, gates_ref, h_ref):
        def chain(n):
            for _ in range(n):
                yield
            rows = slice(n * sub, (n + 1) * sub)
            xt = x_ref[rows, :]
            r = lax.rsqrt(jnp.mean(xt * xt, axis=-1, keepdims=True) + EPS)
            h = ((xt * r) * ng_ref[...]) * (1.0 + sc_ref[...]) + sh_ref[...]
            hb = h.astype(BF16)
            h_ref[rows, :] = hb
            yield
            gates_ref[rows, :] = _dot_nt(hb, wt_ref[N_MAIN:N_MAIN + 128, :])
            for j in range(N_MAIN // tn):
                proj_ref[rows, j * tn:(j + 1) * tn] = _dot_nt(hb, wt_ref[j * tn:(j + 1) * tn, :])

        _in_lockstep(chain(n) for n in range(tm // sub))

    vec = _full((1, D_MODEL))
    tile = pl.BlockSpec((tm, D_MODEL), lambda i: (i, 0))
    return pl.pallas_call(
        body, name="fwd_proj", grid=(seq // tm,),
        out_shape=(jax.ShapeDtypeStruct((seq, N_MAIN), F32), jax.ShapeDtypeStruct((seq, 128), F32),
                   jax.ShapeDtypeStruct((seq, D_MODEL), BF16)),
        in_specs=[tile, vec, vec, vec, _full((N_PAD, D_MODEL))],
        out_specs=(pl.BlockSpec((tm, N_MAIN), lambda i: (i, 0)), pl.BlockSpec((tm, 128), lambda i: (i, 0)), tile),
        compiler_params=_params(("arbitrary",)),
    )(x, norm_g, scale, shift, w_in_t)


def _gate_forms(gpre):
    r = lax.broadcasted_iota(jnp.int32, (CHUNK, CHUNK), 0)
    c = lax.broadcasted_iota(jnp.int32, (CHUNK, CHUNK), 1)
    causal = c <= r
    ltri = jnp.where(causal, 1.0, 0.0).astype(F32)
    utri = jnp.where(r <= c, 1.0, 0.0).astype(F32)
    bcol = _dot_f32(ltri, _log_sigmoid(gpre))
    gt8 = gpre.T[0:8, :]
    brow = _dot_f32(_log_sigmoid(gt8), utri)
    return causal, utri, bcol, gt8, brow


def _in_lockstep(stages):
    alive = list(stages)
    while alive:
        still = []
        for g in alive:
            try:
                next(g)
                still.append(g)
            except StopIteration:
                pass
        alive = still


def _head_fwd(qh, kh, vh, bc, br, igr, m_prev, c_h, n_row, causal):
    qb, kb, vb, cb = qh.astype(BF16), kh.astype(BF16), vh.astype(BF16), c_h.astype(BF16)
    qk = _dot_nt(qb, kb)
    cq = _dot_nt(qb, cb)
    nq = _dot_nt(qb, jnp.broadcast_to(n_row.astype(BF16), (HEAD_DIM, HEAD_DIM)))
    yield
    dlog = jnp.where(causal, bc - br + igr, NEG_BIG)
    inter_log = bc + m_prev
    m_t = jnp.maximum(inter_log, jnp.max(dlog, axis=-1, keepdims=True))
    yield
    dmat = jnp.exp(dlog - m_t)
    inter = jnp.exp(inter_log - m_t)
    s = qk * dmat
    sv = _dot(s.astype(BF16), vb)
    yield
    den = jnp.sum(s, axis=-1, keepdims=True) + inter * nq
    emt = jnp.exp(-m_t)
    yield
    num = sv + inter * cq
    dn = jnp.maximum(jnp.abs(den), emt)
    hm = num / dn
    return dict(dmat=dmat, inter=inter, qb=qb, kb=kb, vb=vb, cb=cb, s=s, cq=cq, nq=nq, den=den, emt=emt,
                dn=dn, hm=hm)


def _state_weights(bc, igc, m_prev, m_new=None):
    last = lax.broadcasted_iota(jnp.int32, (CHUNK, 1), 0) == CHUNK - 1
    b_last = jnp.sum(jnp.where(last, bc, 0.0), axis=0, keepdims=True)
    wlog = b_last - bc + igc
    if m_new is None:
        m_new = jnp.maximum(b_last + m_prev, jnp.max(wlog, axis=0, keepdims=True))
    w_c = jnp.exp(wlog - m_new)
    decay = jnp.exp(b_last + m_prev - m_new)
    return w_c, decay, m_new, last


def _rows_back(x, k):
    return x if k == 0 else pltpu.roll(x, k, 0)


def _rows_ahead(x, k):
    return x if k == 0 else pltpu.roll(x, x.shape[0] - k, 0)


def _conv_taps(xpad):
    return [_rows_back(xpad, CONV_WIDTH - 1 - j)[CONV_HALO:, :] for j in range(CONV_WIDTH)]


def _conv_pre(taps, cw_ref, cb_ref):
    a = cb_ref[...]
    for j in range(CONV_WIDTH):
        a = a + cw_ref[j:j + 1, :] * taps[j]
    return a


def _window_sum(x, w, shift):
    k = 1
    while k < w:
        x = x + shift(x, k)
        k *= 2
    return x


def _pool_window_sum(upad_ref, g, w):
    lanes = slice(g * POOL_GROUP_DIM, (g + 1) * POOL_GROUP_DIM)
    return _window_sum(upad_ref[:, lanes], w, _rows_back)[POOL_HALO:, :]


def _pool_inv_count(row0, rows, w):
    pos = row0 + lax.broadcasted_iota(jnp.int32, (rows, 1), 0) + 1
    return 1.0 / jnp.minimum(pos, w).astype(F32)


FWD_CHUNKS = 4
BWD_CHUNKS = 4


def _mix_fwd(proj, gates, bg_pad, conv_w8, conv_b, w_pool, ls_pool, mh_g):
    seq = proj.shape[0]
    n_chunks = seq // CHUNK
    per_step = FWD_CHUNKS
    blk = per_step * CHUNK

    def body(uz_ref, qk_ref, v_ref, o_ref, zm_ref, uh_ref, qkh_ref, g_ref, bg_ref, cw_ref, cb_ref, wp_ref,
             ls_ref, mhg_ref, mix_ref, cst_ref, nst_ref, mst_ref, a_ref, pooled_ref, c_scr, n_scr, m_scr, xpad, upad):
        i = pl.program_id(0)

        @pl.when(i == 0)
        def _():
            c_scr[...] = jnp.zeros_like(c_scr)
            n_scr[...] = jnp.zeros_like(n_scr)
            m_scr[...] = jnp.zeros_like(m_scr)

        first = i == 0

        upad[0:POOL_HALO, :] = jnp.where(first, 0.0, uh_ref[...])
        upad[POOL_HALO:POOL_HALO + blk, :] = uz_ref[:, 0:D_POOL]
        for g, w in enumerate(POOL_WINDOWS):
            lanes = slice(g * POOL_GROUP_DIM, (g + 1) * POOL_GROUP_DIM)
            pooled = (_pool_window_sum(upad, g, w) * _pool_inv_count(i * blk, blk, w) - uz_ref[:, lanes]).astype(BF16)
            pooled_ref[:, lanes] = pooled
            y = _dot(pooled, wp_ref[g].astype(BF16)) * ls_ref[:, lanes]
            zp = uz_ref[:, D_POOL + g * POOL_GROUP_DIM:D_POOL + (g + 1) * POOL_GROUP_DIM]
            mix_ref[:, lanes] = (y * (zp * _sigmoid(zp))).astype(BF16)

        xpad[0:CONV_HALO, :] = jnp.where(first, 0.0, qkh_ref[...])
        xpad[CONV_HALO:CONV_HALO + blk, :] = qk_ref[...]
        a = _conv_pre(_conv_taps(xpad[...]), cw_ref, cb_ref)
        a_ref[...] = a
        qk = a * _sigmoid(a)

        def head(rows, h, qh, kh, vh, bc, br, igr, m_prev, c_h, n_row, causal):
            lanes = slice(h * HEAD_DIM, (h + 1) * HEAD_DIM)
            f = yield from _head_fwd(qh, kh, vh, bc, br, igr, m_prev, c_h, n_row, causal)
            yield
            hm = f["hm"]
            hn = hm * lax.rsqrt(_row_mean_mxu(hm * hm) + EPS) * mhg_ref[:, lanes]
            zm = zm_ref[rows, lanes]
            out = hn * _sigmoid(o_ref[rows, lanes]) * (zm * _sigmoid(zm))
            mix_ref[rows, D_POOL + h * HEAD_DIM:D_POOL + (h + 1) * HEAD_DIM] = out.astype(BF16)

        c_cur = [c_scr[h] for h in range(N_HEADS)]
        n_cur = [n_scr[h:h + 1, :] for h in range(N_HEADS)]
        m_cur = [m_scr[h:h + 1, 0:1] for h in range(N_HEADS)]
        chains = []
        for s in range(per_step):
            rows = slice(s * CHUNK, (s + 1) * CHUNK)
            gpre = g_ref[rows, :] + bg_ref[...]
            causal, _, bcol, gt8, brow = _gate_forms(gpre)
            nst_ref[s] = jnp.zeros((8, 128), F32)
            mst_ref[s] = jnp.zeros((8, 128), F32)
            for h in range(N_HEADS):
                lanes = slice(h * HEAD_DIM, (h + 1) * HEAD_DIM)
                cst_ref[s, h] = c_cur[h]
                nst_ref[s, h:h + 1, :] = n_cur[h]
                mst_ref[s, h:h + 1, :] = jnp.broadcast_to(m_cur[h], (1, 128))
                qh = qk[rows, lanes]
                kh = qk[rows, D_MLSTM + h * HEAD_DIM:D_MLSTM + (h + 1) * HEAD_DIM] * (HEAD_DIM ** -0.5)
                vh = v_ref[rows, lanes]
                bc = bcol[:, N_HEADS + h:N_HEADS + h + 1]
                br = brow[N_HEADS + h:N_HEADS + h + 1, :]
                igr = gt8[h:h + 1, :]
                igc = gpre[:, h:h + 1]
                chains.append(head(rows, h, qh, kh, vh, bc, br, igr, m_cur[h], c_cur[h], n_cur[h], causal))
                w_c, decay, m_new, _ = _state_weights(bc, igc, m_cur[h])
                c_cur[h] = decay * c_cur[h] + _dot_tn((vh * w_c).astype(BF16), kh.astype(BF16))
                n_cur[h] = decay * n_cur[h] + jnp.sum(w_c * kh, axis=0, keepdims=True)
                m_cur[h] = m_new
        for h in range(N_HEADS):
            c_scr[h] = c_cur[h]
            n_scr[h:h + 1, :] = n_cur[h]
            m_scr[h:h + 1, :] = jnp.broadcast_to(m_cur[h], (1, 128))
        _in_lockstep(chains)

    in_specs = [
        pl.BlockSpec((blk, 1024), lambda i: (i, 0)),
        pl.BlockSpec((blk, 1024), lambda i: (i, 1)),
        pl.BlockSpec((blk, 512), lambda i: (i, 4)),
        pl.BlockSpec((blk, 512), lambda i: (i, 5)),
        pl.BlockSpec((blk, 512), lambda i: (i, 6)),
        pl.BlockSpec((POOL_HALO, 512), lambda i: (jnp.maximum(i * (blk // POOL_HALO) - 1, 0), 0)),
        pl.BlockSpec((CONV_HALO, 1024), lambda i: (jnp.maximum(i * (blk // CONV_HALO) - 1, 0), 1)),
        pl.BlockSpec((blk, 128), lambda i: (i, 0)),
        _full((1, 128)), _full((8, 1024)), _full((1, 1024)), _full((4, 128, 128)), _full((1, 512)),
        _full((1, 512))]
    return pl.pallas_call(
        body, name="mix_fwd", grid=(n_chunks // per_step,),
        out_shape=(jax.ShapeDtypeStruct((seq, D_MODEL), BF16),
                   jax.ShapeDtypeStruct((n_chunks, N_HEADS, HEAD_DIM, HEAD_DIM), F32),
                   jax.ShapeDtypeStruct((n_chunks, 8, 128), F32),
                   jax.ShapeDtypeStruct((n_chunks, 8, 128), F32),
                   jax.ShapeDtypeStruct((seq, 2 * D_MLSTM), F32),
                   jax.ShapeDtypeStruct((seq, D_POOL), BF16)),
        in_specs=in_specs,
        out_specs=(pl.BlockSpec((blk, D_MODEL), lambda i: (i, 0)),
                   pl.BlockSpec((per_step, N_HEADS, HEAD_DIM, HEAD_DIM), lambda i: (i, 0, 0, 0)),
                   pl.BlockSpec((per_step, 8, 128), lambda i: (i, 0, 0)),
                   pl.BlockSpec((per_step, 8, 128), lambda i: (i, 0, 0)),
                   pl.BlockSpec((blk, 2 * D_MLSTM), lambda i: (i, 0)),
                   pl.BlockSpec((blk, D_POOL), lambda i: (i, 0))),
        scratch_shapes=[pltpu.VMEM((N_HEADS, HEAD_DIM, HEAD_DIM), F32), pltpu.VMEM((8, 128), F32),
                        pltpu.VMEM((8, 128), F32), pltpu.VMEM((CONV_HALO + blk, 1024), F32),
                        pltpu.VMEM((POOL_HALO + blk, D_POOL), F32)],
        compiler_params=_params(("arbitrary",)),
    )(proj, proj, proj, proj, proj, proj, proj, gates, bg_pad, conv_w8, conv_b, w_pool, ls_pool, mh_g)


def _out_fwd_bwd(mix, x, tgt, w_out_b, gate, final_g):
    seq = x.shape[0]
    tm = min(512, seq)
    sub = min(256, tm)

    def body(mix_ref, x_ref, t_ref, w_ref, gate_ref, fg_ref, dx2_ref, dmix_ref, dwo_ref, dgate_ref, dfg_ref,
             loss_ref, dwo_acc):
        @pl.when(pl.program_id(0) == 0)
        def _():
            dwo_acc[...] = jnp.zeros_like(dwo_acc)
            dgate_ref[...] = jnp.zeros_like(dgate_ref)
            dfg_ref[...] = jnp.zeros_like(dfg_ref)
            loss_ref[...] = jnp.zeros_like(loss_ref)

        w = w_ref[...]
        gate_v = gate_ref[...]
        fg = fg_ref[...]
        do2_parts = [None] * (tm // sub)

        def chain(n):
            rows = slice(n * sub, (n + 1) * sub)
            o2 = _dot(mix_ref[rows, :], w)
            yield
            x2 = x_ref[rows, :] + gate_v * o2
            r2 = lax.rsqrt(jnp.mean(x2 * x2, axis=-1, keepdims=True) + EPS)
            x2n = x2 * r2
            err = x2n * fg - t_ref[rows, :]
            part = 0.5 * jnp.sum(jnp.sum(err * err, axis=-1, keepdims=True), axis=0, keepdims=True) / D_MODEL
            loss_ref[...] += jnp.broadcast_to(part, loss_ref.shape)
            dy = err / D_MODEL
            dfg_ref[...] += jnp.sum(dy * x2n, axis=0, keepdims=True)
            gdy = dy * fg
            dx2 = r2 * (gdy - x2n * jnp.mean(gdy * x2n, axis=-1, keepdims=True))
            dx2_ref[rows, :] = dx2
            dgate_ref[...] += jnp.sum(dx2 * o2, axis=0, keepdims=True)
            do2 = (dx2 * gate_v).astype(BF16)
            dmix_ref[rows, :] = _dot_nt(do2, w)
            do2_parts[n] = do2

        _in_lockstep(chain(n) for n in range(tm // sub))
        dwo_acc[...] += _dot_tn(mix_ref[...], jnp.concatenate(do2_parts, axis=0))

        @pl.when(pl.program_id(0) == seq // tm - 1)
        def _():
            dwo_ref[...] = dwo_acc[...].astype(BF16)

    tile = pl.BlockSpec((tm, D_MODEL), lambda i: (i, 0))
    vec = _full((1, D_MODEL))
    return pl.pallas_call(
        body, name="out_fwd_bwd", grid=(seq // tm,),
        out_shape=(jax.ShapeDtypeStruct((seq, D_MODEL), F32), jax.ShapeDtypeStruct((seq, D_MODEL), F32),
                   jax.ShapeDtypeStruct((D_MODEL, D_MODEL), BF16), jax.ShapeDtypeStruct((1, D_MODEL), F32),
                   jax.ShapeDtypeStruct((1, D_MODEL), F32), jax.ShapeDtypeStruct((1, 128), F32)),
        in_specs=[tile, tile, tile, _full((D_MODEL, D_MODEL)), vec, vec],
        out_specs=(tile, tile, _full((D_MODEL, D_MODEL)), vec, vec, _full((1, 128))),
        scratch_shapes=[pltpu.VMEM((D_MODEL, D_MODEL), F32)],
        compiler_params=_params(("arbitrary",)),
    )(mix, x, tgt, w_out_b, gate, final_g)


def _mix_bwd(proj, gates, dmix, conv_a, pooled, cst, nst, mst, bg_pad, conv_w8, w_pool, ls_pool, mh_g):
    seq = proj.shape[0]
    n_chunks = seq // CHUNK
    per_step = BWD_CHUNKS
    blk = per_step * CHUNK
    n_blocks = n_chunks // per_step

    def body(zp_ref, qk_ref, v_ref, o_ref, zm_ref, g_ref, dmix_ref, a_ref, pooled_ref, cst_ref, nst_ref, mst_ref,
             mnx_ref, bg_ref, cw_ref, wp_ref, ls_ref, mhg_ref,
             dp_ref, dcw_ref, dcb_ref, dwp_ref, dls_ref, dmhg_ref, dbg_ref,
             dc_scr, dn_scr, dapad, dpipad):
        i = pl.program_id(0)
        bi = n_blocks - 1 - i

        @pl.when(i == 0)
        def _():
            for ref in (dc_scr, dn_scr, dcw_ref, dcb_ref, dwp_ref, dls_ref, dmhg_ref, dbg_ref):
                ref[...] = jnp.zeros_like(ref)
            dapad[blk:blk + CONV_HALO, :] = jnp.zeros((CONV_HALO, 1024), F32)
            dpipad[blk:blk + POOL_HALO, :] = jnp.zeros((POOL_HALO, D_POOL), F32)

        dpooled = []
        for g, w in enumerate(POOL_WINDOWS):
            lanes = slice(g * POOL_GROUP_DIM, (g + 1) * POOL_GROUP_DIM)
            zlanes = slice(D_POOL + g * POOL_GROUP_DIM, D_POOL + (g + 1) * POOL_GROUP_DIM)
            inv = _pool_inv_count(bi * blk, blk, w)
            pb = pooled_ref[:, lanes]
            wpb = wp_ref[g].astype(BF16)
            yw = _dot(pb, wpb)
            ls = ls_ref[:, lanes]
            zp = zp_ref[:, lanes]
            sg = _sigmoid(zp)
            dpo = dmix_ref[:, lanes]
            dp_ref[:, zlanes] = (dpo * (yw * ls) * (sg * (1.0 + zp * (1.0 - sg)))).astype(BF16)
            dy = dpo * (zp * sg)
            dls_ref[:, lanes] += jnp.sum(dy * yw, axis=0, keepdims=True)
            dyw = (dy * ls).astype(BF16)
            dwp_ref[g] += _dot_tn(pb, dyw)
            dpl = _dot_nt(dyw, wpb)
            dpooled.append(dpl)
            dpipad[0:blk, lanes] = dpl * inv
        for g, w in enumerate(POOL_WINDOWS):
            lanes = slice(g * POOL_GROUP_DIM, (g + 1) * POOL_GROUP_DIM)
            du = _window_sum(dpipad[:, lanes], w, _rows_ahead)[0:blk, :] - dpooled[g]
            dp_ref[:, lanes] = du.astype(BF16)
        dpipad[blk:blk + POOL_HALO, :] = dpipad[0:POOL_HALO, :]

        def silu_and_slope(rows, cols):
            a = a_ref[rows, cols]
            sg = _sigmoid(a)
            return a * sg, sg * (1.0 + a * (1.0 - sg))

        lane = lax.broadcasted_iota(jnp.int32, (CHUNK, 128), 1)
        row = lax.broadcasted_iota(jnp.int32, (CHUNK, 128), 0)
        scale_k = HEAD_DIM ** -0.5
        forms = [None] * per_step
        col_g_rows = [[] for _ in range(per_step)]
        dig_parts = [[] for _ in range(per_step)]
        db_parts = [[] for _ in range(per_step)]
        d_state = [[None] * N_HEADS for _ in range(per_step)]

        def state_terms(s, h, c_h, n_row, vb, kb):
            dcn, dnn = d_state[s][h]
            dcnb = dcn.astype(BF16)
            amat = _dot(vb, dcnb) + dnn
            kdc = _dot_nt(kb, dcnb)
            ddecay = (jnp.sum(jnp.sum(dcn * c_h, axis=-1, keepdims=True), axis=0, keepdims=True)
                      + jnp.sum(dnn * n_row, axis=-1, keepdims=True))
            return dcn, dnn, amat, kdc, ddecay

        def head(s, h):
            rows = slice(s * CHUNK, (s + 1) * CHUNK)
            lanes = slice(h * HEAD_DIM, (h + 1) * HEAD_DIM)
            klanes = slice(D_MLSTM + h * HEAD_DIM, D_MLSTM + (h + 1) * HEAD_DIM)
            gpre, causal, utri, bcol, gt8, brow = forms[s]
            qh, dsilu_q = silu_and_slope(rows, lanes)
            kh, dsilu_k = silu_and_slope(rows, klanes)
            kh = kh * scale_k
            vh = v_ref[rows, lanes]
            bc = bcol[:, N_HEADS + h:N_HEADS + h + 1]
            br = brow[N_HEADS + h:N_HEADS + h + 1, :]
            igr = gt8[h:h + 1, :]
            igc = gpre[:, h:h + 1]
            m_prev = mst_ref[s, h:h + 1, 0:1]
            m_next = mnx_ref[0, h:h + 1, 0:1] if s == per_step - 1 else mst_ref[s + 1, h:h + 1, 0:1]
            c_h = cst_ref[s, h]
            n_row = nst_ref[s, h:h + 1, :]
            w_c, decay, _, last = _state_weights(bc, igc, m_prev, m_next)
            terms = None
            if s == per_step - 1:
                terms = state_terms(s, h, c_h, n_row, vh.astype(BF16), kh.astype(BF16))
            f = yield from _head_fwd(qh, kh, vh, bc, br, igr, m_prev, c_h, n_row, causal)
            qb, kb, vb, cb = f["qb"], f["kb"], f["vb"], f["cb"]
            sm, dmat, inter, den, dn, hm = f["s"], f["dmat"], f["inter"], f["den"], f["dn"], f["hm"]
            yield

            rinv = lax.rsqrt(_row_mean_mxu(hm * hm) + EPS)
            hmn = hm * rinv
            gh = mhg_ref[:, lanes]
            o_pre = o_ref[rows, lanes]
            og = _sigmoid(o_pre)
            zm = zm_ref[rows, lanes]
            sgz = _sigmoid(zm)
            sz = zm * sgz
            dout = dmix_ref[rows, D_POOL + h * HEAD_DIM:D_POOL + (h + 1) * HEAD_DIM]
            hn = hmn * gh
            dp_ref[rows, 2560 + h * HEAD_DIM:2560 + (h + 1) * HEAD_DIM] = (
                dout * hn * sz * og * (1.0 - og)).astype(BF16)
            dp_ref[rows, 3072 + h * HEAD_DIM:3072 + (h + 1) * HEAD_DIM] = (
                dout * hn * og * (sgz * (1.0 + zm * (1.0 - sgz)))).astype(BF16)
            dhn = dout * og * sz
            dmhg_ref[:, lanes] += jnp.sum(dhn * hmn, axis=0, keepdims=True)
            dyn = dhn * gh
            dhm = rinv * (dyn - hmn * _row_mean_mxu(dyn * hmn))
            yield

            inv_dn = 1.0 / dn
            dnum = dhm * inv_dn
            hd = jnp.sum(dhm * hm, axis=-1, keepdims=True)
            dden = jnp.where(jnp.abs(den) > f["emt"], -hd / den, 0.0)
            dnb = dnum.astype(BF16)
            dnv = _dot_nt(dnb, vb)
            dv = _dot_tn(sm.astype(BF16), dnb)
            dnc = _dot(dnb, cb)
            dc_prev = _dot_tn((inter * dnum).astype(BF16), qb)
            dn_prev = jnp.sum((inter * dden) * qh, axis=0, keepdims=True)
            yield
            ds = dnv + dden
            dqk = (ds * dmat).astype(BF16)
            dqk_k = _dot(dqk, kb)
            dk = _dot_tn(dqk, qb)
            for _ in range(per_step - 1 - s):
                yield
            if terms is None:
                terms = state_terms(s, h, c_h, n_row, vb, kb)
            dcn, dnn, amat, kdc, ddecay = terms
            d_start = (decay * dcn + dc_prev, decay * dnn + dn_prev)
            if s > 0:
                d_state[s - 1][h] = d_start
            else:
                dc_scr[h] = d_start[0]
                dn_scr[h:h + 1, :] = d_start[1]
            yield
            gmat = ds * sm
            row_g = jnp.sum(gmat, axis=-1, keepdims=True)
            col_g_rows[s].append(jnp.where(row == h, jnp.sum(gmat, axis=0, keepdims=True), 0.0))
            gcol = inter * (jnp.sum(dnum * f["cq"], axis=-1, keepdims=True) + dden * f["nq"])
            dw = jnp.sum(amat * kh, axis=-1, keepdims=True)
            e = dw * w_c
            db_last = ddecay * decay + jnp.sum(e, axis=0, keepdims=True)
            dig_parts[s].append(jnp.where(lane == h, e, 0.0))
            db_parts[s].append(
                jnp.where(lane == N_HEADS + h, row_g + gcol - e + jnp.where(last, db_last, 0.0), 0.0))
            yield
            dq = dqk_k + inter * (dnc + dden * n_row)
            dp_ref[rows, 2048 + h * HEAD_DIM:2048 + (h + 1) * HEAD_DIM] = (dv + w_c * kdc).astype(BF16)
            dapad[rows, lanes] = dq * dsilu_q
            dapad[rows, klanes] = (dk + w_c * amat) * scale_k * dsilu_k

        chains = []
        for s in reversed(range(per_step)):
            gpre = g_ref[s * CHUNK:(s + 1) * CHUNK, :] + bg_ref[...]
            forms[s] = (gpre,) + _gate_forms(gpre)
            for h in range(N_HEADS):
                if s == per_step - 1:
                    d_state[s][h] = (dc_scr[h], dn_scr[h:h + 1, :])
                chains.append(head(s, h))
        _in_lockstep(chains)

        for s in range(per_step):
            rows = slice(s * CHUNK, (s + 1) * CHUNK)
            gpre, utri = forms[s][0], forms[s][2]
            cs_t = sum(col_g_rows[s][1:], col_g_rows[s][0]).T
            dig_all = sum(dig_parts[s][1:], dig_parts[s][0]) + cs_t
            db_cols = sum(db_parts[s][1:], db_parts[s][0])
            shifted = jnp.zeros((CHUNK, 128), F32)
            for h in range(N_HEADS):
                shifted = shifted + jnp.where(lane == N_HEADS + h, cs_t[:, h:h + 1], 0.0)
            dlf = _dot_f32(utri, db_cols - shifted)
            dgates = dig_all + dlf * _sigmoid(-gpre)
            dp_ref[rows, N_MAIN:N_MAIN + 128] = dgates.astype(BF16)
            dbg_ref[...] += jnp.sum(dgates, axis=0, keepdims=True)
        dp_ref[:, N_MAIN + 128:N_PAD] = jnp.zeros((blk, N_PAD - N_MAIN - 128), BF16)

        da_pad = dapad[...]
        da = da_pad[0:blk, :]
        dcb_ref[...] += jnp.sum(da, axis=0, keepdims=True)
        x = qk_ref[...]
        dx = jnp.zeros((blk, 1024), F32)
        for j in range(CONV_WIDTH):
            da_j = _rows_ahead(da_pad, CONV_WIDTH - 1 - j)[0:blk, :]
            dcw_ref[j:j + 1, :] += jnp.sum(da_j * x, axis=0, keepdims=True)
            dx = dx + cw_ref[j:j + 1, :] * da_j
        dp_ref[:, 1024:2048] = dx.astype(BF16)
        dapad[blk:blk + CONV_HALO, :] = dapad[0:CONV_HALO, :]

    bmap = lambda i: n_blocks - 1 - i
    wide = pl.BlockSpec((blk, 1024), lambda i: (bmap(i), 0))
    state = pl.BlockSpec((per_step, 8, 128), lambda i: (bmap(i), 0, 0))
    in_specs = [
        pl.BlockSpec((blk, 512), lambda i: (bmap(i), 1)),
        pl.BlockSpec((blk, 1024), lambda i: (bmap(i), 1)),
        pl.BlockSpec((blk, 512), lambda i: (bmap(i), 4)),
        pl.BlockSpec((blk, 512), lambda i: (bmap(i), 5)),
        pl.BlockSpec((blk, 512), lambda i: (bmap(i), 6)),
        pl.BlockSpec((blk, 128), lambda i: (bmap(i), 0)),
        wide, wide,
        pl.BlockSpec((blk, D_POOL), lambda i: (bmap(i), 0)),
        pl.BlockSpec((per_step, N_HEADS, HEAD_DIM, HEAD_DIM), lambda i: (bmap(i), 0, 0, 0)),
        state, state,
        pl.BlockSpec((1, 8, 128), lambda i: (jnp.minimum((bmap(i) + 1) * per_step, n_chunks - 1), 0, 0)),
        _full((1, 128)), _full((8, 1024)), _full((4, 128, 128)), _full((1, 512)), _full((1, 512))]
    return pl.pallas_call(
        body, name="mix_bwd", grid=(n_blocks,),
        out_shape=(jax.ShapeDtypeStruct((seq, N_PAD), BF16), jax.ShapeDtypeStruct((8, 1024), F32),
                   jax.ShapeDtypeStruct((1, 1024), F32), jax.ShapeDtypeStruct((4, 128, 128), F32),
                   jax.ShapeDtypeStruct((1, 512), F32), jax.ShapeDtypeStruct((1, 512), F32),
                   jax.ShapeDtypeStruct((1, 128), F32)),
        in_specs=in_specs,
        out_specs=(pl.BlockSpec((blk, N_PAD), lambda i: (bmap(i), 0)), _full((8, 1024)), _full((1, 1024)),
                   _full((4, 128, 128)), _full((1, 512)), _full((1, 512)), _full((1, 128))),
        scratch_shapes=[pltpu.VMEM((N_HEADS, HEAD_DIM, HEAD_DIM), F32), pltpu.VMEM((8, 128), F32),
                        pltpu.VMEM((blk + CONV_HALO, 1024), F32), pltpu.VMEM((blk + POOL_HALO, D_POOL), F32)],
        compiler_params=_params(("arbitrary",)),
    )(proj, proj, proj, proj, proj, gates, dmix, conv_a, pooled, cst, nst, mst, mst, bg_pad, conv_w8,
      w_pool, ls_pool, mh_g)


def _bwd_in(dproj, w_in_t, x, dx2, norm_g, scale):
    seq = x.shape[0]
    tm = min(512, seq)
    sub = min(256, tm)

    def body(dp_ref, wt_ref, x_ref, dx2_ref, ng_ref, sc_ref, gx_ref, dsh_ref, dsc_ref, dng_ref):
        @pl.when(pl.program_id(0) == 0)
        def _():
            dsh_ref[...] = jnp.zeros_like(dsh_ref)
            dsc_ref[...] = jnp.zeros_like(dsc_ref)
            dng_ref[...] = jnp.zeros_like(dng_ref)

        ng = ng_ref[...]
        one_sc = 1.0 + sc_ref[...]

        def chain(n):
            rows = slice(n * sub, (n + 1) * sub)
            dh = _dot(dp_ref[rows, :], wt_ref[...])
            yield
            xt = x_ref[rows, :]
            r = lax.rsqrt(jnp.mean(xt * xt, axis=-1, keepdims=True) + EPS)
            xn = xt * r
            dsh_ref[...] += jnp.sum(dh, axis=0, keepdims=True)
            dhxn_sum = jnp.sum(dh * xn, axis=0, keepdims=True)
            dsc_ref[...] += dhxn_sum * ng
            dng_ref[...] += dhxn_sum * one_sc
            dxn = dh * (ng * one_sc)
            gx_ref[rows, :] = r * (dxn - xn * jnp.mean(dxn * xn, axis=-1, keepdims=True)) + dx2_ref[rows, :]

        _in_lockstep(chain(n) for n in range(tm // sub))

    tile = pl.BlockSpec((tm, D_MODEL), lambda i: (i, 0))
    vec = _full((1, D_MODEL))
    return pl.pallas_call(
        body, name="bwd_in", grid=(seq // tm,),
        out_shape=(jax.ShapeDtypeStruct((seq, D_MODEL), F32),) + (jax.ShapeDtypeStruct((1, D_MODEL), F32),) * 3,
        in_specs=[pl.BlockSpec((tm, N_PAD), lambda i: (i, 0)), _full((N_PAD, D_MODEL)), tile, tile, vec, vec],
        out_specs=(tile, vec, vec, vec),
        compiler_params=_params(("arbitrary",)),
    )(dproj, w_in_t, x, dx2, norm_g, scale)


def _dw_in(h_b, dproj):
    seq = h_b.shape[0]
    tk = min(4096, seq)
    tn = 768
    n_t = seq // tk

    def body(h_ref, dp_ref, dwt_ref, acc):
        t = pl.program_id(1)

        @pl.when(t == 0)
        def _():
            acc[...] = jnp.zeros_like(acc)

        acc[...] += _dot_tn(dp_ref[...], h_ref[...])

        @pl.when(t == n_t - 1)
        def _():
            dwt_ref[...] = acc[...].astype(BF16)

    return pl.pallas_call(
        body, name="dw_in", grid=(N_PAD // tn, n_t),
        out_shape=jax.ShapeDtypeStruct((N_PAD, D_MODEL), BF16),
        in_specs=[pl.BlockSpec((tk, D_MODEL), lambda j, t: (t, 0)), pl.BlockSpec((tk, tn), lambda j, t: (t, j))],
        out_specs=pl.BlockSpec((tn, D_MODEL), lambda j, t: (j, 0)),
        scratch_shapes=[pltpu.VMEM((tn, D_MODEL), F32)],
        compiler_params=_params(("arbitrary", "arbitrary")),
    )(h_b, dproj)


def _adam_update(g, w, m, v, g_ref, d_ref, m_ref, v_ref):
    mn = ADAM_B1 * m + (1.0 - ADAM_B1) * g
    vn = ADAM_B2 * v + (1.0 - ADAM_B2) * (g * g)
    m_hat = mn / (1.0 - ADAM_B1 ** ADAM_STEP)
    v_hat = vn / (1.0 - ADAM_B2 ** ADAM_STEP)
    g_ref[...] = g
    d_ref[...] = -ADAM_LR * (m_hat / (jnp.sqrt(v_hat) + ADAM_EPS) + ADAM_WD * w)
    m_ref[...] = mn
    v_ref[...] = vn


def _adam_sum(name, parts, w, m, v, row_tile, col_tile=None):
    rows, cols = w.shape
    col_tile = cols if col_tile is None else col_tile
    n_parts = parts.shape[0]

    def body(p_ref, w_ref, m_ref, v_ref, g_out, d_out, m_out, v_out):
        g = p_ref[0].astype(F32)
        for j in range(1, n_parts):
            g = g + p_ref[j].astype(F32)
        _adam_update(g, w_ref[...], m_ref[...], v_ref[...], g_out, d_out, m_out, v_out)

    tile = pl.BlockSpec((row_tile, col_tile), lambda i, j: (i, j))
    return pl.pallas_call(
        body, name=name, grid=(rows // row_tile, cols // col_tile),
        out_shape=(jax.ShapeDtypeStruct((rows, cols), F32),) * 4,
        in_specs=[pl.BlockSpec((n_parts, row_tile, col_tile), lambda i, j: (0, i, j)), tile, tile, tile],
        out_specs=(tile,) * 4,
        compiler_params=_params(("arbitrary", "arbitrary")),
    )(parts, w, m, v)


def _adam_sum_own(name, me, blocks, landed, w, m, v, row_tile, col_tile=None):
    rows, cols = w.shape
    col_tile = cols if col_tile is None else col_tile

    def body(me_ref, p_ref, own_ref, w_ref, m_ref, v_ref, g_out, d_out, m_out, v_out):
        g = jnp.zeros((row_tile, col_tile), F32)
        for j in range(N_DEV):
            g = g + jnp.where(me_ref[0] == j, own_ref[0], p_ref[j]).astype(F32)
        _adam_update(g, w_ref[...], m_ref[...], v_ref[...], g_out, d_out, m_out, v_out)

    tile = pl.BlockSpec((row_tile, col_tile), lambda i, j, me_ref: (i, j))
    return pl.pallas_call(
        body, name=name, out_shape=(jax.ShapeDtypeStruct((rows, cols), F32),) * 4,
        grid_spec=pltpu.PrefetchScalarGridSpec(
            num_scalar_prefetch=1, grid=(rows // row_tile, cols // col_tile),
            in_specs=[pl.BlockSpec((N_DEV, row_tile, col_tile), lambda i, j, me_ref: (0, i, j)),
                      pl.BlockSpec((1, row_tile, col_tile), lambda i, j, me_ref: (me_ref[0], i, j)),
                      tile, tile, tile],
            out_specs=(tile,) * 4),
        compiler_params=_params(("arbitrary", "arbitrary")),
    )(me, landed, blocks, w, m, v)


def _adam_ada(sc_all16, dmod_blk16, w, m, v):
    rows, cols = w.shape

    def body(sc_ref, dm_ref, w_ref, m_ref, v_ref, g_out, d_out, m_out, v_out):
        g = _dot_tn(sc_ref[...].astype(BF16), dm_ref[...].astype(BF16))
        _adam_update(g, w_ref[...], m_ref[...], v_ref[...], g_out, d_out, m_out, v_out)

    return pl.pallas_call(
        body, name="adam_w_ada", grid=(1,),
        out_shape=(jax.ShapeDtypeStruct((rows, cols), F32),) * 4,
        in_specs=[_full(sc_all16.shape), _full(dmod_blk16.shape)] + [_full((rows, cols))] * 3,
        out_specs=(_full((rows, cols)),) * 4,
        compiler_params=_params(("arbitrary",)),
    )(sc_all16, dmod_blk16, w, m, v)


def _adam_small(me, parts, own, loss_parts, own_loss, w, m, v):
    names = list(w)
    n = len(names)

    def body(me_ref, *refs):
        p_refs, loss_ref = refs[:n], refs[n]
        own_refs, own_loss_ref = refs[n + 1:2 * n + 1], refs[2 * n + 1]
        w_refs, m_refs, v_refs = (refs[2 * n + 2 + k * n:2 * n + 2 + (k + 1) * n] for k in range(3))
        outs = refs[5 * n + 2:]

        def total(p_ref, own_ref):
            acc = jnp.where(me_ref[0] == 0, own_ref[...], p_ref[0])
            for j in range(1, N_DEV):
                acc = acc + jnp.where(me_ref[0] == j, own_ref[...], p_ref[j])
            return acc

        for a in range(n):
            g = total(p_refs[a], own_refs[a])
            width = w_refs[a].shape[-1]
            if g.shape[-1] != width:
                g = g[..., 0:width]
            _adam_update(g, w_refs[a][...], m_refs[a][...], v_refs[a][...], *outs[4 * a:4 * a + 4])
        outs[4 * n][...] = total(loss_ref, own_loss_ref)

    args = [parts[k] for k in names] + [loss_parts] + [own[k] for k in names] + [own_loss]
    args += [d[k] for d in (w, m, v) for k in names]
    out_shape = tuple(jax.ShapeDtypeStruct(w[k].shape, F32) for k in names for _ in range(4))
    out_shape += (jax.ShapeDtypeStruct(loss_parts.shape[1:], F32),)
    out = pl.pallas_call(
        body, name="adam_small", out_shape=out_shape,
        grid_spec=pltpu.PrefetchScalarGridSpec(
            num_scalar_prefetch=1, grid=(1,), in_specs=[_full(a.shape) for a in args],
            out_specs=tuple(_full(s.shape) for s in out_shape)),
        compiler_params=_params(("arbitrary",)),
    )(me, *args)
    return {k: out[4 * a:4 * a + 4] for a, k in enumerate(names)}, out[4 * n]


def _local_step(x2, tgt2, shift, scale, gate, norm_g, w_in_t, w_out_b, conv_w, conv_b, w_pool, ls_pool,
                mh_norm_g, b_gates, final_g, send_dw_out=None, send_dw_in=None):
    bg_pad = jnp.pad(b_gates, ((0, 0), (0, 128 - b_gates.shape[1])))
    conv_w8 = jnp.pad(conv_w, ((0, 8 - CONV_WIDTH), (0, 0)))
    fg = final_g.reshape(1, D_MODEL)

    proj, gates, h_b = _fwd_proj(x2, norm_g, scale, shift, w_in_t)
    mix, cst, nst, mst, conv_a, pooled = _mix_fwd(proj, gates, bg_pad, conv_w8, conv_b, w_pool, ls_pool, mh_norm_g)
    dx2, dmix, dwo, dgate, dfg, loss = _out_fwd_bwd(mix, x2, tgt2, w_out_b, gate, fg)
    if send_dw_out is not None:
        bg_pad = bg_pad + send_dw_out(dwo)
    dproj, dcw8, dcb, dwp, dls, dmhg, dbg = _mix_bwd(proj, gates, dmix, conv_a, pooled, cst, nst, mst, bg_pad,
                                                      conv_w8, w_pool, ls_pool, mh_norm_g)
    dw_in_t = _dw_in(h_b, dproj)[:N_IN]
    ng_in = norm_g
    if send_dw_in is not None:
        ng_in = norm_g + send_dw_in(dw_in_t, dcw8[:CONV_WIDTH])
    gx, dsh, dsc, dng = _bwd_in(dproj, w_in_t, x2, dx2, ng_in, scale)
    return dict(loss=loss, grad_x=gx, dw_in_t=dw_in_t, dw_out=dwo, dconv_w=dcw8[:CONV_WIDTH], conv_b=dcb,
                w_pool=dwp, ls_pool=dls, mh_norm_g=dmhg, b_gates=dbg, final_g=dfg, norm_g=dng,
                dmod=jnp.concatenate([dsh, dsc, dgate], axis=1))


def kernel(x, c, norm_g, w_ada, b_ada, w_in, b_gates, conv_w, conv_b, w_pool, ls_pool, mh_norm_g, w_out, final_g, loss_target, m_norm_g, m_w_ada, m_b_ada, m_w_in, m_b_gates, m_conv_w, m_conv_b, m_w_pool, m_ls_pool, m_mh_norm_g, m_w_out, m_final_g, v_norm_g, v_w_ada, v_b_ada, v_w_in, v_b_gates, v_conv_w, v_conv_b, v_w_pool, v_ls_pool, v_mh_norm_g, v_w_out, v_final_g):
    seq = x.shape[1]
    me = 4 * lax.axis_index("x") + 2 * lax.axis_index("y") + lax.axis_index("c")

    b_ada_blk = lax.dynamic_slice(b_ada, (0, me * ADA_SHARD), (1, ADA_SHARD))
    g_in, g_out, g_cw, mod_all, sc_all16 = _gather_weights_and_mod(
        (w_in[0].astype(BF16).T, w_out[0].astype(BF16), conv_w[0]), c, w_ada[0], b_ada_blk)
    w_in_t = jnp.pad(g_in.reshape(N_IN, D_MODEL), ((0, N_PAD - N_IN), (0, 0)))
    w_out_b = g_out.reshape(D_MODEL, D_MODEL)
    conv_w_full = jnp.transpose(g_cw, (1, 0, 2)).reshape(CONV_WIDTH, 2 * D_MLSTM)
    mod = lax.dynamic_index_in_dim(mod_all, me, axis=1, keepdims=False).reshape(1, 3 * D_MODEL)
    shift, scale, gate = mod[:, :D_MODEL], mod[:, D_MODEL:2 * D_MODEL], mod[:, 2 * D_MODEL:]

    flights = {}

    def send_dw_out(dwo):
        blocks = dwo.reshape(N_DEV, D_MODEL // N_DEV, D_MODEL)
        flights["out"], token = _scatter_start("send_dw_out", (blocks,))
        return token

    def send_dw_in(dw_in_t, dcw):
        blocks = dw_in_t.reshape(N_DEV, N_SHARD, D_MODEL)
        dcw_blocks = jnp.transpose(dcw.reshape(CONV_WIDTH, N_DEV, 128), (1, 0, 2))
        flights["in"], token = _scatter_start("send_dw_in", (blocks, dcw_blocks))
        return token

    r = _local_step(x[0], loss_target[0], shift, scale, gate, norm_g, w_in_t, w_out_b, conv_w_full, conv_b,
                    w_pool[0], ls_pool, mh_norm_g, b_gates, final_g, send_dw_out, send_dw_in)

    small_names = ("norm_g", "b_ada", "b_gates", "conv_b", "w_pool", "ls_pool", "mh_norm_g", "final_g")
    small_grads = dict(norm_g=r["norm_g"], b_ada=r["dmod"], b_gates=r["b_gates"], conv_b=r["conv_b"],
                       w_pool=r["w_pool"], ls_pool=r["ls_pool"], mh_norm_g=r["mh_norm_g"], final_g=r["final_g"])
    flights["small"], _ = _scatter_start(
        "send_small", (r["loss"],) + tuple(small_grads[k] for k in small_names), whole=True)
    (s_in, p_in), p_cw = _scatter_wait("recv_dw_in", flights["in"],
                                       flights["small"][2 + 1 + small_names.index("w_pool")], fill=(False, True))
    ((s_out, p_out),) = _scatter_wait("recv_dw_out", flights["out"], p_cw, fill=(False,))

    me1 = me.astype(jnp.int32).reshape(1)
    in_t = _adam_sum_own("adam_w_in", me1, s_in, p_in, w_in[0].T, m_w_in[0].T, v_w_in[0].T, N_SHARD, 256)
    gi, di, mi, vi = (o.T for o in in_t)
    go, do_, mo, vo = _adam_sum_own("adam_w_out", me1, s_out, p_out, w_out[0], m_w_out[0], v_w_out[0], 128)
    gc, dc, mc, vc = _adam_sum("adam_conv_w", p_cw, conv_w[0], m_conv_w[0], v_conv_w[0], CONV_WIDTH)
    gathered = _scatter_wait("recv_small", flights["small"], go, whole=True, fill=(False,) * (1 + len(small_names)))
    (own_loss, p_loss), rest = gathered[0], gathered[1:]
    own_small = dict(zip(small_names, (sent for sent, _ in rest)))
    p_small = dict(zip(small_names, (landed for _, landed in rest)))

    def plain(norm_g_, b_ada_, b_gates_, conv_b_, w_pool_, ls_pool_, mh_norm_g_, final_g_):
        return dict(norm_g=norm_g_, b_ada=b_ada_, b_gates=b_gates_, conv_b=conv_b_, w_pool=w_pool_[0],
                    ls_pool=ls_pool_, mh_norm_g=mh_norm_g_, final_g=final_g_.reshape(1, D_MODEL))

    small, loss_row = _adam_small(
        me1, p_small, own_small, p_loss, own_loss,
        plain(norm_g, b_ada, b_gates, conv_b, w_pool, ls_pool, mh_norm_g, final_g),
        plain(m_norm_g, m_b_ada, m_b_gates, m_conv_b, m_w_pool, m_ls_pool, m_mh_norm_g, m_final_g),
        plain(v_norm_g, v_b_ada, v_b_gates, v_conv_b, v_w_pool, v_ls_pool, v_mh_norm_g, v_final_g))

    device = lax.broadcasted_iota(jnp.int32, (N_DEV, 1), 0)
    dmod_all = jnp.where(device == me, own_small["b_ada"], p_small["b_ada"].reshape(N_DEV, 3 * D_MODEL))
    dmod_blk16 = jnp.pad(lax.dynamic_slice(dmod_all, (0, me * ADA_SHARD), (N_DEV, ADA_SHARD)), ((0, 8), (0, 0)))
    ga, da, ma, va = _adam_ada(sc_all16, dmod_blk16, w_ada[0], m_w_ada[0], v_w_ada[0])

    names = ("norm_g", "w_ada", "b_ada", "w_in", "b_gates", "conv_w", "conv_b", "w_pool", "ls_pool", "mh_norm_g",
             "w_out", "final_g")
    shapes = dict(norm_g=norm_g.shape, b_ada=b_ada.shape, b_gates=b_gates.shape, conv_b=conv_b.shape,
                  w_pool=w_pool.shape, ls_pool=ls_pool.shape, mh_norm_g=mh_norm_g.shape, final_g=final_g.shape)
    sharded = dict(w_ada=(ga, da, ma, va), w_in=(gi, di, mi, vi), conv_w=(gc, dc, mc, vc), w_out=(go, do_, mo, vo))
    outs = []
    for kind in range(4):
        for nm in names:
            if nm in sharded:
                outs.append(sharded[nm][kind][None])
            else:
                outs.append(small[nm][kind].reshape(shapes[nm]))
    loss = loss_row[0, 0]
    grad_x = r["grad_x"].reshape(1, seq, D_MODEL)
    return (loss, grad_x, *outs)
```

```python
import jax
import jax.numpy as jnp
from jax import lax
from jax.experimental import pallas as pl
from jax.experimental.pallas import tpu as pltpu

F32 = jnp.float32
BF16 = jnp.bfloat16

D_MODEL = 1024
D_POOL = 512
D_MLSTM = 512
N_HEADS = 4
HEAD_DIM = 128
CHUNK = 128
POOL_WINDOWS = (2, 4, 8, 16)
POOL_GROUP_DIM = 128
CONV_WIDTH = 4
EPS = 1e-6
N_MAIN = 3584
N_IN = 3592
N_PAD = 3840
N_SHARD = N_IN // 8
ADA_SHARD = 3 * D_MODEL // 8
N_DEV = 8
CONV_COLS = 256
CONV_HALO = 8
POOL_HALO = 16
NEG_BIG = -1e30
VMEM_LIMIT_BYTES = 56 * 1024 * 1024

ADAM_LR = 0.001
ADAM_B1 = 0.9
ADAM_B2 = 0.999
ADAM_EPS = 1e-08
ADAM_WD = 0.01
ADAM_STEP = 10

def _dot(a, b):
    return jnp.dot(a, b, preferred_element_type=F32)


def _dot_nt(a, b):
    return lax.dot_general(a, b, (((1,), (1,)), ((), ())), preferred_element_type=F32)


def _dot_tn(a, b):
    return lax.dot_general(a, b, (((0,), (0,)), ((), ())), preferred_element_type=F32)


def _dot_f32(a, b):
    return jnp.dot(a, b, precision=lax.Precision.HIGHEST, preferred_element_type=F32)


def _row_mean_mxu(x):
    return _dot(x.astype(BF16), jnp.full((HEAD_DIM, HEAD_DIM), 1.0 / HEAD_DIM, BF16))


def _sigmoid(x):
    return jax.nn.sigmoid(x)


def _log_sigmoid(x):
    return jnp.minimum(x, 0.0) - jnp.log1p(jnp.exp(-jnp.abs(x)))


def _params(sem):
    return pltpu.CompilerParams(dimension_semantics=sem, vmem_limit_bytes=VMEM_LIMIT_BYTES)


def _full(shape):
    n = len(shape)
    return pl.BlockSpec(shape, lambda *_: (0,) * n)


def _mesh_pos():
    return lax.axis_index("x"), lax.axis_index("y"), lax.axis_index("c")


def _peer(k):
    x, y, c = _mesh_pos()
    px = 1 - x if (k >> 2) & 1 else x
    py = 1 - y if (k >> 1) & 1 else y
    pc = 1 - c if k & 1 else c
    return (px, py, pc), 4 * px + 2 * py + pc


def _remote(src, dst, send_sem, recv_sem, to):
    return pltpu.make_async_remote_copy(src_ref=src, dst_ref=dst, send_sem=send_sem, recv_sem=recv_sem, device_id=to,
                                        device_id_type=pl.DeviceIdType.MESH)


def _two_level_gather(src, dst, send_sems, recv_sems, local_sems):
    n = len(src)
    x, y, c = _mesh_pos()
    me = 4 * x + 2 * y + c
    sibling = (x, y, 1 - c)
    south = c == 0
    near = (jnp.where(south, 1 - x, x), jnp.where(south, y, 1 - y))
    far = (jnp.where(south, x, 1 - x), jnp.where(south, 1 - y, y))
    diag = (1 - x, 1 - y)

    def block_of(chip, core):
        return 4 * chip[0] + 2 * chip[1] + core

    def copy(a, k, block, to, own=False):
        return _remote(src[a] if own else dst[a].at[block], dst[a].at[block], send_sems.at[a, k], recv_sems.at[a, k], to)

    local = [pltpu.make_async_copy(src[a], dst[a].at[me], local_sems.at[a]) for a in range(n)]
    sent = [copy(a, 0, me, sibling, True) for a in range(n)]
    sent += [copy(a, 1, me, (*near, c), True) for a in range(n)]
    sent += [copy(a, 2, me, (*far, c), True) for a in range(n)]
    for cp in local + sent:
        cp.start()
    yield
    for a in range(n):
        copy(a, 1, block_of(near, c), sibling).wait_recv()
        sent += [copy(a, 3, block_of(near, c), (*far, c)), copy(a, 4, block_of(near, c), sibling)]
        sent[-2].start()
        sent[-1].start()
    for k, chip in ((2, far), (3, diag)):
        for a in range(n):
            copy(a, k, block_of(chip, c), sibling).wait_recv()
            sent.append(copy(a, 3 + k, block_of(chip, c), sibling))
            sent[-1].start()
    for k, chip in ((0, (x, y)), (4, far), (5, near), (6, diag)):
        for a in range(n):
            copy(a, k, block_of(chip, 1 - c), sibling).wait_recv()
    for cp in sent:
        cp.wait_send()
    for cp in local:
        cp.wait()


GATHER_COPIES = 7


def _swap_with_all(buf, send_sems, recv_sems):
    x, y, c = _mesh_pos()
    me = 4 * x + 2 * y + c
    copies = [_remote(buf.at[me], buf.at[me], send_sems.at[k - 1], recv_sems.at[k - 1], _peer(k)[0])
              for k in range(1, N_DEV)]
    for cp in copies:
        cp.start()
    for cp in copies:
        cp.wait()


def _gather_weights_and_mod(shards, c_row, w_ada_blk, b_ada_blk):
    n = len(shards)

    def body(*refs):
        src, (c_ref, w_ref, b_ref) = refs[:n], refs[n:n + 3]
        dst, (mod_ref, sc_ref) = refs[n + 3:2 * n + 3], refs[2 * n + 3:2 * n + 5]
        c_all, g_send, g_recv, g_local, c_send, c_recv, m_send, m_recv = refs[2 * n + 5:]
        x, y, c = _mesh_pos()
        me = 4 * x + 2 * y + c
        gather = _two_level_gather(src, dst, g_send, g_recv, g_local)
        next(gather)
        c_all[me] = c_ref[...]
        _swap_with_all(c_all, c_send, c_recv)
        cv = jnp.concatenate([c_all[j] for j in range(N_DEV)] + [jnp.zeros((N_DEV, D_MODEL), F32)], axis=0)
        sc = cv * _sigmoid(cv)
        sc_ref[...] = sc
        blk = _dot(sc.astype(BF16), w_ref[...].astype(BF16)) + b_ref[...]
        mod_ref[me] = blk[0:N_DEV, :]
        _swap_with_all(mod_ref, m_send, m_recv)
        for _ in gather:
            pass

    hbm = pl.BlockSpec(memory_space=pltpu.HBM)
    vmem = pl.BlockSpec(memory_space=pltpu.VMEM)
    peers = pltpu.SemaphoreType.DMA((N_DEV - 1,))
    return pl.pallas_call(
        body, name="gather_weights",
        out_shape=tuple(jax.ShapeDtypeStruct((N_DEV,) + s.shape, s.dtype) for s in shards)
        + (jax.ShapeDtypeStruct((N_DEV, N_DEV, ADA_SHARD), F32), jax.ShapeDtypeStruct((2 * N_DEV, D_MODEL), F32)),
        in_specs=[hbm] * n + [vmem] * 3, out_specs=tuple([hbm] * n + [vmem] * 2),
        scratch_shapes=[pltpu.VMEM((N_DEV, 1, D_MODEL), F32),
                        pltpu.SemaphoreType.DMA((n, GATHER_COPIES)), pltpu.SemaphoreType.DMA((n, GATHER_COPIES)),
                        pltpu.SemaphoreType.DMA((n,)), peers, peers, peers, peers],
    )(*shards, c_row, w_ada_blk, b_ada_blk)


def _scatter_copies(src, land, send_sems, recv_sems, whole=False):
    x, y, c = _mesh_pos()
    me = 4 * x + 2 * y + c
    copies = []
    for k in range(1, N_DEV):
        peer, p = _peer(k)
        for a in range(len(src)):
            i = a * (N_DEV - 1) + k - 1
            copies.append(_remote(src[a] if whole else src[a].at[p], land[a].at[me], send_sems.at[i],
                                  recv_sems.at[i], peer))
    return copies


def _scatter_start(name, blocks, whole=False):
    n = len(blocks)

    def body(*refs):
        src, land = refs[:n], refs[n:2 * n]
        send_sems, recv_sems = refs[2 * n], refs[2 * n + 1]
        token_ref = refs[-1]
        for cp in _scatter_copies(src, land, send_sems, recv_sems, whole):
            cp.start()
        token_ref[...] = jnp.zeros_like(token_ref)

    hbm = pl.BlockSpec(memory_space=pltpu.HBM)
    sem = pl.BlockSpec(memory_space=pltpu.SEMAPHORE)
    landing = [((N_DEV,) + b.shape if whole else b.shape, b.dtype) for b in blocks]
    through = tuple(pltpu.HBM(b.shape, b.dtype) for b in blocks) + tuple(pltpu.HBM(s, d) for s, d in landing)
    args = [pltpu.with_memory_space_constraint(b, pltpu.HBM) for b in blocks]
    args += [pltpu.with_memory_space_constraint(lax.empty(s, d), pltpu.HBM) for s, d in landing]
    out = pl.pallas_call(
        body, name=name,
        out_shape=(pltpu.SemaphoreType.DMA((n * (N_DEV - 1),)),) * 2 + through
        + (jax.ShapeDtypeStruct((8, 128), F32),),
        in_specs=[hbm] * (2 * n),
        out_specs=(sem, sem) + (hbm,) * (2 * n) + (pl.BlockSpec(memory_space=pltpu.VMEM),),
        input_output_aliases={i: 2 + i for i in range(2 * n)},
        compiler_params=pltpu.CompilerParams(has_side_effects=pltpu.SideEffectType.DATAFLOW_SIDE_EFFECTING),
    )(*args)
    return out[:-1], out[-1][0:1, 0:1]


def _scatter_wait(name, state, after, whole=False, fill=None):
    n = (len(state) - 2) // 2
    send_sems, recv_sems = state[0], state[1]
    src, land = state[2:2 + n], state[2 + n:]

    def body(*refs):
        src_r, land_r = refs[:n], refs[n:2 * n]
        for cp in _scatter_copies(src_r, land_r, refs[2 * n], refs[2 * n + 1], whole):
            cp.wait_send()
            cp.wait_recv()

    hbm = pl.BlockSpec(memory_space=pltpu.HBM)
    sem = pl.BlockSpec(memory_space=pltpu.SEMAPHORE)
    out = pl.pallas_call(
        body, name=name,
        out_shape=tuple(pltpu.HBM(b.shape, b.dtype) for b in src + land),
        in_specs=[hbm] * (2 * n) + [sem, sem, pl.BlockSpec(memory_space=pl.ANY)],
        out_specs=(hbm,) * (2 * n),
        input_output_aliases={i: i for i in range(2 * n)},
        compiler_params=pltpu.CompilerParams(has_side_effects=pltpu.SideEffectType.DATAFLOW_SIDE_EFFECTING),
    )(*src, *land, send_sems, recv_sems, after)
    me = 4 * lax.axis_index("x") + 2 * lax.axis_index("y") + lax.axis_index("c")
    landed = []
    for a in range(n):
        if fill is not None and not fill[a]:
            landed.append((out[a], out[n + a]))
            continue
        own = out[a][None] if whole else lax.dynamic_index_in_dim(out[a], me, axis=0, keepdims=True)
        landed.append(lax.dynamic_update_slice_in_dim(out[n + a], own, me, axis=0))
    return landed


def _fwd_proj(x, norm_g, scale, shift, w_in_t):
    seq = x.shape[0]
    tm = min(512, seq)
    sub = min(256, tm)
    tn = 512

    def body(x_ref, ng_ref, sc_ref, sh_ref, wt_ref, proj_ref, gates_ref, h_ref):
        def chain(n):
            for _ in range(n):
                yield
            rows = slice(n * sub, (n + 1) * sub)
            xt = x_ref[rows, :]
            r = lax.rsqrt(jnp.mean(xt * xt, axis=-1, keepdims=True) + EPS)
            h = ((xt * r) * ng_ref[...]) * (1.0 + sc_ref[...]) + sh_ref[...]
            hb = h.astype(BF16)
            h_ref[rows, :] = hb
            yield
            gates_ref[rows, :] = _dot_nt(hb, wt_ref[N_MAIN:N_MAIN + 128, :])
            for j in range(N_MAIN // tn):
                proj_ref[rows, j * tn:(j + 1) * tn] = _dot_nt(hb, wt_ref[j * tn:(j + 1) * tn, :])

        _in_lockstep(chain(n) for n in range(tm // sub))

    vec = _full((1, D_MODEL))
    tile = pl.BlockSpec((tm, D_MODEL), lambda i: (i, 0))
    return pl.pallas_call(
        body, name="fwd_proj", grid=(seq // tm,),
        out_shape=(jax.ShapeDtypeStruct((seq, N_MAIN), F32), jax.ShapeDtypeStruct((seq, 128), F32),
                   jax.ShapeDtypeStruct((seq, D_MODEL), BF16)),
        in_specs=[tile, vec, vec, vec, _full((N_PAD, D_MODEL))],
        out_specs=(pl.BlockSpec((tm, N_MAIN), lambda i: (i, 0)), pl.BlockSpec((tm, 128), lambda i: (i, 0)), tile),
        compiler_params=_params(("arbitrary",)),
    )(x, norm_g, scale, shift, w_in_t)


def _gate_forms(gpre):
    r = lax.broadcasted_iota(jnp.int32, (CHUNK, CHUNK), 0)
    c = lax.broadcasted_iota(jnp.int32, (CHUNK, CHUNK), 1)
    causal = c <= r
    ltri = jnp.where(causal, 1.0, 0.0).astype(F32)
    utri = jnp.where(r <= c, 1.0, 0.0).astype(F32)
    bcol = _dot_f32(ltri, _log_sigmoid(gpre))
    gt8 = gpre.T[0:8, :]
    brow = _dot_f32(_log_sigmoid(gt8), utri)
    return causal, utri, bcol, gt8, brow


def _in_lockstep(stages):
    alive = list(stages)
    while alive:
        still = []
        for g in alive:
            try:
                next(g)
                still.append(g)
            except StopIteration:
                pass
        alive = still


def _head_fwd(qh, kh, vh, bc, br, igr, m_prev, c_h, n_row, causal):
    qb, kb, vb, cb = qh.astype(BF16), kh.astype(BF16), vh.astype(BF16), c_h.astype(BF16)
    qk = _dot_nt(qb, kb)
    cq = _dot_nt(qb, cb)
    nq = _dot_nt(qb, jnp.broadcast_to(n_row.astype(BF16), (HEAD_DIM, HEAD_DIM)))
    yield
    dlog = jnp.where(causal, bc - br + igr, NEG_BIG)
    inter_log = bc + m_prev
    m_t = jnp.maximum(inter_log, jnp.max(dlog, axis=-1, keepdims=True))
    yield
    dmat = jnp.exp(dlog - m_t)
    inter = jnp.exp(inter_log - m_t)
    s = qk * dmat
    sv = _dot(s.astype(BF16), vb)
    yield
    den = jnp.sum(s, axis=-1, keepdims=True) + inter * nq
    emt = jnp.exp(-m_t)
    yield
    num = sv + inter * cq
    dn = jnp.maximum(jnp.abs(den), emt)
    hm = num / dn
    return dict(dmat=dmat, inter=inter, qb=qb, kb=kb, vb=vb, cb=cb, s=s, cq=cq, nq=nq, den=den, emt=emt,
                dn=dn, hm=hm)


def _state_weights(bc, igc, m_prev, m_new=None):
    last = lax.broadcasted_iota(jnp.int32, (CHUNK, 1), 0) == CHUNK - 1
    b_last = jnp.sum(jnp.where(last, bc, 0.0), axis=0, keepdims=True)
    wlog = b_last - bc + igc
    if m_new is None:
        m_new = jnp.maximum(b_last + m_prev, jnp.max(wlog, axis=0, keepdims=True))
    w_c = jnp.exp(wlog - m_new)
    decay = jnp.exp(b_last + m_prev - m_new)
    return w_c, decay, m_new, last


def _rows_back(x, k):
    return x if k == 0 else pltpu.roll(x, k, 0)


def _rows_ahead(x, k):
    return x if k == 0 else pltpu.roll(x, x.shape[0] - k, 0)


def _conv_taps(xpad):
    return [_rows_back(xpad, CONV_WIDTH - 1 - j)[CONV_HALO:, :] for j in range(CONV_WIDTH)]


def _window_sum(x, w, shift):
    k = 1
    while k < w:
        x = x + shift(x, k)
        k *= 2
    return x


def _pool_window_sum(upad_ref, g, w):
    lanes = slice(g * POOL_GROUP_DIM, (g + 1) * POOL_GROUP_DIM)
    return _window_sum(upad_ref[:, lanes], w, _rows_back)[POOL_HALO:, :]


def _pool_inv_count(row0, rows, w):
    pos = row0 + lax.broadcasted_iota(jnp.int32, (rows, 1), 0) + 1
    return 1.0 / jnp.minimum(pos, w).astype(F32)


FWD_CHUNKS = 4
BWD_CHUNKS = 4


def _mix_fwd(proj, gates, bg_pad, conv_w8, conv_b, w_pool, ls_pool, mh_g):
    seq = proj.shape[0]
    n_chunks = seq // CHUNK
    per_step = FWD_CHUNKS
    blk = per_step * CHUNK

    def body(uz_ref, qk_ref, v_ref, o_ref, zm_ref, uh_ref, qkh_ref, g_ref, bg_ref, cw_ref, cb_ref, wp_ref,
             ls_ref, mhg_ref, mix_ref, cst_ref, nst_ref, mst_ref, a_ref, pooled_ref, c_scr, n_scr, m_scr, xpad, upad):
        i = pl.program_id(0)

        @pl.when(i == 0)
        def _():
            c_scr[...] = jnp.zeros_like(c_scr)
            n_scr[...] = jnp.zeros_like(n_scr)
            m_scr[...] = jnp.zeros_like(m_scr)

        first = i == 0

        upad[0:POOL_HALO, :] = jnp.where(first, 0.0, uh_ref[...])
        upad[POOL_HALO:POOL_HALO + blk, :] = uz_ref[:, 0:D_POOL]
        for g, w in enumerate(POOL_WINDOWS):
            lanes = slice(g * POOL_GROUP_DIM, (g + 1) * POOL_GROUP_DIM)
            pooled = (_pool_window_sum(upad, g, w) * _pool_inv_count(i * blk, blk, w) - uz_ref[:, lanes]).astype(BF16)
            pooled_ref[:, lanes] = pooled
            y = _dot(pooled, wp_ref[g].astype(BF16)) * ls_ref[:, lanes]
            zp = uz_ref[:, D_POOL + g * POOL_GROUP_DIM:D_POOL + (g + 1) * POOL_GROUP_DIM]
            mix_ref[:, lanes] = (y * (zp * _sigmoid(zp))).astype(BF16)

        xpad[0:CONV_HALO, :] = jnp.where(first, 0.0, qkh_ref[...])
        xpad[CONV_HALO:CONV_HALO + blk, :] = qk_ref[...]
        qk_blocks = []
        for lo in range(0, 2 * D_MLSTM, CONV_COLS):
            cols = slice(lo, lo + CONV_COLS)
            a = cb_ref[:, cols]
            for j, tap in enumerate(_conv_taps(xpad[:, cols])):
                a = a + cw_ref[j:j + 1, cols] * tap
            a_ref[:, cols] = a
            qk_blocks.append(a * _sigmoid(a))

        def qk_lanes(rows, lo):
            return qk_blocks[lo // CONV_COLS][rows, lo % CONV_COLS:lo % CONV_COLS + HEAD_DIM]

        def head(rows, h, qh, kh, vh, bc, br, igr, m_prev, c_h, n_row, causal):
            lanes = slice(h * HEAD_DIM, (h + 1) * HEAD_DIM)
            f = yield from _head_fwd(qh, kh, vh, bc, br, igr, m_prev, c_h, n_row, causal)
            yield
            hm = f["hm"]
            hn = hm * lax.rsqrt(_row_mean_mxu(hm * hm) + EPS) * mhg_ref[:, lanes]
            zm = zm_ref[rows, lanes]
            out = hn * _sigmoid(o_ref[rows, lanes]) * (zm * _sigmoid(zm))
            mix_ref[rows, D_POOL + h * HEAD_DIM:D_POOL + (h + 1) * HEAD_DIM] = out.astype(BF16)

        c_cur = [c_scr[h] for h in range(N_HEADS)]
        n_cur = [n_scr[h:h + 1, :] for h in range(N_HEADS)]
        m_cur = [m_scr[h:h + 1, 0:1] for h in range(N_HEADS)]
        chains = []
        for s in range(per_step):
            rows = slice(s * CHUNK, (s + 1) * CHUNK)
            gpre = g_ref[rows, :] + bg_ref[...]
            causal, _, bcol, gt8, brow = _gate_forms(gpre)
            nst_ref[s] = jnp.zeros((8, 128), F32)
            mst_ref[s] = jnp.zeros((8, 128), F32)
            for h in range(N_HEADS):
                lanes = slice(h * HEAD_DIM, (h + 1) * HEAD_DIM)
                cst_ref[s, h] = c_cur[h]
                nst_ref[s, h:h + 1, :] = n_cur[h]
                mst_ref[s, h:h + 1, :] = jnp.broadcast_to(m_cur[h], (1, 128))
                qh = qk_lanes(rows, h * HEAD_DIM)
                kh = qk_lanes(rows, D_MLSTM + h * HEAD_DIM) * (HEAD_DIM ** -0.5)
                vh = v_ref[rows, lanes]
                bc = bcol[:, N_HEADS + h:N_HEADS + h + 1]
                br = brow[N_HEADS + h:N_HEADS + h + 1, :]
                igr = gt8[h:h + 1, :]
                igc = gpre[:, h:h + 1]
                chains.append(head(rows, h, qh, kh, vh, bc, br, igr, m_cur[h], c_cur[h], n_cur[h], causal))
                w_c, decay, m_new, _ = _state_weights(bc, igc, m_cur[h])
                c_cur[h] = decay * c_cur[h] + _dot_tn((vh * w_c).astype(BF16), kh.astype(BF16))
                n_cur[h] = decay * n_cur[h] + jnp.sum(w_c * kh, axis=0, keepdims=True)
                m_cur[h] = m_new
        for h in range(N_HEADS):
            c_scr[h] = c_cur[h]
            n_scr[h:h + 1, :] = n_cur[h]
            m_scr[h:h + 1, :] = jnp.broadcast_to(m_cur[h], (1, 128))
        _in_lockstep(chains)

    in_specs = [
        pl.BlockSpec((blk, 1024), lambda i: (i, 0)),
        pl.BlockSpec((blk, 1024), lambda i: (i, 1)),
        pl.BlockSpec((blk, 512), lambda i: (i, 4)),
        pl.BlockSpec((blk, 512), lambda i: (i, 5)),
        pl.BlockSpec((blk, 512), lambda i: (i, 6)),
        pl.BlockSpec((POOL_HALO, 512), lambda i: (jnp.maximum(i * (blk // POOL_HALO) - 1, 0), 0)),
        pl.BlockSpec((CONV_HALO, 1024), lambda i: (jnp.maximum(i * (blk // CONV_HALO) - 1, 0), 1)),
        pl.BlockSpec((blk, 128), lambda i: (i, 0)),
        _full((1, 128)), _full((8, 1024)), _full((1, 1024)), _full((4, 128, 128)), _full((1, 512)),
        _full((1, 512))]
    return pl.pallas_call(
        body, name="mix_fwd", grid=(n_chunks // per_step,),
        out_shape=(jax.ShapeDtypeStruct((seq, D_MODEL), BF16),
                   jax.ShapeDtypeStruct((n_chunks, N_HEADS, HEAD_DIM, HEAD_DIM), F32),
                   jax.ShapeDtypeStruct((n_chunks, 8, 128), F32),
                   jax.ShapeDtypeStruct((n_chunks, 8, 128), F32),
                   jax.ShapeDtypeStruct((seq, 2 * D_MLSTM), F32),
                   jax.ShapeDtypeStruct((seq, D_POOL), BF16)),
        in_specs=in_specs,
        out_specs=(pl.BlockSpec((blk, D_MODEL), lambda i: (i, 0)),
                   pl.BlockSpec((per_step, N_HEADS, HEAD_DIM, HEAD_DIM), lambda i: (i, 0, 0, 0)),
                   pl.BlockSpec((per_step, 8, 128), lambda i: (i, 0, 0)),
                   pl.BlockSpec((per_step, 8, 128), lambda i: (i, 0, 0)),
                   pl.BlockSpec((blk, 2 * D_MLSTM), lambda i: (i, 0)),
                   pl.BlockSpec((blk, D_POOL), lambda i: (i, 0))),
        scratch_shapes=[pltpu.VMEM((N_HEADS, HEAD_DIM, HEAD_DIM), F32), pltpu.VMEM((8, 128), F32),
                        pltpu.VMEM((8, 128), F32), pltpu.VMEM((CONV_HALO + blk, 1024), F32),
                        pltpu.VMEM((POOL_HALO + blk, D_POOL), F32)],
        compiler_params=_params(("arbitrary",)),
    )(proj, proj, proj, proj, proj, proj, proj, gates, bg_pad, conv_w8, conv_b, w_pool, ls_pool, mh_g)


def _out_fwd_bwd(mix, x, tgt, w_out_b, gate, final_g):
    seq = x.shape[0]
    tm = min(512, seq)
    sub = min(256, tm)

    def body(mix_ref, x_ref, t_ref, w_ref, gate_ref, fg_ref, dx2_ref, dmix_ref, dwo_ref, dgate_ref, dfg_ref,
             loss_ref, dwo_acc):
        @pl.when(pl.program_id(0) == 0)
        def _():
            dwo_acc[...] = jnp.zeros_like(dwo_acc)
            dgate_ref[...] = jnp.zeros_like(dgate_ref)
            dfg_ref[...] = jnp.zeros_like(dfg_ref)
            loss_ref[...] = jnp.zeros_like(loss_ref)

        w = w_ref[...]
        gate_v = gate_ref[...]
        fg = fg_ref[...]
        do2_parts = [None] * (tm // sub)

        def chain(n):
            rows = slice(n * sub, (n + 1) * sub)
            o2 = _dot(mix_ref[rows, :], w)
            yield
            x2 = x_ref[rows, :] + gate_v * o2
            r2 = lax.rsqrt(jnp.mean(x2 * x2, axis=-1, keepdims=True) + EPS)
            x2n = x2 * r2
            err = x2n * fg - t_ref[rows, :]
            part = 0.5 * jnp.sum(jnp.sum(err * err, axis=-1, keepdims=True), axis=0, keepdims=True) / D_MODEL
            loss_ref[...] += jnp.broadcast_to(part, loss_ref.shape)
            dy = err / D_MODEL
            dfg_ref[...] += jnp.sum(dy * x2n, axis=0, keepdims=True)
            gdy = dy * fg
            dx2 = r2 * (gdy - x2n * jnp.mean(gdy * x2n, axis=-1, keepdims=True))
            dx2_ref[rows, :] = dx2
            dgate_ref[...] += jnp.sum(dx2 * o2, axis=0, keepdims=True)
            do2 = (dx2 * gate_v).astype(BF16)
            dmix_ref[rows, :] = _dot_nt(do2, w)
            do2_parts[n] = do2

        _in_lockstep(chain(n) for n in range(tm // sub))
        dwo_acc[...] += _dot_tn(mix_ref[...], jnp.concatenate(do2_parts, axis=0))

        @pl.when(pl.program_id(0) == seq // tm - 1)
        def _():
            dwo_ref[...] = dwo_acc[...].astype(BF16)

    tile = pl.BlockSpec((tm, D_MODEL), lambda i: (i, 0))
    vec = _full((1, D_MODEL))
    return pl.pallas_call(
        body, name="out_fwd_bwd", grid=(seq // tm,),
        out_shape=(jax.ShapeDtypeStruct((seq, D_MODEL), F32), jax.ShapeDtypeStruct((seq, D_MODEL), F32),
                   jax.ShapeDtypeStruct((D_MODEL, D_MODEL), BF16), jax.ShapeDtypeStruct((1, D_MODEL), F32),
                   jax.ShapeDtypeStruct((1, D_MODEL), F32), jax.ShapeDtypeStruct((1, 128), F32)),
        in_specs=[tile, tile, tile, _full((D_MODEL, D_MODEL)), vec, vec],
        out_specs=(tile, tile, _full((D_MODEL, D_MODEL)), vec, vec, _full((1, 128))),
        scratch_shapes=[pltpu.VMEM((D_MODEL, D_MODEL), F32)],
        compiler_params=_params(("arbitrary",)),
    )(mix, x, tgt, w_out_b, gate, final_g)


def _mix_bwd(proj, gates, dmix, conv_a, pooled, cst, nst, mst, bg_pad, conv_w8, w_pool, ls_pool, mh_g):
    seq = proj.shape[0]
    n_chunks = seq // CHUNK
    per_step = BWD_CHUNKS
    blk = per_step * CHUNK
    n_blocks = n_chunks // per_step

    def body(zp_ref, qk_ref, v_ref, o_ref, zm_ref, g_ref, dmix_ref, a_ref, pooled_ref, cst_ref, nst_ref, mst_ref,
             mnx_ref, bg_ref, cw_ref, wp_ref, ls_ref, mhg_ref,
             dp_ref, dcw_ref, dcb_ref, dwp_ref, dls_ref, dmhg_ref, dbg_ref,
             dc_scr, dn_scr, dapad, dpipad):
        i = pl.program_id(0)
        bi = n_blocks - 1 - i

        @pl.when(i == 0)
        def _():
            for ref in (dc_scr, dn_scr, dcw_ref, dcb_ref, dwp_ref, dls_ref, dmhg_ref, dbg_ref):
                ref[...] = jnp.zeros_like(ref)
            dapad[blk:blk + CONV_HALO, :] = jnp.zeros((CONV_HALO, 1024), F32)
            dpipad[blk:blk + POOL_HALO, :] = jnp.zeros((POOL_HALO, D_POOL), F32)

        dpooled = []
        for g, w in enumerate(POOL_WINDOWS):
            lanes = slice(g * POOL_GROUP_DIM, (g + 1) * POOL_GROUP_DIM)
            zlanes = slice(D_POOL + g * POOL_GROUP_DIM, D_POOL + (g + 1) * POOL_GROUP_DIM)
            inv = _pool_inv_count(bi * blk, blk, w)
            pb = pooled_ref[:, lanes]
            wpb = wp_ref[g].astype(BF16)
            yw = _dot(pb, wpb)
            ls = ls_ref[:, lanes]
            zp = zp_ref[:, lanes]
            sg = _sigmoid(zp)
            dpo = dmix_ref[:, lanes]
            dp_ref[:, zlanes] = (dpo * (yw * ls) * (sg * (1.0 + zp * (1.0 - sg)))).astype(BF16)
            dy = dpo * (zp * sg)
            dls_ref[:, lanes] += jnp.sum(dy * yw, axis=0, keepdims=True)
            dyw = (dy * ls).astype(BF16)
            dwp_ref[g] += _dot_tn(pb, dyw)
            dpl = _dot_nt(dyw, wpb)
            dpooled.append(dpl)
            dpipad[0:blk, lanes] = dpl * inv
        for g, w in enumerate(POOL_WINDOWS):
            lanes = slice(g * POOL_GROUP_DIM, (g + 1) * POOL_GROUP_DIM)
            du = _window_sum(dpipad[:, lanes], w, _rows_ahead)[0:blk, :] - dpooled[g]
            dp_ref[:, lanes] = du.astype(BF16)
        dpipad[blk:blk + POOL_HALO, :] = dpipad[0:POOL_HALO, :]

        def silu_and_slope(rows, cols):
            a = a_ref[rows, cols]
            sg = _sigmoid(a)
            return a * sg, sg * (1.0 + a * (1.0 - sg))

        lane = lax.broadcasted_iota(jnp.int32, (CHUNK, 128), 1)
        row = lax.broadcasted_iota(jnp.int32, (CHUNK, 128), 0)
        scale_k = HEAD_DIM ** -0.5
        forms = [None] * per_step
        col_g_rows = [[] for _ in range(per_step)]
        dig_parts = [[] for _ in range(per_step)]
        db_parts = [[] for _ in range(per_step)]
        d_state = [[None] * N_HEADS for _ in range(per_step)]

        def state_terms(s, h, c_h, n_row, vb, kb):
            dcn, dnn = d_state[s][h]
            dcnb = dcn.astype(BF16)
            amat = _dot(vb, dcnb) + dnn
            kdc = _dot_nt(kb, dcnb)
            ddecay = (jnp.sum(jnp.sum(dcn * c_h, axis=-1, keepdims=True), axis=0, keepdims=True)
                      + jnp.sum(dnn * n_row, axis=-1, keepdims=True))
            return dcn, dnn, amat, kdc, ddecay

        def head(s, h):
            rows = slice(s * CHUNK, (s + 1) * CHUNK)
            lanes = slice(h * HEAD_DIM, (h + 1) * HEAD_DIM)
            klanes = slice(D_MLSTM + h * HEAD_DIM, D_MLSTM + (h + 1) * HEAD_DIM)
            gpre, causal, utri, bcol, gt8, brow = forms[s]
            qh, dsilu_q = silu_and_slope(rows, lanes)
            kh, dsilu_k = silu_and_slope(rows, klanes)
            kh = kh * scale_k
            vh = v_ref[rows, lanes]
            bc = bcol[:, N_HEADS + h:N_HEADS + h + 1]
            br = brow[N_HEADS + h:N_HEADS + h + 1, :]
            igr = gt8[h:h + 1, :]
            igc = gpre[:, h:h + 1]
            m_prev = mst_ref[s, h:h + 1, 0:1]
            m_next = mnx_ref[0, h:h + 1, 0:1] if s == per_step - 1 else mst_ref[s + 1, h:h + 1, 0:1]
            c_h = cst_ref[s, h]
            n_row = nst_ref[s, h:h + 1, :]
            w_c, decay, _, last = _state_weights(bc, igc, m_prev, m_next)
            terms = None
            if s == per_step - 1:
                terms = state_terms(s, h, c_h, n_row, vh.astype(BF16), kh.astype(BF16))
            f = yield from _head_fwd(qh, kh, vh, bc, br, igr, m_prev, c_h, n_row, causal)
            qb, kb, vb, cb = f["qb"], f["kb"], f["vb"], f["cb"]
            sm, dmat, inter, den, dn, hm = f["s"], f["dmat"], f["inter"], f["den"], f["dn"], f["hm"]
            yield

            rinv = lax.rsqrt(_row_mean_mxu(hm * hm) + EPS)
            hmn = hm * rinv
            gh = mhg_ref[:, lanes]
            o_pre = o_ref[rows, lanes]
            og = _sigmoid(o_pre)
            zm = zm_ref[rows, lanes]
            sgz = _sigmoid(zm)
            sz = zm * sgz
            dout = dmix_ref[rows, D_POOL + h * HEAD_DIM:D_POOL + (h + 1) * HEAD_DIM]
            hn = hmn * gh
            dp_ref[rows, 2560 + h * HEAD_DIM:2560 + (h + 1) * HEAD_DIM] = (
                dout * hn * sz * og * (1.0 - og)).astype(BF16)
            dp_ref[rows, 3072 + h * HEAD_DIM:3072 + (h + 1) * HEAD_DIM] = (
                dout * hn * og * (sgz * (1.0 + zm * (1.0 - sgz)))).astype(BF16)
            dhn = dout * og * sz
            dmhg_ref[:, lanes] += jnp.sum(dhn * hmn, axis=0, keepdims=True)
            dyn = dhn * gh
            dhm = rinv * (dyn - hmn * _row_mean_mxu(dyn * hmn))
            yield

            inv_dn = 1.0 / dn
            dnum = dhm * inv_dn
            hd = jnp.sum(dhm * hm, axis=-1, keepdims=True)
            dden = jnp.where(jnp.abs(den) > f["emt"], -hd / den, 0.0)
            dnb = dnum.astype(BF16)
            dnv = _dot_nt(dnb, vb)
            dv = _dot_tn(sm.astype(BF16), dnb)
            dnc = _dot(dnb, cb)
            dc_prev = _dot_tn((inter * dnum).astype(BF16), qb)
            dn_prev = jnp.sum((inter * dden) * qh, axis=0, keepdims=True)
            yield
            ds = dnv + dden
            dqk = (ds * dmat).astype(BF16)
            dqk_k = _dot(dqk, kb)
            dk = _dot_tn(dqk, qb)
            for _ in range(per_step - 1 - s):
                yield
            if terms is None:
                terms = state_terms(s, h, c_h, n_row, vb, kb)
            dcn, dnn, amat, kdc, ddecay = terms
            d_start = (decay * dcn + dc_prev, decay * dnn + dn_prev)
            if s > 0:
                d_state[s - 1][h] = d_start
            else:
                dc_scr[h] = d_start[0]
                dn_scr[h:h + 1, :] = d_start[1]
            yield
            gmat = ds * sm
            row_g = jnp.sum(gmat, axis=-1, keepdims=True)
            col_g_rows[s].append(jnp.where(row == h, jnp.sum(gmat, axis=0, keepdims=True), 0.0))
            gcol = inter * (jnp.sum(dnum * f["cq"], axis=-1, keepdims=True) + dden * f["nq"])
            dw = jnp.sum(amat * kh, axis=-1, keepdims=True)
            e = dw * w_c
            db_last = ddecay * decay + jnp.sum(e, axis=0, keepdims=True)
            dig_parts[s].append(jnp.where(lane == h, e, 0.0))
            db_parts[s].append(
                jnp.where(lane == N_HEADS + h, row_g + gcol - e + jnp.where(last, db_last, 0.0), 0.0))
            yield
            dq = dqk_k + inter * (dnc + dden * n_row)
            dp_ref[rows, 2048 + h * HEAD_DIM:2048 + (h + 1) * HEAD_DIM] = (dv + w_c * kdc).astype(BF16)
            dapad[rows, lanes] = dq * dsilu_q
            dapad[rows, klanes] = (dk + w_c * amat) * scale_k * dsilu_k

        chains = []
        for s in reversed(range(per_step)):
            gpre = g_ref[s * CHUNK:(s + 1) * CHUNK, :] + bg_ref[...]
            forms[s] = (gpre,) + _gate_forms(gpre)
            for h in range(N_HEADS):
                if s == per_step - 1:
                    d_state[s][h] = (dc_scr[h], dn_scr[h:h + 1, :])
                chains.append(head(s, h))
        _in_lockstep(chains)

        for s in range(per_step):
            rows = slice(s * CHUNK, (s + 1) * CHUNK)
            gpre, utri = forms[s][0], forms[s][2]
            cs_t = sum(col_g_rows[s][1:], col_g_rows[s][0]).T
            dig_all = sum(dig_parts[s][1:], dig_parts[s][0]) + cs_t
            db_cols = sum(db_parts[s][1:], db_parts[s][0])
            shifted = jnp.zeros((CHUNK, 128), F32)
            for h in range(N_HEADS):
                shifted = shifted + jnp.where(lane == N_HEADS + h, cs_t[:, h:h + 1], 0.0)
            dlf = _dot_f32(utri, db_cols - shifted)
            dgates = dig_all + dlf * _sigmoid(-gpre)
            dp_ref[rows, N_MAIN:N_MAIN + 128] = dgates.astype(BF16)
            dbg_ref[...] += jnp.sum(dgates, axis=0, keepdims=True)
        dp_ref[:, N_MAIN + 128:N_PAD] = jnp.zeros((blk, N_PAD - N_MAIN - 128), BF16)

        for lo in range(0, 2 * D_MLSTM, CONV_COLS):
            cols = slice(lo, lo + CONV_COLS)
            da_pad = dapad[:, cols]
            dcb_ref[:, cols] += jnp.sum(da_pad[0:blk, :], axis=0, keepdims=True)
            x = qk_ref[:, cols]
            dx = jnp.zeros((blk, CONV_COLS), F32)
            for j in range(CONV_WIDTH):
                da_j = _rows_ahead(da_pad, CONV_WIDTH - 1 - j)[0:blk, :]
                dcw_ref[j:j + 1, cols] += jnp.sum(da_j * x, axis=0, keepdims=True)
                dx = dx + cw_ref[j:j + 1, cols] * da_j
            dp_ref[:, 1024 + lo:1024 + lo + CONV_COLS] = dx.astype(BF16)
        dapad[blk:blk + CONV_HALO, :] = dapad[0:CONV_HALO, :]

    bmap = lambda i: n_blocks - 1 - i
    wide = pl.BlockSpec((blk, 1024), lambda i: (bmap(i), 0))
    state = pl.BlockSpec((per_step, 8, 128), lambda i: (bmap(i), 0, 0))
    in_specs = [
        pl.BlockSpec((blk, 512), lambda i: (bmap(i), 1)),
        pl.BlockSpec((blk, 1024), lambda i: (bmap(i), 1)),
        pl.BlockSpec((blk, 512), lambda i: (bmap(i), 4)),
        pl.BlockSpec((blk, 512), lambda i: (bmap(i), 5)),
        pl.BlockSpec((blk, 512), lambda i: (bmap(i), 6)),
        pl.BlockSpec((blk, 128), lambda i: (bmap(i), 0)),
        wide, wide,
        pl.BlockSpec((blk, D_POOL), lambda i: (bmap(i), 0)),
        pl.BlockSpec((per_step, N_HEADS, HEAD_DIM, HEAD_DIM), lambda i: (bmap(i), 0, 0, 0)),
        state, state,
        pl.BlockSpec((1, 8, 128), lambda i: (jnp.minimum((bmap(i) + 1) * per_step, n_chunks - 1), 0, 0)),
        _full((1, 128)), _full((8, 1024)), _full((4, 128, 128)), _full((1, 512)), _full((1, 512))]
    return pl.pallas_call(
        body, name="mix_bwd", grid=(n_blocks,),
        out_shape=(jax.ShapeDtypeStruct((seq, N_PAD), BF16), jax.ShapeDtypeStruct((8, 1024), F32),
                   jax.ShapeDtypeStruct((1, 1024), F32), jax.ShapeDtypeStruct((4, 128, 128), F32),
                   jax.ShapeDtypeStruct((1, 512), F32), jax.ShapeDtypeStruct((1, 512), F32),
                   jax.ShapeDtypeStruct((1, 128), F32)),
        in_specs=in_specs,
        out_specs=(pl.BlockSpec((blk, N_PAD), lambda i: (bmap(i), 0)), _full((8, 1024)), _full((1, 1024)),
                   _full((4, 128, 128)), _full((1, 512)), _full((1, 512)), _full((1, 128))),
        scratch_shapes=[pltpu.VMEM((N_HEADS, HEAD_DIM, HEAD_DIM), F32), pltpu.VMEM((8, 128), F32),
                        pltpu.VMEM((blk + CONV_HALO, 1024), F32), pltpu.VMEM((blk + POOL_HALO, D_POOL), F32)],
        compiler_params=_params(("arbitrary",)),
    )(proj, proj, proj, proj, proj, gates, dmix, conv_a, pooled, cst, nst, mst, mst, bg_pad, conv_w8,
      w_pool, ls_pool, mh_g)


def _bwd_in(dproj, w_in_t, x, dx2, norm_g, scale):
    seq = x.shape[0]
    tm = min(512, seq)
    sub = min(256, tm)

    def body(dp_ref, wt_ref, x_ref, dx2_ref, ng_ref, sc_ref, gx_ref, dsh_ref, dsc_ref, dng_ref):
        @pl.when(pl.program_id(0) == 0)
        def _():
            dsh_ref[...] = jnp.zeros_like(dsh_ref)
            dsc_ref[...] = jnp.zeros_like(dsc_ref)
            dng_ref[...] = jnp.zeros_like(dng_ref)

        ng = ng_ref[...]
        one_sc = 1.0 + sc_ref[...]

        def chain(n):
            rows = slice(n * sub, (n + 1) * sub)
            dh = _dot(dp_ref[rows, :], wt_ref[...])
            yield
            xt = x_ref[rows, :]
            r = lax.rsqrt(jnp.mean(xt * xt, axis=-1, keepdims=True) + EPS)
            xn = xt * r
            dsh_ref[...] += jnp.sum(dh, axis=0, keepdims=True)
            dhxn_sum = jnp.sum(dh * xn, axis=0, keepdims=True)
            dsc_ref[...] += dhxn_sum * ng
            dng_ref[...] += dhxn_sum * one_sc
            dxn = dh * (ng * one_sc)
            gx_ref[rows, :] = r * (dxn - xn * jnp.mean(dxn * xn, axis=-1, keepdims=True)) + dx2_ref[rows, :]

        _in_lockstep(chain(n) for n in range(tm // sub))

    tile = pl.BlockSpec((tm, D_MODEL), lambda i: (i, 0))
    vec = _full((1, D_MODEL))
    return pl.pallas_call(
        body, name="bwd_in", grid=(seq // tm,),
        out_shape=(jax.ShapeDtypeStruct((seq, D_MODEL), F32),) + (jax.ShapeDtypeStruct((1, D_MODEL), F32),) * 3,
        in_specs=[pl.BlockSpec((tm, N_PAD), lambda i: (i, 0)), _full((N_PAD, D_MODEL)), tile, tile, vec, vec],
        out_specs=(tile, vec, vec, vec),
        compiler_params=_params(("arbitrary",)),
    )(dproj, w_in_t, x, dx2, norm_g, scale)


def _dw_in(h_b, dproj):
    seq = h_b.shape[0]
    tk = min(4096, seq)
    tn = 768
    n_t = seq // tk

    def body(h_ref, dp_ref, dwt_ref, acc):
        t = pl.program_id(1)

        @pl.when(t == 0)
        def _():
            acc[...] = jnp.zeros_like(acc)

        acc[...] += _dot_tn(dp_ref[...], h_ref[...])

        @pl.when(t == n_t - 1)
        def _():
            dwt_ref[...] = acc[...].astype(BF16)

    return pl.pallas_call(
        body, name="dw_in", grid=(N_PAD // tn, n_t),
        out_shape=jax.ShapeDtypeStruct((N_PAD, D_MODEL), BF16),
        in_specs=[pl.BlockSpec((tk, D_MODEL), lambda j, t: (t, 0)), pl.BlockSpec((tk, tn), lambda j, t: (t, j))],
        out_specs=pl.BlockSpec((tn, D_MODEL), lambda j, t: (j, 0)),
        scratch_shapes=[pltpu.VMEM((tn, D_MODEL), F32)],
        compiler_params=_params(("arbitrary", "arbitrary")),
    )(h_b, dproj)


def _adam_update(g, w, m, v, g_ref, d_ref, m_ref, v_ref):
    mn = ADAM_B1 * m + (1.0 - ADAM_B1) * g
    vn = ADAM_B2 * v + (1.0 - ADAM_B2) * (g * g)
    m_hat = mn / (1.0 - ADAM_B1 ** ADAM_STEP)
    v_hat = vn / (1.0 - ADAM_B2 ** ADAM_STEP)
    g_ref[...] = g
    d_ref[...] = -ADAM_LR * (m_hat / (jnp.sqrt(v_hat) + ADAM_EPS) + ADAM_WD * w)
    m_ref[...] = mn
    v_ref[...] = vn


def _adam_sum(name, parts, w, m, v, row_tile, col_tile=None):
    rows, cols = w.shape
    col_tile = cols if col_tile is None else col_tile
    n_parts = parts.shape[0]

    def body(p_ref, w_ref, m_ref, v_ref, g_out, d_out, m_out, v_out):
        g = p_ref[0].astype(F32)
        for j in range(1, n_parts):
            g = g + p_ref[j].astype(F32)
        _adam_update(g, w_ref[...], m_ref[...], v_ref[...], g_out, d_out, m_out, v_out)

    tile = pl.BlockSpec((row_tile, col_tile), lambda i, j: (i, j))
    return pl.pallas_call(
        body, name=name, grid=(rows // row_tile, cols // col_tile),
        out_shape=(jax.ShapeDtypeStruct((rows, cols), F32),) * 4,
        in_specs=[pl.BlockSpec((n_parts, row_tile, col_tile), lambda i, j: (0, i, j)), tile, tile, tile],
        out_specs=(tile,) * 4,
        compiler_params=_params(("arbitrary", "arbitrary")),
    )(parts, w, m, v)


def _adam_sum_own(name, me, blocks, landed, w, m, v, row_tile, col_tile=None):
    rows, cols = w.shape
    col_tile = cols if col_tile is None else col_tile

    def body(me_ref, p_ref, own_ref, w_ref, m_ref, v_ref, g_out, d_out, m_out, v_out):
        g = jnp.zeros((row_tile, col_tile), F32)
        for j in range(N_DEV):
            g = g + jnp.where(me_ref[0] == j, own_ref[0], p_ref[j]).astype(F32)
        _adam_update(g, w_ref[...], m_ref[...], v_ref[...], g_out, d_out, m_out, v_out)

    tile = pl.BlockSpec((row_tile, col_tile), lambda i, j, me_ref: (i, j))
    return pl.pallas_call(
        body, name=name, out_shape=(jax.ShapeDtypeStruct((rows, cols), F32),) * 4,
        grid_spec=pltpu.PrefetchScalarGridSpec(
            num_scalar_prefetch=1, grid=(rows // row_tile, cols // col_tile),
            in_specs=[pl.BlockSpec((N_DEV, row_tile, col_tile), lambda i, j, me_ref: (0, i, j)),
                      pl.BlockSpec((1, row_tile, col_tile), lambda i, j, me_ref: (me_ref[0], i, j)),
                      tile, tile, tile],
            out_specs=(tile,) * 4),
        compiler_params=_params(("arbitrary", "arbitrary")),
    )(me, landed, blocks, w, m, v)


def _adam_ada(sc_all16, dmod_blk16, w, m, v):
    rows, cols = w.shape

    def body(sc_ref, dm_ref, w_ref, m_ref, v_ref, g_out, d_out, m_out, v_out):
        g = _dot_tn(sc_ref[...].astype(BF16), dm_ref[...].astype(BF16))
        _adam_update(g, w_ref[...], m_ref[...], v_ref[...], g_out, d_out, m_out, v_out)

    return pl.pallas_call(
        body, name="adam_w_ada", grid=(1,),
        out_shape=(jax.ShapeDtypeStruct((rows, cols), F32),) * 4,
        in_specs=[_full(sc_all16.shape), _full(dmod_blk16.shape)] + [_full((rows, cols))] * 3,
        out_specs=(_full((rows, cols)),) * 4,
        compiler_params=_params(("arbitrary",)),
    )(sc_all16, dmod_blk16, w, m, v)


def _adam_small(me, parts, own, loss_parts, own_loss, w, m, v):
    names = list(w)
    n = len(names)

    def body(me_ref, *refs):
        p_refs, loss_ref = refs[:n], refs[n]
        own_refs, own_loss_ref = refs[n + 1:2 * n + 1], refs[2 * n + 1]
        w_refs, m_refs, v_refs = (refs[2 * n + 2 + k * n:2 * n + 2 + (k + 1) * n] for k in range(3))
        outs = refs[5 * n + 2:]

        def total(p_ref, own_ref):
            acc = jnp.where(me_ref[0] == 0, own_ref[...], p_ref[0])
            for j in range(1, N_DEV):
                acc = acc + jnp.where(me_ref[0] == j, own_ref[...], p_ref[j])
            return acc

        for a in range(n):
            g = total(p_refs[a], own_refs[a])
            width = w_refs[a].shape[-1]
            if g.shape[-1] != width:
                g = g[..., 0:width]
            _adam_update(g, w_refs[a][...], m_refs[a][...], v_refs[a][...], *outs[4 * a:4 * a + 4])
        outs[4 * n][...] = total(loss_ref, own_loss_ref)

    args = [parts[k] for k in names] + [loss_parts] + [own[k] for k in names] + [own_loss]
    args += [d[k] for d in (w, m, v) for k in names]
    out_shape = tuple(jax.ShapeDtypeStruct(w[k].shape, F32) for k in names for _ in range(4))
    out_shape += (jax.ShapeDtypeStruct(loss_parts.shape[1:], F32),)
    out = pl.pallas_call(
        body, name="adam_small", out_shape=out_shape,
        grid_spec=pltpu.PrefetchScalarGridSpec(
            num_scalar_prefetch=1, grid=(1,), in_specs=[_full(a.shape) for a in args],
            out_specs=tuple(_full(s.shape) for s in out_shape)),
        compiler_params=_params(("arbitrary",)),
    )(me, *args)
    return {k: out[4 * a:4 * a + 4] for a, k in enumerate(names)}, out[4 * n]


def _local_step(x2, tgt2, shift, scale, gate, norm_g, w_in_t, w_out_b, conv_w, conv_b, w_pool, ls_pool,
                mh_norm_g, b_gates, final_g, send_dw_out=None, send_dw_in=None):
    bg_pad = jnp.pad(b_gates, ((0, 0), (0, 128 - b_gates.shape[1])))
    conv_w8 = jnp.pad(conv_w, ((0, 8 - CONV_WIDTH), (0, 0)))
    fg = final_g.reshape(1, D_MODEL)

    proj, gates, h_b = _fwd_proj(x2, norm_g, scale, shift, w_in_t)
    mix, cst, nst, mst, conv_a, pooled = _mix_fwd(proj, gates, bg_pad, conv_w8, conv_b, w_pool, ls_pool, mh_norm_g)
    dx2, dmix, dwo, dgate, dfg, loss = _out_fwd_bwd(mix, x2, tgt2, w_out_b, gate, fg)
    if send_dw_out is not None:
        bg_pad = bg_pad + send_dw_out(dwo)
    dproj, dcw8, dcb, dwp, dls, dmhg, dbg = _mix_bwd(proj, gates, dmix, conv_a, pooled, cst, nst, mst, bg_pad,
                                                      conv_w8, w_pool, ls_pool, mh_norm_g)
    dw_in_t = _dw_in(h_b, dproj)[:N_IN]
    ng_in = norm_g
    if send_dw_in is not None:
        ng_in = norm_g + send_dw_in(dw_in_t, dcw8[:CONV_WIDTH])
    gx, dsh, dsc, dng = _bwd_in(dproj, w_in_t, x2, dx2, ng_in, scale)
    return dict(loss=loss, grad_x=gx, dw_in_t=dw_in_t, dw_out=dwo, dconv_w=dcw8[:CONV_WIDTH], conv_b=dcb,
                w_pool=dwp, ls_pool=dls, mh_norm_g=dmhg, b_gates=dbg, final_g=dfg, norm_g=dng,
                dmod=jnp.concatenate([dsh, dsc, dgate], axis=1))


def kernel(x, c, norm_g, w_ada, b_ada, w_in, b_gates, conv_w, conv_b, w_pool, ls_pool, mh_norm_g, w_out, final_g, loss_target, m_norm_g, m_w_ada, m_b_ada, m_w_in, m_b_gates, m_conv_w, m_conv_b, m_w_pool, m_ls_pool, m_mh_norm_g, m_w_out, m_final_g, v_norm_g, v_w_ada, v_b_ada, v_w_in, v_b_gates, v_conv_w, v_conv_b, v_w_pool, v_ls_pool, v_mh_norm_g, v_w_out, v_final_g):
    seq = x.shape[1]
    me = 4 * lax.axis_index("x") + 2 * lax.axis_index("y") + lax.axis_index("c")

    b_ada_blk = lax.dynamic_slice(b_ada, (0, me * ADA_SHARD), (1, ADA_SHARD))
    g_in, g_out, g_cw, mod_all, sc_all16 = _gather_weights_and_mod(
        (w_in[0].astype(BF16).T, w_out[0].astype(BF16), conv_w[0]), c, w_ada[0], b_ada_blk)
    w_in_t = jnp.pad(g_in.reshape(N_IN, D_MODEL), ((0, N_PAD - N_IN), (0, 0)))
    w_out_b = g_out.reshape(D_MODEL, D_MODEL)
    conv_w_full = jnp.transpose(g_cw, (1, 0, 2)).reshape(CONV_WIDTH, 2 * D_MLSTM)
    mod = lax.dynamic_index_in_dim(mod_all, me, axis=1, keepdims=False).reshape(1, 3 * D_MODEL)
    shift, scale, gate = mod[:, :D_MODEL], mod[:, D_MODEL:2 * D_MODEL], mod[:, 2 * D_MODEL:]

    flights = {}

    def send_dw_out(dwo):
        blocks = dwo.reshape(N_DEV, D_MODEL // N_DEV, D_MODEL)
        flights["out"], token = _scatter_start("send_dw_out", (blocks,))
        return token

    def send_dw_in(dw_in_t, dcw):
        blocks = dw_in_t.reshape(N_DEV, N_SHARD, D_MODEL)
        dcw_blocks = jnp.transpose(dcw.reshape(CONV_WIDTH, N_DEV, 128), (1, 0, 2))
        flights["in"], token = _scatter_start("send_dw_in", (blocks, dcw_blocks))
        return token

    r = _local_step(x[0], loss_target[0], shift, scale, gate, norm_g, w_in_t, w_out_b, conv_w_full, conv_b,
                    w_pool[0], ls_pool, mh_norm_g, b_gates, final_g, send_dw_out, send_dw_in)

    small_names = ("norm_g", "b_ada", "b_gates", "conv_b", "w_pool", "ls_pool", "mh_norm_g", "final_g")
    small_grads = dict(norm_g=r["norm_g"], b_ada=r["dmod"], b_gates=r["b_gates"], conv_b=r["conv_b"],
                       w_pool=r["w_pool"], ls_pool=r["ls_pool"], mh_norm_g=r["mh_norm_g"], final_g=r["final_g"])
    flights["small"], _ = _scatter_start(
        "send_small", (r["loss"],) + tuple(small_grads[k] for k in small_names), whole=True)
    (s_in, p_in), p_cw = _scatter_wait("recv_dw_in", flights["in"],
                                       flights["small"][2 + 1 + small_names.index("w_pool")], fill=(False, True))
    ((s_out, p_out),) = _scatter_wait("recv_dw_out", flights["out"], p_cw, fill=(False,))

    me1 = me.astype(jnp.int32).reshape(1)
    in_t = _adam_sum_own("adam_w_in", me1, s_in, p_in, w_in[0].T, m_w_in[0].T, v_w_in[0].T, N_SHARD, 256)
    gi, di, mi, vi = (o.T for o in in_t)
    go, do_, mo, vo = _adam_sum_own("adam_w_out", me1, s_out, p_out, w_out[0], m_w_out[0], v_w_out[0], 128)
    gc, dc, mc, vc = _adam_sum("adam_conv_w", p_cw, conv_w[0], m_conv_w[0], v_conv_w[0], CONV_WIDTH)
    gathered = _scatter_wait("recv_small", flights["small"], go, whole=True, fill=(False,) * (1 + len(small_names)))
    (own_loss, p_loss), rest = gathered[0], gathered[1:]
    own_small = dict(zip(small_names, (sent for sent, _ in rest)))
    p_small = dict(zip(small_names, (landed for _, landed in rest)))

    def plain(norm_g_, b_ada_, b_gates_, conv_b_, w_pool_, ls_pool_, mh_norm_g_, final_g_):
        return dict(norm_g=norm_g_, b_ada=b_ada_, b_gates=b_gates_, conv_b=conv_b_, w_pool=w_pool_[0],
                    ls_pool=ls_pool_, mh_norm_g=mh_norm_g_, final_g=final_g_.reshape(1, D_MODEL))

    small, loss_row = _adam_small(
        me1, p_small, own_small, p_loss, own_loss,
        plain(norm_g, b_ada, b_gates, conv_b, w_pool, ls_pool, mh_norm_g, final_g),
        plain(m_norm_g, m_b_ada, m_b_gates, m_conv_b, m_w_pool, m_ls_pool, m_mh_norm_g, m_final_g),
        plain(v_norm_g, v_b_ada, v_b_gates, v_conv_b, v_w_pool, v_ls_pool, v_mh_norm_g, v_final_g))

    device = lax.broadcasted_iota(jnp.int32, (N_DEV, 1), 0)
    dmod_all = jnp.where(device == me, own_small["b_ada"], p_small["b_ada"].reshape(N_DEV, 3 * D_MODEL))
    dmod_blk16 = jnp.pad(lax.dynamic_slice(dmod_all, (0, me * ADA_SHARD), (N_DEV, ADA_SHARD)), ((0, 8), (0, 0)))
    ga, da, ma, va = _adam_ada(sc_all16, dmod_blk16, w_ada[0], m_w_ada[0], v_w_ada[0])

    names = ("norm_g", "w_ada", "b_ada", "w_in", "b_gates", "conv_w", "conv_b", "w_pool", "ls_pool", "mh_norm_g",
             "w_out", "final_g")
    shapes = dict(norm_g=norm_g.shape, b_ada=b_ada.shape, b_gates=b_gates.shape, conv_b=conv_b.shape,
                  w_pool=w_pool.shape, ls_pool=ls_pool.shape, mh_norm_g=mh_norm_g.shape, final_g=final_g.shape)
    sharded = dict(w_ada=(ga, da, ma, va), w_in=(gi, di, mi, vi), conv_w=(gc, dc, mc, vc), w_out=(go, do_, mo, vo))
    outs = []
    for kind in range(4):
        for nm in names:
            if nm in sharded:
                outs.append(sharded[nm][kind][None])
            else:
                outs.append(small[nm][kind].reshape(shapes[nm]))
    loss = loss_row[0, 0]
    grad_x = r["grad_x"].reshape(1, seq, D_MODEL)
    return (loss, grad_x, *outs)
```
